```python
import math
import jax
import jax.numpy as jnp
from jax import lax
import numpy as np

D_MODEL = 2048
BATCH = 2
SEQ = 4096
DEPTH = 2
DEC_BATCH = 32
DEC_SEQ = 4
PAST_LEN = 8192
PAGE_SIZE = 128

MIX_WIDTH = D_MODEL
N_EVEN = (DEPTH + 1) // 2
N_ODD = DEPTH // 2
NSA_HEAD_DIM = 128
NSA_HEADS = (MIX_WIDTH // 2) // NSA_HEAD_DIM
NSA_KV_HEADS = 2
NSA_GROUP = NSA_HEADS // NSA_KV_HEADS
CMP_BLOCK = 32
CMP_STRIDE = 16
SLC_BLOCK = 64
SLC_TOPK = 16
WINDOW = 512
ROPE_THETA = 500000.0
ROPE_DIMS = NSA_HEAD_DIM // 4
SC_WIDTH = MIX_WIDTH // 2
SC_KSIZE = 3
RET_HEADS = 4
RET_DQK = (MIX_WIDTH // 2) // RET_HEADS
RET_DV = RET_DQK
RET_THETA = 10000.0
M2_DINNER = MIX_WIDTH // 2
M2_HEADDIM = 64
M2_HEADS = M2_DINNER // M2_HEADDIM
M2_STATE = 128
M2_GROUPS = 2
M2_CONV = 4
M2_CONV_DIM = M2_DINNER + 2 * M2_GROUPS * M2_STATE
PEER_HEADS = 8
PEER_KEYS = 128
PEER_EXPERTS = PEER_KEYS * PEER_KEYS
PEER_QDIM = 256
PEER_TOPK = 16
PEER_TOKEN_BLOCK = 128

Q_BLOCK = 128
CHUNK = 128
EPS = 1e-6

E_Q = NSA_HEADS * NSA_HEAD_DIM
E_KV = 6 * NSA_KV_HEADS * NSA_HEAD_DIM
E_G = 3 * NSA_HEADS
E_SC = 3 * SC_WIDTH
E_IN = E_Q + E_KV + E_G + E_SC
E_OUT_IN = E_Q + SC_WIDTH
R_QK = RET_HEADS * RET_DQK
R_V = RET_HEADS * RET_DV
O_SPLITS = [R_QK, 2 * R_QK, 2 * R_QK + R_V, 2 * R_QK + 2 * R_V,
            2 * R_QK + 2 * R_V + M2_DINNER, 2 * R_QK + 2 * R_V + M2_DINNER + M2_CONV_DIM]
O_IN = O_SPLITS[-1] + M2_HEADS
O_OUT_IN = R_V + M2_DINNER

STATE_KEYS = ('nsa_kv', 'nsa_win', 'sc_conv', 'ret', 'ssm', 'm2_conv')

kernel_name = 'hybrid_nsa_shortconv_retention_ssd_peer_step'


def rms_norm(x, g):
    xf = x.astype(jnp.float32)
    y = xf * lax.rsqrt(jnp.mean(xf * xf, axis=-1, keepdims=True) + EPS)
    return (y * g.astype(jnp.float32)).astype(x.dtype)


def rotary(x, pos, rot_dims, theta):
    half = rot_dims // 2
    inv = theta ** (-jnp.arange(half, dtype=jnp.float32) / half)
    ang = pos.astype(jnp.float32)[:, None] * inv[None, :]
    shape = (pos.shape[0],) + (1,) * (x.ndim - 3) + (half,)
    cos = jnp.cos(ang).reshape(shape)
    sin = jnp.sin(ang).reshape(shape)
    x1 = x[..., :half].astype(jnp.float32)
    x2 = x[..., half:rot_dims].astype(jnp.float32)
    rot = jnp.concatenate([x1 * cos - x2 * sin, x1 * sin + x2 * cos], axis=-1).astype(x.dtype)
    return jnp.concatenate([rot, x[..., rot_dims:]], axis=-1)


def masked_softmax(s, mask, axis=-1):
    s = jnp.where(mask, s.astype(jnp.float32), -1e30)
    p = jnp.exp(s - jnp.max(s, axis=axis, keepdims=True)) * mask
    return p / jnp.maximum(jnp.sum(p, axis=axis, keepdims=True), 1e-30)


def causal_conv(u, buf, w, b=None):
    k_size = w.shape[0]
    t_len = u.shape[1]
    ext = jnp.concatenate([buf.astype(u.dtype), u], axis=1)
    y = ext[:, 0:t_len] * w[0]
    for j in range(1, k_size):
        y = y + ext[:, j:j + t_len] * w[j]
    if b is not None:
        y = y + b
    return y, ext[:, t_len:]


def to_chunks(a, chunk):
    return a.reshape(a.shape[0], a.shape[1] // chunk, chunk, *a.shape[2:]).swapaxes(0, 1)


def nsa_compress(rows, pe, w1, w2):
    bn, length, g, d = rows.shape
    n_part = CMP_BLOCK // CMP_STRIDE
    n_sub = length // CMP_STRIDE
    n_cmp = n_sub - n_part + 1
    sub = rows[:, :n_sub * CMP_STRIDE].reshape(bn, n_sub, CMP_STRIDE, g, d)
    w1p = w1.reshape(n_part, CMP_STRIDE, d, d)
    pre = jnp.einsum('jsd,jsde->e', pe.reshape(n_part, CMP_STRIDE, d), w1p)
    for j in range(n_part):
        pre = pre + jnp.einsum('bnsgd,sde->bnge', sub[:, j:j + n_cmp], w1p[j])
    return jnp.einsum('bnge,ef->bngf', jax.nn.gelu(pre), w2)


def nsa_sparse(q, q_pos, rows, cmp_pe, cmp_w1, cmp_w2):
    bn, t_len, g, r, d = q.shape
    length = rows.shape[1]
    kc = nsa_compress(rows[:, :, 0], cmp_pe[0], cmp_w1[0], cmp_w2[0])
    vc = nsa_compress(rows[:, :, 1], cmp_pe[1], cmp_w1[1], cmp_w2[1])
    n_cmp = kc.shape[1]
    c_start = jnp.arange(n_cmp) * CMP_STRIDE
    c_mask = (c_start + CMP_BLOCK - 1)[None, :] <= q_pos[:, None]
    p_c = masked_softmax(jnp.einsum('btgrd,bngd->btgrn', q, kc), c_mask[None, :, None, None, :])
    o_cmp = jnp.einsum('btgrn,bngd->btgrd', p_c, vc)
    n_slc = -(-length // SLC_BLOCK)
    s_start = jnp.arange(n_slc) * SLC_BLOCK
    overlap = ((c_start[:, None] < s_start[None, :] + SLC_BLOCK)
               & (c_start[:, None] + CMP_BLOCK > s_start[None, :])).astype(jnp.float32)
    imp = jnp.einsum('btgrn,nm->btgm', p_c, overlap)
    blk = jnp.arange(n_slc)[None, :]
    cur = (q_pos // SLC_BLOCK)[:, None]
    valid = (s_start[None, :] <= q_pos[:, None])[None, :, None, :]
    forced = ((blk == 0) | (blk == cur) | (blk == cur - 1))[None, :, None, :]
    score = jnp.where(valid, jnp.where(forced, 1e6, imp), -1e6)
    n_sel = min(SLC_TOPK, n_slc)
    _, idx = lax.top_k(score, n_sel)
    pad = n_slc * SLC_BLOCK - length
    kv_s = jnp.pad(rows[:, :, 2:4], ((0, 0), (0, pad), (0, 0), (0, 0), (0, 0)))
    kv_blk = kv_s.reshape(bn, n_slc, SLC_BLOCK, 2, g, d).transpose(0, 4, 1, 2, 3, 5)
    bi = jnp.arange(bn)[:, None, None, None]
    gi = jnp.arange(g)[None, None, :, None]
    qb = Q_BLOCK if t_len % Q_BLOCK == 0 else t_len
    nqb = t_len // qb

    def sel_block(args):
        q_b, idx_b, pos_b = args
        kv_g = kv_blk[bi, gi, idx_b]
        s = jnp.einsum('bqgrd,bqgksd->bqgrks', q_b, kv_g[..., 0, :])
        kpos = idx_b[..., None] * SLC_BLOCK + jnp.arange(SLC_BLOCK)
        mask = (kpos <= pos_b[None, :, None, None, None])[:, :, :, None]
        p = masked_softmax(s, mask, axis=(-2, -1))
        return jnp.einsum('bqgrks,bqgksd->bqgrd', p, kv_g[..., 1, :])

    q_blocks = q.reshape(bn, nqb, qb, g, r, d).swapaxes(0, 1)
    idx_blocks = idx.reshape(bn, nqb, qb, g, n_sel).swapaxes(0, 1)
    o_slc = lax.map(sel_block, (q_blocks, idx_blocks, q_pos.reshape(nqb, qb)))
    o_slc = o_slc.swapaxes(0, 1).reshape(bn, t_len, g, r, d)
    return o_cmp, o_slc


def window_attention(q, q_pos, k, v, k_pos):
    s = jnp.einsum('ntgrd,nsgd->ntgrs', q, k)
    dpos = q_pos[:, :, None] - k_pos[:, None, :]
    mask = (dpos >= 0) & (dpos < WINDOW) & (k_pos[:, None, :] >= 0)
    p = masked_softmax(s, mask[:, :, None, None, :])
    return jnp.einsum('ntgrs,nsgd->ntgrd', p, v)


def window_prompt(q, win_rows, pos):
    bn, t_len, g, r, d = q.shape
    qb = Q_BLOCK if t_len % Q_BLOCK == 0 else t_len
    nqb = t_len // qb
    span = WINDOW + qb
    padded = jnp.pad(win_rows, ((0, 0), (WINDOW, 0), (0, 0), (0, 0), (0, 0)))
    kidx = jnp.arange(nqb)[:, None] * qb + jnp.arange(span)[None, :]
    kw = padded[:, kidx].reshape(bn * nqb, span, 2, g, d)
    k_pos = jnp.broadcast_to(kidx - WINDOW, (bn, nqb, span)).reshape(bn * nqb, span)
    q_pos = jnp.broadcast_to(pos.reshape(nqb, qb), (bn, nqb, qb)).reshape(bn * nqb, qb)
    o = window_attention(q.reshape(bn * nqb, qb, g, r, d), q_pos, kw[:, :, 0], kw[:, :, 1], k_pos)
    return o.reshape(bn, t_len, g, r, d)


def even_mixer(hn, pos, past_rows, win_buf, conv_buf, w_in, w_out, cmp_pe, cmp_w1, cmp_w2, sc_w):
    bn, t_len, _ = hn.shape
    g, r, d = NSA_KV_HEADS, NSA_GROUP, NSA_HEAD_DIM
    proj = hn @ w_in
    q, kv, gates, sc = jnp.split(proj, [E_Q, E_Q + E_KV, E_Q + E_KV + E_G], axis=-1)
    q = rotary(q.reshape(bn, t_len, g, r, d), pos, ROPE_DIMS, ROPE_THETA) * (d ** -0.5)
    kv = kv.reshape(bn, t_len, 3, 2, g, d)
    k = rotary(kv[:, :, :, 0], pos, ROPE_DIMS, ROPE_THETA)
    v = kv[:, :, :, 1]
    new_rows = jnp.stack([k[:, :, 0], v[:, :, 0], k[:, :, 1], v[:, :, 1]], axis=2)
    win_rows = jnp.stack([k[:, :, 2], v[:, :, 2]], axis=2)
    if past_rows is None:
        rows = new_rows
    else:
        rows = jnp.concatenate([past_rows.astype(new_rows.dtype), new_rows], axis=1)
    o_cmp, o_slc = nsa_sparse(q, pos, rows, cmp_pe, cmp_w1, cmp_w2)
    if win_buf is None:
        o_win = window_prompt(q, win_rows, pos)
        new_win = win_rows[:, t_len - min(WINDOW, t_len):]
    else:
        w_len = win_buf.shape[1]
        ext = jnp.concatenate([win_buf.astype(win_rows.dtype), win_rows], axis=1)
        k_pos = (pos[0] - w_len + jnp.arange(w_len + t_len))[None]
        o_win = window_attention(q, pos[None], ext[:, :, 0], ext[:, :, 1], k_pos)
        new_win = ext[:, t_len:]
    gate = jax.nn.sigmoid(gates.astype(jnp.float32)).reshape(bn, t_len, 3, g, r, 1)
    o_nsa = (gate[:, :, 0] * o_cmp + gate[:, :, 1] * o_slc + gate[:, :, 2] * o_win)
    o_nsa = o_nsa.reshape(bn, t_len, E_Q).astype(hn.dtype)
    b_gate, c_gate, h_in = jnp.split(sc, 3, axis=-1)
    if conv_buf is None:
        conv_buf = jnp.zeros((bn, SC_KSIZE - 1, SC_WIDTH), hn.dtype)
    conv_out, new_conv = causal_conv(c_gate * h_in, conv_buf, sc_w)
    o_sc = (b_gate * conv_out).astype(hn.dtype)
    out = jnp.concatenate([o_nsa, o_sc], axis=-1) @ w_out
    return out, new_rows, new_win, new_conv


def retention(q, k, v, state, chunk):
    bn, t_len, h, _ = q.shape
    dv = v.shape[-1]
    f32 = jnp.float32
    log_g = jnp.log1p(-(2.0 ** (-5.0 - jnp.arange(h, dtype=f32))))
    i = jnp.arange(chunk, dtype=f32)
    diff = i[:, None] - i[None, :]
    intra = jnp.where(diff >= 0, jnp.exp(jnp.maximum(diff, 0.0)[None] * log_g[:, None, None]), 0.0)
    q_dec = jnp.exp((i[:, None] + 1.0) * log_g[None, :])
    k_dec = jnp.exp((chunk - 1.0 - i)[:, None] * log_g[None, :])
    c_dec = jnp.exp(chunk * log_g)

    def step(s_mat, inp):
        qc, kc, vc = inp
        att = jnp.einsum('bihd,bjhd->bhij', qc, kc) * intra
        o = (jnp.einsum('bhij,bjhv->bihv', att, vc)
             + jnp.einsum('bihd,bhdv->bihv', qc, s_mat) * q_dec[None, :, :, None])
        s_mat = s_mat * c_dec[None, :, None, None] + jnp.einsum('bjhd,bjhv->bhdv', kc * k_dec[None, :, :, None], vc)
        return s_mat, o

    s_fin, o = lax.scan(step, state.astype(f32),
                        (to_chunks(q.astype(f32), chunk), to_chunks(k.astype(f32), chunk), to_chunks(v.astype(f32), chunk)))
    return o.swapaxes(0, 1).reshape(bn, t_len, h, dv), s_fin


def ssd(x, dt, a_neg, b_in, c_in, state, chunk):
    bn, t_len, h, p = x.shape
    f32 = jnp.float32
    rep = h // b_in.shape[2]
    a = dt * a_neg
    b_h = jnp.repeat(b_in.astype(f32), rep, axis=2)
    c_h = jnp.repeat(c_in.astype(f32), rep, axis=2)
    xdt = x.astype(f32) * dt[..., None]
    tri = jnp.arange(chunk)[:, None] >= jnp.arange(chunk)[None, :]

    def step(s_mat, inp):
        a_c, x_c, b_c, c_c = inp
        cum = jnp.cumsum(a_c, axis=1)
        seg = cum[:, :, None, :] - cum[:, None, :, :]
        l_mat = jnp.exp(jnp.where(tri[None, :, :, None], seg, -1e30))
        cb = jnp.einsum('bihn,bjhn->bijh', c_c, b_c) * l_mat
        y = (jnp.einsum('bijh,bjhp->bihp', cb, x_c)
             + jnp.einsum('bihn,bhpn->bihp', c_c, s_mat) * jnp.exp(cum)[..., None])
        dec_end = jnp.exp(cum[:, -1:, :] - cum)
        s_mat = (s_mat * jnp.exp(cum[:, -1, :])[:, :, None, None]
                 + jnp.einsum('bjhn,bjhp->bhpn', b_c * dec_end[..., None], x_c))
        return s_mat, y

    s_fin, y = lax.scan(step, state.astype(f32),
                        (to_chunks(a, chunk), to_chunks(xdt, chunk), to_chunks(b_h, chunk), to_chunks(c_h, chunk)))
    return y.swapaxes(0, 1).reshape(bn, t_len, h, p), s_fin


def odd_mixer(hn, pos, ret_state, ssm_state, conv_buf, w_in, w_out, gn_gain, conv_w, conv_b, dt_bias, a_log, d_skip, m2_norm):
    bn, t_len, _ = hn.shape
    f32 = jnp.float32
    chunk = CHUNK if t_len % CHUNK == 0 else t_len
    proj = hn @ w_in
    q, k, v, g, z, xbc, dt = jnp.split(proj, O_SPLITS, axis=-1)
    q = rotary(q.reshape(bn, t_len, RET_HEADS, RET_DQK), pos, RET_DQK, RET_THETA) * (RET_DQK ** -0.5)
    k = rotary(k.reshape(bn, t_len, RET_HEADS, RET_DQK), pos, RET_DQK, RET_THETA)
    v = v.reshape(bn, t_len, RET_HEADS, RET_DV)
    if ret_state is None:
        ret_state = jnp.zeros((bn, RET_HEADS, RET_DQK, RET_DV), f32)
    o_ret, ret_new = retention(q, k, v, ret_state, chunk)
    mu = jnp.mean(o_ret, axis=-1, keepdims=True)
    var = jnp.mean(jnp.square(o_ret - mu), axis=-1, keepdims=True)
    o_ret = ((o_ret - mu) * lax.rsqrt(var + EPS)).reshape(bn, t_len, R_V) * gn_gain
    o_ret = jax.nn.silu(g.astype(f32)) * o_ret
    if conv_buf is None:
        conv_buf = jnp.zeros((bn, M2_CONV - 1, M2_CONV_DIM), hn.dtype)
    xbc_c, conv_new = causal_conv(xbc, conv_buf, conv_w, conv_b)
    xbc_c = jax.nn.silu(xbc_c)
    xs, b_in, c_in = jnp.split(xbc_c, [M2_DINNER, M2_DINNER + M2_GROUPS * M2_STATE], axis=-1)
    xs = xs.reshape(bn, t_len, M2_HEADS, M2_HEADDIM)
    b_in = b_in.reshape(bn, t_len, M2_GROUPS, M2_STATE)
    c_in = c_in.reshape(bn, t_len, M2_GROUPS, M2_STATE)
    dt = jax.nn.softplus(dt.astype(f32) + dt_bias)
    a_neg = -jnp.exp(a_log.astype(f32))
    if ssm_state is None:
        ssm_state = jnp.zeros((bn, M2_HEADS, M2_HEADDIM, M2_STATE), f32)
    y, ssm_new = ssd(xs, dt, a_neg, b_in, c_in, ssm_state, chunk)
    y = y + d_skip[:, None] * xs.astype(f32)
    y = y.reshape(bn, t_len, M2_DINNER) * jax.nn.silu(z.astype(f32))
    yg = y.reshape(bn, t_len, M2_GROUPS, M2_DINNER // M2_GROUPS)
    yg = yg * lax.rsqrt(jnp.mean(yg * yg, axis=-1, keepdims=True) + EPS)
    y = yg.reshape(bn, t_len, M2_DINNER) * m2_norm
    out = jnp.concatenate([o_ret, y], axis=-1).astype(hn.dtype) @ w_out
    return out, ret_new, ssm_new, conv_new


def peer(hn, w_q, sub_keys, u, v):
    bn, t_len, dm = hn.shape
    f32 = jnp.float32
    n_tok = bn * t_len
    xt = hn.reshape(n_tok, dm)
    q = (xt @ w_q).astype(f32).reshape(n_tok, PEER_HEADS, 2, PEER_QDIM // 2)
    q = q * lax.rsqrt(jnp.mean(q * q, axis=-1, keepdims=True) + EPS)
    s = jnp.einsum('nhcq,hckq->nhck', q, sub_keys.astype(f32))
    s1, i1 = lax.top_k(s[:, :, 0], PEER_TOPK)
    s2, i2 = lax.top_k(s[:, :, 1], PEER_TOPK)
    cand_s = (s1[..., :, None] + s2[..., None, :]).reshape(n_tok, PEER_HEADS, -1)
    cand_i = (i1[..., :, None] * PEER_KEYS + i2[..., None, :]).reshape(n_tok, PEER_HEADS, -1)
    top_s, pick = lax.top_k(cand_s, PEER_TOPK)
    experts = jnp.take_along_axis(cand_i, pick, axis=-1)
    gate = jax.nn.softmax(top_s, axis=-1)
    nb = -(-n_tok // PEER_TOKEN_BLOCK)
    padn = nb * PEER_TOKEN_BLOCK - n_tok
    xp = jnp.pad(xt, ((0, padn), (0, 0))).reshape(nb, PEER_TOKEN_BLOCK, dm)
    ep = jnp.pad(experts, ((0, padn), (0, 0), (0, 0))).reshape(nb, PEER_TOKEN_BLOCK, PEER_HEADS, PEER_TOPK)
    gp = jnp.pad(gate, ((0, padn), (0, 0), (0, 0))).reshape(nb, PEER_TOKEN_BLOCK, PEER_HEADS, PEER_TOPK)

    def block(args):
        xb, eb, gb = args
        hb = jax.nn.gelu(jnp.einsum('td,thkd->thk', xb, u[eb]).astype(f32))
        return jnp.einsum('thk,thkd->td', gb * hb, v[eb].astype(f32))

    out = lax.map(block, (xp, ep, gp)).reshape(nb * PEER_TOKEN_BLOCK, dm)[:n_tok]
    return out.reshape(bn, t_len, dm).astype(hn.dtype)


def trunk(x, pos, st, w):
    new = {name: [] for name in STATE_KEYS}
    for layer in range(DEPTH):
        j = layer // 2
        hn = rms_norm(x, w['norm_w'][layer, 0])
        if layer % 2 == 0:
            past_rows = win_buf = conv_buf = None
            if st is not None:
                pages = st['cache_nsa_kv'][j][st['page_table']]
                past_rows = pages.reshape(pages.shape[0], -1, *pages.shape[3:])
                win_buf = st['cache_nsa_win'][j]
                conv_buf = st['state_sc_conv'][j]
            mix, kv_rows, win_new, conv_new = even_mixer(
                hn, pos, past_rows, win_buf, conv_buf, w['e_w_in'][j], w['e_w_out'][j],
                w['e_cmp_pe'][j], w['e_cmp_w1'][j], w['e_cmp_w2'][j], w['e_sc_conv'][j])
            new['nsa_kv'].append(kv_rows)
            new['nsa_win'].append(win_new)
            new['sc_conv'].append(conv_new)
        else:
            ret0 = ssm0 = cbuf = None
            if st is not None:
                ret0 = st['state_ret'][j]
                ssm0 = st['state_ssm'][j]
                cbuf = st['state_m2_conv'][j]
            mix, ret_new, ssm_new, m2c_new = odd_mixer(
                hn, pos, ret0, ssm0, cbuf, w['o_w_in'][j], w['o_w_out'][j], w['o_ret_gn'][j],
                w['o_m2_conv_w'][j], w['o_m2_conv_b'][j], w['o_m2_dt_bias'][j], w['o_m2_a_log'][j],
                w['o_m2_d'][j], w['o_m2_norm'][j])
            new['ret'].append(ret_new)
            new['ssm'].append(ssm_new)
            new['m2_conv'].append(m2c_new)
        x = x + mix
        x = x + peer(rms_norm(x, w['norm_w'][layer, 1]), w['peer_wq'][layer], w['peer_keys'][layer],
                     w['peer_u'][layer], w['peer_v'][layer])
    return rms_norm(x, w['final_norm']), [jnp.stack(new[name]) for name in STATE_KEYS]


def setup_inputs(seed: int = 0) -> dict:
    key = jax.random.key(seed)
    ks = iter(jax.random.split(key, 40))
    f32 = jnp.float32

    def nrm(shape, scale):
        return jax.random.normal(next(ks), shape, f32) * scale

    n_pages = PAST_LEN // PAGE_SIZE
    n_used = DEC_BATCH * n_pages
    n_phys = n_used + max(1, n_used // 4)
    w_buf = min(WINDOW, PAST_LEN)
    hd, kvh = NSA_HEAD_DIM, NSA_KV_HEADS
    page_table = jax.random.permutation(next(ks), n_phys)[:n_used].reshape(DEC_BATCH, n_pages).astype(jnp.int32)
    dt0 = jnp.exp(jax.random.uniform(next(ks), (N_ODD, M2_HEADS), f32, math.log(1e-3), math.log(1e-1)))
    return {
        'x_prompt': nrm((BATCH, SEQ, D_MODEL), 1.0),
        'x_sample': nrm((DEC_BATCH, DEC_SEQ, D_MODEL), 1.0),
        'cache_nsa_kv': nrm((N_EVEN, n_phys, PAGE_SIZE, 4, kvh, hd), 1.0),
        'cache_nsa_win': nrm((N_EVEN, DEC_BATCH, w_buf, 2, kvh, hd), 1.0),
        'state_sc_conv': nrm((N_EVEN, DEC_BATCH, SC_KSIZE - 1, SC_WIDTH), 1.0),
        'state_ret': nrm((N_ODD, DEC_BATCH, RET_HEADS, RET_DQK, RET_DV), 2.0),
        'state_ssm': nrm((N_ODD, DEC_BATCH, M2_HEADS, M2_HEADDIM, M2_STATE), 0.5),
        'state_m2_conv': nrm((N_ODD, DEC_BATCH, M2_CONV - 1, M2_CONV_DIM), 1.0),
        'page_table': page_table,
        'norm_w': 1.0 + nrm((DEPTH, 2, D_MODEL), 0.02),
        'final_norm': 1.0 + nrm((D_MODEL,), 0.02),
        'e_w_in': nrm((N_EVEN, D_MODEL, E_IN), D_MODEL ** -0.5),
        'e_w_out': nrm((N_EVEN, E_OUT_IN, D_MODEL), E_OUT_IN ** -0.5),
        'e_cmp_pe': nrm((N_EVEN, 2, CMP_BLOCK, hd), 0.1),
        'e_cmp_w1': nrm((N_EVEN, 2, CMP_BLOCK * hd, hd), (CMP_BLOCK * hd) ** -0.5),
        'e_cmp_w2': nrm((N_EVEN, 2, hd, hd), hd ** -0.5),
        'e_sc_conv': nrm((N_EVEN, SC_KSIZE, SC_WIDTH), SC_KSIZE ** -0.5),
        'o_w_in': nrm((N_ODD, D_MODEL, O_IN), D_MODEL ** -0.5),
        'o_w_out': nrm((N_ODD, O_OUT_IN, D_MODEL), O_OUT_IN ** -0.5),
        'o_ret_gn': 1.0 + nrm((N_ODD, R_V), 0.02),
        'o_m2_conv_w': nrm((N_ODD, M2_CONV, M2_CONV_DIM), M2_CONV ** -0.5),
        'o_m2_conv_b': nrm((N_ODD, M2_CONV_DIM), 0.01),
        'o_m2_dt_bias': dt0 + jnp.log(-jnp.expm1(-dt0)),
        'o_m2_a_log': jnp.log(jax.random.uniform(next(ks), (N_ODD, M2_HEADS), f32, 1.0, 16.0)),
        'o_m2_d': 1.0 + nrm((N_ODD, M2_HEADS), 0.1),
        'o_m2_norm': 1.0 + nrm((N_ODD, M2_DINNER), 0.02),
        'peer_wq': nrm((DEPTH, D_MODEL, PEER_HEADS * PEER_QDIM), D_MODEL ** -0.5),
        'peer_keys': nrm((DEPTH, PEER_HEADS, 2, PEER_KEYS, PEER_QDIM // 2), (PEER_QDIM // 2) ** -0.5),
        'peer_u': nrm((DEPTH, PEER_EXPERTS, D_MODEL), D_MODEL ** -0.5),
        'peer_v': nrm((DEPTH, PEER_EXPERTS, D_MODEL), PEER_HEADS ** -0.5),
    }


def reference(x_prompt, x_sample, cache_nsa_kv, cache_nsa_win, state_sc_conv, state_ret, state_ssm,
              state_m2_conv, page_table, norm_w, final_norm, e_w_in, e_w_out, e_cmp_pe, e_cmp_w1,
              e_cmp_w2, e_sc_conv, o_w_in, o_w_out, o_ret_gn, o_m2_conv_w, o_m2_conv_b, o_m2_dt_bias,
              o_m2_a_log, o_m2_d, o_m2_norm, peer_wq, peer_keys, peer_u, peer_v):
    w = dict(norm_w=norm_w, final_norm=final_norm, e_w_in=e_w_in, e_w_out=e_w_out, e_cmp_pe=e_cmp_pe,
             e_cmp_w1=e_cmp_w1, e_cmp_w2=e_cmp_w2, e_sc_conv=e_sc_conv, o_w_in=o_w_in, o_w_out=o_w_out,
             o_ret_gn=o_ret_gn, o_m2_conv_w=o_m2_conv_w, o_m2_conv_b=o_m2_conv_b, o_m2_dt_bias=o_m2_dt_bias,
             o_m2_a_log=o_m2_a_log, o_m2_d=o_m2_d, o_m2_norm=o_m2_norm, peer_wq=peer_wq,
             peer_keys=peer_keys, peer_u=peer_u, peer_v=peer_v)
    st = dict(cache_nsa_kv=cache_nsa_kv, cache_nsa_win=cache_nsa_win, state_sc_conv=state_sc_conv,
              state_ret=state_ret, state_ssm=state_ssm, state_m2_conv=state_m2_conv, page_table=page_table)
    past_len = page_table.shape[1] * PAGE_SIZE
    pos_p = jnp.arange(x_prompt.shape[1], dtype=jnp.int32)
    pos_s = past_len + jnp.arange(x_sample.shape[1], dtype=jnp.int32)
    y_prompt, p_new = trunk(x_prompt, pos_p, None, w)
    y_sample, s_new = trunk(x_sample, pos_s, st, w)
    p_kv, p_win, p_sc, p_ret, p_ssm, p_m2c = p_new
    s_kv, s_win, s_sc, s_ret, s_ssm, s_m2c = s_new
    return (y_prompt, y_sample, p_kv, p_win, p_sc, p_ret, p_ssm, p_m2c, s_kv, s_win, s_sc, s_ret, s_ssm, s_m2c)
```

```python
import functools
import math

import jax
import jax.numpy as jnp
from jax import lax
from jax.experimental import pallas as pl
from jax.experimental.pallas import tpu as pltpu

F32 = jnp.float32
BF16 = jnp.bfloat16

D_MODEL = 2048
DEPTH = 2
PAGE_SIZE = 128
NSA_HEAD_DIM = 128
NSA_HEADS = 8
NSA_KV_HEADS = 2
NSA_GROUP = 4
CMP_BLOCK = 32
CMP_STRIDE = 16
SLC_BLOCK = 64
SLC_TOPK = 16
WINDOW = 512
ROPE_THETA = 500000.0
ROPE_DIMS = 32
SC_WIDTH = 1024
SC_KSIZE = 3
RET_HEADS = 4
RET_DQK = 256
RET_DV = 256
RET_THETA = 10000.0
M2_DINNER = 1024
M2_HEADDIM = 64
M2_HEADS = 16
M2_STATE = 128
M2_GROUPS = 2
M2_CONV = 4
M2_CONV_DIM = M2_DINNER + 2 * M2_GROUPS * M2_STATE
PEER_HEADS = 8
PEER_KEYS = 128
PEER_QDIM = 256
PEER_TOPK = 16
PEER_TOKEN_BLOCK = 128
Q_BLOCK = 128
CHUNK = 128
EPS = 1e-6

E_Q = NSA_HEADS * NSA_HEAD_DIM
E_KV = 6 * NSA_KV_HEADS * NSA_HEAD_DIM
E_G = 3 * NSA_HEADS
E_SC = 3 * SC_WIDTH
E_IN = E_Q + E_KV + E_G + E_SC
R_QK = RET_HEADS * RET_DQK
R_V = RET_HEADS * RET_DV
O_SPLITS = [R_QK, 2 * R_QK, 2 * R_QK + R_V, 2 * R_QK + 2 * R_V,
            2 * R_QK + 2 * R_V + M2_DINNER, 2 * R_QK + 2 * R_V + M2_DINNER + M2_CONV_DIM]
O_IN = O_SPLITS[-1] + M2_HEADS

LANES = 128
VMEM_LIMIT = 56 * 1024 * 1024


def _round_up(n, m):
    return -(-n // m) * m


def _pick_tile(n, cap):
    best = LANES
    for t in range(LANES, cap + 1, LANES):
        if n % t == 0:
            best = t
    return best


def _pad_cols(m):
    return min((_round_up(m, t) for t in (768, 640, 512)))


def _row_tile(n, cap):
    best = None
    for t in range(16, cap + 1, 16):
        if n % t == 0:
            best = t
    assert best is not None
    return best


def _norm_matmul_kernel(x_ref, g_ref, w_ref, o_ref, xn_ref):
    @pl.when(pl.program_id(1) == 0)
    def _():
        x = x_ref[...]
        ms = jnp.mean(x * x, axis=-1, keepdims=True)
        xn_ref[...] = (x * lax.rsqrt(ms + EPS) * g_ref[...]).astype(BF16)

    o_ref[...] = jnp.dot(xn_ref[...], w_ref[...], preferred_element_type=F32).reshape(o_ref.shape)


def norm_matmul(x, gain, w_bf, *, head_major_cols=None):
    n, k = x.shape
    m = w_bf.shape[1]
    tm = _row_tile(n, 1040)
    if head_major_cols is None:
        tn = _pick_tile(m, 768)
        out_shape = jax.ShapeDtypeStruct((n, m), F32)
        out_spec = pl.BlockSpec((tm, tn), lambda i, j: (i, j))
    else:
        tn = head_major_cols
        out_shape = jax.ShapeDtypeStruct((m // tn, n, tn), F32)
        out_spec = pl.BlockSpec((1, tm, tn), lambda i, j: (j, i, 0))
    return pl.pallas_call(
        _norm_matmul_kernel,
        grid=(n // tm, m // tn),
        in_specs=[pl.BlockSpec((tm, k), lambda i, j: (i, 0)),
                  pl.BlockSpec((1, k), lambda i, j: (0, 0)),
                  pl.BlockSpec((k, tn), lambda i, j: (0, j))],
        out_specs=out_spec,
        out_shape=out_shape,
        scratch_shapes=[pltpu.VMEM((tm, k), BF16)],
        compiler_params=pltpu.CompilerParams(
            dimension_semantics=("parallel", "arbitrary"), vmem_limit_bytes=VMEM_LIMIT),
        name="norm_matmul",
    )(x, gain.reshape(1, k), w_bf)


def _matmul_res_kernel(a_ref, w_ref, r_ref, o_ref):
    o_ref[...] = r_ref[...] + jnp.dot(a_ref[...], w_ref[...], preferred_element_type=F32)


def matmul_res(a_bf, w_bf, res):
    n, k = a_bf.shape
    m = w_bf.shape[1]
    tm = _row_tile(n, 1040)
    tn = _pick_tile(m, 1024)
    return pl.pallas_call(
        _matmul_res_kernel,
        grid=(n // tm, m // tn),
        in_specs=[pl.BlockSpec((tm, k), lambda i, j: (i, 0)),
                  pl.BlockSpec((k, tn), lambda i, j: (0, j)),
                  pl.BlockSpec((tm, tn), lambda i, j: (i, j))],
        out_specs=pl.BlockSpec((tm, tn), lambda i, j: (i, j)),
        out_shape=jax.ShapeDtypeStruct((n, m), F32),
        compiler_params=pltpu.CompilerParams(
            dimension_semantics=("parallel", "parallel"), vmem_limit_bytes=VMEM_LIMIT),
        name="matmul_res",
    )(a_bf, w_bf, res)


PEER_TOK = 128
PEER_TM = 640
PEER_TA = 8
PEER_TE = PEER_TA * PEER_KEYS
NEG_INF = float("-inf")


def _top_desc(work, count):
    rows = []
    for _ in range(count):
        m = jnp.max(work, axis=0, keepdims=True)
        rows.append(m)
        work = jnp.where(work == m, NEG_INF, work)
    return rows


def _peer_router_kernel(q_ref, k_ref, c_ref, gw_ref, s2_ref, w2_ref):
    def head(h, carry):
        q = q_ref[h]
        scores = []
        for side in range(2):
            qs = q[:, side * PEER_KEYS:(side + 1) * PEER_KEYS]
            qs = qs * lax.rsqrt(jnp.mean(qs * qs, axis=-1, keepdims=True) + EPS)
            scores.append(lax.dot_general(k_ref[h, side], qs, (((1,), (1,)), ((), ())),
                                          preferred_element_type=F32))
        s1, s2 = scores
        v1 = _top_desc(s1, PEER_TOPK + 1)
        v2 = _top_desc(s2, PEER_TOPK + 1)
        v2_lo = jnp.concatenate(v2[:8], axis=0)
        v2_hi = jnp.concatenate(v2[8:16], axis=0)
        row = lax.broadcasted_iota(jnp.int32, v2_lo.shape, 0)
        blocks = [v1[0] + v2_lo, v1[0] + v2_hi, v1[1] + v2_lo]
        for a, lim in ((2, 5), (3, 4), (4, 3), (5, 2), (6, 2), (7, 2)):
            blocks.append(jnp.where(row < lim, v1[a] + v2_lo, NEG_INF))
        blocks.append(jnp.concatenate(v1[8:16], axis=0) + v2[0])
        extra = jnp.where(row == 0, v1[0] + v2[16], jnp.where(row == 1, v1[16] + v2[0], NEG_INF))
        blocks.append(extra)
        cand = jnp.concatenate(blocks, axis=0)
        tops = _top_desc(cand, PEER_TOPK + 1)
        z = jnp.zeros_like(tops[0])
        for r in range(PEER_TOPK):
            z = z + jnp.exp(tops[r] - tops[0])
        tau = 0.5 * (tops[PEER_TOPK - 1] + tops[PEER_TOPK])
        c_ref[h] = tau - s1
        gw_ref[h] = jnp.exp(s1 - v1[0]) / z
        s2_ref[h] = s2
        w2_ref[h] = jnp.exp(s2 - v2[0])
        return carry

    lax.fori_loop(0, PEER_HEADS, head, 0)


def peer_router(q_hm, keys):
    n = q_hm.shape[1]
    out = jax.ShapeDtypeStruct((PEER_HEADS, PEER_KEYS, n), F32)
    spec = pl.BlockSpec((PEER_HEADS, PEER_KEYS, PEER_TOK), lambda i: (0, 0, i))
    return pl.pallas_call(
        _peer_router_kernel,
        grid=(n // PEER_TOK,),
        in_specs=[pl.BlockSpec((PEER_HEADS, PEER_TOK, PEER_QDIM), lambda i: (0, i, 0)),
                  pl.BlockSpec((PEER_HEADS, 2, PEER_KEYS, PEER_QDIM // 2), lambda i: (0, 0, 0, 0))],
        out_specs=[spec, spec, spec, spec],
        out_shape=[out, out, out, out],
        compiler_params=pltpu.CompilerParams(dimension_semantics=("parallel",), vmem_limit_bytes=VMEM_LIMIT),
        name="peer_router",
    )(q_hm, keys)


def _gelu_tanh(x):
    return 0.5 * x * (1.0 + jnp.tanh(math.sqrt(2.0 / math.pi) * (x + 0.044715 * (x * x * x))))


def _peer_expert_kernel(xt_ref, u_ref, vt_ref, c_ref, gw_ref, s2_ref, w2_ref, o_ref, s_scr, hg_scr):
    e = pl.program_id(1)
    s_scr[...] = jnp.dot(u_ref[...], xt_ref[...], preferred_element_type=F32)
    for al in range(PEER_TA):
        rows = slice(al * PEER_KEYS, (al + 1) * PEER_KEYS)
        g = None
        for h in range(PEER_HEADS):
            t = jnp.where(s2_ref[h] >= c_ref[h, al:al + 1, :], w2_ref[h], 0.0) * gw_ref[h, al:al + 1, :]
            g = t if g is None else g + t
        hg_scr[rows, :] = (_gelu_tanh(s_scr[rows, :]) * g).astype(BF16)
    part = jnp.dot(vt_ref[...], hg_scr[...], preferred_element_type=F32)

    @pl.when(e == 0)
    def _():
        o_ref[...] = part

    @pl.when(e != 0)
    def _():
        o_ref[...] += part


def peer_experts(xt_bf, u_bf, vt_bf, c, gw, s2, w2):
    d, n = xt_bf.shape
    n_exp = u_bf.shape[0]
    tm = PEER_TM
    sel_spec = pl.BlockSpec((PEER_HEADS, PEER_TA, tm), lambda i, e: (0, e, i))
    all_spec = pl.BlockSpec((PEER_HEADS, PEER_KEYS, tm), lambda i, e: (0, 0, i))
    return pl.pallas_call(
        _peer_expert_kernel,
        grid=(n // tm, n_exp // PEER_TE),
        in_specs=[pl.BlockSpec((d, tm), lambda i, e: (0, i)),
                  pl.BlockSpec((PEER_TE, d), lambda i, e: (e, 0)),
                  pl.BlockSpec((d, PEER_TE), lambda i, e: (0, e)),
                  sel_spec, sel_spec, all_spec, all_spec],
        out_specs=pl.BlockSpec((d, tm), lambda i, e: (0, i)),
        out_shape=jax.ShapeDtypeStruct((d, n), F32),
        scratch_shapes=[pltpu.VMEM((PEER_TE, tm), F32), pltpu.VMEM((PEER_TE, tm), BF16)],
        compiler_params=pltpu.CompilerParams(
            dimension_semantics=("parallel", "arbitrary"), vmem_limit_bytes=VMEM_LIMIT),
        name="peer_experts",
    )(xt_bf, u_bf, vt_bf, c, gw, s2, w2)


def _rmsnorm_kernel(x_ref, g_ref, o_ref):
    x = x_ref[...]
    ms = jnp.mean(x * x, axis=-1, keepdims=True)
    o_ref[...] = (x * lax.rsqrt(ms + EPS) * g_ref[...]).astype(o_ref.dtype)


def rmsnorm(x, gain, dtype=F32):
    n, k = x.shape
    tm = _row_tile(n, 1040)
    return pl.pallas_call(
        _rmsnorm_kernel,
        grid=(n // tm,),
        in_specs=[pl.BlockSpec((tm, k), lambda i: (i, 0)), pl.BlockSpec((1, k), lambda i: (0, 0))],
        out_specs=pl.BlockSpec((tm, k), lambda i: (i, 0)),
        out_shape=jax.ShapeDtypeStruct((n, k), dtype),
        compiler_params=pltpu.CompilerParams(dimension_semantics=("parallel",), vmem_limit_bytes=VMEM_LIMIT),
        name="rmsnorm",
    )(x, gain.reshape(1, k))


def peer_layer(x, gain, w_q, keys, u, v):
    q_hm = norm_matmul(x, gain, w_q.astype(BF16), head_major_cols=PEER_QDIM)
    c, gw, s2, w2 = peer_router(q_hm, keys)
    xt_bf = rmsnorm(x, gain, BF16).T
    out_t = peer_experts(xt_bf, u.astype(BF16), v.T.astype(BF16), c, gw, s2, w2)
    return x + out_t.T


def _rotary(x, pos, rot_dims, theta):
    half = rot_dims // 2
    inv = theta ** (-jnp.arange(half, dtype=F32) / half)
    ang = pos.astype(F32)[:, None] * inv[None, :]
    shape = (pos.shape[0],) + (1,) * (x.ndim - 3) + (half,)
    cos = jnp.cos(ang).reshape(shape)
    sin = jnp.sin(ang).reshape(shape)
    x1 = x[..., :half]
    x2 = x[..., half:rot_dims]
    rot = jnp.concatenate([x1 * cos - x2 * sin, x1 * sin + x2 * cos], axis=-1)
    return jnp.concatenate([rot, x[..., rot_dims:]], axis=-1)


def _masked_softmax(s, mask, axis=-1):
    s = jnp.where(mask, s, -1e30)
    p = jnp.exp(s - jnp.max(s, axis=axis, keepdims=True)) * mask
    return p / jnp.maximum(jnp.sum(p, axis=axis, keepdims=True), 1e-30)


def _causal_conv(u, buf, w, b=None):
    k_size = w.shape[0]
    t_len = u.shape[1]
    ext = jnp.concatenate([buf.astype(u.dtype), u], axis=1)
    y = ext[:, 0:t_len] * w[0]
    for j in range(1, k_size):
        y = y + ext[:, j:j + t_len] * w[j]
    if b is not None:
        y = y + b
    return y, ext[:, t_len:]


def _to_chunks(a, chunk):
    return a.reshape(a.shape[0], a.shape[1] // chunk, chunk, *a.shape[2:]).swapaxes(0, 1)


def _nsa_compress(rows, pe, w1, w2):
    bn, length, g, d = rows.shape
    n_part = CMP_BLOCK // CMP_STRIDE
    n_sub = length // CMP_STRIDE
    n_cmp = n_sub - n_part + 1
    sub = rows[:, :n_sub * CMP_STRIDE].reshape(bn, n_sub, CMP_STRIDE, g, d)
    w1p = w1.reshape(n_part, CMP_STRIDE, d, d)
    pre = jnp.einsum('jsd,jsde->e', pe.reshape(n_part, CMP_STRIDE, d), w1p)
    for j in range(n_part):
        pre = pre + jnp.einsum('bnsgd,sde->bnge', sub[:, j:j + n_cmp], w1p[j])
    return jnp.einsum('bnge,ef->bngf', jax.nn.gelu(pre), w2)


def _nsa_sparse(q, q_pos, rows, cmp_pe, cmp_w1, cmp_w2):
    bn, t_len, g, r, d = q.shape
    length = rows.shape[1]
    kc = _nsa_compress(rows[:, :, 0], cmp_pe[0], cmp_w1[0], cmp_w2[0])
    vc = _nsa_compress(rows[:, :, 1], cmp_pe[1], cmp_w1[1], cmp_w2[1])
    n_cmp = kc.shape[1]
    c_start = jnp.arange(n_cmp) * CMP_STRIDE
    c_mask = (c_start + CMP_BLOCK - 1)[None, :] <= q_pos[:, None]
    p_c = _masked_softmax(jnp.einsum('btgrd,bngd->btgrn', q, kc), c_mask[None, :, None, None, :])
    o_cmp = jnp.einsum('btgrn,bngd->btgrd', p_c, vc)
    n_slc = -(-length // SLC_BLOCK)
    s_start = jnp.arange(n_slc) * SLC_BLOCK
    overlap = ((c_start[:, None] < s_start[None, :] + SLC_BLOCK)
               & (c_start[:, None] + CMP_BLOCK > s_start[None, :])).astype(F32)
    imp = jnp.einsum('btgrn,nm->btgm', p_c, overlap)
    blk = jnp.arange(n_slc)[None, :]
    cur = (q_pos // SLC_BLOCK)[:, None]
    valid = (s_start[None, :] <= q_pos[:, None])[None, :, None, :]
    forced = ((blk == 0) | (blk == cur) | (blk == cur - 1))[None, :, None, :]
    score = jnp.where(valid, jnp.where(forced, 1e6, imp), -1e6)
    n_sel = min(SLC_TOPK, n_slc)
    _, idx = lax.top_k(score, n_sel)
    pad = n_slc * SLC_BLOCK - length
    kv_s = jnp.pad(rows[:, :, 2:4], ((0, 0), (0, pad), (0, 0), (0, 0), (0, 0)))
    kv_blk = kv_s.reshape(bn, n_slc, SLC_BLOCK, 2, g, d).transpose(0, 4, 1, 2, 3, 5)
    bi = jnp.arange(bn)[:, None, None, None]
    gi = jnp.arange(g)[None, None, :, None]
    qb = Q_BLOCK if t_len % Q_BLOCK == 0 else t_len
    nqb = t_len // qb

    def sel_block(args):
        q_b, idx_b, pos_b = args
        kv_g = kv_blk[bi, gi, idx_b]
        s = jnp.einsum('bqgrd,bqgksd->bqgrks', q_b, kv_g[..., 0, :])
        kpos = idx_b[..., None] * SLC_BLOCK + jnp.arange(SLC_BLOCK)
        mask = (kpos <= pos_b[None, :, None, None, None])[:, :, :, None]
        p = _masked_softmax(s, mask, axis=(-2, -1))
        return jnp.einsum('bqgrks,bqgksd->bqgrd', p, kv_g[..., 1, :])

    q_blocks = q.reshape(bn, nqb, qb, g, r, d).swapaxes(0, 1)
    idx_blocks = idx.reshape(bn, nqb, qb, g, n_sel).swapaxes(0, 1)
    o_slc = lax.map(sel_block, (q_blocks, idx_blocks, q_pos.reshape(nqb, qb)))
    o_slc = o_slc.swapaxes(0, 1).reshape(bn, t_len, g, r, d)
    return o_cmp, o_slc


def _window_attention(q, q_pos, k, v, k_pos):
    s = jnp.einsum('ntgrd,nsgd->ntgrs', q, k)
    dpos = q_pos[:, :, None] - k_pos[:, None, :]
    mask = (dpos >= 0) & (dpos < WINDOW) & (k_pos[:, None, :] >= 0)
    p = _masked_softmax(s, mask[:, :, None, None, :])
    return jnp.einsum('ntgrs,nsgd->ntgrd', p, v)


def _window_prompt(q, win_rows, pos):
    bn, t_len, g, r, d = q.shape
    qb = Q_BLOCK if t_len % Q_BLOCK == 0 else t_len
    nqb = t_len // qb
    span = WINDOW + qb
    padded = jnp.pad(win_rows, ((0, 0), (WINDOW, 0), (0, 0), (0, 0), (0, 0)))
    kidx = jnp.arange(nqb)[:, None] * qb + jnp.arange(span)[None, :]
    kw = padded[:, kidx].reshape(bn * nqb, span, 2, g, d)
    k_pos = jnp.broadcast_to(kidx - WINDOW, (bn, nqb, span)).reshape(bn * nqb, span)
    q_pos = jnp.broadcast_to(pos.reshape(nqb, qb), (bn, nqb, qb)).reshape(bn * nqb, qb)
    o = _window_attention(q.reshape(bn * nqb, qb, g, r, d), q_pos, kw[:, :, 0], kw[:, :, 1], k_pos)
    return o.reshape(bn, t_len, g, r, d)


def _even_mixer(proj, pos, past_rows, win_buf, conv_buf, cmp_pe, cmp_w1, cmp_w2, sc_w):
    bn, t_len, _ = proj.shape
    g, r, d = NSA_KV_HEADS, NSA_GROUP, NSA_HEAD_DIM
    q, kv, gates, sc = jnp.split(proj, [E_Q, E_Q + E_KV, E_Q + E_KV + E_G], axis=-1)
    q = _rotary(q.reshape(bn, t_len, g, r, d), pos, ROPE_DIMS, ROPE_THETA) * (d ** -0.5)
    kv = kv.reshape(bn, t_len, 3, 2, g, d)
    k = _rotary(kv[:, :, :, 0], pos, ROPE_DIMS, ROPE_THETA)
    v = kv[:, :, :, 1]
    new_rows = jnp.stack([k[:, :, 0], v[:, :, 0], k[:, :, 1], v[:, :, 1]], axis=2)
    win_rows = jnp.stack([k[:, :, 2], v[:, :, 2]], axis=2)
    if past_rows is None:
        rows = new_rows
    else:
        rows = jnp.concatenate([past_rows, new_rows], axis=1)
    o_cmp, o_slc = _nsa_sparse(q, pos, rows, cmp_pe, cmp_w1, cmp_w2)
    if win_buf is None:
        o_win = _window_prompt(q, win_rows, pos)
        new_win = win_rows[:, t_len - min(WINDOW, t_len):]
    else:
        w_len = win_buf.shape[1]
        ext = jnp.concatenate([win_buf, win_rows], axis=1)
        k_pos = (pos[0] - w_len + jnp.arange(w_len + t_len))[None]
        o_win = _window_attention(q, pos[None], ext[:, :, 0], ext[:, :, 1], k_pos)
        new_win = ext[:, t_len:]
    gate = jax.nn.sigmoid(gates).reshape(bn, t_len, 3, g, r, 1)
    o_nsa = (gate[:, :, 0] * o_cmp + gate[:, :, 1] * o_slc + gate[:, :, 2] * o_win)
    o_nsa = o_nsa.reshape(bn, t_len, E_Q)
    b_gate, c_gate, h_in = jnp.split(sc, 3, axis=-1)
    if conv_buf is None:
        conv_buf = jnp.zeros((bn, SC_KSIZE - 1, SC_WIDTH), F32)
    conv_out, new_conv = _causal_conv(c_gate * h_in, conv_buf, sc_w)
    o_sc = b_gate * conv_out
    return jnp.concatenate([o_nsa, o_sc], axis=-1), new_rows, new_win, new_conv


def _retention(q, k, v, state, chunk):
    bn, t_len, h, _ = q.shape
    dv = v.shape[-1]
    log_g = jnp.log1p(-(2.0 ** (-5.0 - jnp.arange(h, dtype=F32))))
    i = jnp.arange(chunk, dtype=F32)
    diff = i[:, None] - i[None, :]
    intra = jnp.where(diff >= 0, jnp.exp(jnp.maximum(diff, 0.0)[None] * log_g[:, None, None]), 0.0)
    q_dec = jnp.exp((i[:, None] + 1.0) * log_g[None, :])
    k_dec = jnp.exp((chunk - 1.0 - i)[:, None] * log_g[None, :])
    c_dec = jnp.exp(chunk * log_g)

    def step(s_mat, inp):
        qc, kc, vc = inp
        att = jnp.einsum('bihd,bjhd->bhij', qc, kc) * intra
        o = (jnp.einsum('bhij,bjhv->bihv', att, vc)
             + jnp.einsum('bihd,bhdv->bihv', qc, s_mat) * q_dec[None, :, :, None])
        s_mat = s_mat * c_dec[None, :, None, None] + jnp.einsum('bjhd,bjhv->bhdv', kc * k_dec[None, :, :, None], vc)
        return s_mat, o

    s_fin, o = lax.scan(step, state, (_to_chunks(q, chunk), _to_chunks(k, chunk), _to_chunks(v, chunk)))
    return o.swapaxes(0, 1).reshape(bn, t_len, h, dv), s_fin


def _ssd(x, dt, a_neg, b_in, c_in, state, chunk):
    bn, t_len, h, p = x.shape
    rep = h // b_in.shape[2]
    a = dt * a_neg
    b_h = jnp.repeat(b_in, rep, axis=2)
    c_h = jnp.repeat(c_in, rep, axis=2)
    xdt = x * dt[..., None]
    tri = jnp.arange(chunk)[:, None] >= jnp.arange(chunk)[None, :]

    def step(s_mat, inp):
        a_c, x_c, b_c, c_c = inp
        cum = jnp.cumsum(a_c, axis=1)
        seg = cum[:, :, None, :] - cum[:, None, :, :]
        l_mat = jnp.exp(jnp.where(tri[None, :, :, None], seg, -1e30))
        cb = jnp.einsum('bihn,bjhn->bijh', c_c, b_c) * l_mat
        y = (jnp.einsum('bijh,bjhp->bihp', cb, x_c)
             + jnp.einsum('bihn,bhpn->bihp', c_c, s_mat) * jnp.exp(cum)[..., None])
        dec_end = jnp.exp(cum[:, -1:, :] - cum)
        s_mat = (s_mat * jnp.exp(cum[:, -1, :])[:, :, None, None]
                 + jnp.einsum('bjhn,bjhp->bhpn', b_c * dec_end[..., None], x_c))
        return s_mat, y

    s_fin, y = lax.scan(step, state, (_to_chunks(a, chunk), _to_chunks(xdt, chunk), _to_chunks(b_h, chunk), _to_chunks(c_h, chunk)))
    return y.swapaxes(0, 1).reshape(bn, t_len, h, p), s_fin


def _odd_mixer(proj, pos, ret_state, ssm_state, conv_buf, gn_gain, conv_w, conv_b, dt_bias, a_log, d_skip, m2_norm):
    bn, t_len, _ = proj.shape
    chunk = CHUNK if t_len % CHUNK == 0 else t_len
    q, k, v, g, z, xbc, dt = jnp.split(proj, O_SPLITS, axis=-1)
    q = _rotary(q.reshape(bn, t_len, RET_HEADS, RET_DQK), pos, RET_DQK, RET_THETA) * (RET_DQK ** -0.5)
    k = _rotary(k.reshape(bn, t_len, RET_HEADS, RET_DQK), pos, RET_DQK, RET_THETA)
    v = v.reshape(bn, t_len, RET_HEADS, RET_DV)
    if ret_state is None:
        ret_state = jnp.zeros((bn, RET_HEADS, RET_DQK, RET_DV), F32)
    o_ret, ret_new = _retention(q, k, v, ret_state, chunk)
    mu = jnp.mean(o_ret, axis=-1, keepdims=True)
    var = jnp.mean(jnp.square(o_ret - mu), axis=-1, keepdims=True)
    o_ret = ((o_ret - mu) * lax.rsqrt(var + EPS)).reshape(bn, t_len, R_V) * gn_gain
    o_ret = jax.nn.silu(g) * o_ret
    if conv_buf is None:
        conv_buf = jnp.zeros((bn, M2_CONV - 1, M2_CONV_DIM), F32)
    xbc_c, conv_new = _causal_conv(xbc, conv_buf, conv_w, conv_b)
    xbc_c = jax.nn.silu(xbc_c)
    xs, b_in, c_in = jnp.split(xbc_c, [M2_DINNER, M2_DINNER + M2_GROUPS * M2_STATE], axis=-1)
    xs = xs.reshape(bn, t_len, M2_HEADS, M2_HEADDIM)
    b_in = b_in.reshape(bn, t_len, M2_GROUPS, M2_STATE)
    c_in = c_in.reshape(bn, t_len, M2_GROUPS, M2_STATE)
    dt = jax.nn.softplus(dt + dt_bias)
    a_neg = -jnp.exp(a_log)
    if ssm_state is None:
        ssm_state = jnp.zeros((bn, M2_HEADS, M2_HEADDIM, M2_STATE), F32)
    y, ssm_new = _ssd(xs, dt, a_neg, b_in, c_in, ssm_state, chunk)
    y = y + d_skip[:, None] * xs
    y = y.reshape(bn, t_len, M2_DINNER) * jax.nn.silu(z)
    yg = y.reshape(bn, t_len, M2_GROUPS, M2_DINNER // M2_GROUPS)
    yg = yg * lax.rsqrt(jnp.mean(yg * yg, axis=-1, keepdims=True) + EPS)
    y = yg.reshape(bn, t_len, M2_DINNER) * m2_norm
    return jnp.concatenate([o_ret, y], axis=-1), ret_new, ssm_new, conv_new


def kernel(x_prompt, x_sample, cache_nsa_kv, cache_nsa_win, state_sc_conv, state_ret, state_ssm, state_m2_conv,
           page_table, norm_w, final_norm, e_w_in, e_w_out, e_cmp_pe, e_cmp_w1, e_cmp_w2, e_sc_conv, o_w_in,
           o_w_out, o_ret_gn, o_m2_conv_w, o_m2_conv_b, o_m2_dt_bias, o_m2_a_log, o_m2_d, o_m2_norm, peer_wq,
           peer_keys, peer_u, peer_v):
    bp, tp, dm = x_prompt.shape
    bs, ts, _ = x_sample.shape
    n_p, n_s = bp * tp, bs * ts
    past_len = page_table.shape[1] * PAGE_SIZE
    pos_p = jnp.arange(tp, dtype=jnp.int32)
    pos_s = past_len + jnp.arange(ts, dtype=jnp.int32)
    x = jnp.concatenate([x_prompt.reshape(n_p, dm), x_sample.reshape(n_s, dm)], axis=0)

    m_pad = _pad_cols(E_IN)
    w_in = jnp.pad(e_w_in[0], ((0, 0), (0, m_pad - E_IN))).astype(BF16)
    proj = norm_matmul(x, norm_w[0, 0], w_in)[:, :E_IN]
    mix_p, p_kv, p_win, p_sc = _even_mixer(proj[:n_p].reshape(bp, tp, E_IN), pos_p, None, None, None,
                                           e_cmp_pe[0], e_cmp_w1[0], e_cmp_w2[0], e_sc_conv[0])
    pages = cache_nsa_kv[0][page_table]
    past_rows = pages.reshape(pages.shape[0], -1, *pages.shape[3:])
    mix_s, s_kv, s_win, s_sc = _even_mixer(proj[n_p:].reshape(bs, ts, E_IN), pos_s, past_rows, cache_nsa_win[0],
                                           state_sc_conv[0], e_cmp_pe[0], e_cmp_w1[0], e_cmp_w2[0], e_sc_conv[0])
    mix = jnp.concatenate([mix_p.reshape(n_p, -1), mix_s.reshape(n_s, -1)], axis=0).astype(BF16)
    x = matmul_res(mix, e_w_out[0].astype(BF16), x)
    x = peer_layer(x, norm_w[0, 1], peer_wq[0], peer_keys[0], peer_u[0], peer_v[0])

    m_pad = _pad_cols(O_IN)
    w_in = jnp.pad(o_w_in[0], ((0, 0), (0, m_pad - O_IN))).astype(BF16)
    proj = norm_matmul(x, norm_w[1, 0], w_in)[:, :O_IN]
    odd_w = (o_ret_gn[0], o_m2_conv_w[0], o_m2_conv_b[0], o_m2_dt_bias[0], o_m2_a_log[0], o_m2_d[0], o_m2_norm[0])
    mix_p, p_ret, p_ssm, p_m2c = _odd_mixer(proj[:n_p].reshape(bp, tp, O_IN), pos_p, None, None, None, *odd_w)
    mix_s, s_ret, s_ssm, s_m2c = _odd_mixer(proj[n_p:].reshape(bs, ts, O_IN), pos_s, state_ret[0], state_ssm[0],
                                            state_m2_conv[0], *odd_w)
    mix = jnp.concatenate([mix_p.reshape(n_p, -1), mix_s.reshape(n_s, -1)], axis=0).astype(BF16)
    x = matmul_res(mix, o_w_out[0].astype(BF16), x)
    x = peer_layer(x, norm_w[1, 1], peer_wq[1], peer_keys[1], peer_u[1], peer_v[1])

    y = rmsnorm(x, final_norm)
    y_prompt = y[:n_p].reshape(bp, tp, dm)
    y_sample = y[n_p:].reshape(bs, ts, dm)
    return (y_prompt, y_sample, p_kv[None], p_win[None], p_sc[None], p_ret[None], p_ssm[None], p_m2c[None],
            s_kv[None], s_win[None], s_sc[None], s_ret[None], s_ssm[None], s_m2c[None])
```

```python
import math

import jax
import jax.numpy as jnp
from jax import lax
from jax.experimental import pallas as pl
from jax.experimental.pallas import tpu as pltpu

F32 = jnp.float32
BF16 = jnp.bfloat16

D_MODEL = 2048
DEPTH = 2
PAGE_SIZE = 128
NSA_HEAD_DIM = 128
NSA_HEADS = 8
NSA_KV_HEADS = 2
NSA_GROUP = 4
CMP_BLOCK = 32
CMP_STRIDE = 16
SLC_BLOCK = 64
SLC_TOPK = 16
WINDOW = 512
ROPE_THETA = 500000.0
ROPE_DIMS = 32
SC_WIDTH = 1024
SC_KSIZE = 3
RET_HEADS = 4
RET_DQK = 256
RET_DV = 256
RET_THETA = 10000.0
M2_DINNER = 1024
M2_HEADDIM = 64
M2_HEADS = 16
M2_STATE = 128
M2_GROUPS = 2
M2_CONV = 4
M2_CONV_DIM = M2_DINNER + 2 * M2_GROUPS * M2_STATE
PEER_HEADS = 8
PEER_KEYS = 128
PEER_QDIM = 256
PEER_TOPK = 16
Q_BLOCK = 128
CHUNK = 128
EPS = 1e-6

E_Q = NSA_HEADS * NSA_HEAD_DIM
E_KV = 6 * NSA_KV_HEADS * NSA_HEAD_DIM
E_G = 3 * NSA_HEADS
E_SC = 3 * SC_WIDTH
E_IN = E_Q + E_KV + E_G + E_SC
R_QK = RET_HEADS * RET_DQK
R_V = RET_HEADS * RET_DV
O_SPLITS = [R_QK, 2 * R_QK, 2 * R_QK + R_V, 2 * R_QK + 2 * R_V,
            2 * R_QK + 2 * R_V + M2_DINNER, 2 * R_QK + 2 * R_V + M2_DINNER + M2_CONV_DIM]
O_IN = O_SPLITS[-1] + M2_HEADS

LANES = 128
VMEM_LIMIT = 56 * 1024 * 1024
NEG_INF = float("-inf")


def _round_up(n, m):
    return -(-n // m) * m


def _pick_tile(n, cap):
    best = LANES
    for t in range(LANES, cap + 1, LANES):
        if n % t == 0:
            best = t
    return best


def _pad_cols(m):
    return min((_round_up(m, t) for t in (768, 640, 512)))


def _row_tile(n, cap):
    best = None
    for t in range(16, cap + 1, 16):
        if n % t == 0:
            best = t
    assert best is not None
    return best


def _gelu_tanh(x):
    return 0.5 * x * (1.0 + jnp.tanh(math.sqrt(2.0 / math.pi) * (x + 0.044715 * (x * x * x))))


def _norm_matmul_kernel(x_ref, g_ref, w_ref, o_ref, xn_ref):
    @pl.when(pl.program_id(1) == 0)
    def _():
        x = x_ref[...]
        ms = jnp.mean(x * x, axis=-1, keepdims=True)
        xn_ref[...] = (x * lax.rsqrt(ms + EPS) * g_ref[...]).astype(BF16)

    o_ref[...] = jnp.dot(xn_ref[...], w_ref[...], preferred_element_type=F32).reshape(o_ref.shape)


def norm_matmul(x, gain, w_bf, *, head_major_cols=None):
    n, k = x.shape
    m = w_bf.shape[1]
    tm = _row_tile(n, 1040)
    if head_major_cols is None:
        tn = _pick_tile(m, 768)
        out_shape = jax.ShapeDtypeStruct((n, m), F32)
        out_spec = pl.BlockSpec((tm, tn), lambda i, j: (i, j))
    else:
        tn = head_major_cols
        out_shape = jax.ShapeDtypeStruct((m // tn, n, tn), F32)
        out_spec = pl.BlockSpec((1, tm, tn), lambda i, j: (j, i, 0))
    return pl.pallas_call(
        _norm_matmul_kernel,
        grid=(n // tm, m // tn),
        in_specs=[pl.BlockSpec((tm, k), lambda i, j: (i, 0)),
                  pl.BlockSpec((1, k), lambda i, j: (0, 0)),
                  pl.BlockSpec((k, tn), lambda i, j: (0, j))],
        out_specs=out_spec,
        out_shape=out_shape,
        scratch_shapes=[pltpu.VMEM((tm, k), BF16)],
        compiler_params=pltpu.CompilerParams(
            dimension_semantics=("parallel", "arbitrary"), vmem_limit_bytes=VMEM_LIMIT),
        name="norm_matmul",
    )(x, gain.reshape(1, k), w_bf)


def _matmul2_res_kernel(a1_ref, a2_ref, w1_ref, w2_ref, r_ref, o_ref):
    o_ref[...] = (r_ref[...] + jnp.dot(a1_ref[...], w1_ref[...], preferred_element_type=F32)
                  + jnp.dot(a2_ref[...], w2_ref[...], preferred_element_type=F32))


def matmul2_res(a1, a2, w_bf, res):
    n, k1 = a1.shape
    k2 = a2.shape[1]
    m = w_bf.shape[1]
    assert k1 == k2
    tm = _row_tile(n, 1040)
    tn = _pick_tile(m, 1024)
    return pl.pallas_call(
        _matmul2_res_kernel,
        grid=(n // tm, m // tn),
        in_specs=[pl.BlockSpec((tm, k1), lambda i, j: (i, 0)),
                  pl.BlockSpec((tm, k2), lambda i, j: (i, 0)),
                  pl.BlockSpec((k1, tn), lambda i, j: (0, j)),
                  pl.BlockSpec((k2, tn), lambda i, j: (1, j)),
                  pl.BlockSpec((tm, tn), lambda i, j: (i, j))],
        out_specs=pl.BlockSpec((tm, tn), lambda i, j: (i, j)),
        out_shape=jax.ShapeDtypeStruct((n, m), F32),
        compiler_params=pltpu.CompilerParams(
            dimension_semantics=("parallel", "parallel"), vmem_limit_bytes=VMEM_LIMIT),
        name="matmul2_res",
    )(a1, a2, w_bf, w_bf, res)


PEER_TOK = 128
PEER_TM = 640
PEER_TA = 8
PEER_TE = PEER_TA * PEER_KEYS


def _top_desc(work, count):
    rows = []
    for _ in range(count):
        m = jnp.max(work, axis=0, keepdims=True)
        rows.append(m)
        work = jnp.where(work == m, NEG_INF, work)
    return rows


def _peer_router_kernel(q_ref, k_ref, c_ref, gw_ref, s2_ref, w2_ref):
    def head(h, carry):
        q = q_ref[h]
        scores = []
        for side in range(2):
            qs = q[:, side * PEER_KEYS:(side + 1) * PEER_KEYS]
            qs = qs * lax.rsqrt(jnp.mean(qs * qs, axis=-1, keepdims=True) + EPS)
            scores.append(lax.dot_general(k_ref[h, side], qs, (((1,), (1,)), ((), ())),
                                          preferred_element_type=F32))
        s1, s2 = scores
        v1 = _top_desc(s1, PEER_TOPK + 1)
        v2 = _top_desc(s2, PEER_TOPK + 1)
        v2_lo = jnp.concatenate(v2[:8], axis=0)
        v2_hi = jnp.concatenate(v2[8:16], axis=0)
        row = lax.broadcasted_iota(jnp.int32, v2_lo.shape, 0)
        blocks = [v1[0] + v2_lo, v1[0] + v2_hi, v1[1] + v2_lo]
        for a, lim in ((2, 5), (3, 4), (4, 3), (5, 2), (6, 2), (7, 2)):
            blocks.append(jnp.where(row < lim, v1[a] + v2_lo, NEG_INF))
        blocks.append(jnp.concatenate(v1[8:16], axis=0) + v2[0])
        extra = jnp.where(row == 0, v1[0] + v2[16], jnp.where(row == 1, v1[16] + v2[0], NEG_INF))
        blocks.append(extra)
        cand = jnp.concatenate(blocks, axis=0)
        tops = _top_desc(cand, PEER_TOPK + 1)
        z = jnp.zeros_like(tops[0])
        for r in range(PEER_TOPK):
            z = z + jnp.exp(tops[r] - tops[0])
        tau = 0.5 * (tops[PEER_TOPK - 1] + tops[PEER_TOPK])
        c_ref[h] = tau - s1
        gw_ref[h] = jnp.exp(s1 - v1[0]) / z
        s2_ref[h] = s2
        w2_ref[h] = jnp.exp(s2 - v2[0])
        return carry

    lax.fori_loop(0, PEER_HEADS, head, 0)


def peer_router(q_hm, keys):
    n = q_hm.shape[1]
    out = jax.ShapeDtypeStruct((PEER_HEADS, PEER_KEYS, n), F32)
    spec = pl.BlockSpec((PEER_HEADS, PEER_KEYS, PEER_TOK), lambda i: (0, 0, i))
    return pl.pallas_call(
        _peer_router_kernel,
        grid=(n // PEER_TOK,),
        in_specs=[pl.BlockSpec((PEER_HEADS, PEER_TOK, PEER_QDIM), lambda i: (0, i, 0)),
                  pl.BlockSpec((PEER_HEADS, 2, PEER_KEYS, PEER_QDIM // 2), lambda i: (0, 0, 0, 0))],
        out_specs=[spec, spec, spec, spec],
        out_shape=[out, out, out, out],
        compiler_params=pltpu.CompilerParams(dimension_semantics=("parallel",), vmem_limit_bytes=VMEM_LIMIT),
        name="peer_router",
    )(q_hm, keys)


def _peer_expert_kernel(xt_ref, u_ref, vt_ref, c_ref, gw_ref, s2_ref, w2_ref, o_ref, s_scr, hg_scr):
    e = pl.program_id(1)
    s_scr[...] = jnp.dot(u_ref[...], xt_ref[...], preferred_element_type=F32)
    for al in range(PEER_TA):
        rows = slice(al * PEER_KEYS, (al + 1) * PEER_KEYS)
        g = None
        for h in range(PEER_HEADS):
            t = jnp.where(s2_ref[h] >= c_ref[h, al:al + 1, :], w2_ref[h], 0.0) * gw_ref[h, al:al + 1, :]
            g = t if g is None else g + t
        hg_scr[rows, :] = (_gelu_tanh(s_scr[rows, :]) * g).astype(BF16)
    part = jnp.dot(vt_ref[...], hg_scr[...], preferred_element_type=F32)

    @pl.when(e == 0)
    def _():
        o_ref[...] = part

    @pl.when(e != 0)
    def _():
        o_ref[...] += part


def peer_experts(xt_bf, u_bf, vt_bf, c, gw, s2, w2):
    d, n = xt_bf.shape
    n_exp = u_bf.shape[0]
    tm = PEER_TM
    sel_spec = pl.BlockSpec((PEER_HEADS, PEER_TA, tm), lambda i, e: (0, e, i))
    all_spec = pl.BlockSpec((PEER_HEADS, PEER_KEYS, tm), lambda i, e: (0, 0, i))
    return pl.pallas_call(
        _peer_expert_kernel,
        grid=(n // tm, n_exp // PEER_TE),
        in_specs=[pl.BlockSpec((d, tm), lambda i, e: (0, i)),
                  pl.BlockSpec((PEER_TE, d), lambda i, e: (e, 0)),
                  pl.BlockSpec((d, PEER_TE), lambda i, e: (0, e)),
                  sel_spec, sel_spec, all_spec, all_spec],
        out_specs=pl.BlockSpec((d, tm), lambda i, e: (0, i)),
        out_shape=jax.ShapeDtypeStruct((d, n), F32),
        scratch_shapes=[pltpu.VMEM((PEER_TE, tm), F32), pltpu.VMEM((PEER_TE, tm), BF16)],
        compiler_params=pltpu.CompilerParams(
            dimension_semantics=("parallel", "arbitrary"), vmem_limit_bytes=VMEM_LIMIT),
        name="peer_experts",
    )(xt_bf, u_bf, vt_bf, c, gw, s2, w2)


def _rmsnorm_kernel(x_ref, g_ref, o_ref):
    x = x_ref[...]
    ms = jnp.mean(x * x, axis=-1, keepdims=True)
    o_ref[...] = (x * lax.rsqrt(ms + EPS) * g_ref[...]).astype(o_ref.dtype)


def rmsnorm(x, gain, dtype=F32):
    n, k = x.shape
    tm = _row_tile(n, 1040)
    return pl.pallas_call(
        _rmsnorm_kernel,
        grid=(n // tm,),
        in_specs=[pl.BlockSpec((tm, k), lambda i: (i, 0)), pl.BlockSpec((1, k), lambda i: (0, 0))],
        out_specs=pl.BlockSpec((tm, k), lambda i: (i, 0)),
        out_shape=jax.ShapeDtypeStruct((n, k), dtype),
        compiler_params=pltpu.CompilerParams(dimension_semantics=("parallel",), vmem_limit_bytes=VMEM_LIMIT),
        name="rmsnorm",
    )(x, gain.reshape(1, k))


def peer_layer(x, gain, w_q, keys, u, v):
    q_hm = norm_matmul(x, gain, w_q.astype(BF16), head_major_cols=PEER_QDIM)
    c, gw, s2, w2 = peer_router(q_hm, keys)
    xt_bf = rmsnorm(x, gain, BF16).T
    out_t = peer_experts(xt_bf, u.astype(BF16), v.T.astype(BF16), c, gw, s2, w2)
    return x + out_t.T


ROPE_TM = 416


def _rope_tables(pos):
    half = ROPE_DIMS // 2
    inv = ROPE_THETA ** (-jnp.arange(half, dtype=F32) / half)
    ang = pos.astype(F32)[:, None] * inv[None, :]
    cos, sin = jnp.cos(ang), jnp.sin(ang)
    t = pos.shape[0]
    ones = jnp.ones((t, NSA_HEAD_DIM - ROPE_DIMS), F32)
    zeros = jnp.zeros((t, NSA_HEAD_DIM - ROPE_DIMS), F32)
    zh = jnp.zeros((t, half), F32)
    c = jnp.concatenate([cos, cos, ones], axis=1)
    s_lo = jnp.concatenate([-sin, zh, zeros], axis=1)
    s_hi = jnp.concatenate([zh, sin, zeros], axis=1)
    return c, s_lo, s_hi


def _rope_kernel(p_ref, c_ref, sl_ref, sh_ref, q_ref, rows_ref, win_ref, rows_bf_ref, win_bf_ref):
    c, sl, sh = c_ref[...], sl_ref[...], sh_ref[...]
    half = ROPE_DIMS // 2

    def rot(x):
        return x * c + pltpu.roll(x, LANES - half, 1) * sl + pltpu.roll(x, half, 1) * sh

    d = NSA_HEAD_DIM
    scale = d ** -0.5
    for hd in range(NSA_HEADS):
        q_ref[:, hd * d:(hd + 1) * d] = (rot(p_ref[:, hd * d:(hd + 1) * d]) * scale).astype(BF16)
    for blk in range(12):
        x = p_ref[:, E_Q + blk * d:E_Q + (blk + 1) * d]
        if (blk // 2) % 2 == 0:
            x = rot(x)
        if blk < 8:
            rows_ref[:, blk * d:(blk + 1) * d] = x
            rows_bf_ref[:, blk * d:(blk + 1) * d] = x.astype(BF16)
        else:
            win_ref[:, (blk - 8) * d:(blk - 7) * d] = x
            win_bf_ref[:, (blk - 8) * d:(blk - 7) * d] = x.astype(BF16)


def nsa_rope(proj, pos_rows):
    n = proj.shape[0]
    tm = ROPE_TM
    c, sl, sh = _rope_tables(pos_rows)
    width = E_Q + E_KV
    tab = pl.BlockSpec((tm, LANES), lambda i: (i, 0))
    return pl.pallas_call(
        _rope_kernel,
        grid=(n // tm,),
        in_specs=[pl.BlockSpec((tm, width), lambda i: (i, 0)), tab, tab, tab],
        out_specs=[pl.BlockSpec((tm, E_Q), lambda i: (i, 0)),
                   pl.BlockSpec((tm, 1024), lambda i: (i, 0)),
                   pl.BlockSpec((tm, 512), lambda i: (i, 0)),
                   pl.BlockSpec((tm, 1024), lambda i: (i, 0)),
                   pl.BlockSpec((tm, 512), lambda i: (i, 0))],
        out_shape=[jax.ShapeDtypeStruct((n, E_Q), BF16),
                   jax.ShapeDtypeStruct((n, 1024), F32),
                   jax.ShapeDtypeStruct((n, 512), F32),
                   jax.ShapeDtypeStruct((n, 1024), BF16),
                   jax.ShapeDtypeStruct((n, 512), BF16)],
        compiler_params=pltpu.CompilerParams(dimension_semantics=("parallel",), vmem_limit_bytes=VMEM_LIMIT),
        name="nsa_rope",
    )(proj, c, sl, sh)


def _compress_kernel(x_ref, pe_ref, w1_ref, w2_ref, o_ref):
    n_sub = x_ref.shape[0] // CMP_STRIDE
    acc0 = jnp.zeros((n_sub, NSA_HEAD_DIM), F32)
    acc1 = jnp.zeros((n_sub, NSA_HEAD_DIM), F32)
    for s in range(CMP_STRIDE):
        xs = x_ref[pl.ds(s, n_sub, stride=CMP_STRIDE), :]
        a0 = (xs + pe_ref[0, s:s + 1, :]).astype(BF16)
        a1 = (xs + pe_ref[0, CMP_STRIDE + s:CMP_STRIDE + s + 1, :]).astype(BF16)
        acc0 = acc0 + jnp.dot(a0, w1_ref[0, s], preferred_element_type=F32)
        acc1 = acc1 + jnp.dot(a1, w1_ref[0, CMP_STRIDE + s], preferred_element_type=F32)
    pre = acc0 + pltpu.roll(acc1, n_sub - 1, 0)
    o_ref[0, 0] = jnp.dot(_gelu_tanh(pre).astype(BF16), w2_ref[0], preferred_element_type=F32).astype(BF16)


def nsa_compress(rows, n_batch, t_len, cmp_pe, cmp_w1, cmp_w2):
    n_sub = t_len // CMP_STRIDE
    d = NSA_HEAD_DIM
    w1 = cmp_w1.reshape(2, CMP_BLOCK, d, d).astype(BF16)
    return pl.pallas_call(
        _compress_kernel,
        grid=(n_batch, 4),
        in_specs=[pl.BlockSpec((t_len, d), lambda b, c: (b, c)),
                  pl.BlockSpec((1, CMP_BLOCK, d), lambda b, c: (c // 2, 0, 0)),
                  pl.BlockSpec((1, CMP_BLOCK, d, d), lambda b, c: (c // 2, 0, 0, 0)),
                  pl.BlockSpec((1, d, d), lambda b, c: (c // 2, 0, 0))],
        out_specs=pl.BlockSpec((1, 1, n_sub, d), lambda b, c: (b, c, 0, 0)),
        out_shape=jax.ShapeDtypeStruct((n_batch, 4, n_sub, d), BF16),
        compiler_params=pltpu.CompilerParams(
            dimension_semantics=("parallel", "parallel"), vmem_limit_bytes=VMEM_LIMIT),
        name="nsa_compress",
    )(rows, cmp_pe, w1, cmp_w2.astype(BF16))


NSA_TK = 512
NSA_WTILES = WINDOW // Q_BLOCK + 1


def _masked_softmax_rows(s, mask):
    s = jnp.where(mask, s, -1e30)
    p = jnp.exp(s - jnp.max(s, axis=-1, keepdims=True))
    p = jnp.where(mask, p, 0.0)
    return p / jnp.maximum(jnp.sum(p, axis=-1, keepdims=True), 1e-30)


def _nsa_prompt_kernel(q_ref, kc_ref, vc_ref, ks_ref, vs_ref, kw_ref, vw_ref, gate_ref, o_ref):
    g = pl.program_id(1)
    qi = pl.program_id(2)
    d, r, qb = NSA_HEAD_DIM, NSA_GROUP, Q_BLOCK
    n_cmp = kc_ref.shape[2]
    n_slc = ks_ref.shape[0] // SLC_BLOCK
    nt = (((1,), (1,)), ((), ()))
    q = jnp.concatenate([q_ref[:, h * d:(h + 1) * d] for h in range(r)], axis=0)
    t_col = qi * qb + lax.broadcasted_iota(jnp.int32, (qb, 1), 0)

    s = lax.dot_general(q, kc_ref[0, 0], nt, preferred_element_type=F32).reshape(r, qb, n_cmp)
    n_idx = lax.broadcasted_iota(jnp.int32, (qb, n_cmp), 1)
    c_mask = (n_idx * CMP_STRIDE + (CMP_BLOCK - 1) <= t_col)[None]
    p_c = _masked_softmax_rows(s, c_mask)
    o_cmp = jnp.dot(p_c.reshape(r * qb, n_cmp).astype(BF16), vc_ref[0, 0], preferred_element_type=F32)

    p_sum = jnp.sum(p_c, axis=0)
    ci = lax.broadcasted_iota(jnp.int32, (n_cmp, n_slc), 0) * CMP_STRIDE
    si = lax.broadcasted_iota(jnp.int32, (n_cmp, n_slc), 1) * SLC_BLOCK
    overlap = jnp.where((ci < si + SLC_BLOCK) & (ci + CMP_BLOCK > si), 1.0, 0.0).astype(BF16)
    p_hi = p_sum.astype(BF16)
    p_lo = (p_sum - p_hi.astype(F32)).astype(BF16)
    imp = (jnp.dot(p_hi, overlap, preferred_element_type=F32)
           + jnp.dot(p_lo, overlap, preferred_element_type=F32))
    blk = lax.broadcasted_iota(jnp.int32, (qb, n_slc), 1)
    blk_f = blk.astype(F32)
    cur = t_col // SLC_BLOCK
    forced = (blk == 0) | (blk == cur) | (blk == cur - 1)
    work = jnp.where(blk * SLC_BLOCK <= t_col, jnp.where(forced, 1e6, imp), -1e6)
    sel = jnp.zeros((qb, n_slc), F32)
    for _ in range(min(SLC_TOPK, n_slc)):
        m = jnp.max(work, axis=-1, keepdims=True)
        first = jnp.min(jnp.where(work == m, blk_f, float(n_slc)), axis=-1, keepdims=True)
        pick = blk_f == first
        sel = jnp.where(pick, 1.0, sel)
        work = jnp.where(pick, NEG_INF, work)
    sel_bf = sel.astype(BF16)

    tk = NSA_TK
    bpt = tk // SLC_BLOCK

    def slc_step(kt, carry):
        m_run, l_run, acc = carry
        start = pl.multiple_of(kt * tk, tk)
        k = ks_ref[pl.ds(start, tk), :]
        v = vs_ref[pl.ds(start, tk), :]
        s = lax.dot_general(q, k, nt, preferred_element_type=F32).reshape(r, qb, tk)
        ei = lax.broadcasted_iota(jnp.int32, (n_slc, tk), 0)
        ej = lax.broadcasted_iota(jnp.int32, (n_slc, tk), 1)
        expand = jnp.where(ei == kt * bpt + ej // SLC_BLOCK, 1.0, 0.0).astype(BF16)
        picked = jnp.dot(sel_bf, expand, preferred_element_type=F32)
        kpos = start + lax.broadcasted_iota(jnp.int32, (qb, tk), 1)
        mask = ((picked > 0.5) & (kpos <= t_col))[None]
        s = jnp.where(mask, s, -1e30)
        m_new = jnp.maximum(m_run, jnp.max(s, axis=-1, keepdims=True))
        alpha = jnp.exp(m_run - m_new)
        p = jnp.where(mask, jnp.exp(s - m_new), 0.0)
        l_new = alpha * l_run + jnp.sum(p, axis=-1, keepdims=True)
        pv = jnp.dot(p.reshape(r * qb, tk).astype(BF16), v, preferred_element_type=F32)
        acc = alpha.reshape(r * qb, 1) * acc + pv
        return m_new, l_new, acc

    init = (jnp.full((r, qb, 1), -1e30, F32), jnp.zeros((r, qb, 1), F32), jnp.zeros((r * qb, d), F32))
    _, l_fin, acc = lax.fori_loop(0, (qi * qb) // tk + 1, slc_step, init)
    o_slc = acc / jnp.maximum(l_fin.reshape(r * qb, 1), 1e-30)

    k_tiles, v_tiles, pos_tiles = [], [], []
    for j in range(NSA_WTILES):
        kt = qi - (NSA_WTILES - 1) + j
        ktc = jnp.maximum(kt, 0)
        start = pl.multiple_of(ktc * qb, qb)
        k_tiles.append(kw_ref[pl.ds(start, qb), :])
        v_tiles.append(vw_ref[pl.ds(start, qb), :])
        lane = lax.broadcasted_iota(jnp.int32, (qb, qb), 1)
        pos_tiles.append(jnp.where(kt >= 0, start + lane, -1))
    k_w = jnp.concatenate(k_tiles, axis=0)
    v_w = jnp.concatenate(v_tiles, axis=0)
    k_pos = jnp.concatenate(pos_tiles, axis=1)
    span = NSA_WTILES * qb
    s = lax.dot_general(q, k_w, nt, preferred_element_type=F32).reshape(r, qb, span)
    dpos = t_col - k_pos
    w_mask = ((dpos >= 0) & (dpos < WINDOW) & (k_pos >= 0))[None]
    p_w = _masked_softmax_rows(s, w_mask)
    o_win = jnp.dot(p_w.reshape(r * qb, span).astype(BF16), v_w, preferred_element_type=F32)

    sig = jax.nn.sigmoid(gate_ref[...])
    lane = lax.broadcasted_iota(jnp.int32, sig.shape, 1)
    for h in range(r):
        rows = slice(h * qb, (h + 1) * qb)
        out = jnp.zeros((qb, d), F32)
        for branch, o_b in enumerate((o_cmp, o_slc, o_win)):
            col = branch * NSA_HEADS + g * r + h
            gate = jnp.sum(jnp.where(lane == col, sig, 0.0), axis=-1, keepdims=True)
            out = out + gate * o_b[rows]
        o_ref[:, h * d:(h + 1) * d] = out.astype(o_ref.dtype)


def nsa_prompt(q_bf, kvc, rows_bf, win_bf, proj, n_batch, t_len):
    d, r, qb = NSA_HEAD_DIM, NSA_GROUP, Q_BLOCK
    nqb = t_len // qb
    n_sub = kvc.shape[2]
    gate_blk = (E_Q + E_KV) // LANES
    seq = lambda col: pl.BlockSpec((t_len, d), lambda b, g, i: (b, col(g)))
    return pl.pallas_call(
        _nsa_prompt_kernel,
        grid=(n_batch, NSA_KV_HEADS, nqb),
        in_specs=[pl.BlockSpec((qb, r * d), lambda b, g, i: (b * nqb + i, g)),
                  pl.BlockSpec((1, 1, n_sub, d), lambda b, g, i: (b, g, 0, 0)),
                  pl.BlockSpec((1, 1, n_sub, d), lambda b, g, i: (b, 2 + g, 0, 0)),
                  seq(lambda g: 4 + g), seq(lambda g: 6 + g),
                  seq(lambda g: g), seq(lambda g: 2 + g),
                  pl.BlockSpec((qb, LANES), lambda b, g, i: (b * nqb + i, gate_blk))],
        out_specs=pl.BlockSpec((qb, r * d), lambda b, g, i: (b * nqb + i, g)),
        out_shape=jax.ShapeDtypeStruct((n_batch * t_len, E_Q), BF16),
        compiler_params=pltpu.CompilerParams(
            dimension_semantics=("parallel", "parallel", "arbitrary"), vmem_limit_bytes=VMEM_LIMIT),
        name="nsa_prompt",
    )(q_bf, kvc, kvc, rows_bf, rows_bf, win_bf, win_bf, proj)


def _ret_tables(pos, chunk):
    half = RET_DQK // 2
    inv = RET_THETA ** (-jnp.arange(half, dtype=F32) / half)
    ang = pos.astype(F32)[:, None] * inv[None, :]
    log_g = jnp.log1p(-(2.0 ** (-5.0 - jnp.arange(RET_HEADS, dtype=F32))))
    i = jnp.arange(chunk, dtype=F32)
    diff = i[:, None] - i[None, :]
    intra = jnp.where(diff >= 0, jnp.exp(jnp.maximum(diff, 0.0)[None] * log_g[:, None, None]), 0.0)
    q_dec = jnp.exp((i[None, :] + 1.0) * log_g[:, None])[..., None]
    k_dec = jnp.exp((chunk - 1.0 - i)[None, :] * log_g[:, None])[..., None]
    c_dec = jnp.exp(chunk * log_g)[:, None, None]
    return jnp.cos(ang), jnp.sin(ang), intra, q_dec, k_dec, c_dec


def _retention_kernel(q_ref, k_ref, v_ref, g_ref, cos_ref, sin_ref, intra_ref, qd_ref, kd_ref, cd_ref, gn_ref,
                      s0_ref, o_ref, s_out_ref, s_scr):
    c_idx = pl.program_id(1)

    @pl.when(c_idx == 0)
    def _():
        s_scr[...] = s0_ref[0]

    cos, sin = cos_ref[...], sin_ref[...]
    half = RET_DQK // 2
    nt = (((1,), (1,)), ((), ()))
    tn = (((0,), (0,)), ((), ()))

    def rot(x):
        x1, x2 = x[:, :half], x[:, half:]
        return jnp.concatenate([x1 * cos - x2 * sin, x1 * sin + x2 * cos], axis=-1)

    for h in range(RET_HEADS):
        cols = slice(h * RET_DQK, (h + 1) * RET_DQK)
        qr = (rot(q_ref[:, cols]) * (RET_DQK ** -0.5)).astype(BF16)
        kr = rot(k_ref[:, cols])
        v = v_ref[:, cols].astype(BF16)
        att = lax.dot_general(qr, kr.astype(BF16), nt, preferred_element_type=F32) * intra_ref[h]
        s_old = s_scr[h]
        o = (jnp.dot(att.astype(BF16), v, preferred_element_type=F32)
             + jnp.dot(qr, s_old.astype(BF16), preferred_element_type=F32) * qd_ref[h])
        s_scr[h] = s_old * cd_ref[h] + lax.dot_general((kr * kd_ref[h]).astype(BF16), v, tn,
                                                       preferred_element_type=F32)
        mu = jnp.mean(o, axis=-1, keepdims=True)
        dev = o - mu
        var = jnp.mean(dev * dev, axis=-1, keepdims=True)
        gate = g_ref[:, cols]
        on = dev * lax.rsqrt(var + EPS) * gn_ref[:, cols] * (gate * jax.nn.sigmoid(gate))
        o_ref[:, cols] = on.astype(o_ref.dtype)
    s_out_ref[0] = s_scr[...]


def retention(proj, pos, n_batch, t_len, chunk, state0, gn_gain):
    nc = t_len // chunk
    cos, sin, intra, q_dec, k_dec, c_dec = _ret_tables(pos, chunk)
    half = RET_DQK // 2
    col = lambda j: pl.BlockSpec((chunk, R_QK), lambda b, c: (b * nc + c, j))
    tab = pl.BlockSpec((chunk, half), lambda b, c: (c, 0))
    full = lambda a: pl.BlockSpec(a.shape, lambda b, c: (0,) * a.ndim)
    st = pl.BlockSpec((1, RET_HEADS, RET_DQK, RET_DV), lambda b, c: (b, 0, 0, 0))
    return pl.pallas_call(
        _retention_kernel,
        grid=(n_batch, nc),
        in_specs=[col(0), col(1), col(2), col(3), tab, tab, full(intra), full(q_dec), full(k_dec), full(c_dec),
                  pl.BlockSpec((1, R_V), lambda b, c: (0, 0)), st],
        out_specs=[pl.BlockSpec((chunk, R_V), lambda b, c: (b * nc + c, 0)), st],
        out_shape=[jax.ShapeDtypeStruct((n_batch * t_len, R_V), BF16),
                   jax.ShapeDtypeStruct((n_batch, RET_HEADS, RET_DQK, RET_DV), F32)],
        scratch_shapes=[pltpu.VMEM((RET_HEADS, RET_DQK, RET_DV), F32)],
        compiler_params=pltpu.CompilerParams(
            dimension_semantics=("parallel", "arbitrary"), vmem_limit_bytes=VMEM_LIMIT),
        name="retention",
    )(proj, proj, proj, proj, cos, sin, intra, q_dec, k_dec, c_dec, gn_gain.reshape(1, R_V), state0)


SSD_COL0 = 2 * R_QK + 2 * R_V
HEADS_PER_GROUP = M2_HEADS // M2_GROUPS


def _split3(x):
    a = x.astype(BF16)
    r = x - a.astype(F32)
    b = r.astype(BF16)
    c = (r - b.astype(F32)).astype(BF16)
    return a, b, c


def _exact_dot(mat_bf, x):
    out = None
    for piece in _split3(x):
        t = jnp.dot(mat_bf, piece, preferred_element_type=F32)
        out = t if out is None else out + t
    return out


def _exact_dot_r(x, mat_bf):
    out = None
    for piece in _split3(x):
        t = jnp.dot(piece, mat_bf, preferred_element_type=F32)
        out = t if out is None else out + t
    return out


def _ssd_kernel(z_ref, xa_ref, xb_ref, xc_ref, dt_ref, cw_ref, cb_ref, dtb_ref, aneg_ref, dskip_ref, norm_ref,
                buf0_ref, s0_ref, o_ref, s_out_ref, s_scr, prev_scr):
    c_idx = pl.program_id(1)
    chunk = z_ref.shape[0]
    nt = (((1,), (1,)), ((), ()))
    tn = (((0,), (0,)), ((), ()))

    @pl.when(c_idx == 0)
    def _():
        s_scr[...] = s0_ref[0]
        prev_scr[...] = buf0_ref[0]

    x = jnp.concatenate([xa_ref[...], xb_ref[...], xc_ref[...]], axis=1)
    prev = prev_scr[...]
    row8 = lax.broadcasted_iota(jnp.int32, prev.shape, 0)
    conv = x * cw_ref[M2_CONV - 1:M2_CONV, :]
    for k in range(1, M2_CONV):
        rolled = pltpu.roll(x, k, 0)
        top = jnp.where(row8 < k, pltpu.roll(prev, k, 0), rolled[0:8])
        shifted = jnp.concatenate([top, rolled[8:]], axis=0)
        conv = conv + shifted * cw_ref[M2_CONV - 1 - k:M2_CONV - k, :]
    prev_scr[...] = x[chunk - 8:chunk]
    conv = conv + cb_ref[...]
    xbc = conv * jax.nn.sigmoid(conv)
    xs = xbc[:, :M2_DINNER]

    dt_raw = dt_ref[...] + dtb_ref[...]
    dt = jnp.where(dt_raw > 20.0, dt_raw, jnp.log1p(jnp.exp(jnp.minimum(dt_raw, 20.0))))
    a = dt * aneg_ref[...]
    ri = lax.broadcasted_iota(jnp.int32, (chunk, chunk), 0)
    ci = lax.broadcasted_iota(jnp.int32, (chunk, chunk), 1)
    tri = ri >= ci
    cum = _exact_dot(jnp.where(tri, 1.0, 0.0).astype(BF16), a)
    cum_t = cum.T
    cum_last = cum[chunk - 1:chunk, :]
    hi = lax.broadcasted_iota(jnp.int32, (LANES, M2_DINNER), 0)
    li = lax.broadcasted_iota(jnp.int32, (LANES, M2_DINNER), 1)
    expand = jnp.where(hi == li // M2_HEADDIM, 1.0, 0.0).astype(BF16)
    dt_x = _exact_dot_r(dt, expand)
    cum_x = _exact_dot_r(cum, expand)
    last_x = _exact_dot_r(cum_last, expand)
    xdt = xs * dt_x
    x_dec = (xdt * jnp.exp(last_x - cum_x)).astype(BF16)
    xdt_bf = xdt.astype(BF16)
    e_cum_x = jnp.exp(cum_x)
    e_last_x = jnp.exp(last_x)

    y_parts = []
    for gi in range(M2_GROUPS):
        b_g = xbc[:, M2_DINNER + gi * M2_STATE:M2_DINNER + (gi + 1) * M2_STATE].astype(BF16)
        c_g = xbc[:, M2_DINNER + (M2_GROUPS + gi) * M2_STATE:M2_DINNER + (M2_GROUPS + gi + 1) * M2_STATE].astype(BF16)
        cb = lax.dot_general(c_g, b_g, nt, preferred_element_type=F32)
        gcols = slice(gi * HEADS_PER_GROUP * M2_HEADDIM, (gi + 1) * HEADS_PER_GROUP * M2_HEADDIM)
        s_old = s_scr[:, gcols]
        y_state = jnp.dot(c_g, s_old.astype(BF16), preferred_element_type=F32) * e_cum_x[:, gcols]
        s_scr[:, gcols] = s_old * e_last_x[:, gcols] + lax.dot_general(b_g, x_dec[:, gcols], tn,
                                                                        preferred_element_type=F32)
        pair_lane = lax.broadcasted_iota(jnp.int32, (chunk, LANES), 1)
        intra = []
        for pr in range(HEADS_PER_GROUP // 2):
            outs = []
            for sub in range(2):
                h = gi * HEADS_PER_GROUP + pr * 2 + sub
                seg = cum[:, h:h + 1] - cum_t[h:h + 1, :]
                l_mat = jnp.where(tri, jnp.exp(jnp.where(tri, seg, 0.0)), 0.0)
                lanes = slice(gi * HEADS_PER_GROUP * M2_HEADDIM + pr * LANES,
                              gi * HEADS_PER_GROUP * M2_HEADDIM + (pr + 1) * LANES)
                outs.append(jnp.dot((cb * l_mat).astype(BF16), xdt_bf[:, lanes], preferred_element_type=F32))
            intra.append(jnp.where(pair_lane < M2_HEADDIM, outs[0], outs[1]))
        y_parts.append(jnp.concatenate(intra, axis=1) + y_state)
    y = jnp.concatenate(y_parts, axis=1) + dskip_ref[...] * xs
    z = z_ref[...]
    y = y * (z * jax.nn.sigmoid(z))
    gw = M2_DINNER // M2_GROUPS
    outs = []
    for gi in range(M2_GROUPS):
        yg = y[:, gi * gw:(gi + 1) * gw]
        outs.append(yg * lax.rsqrt(jnp.mean(yg * yg, axis=-1, keepdims=True) + EPS))
    o_ref[...] = (jnp.concatenate(outs, axis=1) * norm_ref[...]).astype(o_ref.dtype)
    s_out_ref[0] = s_scr[...]


def ssd(proj, n_batch, t_len, chunk, conv_buf, state0, conv_w, conv_b, dt_bias, a_log, d_skip, m2_norm):
    nc = t_len // chunk
    z_blk = SSD_COL0 // M2_DINNER
    xw = M2_CONV_DIM // 3
    xbc_blk = (SSD_COL0 + M2_DINNER) // xw
    assert (SSD_COL0 + M2_DINNER) % xw == 0 and xw % LANES == 0
    dt_blk = (SSD_COL0 + M2_DINNER + M2_CONV_DIM) // LANES
    pad = lambda v: jnp.pad(v.reshape(1, -1), ((0, 0), (0, LANES - v.shape[-1])))
    buf8 = jnp.pad(conv_buf, ((0, 0), (8 - (M2_CONV - 1), 0), (0, 0)))
    st_t = state0.transpose(0, 3, 1, 2).reshape(n_batch, M2_STATE, M2_DINNER)
    row = lambda a: pl.BlockSpec(a.shape, lambda b, c: (0, 0))
    cw = conv_w
    cb = conv_b.reshape(1, -1)
    dtb, aneg = pad(dt_bias), pad(-jnp.exp(a_log))
    dsk = jnp.repeat(d_skip, M2_HEADDIM).reshape(1, -1)
    nrm = m2_norm.reshape(1, -1)
    st = pl.BlockSpec((1, M2_STATE, M2_DINNER), lambda b, c: (b, 0, 0))
    out, s_fin = pl.pallas_call(
        _ssd_kernel,
        grid=(n_batch, nc),
        in_specs=[pl.BlockSpec((chunk, M2_DINNER), lambda b, c: (b * nc + c, z_blk)),
                  pl.BlockSpec((chunk, xw), lambda b, c: (b * nc + c, xbc_blk)),
                  pl.BlockSpec((chunk, xw), lambda b, c: (b * nc + c, xbc_blk + 1)),
                  pl.BlockSpec((chunk, xw), lambda b, c: (b * nc + c, xbc_blk + 2)),
                  pl.BlockSpec((chunk, LANES), lambda b, c: (b * nc + c, dt_blk)),
                  row(cw), row(cb), row(dtb), row(aneg), row(dsk), row(nrm),
                  pl.BlockSpec((1, 8, M2_CONV_DIM), lambda b, c: (b, 0, 0)), st],
        out_specs=[pl.BlockSpec((chunk, M2_DINNER), lambda b, c: (b * nc + c, 0)), st],
        out_shape=[jax.ShapeDtypeStruct((n_batch * t_len, M2_DINNER), BF16),
                   jax.ShapeDtypeStruct((n_batch, M2_STATE, M2_DINNER), F32)],
        scratch_shapes=[pltpu.VMEM((M2_STATE, M2_DINNER), F32), pltpu.VMEM((8, M2_CONV_DIM), F32)],
        compiler_params=pltpu.CompilerParams(
            dimension_semantics=("parallel", "arbitrary"), vmem_limit_bytes=VMEM_LIMIT),
        name="ssd",
    )(proj, proj, proj, proj, proj, cw, cb, dtb, aneg, dsk, nrm, buf8, st_t)
    s_fin = s_fin.reshape(n_batch, M2_STATE, M2_HEADS, M2_HEADDIM).transpose(0, 2, 3, 1)
    return out, s_fin


def _rotary(x, pos, rot_dims, theta):
    half = rot_dims // 2
    inv = theta ** (-jnp.arange(half, dtype=F32) / half)
    ang = pos.astype(F32)[:, None] * inv[None, :]
    shape = (pos.shape[0],) + (1,) * (x.ndim - 3) + (half,)
    cos = jnp.cos(ang).reshape(shape)
    sin = jnp.sin(ang).reshape(shape)
    x1 = x[..., :half]
    x2 = x[..., half:rot_dims]
    rot = jnp.concatenate([x1 * cos - x2 * sin, x1 * sin + x2 * cos], axis=-1)
    return jnp.concatenate([rot, x[..., rot_dims:]], axis=-1)


def _masked_softmax(s, mask, axis=-1):
    s = jnp.where(mask, s, -1e30)
    p = jnp.exp(s - jnp.max(s, axis=axis, keepdims=True)) * mask
    return p / jnp.maximum(jnp.sum(p, axis=axis, keepdims=True), 1e-30)


def _causal_conv(u, buf, w, b=None):
    k_size = w.shape[0]
    t_len = u.shape[1]
    ext = jnp.concatenate([buf.astype(u.dtype), u], axis=1)
    y = ext[:, 0:t_len] * w[0]
    for j in range(1, k_size):
        y = y + ext[:, j:j + t_len] * w[j]
    if b is not None:
        y = y + b
    return y, ext[:, t_len:]


def _to_chunks(a, chunk):
    return a.reshape(a.shape[0], a.shape[1] // chunk, chunk, *a.shape[2:]).swapaxes(0, 1)


def _nsa_compress_jnp(rows, pe, w1, w2):
    bn, length, g, d = rows.shape
    n_part = CMP_BLOCK // CMP_STRIDE
    n_sub = length // CMP_STRIDE
    n_cmp = n_sub - n_part + 1
    sub = rows[:, :n_sub * CMP_STRIDE].reshape(bn, n_sub, CMP_STRIDE, g, d)
    w1p = w1.reshape(n_part, CMP_STRIDE, d, d)
    pre = jnp.einsum('jsd,jsde->e', pe.reshape(n_part, CMP_STRIDE, d), w1p)
    for j in range(n_part):
        pre = pre + jnp.einsum('bnsgd,sde->bnge', sub[:, j:j + n_cmp], w1p[j])
    return jnp.einsum('bnge,ef->bngf', jax.nn.gelu(pre), w2)


def _nsa_sparse_jnp(q, q_pos, rows, cmp_pe, cmp_w1, cmp_w2):
    bn, t_len, g, r, d = q.shape
    length = rows.shape[1]
    kc = _nsa_compress_jnp(rows[:, :, 0], cmp_pe[0], cmp_w1[0], cmp_w2[0])
    vc = _nsa_compress_jnp(rows[:, :, 1], cmp_pe[1], cmp_w1[1], cmp_w2[1])
    n_cmp = kc.shape[1]
    c_start = jnp.arange(n_cmp) * CMP_STRIDE
    c_mask = (c_start + CMP_BLOCK - 1)[None, :] <= q_pos[:, None]
    p_c = _masked_softmax(jnp.einsum('btgrd,bngd->btgrn', q, kc), c_mask[None, :, None, None, :])
    o_cmp = jnp.einsum('btgrn,bngd->btgrd', p_c, vc)
    n_slc = -(-length // SLC_BLOCK)
    s_start = jnp.arange(n_slc) * SLC_BLOCK
    overlap = ((c_start[:, None] < s_start[None, :] + SLC_BLOCK)
               & (c_start[:, None] + CMP_BLOCK > s_start[None, :])).astype(F32)
    imp = jnp.einsum('btgrn,nm->btgm', p_c, overlap)
    blk = jnp.arange(n_slc)[None, :]
    cur = (q_pos // SLC_BLOCK)[:, None]
    valid = (s_start[None, :] <= q_pos[:, None])[None, :, None, :]
    forced = ((blk == 0) | (blk == cur) | (blk == cur - 1))[None, :, None, :]
    score = jnp.where(valid, jnp.where(forced, 1e6, imp), -1e6)
    n_sel = min(SLC_TOPK, n_slc)
    _, idx = lax.top_k(score, n_sel)
    picked = jnp.any(idx[..., None] == jnp.arange(n_slc), axis=-2)
    kmask = jnp.repeat(picked, SLC_BLOCK, axis=-1)[..., :length]
    kmask = kmask & (jnp.arange(length)[None, :] <= q_pos[:, None])[None, :, None, :]
    s = jnp.einsum('btgrd,blgd->btgrl', q, rows[:, :, 2])
    p = _masked_softmax(s, kmask[:, :, :, None, :])
    o_slc = jnp.einsum('btgrl,blgd->btgrd', p, rows[:, :, 3])
    return o_cmp, o_slc


def _window_attention_jnp(q, q_pos, k, v, k_pos):
    s = jnp.einsum('ntgrd,nsgd->ntgrs', q, k)
    dpos = q_pos[:, :, None] - k_pos[:, None, :]
    mask = (dpos >= 0) & (dpos < WINDOW) & (k_pos[:, None, :] >= 0)
    p = _masked_softmax(s, mask[:, :, None, None, :])
    return jnp.einsum('ntgrs,nsgd->ntgrd', p, v)


def _nsa_sample_jnp(q, new_rows, win_rows, gates, pos, past_rows, win_buf, cmp_pe, cmp_w1, cmp_w2):
    bn, t_len, g, r, d = q.shape
    rows = jnp.concatenate([past_rows, new_rows], axis=1)
    o_cmp, o_slc = _nsa_sparse_jnp(q, pos, rows, cmp_pe, cmp_w1, cmp_w2)
    w_len = win_buf.shape[1]
    ext = jnp.concatenate([win_buf, win_rows], axis=1)
    k_pos = (pos[0] - w_len + jnp.arange(w_len + t_len))[None]
    o_win = _window_attention_jnp(q, pos[None], ext[:, :, 0], ext[:, :, 1], k_pos)
    gate = jax.nn.sigmoid(gates).reshape(bn, t_len, 3, g, r, 1)
    o_nsa = gate[:, :, 0] * o_cmp + gate[:, :, 1] * o_slc + gate[:, :, 2] * o_win
    return o_nsa.reshape(bn * t_len, E_Q), ext[:, t_len:]


def _retention_jnp(q, k, v, state, chunk):
    bn, t_len, h, _ = q.shape
    dv = v.shape[-1]
    log_g = jnp.log1p(-(2.0 ** (-5.0 - jnp.arange(h, dtype=F32))))
    i = jnp.arange(chunk, dtype=F32)
    diff = i[:, None] - i[None, :]
    intra = jnp.where(diff >= 0, jnp.exp(jnp.maximum(diff, 0.0)[None] * log_g[:, None, None]), 0.0)
    q_dec = jnp.exp((i[:, None] + 1.0) * log_g[None, :])
    k_dec = jnp.exp((chunk - 1.0 - i)[:, None] * log_g[None, :])
    c_dec = jnp.exp(chunk * log_g)

    def step(s_mat, inp):
        qc, kc, vc = inp
        att = jnp.einsum('bihd,bjhd->bhij', qc, kc) * intra
        o = (jnp.einsum('bhij,bjhv->bihv', att, vc)
             + jnp.einsum('bihd,bhdv->bihv', qc, s_mat) * q_dec[None, :, :, None])
        s_mat = s_mat * c_dec[None, :, None, None] + jnp.einsum('bjhd,bjhv->bhdv', kc * k_dec[None, :, :, None], vc)
        return s_mat, o

    s_fin, o = lax.scan(step, state, (_to_chunks(q, chunk), _to_chunks(k, chunk), _to_chunks(v, chunk)))
    return o.swapaxes(0, 1).reshape(bn, t_len, h, dv), s_fin


def _ssd_jnp(x, dt, a_neg, b_in, c_in, state, chunk):
    bn, t_len, h, p = x.shape
    rep = h // b_in.shape[2]
    a = dt * a_neg
    b_h = jnp.repeat(b_in, rep, axis=2)
    c_h = jnp.repeat(c_in, rep, axis=2)
    xdt = x * dt[..., None]
    tri = jnp.arange(chunk)[:, None] >= jnp.arange(chunk)[None, :]

    def step(s_mat, inp):
        a_c, x_c, b_c, c_c = inp
        cum = jnp.cumsum(a_c, axis=1)
        seg = cum[:, :, None, :] - cum[:, None, :, :]
        l_mat = jnp.exp(jnp.where(tri[None, :, :, None], seg, -1e30))
        cb = jnp.einsum('bihn,bjhn->bijh', c_c, b_c) * l_mat
        y = (jnp.einsum('bijh,bjhp->bihp', cb, x_c)
             + jnp.einsum('bihn,bhpn->bihp', c_c, s_mat) * jnp.exp(cum)[..., None])
        dec_end = jnp.exp(cum[:, -1:, :] - cum)
        s_mat = (s_mat * jnp.exp(cum[:, -1, :])[:, :, None, None]
                 + jnp.einsum('bjhn,bjhp->bhpn', b_c * dec_end[..., None], x_c))
        return s_mat, y

    s_fin, y = lax.scan(step, state, (_to_chunks(a, chunk), _to_chunks(xdt, chunk), _to_chunks(b_h, chunk),
                                      _to_chunks(c_h, chunk)))
    return y.swapaxes(0, 1).reshape(bn, t_len, h, p), s_fin


def _odd_mixer_jnp(proj, pos, ret_state, ssm_state, conv_buf, gn_gain, conv_w, conv_b, dt_bias, a_log, d_skip,
                   m2_norm):
    bn, t_len, _ = proj.shape
    chunk = CHUNK if t_len % CHUNK == 0 else t_len
    q, k, v, g, z, xbc, dt = jnp.split(proj, O_SPLITS, axis=-1)
    q = _rotary(q.reshape(bn, t_len, RET_HEADS, RET_DQK), pos, RET_DQK, RET_THETA) * (RET_DQK ** -0.5)
    k = _rotary(k.reshape(bn, t_len, RET_HEADS, RET_DQK), pos, RET_DQK, RET_THETA)
    v = v.reshape(bn, t_len, RET_HEADS, RET_DV)
    o_ret, ret_new = _retention_jnp(q, k, v, ret_state, chunk)
    mu = jnp.mean(o_ret, axis=-1, keepdims=True)
    var = jnp.mean(jnp.square(o_ret - mu), axis=-1, keepdims=True)
    o_ret = ((o_ret - mu) * lax.rsqrt(var + EPS)).reshape(bn, t_len, R_V) * gn_gain
    o_ret = jax.nn.silu(g) * o_ret
    xbc_c, _ = _causal_conv(xbc, conv_buf, conv_w, conv_b)
    xbc_c = jax.nn.silu(xbc_c)
    xs, b_in, c_in = jnp.split(xbc_c, [M2_DINNER, M2_DINNER + M2_GROUPS * M2_STATE], axis=-1)
    xs = xs.reshape(bn, t_len, M2_HEADS, M2_HEADDIM)
    b_in = b_in.reshape(bn, t_len, M2_GROUPS, M2_STATE)
    c_in = c_in.reshape(bn, t_len, M2_GROUPS, M2_STATE)
    dt = jax.nn.softplus(dt + dt_bias)
    a_neg = -jnp.exp(a_log)
    y, ssm_new = _ssd_jnp(xs, dt, a_neg, b_in, c_in, ssm_state, chunk)
    y = y + d_skip[:, None] * xs
    y = y.reshape(bn, t_len, M2_DINNER) * jax.nn.silu(z)
    yg = y.reshape(bn, t_len, M2_GROUPS, M2_DINNER // M2_GROUPS)
    yg = yg * lax.rsqrt(jnp.mean(yg * yg, axis=-1, keepdims=True) + EPS)
    y = yg.reshape(bn, t_len, M2_DINNER) * m2_norm
    return o_ret.reshape(bn * t_len, R_V), y.reshape(bn * t_len, M2_DINNER), ret_new, ssm_new


def _short_conv_jnp(sc, n_batch, t_len, conv_buf, sc_w):
    b_gate, c_gate, h_in = jnp.split(sc.reshape(n_batch, t_len, E_SC), 3, axis=-1)
    conv_out, new_conv = _causal_conv(c_gate * h_in, conv_buf, sc_w)
    return (b_gate * conv_out).reshape(n_batch * t_len, SC_WIDTH), new_conv


def kernel(x_prompt, x_sample, cache_nsa_kv, cache_nsa_win, state_sc_conv, state_ret, state_ssm, state_m2_conv,
           page_table, norm_w, final_norm, e_w_in, e_w_out, e_cmp_pe, e_cmp_w1, e_cmp_w2, e_sc_conv, o_w_in,
           o_w_out, o_ret_gn, o_m2_conv_w, o_m2_conv_b, o_m2_dt_bias, o_m2_a_log, o_m2_d, o_m2_norm, peer_wq,
           peer_keys, peer_u, peer_v):
    bp, tp, dm = x_prompt.shape
    bs, ts, _ = x_sample.shape
    n_p, n_s = bp * tp, bs * ts
    g, r, d = NSA_KV_HEADS, NSA_GROUP, NSA_HEAD_DIM
    past_len = page_table.shape[1] * PAGE_SIZE
    pos_p = jnp.arange(tp, dtype=jnp.int32)
    pos_s = past_len + jnp.arange(ts, dtype=jnp.int32)
    pos_rows = jnp.concatenate([jnp.tile(pos_p, bp), jnp.tile(pos_s, bs)])
    x = jnp.concatenate([x_prompt.reshape(n_p, dm), x_sample.reshape(n_s, dm)], axis=0)

    w_in = jnp.pad(e_w_in[0], ((0, 0), (0, _pad_cols(E_IN) - E_IN))).astype(BF16)
    proj = norm_matmul(x, norm_w[0, 0], w_in)
    q_bf, rows, win, rows_bf, win_bf = nsa_rope(proj, pos_rows)
    kvc = nsa_compress(rows, bp, tp, e_cmp_pe[0], e_cmp_w1[0], e_cmp_w2[0])
    o_nsa_p = nsa_prompt(q_bf, kvc, rows_bf, win_bf, proj, bp, tp)
    p_kv = rows[:n_p].reshape(bp, tp, 4, g, d)
    s_kv = rows[n_p:].reshape(bs, ts, 4, g, d)
    p_win = win[:n_p].reshape(bp, tp, 2, g, d)[:, tp - min(WINDOW, tp):]
    win_s = win[n_p:].reshape(bs, ts, 2, g, d)
    pages = cache_nsa_kv[0][page_table]
    past_rows = pages.reshape(pages.shape[0], -1, *pages.shape[3:])
    gates_s = proj[n_p:, E_Q + E_KV:E_Q + E_KV + E_G].reshape(bs, ts, E_G)
    o_nsa_s, s_win = _nsa_sample_jnp(q_bf[n_p:].astype(F32).reshape(bs, ts, g, r, d), s_kv, win_s, gates_s, pos_s,
                                     past_rows, cache_nsa_win[0], e_cmp_pe[0], e_cmp_w1[0], e_cmp_w2[0])
    o_nsa = jnp.concatenate([o_nsa_p, o_nsa_s.astype(BF16)], axis=0)
    sc = proj[:, E_Q + E_KV + E_G:E_IN]
    o_sc_p, p_sc = _short_conv_jnp(sc[:n_p], bp, tp, jnp.zeros((bp, SC_KSIZE - 1, SC_WIDTH), F32), e_sc_conv[0])
    o_sc_s, s_sc = _short_conv_jnp(sc[n_p:], bs, ts, state_sc_conv[0], e_sc_conv[0])
    o_sc = jnp.concatenate([o_sc_p, o_sc_s], axis=0).astype(BF16)
    x = matmul2_res(o_nsa, o_sc, e_w_out[0].astype(BF16), x)
    x = peer_layer(x, norm_w[0, 1], peer_wq[0], peer_keys[0], peer_u[0], peer_v[0])

    w_in = jnp.pad(o_w_in[0], ((0, 0), (0, _pad_cols(O_IN) - O_IN))).astype(BF16)
    proj = norm_matmul(x, norm_w[1, 0], w_in)
    odd_w = (o_m2_conv_w[0], o_m2_conv_b[0], o_m2_dt_bias[0], o_m2_a_log[0], o_m2_d[0], o_m2_norm[0])
    o_ret_p, p_ret = retention(proj, pos_p, bp, tp, CHUNK, jnp.zeros((bp, RET_HEADS, RET_DQK, RET_DV), F32),
                               o_ret_gn[0])
    o_ssd_p, p_ssm = ssd(proj, bp, tp, CHUNK, jnp.zeros((bp, M2_CONV - 1, M2_CONV_DIM), F32),
                         jnp.zeros((bp, M2_HEADS, M2_HEADDIM, M2_STATE), F32), *odd_w)
    xbc_cols = slice(O_SPLITS[4], O_SPLITS[5])
    p_m2c = proj[:n_p, xbc_cols].reshape(bp, tp, M2_CONV_DIM)[:, tp - (M2_CONV - 1):]
    xbc_s = proj[n_p:, xbc_cols].reshape(bs, ts, M2_CONV_DIM)
    s_m2c = jnp.concatenate([state_m2_conv[0], xbc_s], axis=1)[:, ts:]
    o_ret_s, o_ssd_s, s_ret, s_ssm = _odd_mixer_jnp(proj[n_p:, :O_IN].reshape(bs, ts, O_IN), pos_s, state_ret[0],
                                                    state_ssm[0], state_m2_conv[0], o_ret_gn[0], *odd_w)
    o_ret = jnp.concatenate([o_ret_p, o_ret_s.astype(BF16)], axis=0)
    o_ssd = jnp.concatenate([o_ssd_p, o_ssd_s.astype(BF16)], axis=0)
    x = matmul2_res(o_ret, o_ssd, o_w_out[0].astype(BF16), x)
    x = peer_layer(x, norm_w[1, 1], peer_wq[1], peer_keys[1], peer_u[1], peer_v[1])

    y = rmsnorm(x, final_norm)
    y_prompt = y[:n_p].reshape(bp, tp, dm)
    y_sample = y[n_p:].reshape(bs, ts, dm)
    return (y_prompt, y_sample, p_kv[None], p_win[None], p_sc[None], p_ret[None], p_ssm[None], p_m2c[None],
            s_kv[None], s_win[None], s_sc[None], s_ret[None], s_ssm[None], s_m2c[None])
```

```python
import math

import jax
import jax.numpy as jnp
from jax import lax
from jax.experimental import pallas as pl
from jax.experimental.pallas import tpu as pltpu

F32 = jnp.float32
BF16 = jnp.bfloat16

D_MODEL = 2048
DEPTH = 2
PAGE_SIZE = 128
NSA_HEAD_DIM = 128
NSA_HEADS = 8
NSA_KV_HEADS = 2
NSA_GROUP = 4
CMP_BLOCK = 32
CMP_STRIDE = 16
SLC_BLOCK = 64
SLC_TOPK = 16
WINDOW = 512
ROPE_THETA = 500000.0
ROPE_DIMS = 32
SC_WIDTH = 1024
SC_KSIZE = 3
RET_HEADS = 4
RET_DQK = 256
RET_DV = 256
RET_THETA = 10000.0
M2_DINNER = 1024
M2_HEADDIM = 64
M2_HEADS = 16
M2_STATE = 128
M2_GROUPS = 2
M2_CONV = 4
M2_CONV_DIM = M2_DINNER + 2 * M2_GROUPS * M2_STATE
PEER_HEADS = 8
PEER_KEYS = 128
PEER_QDIM = 256
PEER_TOPK = 16
Q_BLOCK = 128
CHUNK = 128
EPS = 1e-6

E_Q = NSA_HEADS * NSA_HEAD_DIM
E_KV = 6 * NSA_KV_HEADS * NSA_HEAD_DIM
E_G = 3 * NSA_HEADS
E_SC = 3 * SC_WIDTH
E_IN = E_Q + E_KV + E_G + E_SC
R_QK = RET_HEADS * RET_DQK
R_V = RET_HEADS * RET_DV
O_SPLITS = [R_QK, 2 * R_QK, 2 * R_QK + R_V, 2 * R_QK + 2 * R_V,
            2 * R_QK + 2 * R_V + M2_DINNER, 2 * R_QK + 2 * R_V + M2_DINNER + M2_CONV_DIM]
O_IN = O_SPLITS[-1] + M2_HEADS
SC_COL0 = E_Q + E_KV
GATE_COL0 = SC_COL0 + E_SC

LANES = 128
VMEM_LIMIT = 56 * 1024 * 1024
NEG_INF = float("-inf")


def _round_up(n, m):
    return -(-n // m) * m


def _pick_tile(n, cap):
    best = LANES
    for t in range(LANES, cap + 1, LANES):
        if n % t == 0:
            best = t
    return best


def _pad_cols(m):
    return min((_round_up(m, t) for t in (768, 640, 512)))


def _row_tile(n, cap):
    best = None
    for t in range(16, cap + 1, 16):
        if n % t == 0:
            best = t
    assert best is not None
    return best


def _gelu_tanh(x):
    return 0.5 * x * (1.0 + jnp.tanh(math.sqrt(2.0 / math.pi) * (x + 0.044715 * (x * x * x))))


def _norm_matmul_kernel(x_ref, g_ref, w_ref, o_ref, xn_ref):
    @pl.when(pl.program_id(1) == 0)
    def _():
        x = x_ref[...]
        ms = jnp.mean(x * x, axis=-1, keepdims=True)
        xn_ref[...] = (x * lax.rsqrt(ms + EPS) * g_ref[...]).astype(BF16)

    o_ref[...] = jnp.dot(xn_ref[...], w_ref[...], preferred_element_type=F32).reshape(o_ref.shape)


def norm_matmul(x, gain, w_bf):
    n, k = x.shape
    m = w_bf.shape[1]
    tm = _row_tile(n, 1040)
    tn = _pick_tile(m, 768)
    return pl.pallas_call(
        _norm_matmul_kernel,
        grid=(n // tm, m // tn),
        in_specs=[pl.BlockSpec((tm, k), lambda i, j: (i, 0)),
                  pl.BlockSpec((1, k), lambda i, j: (0, 0)),
                  pl.BlockSpec((k, tn), lambda i, j: (0, j))],
        out_specs=pl.BlockSpec((tm, tn), lambda i, j: (i, j)),
        out_shape=jax.ShapeDtypeStruct((n, m), F32),
        scratch_shapes=[pltpu.VMEM((tm, k), BF16)],
        compiler_params=pltpu.CompilerParams(
            dimension_semantics=("parallel", "arbitrary"), vmem_limit_bytes=VMEM_LIMIT),
        name="norm_matmul",
    )(x, gain.reshape(1, k), w_bf)


def _matmul2_res_kernel(a1_ref, a2_ref, w1_ref, w2_ref, r_ref, o_ref):
    o_ref[...] = (r_ref[...] + jnp.dot(a1_ref[...], w1_ref[...], preferred_element_type=F32)
                  + jnp.dot(a2_ref[...], w2_ref[...], preferred_element_type=F32))


def matmul2_res(a1, a2, w_bf, res):
    n, k1 = a1.shape
    k2 = a2.shape[1]
    m = w_bf.shape[1]
    assert k1 == k2
    tm = _row_tile(n, 1040)
    tn = _pick_tile(m, 1024)
    return pl.pallas_call(
        _matmul2_res_kernel,
        grid=(n // tm, m // tn),
        in_specs=[pl.BlockSpec((tm, k1), lambda i, j: (i, 0)),
                  pl.BlockSpec((tm, k2), lambda i, j: (i, 0)),
                  pl.BlockSpec((k1, tn), lambda i, j: (0, j)),
                  pl.BlockSpec((k2, tn), lambda i, j: (1, j)),
                  pl.BlockSpec((tm, tn), lambda i, j: (i, j))],
        out_specs=pl.BlockSpec((tm, tn), lambda i, j: (i, j)),
        out_shape=jax.ShapeDtypeStruct((n, m), F32),
        compiler_params=pltpu.CompilerParams(
            dimension_semantics=("parallel", "parallel"), vmem_limit_bytes=VMEM_LIMIT),
        name="matmul2_res",
    )(a1, a2, w_bf, w_bf, res)


PEER_TOK = 128
PEER_HEAD_UNROLL = 4
PEER_TM = 640
PEER_TA = 8
PEER_TE = PEER_TA * PEER_KEYS


def _top_desc(work, count):
    rows = []
    for _ in range(count):
        m = jnp.max(work, axis=0, keepdims=True)
        rows.append(m)
        work = jnp.where(work == m, NEG_INF, work)
    return rows


def _peer_router_kernel(q_ref, k_ref, cnt_ref, gw_ref, r2_ref, w2_ref):
    def head(h, carry):
        q = q_ref[h]
        scores = []
        for side in range(2):
            qs = q[:, side * PEER_KEYS:(side + 1) * PEER_KEYS]
            qs = qs * lax.rsqrt(jnp.mean(qs * qs, axis=-1, keepdims=True) + EPS)
            scores.append(lax.dot_general(k_ref[h, side], qs, (((1,), (1,)), ((), ())),
                                          preferred_element_type=F32))
        s1, s2 = scores
        v1 = _top_desc(s1, PEER_TOPK + 1)
        v2 = _top_desc(s2, PEER_TOPK + 1)
        v2_lo = jnp.concatenate(v2[:8], axis=0)
        v2_hi = jnp.concatenate(v2[8:16], axis=0)
        row = lax.broadcasted_iota(jnp.int32, v2_lo.shape, 0)
        blocks = [v1[0] + v2_lo, v1[0] + v2_hi, v1[1] + v2_lo]
        for a, lim in ((2, 5), (3, 4), (4, 3), (5, 2), (6, 2), (7, 2)):
            blocks.append(jnp.where(row < lim, v1[a] + v2_lo, NEG_INF))
        blocks.append(jnp.concatenate(v1[8:16], axis=0) + v2[0])
        extra = jnp.where(row == 0, v1[0] + v2[16], jnp.where(row == 1, v1[16] + v2[0], NEG_INF))
        blocks.append(extra)
        cand = jnp.concatenate(blocks, axis=0)
        tops = _top_desc(cand, PEER_TOPK + 1)
        z = jnp.zeros_like(tops[0])
        for r in range(PEER_TOPK):
            z = z + jnp.exp(tops[r] - tops[0])
        tau = 0.5 * (tops[PEER_TOPK - 1] + tops[PEER_TOPK])
        count = jnp.zeros_like(s1)
        rank2 = jnp.zeros_like(s2)
        for r in range(PEER_TOPK):
            count = count + jnp.where(s1 + v2[r] >= tau, 1.0, 0.0)
        for r in range(PEER_TOPK + 1):
            rank2 = rank2 + jnp.where(v2[r] > s2, 1.0, 0.0)
        cnt_ref[h] = count
        gw_ref[h] = jnp.exp(s1 - v1[0]) / z
        r2_ref[h] = rank2.astype(BF16)
        w2_ref[h] = jnp.exp(s2 - v2[0]).astype(BF16)
        return carry

    lax.fori_loop(0, PEER_HEADS, head, 0, unroll=PEER_HEAD_UNROLL)


def peer_router(q_hm, keys):
    n = q_hm.shape[1]
    out = jax.ShapeDtypeStruct((PEER_HEADS, PEER_KEYS, n), F32)
    out_bf = jax.ShapeDtypeStruct((PEER_HEADS, PEER_KEYS, n), BF16)
    spec = pl.BlockSpec((PEER_HEADS, PEER_KEYS, PEER_TOK), lambda i: (0, 0, i))
    return pl.pallas_call(
        _peer_router_kernel,
        grid=(n // PEER_TOK,),
        in_specs=[pl.BlockSpec((PEER_HEADS, PEER_TOK, PEER_QDIM), lambda i: (0, i, 0)),
                  pl.BlockSpec((PEER_HEADS, 2, PEER_KEYS, PEER_QDIM // 2), lambda i: (0, 0, 0, 0))],
        out_specs=[spec, spec, spec, spec],
        out_shape=[out, out, out_bf, out_bf],
        compiler_params=pltpu.CompilerParams(dimension_semantics=("parallel",), vmem_limit_bytes=VMEM_LIMIT),
        name="peer_router",
    )(q_hm, keys)


def _peer_expert_kernel(xt_ref, u_ref, vt_ref, cnt_ref, gw_ref, r2_ref, w2_ref, o_ref, s_scr, hg_scr, acc_scr):
    e = pl.program_id(1)
    s_scr[...] = jnp.dot(u_ref[...], xt_ref[...], preferred_element_type=F32)
    zero = jnp.zeros((), BF16)
    for al in range(PEER_TA):
        rows = slice(al * PEER_KEYS, (al + 1) * PEER_KEYS)
        g = None
        for h in range(PEER_HEADS):
            cnt = cnt_ref[h, al:al + 1, :].astype(BF16)
            gate = gw_ref[h, al:al + 1, :].astype(BF16)
            t = jnp.where(r2_ref[h] < cnt, w2_ref[h], zero) * gate
            g = t if g is None else g + t
        hg_scr[rows, :] = _gelu_tanh(s_scr[rows, :]).astype(BF16) * g
    part = jnp.dot(vt_ref[...], hg_scr[...], preferred_element_type=F32)

    @pl.when(e == 0)
    def _():
        acc_scr[...] = part

    @pl.when(e != 0)
    def _():
        acc_scr[...] += part

    @pl.when(e == pl.num_programs(1) - 1)
    def _():
        o_ref[...] = acc_scr[...].T


def peer_experts(xt_bf, u_bf, vt_bf, cnt, gw, r2, w2):
    d, n = xt_bf.shape
    n_exp = u_bf.shape[0]
    tm = PEER_TM
    sel_spec = pl.BlockSpec((PEER_HEADS, PEER_TA, tm), lambda i, e: (0, e, i))
    all_spec = pl.BlockSpec((PEER_HEADS, PEER_KEYS, tm), lambda i, e: (0, 0, i))
    return pl.pallas_call(
        _peer_expert_kernel,
        grid=(n // tm, n_exp // PEER_TE),
        in_specs=[pl.BlockSpec((d, tm), lambda i, e: (0, i)),
                  pl.BlockSpec((PEER_TE, d), lambda i, e: (e, 0)),
                  pl.BlockSpec((d, PEER_TE), lambda i, e: (0, e)),
                  sel_spec, sel_spec, all_spec, all_spec],
        out_specs=pl.BlockSpec((tm, d), lambda i, e: (i, 0)),
        out_shape=jax.ShapeDtypeStruct((n, d), F32),
        scratch_shapes=[pltpu.VMEM((PEER_TE, tm), F32), pltpu.VMEM((PEER_TE, tm), BF16),
                        pltpu.VMEM((d, tm), F32)],
        compiler_params=pltpu.CompilerParams(
            dimension_semantics=("parallel", "arbitrary"), vmem_limit_bytes=VMEM_LIMIT),
        name="peer_experts",
    )(xt_bf, u_bf, vt_bf, cnt, gw, r2, w2)


def _norm_matmul_t_kernel(x_ref, g_ref, w_ref, o_ref, xt_ref, xn_ref):
    @pl.when(pl.program_id(1) == 0)
    def _():
        x = x_ref[...]
        ms = jnp.mean(x * x, axis=-1, keepdims=True)
        xn = x * lax.rsqrt(ms + EPS) * g_ref[...]
        xn_ref[...] = xn.astype(BF16)
        xt_ref[...] = xn.T.astype(BF16)

    o_ref[...] = jnp.dot(xn_ref[...], w_ref[...], preferred_element_type=F32).reshape(o_ref.shape)


def norm_matmul_t(x, gain, w_bf, cols):
    n, k = x.shape
    m = w_bf.shape[1]
    tm = PEER_TM
    return pl.pallas_call(
        _norm_matmul_t_kernel,
        grid=(n // tm, m // cols),
        in_specs=[pl.BlockSpec((tm, k), lambda i, j: (i, 0)),
                  pl.BlockSpec((1, k), lambda i, j: (0, 0)),
                  pl.BlockSpec((k, cols), lambda i, j: (0, j))],
        out_specs=[pl.BlockSpec((1, tm, cols), lambda i, j: (j, i, 0)),
                   pl.BlockSpec((k, tm), lambda i, j: (0, i))],
        out_shape=[jax.ShapeDtypeStruct((m // cols, n, cols), F32), jax.ShapeDtypeStruct((k, n), BF16)],
        scratch_shapes=[pltpu.VMEM((tm, k), BF16)],
        compiler_params=pltpu.CompilerParams(
            dimension_semantics=("parallel", "arbitrary"), vmem_limit_bytes=VMEM_LIMIT),
        name="norm_matmul_t",
    )(x, gain.reshape(1, k), w_bf)


def _rmsnorm_kernel(x_ref, g_ref, o_ref):
    x = x_ref[...]
    ms = jnp.mean(x * x, axis=-1, keepdims=True)
    o_ref[...] = (x * lax.rsqrt(ms + EPS) * g_ref[...]).astype(o_ref.dtype)


def rmsnorm(x, gain, dtype=F32):
    n, k = x.shape
    tm = _row_tile(n, 1040)
    return pl.pallas_call(
        _rmsnorm_kernel,
        grid=(n // tm,),
        in_specs=[pl.BlockSpec((tm, k), lambda i: (i, 0)), pl.BlockSpec((1, k), lambda i: (0, 0))],
        out_specs=pl.BlockSpec((tm, k), lambda i: (i, 0)),
        out_shape=jax.ShapeDtypeStruct((n, k), dtype),
        compiler_params=pltpu.CompilerParams(dimension_semantics=("parallel",), vmem_limit_bytes=VMEM_LIMIT),
        name="rmsnorm",
    )(x, gain.reshape(1, k))


def peer_layer(x, gain, w_q, keys, u, v):
    q_hm, xt_bf = norm_matmul_t(x, gain, w_q.astype(BF16), PEER_QDIM)
    cnt, gw, r2, w2 = peer_router(q_hm, keys)
    return x + peer_experts(xt_bf, u.astype(BF16), v.T.astype(BF16), cnt, gw, r2, w2)


ROPE_TM = 416


def _rope_tables(pos):
    half = ROPE_DIMS // 2
    inv = ROPE_THETA ** (-jnp.arange(half, dtype=F32) / half)
    ang = pos.astype(F32)[:, None] * inv[None, :]
    cos, sin = jnp.cos(ang), jnp.sin(ang)
    t = pos.shape[0]
    ones = jnp.ones((t, NSA_HEAD_DIM - ROPE_DIMS), F32)
    zeros = jnp.zeros((t, NSA_HEAD_DIM - ROPE_DIMS), F32)
    zh = jnp.zeros((t, half), F32)
    c = jnp.concatenate([cos, cos, ones], axis=1)
    s_lo = jnp.concatenate([-sin, zh, zeros], axis=1)
    s_hi = jnp.concatenate([zh, sin, zeros], axis=1)
    return c, s_lo, s_hi


def _rope_kernel(p_ref, c_ref, sl_ref, sh_ref, q_ref, rows_ref, win_ref, rows_bf_ref, win_bf_ref):
    c, sl, sh = c_ref[...], sl_ref[...], sh_ref[...]
    half = ROPE_DIMS // 2

    def rot(x):
        return x * c + pltpu.roll(x, LANES - half, 1) * sl + pltpu.roll(x, half, 1) * sh

    d = NSA_HEAD_DIM
    scale = d ** -0.5
    for hd in range(NSA_HEADS):
        q_ref[:, hd * d:(hd + 1) * d] = (rot(p_ref[:, hd * d:(hd + 1) * d]) * scale).astype(BF16)
    for blk in range(12):
        x = p_ref[:, E_Q + blk * d:E_Q + (blk + 1) * d]
        if (blk // 2) % 2 == 0:
            x = rot(x)
        if blk < 8:
            rows_ref[:, blk * d:(blk + 1) * d] = x
            rows_bf_ref[:, blk * d:(blk + 1) * d] = x.astype(BF16)
        else:
            win_ref[:, (blk - 8) * d:(blk - 7) * d] = x
            win_bf_ref[:, (blk - 8) * d:(blk - 7) * d] = x.astype(BF16)


def nsa_rope(proj, pos_rows):
    n = proj.shape[0]
    tm = ROPE_TM
    c, sl, sh = _rope_tables(pos_rows)
    width = E_Q + E_KV
    tab = pl.BlockSpec((tm, LANES), lambda i: (i, 0))
    return pl.pallas_call(
        _rope_kernel,
        grid=(n // tm,),
        in_specs=[pl.BlockSpec((tm, width), lambda i: (i, 0)), tab, tab, tab],
        out_specs=[pl.BlockSpec((tm, E_Q), lambda i: (i, 0)),
                   pl.BlockSpec((tm, 1024), lambda i: (i, 0)),
                   pl.BlockSpec((tm, 512), lambda i: (i, 0)),
                   pl.BlockSpec((tm, 1024), lambda i: (i, 0)),
                   pl.BlockSpec((tm, 512), lambda i: (i, 0))],
        out_shape=[jax.ShapeDtypeStruct((n, E_Q), BF16),
                   jax.ShapeDtypeStruct((n, 1024), F32),
                   jax.ShapeDtypeStruct((n, 512), F32),
                   jax.ShapeDtypeStruct((n, 1024), BF16),
                   jax.ShapeDtypeStruct((n, 512), BF16)],
        compiler_params=pltpu.CompilerParams(dimension_semantics=("parallel",), vmem_limit_bytes=VMEM_LIMIT),
        name="nsa_rope",
    )(proj, c, sl, sh)


def _compress_kernel(x_ref, pe_ref, w1_ref, w2_ref, o_ref):
    n_sub = x_ref.shape[0] // CMP_STRIDE
    acc0 = jnp.zeros((n_sub, NSA_HEAD_DIM), F32)
    acc1 = jnp.zeros((n_sub, NSA_HEAD_DIM), F32)
    for s in range(CMP_STRIDE):
        xs = x_ref[pl.ds(s, n_sub, stride=CMP_STRIDE), :]
        a0 = (xs + pe_ref[0, s:s + 1, :]).astype(BF16)
        a1 = (xs + pe_ref[0, CMP_STRIDE + s:CMP_STRIDE + s + 1, :]).astype(BF16)
        acc0 = acc0 + jnp.dot(a0, w1_ref[0, s], preferred_element_type=F32)
        acc1 = acc1 + jnp.dot(a1, w1_ref[0, CMP_STRIDE + s], preferred_element_type=F32)
    pre = acc0 + pltpu.roll(acc1, n_sub - 1, 0)
    o_ref[0, 0] = jnp.dot(_gelu_tanh(pre).astype(BF16), w2_ref[0], preferred_element_type=F32).astype(BF16)


def nsa_compress(rows, n_batch, t_len, cmp_pe, cmp_w1, cmp_w2):
    n_sub = t_len // CMP_STRIDE
    d = NSA_HEAD_DIM
    w1 = cmp_w1.reshape(2, CMP_BLOCK, d, d).astype(BF16)
    return pl.pallas_call(
        _compress_kernel,
        grid=(n_batch, 4),
        in_specs=[pl.BlockSpec((t_len, d), lambda b, c: (b, c)),
                  pl.BlockSpec((1, CMP_BLOCK, d), lambda b, c: (c // 2, 0, 0)),
                  pl.BlockSpec((1, CMP_BLOCK, d, d), lambda b, c: (c // 2, 0, 0, 0)),
                  pl.BlockSpec((1, d, d), lambda b, c: (c // 2, 0, 0))],
        out_specs=pl.BlockSpec((1, 1, n_sub, d), lambda b, c: (b, c, 0, 0)),
        out_shape=jax.ShapeDtypeStruct((n_batch, 4, n_sub, d), BF16),
        compiler_params=pltpu.CompilerParams(
            dimension_semantics=("parallel", "parallel"), vmem_limit_bytes=VMEM_LIMIT),
        name="nsa_compress",
    )(rows, cmp_pe, w1, cmp_w2.astype(BF16))


NSA_TK = 512
NSA_WTILES = WINDOW // Q_BLOCK + 1


def _masked_softmax_rows(s, mask):
    s = jnp.where(mask, s, -1e30)
    p = jnp.exp(s - jnp.max(s, axis=-1, keepdims=True))
    p = jnp.where(mask, p, 0.0)
    return p / jnp.maximum(jnp.sum(p, axis=-1, keepdims=True), 1e-30)


def _nsa_prompt_kernel(q_ref, kc_ref, vc_ref, ks_ref, vs_ref, kw_ref, vw_ref, gate_ref, o_ref):
    g = pl.program_id(1)
    qi = pl.program_id(2)
    d, r, qb = NSA_HEAD_DIM, NSA_GROUP, Q_BLOCK
    n_cmp = kc_ref.shape[2]
    n_slc = ks_ref.shape[0] // SLC_BLOCK
    nt = (((1,), (1,)), ((), ()))
    q = jnp.concatenate([q_ref[:, h * d:(h + 1) * d] for h in range(r)], axis=0)
    t_col = qi * qb + lax.broadcasted_iota(jnp.int32, (qb, 1), 0)

    s = lax.dot_general(q, kc_ref[0, 0], nt, preferred_element_type=F32).reshape(r, qb, n_cmp)
    n_idx = lax.broadcasted_iota(jnp.int32, (qb, n_cmp), 1)
    c_mask = (n_idx * CMP_STRIDE + (CMP_BLOCK - 1) <= t_col)[None]
    p_c = _masked_softmax_rows(s, c_mask)
    o_cmp = jnp.dot(p_c.reshape(r * qb, n_cmp).astype(BF16), vc_ref[0, 0], preferred_element_type=F32)

    p_sum = jnp.sum(p_c, axis=0)
    ci = lax.broadcasted_iota(jnp.int32, (n_cmp, n_slc), 0) * CMP_STRIDE
    si = lax.broadcasted_iota(jnp.int32, (n_cmp, n_slc), 1) * SLC_BLOCK
    overlap = jnp.where((ci < si + SLC_BLOCK) & (ci + CMP_BLOCK > si), 1.0, 0.0).astype(BF16)
    p_hi = p_sum.astype(BF16)
    p_lo = (p_sum - p_hi.astype(F32)).astype(BF16)
    imp = (jnp.dot(p_hi, overlap, preferred_element_type=F32)
           + jnp.dot(p_lo, overlap, preferred_element_type=F32))
    blk = lax.broadcasted_iota(jnp.int32, (qb, n_slc), 1)
    blk_f = blk.astype(F32)
    cur = t_col // SLC_BLOCK
    forced = (blk == 0) | (blk == cur) | (blk == cur - 1)
    work = jnp.where(blk * SLC_BLOCK <= t_col, jnp.where(forced, 1e6, imp), -1e6)
    sel = jnp.zeros((qb, n_slc), F32)
    for _ in range(min(SLC_TOPK, n_slc)):
        m = jnp.max(work, axis=-1, keepdims=True)
        first = jnp.min(jnp.where(work == m, blk_f, float(n_slc)), axis=-1, keepdims=True)
        pick = blk_f == first
        sel = jnp.where(pick, 1.0, sel)
        work = jnp.where(pick, NEG_INF, work)
    sel_bf = sel.astype(BF16)

    tk = NSA_TK
    bpt = tk // SLC_BLOCK

    def slc_step(kt, carry):
        m_run, l_run, acc = carry
        start = pl.multiple_of(kt * tk, tk)
        k = ks_ref[pl.ds(start, tk), :]
        v = vs_ref[pl.ds(start, tk), :]
        s = lax.dot_general(q, k, nt, preferred_element_type=F32).reshape(r, qb, tk)
        ei = lax.broadcasted_iota(jnp.int32, (n_slc, tk), 0)
        ej = lax.broadcasted_iota(jnp.int32, (n_slc, tk), 1)
        expand = jnp.where(ei == kt * bpt + ej // SLC_BLOCK, 1.0, 0.0).astype(BF16)
        picked = jnp.dot(sel_bf, expand, preferred_element_type=F32)
        kpos = start + lax.broadcasted_iota(jnp.int32, (qb, tk), 1)
        mask = ((picked > 0.5) & (kpos <= t_col))[None]
        s = jnp.where(mask, s, -1e30)
        m_new = jnp.maximum(m_run, jnp.max(s, axis=-1, keepdims=True))
        alpha = jnp.exp(m_run - m_new)
        p = jnp.where(mask, jnp.exp(s - m_new), 0.0)
        l_new = alpha * l_run + jnp.sum(p, axis=-1, keepdims=True)
        pv = jnp.dot(p.reshape(r * qb, tk).astype(BF16), v, preferred_element_type=F32)
        acc = alpha.reshape(r * qb, 1) * acc + pv
        return m_new, l_new, acc

    init = (jnp.full((r, qb, 1), -1e30, F32), jnp.zeros((r, qb, 1), F32), jnp.zeros((r * qb, d), F32))
    _, l_fin, acc = lax.fori_loop(0, (qi * qb) // tk + 1, slc_step, init)
    o_slc = acc / jnp.maximum(l_fin.reshape(r * qb, 1), 1e-30)

    k_tiles, v_tiles, pos_tiles = [], [], []
    for j in range(NSA_WTILES):
        kt = qi - (NSA_WTILES - 1) + j
        ktc = jnp.maximum(kt, 0)
        start = pl.multiple_of(ktc * qb, qb)
        k_tiles.append(kw_ref[pl.ds(start, qb), :])
        v_tiles.append(vw_ref[pl.ds(start, qb), :])
        lane = lax.broadcasted_iota(jnp.int32, (qb, qb), 1)
        pos_tiles.append(jnp.where(kt >= 0, start + lane, -1))
    k_w = jnp.concatenate(k_tiles, axis=0)
    v_w = jnp.concatenate(v_tiles, axis=0)
    k_pos = jnp.concatenate(pos_tiles, axis=1)
    span = NSA_WTILES * qb
    s = lax.dot_general(q, k_w, nt, preferred_element_type=F32).reshape(r, qb, span)
    dpos = t_col - k_pos
    w_mask = ((dpos >= 0) & (dpos < WINDOW) & (k_pos >= 0))[None]
    p_w = _masked_softmax_rows(s, w_mask)
    o_win = jnp.dot(p_w.reshape(r * qb, span).astype(BF16), v_w, preferred_element_type=F32)

    sig = jax.nn.sigmoid(gate_ref[...])
    lane = lax.broadcasted_iota(jnp.int32, sig.shape, 1)
    for h in range(r):
        rows = slice(h * qb, (h + 1) * qb)
        out = jnp.zeros((qb, d), F32)
        for branch, o_b in enumerate((o_cmp, o_slc, o_win)):
            col = branch * NSA_HEADS + g * r + h
            gate = jnp.sum(jnp.where(lane == col, sig, 0.0), axis=-1, keepdims=True)
            out = out + gate * o_b[rows]
        o_ref[:, h * d:(h + 1) * d] = out.astype(o_ref.dtype)


def nsa_prompt(q_bf, kvc, rows_bf, win_bf, proj, n_batch, t_len):
    d, r, qb = NSA_HEAD_DIM, NSA_GROUP, Q_BLOCK
    nqb = t_len // qb
    n_sub = kvc.shape[2]
    gate_blk = GATE_COL0 // LANES
    seq = lambda col: pl.BlockSpec((t_len, d), lambda b, g, i: (b, col(g)))
    return pl.pallas_call(
        _nsa_prompt_kernel,
        grid=(n_batch, NSA_KV_HEADS, nqb),
        in_specs=[pl.BlockSpec((qb, r * d), lambda b, g, i: (b * nqb + i, g)),
                  pl.BlockSpec((1, 1, n_sub, d), lambda b, g, i: (b, g, 0, 0)),
                  pl.BlockSpec((1, 1, n_sub, d), lambda b, g, i: (b, 2 + g, 0, 0)),
                  seq(lambda g: 4 + g), seq(lambda g: 6 + g),
                  seq(lambda g: g), seq(lambda g: 2 + g),
                  pl.BlockSpec((qb, LANES), lambda b, g, i: (b * nqb + i, gate_blk))],
        out_specs=pl.BlockSpec((qb, r * d), lambda b, g, i: (b * nqb + i, g)),
        out_shape=jax.ShapeDtypeStruct((n_batch * t_len, E_Q), BF16),
        compiler_params=pltpu.CompilerParams(
            dimension_semantics=("parallel", "parallel", "arbitrary"), vmem_limit_bytes=VMEM_LIMIT),
        name="nsa_prompt",
    )(q_bf, kvc, kvc, rows_bf, rows_bf, win_bf, win_bf, proj)


SC_ROWS = 512
SC_COLS = 512


def _short_conv_kernel(b_ref, c_ref, h_ref, w_ref, o_ref, tail_ref, prev_scr):
    @pl.when(pl.program_id(2) == 0)
    def _():
        prev_scr[...] = jnp.zeros_like(prev_scr)

    rows = c_ref.shape[0]
    u = c_ref[...] * h_ref[...]
    prev = prev_scr[...]
    row8 = lax.broadcasted_iota(jnp.int32, prev.shape, 0)
    conv = u * w_ref[SC_KSIZE - 1:SC_KSIZE, :]
    for k in range(1, SC_KSIZE):
        rolled = pltpu.roll(u, k, 0)
        top = jnp.where(row8 < k, pltpu.roll(prev, k, 0), rolled[0:8])
        shifted = jnp.concatenate([top, rolled[8:]], axis=0)
        conv = conv + shifted * w_ref[SC_KSIZE - 1 - k:SC_KSIZE - k, :]
    prev_scr[...] = u[rows - 8:rows]
    o_ref[...] = (b_ref[...] * conv).astype(o_ref.dtype)
    tail_ref[0] = u[rows - 8:rows]


def short_conv_prompt(proj, n_batch, t_len, sc_w):
    nr = t_len // SC_ROWS
    nh = SC_WIDTH // SC_COLS
    blk0 = SC_COL0 // SC_COLS
    assert SC_COL0 % SC_COLS == 0
    col = lambda part: pl.BlockSpec((SC_ROWS, SC_COLS), lambda b, j, i: (b * nr + i, blk0 + part * nh + j))
    out, tail = pl.pallas_call(
        _short_conv_kernel,
        grid=(n_batch, nh, nr),
        in_specs=[col(0), col(1), col(2), pl.BlockSpec((SC_KSIZE, SC_COLS), lambda b, j, i: (0, j))],
        out_specs=[pl.BlockSpec((SC_ROWS, SC_COLS), lambda b, j, i: (b * nr + i, j)),
                   pl.BlockSpec((1, 8, SC_COLS), lambda b, j, i: (b, 0, j))],
        out_shape=[jax.ShapeDtypeStruct((n_batch * t_len, SC_WIDTH), BF16),
                   jax.ShapeDtypeStruct((n_batch, 8, SC_WIDTH), F32)],
        scratch_shapes=[pltpu.VMEM((8, SC_COLS), F32)],
        compiler_params=pltpu.CompilerParams(
            dimension_semantics=("parallel", "parallel", "arbitrary"), vmem_limit_bytes=VMEM_LIMIT),
        name="short_conv",
    )(proj, proj, proj, sc_w)
    return out, tail[:, 8 - (SC_KSIZE - 1):]


def _ret_tables(pos, chunk):
    half = RET_DQK // 2
    inv = RET_THETA ** (-jnp.arange(half, dtype=F32) / half)
    ang = pos.astype(F32)[:, None] * inv[None, :]
    log_g = jnp.log1p(-(2.0 ** (-5.0 - jnp.arange(RET_HEADS, dtype=F32))))
    i = jnp.arange(chunk, dtype=F32)
    diff = i[:, None] - i[None, :]
    intra = jnp.where(diff >= 0, jnp.exp(jnp.maximum(diff, 0.0)[None] * log_g[:, None, None]), 0.0)
    q_dec = jnp.exp((i[None, :] + 1.0) * log_g[:, None])[..., None]
    k_dec = jnp.exp((chunk - 1.0 - i)[None, :] * log_g[:, None])[..., None]
    c_dec = jnp.exp(chunk * log_g)[:, None, None]
    return jnp.cos(ang), jnp.sin(ang), intra, q_dec, k_dec, c_dec


def _retention_kernel(q_ref, k_ref, v_ref, g_ref, cos_ref, sin_ref, intra_ref, qd_ref, kd_ref, cd_ref, gn_ref,
                      s0_ref, o_ref, s_out_ref, s_scr):
    c_idx = pl.program_id(1)

    @pl.when(c_idx == 0)
    def _():
        s_scr[...] = s0_ref[0]

    cos, sin = cos_ref[...], sin_ref[...]
    half = RET_DQK // 2
    nt = (((1,), (1,)), ((), ()))
    tn = (((0,), (0,)), ((), ()))

    def rot(x):
        x1, x2 = x[:, :half], x[:, half:]
        return jnp.concatenate([x1 * cos - x2 * sin, x1 * sin + x2 * cos], axis=-1)

    for h in range(RET_HEADS):
        cols = slice(h * RET_DQK, (h + 1) * RET_DQK)
        qr = (rot(q_ref[:, cols]) * (RET_DQK ** -0.5)).astype(BF16)
        kr = rot(k_ref[:, cols])
        v = v_ref[:, cols].astype(BF16)
        att = lax.dot_general(qr, kr.astype(BF16), nt, preferred_element_type=F32) * intra_ref[h]
        s_old = s_scr[h]
        o = (jnp.dot(att.astype(BF16), v, preferred_element_type=F32)
             + jnp.dot(qr, s_old.astype(BF16), preferred_element_type=F32) * qd_ref[h])
        s_scr[h] = s_old * cd_ref[h] + lax.dot_general((kr * kd_ref[h]).astype(BF16), v, tn,
                                                       preferred_element_type=F32)
        mu = jnp.mean(o, axis=-1, keepdims=True)
        dev = o - mu
        var = jnp.mean(dev * dev, axis=-1, keepdims=True)
        gate = g_ref[:, cols]
        on = dev * lax.rsqrt(var + EPS) * gn_ref[:, cols] * (gate * jax.nn.sigmoid(gate))
        o_ref[:, cols] = on.astype(o_ref.dtype)
    s_out_ref[0] = s_scr[...]


def retention(proj, pos, n_batch, t_len, chunk, state0, gn_gain):
    nc = t_len // chunk
    cos, sin, intra, q_dec, k_dec, c_dec = _ret_tables(pos, chunk)
    half = RET_DQK // 2
    col = lambda j: pl.BlockSpec((chunk, R_QK), lambda b, c: (b * nc + c, j))
    tab = pl.BlockSpec((chunk, half), lambda b, c: (c, 0))
    full = lambda a: pl.BlockSpec(a.shape, lambda b, c: (0,) * a.ndim)
    st = pl.BlockSpec((1, RET_HEADS, RET_DQK, RET_DV), lambda b, c: (b, 0, 0, 0))
    return pl.pallas_call(
        _retention_kernel,
        grid=(n_batch, nc),
        in_specs=[col(0), col(1), col(2), col(3), tab, tab, full(intra), full(q_dec), full(k_dec), full(c_dec),
                  pl.BlockSpec((1, R_V), lambda b, c: (0, 0)), st],
        out_specs=[pl.BlockSpec((chunk, R_V), lambda b, c: (b * nc + c, 0)), st],
        out_shape=[jax.ShapeDtypeStruct((n_batch * t_len, R_V), BF16),
                   jax.ShapeDtypeStruct((n_batch, RET_HEADS, RET_DQK, RET_DV), F32)],
        scratch_shapes=[pltpu.VMEM((RET_HEADS, RET_DQK, RET_DV), F32)],
        compiler_params=pltpu.CompilerParams(
            dimension_semantics=("parallel", "arbitrary"), vmem_limit_bytes=VMEM_LIMIT),
        name="retention",
    )(proj, proj, proj, proj, cos, sin, intra, q_dec, k_dec, c_dec, gn_gain.reshape(1, R_V), state0)


SSD_COL0 = 2 * R_QK + 2 * R_V
HEADS_PER_GROUP = M2_HEADS // M2_GROUPS


def _split3(x):
    a = x.astype(BF16)
    r = x - a.astype(F32)
    b = r.astype(BF16)
    c = (r - b.astype(F32)).astype(BF16)
    return a, b, c


def _exact_dot(mat_bf, x):
    out = None
    for piece in _split3(x):
        t = jnp.dot(mat_bf, piece, preferred_element_type=F32)
        out = t if out is None else out + t
    return out


def _exact_dot_r(x, mat_bf):
    out = None
    for piece in _split3(x):
        t = jnp.dot(piece, mat_bf, preferred_element_type=F32)
        out = t if out is None else out + t
    return out


def _ssd_kernel(z_ref, xa_ref, xb_ref, xc_ref, dt_ref, cw_ref, cb_ref, dtb_ref, aneg_ref, dskip_ref, norm_ref,
                buf0_ref, s0_ref, o_ref, s_out_ref, s_scr, prev_scr):
    c_idx = pl.program_id(1)
    chunk = z_ref.shape[0]
    nt = (((1,), (1,)), ((), ()))
    tn = (((0,), (0,)), ((), ()))

    @pl.when(c_idx == 0)
    def _():
        s_scr[...] = s0_ref[0]
        prev_scr[...] = buf0_ref[0]

    x = jnp.concatenate([xa_ref[...], xb_ref[...], xc_ref[...]], axis=1)
    prev = prev_scr[...]
    row8 = lax.broadcasted_iota(jnp.int32, prev.shape, 0)
    conv = x * cw_ref[M2_CONV - 1:M2_CONV, :]
    for k in range(1, M2_CONV):
        rolled = pltpu.roll(x, k, 0)
        top = jnp.where(row8 < k, pltpu.roll(prev, k, 0), rolled[0:8])
        shifted = jnp.concatenate([top, rolled[8:]], axis=0)
        conv = conv + shifted * cw_ref[M2_CONV - 1 - k:M2_CONV - k, :]
    prev_scr[...] = x[chunk - 8:chunk]
    conv = conv + cb_ref[...]
    xbc = conv * jax.nn.sigmoid(conv)
    xs = xbc[:, :M2_DINNER]

    dt_raw = dt_ref[...] + dtb_ref[...]
    dt = jnp.where(dt_raw > 20.0, dt_raw, jnp.log1p(jnp.exp(jnp.minimum(dt_raw, 20.0))))
    a = dt * aneg_ref[...]
    ri = lax.broadcasted_iota(jnp.int32, (chunk, chunk), 0)
    ci = lax.broadcasted_iota(jnp.int32, (chunk, chunk), 1)
    tri = ri >= ci
    cum = _exact_dot(jnp.where(tri, 1.0, 0.0).astype(BF16), a)
    cum_t = cum.T
    cum_last = cum[chunk - 1:chunk, :]
    hi = lax.broadcasted_iota(jnp.int32, (LANES, M2_DINNER), 0)
    li = lax.broadcasted_iota(jnp.int32, (LANES, M2_DINNER), 1)
    expand = jnp.where(hi == li // M2_HEADDIM, 1.0, 0.0).astype(BF16)
    dt_x = _exact_dot_r(dt, expand)
    cum_x = _exact_dot_r(cum, expand)
    last_x = _exact_dot_r(cum_last, expand)
    xdt = xs * dt_x
    x_dec = (xdt * jnp.exp(last_x - cum_x)).astype(BF16)
    xdt_bf = xdt.astype(BF16)
    e_cum_x = jnp.exp(cum_x)
    e_last_x = jnp.exp(last_x)

    y_parts = []
    for gi in range(M2_GROUPS):
        b_g = xbc[:, M2_DINNER + gi * M2_STATE:M2_DINNER + (gi + 1) * M2_STATE].astype(BF16)
        c_g = xbc[:, M2_DINNER + (M2_GROUPS + gi) * M2_STATE:M2_DINNER + (M2_GROUPS + gi + 1) * M2_STATE].astype(BF16)
        cb = lax.dot_general(c_g, b_g, nt, preferred_element_type=F32)
        gcols = slice(gi * HEADS_PER_GROUP * M2_HEADDIM, (gi + 1) * HEADS_PER_GROUP * M2_HEADDIM)
        s_old = s_scr[:, gcols]
        y_state = jnp.dot(c_g, s_old.astype(BF16), preferred_element_type=F32) * e_cum_x[:, gcols]
        s_scr[:, gcols] = s_old * e_last_x[:, gcols] + lax.dot_general(b_g, x_dec[:, gcols], tn,
                                                                        preferred_element_type=F32)
        pair_lane = lax.broadcasted_iota(jnp.int32, (chunk, LANES), 1)
        intra = []
        for pr in range(HEADS_PER_GROUP // 2):
            outs = []
            for sub in range(2):
                h = gi * HEADS_PER_GROUP + pr * 2 + sub
                seg = cum[:, h:h + 1] - cum_t[h:h + 1, :]
                l_mat = jnp.where(tri, jnp.exp(jnp.where(tri, seg, 0.0)), 0.0)
                lanes = slice(gi * HEADS_PER_GROUP * M2_HEADDIM + pr * LANES,
                              gi * HEADS_PER_GROUP * M2_HEADDIM + (pr + 1) * LANES)
                outs.append(jnp.dot((cb * l_mat).astype(BF16), xdt_bf[:, lanes], preferred_element_type=F32))
            intra.append(jnp.where(pair_lane < M2_HEADDIM, outs[0], outs[1]))
        y_parts.append(jnp.concatenate(intra, axis=1) + y_state)
    y = jnp.concatenate(y_parts, axis=1) + dskip_ref[...] * xs
    z = z_ref[...]
    y = y * (z * jax.nn.sigmoid(z))
    gw = M2_DINNER // M2_GROUPS
    outs = []
    for gi in range(M2_GROUPS):
        yg = y[:, gi * gw:(gi + 1) * gw]
        outs.append(yg * lax.rsqrt(jnp.mean(yg * yg, axis=-1, keepdims=True) + EPS))
    o_ref[...] = (jnp.concatenate(outs, axis=1) * norm_ref[...]).astype(o_ref.dtype)
    s_out_ref[0] = s_scr[...]


def ssd(proj, n_batch, t_len, chunk, conv_buf, state0, conv_w, conv_b, dt_bias, a_log, d_skip, m2_norm):
    nc = t_len // chunk
    z_blk = SSD_COL0 // M2_DINNER
    xw = M2_CONV_DIM // 3
    xbc_blk = (SSD_COL0 + M2_DINNER) // xw
    assert (SSD_COL0 + M2_DINNER) % xw == 0 and xw % LANES == 0
    dt_blk = (SSD_COL0 + M2_DINNER + M2_CONV_DIM) // LANES
    pad = lambda v: jnp.pad(v.reshape(1, -1), ((0, 0), (0, LANES - v.shape[-1])))
    buf8 = jnp.pad(conv_buf, ((0, 0), (8 - (M2_CONV - 1), 0), (0, 0)))
    st_t = state0.transpose(0, 3, 1, 2).reshape(n_batch, M2_STATE, M2_DINNER)
    row = lambda a: pl.BlockSpec(a.shape, lambda b, c: (0, 0))
    cw = conv_w
    cb = conv_b.reshape(1, -1)
    dtb, aneg = pad(dt_bias), pad(-jnp.exp(a_log))
    dsk = jnp.repeat(d_skip, M2_HEADDIM).reshape(1, -1)
    nrm = m2_norm.reshape(1, -1)
    st = pl.BlockSpec((1, M2_STATE, M2_DINNER), lambda b, c: (b, 0, 0))
    out, s_fin = pl.pallas_call(
        _ssd_kernel,
        grid=(n_batch, nc),
        in_specs=[pl.BlockSpec((chunk, M2_DINNER), lambda b, c: (b * nc + c, z_blk)),
                  pl.BlockSpec((chunk, xw), lambda b, c: (b * nc + c, xbc_blk)),
                  pl.BlockSpec((chunk, xw), lambda b, c: (b * nc + c, xbc_blk + 1)),
                  pl.BlockSpec((chunk, xw), lambda b, c: (b * nc + c, xbc_blk + 2)),
                  pl.BlockSpec((chunk, LANES), lambda b, c: (b * nc + c, dt_blk)),
                  row(cw), row(cb), row(dtb), row(aneg), row(dsk), row(nrm),
                  pl.BlockSpec((1, 8, M2_CONV_DIM), lambda b, c: (b, 0, 0)), st],
        out_specs=[pl.BlockSpec((chunk, M2_DINNER), lambda b, c: (b * nc + c, 0)), st],
        out_shape=[jax.ShapeDtypeStruct((n_batch * t_len, M2_DINNER), BF16),
                   jax.ShapeDtypeStruct((n_batch, M2_STATE, M2_DINNER), F32)],
        scratch_shapes=[pltpu.VMEM((M2_STATE, M2_DINNER), F32), pltpu.VMEM((8, M2_CONV_DIM), F32)],
        compiler_params=pltpu.CompilerParams(
            dimension_semantics=("parallel", "arbitrary"), vmem_limit_bytes=VMEM_LIMIT),
        name="ssd",
    )(proj, proj, proj, proj, proj, cw, cb, dtb, aneg, dsk, nrm, buf8, st_t)
    s_fin = s_fin.reshape(n_batch, M2_STATE, M2_HEADS, M2_HEADDIM).transpose(0, 2, 3, 1)
    return out, s_fin


def _rotary(x, pos, rot_dims, theta):
    half = rot_dims // 2
    inv = theta ** (-jnp.arange(half, dtype=F32) / half)
    ang = pos.astype(F32)[:, None] * inv[None, :]
    shape = (pos.shape[0],) + (1,) * (x.ndim - 3) + (half,)
    cos = jnp.cos(ang).reshape(shape)
    sin = jnp.sin(ang).reshape(shape)
    x1 = x[..., :half]
    x2 = x[..., half:rot_dims]
    rot = jnp.concatenate([x1 * cos - x2 * sin, x1 * sin + x2 * cos], axis=-1)
    return jnp.concatenate([rot, x[..., rot_dims:]], axis=-1)


def _masked_softmax(s, mask, axis=-1):
    s = jnp.where(mask, s, -1e30)
    p = jnp.exp(s - jnp.max(s, axis=axis, keepdims=True)) * mask
    return p / jnp.maximum(jnp.sum(p, axis=axis, keepdims=True), 1e-30)


def _causal_conv(u, buf, w, b=None):
    k_size = w.shape[0]
    t_len = u.shape[1]
    ext = jnp.concatenate([buf.astype(u.dtype), u], axis=1)
    y = ext[:, 0:t_len] * w[0]
    for j in range(1, k_size):
        y = y + ext[:, j:j + t_len] * w[j]
    if b is not None:
        y = y + b
    return y, ext[:, t_len:]


def _to_chunks(a, chunk):
    return a.reshape(a.shape[0], a.shape[1] // chunk, chunk, *a.shape[2:]).swapaxes(0, 1)


def _nsa_compress_jnp(rows, pe, w1, w2):
    bn, length, g, d = rows.shape
    n_part = CMP_BLOCK // CMP_STRIDE
    n_sub = length // CMP_STRIDE
    n_cmp = n_sub - n_part + 1
    sub = rows[:, :n_sub * CMP_STRIDE].reshape(bn, n_sub, CMP_STRIDE, g, d)
    w1p = w1.reshape(n_part, CMP_STRIDE, d, d)
    pre = jnp.einsum('jsd,jsde->e', pe.reshape(n_part, CMP_STRIDE, d), w1p)
    for j in range(n_part):
        pre = pre + jnp.einsum('bnsgd,sde->bnge', sub[:, j:j + n_cmp], w1p[j])
    return jnp.einsum('bnge,ef->bngf', jax.nn.gelu(pre), w2)


def _nsa_sparse_jnp(q, q_pos, rows, cmp_pe, cmp_w1, cmp_w2):
    bn, t_len, g, r, d = q.shape
    length = rows.shape[1]
    kc = _nsa_compress_jnp(rows[:, :, 0], cmp_pe[0], cmp_w1[0], cmp_w2[0])
    vc = _nsa_compress_jnp(rows[:, :, 1], cmp_pe[1], cmp_w1[1], cmp_w2[1])
    n_cmp = kc.shape[1]
    c_start = jnp.arange(n_cmp) * CMP_STRIDE
    c_mask = (c_start + CMP_BLOCK - 1)[None, :] <= q_pos[:, None]
    p_c = _masked_softmax(jnp.einsum('btgrd,bngd->btgrn', q, kc), c_mask[None, :, None, None, :])
    o_cmp = jnp.einsum('btgrn,bngd->btgrd', p_c, vc)
    n_slc = -(-length // SLC_BLOCK)
    s_start = jnp.arange(n_slc) * SLC_BLOCK
    overlap = ((c_start[:, None] < s_start[None, :] + SLC_BLOCK)
               & (c_start[:, None] + CMP_BLOCK > s_start[None, :])).astype(F32)
    imp = jnp.einsum('btgrn,nm->btgm', p_c, overlap)
    blk = jnp.arange(n_slc)[None, :]
    cur = (q_pos // SLC_BLOCK)[:, None]
    valid = (s_start[None, :] <= q_pos[:, None])[None, :, None, :]
    forced = ((blk == 0) | (blk == cur) | (blk == cur - 1))[None, :, None, :]
    score = jnp.where(valid, jnp.where(forced, 1e6, imp), -1e6)
    n_sel = min(SLC_TOPK, n_slc)
    _, idx = lax.top_k(score, n_sel)
    picked = jnp.any(idx[..., None] == jnp.arange(n_slc), axis=-2)
    kmask = jnp.repeat(picked, SLC_BLOCK, axis=-1)[..., :length]
    kmask = kmask & (jnp.arange(length)[None, :] <= q_pos[:, None])[None, :, None, :]
    s = jnp.einsum('btgrd,blgd->btgrl', q, rows[:, :, 2])
    p = _masked_softmax(s, kmask[:, :, :, None, :])
    o_slc = jnp.einsum('btgrl,blgd->btgrd', p, rows[:, :, 3])
    return o_cmp, o_slc


def _window_attention_jnp(q, q_pos, k, v, k_pos):
    s = jnp.einsum('ntgrd,nsgd->ntgrs', q, k)
    dpos = q_pos[:, :, None] - k_pos[:, None, :]
    mask = (dpos >= 0) & (dpos < WINDOW) & (k_pos[:, None, :] >= 0)
    p = _masked_softmax(s, mask[:, :, None, None, :])
    return jnp.einsum('ntgrs,nsgd->ntgrd', p, v)


def _nsa_sample_jnp(q, new_rows, win_rows, gates, pos, past_rows, win_buf, cmp_pe, cmp_w1, cmp_w2):
    bn, t_len, g, r, d = q.shape
    rows = jnp.concatenate([past_rows, new_rows], axis=1)
    o_cmp, o_slc = _nsa_sparse_jnp(q, pos, rows, cmp_pe, cmp_w1, cmp_w2)
    w_len = win_buf.shape[1]
    ext = jnp.concatenate([win_buf, win_rows], axis=1)
    k_pos = (pos[0] - w_len + jnp.arange(w_len + t_len))[None]
    o_win = _window_attention_jnp(q, pos[None], ext[:, :, 0], ext[:, :, 1], k_pos)
    gate = jax.nn.sigmoid(gates).reshape(bn, t_len, 3, g, r, 1)
    o_nsa = gate[:, :, 0] * o_cmp + gate[:, :, 1] * o_slc + gate[:, :, 2] * o_win
    return o_nsa.reshape(bn * t_len, E_Q), ext[:, t_len:]


def _retention_jnp(q, k, v, state, chunk):
    bn, t_len, h, _ = q.shape
    dv = v.shape[-1]
    log_g = jnp.log1p(-(2.0 ** (-5.0 - jnp.arange(h, dtype=F32))))
    i = jnp.arange(chunk, dtype=F32)
    diff = i[:, None] - i[None, :]
    intra = jnp.where(diff >= 0, jnp.exp(jnp.maximum(diff, 0.0)[None] * log_g[:, None, None]), 0.0)
    q_dec = jnp.exp((i[:, None] + 1.0) * log_g[None, :])
    k_dec = jnp.exp((chunk - 1.0 - i)[:, None] * log_g[None, :])
    c_dec = jnp.exp(chunk * log_g)

    def step(s_mat, inp):
        qc, kc, vc = inp
        att = jnp.einsum('bihd,bjhd->bhij', qc, kc) * intra
        o = (jnp.einsum('bhij,bjhv->bihv', att, vc)
             + jnp.einsum('bihd,bhdv->bihv', qc, s_mat) * q_dec[None, :, :, None])
        s_mat = s_mat * c_dec[None, :, None, None] + jnp.einsum('bjhd,bjhv->bhdv', kc * k_dec[None, :, :, None], vc)
        return s_mat, o

    s_fin, o = lax.scan(step, state, (_to_chunks(q, chunk), _to_chunks(k, chunk), _to_chunks(v, chunk)))
    return o.swapaxes(0, 1).reshape(bn, t_len, h, dv), s_fin


def _ssd_jnp(x, dt, a_neg, b_in, c_in, state, chunk):
    bn, t_len, h, p = x.shape
    rep = h // b_in.shape[2]
    a = dt * a_neg
    b_h = jnp.repeat(b_in, rep, axis=2)
    c_h = jnp.repeat(c_in, rep, axis=2)
    xdt = x * dt[..., None]
    tri = jnp.arange(chunk)[:, None] >= jnp.arange(chunk)[None, :]

    def step(s_mat, inp):
        a_c, x_c, b_c, c_c = inp
        cum = jnp.cumsum(a_c, axis=1)
        seg = cum[:, :, None, :] - cum[:, None, :, :]
        l_mat = jnp.exp(jnp.where(tri[None, :, :, None], seg, -1e30))
        cb = jnp.einsum('bihn,bjhn->bijh', c_c, b_c) * l_mat
        y = (jnp.einsum('bijh,bjhp->bihp', cb, x_c)
             + jnp.einsum('bihn,bhpn->bihp', c_c, s_mat) * jnp.exp(cum)[..., None])
        dec_end = jnp.exp(cum[:, -1:, :] - cum)
        s_mat = (s_mat * jnp.exp(cum[:, -1, :])[:, :, None, None]
                 + jnp.einsum('bjhn,bjhp->bhpn', b_c * dec_end[..., None], x_c))
        return s_mat, y

    s_fin, y = lax.scan(step, state, (_to_chunks(a, chunk), _to_chunks(xdt, chunk), _to_chunks(b_h, chunk),
                                      _to_chunks(c_h, chunk)))
    return y.swapaxes(0, 1).reshape(bn, t_len, h, p), s_fin


def _odd_mixer_jnp(proj, pos, ret_state, ssm_state, conv_buf, gn_gain, conv_w, conv_b, dt_bias, a_log, d_skip,
                   m2_norm):
    bn, t_len, _ = proj.shape
    chunk = CHUNK if t_len % CHUNK == 0 else t_len
    q, k, v, g, z, xbc, dt = jnp.split(proj, O_SPLITS, axis=-1)
    q = _rotary(q.reshape(bn, t_len, RET_HEADS, RET_DQK), pos, RET_DQK, RET_THETA) * (RET_DQK ** -0.5)
    k = _rotary(k.reshape(bn, t_len, RET_HEADS, RET_DQK), pos, RET_DQK, RET_THETA)
    v = v.reshape(bn, t_len, RET_HEADS, RET_DV)
    o_ret, ret_new = _retention_jnp(q, k, v, ret_state, chunk)
    mu = jnp.mean(o_ret, axis=-1, keepdims=True)
    var = jnp.mean(jnp.square(o_ret - mu), axis=-1, keepdims=True)
    o_ret = ((o_ret - mu) * lax.rsqrt(var + EPS)).reshape(bn, t_len, R_V) * gn_gain
    o_ret = jax.nn.silu(g) * o_ret
    xbc_c, _ = _causal_conv(xbc, conv_buf, conv_w, conv_b)
    xbc_c = jax.nn.silu(xbc_c)
    xs, b_in, c_in = jnp.split(xbc_c, [M2_DINNER, M2_DINNER + M2_GROUPS * M2_STATE], axis=-1)
    xs = xs.reshape(bn, t_len, M2_HEADS, M2_HEADDIM)
    b_in = b_in.reshape(bn, t_len, M2_GROUPS, M2_STATE)
    c_in = c_in.reshape(bn, t_len, M2_GROUPS, M2_STATE)
    dt = jax.nn.softplus(dt + dt_bias)
    a_neg = -jnp.exp(a_log)
    y, ssm_new = _ssd_jnp(xs, dt, a_neg, b_in, c_in, ssm_state, chunk)
    y = y + d_skip[:, None] * xs
    y = y.reshape(bn, t_len, M2_DINNER) * jax.nn.silu(z)
    yg = y.reshape(bn, t_len, M2_GROUPS, M2_DINNER // M2_GROUPS)
    yg = yg * lax.rsqrt(jnp.mean(yg * yg, axis=-1, keepdims=True) + EPS)
    y = yg.reshape(bn, t_len, M2_DINNER) * m2_norm
    return o_ret.reshape(bn * t_len, R_V), y.reshape(bn * t_len, M2_DINNER), ret_new, ssm_new


def _short_conv_jnp(sc, n_batch, t_len, conv_buf, sc_w):
    b_gate, c_gate, h_in = jnp.split(sc.reshape(n_batch, t_len, E_SC), 3, axis=-1)
    conv_out, new_conv = _causal_conv(c_gate * h_in, conv_buf, sc_w)
    return (b_gate * conv_out).reshape(n_batch * t_len, SC_WIDTH), new_conv


def kernel(x_prompt, x_sample, cache_nsa_kv, cache_nsa_win, state_sc_conv, state_ret, state_ssm, state_m2_conv,
           page_table, norm_w, final_norm, e_w_in, e_w_out, e_cmp_pe, e_cmp_w1, e_cmp_w2, e_sc_conv, o_w_in,
           o_w_out, o_ret_gn, o_m2_conv_w, o_m2_conv_b, o_m2_dt_bias, o_m2_a_log, o_m2_d, o_m2_norm, peer_wq,
           peer_keys, peer_u, peer_v):
    bp, tp, dm = x_prompt.shape
    bs, ts, _ = x_sample.shape
    n_p, n_s = bp * tp, bs * ts
    g, r, d = NSA_KV_HEADS, NSA_GROUP, NSA_HEAD_DIM
    past_len = page_table.shape[1] * PAGE_SIZE
    pos_p = jnp.arange(tp, dtype=jnp.int32)
    pos_s = past_len + jnp.arange(ts, dtype=jnp.int32)
    pos_rows = jnp.concatenate([jnp.tile(pos_p, bp), jnp.tile(pos_s, bs)])
    x = jnp.concatenate([x_prompt.reshape(n_p, dm), x_sample.reshape(n_s, dm)], axis=0)

    w0 = e_w_in[0]
    w_in = jnp.concatenate([w0[:, :SC_COL0], w0[:, SC_COL0 + E_G:], w0[:, SC_COL0:SC_COL0 + E_G],
                            jnp.zeros((dm, _pad_cols(E_IN) - E_IN), F32)], axis=1).astype(BF16)
    proj = norm_matmul(x, norm_w[0, 0], w_in)
    q_bf, rows, win, rows_bf, win_bf = nsa_rope(proj, pos_rows)
    kvc = nsa_compress(rows, bp, tp, e_cmp_pe[0], e_cmp_w1[0], e_cmp_w2[0])
    o_nsa_p = nsa_prompt(q_bf, kvc, rows_bf, win_bf, proj, bp, tp)
    p_kv = rows[:n_p].reshape(bp, tp, 4, g, d)
    s_kv = rows[n_p:].reshape(bs, ts, 4, g, d)
    p_win = win[:n_p].reshape(bp, tp, 2, g, d)[:, tp - min(WINDOW, tp):]
    win_s = win[n_p:].reshape(bs, ts, 2, g, d)
    pages = cache_nsa_kv[0][page_table]
    past_rows = pages.reshape(pages.shape[0], -1, *pages.shape[3:])
    gates_s = proj[n_p:, GATE_COL0:GATE_COL0 + E_G].reshape(bs, ts, E_G)
    o_nsa_s, s_win = _nsa_sample_jnp(q_bf[n_p:].astype(F32).reshape(bs, ts, g, r, d), s_kv, win_s, gates_s, pos_s,
                                     past_rows, cache_nsa_win[0], e_cmp_pe[0], e_cmp_w1[0], e_cmp_w2[0])
    o_nsa = jnp.concatenate([o_nsa_p, o_nsa_s.astype(BF16)], axis=0)
    o_sc_p, p_sc = short_conv_prompt(proj, bp, tp, e_sc_conv[0])
    o_sc_s, s_sc = _short_conv_jnp(proj[n_p:, SC_COL0:GATE_COL0], bs, ts, state_sc_conv[0], e_sc_conv[0])
    o_sc = jnp.concatenate([o_sc_p, o_sc_s.astype(BF16)], axis=0)
    x = matmul2_res(o_nsa, o_sc, e_w_out[0].astype(BF16), x)
    x = peer_layer(x, norm_w[0, 1], peer_wq[0], peer_keys[0], peer_u[0], peer_v[0])

    w_in = jnp.pad(o_w_in[0], ((0, 0), (0, _pad_cols(O_IN) - O_IN))).astype(BF16)
    proj = norm_matmul(x, norm_w[1, 0], w_in)
    odd_w = (o_m2_conv_w[0], o_m2_conv_b[0], o_m2_dt_bias[0], o_m2_a_log[0], o_m2_d[0], o_m2_norm[0])
    o_ret_p, p_ret = retention(proj, pos_p, bp, tp, CHUNK, jnp.zeros((bp, RET_HEADS, RET_DQK, RET_DV), F32),
                               o_ret_gn[0])
    o_ssd_p, p_ssm = ssd(proj, bp, tp, CHUNK, jnp.zeros((bp, M2_CONV - 1, M2_CONV_DIM), F32),
                         jnp.zeros((bp, M2_HEADS, M2_HEADDIM, M2_STATE), F32), *odd_w)
    xbc_cols = slice(O_SPLITS[4], O_SPLITS[5])
    p_m2c = proj[:n_p, xbc_cols].reshape(bp, tp, M2_CONV_DIM)[:, tp - (M2_CONV - 1):]
    xbc_s = proj[n_p:, xbc_cols].reshape(bs, ts, M2_CONV_DIM)
    s_m2c = jnp.concatenate([state_m2_conv[0], xbc_s], axis=1)[:, ts:]
    o_ret_s, o_ssd_s, s_ret, s_ssm = _odd_mixer_jnp(proj[n_p:, :O_IN].reshape(bs, ts, O_IN), pos_s, state_ret[0],
                                                    state_ssm[0], state_m2_conv[0], o_ret_gn[0], *odd_w)
    o_ret = jnp.concatenate([o_ret_p, o_ret_s.astype(BF16)], axis=0)
    o_ssd = jnp.concatenate([o_ssd_p, o_ssd_s.astype(BF16)], axis=0)
    x = matmul2_res(o_ret, o_ssd, o_w_out[0].astype(BF16), x)
    x = peer_layer(x, norm_w[1, 1], peer_wq[1], peer_keys[1], peer_u[1], peer_v[1])

    y = rmsnorm(x, final_norm)
    y_prompt = y[:n_p].reshape(bp, tp, dm)
    y_sample = y[n_p:].reshape(bs, ts, dm)
    return (y_prompt, y_sample, p_kv[None], p_win[None], p_sc[None], p_ret[None], p_ssm[None], p_m2c[None],
            s_kv[None], s_win[None], s_sc[None], s_ret[None], s_ssm[None], s_m2c[None])
```

```python
import functools
import math

import jax
import jax.numpy as jnp
from jax import lax
from jax.experimental import pallas as pl
from jax.experimental.pallas import tpu as pltpu

F32 = jnp.float32
BF16 = jnp.bfloat16

D_MODEL = 2048
DEPTH = 2
PAGE_SIZE = 128
NSA_HEAD_DIM = 128
NSA_HEADS = 8
NSA_KV_HEADS = 2
NSA_GROUP = 4
CMP_BLOCK = 32
CMP_STRIDE = 16
SLC_BLOCK = 64
SLC_TOPK = 16
WINDOW = 512
ROPE_THETA = 500000.0
ROPE_DIMS = 32
SC_WIDTH = 1024
SC_KSIZE = 3
RET_HEADS = 4
RET_DQK = 256
RET_DV = 256
RET_THETA = 10000.0
M2_DINNER = 1024
M2_HEADDIM = 64
M2_HEADS = 16
M2_STATE = 128
M2_GROUPS = 2
M2_CONV = 4
M2_CONV_DIM = M2_DINNER + 2 * M2_GROUPS * M2_STATE
PEER_HEADS = 8
PEER_KEYS = 128
PEER_QDIM = 256
PEER_TOPK = 16
Q_BLOCK = 128
CHUNK = 128
EPS = 1e-6

E_Q = NSA_HEADS * NSA_HEAD_DIM
E_KV = 6 * NSA_KV_HEADS * NSA_HEAD_DIM
E_G = 3 * NSA_HEADS
E_SC = 3 * SC_WIDTH
E_IN = E_Q + E_KV + E_G + E_SC
R_QK = RET_HEADS * RET_DQK
R_V = RET_HEADS * RET_DV
O_SPLITS = [R_QK, 2 * R_QK, 2 * R_QK + R_V, 2 * R_QK + 2 * R_V,
            2 * R_QK + 2 * R_V + M2_DINNER, 2 * R_QK + 2 * R_V + M2_DINNER + M2_CONV_DIM]
O_IN = O_SPLITS[-1] + M2_HEADS
SC_COL0 = E_Q + E_KV
GATE_COL0 = SC_COL0 + E_SC

LANES = 128
VMEM_LIMIT = 56 * 1024 * 1024
NEG_INF = float("-inf")


def _round_up(n, m):
    return -(-n // m) * m


def _pick_tile(n, cap):
    best = LANES
    for t in range(LANES, cap + 1, LANES):
        if n % t == 0:
            best = t
    return best


def _pad_cols(m):
    return min((_round_up(m, t) for t in (768, 640, 512)))


def _row_tile(n, cap):
    best = None
    for t in range(16, cap + 1, 16):
        if n % t == 0:
            best = t
    assert best is not None
    return best


def _gelu_tanh(x):
    return 0.5 * x * (1.0 + jnp.tanh(math.sqrt(2.0 / math.pi) * (x + 0.044715 * (x * x * x))))


def _norm_matmul_kernel(x_ref, g_ref, w_ref, o_ref, xn_ref):
    @pl.when(pl.program_id(1) == 0)
    def _():
        x = x_ref[...]
        ms = jnp.mean(x * x, axis=-1, keepdims=True)
        xn_ref[...] = (x * lax.rsqrt(ms + EPS) * g_ref[...]).astype(BF16)

    o_ref[...] = jnp.dot(xn_ref[...], w_ref[...], preferred_element_type=F32).reshape(o_ref.shape)


def norm_matmul(x, gain, w_bf):
    n, k = x.shape
    m = w_bf.shape[1]
    tm = _row_tile(n, 1040)
    tn = _pick_tile(m, 768)
    return pl.pallas_call(
        _norm_matmul_kernel,
        grid=(n // tm, m // tn),
        in_specs=[pl.BlockSpec((tm, k), lambda i, j: (i, 0)),
                  pl.BlockSpec((1, k), lambda i, j: (0, 0)),
                  pl.BlockSpec((k, tn), lambda i, j: (0, j))],
        out_specs=pl.BlockSpec((tm, tn), lambda i, j: (i, j)),
        out_shape=jax.ShapeDtypeStruct((n, m), F32),
        scratch_shapes=[pltpu.VMEM((tm, k), BF16)],
        compiler_params=pltpu.CompilerParams(
            dimension_semantics=("parallel", "arbitrary"), vmem_limit_bytes=VMEM_LIMIT),
        name="norm_matmul",
    )(x, gain.reshape(1, k), w_bf)


def _matmul2_res_kernel(a1_ref, a2_ref, w1_ref, w2_ref, r_ref, o_ref):
    o_ref[...] = (r_ref[...] + jnp.dot(a1_ref[...], w1_ref[...], preferred_element_type=F32)
                  + jnp.dot(a2_ref[...], w2_ref[...], preferred_element_type=F32))


def matmul2_res(a1, a2, w_bf, res):
    n, k1 = a1.shape
    k2 = a2.shape[1]
    m = w_bf.shape[1]
    assert k1 == k2
    tm = _row_tile(n, 1040)
    tn = _pick_tile(m, 1024)
    return pl.pallas_call(
        _matmul2_res_kernel,
        grid=(n // tm, m // tn),
        in_specs=[pl.BlockSpec((tm, k1), lambda i, j: (i, 0)),
                  pl.BlockSpec((tm, k2), lambda i, j: (i, 0)),
                  pl.BlockSpec((k1, tn), lambda i, j: (0, j)),
                  pl.BlockSpec((k2, tn), lambda i, j: (1, j)),
                  pl.BlockSpec((tm, tn), lambda i, j: (i, j))],
        out_specs=pl.BlockSpec((tm, tn), lambda i, j: (i, j)),
        out_shape=jax.ShapeDtypeStruct((n, m), F32),
        compiler_params=pltpu.CompilerParams(
            dimension_semantics=("parallel", "parallel"), vmem_limit_bytes=VMEM_LIMIT),
        name="matmul2_res",
    )(a1, a2, w_bf, w_bf, res)


PEER_TOK = 128
PEER_HEAD_UNROLL = 4
PEER_TM = 640
PEER_TA = 8
PEER_TE = PEER_TA * PEER_KEYS


def _top_desc(work, count):
    rows = []
    for _ in range(count):
        m = jnp.max(work, axis=0, keepdims=True)
        rows.append(m)
        work = jnp.where(work == m, NEG_INF, work)
    return rows


def _peer_router_kernel(q_ref, k_ref, cnt_ref, gw_ref, r2_ref, w2_ref):
    def head(h, carry):
        q = q_ref[h]
        scores = []
        for side in range(2):
            qs = q[:, side * PEER_KEYS:(side + 1) * PEER_KEYS]
            qs = qs * lax.rsqrt(jnp.mean(qs * qs, axis=-1, keepdims=True) + EPS)
            scores.append(lax.dot_general(k_ref[h, side], qs, (((1,), (1,)), ((), ())),
                                          preferred_element_type=F32))
        s1, s2 = scores
        v1 = _top_desc(s1, PEER_TOPK + 1)
        v2 = _top_desc(s2, PEER_TOPK + 1)
        v2_lo = jnp.concatenate(v2[:8], axis=0)
        v2_hi = jnp.concatenate(v2[8:16], axis=0)
        row = lax.broadcasted_iota(jnp.int32, v2_lo.shape, 0)
        blocks = [v1[0] + v2_lo, v1[0] + v2_hi, v1[1] + v2_lo]
        for a, lim in ((2, 5), (3, 4), (4, 3), (5, 2), (6, 2), (7, 2)):
            blocks.append(jnp.where(row < lim, v1[a] + v2_lo, NEG_INF))
        blocks.append(jnp.concatenate(v1[8:16], axis=0) + v2[0])
        extra = jnp.where(row == 0, v1[0] + v2[16], jnp.where(row == 1, v1[16] + v2[0], NEG_INF))
        blocks.append(extra)
        cand = jnp.concatenate(blocks, axis=0)
        tops = _top_desc(cand, PEER_TOPK + 1)
        z = jnp.zeros_like(tops[0])
        for r in range(PEER_TOPK):
            z = z + jnp.exp(tops[r] - tops[0])
        tau = 0.5 * (tops[PEER_TOPK - 1] + tops[PEER_TOPK])
        count = jnp.zeros_like(s1)
        rank2 = jnp.zeros_like(s2)
        for r in range(PEER_TOPK):
            count = count + jnp.where(s1 + v2[r] >= tau, 1.0, 0.0)
        for r in range(PEER_TOPK + 1):
            rank2 = rank2 + jnp.where(v2[r] > s2, 1.0, 0.0)
        cnt_ref[h] = count
        gw_ref[h] = jnp.exp(s1 - v1[0]) / z
        r2_ref[h] = rank2.astype(BF16)
        w2_ref[h] = jnp.exp(s2 - v2[0]).astype(BF16)
        return carry

    lax.fori_loop(0, PEER_HEADS, head, 0, unroll=PEER_HEAD_UNROLL)


def peer_router(q_hm, keys):
    n = q_hm.shape[1]
    out = jax.ShapeDtypeStruct((PEER_HEADS, PEER_KEYS, n), F32)
    out_bf = jax.ShapeDtypeStruct((PEER_HEADS, PEER_KEYS, n), BF16)
    spec = pl.BlockSpec((PEER_HEADS, PEER_KEYS, PEER_TOK), lambda i: (0, 0, i))
    return pl.pallas_call(
        _peer_router_kernel,
        grid=(n // PEER_TOK,),
        in_specs=[pl.BlockSpec((PEER_HEADS, PEER_TOK, PEER_QDIM), lambda i: (0, i, 0)),
                  pl.BlockSpec((PEER_HEADS, 2, PEER_KEYS, PEER_QDIM // 2), lambda i: (0, 0, 0, 0))],
        out_specs=[spec, spec, spec, spec],
        out_shape=[out, out, out_bf, out_bf],
        compiler_params=pltpu.CompilerParams(dimension_semantics=("parallel",), vmem_limit_bytes=VMEM_LIMIT),
        name="peer_router",
    )(q_hm, keys)


def _peer_expert_kernel(xt_ref, u_ref, vt_ref, cnt_ref, gw_ref, r2_ref, w2_ref, o_ref, s_scr, hg_scr, acc_scr):
    e = pl.program_id(1)
    s_scr[...] = jnp.dot(u_ref[...], xt_ref[...], preferred_element_type=F32)
    zero = jnp.zeros((), BF16)
    for al in range(PEER_TA):
        rows = slice(al * PEER_KEYS, (al + 1) * PEER_KEYS)
        g = None
        for h in range(PEER_HEADS):
            cnt = cnt_ref[h, al:al + 1, :].astype(BF16)
            gate = gw_ref[h, al:al + 1, :].astype(BF16)
            t = jnp.where(r2_ref[h] < cnt, w2_ref[h], zero) * gate
            g = t if g is None else g + t
        hg_scr[rows, :] = _gelu_tanh(s_scr[rows, :]).astype(BF16) * g
    part = jnp.dot(vt_ref[...], hg_scr[...], preferred_element_type=F32)

    @pl.when(e == 0)
    def _():
        acc_scr[...] = part

    @pl.when(e != 0)
    def _():
        acc_scr[...] += part

    @pl.when(e == pl.num_programs(1) - 1)
    def _():
        o_ref[...] = acc_scr[...].T


def peer_experts(xt_bf, u_bf, vt_bf, cnt, gw, r2, w2):
    d, n = xt_bf.shape
    n_exp = u_bf.shape[0]
    tm = PEER_TM
    sel_spec = pl.BlockSpec((PEER_HEADS, PEER_TA, tm), lambda i, e: (0, e, i))
    all_spec = pl.BlockSpec((PEER_HEADS, PEER_KEYS, tm), lambda i, e: (0, 0, i))
    return pl.pallas_call(
        _peer_expert_kernel,
        grid=(n // tm, n_exp // PEER_TE),
        in_specs=[pl.BlockSpec((d, tm), lambda i, e: (0, i)),
                  pl.BlockSpec((PEER_TE, d), lambda i, e: (e, 0)),
                  pl.BlockSpec((d, PEER_TE), lambda i, e: (0, e)),
                  sel_spec, sel_spec, all_spec, all_spec],
        out_specs=pl.BlockSpec((tm, d), lambda i, e: (i, 0)),
        out_shape=jax.ShapeDtypeStruct((n, d), F32),
        scratch_shapes=[pltpu.VMEM((PEER_TE, tm), F32), pltpu.VMEM((PEER_TE, tm), BF16),
                        pltpu.VMEM((d, tm), F32)],
        compiler_params=pltpu.CompilerParams(
            dimension_semantics=("parallel", "arbitrary"), vmem_limit_bytes=VMEM_LIMIT),
        name="peer_experts",
    )(xt_bf, u_bf, vt_bf, cnt, gw, r2, w2)


def _norm_matmul_t_kernel(x_ref, g_ref, w_ref, o_ref, xt_ref, xn_ref):
    @pl.when(pl.program_id(1) == 0)
    def _():
        x = x_ref[...]
        ms = jnp.mean(x * x, axis=-1, keepdims=True)
        xn = x * lax.rsqrt(ms + EPS) * g_ref[...]
        xn_ref[...] = xn.astype(BF16)
        xt_ref[...] = xn.T.astype(BF16)

    o_ref[...] = jnp.dot(xn_ref[...], w_ref[...], preferred_element_type=F32).reshape(o_ref.shape)


def norm_matmul_t(x, gain, w_bf, cols):
    n, k = x.shape
    m = w_bf.shape[1]
    tm = PEER_TM
    return pl.pallas_call(
        _norm_matmul_t_kernel,
        grid=(n // tm, m // cols),
        in_specs=[pl.BlockSpec((tm, k), lambda i, j: (i, 0)),
                  pl.BlockSpec((1, k), lambda i, j: (0, 0)),
                  pl.BlockSpec((k, cols), lambda i, j: (0, j))],
        out_specs=[pl.BlockSpec((1, tm, cols), lambda i, j: (j, i, 0)),
                   pl.BlockSpec((k, tm), lambda i, j: (0, i))],
        out_shape=[jax.ShapeDtypeStruct((m // cols, n, cols), F32), jax.ShapeDtypeStruct((k, n), BF16)],
        scratch_shapes=[pltpu.VMEM((tm, k), BF16)],
        compiler_params=pltpu.CompilerParams(
            dimension_semantics=("parallel", "arbitrary"), vmem_limit_bytes=VMEM_LIMIT),
        name="norm_matmul_t",
    )(x, gain.reshape(1, k), w_bf)


def _rmsnorm_kernel(x_ref, g_ref, o_ref):
    x = x_ref[...]
    ms = jnp.mean(x * x, axis=-1, keepdims=True)
    o_ref[...] = (x * lax.rsqrt(ms + EPS) * g_ref[...]).astype(o_ref.dtype)


def rmsnorm(x, gain, dtype=F32):
    n, k = x.shape
    tm = _row_tile(n, 1040)
    return pl.pallas_call(
        _rmsnorm_kernel,
        grid=(n // tm,),
        in_specs=[pl.BlockSpec((tm, k), lambda i: (i, 0)), pl.BlockSpec((1, k), lambda i: (0, 0))],
        out_specs=pl.BlockSpec((tm, k), lambda i: (i, 0)),
        out_shape=jax.ShapeDtypeStruct((n, k), dtype),
        compiler_params=pltpu.CompilerParams(dimension_semantics=("parallel",), vmem_limit_bytes=VMEM_LIMIT),
        name="rmsnorm",
    )(x, gain.reshape(1, k))


def peer_layer(x, gain, w_q, keys, u, v):
    q_hm, xt_bf = norm_matmul_t(x, gain, w_q.astype(BF16), PEER_QDIM)
    cnt, gw, r2, w2 = peer_router(q_hm, keys)
    return x + peer_experts(xt_bf, u.astype(BF16), v.T.astype(BF16), cnt, gw, r2, w2)


ROPE_TM = 416


def _rope_tables(pos):
    half = ROPE_DIMS // 2
    inv = ROPE_THETA ** (-jnp.arange(half, dtype=F32) / half)
    ang = pos.astype(F32)[:, None] * inv[None, :]
    cos, sin = jnp.cos(ang), jnp.sin(ang)
    t = pos.shape[0]
    ones = jnp.ones((t, NSA_HEAD_DIM - ROPE_DIMS), F32)
    zeros = jnp.zeros((t, NSA_HEAD_DIM - ROPE_DIMS), F32)
    zh = jnp.zeros((t, half), F32)
    c = jnp.concatenate([cos, cos, ones], axis=1)
    s_lo = jnp.concatenate([-sin, zh, zeros], axis=1)
    s_hi = jnp.concatenate([zh, sin, zeros], axis=1)
    return c, s_lo, s_hi


def _rope_kernel(p_ref, c_ref, sl_ref, sh_ref, q_ref, rows_ref, win_ref, rows_bf_ref, win_bf_ref):
    c, sl, sh = c_ref[...], sl_ref[...], sh_ref[...]
    half = ROPE_DIMS // 2

    def rot(x):
        return x * c + pltpu.roll(x, LANES - half, 1) * sl + pltpu.roll(x, half, 1) * sh

    d = NSA_HEAD_DIM
    scale = d ** -0.5
    for hd in range(NSA_HEADS):
        q_ref[:, hd * d:(hd + 1) * d] = (rot(p_ref[:, hd * d:(hd + 1) * d]) * scale).astype(BF16)
    for blk in range(12):
        x = p_ref[:, E_Q + blk * d:E_Q + (blk + 1) * d]
        if (blk // 2) % 2 == 0:
            x = rot(x)
        if blk < 8:
            rows_ref[:, blk * d:(blk + 1) * d] = x
            rows_bf_ref[:, blk * d:(blk + 1) * d] = x.astype(BF16)
        else:
            win_ref[:, (blk - 8) * d:(blk - 7) * d] = x
            win_bf_ref[:, (blk - 8) * d:(blk - 7) * d] = x.astype(BF16)


def nsa_rope(proj, pos_rows):
    n = proj.shape[0]
    tm = ROPE_TM
    c, sl, sh = _rope_tables(pos_rows)
    width = E_Q + E_KV
    tab = pl.BlockSpec((tm, LANES), lambda i: (i, 0))
    return pl.pallas_call(
        _rope_kernel,
        grid=(n // tm,),
        in_specs=[pl.BlockSpec((tm, width), lambda i: (i, 0)), tab, tab, tab],
        out_specs=[pl.BlockSpec((tm, E_Q), lambda i: (i, 0)),
                   pl.BlockSpec((tm, 1024), lambda i: (i, 0)),
                   pl.BlockSpec((tm, 512), lambda i: (i, 0)),
                   pl.BlockSpec((tm, 1024), lambda i: (i, 0)),
                   pl.BlockSpec((tm, 512), lambda i: (i, 0))],
        out_shape=[jax.ShapeDtypeStruct((n, E_Q), BF16),
                   jax.ShapeDtypeStruct((n, 1024), F32),
                   jax.ShapeDtypeStruct((n, 512), F32),
                   jax.ShapeDtypeStruct((n, 1024), BF16),
                   jax.ShapeDtypeStruct((n, 512), BF16)],
        compiler_params=pltpu.CompilerParams(dimension_semantics=("parallel",), vmem_limit_bytes=VMEM_LIMIT),
        name="nsa_rope",
    )(proj, c, sl, sh)


def _compress_kernel(x_ref, pe_ref, w1_ref, w2_ref, o_ref):
    n_sub = x_ref.shape[0] // CMP_STRIDE
    acc0 = jnp.zeros((n_sub, NSA_HEAD_DIM), F32)
    acc1 = jnp.zeros((n_sub, NSA_HEAD_DIM), F32)
    for s in range(CMP_STRIDE):
        xs = x_ref[pl.ds(s, n_sub, stride=CMP_STRIDE), :]
        a0 = (xs + pe_ref[0, s:s + 1, :]).astype(BF16)
        a1 = (xs + pe_ref[0, CMP_STRIDE + s:CMP_STRIDE + s + 1, :]).astype(BF16)
        acc0 = acc0 + jnp.dot(a0, w1_ref[0, s], preferred_element_type=F32)
        acc1 = acc1 + jnp.dot(a1, w1_ref[0, CMP_STRIDE + s], preferred_element_type=F32)
    pre = acc0 + pltpu.roll(acc1, n_sub - 1, 0)
    o_ref[0, 0] = jnp.dot(_gelu_tanh(pre).astype(BF16), w2_ref[0], preferred_element_type=F32).astype(BF16)


def nsa_compress(rows, n_batch, t_len, cmp_pe, cmp_w1, cmp_w2):
    n_sub = t_len // CMP_STRIDE
    d = NSA_HEAD_DIM
    w1 = cmp_w1.reshape(2, CMP_BLOCK, d, d).astype(BF16)
    return pl.pallas_call(
        _compress_kernel,
        grid=(n_batch, 4),
        in_specs=[pl.BlockSpec((t_len, d), lambda b, c: (b, c)),
                  pl.BlockSpec((1, CMP_BLOCK, d), lambda b, c: (c // 2, 0, 0)),
                  pl.BlockSpec((1, CMP_BLOCK, d, d), lambda b, c: (c // 2, 0, 0, 0)),
                  pl.BlockSpec((1, d, d), lambda b, c: (c // 2, 0, 0))],
        out_specs=pl.BlockSpec((1, 1, n_sub, d), lambda b, c: (b, c, 0, 0)),
        out_shape=jax.ShapeDtypeStruct((n_batch, 4, n_sub, d), BF16),
        compiler_params=pltpu.CompilerParams(
            dimension_semantics=("parallel", "parallel"), vmem_limit_bytes=VMEM_LIMIT),
        name="nsa_compress",
    )(rows, cmp_pe, w1, cmp_w2.astype(BF16))


NSA_TK = 512
NSA_WTILES = WINDOW // Q_BLOCK + 1


def _masked_softmax_rows(s, mask):
    s = jnp.where(mask, s, -1e30)
    p = jnp.exp(s - jnp.max(s, axis=-1, keepdims=True))
    p = jnp.where(mask, p, 0.0)
    return p / jnp.maximum(jnp.sum(p, axis=-1, keepdims=True), 1e-30)


def _nsa_prompt_kernel(q_ref, kc_ref, vc_ref, ks_ref, vs_ref, kw_ref, vw_ref, gate_ref, o_ref):
    g = pl.program_id(1)
    qi = pl.program_id(2)
    d, r, qb = NSA_HEAD_DIM, NSA_GROUP, Q_BLOCK
    n_cmp = kc_ref.shape[2]
    n_slc = ks_ref.shape[0] // SLC_BLOCK
    nt = (((1,), (1,)), ((), ()))
    q = jnp.concatenate([q_ref[:, h * d:(h + 1) * d] for h in range(r)], axis=0)
    t_col = qi * qb + lax.broadcasted_iota(jnp.int32, (qb, 1), 0)

    s = lax.dot_general(q, kc_ref[0, 0], nt, preferred_element_type=F32).reshape(r, qb, n_cmp)
    n_idx = lax.broadcasted_iota(jnp.int32, (qb, n_cmp), 1)
    c_mask = (n_idx * CMP_STRIDE + (CMP_BLOCK - 1) <= t_col)[None]
    p_c = _masked_softmax_rows(s, c_mask)
    o_cmp = jnp.dot(p_c.reshape(r * qb, n_cmp).astype(BF16), vc_ref[0, 0], preferred_element_type=F32)

    p_sum = jnp.sum(p_c, axis=0)
    ci = lax.broadcasted_iota(jnp.int32, (n_cmp, n_slc), 0) * CMP_STRIDE
    si = lax.broadcasted_iota(jnp.int32, (n_cmp, n_slc), 1) * SLC_BLOCK
    overlap = jnp.where((ci < si + SLC_BLOCK) & (ci + CMP_BLOCK > si), 1.0, 0.0).astype(BF16)
    p_hi = p_sum.astype(BF16)
    p_lo = (p_sum - p_hi.astype(F32)).astype(BF16)
    imp = (jnp.dot(p_hi, overlap, preferred_element_type=F32)
           + jnp.dot(p_lo, overlap, preferred_element_type=F32))
    blk = lax.broadcasted_iota(jnp.int32, (qb, n_slc), 1)
    blk_f = blk.astype(F32)
    cur = t_col // SLC_BLOCK
    forced = (blk == 0) | (blk == cur) | (blk == cur - 1)
    work = jnp.where(blk * SLC_BLOCK <= t_col, jnp.where(forced, 1e6, imp), -1e6)
    sel = jnp.zeros((qb, n_slc), F32)
    for _ in range(min(SLC_TOPK, n_slc)):
        m = jnp.max(work, axis=-1, keepdims=True)
        first = jnp.min(jnp.where(work == m, blk_f, float(n_slc)), axis=-1, keepdims=True)
        pick = blk_f == first
        sel = jnp.where(pick, 1.0, sel)
        work = jnp.where(pick, NEG_INF, work)
    sel_bf = sel.astype(BF16)

    tk = NSA_TK
    bpt = tk // SLC_BLOCK

    def slc_step(kt, carry):
        m_run, l_run, acc = carry
        start = pl.multiple_of(kt * tk, tk)
        k = ks_ref[pl.ds(start, tk), :]
        v = vs_ref[pl.ds(start, tk), :]
        s = lax.dot_general(q, k, nt, preferred_element_type=F32).reshape(r, qb, tk)
        ei = lax.broadcasted_iota(jnp.int32, (n_slc, tk), 0)
        ej = lax.broadcasted_iota(jnp.int32, (n_slc, tk), 1)
        expand = jnp.where(ei == kt * bpt + ej // SLC_BLOCK, 1.0, 0.0).astype(BF16)
        picked = jnp.dot(sel_bf, expand, preferred_element_type=F32)
        kpos = start + lax.broadcasted_iota(jnp.int32, (qb, tk), 1)
        mask = ((picked > 0.5) & (kpos <= t_col))[None]
        s = jnp.where(mask, s, -1e30)
        m_new = jnp.maximum(m_run, jnp.max(s, axis=-1, keepdims=True))
        alpha = jnp.exp(m_run - m_new)
        p = jnp.where(mask, jnp.exp(s - m_new), 0.0)
        l_new = alpha * l_run + jnp.sum(p, axis=-1, keepdims=True)
        pv = jnp.dot(p.reshape(r * qb, tk).astype(BF16), v, preferred_element_type=F32)
        acc = alpha.reshape(r * qb, 1) * acc + pv
        return m_new, l_new, acc

    init = (jnp.full((r, qb, 1), -1e30, F32), jnp.zeros((r, qb, 1), F32), jnp.zeros((r * qb, d), F32))
    _, l_fin, acc = lax.fori_loop(0, (qi * qb) // tk + 1, slc_step, init)
    o_slc = acc / jnp.maximum(l_fin.reshape(r * qb, 1), 1e-30)

    k_tiles, v_tiles, pos_tiles = [], [], []
    for j in range(NSA_WTILES):
        kt = qi - (NSA_WTILES - 1) + j
        ktc = jnp.maximum(kt, 0)
        start = pl.multiple_of(ktc * qb, qb)
        k_tiles.append(kw_ref[pl.ds(start, qb), :])
        v_tiles.append(vw_ref[pl.ds(start, qb), :])
        lane = lax.broadcasted_iota(jnp.int32, (qb, qb), 1)
        pos_tiles.append(jnp.where(kt >= 0, start + lane, -1))
    k_w = jnp.concatenate(k_tiles, axis=0)
    v_w = jnp.concatenate(v_tiles, axis=0)
    k_pos = jnp.concatenate(pos_tiles, axis=1)
    span = NSA_WTILES * qb
    s = lax.dot_general(q, k_w, nt, preferred_element_type=F32).reshape(r, qb, span)
    dpos = t_col - k_pos
    w_mask = ((dpos >= 0) & (dpos < WINDOW) & (k_pos >= 0))[None]
    p_w = _masked_softmax_rows(s, w_mask)
    o_win = jnp.dot(p_w.reshape(r * qb, span).astype(BF16), v_w, preferred_element_type=F32)

    sig = jax.nn.sigmoid(gate_ref[...])
    lane = lax.broadcasted_iota(jnp.int32, sig.shape, 1)
    for h in range(r):
        rows = slice(h * qb, (h + 1) * qb)
        out = jnp.zeros((qb, d), F32)
        for branch, o_b in enumerate((o_cmp, o_slc, o_win)):
            col = branch * NSA_HEADS + g * r + h
            gate = jnp.sum(jnp.where(lane == col, sig, 0.0), axis=-1, keepdims=True)
            out = out + gate * o_b[rows]
        o_ref[:, h * d:(h + 1) * d] = out.astype(o_ref.dtype)


def nsa_prompt(q_bf, kvc, rows_bf, win_bf, proj, n_batch, t_len):
    d, r, qb = NSA_HEAD_DIM, NSA_GROUP, Q_BLOCK
    nqb = t_len // qb
    n_sub = kvc.shape[2]
    gate_blk = GATE_COL0 // LANES
    seq = lambda col: pl.BlockSpec((t_len, d), lambda b, g, i: (b, col(g)))
    return pl.pallas_call(
        _nsa_prompt_kernel,
        grid=(n_batch, NSA_KV_HEADS, nqb),
        in_specs=[pl.BlockSpec((qb, r * d), lambda b, g, i: (b * nqb + i, g)),
                  pl.BlockSpec((1, 1, n_sub, d), lambda b, g, i: (b, g, 0, 0)),
                  pl.BlockSpec((1, 1, n_sub, d), lambda b, g, i: (b, 2 + g, 0, 0)),
                  seq(lambda g: 4 + g), seq(lambda g: 6 + g),
                  seq(lambda g: g), seq(lambda g: 2 + g),
                  pl.BlockSpec((qb, LANES), lambda b, g, i: (b * nqb + i, gate_blk))],
        out_specs=pl.BlockSpec((qb, r * d), lambda b, g, i: (b * nqb + i, g)),
        out_shape=jax.ShapeDtypeStruct((n_batch * t_len, E_Q), BF16),
        compiler_params=pltpu.CompilerParams(
            dimension_semantics=("parallel", "parallel", "arbitrary"), vmem_limit_bytes=VMEM_LIMIT),
        name="nsa_prompt",
    )(q_bf, kvc, kvc, rows_bf, rows_bf, win_bf, win_bf, proj)


SAMPLE_PAGES = 16
NEW_ROWS_BLK = 16


def _page_specs(col_block, width):
    def spec(i):
        return pl.BlockSpec((1, PAGE_SIZE, width),
                            lambda s, c, pt: (pt[s, c * SAMPLE_PAGES + i], 0, col_block))
    return [spec(i) for i in range(SAMPLE_PAGES)]


def _sample_compress_kernel(pt_ref, *refs):
    n_in = 4 * SAMPLE_PAGES
    pages = refs[:n_in]
    pe_ref, w1_ref, a0_ref, a1_ref = refs[n_in:]
    d = NSA_HEAD_DIM
    per_page = PAGE_SIZE // CMP_STRIDE
    rows = SAMPLE_PAGES * per_page
    for kv in range(2):
        acc0 = jnp.zeros((NSA_KV_HEADS * rows, d), F32)
        acc1 = jnp.zeros((NSA_KV_HEADS * rows, d), F32)
        for s in range(CMP_STRIDE):
            pieces = []
            for g in range(NSA_KV_HEADS):
                kind = kv * NSA_KV_HEADS + g
                for p in range(SAMPLE_PAGES):
                    pieces.append(pages[kind * SAMPLE_PAGES + p][0, pl.ds(s, per_page, stride=CMP_STRIDE), :])
            xs = jnp.concatenate(pieces, axis=0)
            a0 = (xs + pe_ref[kv, s:s + 1, :]).astype(BF16)
            a1 = (xs + pe_ref[kv, CMP_STRIDE + s:CMP_STRIDE + s + 1, :]).astype(BF16)
            acc0 = acc0 + jnp.dot(a0, w1_ref[kv, s], preferred_element_type=F32)
            acc1 = acc1 + jnp.dot(a1, w1_ref[kv, CMP_STRIDE + s], preferred_element_type=F32)
        for g in range(NSA_KV_HEADS):
            a0_ref[0, kv * NSA_KV_HEADS + g] = acc0[g * rows:(g + 1) * rows]
            a1_ref[0, kv * NSA_KV_HEADS + g] = acc1[g * rows:(g + 1) * rows]


def nsa_sample_compress(cache, page_table, cmp_pe, cmp_w1):
    bs, n_pages = page_table.shape
    d = NSA_HEAD_DIM
    per_page = PAGE_SIZE // CMP_STRIDE
    n_sub = n_pages * per_page
    rows = SAMPLE_PAGES * per_page
    w1 = cmp_w1.reshape(2, CMP_BLOCK, d, d).astype(BF16)
    out = jax.ShapeDtypeStruct((bs, 4, n_sub, d), F32)
    ospec = pl.BlockSpec((1, 4, rows, d), lambda s, c, pt: (s, 0, c, 0))
    page_specs = [spec for kind in range(4) for spec in _page_specs(kind, d)]
    return pl.pallas_call(
        _sample_compress_kernel,
        grid_spec=pltpu.PrefetchScalarGridSpec(
            num_scalar_prefetch=1,
            grid=(bs, n_pages // SAMPLE_PAGES),
            in_specs=page_specs + [
                pl.BlockSpec((2, CMP_BLOCK, d), lambda s, c, pt: (0, 0, 0)),
                pl.BlockSpec((2, CMP_BLOCK, d, d), lambda s, c, pt: (0, 0, 0, 0))],
            out_specs=[ospec, ospec]),
        out_shape=[out, out],
        compiler_params=pltpu.CompilerParams(
            dimension_semantics=("parallel", "arbitrary"), vmem_limit_bytes=VMEM_LIMIT),
        name="nsa_sample_compress",
    )(page_table, *([cache] * (4 * SAMPLE_PAGES)), cmp_pe, w1)


def _sample_select_kernel(a0_ref, a1_ref, w2_ref, q_ref, ocmp_ref, sel_ref, *, past_len, t_len):
    d, r = NSA_HEAD_DIM, NSA_GROUP
    n_sub = a0_ref.shape[2]
    n_cmp = n_sub - 1
    n_slc = -(-(past_len + t_len) // SLC_BLOCK)
    lanes = sel_ref.shape[3]
    nt = (((1,), (1,)), ((), ()))
    rq = r * t_len
    for g in range(NSA_KV_HEADS):
        kc = jnp.dot(_gelu_tanh(a0_ref[0, g] + pltpu.roll(a1_ref[0, g], n_sub - 1, 0)).astype(BF16), w2_ref[0],
                     preferred_element_type=F32).astype(BF16)
        vc = jnp.dot(_gelu_tanh(a0_ref[0, 2 + g] + pltpu.roll(a1_ref[0, 2 + g], n_sub - 1, 0)).astype(BF16),
                     w2_ref[1], preferred_element_type=F32).astype(BF16)
        q = q_ref[0, g]
        s = lax.dot_general(q, kc, nt, preferred_element_type=F32)
        pos = past_len + lax.broadcasted_iota(jnp.int32, (rq, 1), 0) % t_len
        n_idx = lax.broadcasted_iota(jnp.int32, (rq, n_sub), 1)
        p_c = _masked_softmax_rows(s, (n_idx * CMP_STRIDE + (CMP_BLOCK - 1) <= pos) & (n_idx < n_cmp))
        ocmp_ref[0, g] = jnp.dot(p_c.astype(BF16), vc, preferred_element_type=F32)
        ri = lax.broadcasted_iota(jnp.int32, (8, rq), 0)
        cj = lax.broadcasted_iota(jnp.int32, (8, rq), 1)
        head_sum = jnp.where(cj % t_len == ri, 1.0, 0.0).astype(BF16)
        p_hi = p_c.astype(BF16)
        p_lo = (p_c - p_hi.astype(F32)).astype(BF16)
        p_sum = (jnp.dot(head_sum, p_hi, preferred_element_type=F32)
                 + jnp.dot(head_sum, p_lo, preferred_element_type=F32))
        ci = lax.broadcasted_iota(jnp.int32, (n_sub, lanes), 0)
        mi = lax.broadcasted_iota(jnp.int32, (n_sub, lanes), 1)
        overlap = jnp.where((ci * CMP_STRIDE < mi * SLC_BLOCK + SLC_BLOCK)
                            & (ci * CMP_STRIDE + CMP_BLOCK > mi * SLC_BLOCK) & (ci < n_cmp), 1.0, 0.0).astype(BF16)
        s_hi = p_sum.astype(BF16)
        s_lo = (p_sum - s_hi.astype(F32)).astype(BF16)
        imp = (jnp.dot(s_hi, overlap, preferred_element_type=F32)
               + jnp.dot(s_lo, overlap, preferred_element_type=F32))
        blk = lax.broadcasted_iota(jnp.int32, (8, lanes), 1)
        blk_f = blk.astype(F32)
        tpos = past_len + lax.broadcasted_iota(jnp.int32, (8, 1), 0) % t_len
        cur = tpos // SLC_BLOCK
        forced = (blk == 0) | (blk == cur) | (blk == cur - 1)
        work = jnp.where(blk * SLC_BLOCK <= tpos, jnp.where(forced, 1e6, imp), -1e6)
        work = jnp.where(blk < n_slc, work, NEG_INF)
        sel = jnp.zeros((8, lanes), F32)
        for _ in range(min(SLC_TOPK, n_slc)):
            m = jnp.max(work, axis=-1, keepdims=True)
            first = jnp.min(jnp.where(work == m, blk_f, float(lanes)), axis=-1, keepdims=True)
            pick = blk_f == first
            sel = jnp.where(pick, 1.0, sel)
            work = jnp.where(pick, NEG_INF, work)
        sel_ref[0, g] = sel


def nsa_sample_select(a0, a1, cmp_w2, q_s, past_len, t_len):
    bs, _, n_sub, d = a0.shape
    n_slc = -(-(past_len + t_len) // SLC_BLOCK)
    lanes = _round_up(n_slc, LANES)
    rq = NSA_GROUP * t_len
    aspec = pl.BlockSpec((1, 4, n_sub, d), lambda s: (s, 0, 0, 0))
    return pl.pallas_call(
        functools.partial(_sample_select_kernel, past_len=past_len, t_len=t_len),
        grid=(bs,),
        in_specs=[aspec, aspec, pl.BlockSpec((2, d, d), lambda s: (0, 0, 0)),
                  pl.BlockSpec((1, NSA_KV_HEADS, rq, d), lambda s: (s, 0, 0, 0))],
        out_specs=[pl.BlockSpec((1, NSA_KV_HEADS, rq, d), lambda s: (s, 0, 0, 0)),
                   pl.BlockSpec((1, NSA_KV_HEADS, 8, lanes), lambda s: (s, 0, 0, 0))],
        out_shape=[jax.ShapeDtypeStruct((bs, NSA_KV_HEADS, rq, d), F32),
                   jax.ShapeDtypeStruct((bs, NSA_KV_HEADS, 8, lanes), F32)],
        compiler_params=pltpu.CompilerParams(dimension_semantics=("parallel",), vmem_limit_bytes=VMEM_LIMIT),
        name="nsa_sample_select",
    )(a0, a1, cmp_w2.astype(BF16), q_s)


def _sample_attend_kernel(pt_ref, *refs, past_len, t_len, row0):
    pages = refs[:SAMPLE_PAGES]
    (q_ref, sel_ref, ocmp_ref, newrows_ref, wincache_ref, newwin_ref, gate_ref, o_ref,
     m_scr, l_scr, acc_scr) = refs[SAMPLE_PAGES:]
    s_idx = pl.program_id(0)
    c = pl.program_id(1)
    d, r = NSA_HEAD_DIM, NSA_GROUP
    rq = r * t_len
    lanes = sel_ref.shape[3]
    tk = SAMPLE_PAGES * PAGE_SIZE
    nt = (((1,), (1,)), ((), ()))

    @pl.when(c == 0)
    def _():
        m_scr[...] = jnp.full(m_scr.shape, -1e30, F32)
        l_scr[...] = jnp.zeros(l_scr.shape, F32)
        acc_scr[...] = jnp.zeros(acc_scr.shape, F32)

    row = lax.broadcasted_iota(jnp.int32, (rq, 1), 0)
    t_row = row % t_len
    pos = past_len + t_row
    ti = lax.broadcasted_iota(jnp.int32, (rq, 8), 1)
    tok_expand = jnp.where(ti == t_row, 1.0, 0.0).astype(BF16)

    def online(g, s, mask, v):
        s = jnp.where(mask, s, -1e30)
        m_old = m_scr[g]
        m_new = jnp.maximum(m_old, jnp.max(s, axis=-1, keepdims=True))
        alpha = jnp.exp(m_old - m_new)
        p = jnp.where(mask, jnp.exp(s - m_new), 0.0)
        l_scr[g] = alpha * l_scr[g] + jnp.sum(p, axis=-1, keepdims=True)
        acc_scr[g] = alpha * acc_scr[g] + jnp.dot(p.astype(BF16), v, preferred_element_type=F32)
        m_scr[g] = m_new

    sel16 = []
    for g in range(NSA_KV_HEADS):
        sel16.append(jnp.dot(tok_expand, sel_ref[0, g].astype(BF16), preferred_element_type=F32))
        k = jnp.concatenate([pg[0, :, g * d:(g + 1) * d] for pg in pages], axis=0)
        v = jnp.concatenate([pg[0, :, (2 + g) * d:(3 + g) * d] for pg in pages], axis=0)
        q = q_ref[0, g]
        s = lax.dot_general(q, k.astype(BF16), nt, preferred_element_type=F32)
        ei = lax.broadcasted_iota(jnp.int32, (lanes, tk), 0)
        ej = lax.broadcasted_iota(jnp.int32, (lanes, tk), 1)
        expand = jnp.where(ei == c * (tk // SLC_BLOCK) + ej // SLC_BLOCK, 1.0, 0.0).astype(BF16)
        picked = jnp.dot(sel16[g].astype(BF16), expand, preferred_element_type=F32)
        kpos = c * tk + lax.broadcasted_iota(jnp.int32, (rq, tk), 1)
        online(g, s, (picked > 0.5) & (kpos <= pos), v.astype(BF16))

    @pl.when(c == pl.num_programs(1) - 1)
    def _():
        mine = (row0 // t_len + s_idx) % (NEW_ROWS_BLK // t_len)
        j = lax.broadcasted_iota(jnp.int32, (rq, NEW_ROWS_BLK), 1)
        own = j // t_len == mine
        new_pos = past_len + j % t_len
        sig = jax.nn.sigmoid(gate_ref[...])
        gsel = jnp.where(j == mine * t_len + t_row, 1.0, 0.0).astype(BF16)
        s_hi = sig.astype(BF16)
        s_lo = (sig - s_hi.astype(F32)).astype(BF16)
        sig_rows = (jnp.dot(gsel, s_hi, preferred_element_type=F32)
                    + jnp.dot(gsel, s_lo, preferred_element_type=F32))
        lane = lax.broadcasted_iota(jnp.int32, sig_rows.shape, 1)
        lane_l = lax.broadcasted_iota(jnp.int32, (rq, lanes), 1)
        for g in range(NSA_KV_HEADS):
            q = q_ref[0, g]
            kn = newrows_ref[:, (4 + g) * d:(5 + g) * d].astype(BF16)
            vn = newrows_ref[:, (6 + g) * d:(7 + g) * d].astype(BF16)
            s = lax.dot_general(q, kn, nt, preferred_element_type=F32)
            last_picked = jnp.sum(jnp.where(lane_l == past_len // SLC_BLOCK, sel16[g], 0.0), axis=-1, keepdims=True)
            online(g, s, own & (new_pos <= pos) & (last_picked > 0.5), vn)
            o_slc = acc_scr[g] / jnp.maximum(l_scr[g], 1e-30)
            w_len = wincache_ref.shape[1]
            kw = wincache_ref[0, :, g * d:(g + 1) * d].astype(BF16)
            vw = wincache_ref[0, :, (2 + g) * d:(3 + g) * d].astype(BF16)
            s1 = lax.dot_general(q, kw, nt, preferred_element_type=F32)
            kp1 = past_len - w_len + lax.broadcasted_iota(jnp.int32, (rq, w_len), 1)
            d1 = pos - kp1
            m1 = (d1 >= 0) & (d1 < WINDOW) & (kp1 >= 0)
            knw = newwin_ref[:, g * d:(g + 1) * d].astype(BF16)
            vnw = newwin_ref[:, (2 + g) * d:(3 + g) * d].astype(BF16)
            s2 = lax.dot_general(q, knw, nt, preferred_element_type=F32)
            d2 = pos - new_pos
            m2 = own & (d2 >= 0) & (d2 < WINDOW)
            s1 = jnp.where(m1, s1, -1e30)
            s2 = jnp.where(m2, s2, -1e30)
            mx = jnp.maximum(jnp.max(s1, axis=-1, keepdims=True), jnp.max(s2, axis=-1, keepdims=True))
            p1 = jnp.where(m1, jnp.exp(s1 - mx), 0.0)
            p2 = jnp.where(m2, jnp.exp(s2 - mx), 0.0)
            den = jnp.sum(p1, axis=-1, keepdims=True) + jnp.sum(p2, axis=-1, keepdims=True)
            o_win = (jnp.dot(p1.astype(BF16), vw, preferred_element_type=F32)
                     + jnp.dot(p2.astype(BF16), vnw, preferred_element_type=F32)) / jnp.maximum(den, 1e-30)
            out = jnp.zeros((rq, d), F32)
            for branch, o_b in enumerate((ocmp_ref[0, g], o_slc, o_win)):
                col = branch * NSA_HEADS + g * r + row // t_len
                gate = jnp.sum(jnp.where(lane == col, sig_rows, 0.0), axis=-1, keepdims=True)
                out = out + gate * o_b
            o_ref[0, g] = out


def nsa_sample_attend(cache, page_table, q_s, sel, o_cmp, rows, win, win_cache, proj, past_len, t_len, row0):
    bs, n_pages = page_table.shape
    d = NSA_HEAD_DIM
    rq = NSA_GROUP * t_len
    lanes = sel.shape[3]
    w_len = win_cache.shape[1]
    assert row0 % t_len == 0 and NEW_ROWS_BLK % t_len == 0 and past_len % SLC_BLOCK == 0
    blk = lambda s: (row0 + s * t_len) // NEW_ROWS_BLK
    per_seq = lambda shape: pl.BlockSpec((1,) + shape, lambda s, c, pt: (s, 0, 0, 0))
    return pl.pallas_call(
        functools.partial(_sample_attend_kernel, past_len=past_len, t_len=t_len, row0=row0),
        grid_spec=pltpu.PrefetchScalarGridSpec(
            num_scalar_prefetch=1,
            grid=(bs, n_pages // SAMPLE_PAGES),
            in_specs=_page_specs(1, 4 * d) + [
                per_seq((NSA_KV_HEADS, rq, d)), per_seq((NSA_KV_HEADS, 8, lanes)), per_seq((NSA_KV_HEADS, rq, d)),
                pl.BlockSpec((NEW_ROWS_BLK, 8 * d), lambda s, c, pt: (blk(s), 0)),
                pl.BlockSpec((1, w_len, 4 * d), lambda s, c, pt: (s, 0, 0)),
                pl.BlockSpec((NEW_ROWS_BLK, 4 * d), lambda s, c, pt: (blk(s), 0)),
                pl.BlockSpec((NEW_ROWS_BLK, LANES), lambda s, c, pt: (blk(s), GATE_COL0 // LANES))],
            out_specs=per_seq((NSA_KV_HEADS, rq, d)),
            scratch_shapes=[pltpu.VMEM((NSA_KV_HEADS, rq, 1), F32), pltpu.VMEM((NSA_KV_HEADS, rq, 1), F32),
                            pltpu.VMEM((NSA_KV_HEADS, rq, d), F32)]),
        out_shape=jax.ShapeDtypeStruct((bs, NSA_KV_HEADS, rq, d), F32),
        compiler_params=pltpu.CompilerParams(
            dimension_semantics=("parallel", "arbitrary"), vmem_limit_bytes=VMEM_LIMIT),
        name="nsa_sample_attend",
    )(page_table, *([cache] * SAMPLE_PAGES), q_s, sel, o_cmp, rows, win_cache, win, proj)


SC_ROWS = 512
SC_COLS = 512


def _short_conv_kernel(b_ref, c_ref, h_ref, w_ref, o_ref, tail_ref, prev_scr):
    @pl.when(pl.program_id(2) == 0)
    def _():
        prev_scr[...] = jnp.zeros_like(prev_scr)

    rows = c_ref.shape[0]
    u = c_ref[...] * h_ref[...]
    prev = prev_scr[...]
    row8 = lax.broadcasted_iota(jnp.int32, prev.shape, 0)
    conv = u * w_ref[SC_KSIZE - 1:SC_KSIZE, :]
    for k in range(1, SC_KSIZE):
        rolled = pltpu.roll(u, k, 0)
        top = jnp.where(row8 < k, pltpu.roll(prev, k, 0), rolled[0:8])
        shifted = jnp.concatenate([top, rolled[8:]], axis=0)
        conv = conv + shifted * w_ref[SC_KSIZE - 1 - k:SC_KSIZE - k, :]
    prev_scr[...] = u[rows - 8:rows]
    o_ref[...] = (b_ref[...] * conv).astype(o_ref.dtype)
    tail_ref[0] = u[rows - 8:rows]


def short_conv_prompt(proj, n_batch, t_len, sc_w):
    nr = t_len // SC_ROWS
    nh = SC_WIDTH // SC_COLS
    blk0 = SC_COL0 // SC_COLS
    assert SC_COL0 % SC_COLS == 0
    col = lambda part: pl.BlockSpec((SC_ROWS, SC_COLS), lambda b, j, i: (b * nr + i, blk0 + part * nh + j))
    out, tail = pl.pallas_call(
        _short_conv_kernel,
        grid=(n_batch, nh, nr),
        in_specs=[col(0), col(1), col(2), pl.BlockSpec((SC_KSIZE, SC_COLS), lambda b, j, i: (0, j))],
        out_specs=[pl.BlockSpec((SC_ROWS, SC_COLS), lambda b, j, i: (b * nr + i, j)),
                   pl.BlockSpec((1, 8, SC_COLS), lambda b, j, i: (b, 0, j))],
        out_shape=[jax.ShapeDtypeStruct((n_batch * t_len, SC_WIDTH), BF16),
                   jax.ShapeDtypeStruct((n_batch, 8, SC_WIDTH), F32)],
        scratch_shapes=[pltpu.VMEM((8, SC_COLS), F32)],
        compiler_params=pltpu.CompilerParams(
            dimension_semantics=("parallel", "parallel", "arbitrary"), vmem_limit_bytes=VMEM_LIMIT),
        name="short_conv",
    )(proj, proj, proj, sc_w)
    return out, tail[:, 8 - (SC_KSIZE - 1):]


def _ret_tables(pos, chunk):
    half = RET_DQK // 2
    inv = RET_THETA ** (-jnp.arange(half, dtype=F32) / half)
    ang = pos.astype(F32)[:, None] * inv[None, :]
    log_g = jnp.log1p(-(2.0 ** (-5.0 - jnp.arange(RET_HEADS, dtype=F32))))
    i = jnp.arange(chunk, dtype=F32)
    diff = i[:, None] - i[None, :]
    intra = jnp.where(diff >= 0, jnp.exp(jnp.maximum(diff, 0.0)[None] * log_g[:, None, None]), 0.0)
    q_dec = jnp.exp((i[None, :] + 1.0) * log_g[:, None])[..., None]
    k_dec = jnp.exp((chunk - 1.0 - i)[None, :] * log_g[:, None])[..., None]
    c_dec = jnp.exp(chunk * log_g)[:, None, None]
    return jnp.cos(ang), jnp.sin(ang), intra, q_dec, k_dec, c_dec


def _retention_kernel(q_ref, k_ref, v_ref, g_ref, cos_ref, sin_ref, intra_ref, qd_ref, kd_ref, cd_ref, gn_ref,
                      s0_ref, o_ref, s_out_ref, s_scr):
    c_idx = pl.program_id(1)

    @pl.when(c_idx == 0)
    def _():
        s_scr[...] = s0_ref[0]

    cos, sin = cos_ref[...], sin_ref[...]
    half = RET_DQK // 2
    nt = (((1,), (1,)), ((), ()))
    tn = (((0,), (0,)), ((), ()))

    def rot(x):
        x1, x2 = x[:, :half], x[:, half:]
        return jnp.concatenate([x1 * cos - x2 * sin, x1 * sin + x2 * cos], axis=-1)

    for h in range(RET_HEADS):
        cols = slice(h * RET_DQK, (h + 1) * RET_DQK)
        qr = (rot(q_ref[:, cols]) * (RET_DQK ** -0.5)).astype(BF16)
        kr = rot(k_ref[:, cols])
        v = v_ref[:, cols].astype(BF16)
        att = lax.dot_general(qr, kr.astype(BF16), nt, preferred_element_type=F32) * intra_ref[h]
        s_old = s_scr[h]
        o = (jnp.dot(att.astype(BF16), v, preferred_element_type=F32)
             + jnp.dot(qr, s_old.astype(BF16), preferred_element_type=F32) * qd_ref[h])
        s_scr[h] = s_old * cd_ref[h] + lax.dot_general((kr * kd_ref[h]).astype(BF16), v, tn,
                                                       preferred_element_type=F32)
        mu = jnp.mean(o, axis=-1, keepdims=True)
        dev = o - mu
        var = jnp.mean(dev * dev, axis=-1, keepdims=True)
        gate = g_ref[:, cols]
        on = dev * lax.rsqrt(var + EPS) * gn_ref[:, cols] * (gate * jax.nn.sigmoid(gate))
        o_ref[:, cols] = on.astype(o_ref.dtype)
    s_out_ref[0] = s_scr[...]


def retention(proj, pos, n_batch, t_len, chunk, state0, gn_gain):
    nc = t_len // chunk
    cos, sin, intra, q_dec, k_dec, c_dec = _ret_tables(pos, chunk)
    half = RET_DQK // 2
    col = lambda j: pl.BlockSpec((chunk, R_QK), lambda b, c: (b * nc + c, j))
    tab = pl.BlockSpec((chunk, half), lambda b, c: (c, 0))
    full = lambda a: pl.BlockSpec(a.shape, lambda b, c: (0,) * a.ndim)
    st = pl.BlockSpec((1, RET_HEADS, RET_DQK, RET_DV), lambda b, c: (b, 0, 0, 0))
    return pl.pallas_call(
        _retention_kernel,
        grid=(n_batch, nc),
        in_specs=[col(0), col(1), col(2), col(3), tab, tab, full(intra), full(q_dec), full(k_dec), full(c_dec),
                  pl.BlockSpec((1, R_V), lambda b, c: (0, 0)), st],
        out_specs=[pl.BlockSpec((chunk, R_V), lambda b, c: (b * nc + c, 0)), st],
        out_shape=[jax.ShapeDtypeStruct((n_batch * t_len, R_V), BF16),
                   jax.ShapeDtypeStruct((n_batch, RET_HEADS, RET_DQK, RET_DV), F32)],
        scratch_shapes=[pltpu.VMEM((RET_HEADS, RET_DQK, RET_DV), F32)],
        compiler_params=pltpu.CompilerParams(
            dimension_semantics=("parallel", "arbitrary"), vmem_limit_bytes=VMEM_LIMIT),
        name="retention",
    )(proj, proj, proj, proj, cos, sin, intra, q_dec, k_dec, c_dec, gn_gain.reshape(1, R_V), state0)


SSD_COL0 = 2 * R_QK + 2 * R_V
HEADS_PER_GROUP = M2_HEADS // M2_GROUPS


def _split3(x):
    a = x.astype(BF16)
    r = x - a.astype(F32)
    b = r.astype(BF16)
    c = (r - b.astype(F32)).astype(BF16)
    return a, b, c


def _exact_dot(mat_bf, x):
    out = None
    for piece in _split3(x):
        t = jnp.dot(mat_bf, piece, preferred_element_type=F32)
        out = t if out is None else out + t
    return out


def _exact_dot_r(x, mat_bf):
    out = None
    for piece in _split3(x):
        t = jnp.dot(piece, mat_bf, preferred_element_type=F32)
        out = t if out is None else out + t
    return out


def _ssd_kernel(z_ref, xa_ref, xb_ref, xc_ref, dt_ref, cw_ref, cb_ref, dtb_ref, aneg_ref, dskip_ref, norm_ref,
                buf0_ref, s0_ref, o_ref, s_out_ref, s_scr, prev_scr):
    c_idx = pl.program_id(1)
    chunk = z_ref.shape[0]
    nt = (((1,), (1,)), ((), ()))
    tn = (((0,), (0,)), ((), ()))

    @pl.when(c_idx == 0)
    def _():
        s_scr[...] = s0_ref[0]
        prev_scr[...] = buf0_ref[0]

    x = jnp.concatenate([xa_ref[...], xb_ref[...], xc_ref[...]], axis=1)
    prev = prev_scr[...]
    row8 = lax.broadcasted_iota(jnp.int32, prev.shape, 0)
    conv = x * cw_ref[M2_CONV - 1:M2_CONV, :]
    for k in range(1, M2_CONV):
        rolled = pltpu.roll(x, k, 0)
        top = jnp.where(row8 < k, pltpu.roll(prev, k, 0), rolled[0:8])
        shifted = jnp.concatenate([top, rolled[8:]], axis=0)
        conv = conv + shifted * cw_ref[M2_CONV - 1 - k:M2_CONV - k, :]
    prev_scr[...] = x[chunk - 8:chunk]
    conv = conv + cb_ref[...]
    xbc = conv * jax.nn.sigmoid(conv)
    xs = xbc[:, :M2_DINNER]

    dt_raw = dt_ref[...] + dtb_ref[...]
    dt = jnp.where(dt_raw > 20.0, dt_raw, jnp.log1p(jnp.exp(jnp.minimum(dt_raw, 20.0))))
    a = dt * aneg_ref[...]
    ri = lax.broadcasted_iota(jnp.int32, (chunk, chunk), 0)
    ci = lax.broadcasted_iota(jnp.int32, (chunk, chunk), 1)
    tri = ri >= ci
    cum = _exact_dot(jnp.where(tri, 1.0, 0.0).astype(BF16), a)
    cum_t = cum.T
    cum_last = cum[chunk - 1:chunk, :]
    hi = lax.broadcasted_iota(jnp.int32, (LANES, M2_DINNER), 0)
    li = lax.broadcasted_iota(jnp.int32, (LANES, M2_DINNER), 1)
    expand = jnp.where(hi == li // M2_HEADDIM, 1.0, 0.0).astype(BF16)
    dt_x = _exact_dot_r(dt, expand)
    cum_x = _exact_dot_r(cum, expand)
    last_x = _exact_dot_r(cum_last, expand)
    xdt = xs * dt_x
    x_dec = (xdt * jnp.exp(last_x - cum_x)).astype(BF16)
    xdt_bf = xdt.astype(BF16)
    e_cum_x = jnp.exp(cum_x)
    e_last_x = jnp.exp(last_x)

    y_parts = []
    for gi in range(M2_GROUPS):
        b_g = xbc[:, M2_DINNER + gi * M2_STATE:M2_DINNER + (gi + 1) * M2_STATE].astype(BF16)
        c_g = xbc[:, M2_DINNER + (M2_GROUPS + gi) * M2_STATE:M2_DINNER + (M2_GROUPS + gi + 1) * M2_STATE].astype(BF16)
        cb = lax.dot_general(c_g, b_g, nt, preferred_element_type=F32)
        gcols = slice(gi * HEADS_PER_GROUP * M2_HEADDIM, (gi + 1) * HEADS_PER_GROUP * M2_HEADDIM)
        s_old = s_scr[:, gcols]
        y_state = jnp.dot(c_g, s_old.astype(BF16), preferred_element_type=F32) * e_cum_x[:, gcols]
        s_scr[:, gcols] = s_old * e_last_x[:, gcols] + lax.dot_general(b_g, x_dec[:, gcols], tn,
                                                                        preferred_element_type=F32)
        pair_lane = lax.broadcasted_iota(jnp.int32, (chunk, LANES), 1)
        intra = []
        for pr in range(HEADS_PER_GROUP // 2):
            outs = []
            for sub in range(2):
                h = gi * HEADS_PER_GROUP + pr * 2 + sub
                seg = cum[:, h:h + 1] - cum_t[h:h + 1, :]
                l_mat = jnp.where(tri, jnp.exp(jnp.where(tri, seg, 0.0)), 0.0)
                lanes = slice(gi * HEADS_PER_GROUP * M2_HEADDIM + pr * LANES,
                              gi * HEADS_PER_GROUP * M2_HEADDIM + (pr + 1) * LANES)
                outs.append(jnp.dot((cb * l_mat).astype(BF16), xdt_bf[:, lanes], preferred_element_type=F32))
            intra.append(jnp.where(pair_lane < M2_HEADDIM, outs[0], outs[1]))
        y_parts.append(jnp.concatenate(intra, axis=1) + y_state)
    y = jnp.concatenate(y_parts, axis=1) + dskip_ref[...] * xs
    z = z_ref[...]
    y = y * (z * jax.nn.sigmoid(z))
    gw = M2_DINNER // M2_GROUPS
    outs = []
    for gi in range(M2_GROUPS):
        yg = y[:, gi * gw:(gi + 1) * gw]
        outs.append(yg * lax.rsqrt(jnp.mean(yg * yg, axis=-1, keepdims=True) + EPS))
    o_ref[...] = (jnp.concatenate(outs, axis=1) * norm_ref[...]).astype(o_ref.dtype)
    s_out_ref[0] = s_scr[...]


def ssd(proj, n_batch, t_len, chunk, conv_buf, state0, conv_w, conv_b, dt_bias, a_log, d_skip, m2_norm):
    nc = t_len // chunk
    z_blk = SSD_COL0 // M2_DINNER
    xw = M2_CONV_DIM // 3
    xbc_blk = (SSD_COL0 + M2_DINNER) // xw
    assert (SSD_COL0 + M2_DINNER) % xw == 0 and xw % LANES == 0
    dt_blk = (SSD_COL0 + M2_DINNER + M2_CONV_DIM) // LANES
    pad = lambda v: jnp.pad(v.reshape(1, -1), ((0, 0), (0, LANES - v.shape[-1])))
    buf8 = jnp.pad(conv_buf, ((0, 0), (8 - (M2_CONV - 1), 0), (0, 0)))
    st_t = state0.transpose(0, 3, 1, 2).reshape(n_batch, M2_STATE, M2_DINNER)
    row = lambda a: pl.BlockSpec(a.shape, lambda b, c: (0, 0))
    cw = conv_w
    cb = conv_b.reshape(1, -1)
    dtb, aneg = pad(dt_bias), pad(-jnp.exp(a_log))
    dsk = jnp.repeat(d_skip, M2_HEADDIM).reshape(1, -1)
    nrm = m2_norm.reshape(1, -1)
    st = pl.BlockSpec((1, M2_STATE, M2_DINNER), lambda b, c: (b, 0, 0))
    out, s_fin = pl.pallas_call(
        _ssd_kernel,
        grid=(n_batch, nc),
        in_specs=[pl.BlockSpec((chunk, M2_DINNER), lambda b, c: (b * nc + c, z_blk)),
                  pl.BlockSpec((chunk, xw), lambda b, c: (b * nc + c, xbc_blk)),
                  pl.BlockSpec((chunk, xw), lambda b, c: (b * nc + c, xbc_blk + 1)),
                  pl.BlockSpec((chunk, xw), lambda b, c: (b * nc + c, xbc_blk + 2)),
                  pl.BlockSpec((chunk, LANES), lambda b, c: (b * nc + c, dt_blk)),
                  row(cw), row(cb), row(dtb), row(aneg), row(dsk), row(nrm),
                  pl.BlockSpec((1, 8, M2_CONV_DIM), lambda b, c: (b, 0, 0)), st],
        out_specs=[pl.BlockSpec((chunk, M2_DINNER), lambda b, c: (b * nc + c, 0)), st],
        out_shape=[jax.ShapeDtypeStruct((n_batch * t_len, M2_DINNER), BF16),
                   jax.ShapeDtypeStruct((n_batch, M2_STATE, M2_DINNER), F32)],
        scratch_shapes=[pltpu.VMEM((M2_STATE, M2_DINNER), F32), pltpu.VMEM((8, M2_CONV_DIM), F32)],
        compiler_params=pltpu.CompilerParams(
            dimension_semantics=("parallel", "arbitrary"), vmem_limit_bytes=VMEM_LIMIT),
        name="ssd",
    )(proj, proj, proj, proj, proj, cw, cb, dtb, aneg, dsk, nrm, buf8, st_t)
    s_fin = s_fin.reshape(n_batch, M2_STATE, M2_HEADS, M2_HEADDIM).transpose(0, 2, 3, 1)
    return out, s_fin


def _rotary(x, pos, rot_dims, theta):
    half = rot_dims // 2
    inv = theta ** (-jnp.arange(half, dtype=F32) / half)
    ang = pos.astype(F32)[:, None] * inv[None, :]
    shape = (pos.shape[0],) + (1,) * (x.ndim - 3) + (half,)
    cos = jnp.cos(ang).reshape(shape)
    sin = jnp.sin(ang).reshape(shape)
    x1 = x[..., :half]
    x2 = x[..., half:rot_dims]
    rot = jnp.concatenate([x1 * cos - x2 * sin, x1 * sin + x2 * cos], axis=-1)
    return jnp.concatenate([rot, x[..., rot_dims:]], axis=-1)


def _masked_softmax(s, mask, axis=-1):
    s = jnp.where(mask, s, -1e30)
    p = jnp.exp(s - jnp.max(s, axis=axis, keepdims=True)) * mask
    return p / jnp.maximum(jnp.sum(p, axis=axis, keepdims=True), 1e-30)


def _causal_conv(u, buf, w, b=None):
    k_size = w.shape[0]
    t_len = u.shape[1]
    ext = jnp.concatenate([buf.astype(u.dtype), u], axis=1)
    y = ext[:, 0:t_len] * w[0]
    for j in range(1, k_size):
        y = y + ext[:, j:j + t_len] * w[j]
    if b is not None:
        y = y + b
    return y, ext[:, t_len:]


def _to_chunks(a, chunk):
    return a.reshape(a.shape[0], a.shape[1] // chunk, chunk, *a.shape[2:]).swapaxes(0, 1)


def _nsa_compress_jnp(rows, pe, w1, w2):
    bn, length, g, d = rows.shape
    n_part = CMP_BLOCK // CMP_STRIDE
    n_sub = length // CMP_STRIDE
    n_cmp = n_sub - n_part + 1
    sub = rows[:, :n_sub * CMP_STRIDE].reshape(bn, n_sub, CMP_STRIDE, g, d)
    w1p = w1.reshape(n_part, CMP_STRIDE, d, d)
    pre = jnp.einsum('jsd,jsde->e', pe.reshape(n_part, CMP_STRIDE, d), w1p)
    for j in range(n_part):
        pre = pre + jnp.einsum('bnsgd,sde->bnge', sub[:, j:j + n_cmp], w1p[j])
    return jnp.einsum('bnge,ef->bngf', jax.nn.gelu(pre), w2)


def _nsa_sparse_jnp(q, q_pos, rows, cmp_pe, cmp_w1, cmp_w2):
    bn, t_len, g, r, d = q.shape
    length = rows.shape[1]
    kc = _nsa_compress_jnp(rows[:, :, 0], cmp_pe[0], cmp_w1[0], cmp_w2[0])
    vc = _nsa_compress_jnp(rows[:, :, 1], cmp_pe[1], cmp_w1[1], cmp_w2[1])
    n_cmp = kc.shape[1]
    c_start = jnp.arange(n_cmp) * CMP_STRIDE
    c_mask = (c_start + CMP_BLOCK - 1)[None, :] <= q_pos[:, None]
    p_c = _masked_softmax(jnp.einsum('btgrd,bngd->btgrn', q, kc), c_mask[None, :, None, None, :])
    o_cmp = jnp.einsum('btgrn,bngd->btgrd', p_c, vc)
    n_slc = -(-length // SLC_BLOCK)
    s_start = jnp.arange(n_slc) * SLC_BLOCK
    overlap = ((c_start[:, None] < s_start[None, :] + SLC_BLOCK)
               & (c_start[:, None] + CMP_BLOCK > s_start[None, :])).astype(F32)
    imp = jnp.einsum('btgrn,nm->btgm', p_c, overlap)
    blk = jnp.arange(n_slc)[None, :]
    cur = (q_pos // SLC_BLOCK)[:, None]
    valid = (s_start[None, :] <= q_pos[:, None])[None, :, None, :]
    forced = ((blk == 0) | (blk == cur) | (blk == cur - 1))[None, :, None, :]
    score = jnp.where(valid, jnp.where(forced, 1e6, imp), -1e6)
    n_sel = min(SLC_TOPK, n_slc)
    _, idx = lax.top_k(score, n_sel)
    picked = jnp.any(idx[..., None] == jnp.arange(n_slc), axis=-2)
    kmask = jnp.repeat(picked, SLC_BLOCK, axis=-1)[..., :length]
    kmask = kmask & (jnp.arange(length)[None, :] <= q_pos[:, None])[None, :, None, :]
    s = jnp.einsum('btgrd,blgd->btgrl', q, rows[:, :, 2])
    p = _masked_softmax(s, kmask[:, :, :, None, :])
    o_slc = jnp.einsum('btgrl,blgd->btgrd', p, rows[:, :, 3])
    return o_cmp, o_slc


def _window_attention_jnp(q, q_pos, k, v, k_pos):
    s = jnp.einsum('ntgrd,nsgd->ntgrs', q, k)
    dpos = q_pos[:, :, None] - k_pos[:, None, :]
    mask = (dpos >= 0) & (dpos < WINDOW) & (k_pos[:, None, :] >= 0)
    p = _masked_softmax(s, mask[:, :, None, None, :])
    return jnp.einsum('ntgrs,nsgd->ntgrd', p, v)


def _nsa_sample_jnp(q, new_rows, win_rows, gates, pos, past_rows, win_buf, cmp_pe, cmp_w1, cmp_w2):
    bn, t_len, g, r, d = q.shape
    rows = jnp.concatenate([past_rows, new_rows], axis=1)
    o_cmp, o_slc = _nsa_sparse_jnp(q, pos, rows, cmp_pe, cmp_w1, cmp_w2)
    w_len = win_buf.shape[1]
    ext = jnp.concatenate([win_buf, win_rows], axis=1)
    k_pos = (pos[0] - w_len + jnp.arange(w_len + t_len))[None]
    o_win = _window_attention_jnp(q, pos[None], ext[:, :, 0], ext[:, :, 1], k_pos)
    gate = jax.nn.sigmoid(gates).reshape(bn, t_len, 3, g, r, 1)
    o_nsa = gate[:, :, 0] * o_cmp + gate[:, :, 1] * o_slc + gate[:, :, 2] * o_win
    return o_nsa.reshape(bn * t_len, E_Q), ext[:, t_len:]


def _retention_jnp(q, k, v, state, chunk):
    bn, t_len, h, _ = q.shape
    dv = v.shape[-1]
    log_g = jnp.log1p(-(2.0 ** (-5.0 - jnp.arange(h, dtype=F32))))
    i = jnp.arange(chunk, dtype=F32)
    diff = i[:, None] - i[None, :]
    intra = jnp.where(diff >= 0, jnp.exp(jnp.maximum(diff, 0.0)[None] * log_g[:, None, None]), 0.0)
    q_dec = jnp.exp((i[:, None] + 1.0) * log_g[None, :])
    k_dec = jnp.exp((chunk - 1.0 - i)[:, None] * log_g[None, :])
    c_dec = jnp.exp(chunk * log_g)

    def step(s_mat, inp):
        qc, kc, vc = inp
        att = jnp.einsum('bihd,bjhd->bhij', qc, kc) * intra
        o = (jnp.einsum('bhij,bjhv->bihv', att, vc)
             + jnp.einsum('bihd,bhdv->bihv', qc, s_mat) * q_dec[None, :, :, None])
        s_mat = s_mat * c_dec[None, :, None, None] + jnp.einsum('bjhd,bjhv->bhdv', kc * k_dec[None, :, :, None], vc)
        return s_mat, o

    s_fin, o = lax.scan(step, state, (_to_chunks(q, chunk), _to_chunks(k, chunk), _to_chunks(v, chunk)))
    return o.swapaxes(0, 1).reshape(bn, t_len, h, dv), s_fin


def _ssd_jnp(x, dt, a_neg, b_in, c_in, state, chunk):
    bn, t_len, h, p = x.shape
    rep = h // b_in.shape[2]
    a = dt * a_neg
    b_h = jnp.repeat(b_in, rep, axis=2)
    c_h = jnp.repeat(c_in, rep, axis=2)
    xdt = x * dt[..., None]
    tri = jnp.arange(chunk)[:, None] >= jnp.arange(chunk)[None, :]

    def step(s_mat, inp):
        a_c, x_c, b_c, c_c = inp
        cum = jnp.cumsum(a_c, axis=1)
        seg = cum[:, :, None, :] - cum[:, None, :, :]
        l_mat = jnp.exp(jnp.where(tri[None, :, :, None], seg, -1e30))
        cb = jnp.einsum('bihn,bjhn->bijh', c_c, b_c) * l_mat
        y = (jnp.einsum('bijh,bjhp->bihp', cb, x_c)
             + jnp.einsum('bihn,bhpn->bihp', c_c, s_mat) * jnp.exp(cum)[..., None])
        dec_end = jnp.exp(cum[:, -1:, :] - cum)
        s_mat = (s_mat * jnp.exp(cum[:, -1, :])[:, :, None, None]
                 + jnp.einsum('bjhn,bjhp->bhpn', b_c * dec_end[..., None], x_c))
        return s_mat, y

    s_fin, y = lax.scan(step, state, (_to_chunks(a, chunk), _to_chunks(xdt, chunk), _to_chunks(b_h, chunk),
                                      _to_chunks(c_h, chunk)))
    return y.swapaxes(0, 1).reshape(bn, t_len, h, p), s_fin


def _odd_mixer_jnp(proj, pos, ret_state, ssm_state, conv_buf, gn_gain, conv_w, conv_b, dt_bias, a_log, d_skip,
                   m2_norm):
    bn, t_len, _ = proj.shape
    chunk = CHUNK if t_len % CHUNK == 0 else t_len
    q, k, v, g, z, xbc, dt = jnp.split(proj, O_SPLITS, axis=-1)
    q = _rotary(q.reshape(bn, t_len, RET_HEADS, RET_DQK), pos, RET_DQK, RET_THETA) * (RET_DQK ** -0.5)
    k = _rotary(k.reshape(bn, t_len, RET_HEADS, RET_DQK), pos, RET_DQK, RET_THETA)
    v = v.reshape(bn, t_len, RET_HEADS, RET_DV)
    o_ret, ret_new = _retention_jnp(q, k, v, ret_state, chunk)
    mu = jnp.mean(o_ret, axis=-1, keepdims=True)
    var = jnp.mean(jnp.square(o_ret - mu), axis=-1, keepdims=True)
    o_ret = ((o_ret - mu) * lax.rsqrt(var + EPS)).reshape(bn, t_len, R_V) * gn_gain
    o_ret = jax.nn.silu(g) * o_ret
    xbc_c, _ = _causal_conv(xbc, conv_buf, conv_w, conv_b)
    xbc_c = jax.nn.silu(xbc_c)
    xs, b_in, c_in = jnp.split(xbc_c, [M2_DINNER, M2_DINNER + M2_GROUPS * M2_STATE], axis=-1)
    xs = xs.reshape(bn, t_len, M2_HEADS, M2_HEADDIM)
    b_in = b_in.reshape(bn, t_len, M2_GROUPS, M2_STATE)
    c_in = c_in.reshape(bn, t_len, M2_GROUPS, M2_STATE)
    dt = jax.nn.softplus(dt + dt_bias)
    a_neg = -jnp.exp(a_log)
    y, ssm_new = _ssd_jnp(xs, dt, a_neg, b_in, c_in, ssm_state, chunk)
    y = y + d_skip[:, None] * xs
    y = y.reshape(bn, t_len, M2_DINNER) * jax.nn.silu(z)
    yg = y.reshape(bn, t_len, M2_GROUPS, M2_DINNER // M2_GROUPS)
    yg = yg * lax.rsqrt(jnp.mean(yg * yg, axis=-1, keepdims=True) + EPS)
    y = yg.reshape(bn, t_len, M2_DINNER) * m2_norm
    return o_ret.reshape(bn * t_len, R_V), y.reshape(bn * t_len, M2_DINNER), ret_new, ssm_new


def _short_conv_jnp(sc, n_batch, t_len, conv_buf, sc_w):
    b_gate, c_gate, h_in = jnp.split(sc.reshape(n_batch, t_len, E_SC), 3, axis=-1)
    conv_out, new_conv = _causal_conv(c_gate * h_in, conv_buf, sc_w)
    return (b_gate * conv_out).reshape(n_batch * t_len, SC_WIDTH), new_conv


def kernel(x_prompt, x_sample, cache_nsa_kv, cache_nsa_win, state_sc_conv, state_ret, state_ssm, state_m2_conv,
           page_table, norm_w, final_norm, e_w_in, e_w_out, e_cmp_pe, e_cmp_w1, e_cmp_w2, e_sc_conv, o_w_in,
           o_w_out, o_ret_gn, o_m2_conv_w, o_m2_conv_b, o_m2_dt_bias, o_m2_a_log, o_m2_d, o_m2_norm, peer_wq,
           peer_keys, peer_u, peer_v):
    bp, tp, dm = x_prompt.shape
    bs, ts, _ = x_sample.shape
    n_p, n_s = bp * tp, bs * ts
    g, r, d = NSA_KV_HEADS, NSA_GROUP, NSA_HEAD_DIM
    past_len = page_table.shape[1] * PAGE_SIZE
    pos_p = jnp.arange(tp, dtype=jnp.int32)
    pos_s = past_len + jnp.arange(ts, dtype=jnp.int32)
    pos_rows = jnp.concatenate([jnp.tile(pos_p, bp), jnp.tile(pos_s, bs)])
    x = jnp.concatenate([x_prompt.reshape(n_p, dm), x_sample.reshape(n_s, dm)], axis=0)

    w0 = e_w_in[0]
    w_in = jnp.concatenate([w0[:, :SC_COL0], w0[:, SC_COL0 + E_G:], w0[:, SC_COL0:SC_COL0 + E_G],
                            jnp.zeros((dm, _pad_cols(E_IN) - E_IN), F32)], axis=1).astype(BF16)
    proj = norm_matmul(x, norm_w[0, 0], w_in)
    q_bf, rows, win, rows_bf, win_bf = nsa_rope(proj, pos_rows)
    kvc = nsa_compress(rows, bp, tp, e_cmp_pe[0], e_cmp_w1[0], e_cmp_w2[0])
    o_nsa_p = nsa_prompt(q_bf, kvc, rows_bf, win_bf, proj, bp, tp)
    p_kv = rows[:n_p].reshape(bp, tp, 4, g, d)
    s_kv = rows[n_p:].reshape(bs, ts, 4, g, d)
    p_win = win[:n_p].reshape(bp, tp, 2, g, d)[:, tp - min(WINDOW, tp):]
    win_s = win[n_p:].reshape(bs, ts, 2, g, d)
    s_win = jnp.concatenate([cache_nsa_win[0], win_s], axis=1)[:, ts:]
    cache = cache_nsa_kv[0].reshape(cache_nsa_kv.shape[1], PAGE_SIZE, 4 * g * d)
    q_s = q_bf[n_p:].reshape(bs, ts, g, r, d).transpose(0, 2, 3, 1, 4).reshape(bs, g, r * ts, d)
    a0, a1 = nsa_sample_compress(cache, page_table, e_cmp_pe[0], e_cmp_w1[0])
    o_cmp_s, sel_s = nsa_sample_select(a0, a1, e_cmp_w2[0], q_s, past_len, ts)
    o_nsa_s = nsa_sample_attend(cache, page_table, q_s, sel_s, o_cmp_s, rows, win,
                                cache_nsa_win[0].reshape(bs, -1, 2 * g * d), proj, past_len, ts, n_p)
    o_nsa_s = o_nsa_s.reshape(bs, g, r, ts, d).transpose(0, 3, 1, 2, 4).reshape(n_s, E_Q)
    o_nsa = jnp.concatenate([o_nsa_p, o_nsa_s.astype(BF16)], axis=0)
    o_sc_p, p_sc = short_conv_prompt(proj, bp, tp, e_sc_conv[0])
    o_sc_s, s_sc = _short_conv_jnp(proj[n_p:, SC_COL0:GATE_COL0], bs, ts, state_sc_conv[0], e_sc_conv[0])
    o_sc = jnp.concatenate([o_sc_p, o_sc_s.astype(BF16)], axis=0)
    x = matmul2_res(o_nsa, o_sc, e_w_out[0].astype(BF16), x)
    x = peer_layer(x, norm_w[0, 1], peer_wq[0], peer_keys[0], peer_u[0], peer_v[0])

    w_in = jnp.pad(o_w_in[0], ((0, 0), (0, _pad_cols(O_IN) - O_IN))).astype(BF16)
    proj = norm_matmul(x, norm_w[1, 0], w_in)
    odd_w = (o_m2_conv_w[0], o_m2_conv_b[0], o_m2_dt_bias[0], o_m2_a_log[0], o_m2_d[0], o_m2_norm[0])
    o_ret_p, p_ret = retention(proj, pos_p, bp, tp, CHUNK, jnp.zeros((bp, RET_HEADS, RET_DQK, RET_DV), F32),
                               o_ret_gn[0])
    o_ssd_p, p_ssm = ssd(proj, bp, tp, CHUNK, jnp.zeros((bp, M2_CONV - 1, M2_CONV_DIM), F32),
                         jnp.zeros((bp, M2_HEADS, M2_HEADDIM, M2_STATE), F32), *odd_w)
    xbc_cols = slice(O_SPLITS[4], O_SPLITS[5])
    p_m2c = proj[:n_p, xbc_cols].reshape(bp, tp, M2_CONV_DIM)[:, tp - (M2_CONV - 1):]
    xbc_s = proj[n_p:, xbc_cols].reshape(bs, ts, M2_CONV_DIM)
    s_m2c = jnp.concatenate([state_m2_conv[0], xbc_s], axis=1)[:, ts:]
    o_ret_s, o_ssd_s, s_ret, s_ssm = _odd_mixer_jnp(proj[n_p:, :O_IN].reshape(bs, ts, O_IN), pos_s, state_ret[0],
                                                    state_ssm[0], state_m2_conv[0], o_ret_gn[0], *odd_w)
    o_ret = jnp.concatenate([o_ret_p, o_ret_s.astype(BF16)], axis=0)
    o_ssd = jnp.concatenate([o_ssd_p, o_ssd_s.astype(BF16)], axis=0)
    x = matmul2_res(o_ret, o_ssd, o_w_out[0].astype(BF16), x)
    x = peer_layer(x, norm_w[1, 1], peer_wq[1], peer_keys[1], peer_u[1], peer_v[1])

    y = rmsnorm(x, final_norm)
    y_prompt = y[:n_p].reshape(bp, tp, dm)
    y_sample = y[n_p:].reshape(bs, ts, dm)
    return (y_prompt, y_sample, p_kv[None], p_win[None], p_sc[None], p_ret[None], p_ssm[None], p_m2c[None],
            s_kv[None], s_win[None], s_sc[None], s_ret[None], s_ssm[None], s_m2c[None])
```

```python
import functools
import math

import jax
import jax.numpy as jnp
from jax import lax
from jax.experimental import pallas as pl
from jax.experimental.pallas import tpu as pltpu

F32 = jnp.float32
BF16 = jnp.bfloat16

D_MODEL = 2048
DEPTH = 2
PAGE_SIZE = 128
NSA_HEAD_DIM = 128
NSA_HEADS = 8
NSA_KV_HEADS = 2
NSA_GROUP = 4
CMP_BLOCK = 32
CMP_STRIDE = 16
SLC_BLOCK = 64
SLC_TOPK = 16
WINDOW = 512
ROPE_THETA = 500000.0
ROPE_DIMS = 32
SC_WIDTH = 1024
SC_KSIZE = 3
RET_HEADS = 4
RET_DQK = 256
RET_DV = 256
RET_THETA = 10000.0
M2_DINNER = 1024
M2_HEADDIM = 64
M2_HEADS = 16
M2_STATE = 128
M2_GROUPS = 2
M2_CONV = 4
M2_CONV_DIM = M2_DINNER + 2 * M2_GROUPS * M2_STATE
PEER_HEADS = 8
PEER_KEYS = 128
PEER_QDIM = 256
PEER_TOPK = 16
Q_BLOCK = 128
CHUNK = 128
EPS = 1e-6

E_Q = NSA_HEADS * NSA_HEAD_DIM
E_KV = 6 * NSA_KV_HEADS * NSA_HEAD_DIM
E_G = 3 * NSA_HEADS
E_SC = 3 * SC_WIDTH
E_IN = E_Q + E_KV + E_G + E_SC
R_QK = RET_HEADS * RET_DQK
R_V = RET_HEADS * RET_DV
O_SPLITS = [R_QK, 2 * R_QK, 2 * R_QK + R_V, 2 * R_QK + 2 * R_V,
            2 * R_QK + 2 * R_V + M2_DINNER, 2 * R_QK + 2 * R_V + M2_DINNER + M2_CONV_DIM]
O_IN = O_SPLITS[-1] + M2_HEADS
SC_COL0 = E_Q + E_KV
GATE_COL0 = SC_COL0 + E_SC

LANES = 128
VMEM_LIMIT = 56 * 1024 * 1024
NEG_INF = float("-inf")


def _round_up(n, m):
    return -(-n // m) * m


def _pick_tile(n, cap):
    best = LANES
    for t in range(LANES, cap + 1, LANES):
        if n % t == 0:
            best = t
    return best


def _pad_cols(m):
    return min((_round_up(m, t) for t in (768, 640, 512)))


def _row_tile(n, cap):
    best = None
    for t in range(16, cap + 1, 16):
        if n % t == 0:
            best = t
    assert best is not None
    return best


def _gelu_tanh(x):
    return 0.5 * x * (1.0 + jnp.tanh(math.sqrt(2.0 / math.pi) * (x + 0.044715 * (x * x * x))))


def _norm_matmul_kernel(x_ref, g_ref, w_ref, o_ref, xn_ref):
    @pl.when(pl.program_id(1) == 0)
    def _():
        x = x_ref[...]
        ms = jnp.mean(x * x, axis=-1, keepdims=True)
        xn_ref[...] = (x * lax.rsqrt(ms + EPS) * g_ref[...]).astype(BF16)

    o_ref[...] = jnp.dot(xn_ref[...], w_ref[...], preferred_element_type=F32).reshape(o_ref.shape)


def norm_matmul(x, gain, w_bf):
    n, k = x.shape
    m = w_bf.shape[1]
    tm = _row_tile(n, 1040)
    tn = _pick_tile(m, 768)
    return pl.pallas_call(
        _norm_matmul_kernel,
        grid=(n // tm, m // tn),
        in_specs=[pl.BlockSpec((tm, k), lambda i, j: (i, 0)),
                  pl.BlockSpec((1, k), lambda i, j: (0, 0)),
                  pl.BlockSpec((k, tn), lambda i, j: (0, j))],
        out_specs=pl.BlockSpec((tm, tn), lambda i, j: (i, j)),
        out_shape=jax.ShapeDtypeStruct((n, m), F32),
        scratch_shapes=[pltpu.VMEM((tm, k), BF16)],
        compiler_params=pltpu.CompilerParams(
            dimension_semantics=("parallel", "arbitrary"), vmem_limit_bytes=VMEM_LIMIT),
        name="norm_matmul",
    )(x, gain.reshape(1, k), w_bf)


def _matmul2_res_kernel(a1_ref, a2_ref, w1_ref, w2_ref, r_ref, o_ref):
    o_ref[...] = (r_ref[...] + jnp.dot(a1_ref[...], w1_ref[...], preferred_element_type=F32)
                  + jnp.dot(a2_ref[...], w2_ref[...], preferred_element_type=F32))


def matmul2_res(a1, a2, w_bf, res):
    n, k1 = a1.shape
    k2 = a2.shape[1]
    m = w_bf.shape[1]
    assert k1 == k2
    tm = _row_tile(n, 1040)
    tn = _pick_tile(m, 1024)
    return pl.pallas_call(
        _matmul2_res_kernel,
        grid=(n // tm, m // tn),
        in_specs=[pl.BlockSpec((tm, k1), lambda i, j: (i, 0)),
                  pl.BlockSpec((tm, k2), lambda i, j: (i, 0)),
                  pl.BlockSpec((k1, tn), lambda i, j: (0, j)),
                  pl.BlockSpec((k2, tn), lambda i, j: (1, j)),
                  pl.BlockSpec((tm, tn), lambda i, j: (i, j))],
        out_specs=pl.BlockSpec((tm, tn), lambda i, j: (i, j)),
        out_shape=jax.ShapeDtypeStruct((n, m), F32),
        compiler_params=pltpu.CompilerParams(
            dimension_semantics=("parallel", "parallel"), vmem_limit_bytes=VMEM_LIMIT),
        name="matmul2_res",
    )(a1, a2, w_bf, w_bf, res)


PEER_TOK = 128
PEER_HEAD_UNROLL = 4
PEER_TM = 640
PEER_TA = 8
PEER_TE = PEER_TA * PEER_KEYS


def _top_desc(work, count):
    rows = []
    for _ in range(count):
        m = jnp.max(work, axis=0, keepdims=True)
        rows.append(m)
        work = jnp.where(work == m, NEG_INF, work)
    return rows


def _peer_router_kernel(q_ref, k_ref, cnt_ref, gw_ref, r2_ref, w2_ref):
    def head(h, carry):
        q = q_ref[h]
        scores = []
        for side in range(2):
            qs = q[:, side * PEER_KEYS:(side + 1) * PEER_KEYS]
            qs = qs * lax.rsqrt(jnp.mean(qs * qs, axis=-1, keepdims=True) + EPS)
            scores.append(lax.dot_general(k_ref[h, side], qs, (((1,), (1,)), ((), ())),
                                          preferred_element_type=F32))
        s1, s2 = scores
        v1 = _top_desc(s1, PEER_TOPK + 1)
        v2 = _top_desc(s2, PEER_TOPK + 1)
        v2_lo = jnp.concatenate(v2[:8], axis=0)
        v2_hi = jnp.concatenate(v2[8:16], axis=0)
        row = lax.broadcasted_iota(jnp.int32, v2_lo.shape, 0)
        blocks = [v1[0] + v2_lo, v1[0] + v2_hi, v1[1] + v2_lo]
        for a, lim in ((2, 5), (3, 4), (4, 3), (5, 2), (6, 2), (7, 2)):
            blocks.append(jnp.where(row < lim, v1[a] + v2_lo, NEG_INF))
        blocks.append(jnp.concatenate(v1[8:16], axis=0) + v2[0])
        extra = jnp.where(row == 0, v1[0] + v2[16], jnp.where(row == 1, v1[16] + v2[0], NEG_INF))
        blocks.append(extra)
        cand = jnp.concatenate(blocks, axis=0)
        tops = _top_desc(cand, PEER_TOPK + 1)
        z = jnp.zeros_like(tops[0])
        for r in range(PEER_TOPK):
            z = z + jnp.exp(tops[r] - tops[0])
        tau = 0.5 * (tops[PEER_TOPK - 1] + tops[PEER_TOPK])
        count = jnp.zeros_like(s1)
        rank2 = jnp.zeros_like(s2)
        for r in range(PEER_TOPK):
            count = count + jnp.where(s1 + v2[r] >= tau, 1.0, 0.0)
        for r in range(PEER_TOPK + 1):
            rank2 = rank2 + jnp.where(v2[r] > s2, 1.0, 0.0)
        cnt_ref[h] = count
        gw_ref[h] = jnp.exp(s1 - v1[0]) / z
        r2_ref[h] = rank2.astype(BF16)
        w2_ref[h] = jnp.exp(s2 - v2[0]).astype(BF16)
        return carry

    lax.fori_loop(0, PEER_HEADS, head, 0, unroll=PEER_HEAD_UNROLL)


def peer_router(q_hm, keys):
    n = q_hm.shape[1]
    out = jax.ShapeDtypeStruct((PEER_HEADS, PEER_KEYS, n), F32)
    out_bf = jax.ShapeDtypeStruct((PEER_HEADS, PEER_KEYS, n), BF16)
    spec = pl.BlockSpec((PEER_HEADS, PEER_KEYS, PEER_TOK), lambda i: (0, 0, i))
    return pl.pallas_call(
        _peer_router_kernel,
        grid=(n // PEER_TOK,),
        in_specs=[pl.BlockSpec((PEER_HEADS, PEER_TOK, PEER_QDIM), lambda i: (0, i, 0)),
                  pl.BlockSpec((PEER_HEADS, 2, PEER_KEYS, PEER_QDIM // 2), lambda i: (0, 0, 0, 0))],
        out_specs=[spec, spec, spec, spec],
        out_shape=[out, out, out_bf, out_bf],
        compiler_params=pltpu.CompilerParams(dimension_semantics=("parallel",), vmem_limit_bytes=VMEM_LIMIT),
        name="peer_router",
    )(q_hm, keys)


def _peer_expert_kernel(xt_ref, u_ref, vt_ref, cnt_ref, gw_ref, r2_ref, w2_ref, o_ref, s_scr, hg_scr, acc_scr):
    e = pl.program_id(1)
    s_scr[...] = jnp.dot(u_ref[...], xt_ref[...], preferred_element_type=F32)
    zero = jnp.zeros((), BF16)
    for al in range(PEER_TA):
        rows = slice(al * PEER_KEYS, (al + 1) * PEER_KEYS)
        g = None
        for h in range(PEER_HEADS):
            cnt = cnt_ref[h, al:al + 1, :].astype(BF16)
            gate = gw_ref[h, al:al + 1, :].astype(BF16)
            t = jnp.where(r2_ref[h] < cnt, w2_ref[h], zero) * gate
            g = t if g is None else g + t
        hg_scr[rows, :] = _gelu_tanh(s_scr[rows, :]).astype(BF16) * g
    part = jnp.dot(vt_ref[...], hg_scr[...], preferred_element_type=F32)

    @pl.when(e == 0)
    def _():
        acc_scr[...] = part

    @pl.when(e != 0)
    def _():
        acc_scr[...] += part

    @pl.when(e == pl.num_programs(1) - 1)
    def _():
        o_ref[...] = acc_scr[...].T


def peer_experts(xt_bf, u_bf, vt_bf, cnt, gw, r2, w2):
    d, n = xt_bf.shape
    n_exp = u_bf.shape[0]
    tm = PEER_TM
    sel_spec = pl.BlockSpec((PEER_HEADS, PEER_TA, tm), lambda i, e: (0, e, i))
    all_spec = pl.BlockSpec((PEER_HEADS, PEER_KEYS, tm), lambda i, e: (0, 0, i))
    return pl.pallas_call(
        _peer_expert_kernel,
        grid=(n // tm, n_exp // PEER_TE),
        in_specs=[pl.BlockSpec((d, tm), lambda i, e: (0, i)),
                  pl.BlockSpec((PEER_TE, d), lambda i, e: (e, 0)),
                  pl.BlockSpec((d, PEER_TE), lambda i, e: (0, e)),
                  sel_spec, sel_spec, all_spec, all_spec],
        out_specs=pl.BlockSpec((tm, d), lambda i, e: (i, 0)),
        out_shape=jax.ShapeDtypeStruct((n, d), F32),
        scratch_shapes=[pltpu.VMEM((PEER_TE, tm), F32), pltpu.VMEM((PEER_TE, tm), BF16),
                        pltpu.VMEM((d, tm), F32)],
        compiler_params=pltpu.CompilerParams(
            dimension_semantics=("parallel", "arbitrary"), vmem_limit_bytes=VMEM_LIMIT),
        name="peer_experts",
    )(xt_bf, u_bf, vt_bf, cnt, gw, r2, w2)


def _norm_matmul_t_kernel(x_ref, g_ref, w_ref, o_ref, xt_ref, xn_ref):
    @pl.when(pl.program_id(1) == 0)
    def _():
        x = x_ref[...]
        ms = jnp.mean(x * x, axis=-1, keepdims=True)
        xn = x * lax.rsqrt(ms + EPS) * g_ref[...]
        xn_ref[...] = xn.astype(BF16)
        xt_ref[...] = xn.T.astype(BF16)

    o_ref[...] = jnp.dot(xn_ref[...], w_ref[...], preferred_element_type=F32).reshape(o_ref.shape)


def norm_matmul_t(x, gain, w_bf, cols):
    n, k = x.shape
    m = w_bf.shape[1]
    tm = PEER_TM
    return pl.pallas_call(
        _norm_matmul_t_kernel,
        grid=(n // tm, m // cols),
        in_specs=[pl.BlockSpec((tm, k), lambda i, j: (i, 0)),
                  pl.BlockSpec((1, k), lambda i, j: (0, 0)),
                  pl.BlockSpec((k, cols), lambda i, j: (0, j))],
        out_specs=[pl.BlockSpec((1, tm, cols), lambda i, j: (j, i, 0)),
                   pl.BlockSpec((k, tm), lambda i, j: (0, i))],
        out_shape=[jax.ShapeDtypeStruct((m // cols, n, cols), F32), jax.ShapeDtypeStruct((k, n), BF16)],
        scratch_shapes=[pltpu.VMEM((tm, k), BF16)],
        compiler_params=pltpu.CompilerParams(
            dimension_semantics=("parallel", "arbitrary"), vmem_limit_bytes=VMEM_LIMIT),
        name="norm_matmul_t",
    )(x, gain.reshape(1, k), w_bf)


def _rmsnorm_kernel(x_ref, g_ref, o_ref):
    x = x_ref[...]
    ms = jnp.mean(x * x, axis=-1, keepdims=True)
    o_ref[...] = (x * lax.rsqrt(ms + EPS) * g_ref[...]).astype(o_ref.dtype)


def rmsnorm(x, gain, dtype=F32):
    n, k = x.shape
    tm = _row_tile(n, 1040)
    return pl.pallas_call(
        _rmsnorm_kernel,
        grid=(n // tm,),
        in_specs=[pl.BlockSpec((tm, k), lambda i: (i, 0)), pl.BlockSpec((1, k), lambda i: (0, 0))],
        out_specs=pl.BlockSpec((tm, k), lambda i: (i, 0)),
        out_shape=jax.ShapeDtypeStruct((n, k), dtype),
        compiler_params=pltpu.CompilerParams(dimension_semantics=("parallel",), vmem_limit_bytes=VMEM_LIMIT),
        name="rmsnorm",
    )(x, gain.reshape(1, k))


def peer_layer(x, gain, w_q, keys, u, v):
    q_hm, xt_bf = norm_matmul_t(x, gain, w_q.astype(BF16), PEER_QDIM)
    cnt, gw, r2, w2 = peer_router(q_hm, keys)
    return x + peer_experts(xt_bf, u.astype(BF16), v.T.astype(BF16), cnt, gw, r2, w2)


ROPE_TM = 416


def _rope_tables(pos):
    half = ROPE_DIMS // 2
    inv = ROPE_THETA ** (-jnp.arange(half, dtype=F32) / half)
    ang = pos.astype(F32)[:, None] * inv[None, :]
    cos, sin = jnp.cos(ang), jnp.sin(ang)
    t = pos.shape[0]
    ones = jnp.ones((t, NSA_HEAD_DIM - ROPE_DIMS), F32)
    zeros = jnp.zeros((t, NSA_HEAD_DIM - ROPE_DIMS), F32)
    zh = jnp.zeros((t, half), F32)
    c = jnp.concatenate([cos, cos, ones], axis=1)
    s_lo = jnp.concatenate([-sin, zh, zeros], axis=1)
    s_hi = jnp.concatenate([zh, sin, zeros], axis=1)
    return c, s_lo, s_hi


def _rope_kernel(p_ref, c_ref, sl_ref, sh_ref, q_ref, rows_ref, win_ref, rows_bf_ref, win_bf_ref):
    c, sl, sh = c_ref[...], sl_ref[...], sh_ref[...]
    half = ROPE_DIMS // 2

    def rot(x):
        return x * c + pltpu.roll(x, LANES - half, 1) * sl + pltpu.roll(x, half, 1) * sh

    d = NSA_HEAD_DIM
    scale = d ** -0.5
    for hd in range(NSA_HEADS):
        q_ref[:, hd * d:(hd + 1) * d] = (rot(p_ref[:, hd * d:(hd + 1) * d]) * scale).astype(BF16)
    for blk in range(12):
        x = p_ref[:, E_Q + blk * d:E_Q + (blk + 1) * d]
        if (blk // 2) % 2 == 0:
            x = rot(x)
        if blk < 8:
            rows_ref[:, blk * d:(blk + 1) * d] = x
            rows_bf_ref[:, blk * d:(blk + 1) * d] = x.astype(BF16)
        else:
            win_ref[:, (blk - 8) * d:(blk - 7) * d] = x
            win_bf_ref[:, (blk - 8) * d:(blk - 7) * d] = x.astype(BF16)


def nsa_rope(proj, pos_rows):
    n = proj.shape[0]
    tm = ROPE_TM
    c, sl, sh = _rope_tables(pos_rows)
    width = E_Q + E_KV
    tab = pl.BlockSpec((tm, LANES), lambda i: (i, 0))
    return pl.pallas_call(
        _rope_kernel,
        grid=(n // tm,),
        in_specs=[pl.BlockSpec((tm, width), lambda i: (i, 0)), tab, tab, tab],
        out_specs=[pl.BlockSpec((tm, E_Q), lambda i: (i, 0)),
                   pl.BlockSpec((tm, 1024), lambda i: (i, 0)),
                   pl.BlockSpec((tm, 512), lambda i: (i, 0)),
                   pl.BlockSpec((tm, 1024), lambda i: (i, 0)),
                   pl.BlockSpec((tm, 512), lambda i: (i, 0))],
        out_shape=[jax.ShapeDtypeStruct((n, E_Q), BF16),
                   jax.ShapeDtypeStruct((n, 1024), F32),
                   jax.ShapeDtypeStruct((n, 512), F32),
                   jax.ShapeDtypeStruct((n, 1024), BF16),
                   jax.ShapeDtypeStruct((n, 512), BF16)],
        compiler_params=pltpu.CompilerParams(dimension_semantics=("parallel",), vmem_limit_bytes=VMEM_LIMIT),
        name="nsa_rope",
    )(proj, c, sl, sh)


def _compress_kernel(x_ref, pe_ref, w1_ref, w2_ref, o_ref):
    n_sub = x_ref.shape[0] // CMP_STRIDE
    acc0 = jnp.zeros((n_sub, NSA_HEAD_DIM), F32)
    acc1 = jnp.zeros((n_sub, NSA_HEAD_DIM), F32)
    for s in range(CMP_STRIDE):
        xs = x_ref[pl.ds(s, n_sub, stride=CMP_STRIDE), :]
        a0 = (xs + pe_ref[0, s:s + 1, :]).astype(BF16)
        a1 = (xs + pe_ref[0, CMP_STRIDE + s:CMP_STRIDE + s + 1, :]).astype(BF16)
        acc0 = acc0 + jnp.dot(a0, w1_ref[0, s], preferred_element_type=F32)
        acc1 = acc1 + jnp.dot(a1, w1_ref[0, CMP_STRIDE + s], preferred_element_type=F32)
    pre = acc0 + pltpu.roll(acc1, n_sub - 1, 0)
    o_ref[0, 0] = jnp.dot(_gelu_tanh(pre).astype(BF16), w2_ref[0], preferred_element_type=F32).astype(BF16)


def nsa_compress(rows, n_batch, t_len, cmp_pe, cmp_w1, cmp_w2):
    n_sub = t_len // CMP_STRIDE
    d = NSA_HEAD_DIM
    w1 = cmp_w1.reshape(2, CMP_BLOCK, d, d).astype(BF16)
    return pl.pallas_call(
        _compress_kernel,
        grid=(n_batch, 4),
        in_specs=[pl.BlockSpec((t_len, d), lambda b, c: (b, c)),
                  pl.BlockSpec((1, CMP_BLOCK, d), lambda b, c: (c // 2, 0, 0)),
                  pl.BlockSpec((1, CMP_BLOCK, d, d), lambda b, c: (c // 2, 0, 0, 0)),
                  pl.BlockSpec((1, d, d), lambda b, c: (c // 2, 0, 0))],
        out_specs=pl.BlockSpec((1, 1, n_sub, d), lambda b, c: (b, c, 0, 0)),
        out_shape=jax.ShapeDtypeStruct((n_batch, 4, n_sub, d), BF16),
        compiler_params=pltpu.CompilerParams(
            dimension_semantics=("parallel", "parallel"), vmem_limit_bytes=VMEM_LIMIT),
        name="nsa_compress",
    )(rows, cmp_pe, w1, cmp_w2.astype(BF16))


NSA_TK = 512
NSA_WTILES = WINDOW // Q_BLOCK + 1


def _masked_softmax_rows(s, mask):
    s = jnp.where(mask, s, -1e30)
    p = jnp.exp(s - jnp.max(s, axis=-1, keepdims=True))
    p = jnp.where(mask, p, 0.0)
    return p / jnp.maximum(jnp.sum(p, axis=-1, keepdims=True), 1e-30)


def _nsa_prompt_kernel(q_ref, kc_ref, vc_ref, ks_ref, vs_ref, kw_ref, vw_ref, gate_ref, o_ref):
    g = pl.program_id(1)
    qi = pl.program_id(2)
    d, r, qb = NSA_HEAD_DIM, NSA_GROUP, Q_BLOCK
    n_cmp = kc_ref.shape[2]
    n_slc = ks_ref.shape[0] // SLC_BLOCK
    nt = (((1,), (1,)), ((), ()))
    q = jnp.concatenate([q_ref[:, h * d:(h + 1) * d] for h in range(r)], axis=0)
    t_col = qi * qb + lax.broadcasted_iota(jnp.int32, (qb, 1), 0)

    s = lax.dot_general(q, kc_ref[0, 0], nt, preferred_element_type=F32).reshape(r, qb, n_cmp)
    n_idx = lax.broadcasted_iota(jnp.int32, (qb, n_cmp), 1)
    c_mask = (n_idx * CMP_STRIDE + (CMP_BLOCK - 1) <= t_col)[None]
    p_c = _masked_softmax_rows(s, c_mask)
    o_cmp = jnp.dot(p_c.reshape(r * qb, n_cmp).astype(BF16), vc_ref[0, 0], preferred_element_type=F32)

    nb = LANES
    assert n_slc <= nb and qb == LANES
    p_sum = jnp.sum(p_c, axis=0)
    si = lax.broadcasted_iota(jnp.int32, (nb, n_cmp), 0) * SLC_BLOCK
    ci = lax.broadcasted_iota(jnp.int32, (nb, n_cmp), 1) * CMP_STRIDE
    overlap_t = jnp.where((ci < si + SLC_BLOCK) & (ci + CMP_BLOCK > si), 1.0, 0.0).astype(BF16)
    p_hi = p_sum.astype(BF16)
    p_lo = (p_sum - p_hi.astype(F32)).astype(BF16)
    imp_t = (lax.dot_general(overlap_t, p_hi, nt, preferred_element_type=F32)
             + lax.dot_general(overlap_t, p_lo, nt, preferred_element_type=F32))
    blk = lax.broadcasted_iota(jnp.int32, (nb, qb), 0)
    blk_f = blk.astype(F32)
    t_row = qi * qb + lax.broadcasted_iota(jnp.int32, (1, qb), 1)
    cur = t_row // SLC_BLOCK
    forced = (blk == 0) | (blk == cur) | (blk == cur - 1)
    work = jnp.where(blk * SLC_BLOCK <= t_row, jnp.where(forced, 1e6, imp_t), -1e6)
    work = jnp.where(blk < n_slc, work, NEG_INF)
    sel_t = jnp.zeros((nb, qb), F32)
    for _ in range(min(SLC_TOPK, n_slc)):
        m = jnp.max(work, axis=0, keepdims=True)
        first = jnp.min(jnp.where(work == m, blk_f, float(nb)), axis=0, keepdims=True)
        pick = blk_f == first
        sel_t = jnp.where(pick, 1.0, sel_t)
        work = jnp.where(pick, NEG_INF, work)
    sel_bf = sel_t.T.astype(BF16)

    tk = NSA_TK
    bpt = tk // SLC_BLOCK

    def slc_step(kt, carry):
        m_run, l_run, acc = carry
        start = pl.multiple_of(kt * tk, tk)
        k = ks_ref[pl.ds(start, tk), :]
        v = vs_ref[pl.ds(start, tk), :]
        s = lax.dot_general(q, k, nt, preferred_element_type=F32).reshape(r, qb, tk)
        ei = lax.broadcasted_iota(jnp.int32, (nb, tk), 0)
        ej = lax.broadcasted_iota(jnp.int32, (nb, tk), 1)
        expand = jnp.where(ei == kt * bpt + ej // SLC_BLOCK, 1.0, 0.0).astype(BF16)
        picked = jnp.dot(sel_bf, expand, preferred_element_type=F32)
        kpos = start + lax.broadcasted_iota(jnp.int32, (qb, tk), 1)
        mask = ((picked > 0.5) & (kpos <= t_col))[None]
        s = jnp.where(mask, s, -1e30)
        m_new = jnp.maximum(m_run, jnp.max(s, axis=-1, keepdims=True))
        alpha = jnp.exp(m_run - m_new)
        p = jnp.where(mask, jnp.exp(s - m_new), 0.0)
        l_new = alpha * l_run + jnp.sum(p, axis=-1, keepdims=True)
        pv = jnp.dot(p.reshape(r * qb, tk).astype(BF16), v, preferred_element_type=F32)
        acc = alpha.reshape(r * qb, 1) * acc + pv
        return m_new, l_new, acc

    init = (jnp.full((r, qb, 1), -1e30, F32), jnp.zeros((r, qb, 1), F32), jnp.zeros((r * qb, d), F32))
    _, l_fin, acc = lax.fori_loop(0, (qi * qb) // tk + 1, slc_step, init)
    o_slc = acc / jnp.maximum(l_fin.reshape(r * qb, 1), 1e-30)

    k_tiles, v_tiles, pos_tiles = [], [], []
    for j in range(NSA_WTILES):
        kt = qi - (NSA_WTILES - 1) + j
        ktc = jnp.maximum(kt, 0)
        start = pl.multiple_of(ktc * qb, qb)
        k_tiles.append(kw_ref[pl.ds(start, qb), :])
        v_tiles.append(vw_ref[pl.ds(start, qb), :])
        lane = lax.broadcasted_iota(jnp.int32, (qb, qb), 1)
        pos_tiles.append(jnp.where(kt >= 0, start + lane, -1))
    k_w = jnp.concatenate(k_tiles, axis=0)
    v_w = jnp.concatenate(v_tiles, axis=0)
    k_pos = jnp.concatenate(pos_tiles, axis=1)
    span = NSA_WTILES * qb
    s = lax.dot_general(q, k_w, nt, preferred_element_type=F32).reshape(r, qb, span)
    dpos = t_col - k_pos
    w_mask = ((dpos >= 0) & (dpos < WINDOW) & (k_pos >= 0))[None]
    p_w = _masked_softmax_rows(s, w_mask)
    o_win = jnp.dot(p_w.reshape(r * qb, span).astype(BF16), v_w, preferred_element_type=F32)

    sig = jax.nn.sigmoid(gate_ref[...])
    lane = lax.broadcasted_iota(jnp.int32, sig.shape, 1)
    for h in range(r):
        rows = slice(h * qb, (h + 1) * qb)
        out = jnp.zeros((qb, d), F32)
        for branch, o_b in enumerate((o_cmp, o_slc, o_win)):
            col = branch * NSA_HEADS + g * r + h
            gate = jnp.sum(jnp.where(lane == col, sig, 0.0), axis=-1, keepdims=True)
            out = out + gate * o_b[rows]
        o_ref[:, h * d:(h + 1) * d] = out.astype(o_ref.dtype)


def nsa_prompt(q_bf, kvc, rows_bf, win_bf, proj, n_batch, t_len):
    d, r, qb = NSA_HEAD_DIM, NSA_GROUP, Q_BLOCK
    nqb = t_len // qb
    n_sub = kvc.shape[2]
    gate_blk = GATE_COL0 // LANES
    seq = lambda col: pl.BlockSpec((t_len, d), lambda b, g, i: (b, col(g)))
    return pl.pallas_call(
        _nsa_prompt_kernel,
        grid=(n_batch, NSA_KV_HEADS, nqb),
        in_specs=[pl.BlockSpec((qb, r * d), lambda b, g, i: (b * nqb + i, g)),
                  pl.BlockSpec((1, 1, n_sub, d), lambda b, g, i: (b, g, 0, 0)),
                  pl.BlockSpec((1, 1, n_sub, d), lambda b, g, i: (b, 2 + g, 0, 0)),
                  seq(lambda g: 4 + g), seq(lambda g: 6 + g),
                  seq(lambda g: g), seq(lambda g: 2 + g),
                  pl.BlockSpec((qb, LANES), lambda b, g, i: (b * nqb + i, gate_blk))],
        out_specs=pl.BlockSpec((qb, r * d), lambda b, g, i: (b * nqb + i, g)),
        out_shape=jax.ShapeDtypeStruct((n_batch * t_len, E_Q), BF16),
        compiler_params=pltpu.CompilerParams(
            dimension_semantics=("parallel", "parallel", "arbitrary"), vmem_limit_bytes=VMEM_LIMIT),
        name="nsa_prompt",
    )(q_bf, kvc, kvc, rows_bf, rows_bf, win_bf, win_bf, proj)


SAMPLE_PAGES = 16
NEW_ROWS_BLK = 16


def _page_specs(kind):
    def spec(i):
        return pl.BlockSpec((1, PAGE_SIZE, 1, NSA_KV_HEADS, NSA_HEAD_DIM),
                            lambda s, c, pt: (pt[s, c * SAMPLE_PAGES + i], 0, kind, 0, 0))
    return [spec(i) for i in range(SAMPLE_PAGES)]


def _sample_compress_kernel(pt_ref, *refs):
    n_in = 2 * SAMPLE_PAGES
    pages = refs[:n_in]
    pe_ref, w1_ref, a0_ref, a1_ref = refs[n_in:]
    d = NSA_HEAD_DIM
    per_page = PAGE_SIZE // CMP_STRIDE
    rows = SAMPLE_PAGES * per_page
    for kv in range(2):
        acc0 = jnp.zeros((NSA_KV_HEADS * rows, d), F32)
        acc1 = jnp.zeros((NSA_KV_HEADS * rows, d), F32)
        for s in range(CMP_STRIDE):
            pieces = []
            for g in range(NSA_KV_HEADS):
                for p in range(SAMPLE_PAGES):
                    pieces.append(pages[kv * SAMPLE_PAGES + p][0, pl.ds(s, per_page, stride=CMP_STRIDE), 0, g, :])
            xs = jnp.concatenate(pieces, axis=0)
            a0 = (xs + pe_ref[kv, s:s + 1, :]).astype(BF16)
            a1 = (xs + pe_ref[kv, CMP_STRIDE + s:CMP_STRIDE + s + 1, :]).astype(BF16)
            acc0 = acc0 + jnp.dot(a0, w1_ref[kv, s], preferred_element_type=F32)
            acc1 = acc1 + jnp.dot(a1, w1_ref[kv, CMP_STRIDE + s], preferred_element_type=F32)
        for g in range(NSA_KV_HEADS):
            a0_ref[0, kv * NSA_KV_HEADS + g] = acc0[g * rows:(g + 1) * rows]
            a1_ref[0, kv * NSA_KV_HEADS + g] = acc1[g * rows:(g + 1) * rows]


def nsa_sample_compress(cache, page_table, cmp_pe, cmp_w1):
    bs, n_pages = page_table.shape
    d = NSA_HEAD_DIM
    per_page = PAGE_SIZE // CMP_STRIDE
    n_sub = n_pages * per_page
    rows = SAMPLE_PAGES * per_page
    w1 = cmp_w1.reshape(2, CMP_BLOCK, d, d).astype(BF16)
    out = jax.ShapeDtypeStruct((bs, 4, n_sub, d), F32)
    ospec = pl.BlockSpec((1, 4, rows, d), lambda s, c, pt: (s, 0, c, 0))
    page_specs = _page_specs(0) + _page_specs(1)
    return pl.pallas_call(
        _sample_compress_kernel,
        grid_spec=pltpu.PrefetchScalarGridSpec(
            num_scalar_prefetch=1,
            grid=(bs, n_pages // SAMPLE_PAGES),
            in_specs=page_specs + [
                pl.BlockSpec((2, CMP_BLOCK, d), lambda s, c, pt: (0, 0, 0)),
                pl.BlockSpec((2, CMP_BLOCK, d, d), lambda s, c, pt: (0, 0, 0, 0))],
            out_specs=[ospec, ospec]),
        out_shape=[out, out],
        compiler_params=pltpu.CompilerParams(
            dimension_semantics=("parallel", "arbitrary"), vmem_limit_bytes=VMEM_LIMIT),
        name="nsa_sample_compress",
    )(page_table, *([cache] * (2 * SAMPLE_PAGES)), cmp_pe, w1)


def _sample_select_kernel(a0_ref, a1_ref, w2_ref, q_ref, ocmp_ref, sel_ref, *, past_len, t_len):
    d, r = NSA_HEAD_DIM, NSA_GROUP
    n_sub = a0_ref.shape[2]
    n_cmp = n_sub - 1
    n_slc = -(-(past_len + t_len) // SLC_BLOCK)
    lanes = sel_ref.shape[3]
    nt = (((1,), (1,)), ((), ()))
    rq = r * t_len
    for g in range(NSA_KV_HEADS):
        kc = jnp.dot(_gelu_tanh(a0_ref[0, g] + pltpu.roll(a1_ref[0, g], n_sub - 1, 0)).astype(BF16), w2_ref[0],
                     preferred_element_type=F32).astype(BF16)
        vc = jnp.dot(_gelu_tanh(a0_ref[0, 2 + g] + pltpu.roll(a1_ref[0, 2 + g], n_sub - 1, 0)).astype(BF16),
                     w2_ref[1], preferred_element_type=F32).astype(BF16)
        q = q_ref[0, g]
        s = lax.dot_general(q, kc, nt, preferred_element_type=F32)
        pos = past_len + lax.broadcasted_iota(jnp.int32, (rq, 1), 0) % t_len
        n_idx = lax.broadcasted_iota(jnp.int32, (rq, n_sub), 1)
        p_c = _masked_softmax_rows(s, (n_idx * CMP_STRIDE + (CMP_BLOCK - 1) <= pos) & (n_idx < n_cmp))
        ocmp_ref[0, g] = jnp.dot(p_c.astype(BF16), vc, preferred_element_type=F32)
        ri = lax.broadcasted_iota(jnp.int32, (8, rq), 0)
        cj = lax.broadcasted_iota(jnp.int32, (8, rq), 1)
        head_sum = jnp.where(cj % t_len == ri, 1.0, 0.0).astype(BF16)
        p_hi = p_c.astype(BF16)
        p_lo = (p_c - p_hi.astype(F32)).astype(BF16)
        p_sum = (jnp.dot(head_sum, p_hi, preferred_element_type=F32)
                 + jnp.dot(head_sum, p_lo, preferred_element_type=F32))
        ci = lax.broadcasted_iota(jnp.int32, (n_sub, lanes), 0)
        mi = lax.broadcasted_iota(jnp.int32, (n_sub, lanes), 1)
        overlap = jnp.where((ci * CMP_STRIDE < mi * SLC_BLOCK + SLC_BLOCK)
                            & (ci * CMP_STRIDE + CMP_BLOCK > mi * SLC_BLOCK) & (ci < n_cmp), 1.0, 0.0).astype(BF16)
        s_hi = p_sum.astype(BF16)
        s_lo = (p_sum - s_hi.astype(F32)).astype(BF16)
        imp = (jnp.dot(s_hi, overlap, preferred_element_type=F32)
               + jnp.dot(s_lo, overlap, preferred_element_type=F32))
        blk = lax.broadcasted_iota(jnp.int32, (8, lanes), 1)
        blk_f = blk.astype(F32)
        tpos = past_len + lax.broadcasted_iota(jnp.int32, (8, 1), 0) % t_len
        cur = tpos // SLC_BLOCK
        forced = (blk == 0) | (blk == cur) | (blk == cur - 1)
        work = jnp.where(blk * SLC_BLOCK <= tpos, jnp.where(forced, 1e6, imp), -1e6)
        work = jnp.where(blk < n_slc, work, NEG_INF)
        sel = jnp.zeros((8, lanes), F32)
        for _ in range(min(SLC_TOPK, n_slc)):
            m = jnp.max(work, axis=-1, keepdims=True)
            first = jnp.min(jnp.where(work == m, blk_f, float(lanes)), axis=-1, keepdims=True)
            pick = blk_f == first
            sel = jnp.where(pick, 1.0, sel)
            work = jnp.where(pick, NEG_INF, work)
        sel_ref[0, g] = sel


def nsa_sample_select(a0, a1, cmp_w2, q_s, past_len, t_len):
    bs, _, n_sub, d = a0.shape
    n_slc = -(-(past_len + t_len) // SLC_BLOCK)
    lanes = _round_up(n_slc, LANES)
    rq = NSA_GROUP * t_len
    aspec = pl.BlockSpec((1, 4, n_sub, d), lambda s: (s, 0, 0, 0))
    return pl.pallas_call(
        functools.partial(_sample_select_kernel, past_len=past_len, t_len=t_len),
        grid=(bs,),
        in_specs=[aspec, aspec, pl.BlockSpec((2, d, d), lambda s: (0, 0, 0)),
                  pl.BlockSpec((1, NSA_KV_HEADS, rq, d), lambda s: (s, 0, 0, 0))],
        out_specs=[pl.BlockSpec((1, NSA_KV_HEADS, rq, d), lambda s: (s, 0, 0, 0)),
                   pl.BlockSpec((1, NSA_KV_HEADS, 8, lanes), lambda s: (s, 0, 0, 0))],
        out_shape=[jax.ShapeDtypeStruct((bs, NSA_KV_HEADS, rq, d), F32),
                   jax.ShapeDtypeStruct((bs, NSA_KV_HEADS, 8, lanes), F32)],
        compiler_params=pltpu.CompilerParams(dimension_semantics=("parallel",), vmem_limit_bytes=VMEM_LIMIT),
        name="nsa_sample_select",
    )(a0, a1, cmp_w2.astype(BF16), q_s)


def _sample_attend_kernel(pt_ref, *refs, past_len, t_len, row0):
    k_pages = refs[:SAMPLE_PAGES]
    v_pages = refs[SAMPLE_PAGES:2 * SAMPLE_PAGES]
    (q_ref, sel_ref, ocmp_ref, newrows_ref, kwin_ref, vwin_ref, newwin_ref, gate_ref, o_ref,
     m_scr, l_scr, acc_scr) = refs[2 * SAMPLE_PAGES:]
    s_idx = pl.program_id(0)
    c = pl.program_id(1)
    d, r = NSA_HEAD_DIM, NSA_GROUP
    rq = r * t_len
    lanes = sel_ref.shape[3]
    tk = SAMPLE_PAGES * PAGE_SIZE
    nt = (((1,), (1,)), ((), ()))

    @pl.when(c == 0)
    def _():
        m_scr[...] = jnp.full(m_scr.shape, -1e30, F32)
        l_scr[...] = jnp.zeros(l_scr.shape, F32)
        acc_scr[...] = jnp.zeros(acc_scr.shape, F32)

    row = lax.broadcasted_iota(jnp.int32, (rq, 1), 0)
    t_row = row % t_len
    pos = past_len + t_row
    ti = lax.broadcasted_iota(jnp.int32, (rq, 8), 1)
    tok_expand = jnp.where(ti == t_row, 1.0, 0.0).astype(BF16)

    def online(g, s, mask, v):
        s = jnp.where(mask, s, -1e30)
        m_old = m_scr[g]
        m_new = jnp.maximum(m_old, jnp.max(s, axis=-1, keepdims=True))
        alpha = jnp.exp(m_old - m_new)
        p = jnp.where(mask, jnp.exp(s - m_new), 0.0)
        l_scr[g] = alpha * l_scr[g] + jnp.sum(p, axis=-1, keepdims=True)
        acc_scr[g] = alpha * acc_scr[g] + jnp.dot(p.astype(BF16), v, preferred_element_type=F32)
        m_scr[g] = m_new

    sel16 = []
    for g in range(NSA_KV_HEADS):
        sel16.append(jnp.dot(tok_expand, sel_ref[0, g].astype(BF16), preferred_element_type=F32))
        k = jnp.concatenate([pg[0, :, 0, g, :] for pg in k_pages], axis=0)
        v = jnp.concatenate([pg[0, :, 0, g, :] for pg in v_pages], axis=0)
        q = q_ref[0, g]
        s = lax.dot_general(q, k.astype(BF16), nt, preferred_element_type=F32)
        ei = lax.broadcasted_iota(jnp.int32, (lanes, tk), 0)
        ej = lax.broadcasted_iota(jnp.int32, (lanes, tk), 1)
        expand = jnp.where(ei == c * (tk // SLC_BLOCK) + ej // SLC_BLOCK, 1.0, 0.0).astype(BF16)
        picked = jnp.dot(sel16[g].astype(BF16), expand, preferred_element_type=F32)
        kpos = c * tk + lax.broadcasted_iota(jnp.int32, (rq, tk), 1)
        online(g, s, (picked > 0.5) & (kpos <= pos), v.astype(BF16))

    @pl.when(c == pl.num_programs(1) - 1)
    def _():
        mine = (row0 // t_len + s_idx) % (NEW_ROWS_BLK // t_len)
        j = lax.broadcasted_iota(jnp.int32, (rq, NEW_ROWS_BLK), 1)
        own = j // t_len == mine
        new_pos = past_len + j % t_len
        sig = jax.nn.sigmoid(gate_ref[...])
        gsel = jnp.where(j == mine * t_len + t_row, 1.0, 0.0).astype(BF16)
        s_hi = sig.astype(BF16)
        s_lo = (sig - s_hi.astype(F32)).astype(BF16)
        sig_rows = (jnp.dot(gsel, s_hi, preferred_element_type=F32)
                    + jnp.dot(gsel, s_lo, preferred_element_type=F32))
        lane = lax.broadcasted_iota(jnp.int32, sig_rows.shape, 1)
        lane_l = lax.broadcasted_iota(jnp.int32, (rq, lanes), 1)
        for g in range(NSA_KV_HEADS):
            q = q_ref[0, g]
            kn = newrows_ref[:, (4 + g) * d:(5 + g) * d].astype(BF16)
            vn = newrows_ref[:, (6 + g) * d:(7 + g) * d].astype(BF16)
            s = lax.dot_general(q, kn, nt, preferred_element_type=F32)
            last_picked = jnp.sum(jnp.where(lane_l == past_len // SLC_BLOCK, sel16[g], 0.0), axis=-1, keepdims=True)
            online(g, s, own & (new_pos <= pos) & (last_picked > 0.5), vn)
            o_slc = acc_scr[g] / jnp.maximum(l_scr[g], 1e-30)
            w_len = kwin_ref.shape[1]
            kw = kwin_ref[0, :, 0, g, :].astype(BF16)
            vw = vwin_ref[0, :, 0, g, :].astype(BF16)
            s1 = lax.dot_general(q, kw, nt, preferred_element_type=F32)
            kp1 = past_len - w_len + lax.broadcasted_iota(jnp.int32, (rq, w_len), 1)
            d1 = pos - kp1
            m1 = (d1 >= 0) & (d1 < WINDOW) & (kp1 >= 0)
            knw = newwin_ref[:, g * d:(g + 1) * d].astype(BF16)
            vnw = newwin_ref[:, (2 + g) * d:(3 + g) * d].astype(BF16)
            s2 = lax.dot_general(q, knw, nt, preferred_element_type=F32)
            d2 = pos - new_pos
            m2 = own & (d2 >= 0) & (d2 < WINDOW)
            s1 = jnp.where(m1, s1, -1e30)
            s2 = jnp.where(m2, s2, -1e30)
            mx = jnp.maximum(jnp.max(s1, axis=-1, keepdims=True), jnp.max(s2, axis=-1, keepdims=True))
            p1 = jnp.where(m1, jnp.exp(s1 - mx), 0.0)
            p2 = jnp.where(m2, jnp.exp(s2 - mx), 0.0)
            den = jnp.sum(p1, axis=-1, keepdims=True) + jnp.sum(p2, axis=-1, keepdims=True)
            o_win = (jnp.dot(p1.astype(BF16), vw, preferred_element_type=F32)
                     + jnp.dot(p2.astype(BF16), vnw, preferred_element_type=F32)) / jnp.maximum(den, 1e-30)
            out = jnp.zeros((rq, d), F32)
            for branch, o_b in enumerate((ocmp_ref[0, g], o_slc, o_win)):
                col = branch * NSA_HEADS + g * r + row // t_len
                gate = jnp.sum(jnp.where(lane == col, sig_rows, 0.0), axis=-1, keepdims=True)
                out = out + gate * o_b
            o_ref[0, g] = out


def nsa_sample_attend(cache, page_table, q_s, sel, o_cmp, rows, win, win_cache, proj, past_len, t_len, row0):
    bs, n_pages = page_table.shape
    d = NSA_HEAD_DIM
    rq = NSA_GROUP * t_len
    lanes = sel.shape[3]
    w_len = win_cache.shape[1]
    assert row0 % t_len == 0 and NEW_ROWS_BLK % t_len == 0 and past_len % SLC_BLOCK == 0
    blk = lambda s: (row0 + s * t_len) // NEW_ROWS_BLK
    per_seq = lambda shape: pl.BlockSpec((1,) + shape, lambda s, c, pt: (s, 0, 0, 0))
    return pl.pallas_call(
        functools.partial(_sample_attend_kernel, past_len=past_len, t_len=t_len, row0=row0),
        grid_spec=pltpu.PrefetchScalarGridSpec(
            num_scalar_prefetch=1,
            grid=(bs, n_pages // SAMPLE_PAGES),
            in_specs=_page_specs(2) + _page_specs(3) + [
                per_seq((NSA_KV_HEADS, rq, d)), per_seq((NSA_KV_HEADS, 8, lanes)), per_seq((NSA_KV_HEADS, rq, d)),
                pl.BlockSpec((NEW_ROWS_BLK, 8 * d), lambda s, c, pt: (blk(s), 0)),
                pl.BlockSpec((1, w_len, 1, NSA_KV_HEADS, d), lambda s, c, pt: (s, 0, 0, 0, 0)),
                pl.BlockSpec((1, w_len, 1, NSA_KV_HEADS, d), lambda s, c, pt: (s, 0, 1, 0, 0)),
                pl.BlockSpec((NEW_ROWS_BLK, 4 * d), lambda s, c, pt: (blk(s), 0)),
                pl.BlockSpec((NEW_ROWS_BLK, LANES), lambda s, c, pt: (blk(s), GATE_COL0 // LANES))],
            out_specs=per_seq((NSA_KV_HEADS, rq, d)),
            scratch_shapes=[pltpu.VMEM((NSA_KV_HEADS, rq, 1), F32), pltpu.VMEM((NSA_KV_HEADS, rq, 1), F32),
                            pltpu.VMEM((NSA_KV_HEADS, rq, d), F32)]),
        out_shape=jax.ShapeDtypeStruct((bs, NSA_KV_HEADS, rq, d), F32),
        compiler_params=pltpu.CompilerParams(
            dimension_semantics=("parallel", "arbitrary"), vmem_limit_bytes=VMEM_LIMIT),
        name="nsa_sample_attend",
    )(page_table, *([cache] * (2 * SAMPLE_PAGES)), q_s, sel, o_cmp, rows, win_cache, win_cache, win, proj)


SC_ROWS = 512
SC_COLS = 512


def _short_conv_kernel(b_ref, c_ref, h_ref, w_ref, o_ref, tail_ref, prev_scr):
    @pl.when(pl.program_id(2) == 0)
    def _():
        prev_scr[...] = jnp.zeros_like(prev_scr)

    rows = c_ref.shape[0]
    u = c_ref[...] * h_ref[...]
    prev = prev_scr[...]
    row8 = lax.broadcasted_iota(jnp.int32, prev.shape, 0)
    conv = u * w_ref[SC_KSIZE - 1:SC_KSIZE, :]
    for k in range(1, SC_KSIZE):
        rolled = pltpu.roll(u, k, 0)
        top = jnp.where(row8 < k, pltpu.roll(prev, k, 0), rolled[0:8])
        shifted = jnp.concatenate([top, rolled[8:]], axis=0)
        conv = conv + shifted * w_ref[SC_KSIZE - 1 - k:SC_KSIZE - k, :]
    prev_scr[...] = u[rows - 8:rows]
    o_ref[...] = (b_ref[...] * conv).astype(o_ref.dtype)
    tail_ref[0] = u[rows - 8:rows]


def short_conv_prompt(proj, n_batch, t_len, sc_w):
    nr = t_len // SC_ROWS
    nh = SC_WIDTH // SC_COLS
    blk0 = SC_COL0 // SC_COLS
    assert SC_COL0 % SC_COLS == 0
    col = lambda part: pl.BlockSpec((SC_ROWS, SC_COLS), lambda b, j, i: (b * nr + i, blk0 + part * nh + j))
    out, tail = pl.pallas_call(
        _short_conv_kernel,
        grid=(n_batch, nh, nr),
        in_specs=[col(0), col(1), col(2), pl.BlockSpec((SC_KSIZE, SC_COLS), lambda b, j, i: (0, j))],
        out_specs=[pl.BlockSpec((SC_ROWS, SC_COLS), lambda b, j, i: (b * nr + i, j)),
                   pl.BlockSpec((1, 8, SC_COLS), lambda b, j, i: (b, 0, j))],
        out_shape=[jax.ShapeDtypeStruct((n_batch * t_len, SC_WIDTH), BF16),
                   jax.ShapeDtypeStruct((n_batch, 8, SC_WIDTH), F32)],
        scratch_shapes=[pltpu.VMEM((8, SC_COLS), F32)],
        compiler_params=pltpu.CompilerParams(
            dimension_semantics=("parallel", "parallel", "arbitrary"), vmem_limit_bytes=VMEM_LIMIT),
        name="short_conv",
    )(proj, proj, proj, sc_w)
    return out, tail[:, 8 - (SC_KSIZE - 1):]


def _ret_tables(pos, chunk):
    half = RET_DQK // 2
    inv = RET_THETA ** (-jnp.arange(half, dtype=F32) / half)
    ang = pos.astype(F32)[:, None] * inv[None, :]
    log_g = jnp.log1p(-(2.0 ** (-5.0 - jnp.arange(RET_HEADS, dtype=F32))))
    i = jnp.arange(chunk, dtype=F32)
    diff = i[:, None] - i[None, :]
    intra = jnp.where(diff >= 0, jnp.exp(jnp.maximum(diff, 0.0)[None] * log_g[:, None, None]), 0.0)
    q_dec = jnp.exp((i[None, :] + 1.0) * log_g[:, None])[..., None]
    k_dec = jnp.exp((chunk - 1.0 - i)[None, :] * log_g[:, None])[..., None]
    c_dec = jnp.exp(chunk * log_g)[:, None, None]
    return jnp.cos(ang), jnp.sin(ang), intra, q_dec, k_dec, c_dec


def _retention_kernel(q_ref, k_ref, v_ref, g_ref, cos_ref, sin_ref, intra_ref, qd_ref, kd_ref, cd_ref, gn_ref,
                      s0_ref, o_ref, s_out_ref, s_scr):
    c_idx = pl.program_id(1)

    @pl.when(c_idx == 0)
    def _():
        s_scr[...] = s0_ref[0]

    cos, sin = cos_ref[...], sin_ref[...]
    half = RET_DQK // 2
    nt = (((1,), (1,)), ((), ()))
    tn = (((0,), (0,)), ((), ()))

    def rot(x):
        x1, x2 = x[:, :half], x[:, half:]
        return jnp.concatenate([x1 * cos - x2 * sin, x1 * sin + x2 * cos], axis=-1)

    for h in range(RET_HEADS):
        cols = slice(h * RET_DQK, (h + 1) * RET_DQK)
        qr = (rot(q_ref[:, cols]) * (RET_DQK ** -0.5)).astype(BF16)
        kr = rot(k_ref[:, cols])
        v = v_ref[:, cols].astype(BF16)
        att = lax.dot_general(qr, kr.astype(BF16), nt, preferred_element_type=F32) * intra_ref[h]
        s_old = s_scr[h]
        o = (jnp.dot(att.astype(BF16), v, preferred_element_type=F32)
             + jnp.dot(qr, s_old.astype(BF16), preferred_element_type=F32) * qd_ref[h])
        s_scr[h] = s_old * cd_ref[h] + lax.dot_general((kr * kd_ref[h]).astype(BF16), v, tn,
                                                       preferred_element_type=F32)
        mu = jnp.mean(o, axis=-1, keepdims=True)
        dev = o - mu
        var = jnp.mean(dev * dev, axis=-1, keepdims=True)
        gate = g_ref[:, cols]
        on = dev * lax.rsqrt(var + EPS) * gn_ref[:, cols] * (gate * jax.nn.sigmoid(gate))
        o_ref[:, cols] = on.astype(o_ref.dtype)
    s_out_ref[0] = s_scr[...]


def retention(proj, pos, n_batch, t_len, chunk, state0, gn_gain):
    nc = t_len // chunk
    cos, sin, intra, q_dec, k_dec, c_dec = _ret_tables(pos, chunk)
    half = RET_DQK // 2
    col = lambda j: pl.BlockSpec((chunk, R_QK), lambda b, c: (b * nc + c, j))
    tab = pl.BlockSpec((chunk, half), lambda b, c: (c, 0))
    full = lambda a: pl.BlockSpec(a.shape, lambda b, c: (0,) * a.ndim)
    st = pl.BlockSpec((1, RET_HEADS, RET_DQK, RET_DV), lambda b, c: (b, 0, 0, 0))
    return pl.pallas_call(
        _retention_kernel,
        grid=(n_batch, nc),
        in_specs=[col(0), col(1), col(2), col(3), tab, tab, full(intra), full(q_dec), full(k_dec), full(c_dec),
                  pl.BlockSpec((1, R_V), lambda b, c: (0, 0)), st],
        out_specs=[pl.BlockSpec((chunk, R_V), lambda b, c: (b * nc + c, 0)), st],
        out_shape=[jax.ShapeDtypeStruct((n_batch * t_len, R_V), BF16),
                   jax.ShapeDtypeStruct((n_batch, RET_HEADS, RET_DQK, RET_DV), F32)],
        scratch_shapes=[pltpu.VMEM((RET_HEADS, RET_DQK, RET_DV), F32)],
        compiler_params=pltpu.CompilerParams(
            dimension_semantics=("parallel", "arbitrary"), vmem_limit_bytes=VMEM_LIMIT),
        name="retention",
    )(proj, proj, proj, proj, cos, sin, intra, q_dec, k_dec, c_dec, gn_gain.reshape(1, R_V), state0)


SSD_COL0 = 2 * R_QK + 2 * R_V
HEADS_PER_GROUP = M2_HEADS // M2_GROUPS


def _split3(x):
    a = x.astype(BF16)
    r = x - a.astype(F32)
    b = r.astype(BF16)
    c = (r - b.astype(F32)).astype(BF16)
    return a, b, c


def _exact_dot(mat_bf, x):
    out = None
    for piece in _split3(x):
        t = jnp.dot(mat_bf, piece, preferred_element_type=F32)
        out = t if out is None else out + t
    return out


def _exact_dot_r(x, mat_bf):
    out = None
    for piece in _split3(x):
        t = jnp.dot(piece, mat_bf, preferred_element_type=F32)
        out = t if out is None else out + t
    return out


def _ssd_kernel(z_ref, xa_ref, xb_ref, xc_ref, dt_ref, cw_ref, cb_ref, dtb_ref, aneg_ref, dskip_ref, norm_ref,
                buf0_ref, s0_ref, o_ref, s_out_ref, s_scr, prev_scr):
    c_idx = pl.program_id(1)
    chunk = z_ref.shape[0]
    nt = (((1,), (1,)), ((), ()))
    tn = (((0,), (0,)), ((), ()))

    @pl.when(c_idx == 0)
    def _():
        s_scr[...] = s0_ref[0]
        prev_scr[...] = buf0_ref[0]

    x = jnp.concatenate([xa_ref[...], xb_ref[...], xc_ref[...]], axis=1)
    prev = prev_scr[...]
    row8 = lax.broadcasted_iota(jnp.int32, prev.shape, 0)
    conv = x * cw_ref[M2_CONV - 1:M2_CONV, :]
    for k in range(1, M2_CONV):
        rolled = pltpu.roll(x, k, 0)
        top = jnp.where(row8 < k, pltpu.roll(prev, k, 0), rolled[0:8])
        shifted = jnp.concatenate([top, rolled[8:]], axis=0)
        conv = conv + shifted * cw_ref[M2_CONV - 1 - k:M2_CONV - k, :]
    prev_scr[...] = x[chunk - 8:chunk]
    conv = conv + cb_ref[...]
    xbc = conv * jax.nn.sigmoid(conv)
    xs = xbc[:, :M2_DINNER]

    dt_raw = dt_ref[...] + dtb_ref[...]
    dt = jnp.where(dt_raw > 20.0, dt_raw, jnp.log1p(jnp.exp(jnp.minimum(dt_raw, 20.0))))
    a = dt * aneg_ref[...]
    ri = lax.broadcasted_iota(jnp.int32, (chunk, chunk), 0)
    ci = lax.broadcasted_iota(jnp.int32, (chunk, chunk), 1)
    tri = ri >= ci
    cum = _exact_dot(jnp.where(tri, 1.0, 0.0).astype(BF16), a)
    cum_t = cum.T
    cum_last = cum[chunk - 1:chunk, :]
    hi = lax.broadcasted_iota(jnp.int32, (LANES, M2_DINNER), 0)
    li = lax.broadcasted_iota(jnp.int32, (LANES, M2_DINNER), 1)
    expand = jnp.where(hi == li // M2_HEADDIM, 1.0, 0.0).astype(BF16)
    dt_x = _exact_dot_r(dt, expand)
    cum_x = _exact_dot_r(cum, expand)
    last_x = _exact_dot_r(cum_last, expand)
    xdt = xs * dt_x
    x_dec = (xdt * jnp.exp(last_x - cum_x)).astype(BF16)
    xdt_bf = xdt.astype(BF16)
    e_cum_x = jnp.exp(cum_x)
    e_last_x = jnp.exp(last_x)

    y_parts = []
    for gi in range(M2_GROUPS):
        b_g = xbc[:, M2_DINNER + gi * M2_STATE:M2_DINNER + (gi + 1) * M2_STATE].astype(BF16)
        c_g = xbc[:, M2_DINNER + (M2_GROUPS + gi) * M2_STATE:M2_DINNER + (M2_GROUPS + gi + 1) * M2_STATE].astype(BF16)
        cb = lax.dot_general(c_g, b_g, nt, preferred_element_type=F32)
        gcols = slice(gi * HEADS_PER_GROUP * M2_HEADDIM, (gi + 1) * HEADS_PER_GROUP * M2_HEADDIM)
        s_old = s_scr[:, gcols]
        y_state = jnp.dot(c_g, s_old.astype(BF16), preferred_element_type=F32) * e_cum_x[:, gcols]
        s_scr[:, gcols] = s_old * e_last_x[:, gcols] + lax.dot_general(b_g, x_dec[:, gcols], tn,
                                                                        preferred_element_type=F32)
        pair_lane = lax.broadcasted_iota(jnp.int32, (chunk, LANES), 1)
        intra = []
        for pr in range(HEADS_PER_GROUP // 2):
            outs = []
            for sub in range(2):
                h = gi * HEADS_PER_GROUP + pr * 2 + sub
                seg = cum[:, h:h + 1] - cum_t[h:h + 1, :]
                l_mat = jnp.where(tri, jnp.exp(jnp.where(tri, seg, 0.0)), 0.0)
                lanes = slice(gi * HEADS_PER_GROUP * M2_HEADDIM + pr * LANES,
                              gi * HEADS_PER_GROUP * M2_HEADDIM + (pr + 1) * LANES)
                outs.append(jnp.dot((cb * l_mat).astype(BF16), xdt_bf[:, lanes], preferred_element_type=F32))
            intra.append(jnp.where(pair_lane < M2_HEADDIM, outs[0], outs[1]))
        y_parts.append(jnp.concatenate(intra, axis=1) + y_state)
    y = jnp.concatenate(y_parts, axis=1) + dskip_ref[...] * xs
    z = z_ref[...]
    y = y * (z * jax.nn.sigmoid(z))
    gw = M2_DINNER // M2_GROUPS
    outs = []
    for gi in range(M2_GROUPS):
        yg = y[:, gi * gw:(gi + 1) * gw]
        outs.append(yg * lax.rsqrt(jnp.mean(yg * yg, axis=-1, keepdims=True) + EPS))
    o_ref[...] = (jnp.concatenate(outs, axis=1) * norm_ref[...]).astype(o_ref.dtype)
    s_out_ref[0] = s_scr[...]


def ssd(proj, n_batch, t_len, chunk, conv_buf, state0, conv_w, conv_b, dt_bias, a_log, d_skip, m2_norm):
    nc = t_len // chunk
    z_blk = SSD_COL0 // M2_DINNER
    xw = M2_CONV_DIM // 3
    xbc_blk = (SSD_COL0 + M2_DINNER) // xw
    assert (SSD_COL0 + M2_DINNER) % xw == 0 and xw % LANES == 0
    dt_blk = (SSD_COL0 + M2_DINNER + M2_CONV_DIM) // LANES
    pad = lambda v: jnp.pad(v.reshape(1, -1), ((0, 0), (0, LANES - v.shape[-1])))
    buf8 = jnp.pad(conv_buf, ((0, 0), (8 - (M2_CONV - 1), 0), (0, 0)))
    st_t = state0.transpose(0, 3, 1, 2).reshape(n_batch, M2_STATE, M2_DINNER)
    row = lambda a: pl.BlockSpec(a.shape, lambda b, c: (0, 0))
    cw = conv_w
    cb = conv_b.reshape(1, -1)
    dtb, aneg = pad(dt_bias), pad(-jnp.exp(a_log))
    dsk = jnp.repeat(d_skip, M2_HEADDIM).reshape(1, -1)
    nrm = m2_norm.reshape(1, -1)
    st = pl.BlockSpec((1, M2_STATE, M2_DINNER), lambda b, c: (b, 0, 0))
    out, s_fin = pl.pallas_call(
        _ssd_kernel,
        grid=(n_batch, nc),
        in_specs=[pl.BlockSpec((chunk, M2_DINNER), lambda b, c: (b * nc + c, z_blk)),
                  pl.BlockSpec((chunk, xw), lambda b, c: (b * nc + c, xbc_blk)),
                  pl.BlockSpec((chunk, xw), lambda b, c: (b * nc + c, xbc_blk + 1)),
                  pl.BlockSpec((chunk, xw), lambda b, c: (b * nc + c, xbc_blk + 2)),
                  pl.BlockSpec((chunk, LANES), lambda b, c: (b * nc + c, dt_blk)),
                  row(cw), row(cb), row(dtb), row(aneg), row(dsk), row(nrm),
                  pl.BlockSpec((1, 8, M2_CONV_DIM), lambda b, c: (b, 0, 0)), st],
        out_specs=[pl.BlockSpec((chunk, M2_DINNER), lambda b, c: (b * nc + c, 0)), st],
        out_shape=[jax.ShapeDtypeStruct((n_batch * t_len, M2_DINNER), BF16),
                   jax.ShapeDtypeStruct((n_batch, M2_STATE, M2_DINNER), F32)],
        scratch_shapes=[pltpu.VMEM((M2_STATE, M2_DINNER), F32), pltpu.VMEM((8, M2_CONV_DIM), F32)],
        compiler_params=pltpu.CompilerParams(
            dimension_semantics=("parallel", "arbitrary"), vmem_limit_bytes=VMEM_LIMIT),
        name="ssd",
    )(proj, proj, proj, proj, proj, cw, cb, dtb, aneg, dsk, nrm, buf8, st_t)
    s_fin = s_fin.reshape(n_batch, M2_STATE, M2_HEADS, M2_HEADDIM).transpose(0, 2, 3, 1)
    return out, s_fin


def _rotary(x, pos, rot_dims, theta):
    half = rot_dims // 2
    inv = theta ** (-jnp.arange(half, dtype=F32) / half)
    ang = pos.astype(F32)[:, None] * inv[None, :]
    shape = (pos.shape[0],) + (1,) * (x.ndim - 3) + (half,)
    cos = jnp.cos(ang).reshape(shape)
    sin = jnp.sin(ang).reshape(shape)
    x1 = x[..., :half]
    x2 = x[..., half:rot_dims]
    rot = jnp.concatenate([x1 * cos - x2 * sin, x1 * sin + x2 * cos], axis=-1)
    return jnp.concatenate([rot, x[..., rot_dims:]], axis=-1)


def _masked_softmax(s, mask, axis=-1):
    s = jnp.where(mask, s, -1e30)
    p = jnp.exp(s - jnp.max(s, axis=axis, keepdims=True)) * mask
    return p / jnp.maximum(jnp.sum(p, axis=axis, keepdims=True), 1e-30)


def _causal_conv(u, buf, w, b=None):
    k_size = w.shape[0]
    t_len = u.shape[1]
    ext = jnp.concatenate([buf.astype(u.dtype), u], axis=1)
    y = ext[:, 0:t_len] * w[0]
    for j in range(1, k_size):
        y = y + ext[:, j:j + t_len] * w[j]
    if b is not None:
        y = y + b
    return y, ext[:, t_len:]


def _to_chunks(a, chunk):
    return a.reshape(a.shape[0], a.shape[1] // chunk, chunk, *a.shape[2:]).swapaxes(0, 1)


def _nsa_compress_jnp(rows, pe, w1, w2):
    bn, length, g, d = rows.shape
    n_part = CMP_BLOCK // CMP_STRIDE
    n_sub = length // CMP_STRIDE
    n_cmp = n_sub - n_part + 1
    sub = rows[:, :n_sub * CMP_STRIDE].reshape(bn, n_sub, CMP_STRIDE, g, d)
    w1p = w1.reshape(n_part, CMP_STRIDE, d, d)
    pre = jnp.einsum('jsd,jsde->e', pe.reshape(n_part, CMP_STRIDE, d), w1p)
    for j in range(n_part):
        pre = pre + jnp.einsum('bnsgd,sde->bnge', sub[:, j:j + n_cmp], w1p[j])
    return jnp.einsum('bnge,ef->bngf', jax.nn.gelu(pre), w2)


def _nsa_sparse_jnp(q, q_pos, rows, cmp_pe, cmp_w1, cmp_w2):
    bn, t_len, g, r, d = q.shape
    length = rows.shape[1]
    kc = _nsa_compress_jnp(rows[:, :, 0], cmp_pe[0], cmp_w1[0], cmp_w2[0])
    vc = _nsa_compress_jnp(rows[:, :, 1], cmp_pe[1], cmp_w1[1], cmp_w2[1])
    n_cmp = kc.shape[1]
    c_start = jnp.arange(n_cmp) * CMP_STRIDE
    c_mask = (c_start + CMP_BLOCK - 1)[None, :] <= q_pos[:, None]
    p_c = _masked_softmax(jnp.einsum('btgrd,bngd->btgrn', q, kc), c_mask[None, :, None, None, :])
    o_cmp = jnp.einsum('btgrn,bngd->btgrd', p_c, vc)
    n_slc = -(-length // SLC_BLOCK)
    s_start = jnp.arange(n_slc) * SLC_BLOCK
    overlap = ((c_start[:, None] < s_start[None, :] + SLC_BLOCK)
               & (c_start[:, None] + CMP_BLOCK > s_start[None, :])).astype(F32)
    imp = jnp.einsum('btgrn,nm->btgm', p_c, overlap)
    blk = jnp.arange(n_slc)[None, :]
    cur = (q_pos // SLC_BLOCK)[:, None]
    valid = (s_start[None, :] <= q_pos[:, None])[None, :, None, :]
    forced = ((blk == 0) | (blk == cur) | (blk == cur - 1))[None, :, None, :]
    score = jnp.where(valid, jnp.where(forced, 1e6, imp), -1e6)
    n_sel = min(SLC_TOPK, n_slc)
    _, idx = lax.top_k(score, n_sel)
    picked = jnp.any(idx[..., None] == jnp.arange(n_slc), axis=-2)
    kmask = jnp.repeat(picked, SLC_BLOCK, axis=-1)[..., :length]
    kmask = kmask & (jnp.arange(length)[None, :] <= q_pos[:, None])[None, :, None, :]
    s = jnp.einsum('btgrd,blgd->btgrl', q, rows[:, :, 2])
    p = _masked_softmax(s, kmask[:, :, :, None, :])
    o_slc = jnp.einsum('btgrl,blgd->btgrd', p, rows[:, :, 3])
    return o_cmp, o_slc


def _window_attention_jnp(q, q_pos, k, v, k_pos):
    s = jnp.einsum('ntgrd,nsgd->ntgrs', q, k)
    dpos = q_pos[:, :, None] - k_pos[:, None, :]
    mask = (dpos >= 0) & (dpos < WINDOW) & (k_pos[:, None, :] >= 0)
    p = _masked_softmax(s, mask[:, :, None, None, :])
    return jnp.einsum('ntgrs,nsgd->ntgrd', p, v)


def _nsa_sample_jnp(q, new_rows, win_rows, gates, pos, past_rows, win_buf, cmp_pe, cmp_w1, cmp_w2):
    bn, t_len, g, r, d = q.shape
    rows = jnp.concatenate([past_rows, new_rows], axis=1)
    o_cmp, o_slc = _nsa_sparse_jnp(q, pos, rows, cmp_pe, cmp_w1, cmp_w2)
    w_len = win_buf.shape[1]
    ext = jnp.concatenate([win_buf, win_rows], axis=1)
    k_pos = (pos[0] - w_len + jnp.arange(w_len + t_len))[None]
    o_win = _window_attention_jnp(q, pos[None], ext[:, :, 0], ext[:, :, 1], k_pos)
    gate = jax.nn.sigmoid(gates).reshape(bn, t_len, 3, g, r, 1)
    o_nsa = gate[:, :, 0] * o_cmp + gate[:, :, 1] * o_slc + gate[:, :, 2] * o_win
    return o_nsa.reshape(bn * t_len, E_Q), ext[:, t_len:]


def _retention_jnp(q, k, v, state, chunk):
    bn, t_len, h, _ = q.shape
    dv = v.shape[-1]
    log_g = jnp.log1p(-(2.0 ** (-5.0 - jnp.arange(h, dtype=F32))))
    i = jnp.arange(chunk, dtype=F32)
    diff = i[:, None] - i[None, :]
    intra = jnp.where(diff >= 0, jnp.exp(jnp.maximum(diff, 0.0)[None] * log_g[:, None, None]), 0.0)
    q_dec = jnp.exp((i[:, None] + 1.0) * log_g[None, :])
    k_dec = jnp.exp((chunk - 1.0 - i)[:, None] * log_g[None, :])
    c_dec = jnp.exp(chunk * log_g)

    def step(s_mat, inp):
        qc, kc, vc = inp
        att = jnp.einsum('bihd,bjhd->bhij', qc, kc) * intra
        o = (jnp.einsum('bhij,bjhv->bihv', att, vc)
             + jnp.einsum('bihd,bhdv->bihv', qc, s_mat) * q_dec[None, :, :, None])
        s_mat = s_mat * c_dec[None, :, None, None] + jnp.einsum('bjhd,bjhv->bhdv', kc * k_dec[None, :, :, None], vc)
        return s_mat, o

    s_fin, o = lax.scan(step, state, (_to_chunks(q, chunk), _to_chunks(k, chunk), _to_chunks(v, chunk)))
    return o.swapaxes(0, 1).reshape(bn, t_len, h, dv), s_fin


def _ssd_jnp(x, dt, a_neg, b_in, c_in, state, chunk):
    bn, t_len, h, p = x.shape
    rep = h // b_in.shape[2]
    a = dt * a_neg
    b_h = jnp.repeat(b_in, rep, axis=2)
    c_h = jnp.repeat(c_in, rep, axis=2)
    xdt = x * dt[..., None]
    tri = jnp.arange(chunk)[:, None] >= jnp.arange(chunk)[None, :]

    def step(s_mat, inp):
        a_c, x_c, b_c, c_c = inp
        cum = jnp.cumsum(a_c, axis=1)
        seg = cum[:, :, None, :] - cum[:, None, :, :]
        l_mat = jnp.exp(jnp.where(tri[None, :, :, None], seg, -1e30))
        cb = jnp.einsum('bihn,bjhn->bijh', c_c, b_c) * l_mat
        y = (jnp.einsum('bijh,bjhp->bihp', cb, x_c)
             + jnp.einsum('bihn,bhpn->bihp', c_c, s_mat) * jnp.exp(cum)[..., None])
        dec_end = jnp.exp(cum[:, -1:, :] - cum)
        s_mat = (s_mat * jnp.exp(cum[:, -1, :])[:, :, None, None]
                 + jnp.einsum('bjhn,bjhp->bhpn', b_c * dec_end[..., None], x_c))
        return s_mat, y

    s_fin, y = lax.scan(step, state, (_to_chunks(a, chunk), _to_chunks(xdt, chunk), _to_chunks(b_h, chunk),
                                      _to_chunks(c_h, chunk)))
    return y.swapaxes(0, 1).reshape(bn, t_len, h, p), s_fin


def _odd_mixer_jnp(proj, pos, ret_state, ssm_state, conv_buf, gn_gain, conv_w, conv_b, dt_bias, a_log, d_skip,
                   m2_norm):
    bn, t_len, _ = proj.shape
    chunk = CHUNK if t_len % CHUNK == 0 else t_len
    q, k, v, g, z, xbc, dt = jnp.split(proj, O_SPLITS, axis=-1)
    q = _rotary(q.reshape(bn, t_len, RET_HEADS, RET_DQK), pos, RET_DQK, RET_THETA) * (RET_DQK ** -0.5)
    k = _rotary(k.reshape(bn, t_len, RET_HEADS, RET_DQK), pos, RET_DQK, RET_THETA)
    v = v.reshape(bn, t_len, RET_HEADS, RET_DV)
    o_ret, ret_new = _retention_jnp(q, k, v, ret_state, chunk)
    mu = jnp.mean(o_ret, axis=-1, keepdims=True)
    var = jnp.mean(jnp.square(o_ret - mu), axis=-1, keepdims=True)
    o_ret = ((o_ret - mu) * lax.rsqrt(var + EPS)).reshape(bn, t_len, R_V) * gn_gain
    o_ret = jax.nn.silu(g) * o_ret
    xbc_c, _ = _causal_conv(xbc, conv_buf, conv_w, conv_b)
    xbc_c = jax.nn.silu(xbc_c)
    xs, b_in, c_in = jnp.split(xbc_c, [M2_DINNER, M2_DINNER + M2_GROUPS * M2_STATE], axis=-1)
    xs = xs.reshape(bn, t_len, M2_HEADS, M2_HEADDIM)
    b_in = b_in.reshape(bn, t_len, M2_GROUPS, M2_STATE)
    c_in = c_in.reshape(bn, t_len, M2_GROUPS, M2_STATE)
    dt = jax.nn.softplus(dt + dt_bias)
    a_neg = -jnp.exp(a_log)
    y, ssm_new = _ssd_jnp(xs, dt, a_neg, b_in, c_in, ssm_state, chunk)
    y = y + d_skip[:, None] * xs
    y = y.reshape(bn, t_len, M2_DINNER) * jax.nn.silu(z)
    yg = y.reshape(bn, t_len, M2_GROUPS, M2_DINNER // M2_GROUPS)
    yg = yg * lax.rsqrt(jnp.mean(yg * yg, axis=-1, keepdims=True) + EPS)
    y = yg.reshape(bn, t_len, M2_DINNER) * m2_norm
    return o_ret.reshape(bn * t_len, R_V), y.reshape(bn * t_len, M2_DINNER), ret_new, ssm_new


def _short_conv_jnp(sc, n_batch, t_len, conv_buf, sc_w):
    b_gate, c_gate, h_in = jnp.split(sc.reshape(n_batch, t_len, E_SC), 3, axis=-1)
    conv_out, new_conv = _causal_conv(c_gate * h_in, conv_buf, sc_w)
    return (b_gate * conv_out).reshape(n_batch * t_len, SC_WIDTH), new_conv


def kernel(x_prompt, x_sample, cache_nsa_kv, cache_nsa_win, state_sc_conv, state_ret, state_ssm, state_m2_conv,
           page_table, norm_w, final_norm, e_w_in, e_w_out, e_cmp_pe, e_cmp_w1, e_cmp_w2, e_sc_conv, o_w_in,
           o_w_out, o_ret_gn, o_m2_conv_w, o_m2_conv_b, o_m2_dt_bias, o_m2_a_log, o_m2_d, o_m2_norm, peer_wq,
           peer_keys, peer_u, peer_v):
    bp, tp, dm = x_prompt.shape
    bs, ts, _ = x_sample.shape
    n_p, n_s = bp * tp, bs * ts
    g, r, d = NSA_KV_HEADS, NSA_GROUP, NSA_HEAD_DIM
    past_len = page_table.shape[1] * PAGE_SIZE
    pos_p = jnp.arange(tp, dtype=jnp.int32)
    pos_s = past_len + jnp.arange(ts, dtype=jnp.int32)
    pos_rows = jnp.concatenate([jnp.tile(pos_p, bp), jnp.tile(pos_s, bs)])
    x = jnp.concatenate([x_prompt.reshape(n_p, dm), x_sample.reshape(n_s, dm)], axis=0)

    w0 = e_w_in[0]
    w_in = jnp.concatenate([w0[:, :SC_COL0], w0[:, SC_COL0 + E_G:], w0[:, SC_COL0:SC_COL0 + E_G],
                            jnp.zeros((dm, _pad_cols(E_IN) - E_IN), F32)], axis=1).astype(BF16)
    proj = norm_matmul(x, norm_w[0, 0], w_in)
    q_bf, rows, win, rows_bf, win_bf = nsa_rope(proj, pos_rows)
    kvc = nsa_compress(rows, bp, tp, e_cmp_pe[0], e_cmp_w1[0], e_cmp_w2[0])
    o_nsa_p = nsa_prompt(q_bf, kvc, rows_bf, win_bf, proj, bp, tp)
    p_kv = rows[:n_p].reshape(bp, tp, 4, g, d)
    s_kv = rows[n_p:].reshape(bs, ts, 4, g, d)
    p_win = win[:n_p].reshape(bp, tp, 2, g, d)[:, tp - min(WINDOW, tp):]
    win_s = win[n_p:].reshape(bs, ts, 2, g, d)
    s_win = jnp.concatenate([cache_nsa_win[0], win_s], axis=1)[:, ts:]
    cache = cache_nsa_kv[0]
    q_s = q_bf[n_p:].reshape(bs, ts, g, r, d).transpose(0, 2, 3, 1, 4).reshape(bs, g, r * ts, d)
    a0, a1 = nsa_sample_compress(cache, page_table, e_cmp_pe[0], e_cmp_w1[0])
    o_cmp_s, sel_s = nsa_sample_select(a0, a1, e_cmp_w2[0], q_s, past_len, ts)
    o_nsa_s = nsa_sample_attend(cache, page_table, q_s, sel_s, o_cmp_s, rows, win,
                                cache_nsa_win[0], proj, past_len, ts, n_p)
    o_nsa_s = o_nsa_s.reshape(bs, g, r, ts, d).transpose(0, 3, 1, 2, 4).reshape(n_s, E_Q)
    o_nsa = jnp.concatenate([o_nsa_p, o_nsa_s.astype(BF16)], axis=0)
    o_sc_p, p_sc = short_conv_prompt(proj, bp, tp, e_sc_conv[0])
    o_sc_s, s_sc = _short_conv_jnp(proj[n_p:, SC_COL0:GATE_COL0], bs, ts, state_sc_conv[0], e_sc_conv[0])
    o_sc = jnp.concatenate([o_sc_p, o_sc_s.astype(BF16)], axis=0)
    x = matmul2_res(o_nsa, o_sc, e_w_out[0].astype(BF16), x)
    x = peer_layer(x, norm_w[0, 1], peer_wq[0], peer_keys[0], peer_u[0], peer_v[0])

    w_in = jnp.pad(o_w_in[0], ((0, 0), (0, _pad_cols(O_IN) - O_IN))).astype(BF16)
    proj = norm_matmul(x, norm_w[1, 0], w_in)
    odd_w = (o_m2_conv_w[0], o_m2_conv_b[0], o_m2_dt_bias[0], o_m2_a_log[0], o_m2_d[0], o_m2_norm[0])
    o_ret_p, p_ret = retention(proj, pos_p, bp, tp, CHUNK, jnp.zeros((bp, RET_HEADS, RET_DQK, RET_DV), F32),
                               o_ret_gn[0])
    o_ssd_p, p_ssm = ssd(proj, bp, tp, CHUNK, jnp.zeros((bp, M2_CONV - 1, M2_CONV_DIM), F32),
                         jnp.zeros((bp, M2_HEADS, M2_HEADDIM, M2_STATE), F32), *odd_w)
    xbc_cols = slice(O_SPLITS[4], O_SPLITS[5])
    p_m2c = proj[:n_p, xbc_cols].reshape(bp, tp, M2_CONV_DIM)[:, tp - (M2_CONV - 1):]
    xbc_s = proj[n_p:, xbc_cols].reshape(bs, ts, M2_CONV_DIM)
    s_m2c = jnp.concatenate([state_m2_conv[0], xbc_s], axis=1)[:, ts:]
    o_ret_s, o_ssd_s, s_ret, s_ssm = _odd_mixer_jnp(proj[n_p:, :O_IN].reshape(bs, ts, O_IN), pos_s, state_ret[0],
                                                    state_ssm[0], state_m2_conv[0], o_ret_gn[0], *odd_w)
    o_ret = jnp.concatenate([o_ret_p, o_ret_s.astype(BF16)], axis=0)
    o_ssd = jnp.concatenate([o_ssd_p, o_ssd_s.astype(BF16)], axis=0)
    x = matmul2_res(o_ret, o_ssd, o_w_out[0].astype(BF16), x)
    x = peer_layer(x, norm_w[1, 1], peer_wq[1], peer_keys[1], peer_u[1], peer_v[1])

    y = rmsnorm(x, final_norm)
    y_prompt = y[:n_p].reshape(bp, tp, dm)
    y_sample = y[n_p:].reshape(bs, ts, dm)
    return (y_prompt, y_sample, p_kv[None], p_win[None], p_sc[None], p_ret[None], p_ssm[None], p_m2c[None],
            s_kv[None], s_win[None], s_sc[None], s_ret[None], s_ssm[None], s_m2c[None])
```

```python
import functools
import math

import jax
import jax.numpy as jnp
from jax import lax
from jax.experimental import pallas as pl
from jax.experimental.pallas import tpu as pltpu

F32 = jnp.float32
BF16 = jnp.bfloat16

D_MODEL = 2048
DEPTH = 2
PAGE_SIZE = 128
NSA_HEAD_DIM = 128
NSA_HEADS = 8
NSA_KV_HEADS = 2
NSA_GROUP = 4
CMP_BLOCK = 32
CMP_STRIDE = 16
SLC_BLOCK = 64
SLC_TOPK = 16
WINDOW = 512
ROPE_THETA = 500000.0
ROPE_DIMS = 32
SC_WIDTH = 1024
SC_KSIZE = 3
RET_HEADS = 4
RET_DQK = 256
RET_DV = 256
RET_THETA = 10000.0
M2_DINNER = 1024
M2_HEADDIM = 64
M2_HEADS = 16
M2_STATE = 128
M2_GROUPS = 2
M2_CONV = 4
M2_CONV_DIM = M2_DINNER + 2 * M2_GROUPS * M2_STATE
PEER_HEADS = 8
PEER_KEYS = 128
PEER_QDIM = 256
PEER_TOPK = 16
Q_BLOCK = 128
CHUNK = 128
EPS = 1e-6

E_Q = NSA_HEADS * NSA_HEAD_DIM
E_KV = 6 * NSA_KV_HEADS * NSA_HEAD_DIM
E_G = 3 * NSA_HEADS
E_SC = 3 * SC_WIDTH
E_IN = E_Q + E_KV + E_G + E_SC
R_QK = RET_HEADS * RET_DQK
R_V = RET_HEADS * RET_DV
O_SPLITS = [R_QK, 2 * R_QK, 2 * R_QK + R_V, 2 * R_QK + 2 * R_V,
            2 * R_QK + 2 * R_V + M2_DINNER, 2 * R_QK + 2 * R_V + M2_DINNER + M2_CONV_DIM]
O_IN = O_SPLITS[-1] + M2_HEADS
SC_COL0 = E_Q + E_KV
GATE_COL0 = SC_COL0 + E_SC

LANES = 128
VMEM_LIMIT = 56 * 1024 * 1024
ROW_TILE_CAP = 1056
ROPE_ROWS_CAP = 544
NEG_INF = float("-inf")


def _round_up(n, m):
    return -(-n // m) * m


def _pick_tile(n, cap):
    best = LANES
    for t in range(LANES, cap + 1, LANES):
        if n % t == 0:
            best = t
    return best


def _pad_cols(m):
    return min((_round_up(m, t) for t in (768, 640, 512)))


def _row_tile(n, cap):
    best = None
    for t in range(16, cap + 1, 16):
        if n % t == 0:
            best = t
    assert best is not None
    return best


def _gelu_tanh(x):
    return 0.5 * x * (1.0 + jnp.tanh(math.sqrt(2.0 / math.pi) * (x + 0.044715 * (x * x * x))))


def _norm_matmul_kernel(x_ref, g_ref, w_ref, o_ref, xn_ref):
    @pl.when(pl.program_id(1) == 0)
    def _():
        x = x_ref[...]
        ms = jnp.mean(x * x, axis=-1, keepdims=True)
        xn_ref[...] = (x * lax.rsqrt(ms + EPS) * g_ref[...]).astype(BF16)

    o_ref[...] = jnp.dot(xn_ref[...], w_ref[...], preferred_element_type=F32).reshape(o_ref.shape)


def norm_matmul(x, gain, w_bf):
    n, k = x.shape
    m = w_bf.shape[1]
    tm = _row_tile(n, ROW_TILE_CAP)
    tn = _pick_tile(m, 768)
    return pl.pallas_call(
        _norm_matmul_kernel,
        grid=(n // tm, m // tn),
        in_specs=[pl.BlockSpec((tm, k), lambda i, j: (i, 0)),
                  pl.BlockSpec((1, k), lambda i, j: (0, 0)),
                  pl.BlockSpec((k, tn), lambda i, j: (0, j))],
        out_specs=pl.BlockSpec((tm, tn), lambda i, j: (i, j)),
        out_shape=jax.ShapeDtypeStruct((n, m), F32),
        scratch_shapes=[pltpu.VMEM((tm, k), BF16)],
        compiler_params=pltpu.CompilerParams(
            dimension_semantics=("parallel", "arbitrary"), vmem_limit_bytes=VMEM_LIMIT),
        name="norm_matmul",
    )(x, gain.reshape(1, k), w_bf)


def _matmul2_res_kernel(a1_ref, a2_ref, w1_ref, w2_ref, r_ref, o_ref):
    o_ref[...] = (r_ref[...] + jnp.dot(a1_ref[...], w1_ref[...], preferred_element_type=F32)
                  + jnp.dot(a2_ref[...], w2_ref[...], preferred_element_type=F32))


def matmul2_res(a1, a2, w_bf, res):
    n, k1 = a1.shape
    k2 = a2.shape[1]
    m = w_bf.shape[1]
    assert k1 == k2
    tm = _row_tile(n, ROW_TILE_CAP)
    tn = _pick_tile(m, 1024)
    return pl.pallas_call(
        _matmul2_res_kernel,
        grid=(n // tm, m // tn),
        in_specs=[pl.BlockSpec((tm, k1), lambda i, j: (i, 0)),
                  pl.BlockSpec((tm, k2), lambda i, j: (i, 0)),
                  pl.BlockSpec((k1, tn), lambda i, j: (0, j)),
                  pl.BlockSpec((k2, tn), lambda i, j: (1, j)),
                  pl.BlockSpec((tm, tn), lambda i, j: (i, j))],
        out_specs=pl.BlockSpec((tm, tn), lambda i, j: (i, j)),
        out_shape=jax.ShapeDtypeStruct((n, m), F32),
        compiler_params=pltpu.CompilerParams(
            dimension_semantics=("parallel", "parallel"), vmem_limit_bytes=VMEM_LIMIT),
        name="matmul2_res",
    )(a1, a2, w_bf, w_bf, res)


PEER_TOK = 128
PEER_HEAD_UNROLL = 4
PEER_TM = 768
PEER_TA = 8
PEER_TE = PEER_TA * PEER_KEYS


def _top_desc(work, count):
    rows = []
    for _ in range(count):
        m = jnp.max(work, axis=0, keepdims=True)
        rows.append(m)
        work = jnp.where(work == m, NEG_INF, work)
    return rows


def _peer_router_kernel(q_ref, k_ref, cnt_ref, gw_ref, r2_ref, w2_ref):
    def head(h, carry):
        q = q_ref[h]
        scores = []
        for side in range(2):
            qs = q[:, side * PEER_KEYS:(side + 1) * PEER_KEYS]
            qs = qs * lax.rsqrt(jnp.mean(qs * qs, axis=-1, keepdims=True) + EPS)
            scores.append(lax.dot_general(k_ref[h, side], qs, (((1,), (1,)), ((), ())),
                                          preferred_element_type=F32))
        s1, s2 = scores
        v1 = _top_desc(s1, PEER_TOPK + 1)
        v2 = _top_desc(s2, PEER_TOPK + 1)
        v2_lo = jnp.concatenate(v2[:8], axis=0)
        v2_hi = jnp.concatenate(v2[8:16], axis=0)
        row = lax.broadcasted_iota(jnp.int32, v2_lo.shape, 0)
        blocks = [v1[0] + v2_lo, v1[0] + v2_hi, v1[1] + v2_lo]
        for a, lim in ((2, 5), (3, 4), (4, 3), (5, 2), (6, 2), (7, 2)):
            blocks.append(jnp.where(row < lim, v1[a] + v2_lo, NEG_INF))
        blocks.append(jnp.concatenate(v1[8:16], axis=0) + v2[0])
        extra = jnp.where(row == 0, v1[0] + v2[16], jnp.where(row == 1, v1[16] + v2[0], NEG_INF))
        blocks.append(extra)
        cand = jnp.concatenate(blocks, axis=0)
        tops = _top_desc(cand, PEER_TOPK + 1)
        z = jnp.zeros_like(tops[0])
        for r in range(PEER_TOPK):
            z = z + jnp.exp(tops[r] - tops[0])
        tau = 0.5 * (tops[PEER_TOPK - 1] + tops[PEER_TOPK])
        count = jnp.zeros_like(s1)
        rank2 = jnp.zeros_like(s2)
        for r in range(PEER_TOPK):
            count = count + jnp.where(s1 + v2[r] >= tau, 1.0, 0.0)
        for r in range(PEER_TOPK + 1):
            rank2 = rank2 + jnp.where(v2[r] > s2, 1.0, 0.0)
        cnt_ref[h] = count
        gw_ref[h] = jnp.exp(s1 - v1[0]) / z
        r2_ref[h] = rank2.astype(BF16)
        w2_ref[h] = jnp.exp(s2 - v2[0]).astype(BF16)
        return carry

    lax.fori_loop(0, PEER_HEADS, head, 0, unroll=PEER_HEAD_UNROLL)


def peer_router(q_hm, keys):
    n = q_hm.shape[1]
    out = jax.ShapeDtypeStruct((PEER_HEADS, PEER_KEYS, n), F32)
    out_bf = jax.ShapeDtypeStruct((PEER_HEADS, PEER_KEYS, n), BF16)
    spec = pl.BlockSpec((PEER_HEADS, PEER_KEYS, PEER_TOK), lambda i: (0, 0, i))
    return pl.pallas_call(
        _peer_router_kernel,
        grid=(n // PEER_TOK,),
        in_specs=[pl.BlockSpec((PEER_HEADS, PEER_TOK, PEER_QDIM), lambda i: (0, i, 0)),
                  pl.BlockSpec((PEER_HEADS, 2, PEER_KEYS, PEER_QDIM // 2), lambda i: (0, 0, 0, 0))],
        out_specs=[spec, spec, spec, spec],
        out_shape=[out, out, out_bf, out_bf],
        compiler_params=pltpu.CompilerParams(dimension_semantics=("parallel",), vmem_limit_bytes=VMEM_LIMIT),
        name="peer_router",
    )(q_hm, keys)


def _peer_expert_kernel(xt_ref, u_ref, vt_ref, cnt_ref, gw_ref, r2_ref, w2_ref, o_ref, s_scr, hg_scr):
    e = pl.program_id(1)
    s_scr[...] = jnp.dot(u_ref[...], xt_ref[...], preferred_element_type=F32)
    zero = jnp.zeros((), BF16)
    for al in range(PEER_TA):
        rows = slice(al * PEER_KEYS, (al + 1) * PEER_KEYS)
        g = None
        for h in range(PEER_HEADS):
            cnt = cnt_ref[h, al:al + 1, :].astype(BF16)
            gate = gw_ref[h, al:al + 1, :].astype(BF16)
            t = jnp.where(r2_ref[h] < cnt, w2_ref[h], zero) * gate
            g = t if g is None else g + t
        hg_scr[rows, :] = _gelu_tanh(s_scr[rows, :]).astype(BF16) * g
    part = jnp.dot(vt_ref[...], hg_scr[...], preferred_element_type=F32)

    @pl.when(e == 0)
    def _():
        o_ref[...] = part

    @pl.when(e != 0)
    def _():
        o_ref[...] += part


def peer_experts(xt_bf, u_bf, vt_bf, cnt, gw, r2, w2):
    d, n = xt_bf.shape
    n_exp = u_bf.shape[0]
    tm = PEER_TM
    sel_spec = pl.BlockSpec((PEER_HEADS, PEER_TA, tm), lambda i, e: (0, e, i))
    all_spec = pl.BlockSpec((PEER_HEADS, PEER_KEYS, tm), lambda i, e: (0, 0, i))
    return pl.pallas_call(
        _peer_expert_kernel,
        grid=(n // tm, n_exp // PEER_TE),
        in_specs=[pl.BlockSpec((d, tm), lambda i, e: (0, i)),
                  pl.BlockSpec((PEER_TE, d), lambda i, e: (e, 0)),
                  pl.BlockSpec((d, PEER_TE), lambda i, e: (0, e)),
                  sel_spec, sel_spec, all_spec, all_spec],
        out_specs=pl.BlockSpec((d, tm), lambda i, e: (0, i)),
        out_shape=jax.ShapeDtypeStruct((d, n), F32),
        scratch_shapes=[pltpu.VMEM((PEER_TE, tm), F32), pltpu.VMEM((PEER_TE, tm), BF16)],
        compiler_params=pltpu.CompilerParams(
            dimension_semantics=("parallel", "arbitrary"), vmem_limit_bytes=VMEM_LIMIT),
        name="peer_experts",
    )(xt_bf, u_bf, vt_bf, cnt, gw, r2, w2)


def _norm_matmul_t_kernel(x_ref, g_ref, w_ref, o_ref, xt_ref, xn_ref):
    @pl.when(pl.program_id(1) == 0)
    def _():
        x = x_ref[...]
        ms = jnp.mean(x * x, axis=-1, keepdims=True)
        xn = x * lax.rsqrt(ms + EPS) * g_ref[...]
        xn_ref[...] = xn.astype(BF16)
        xt_ref[...] = xn.T.astype(BF16)

    o_ref[...] = jnp.dot(xn_ref[...], w_ref[...], preferred_element_type=F32).reshape(o_ref.shape)


def norm_matmul_t(x, gain, w_bf, cols):
    n, k = x.shape
    m = w_bf.shape[1]
    tm = PEER_TM
    return pl.pallas_call(
        _norm_matmul_t_kernel,
        grid=(n // tm, m // cols),
        in_specs=[pl.BlockSpec((tm, k), lambda i, j: (i, 0)),
                  pl.BlockSpec((1, k), lambda i, j: (0, 0)),
                  pl.BlockSpec((k, cols), lambda i, j: (0, j))],
        out_specs=[pl.BlockSpec((1, tm, cols), lambda i, j: (j, i, 0)),
                   pl.BlockSpec((k, tm), lambda i, j: (0, i))],
        out_shape=[jax.ShapeDtypeStruct((m // cols, n, cols), F32), jax.ShapeDtypeStruct((k, n), BF16)],
        scratch_shapes=[pltpu.VMEM((tm, k), BF16)],
        compiler_params=pltpu.CompilerParams(
            dimension_semantics=("parallel", "arbitrary"), vmem_limit_bytes=VMEM_LIMIT),
        name="norm_matmul_t",
    )(x, gain.reshape(1, k), w_bf)


def _rmsnorm_kernel(x_ref, g_ref, o_ref):
    x = x_ref[...]
    ms = jnp.mean(x * x, axis=-1, keepdims=True)
    o_ref[...] = (x * lax.rsqrt(ms + EPS) * g_ref[...]).astype(o_ref.dtype)


def rmsnorm(x, gain, dtype=F32):
    n, k = x.shape
    tm = _row_tile(n, ROW_TILE_CAP)
    return pl.pallas_call(
        _rmsnorm_kernel,
        grid=(n // tm,),
        in_specs=[pl.BlockSpec((tm, k), lambda i: (i, 0)), pl.BlockSpec((1, k), lambda i: (0, 0))],
        out_specs=pl.BlockSpec((tm, k), lambda i: (i, 0)),
        out_shape=jax.ShapeDtypeStruct((n, k), dtype),
        compiler_params=pltpu.CompilerParams(dimension_semantics=("parallel",), vmem_limit_bytes=VMEM_LIMIT),
        name="rmsnorm",
    )(x, gain.reshape(1, k))


def peer_layer(x, gain, w_q, keys, u, v):
    q_hm, xt_bf = norm_matmul_t(x, gain, w_q.astype(BF16), PEER_QDIM)
    cnt, gw, r2, w2 = peer_router(q_hm, keys)
    return x + peer_experts(xt_bf, u.astype(BF16), v.T.astype(BF16), cnt, gw, r2, w2).T


def _rope_tables(pos):
    half = ROPE_DIMS // 2
    inv = ROPE_THETA ** (-jnp.arange(half, dtype=F32) / half)
    ang = pos.astype(F32)[:, None] * inv[None, :]
    cos, sin = jnp.cos(ang), jnp.sin(ang)
    t = pos.shape[0]
    ones = jnp.ones((t, NSA_HEAD_DIM - ROPE_DIMS), F32)
    zeros = jnp.zeros((t, NSA_HEAD_DIM - ROPE_DIMS), F32)
    zh = jnp.zeros((t, half), F32)
    c = jnp.concatenate([cos, cos, ones], axis=1)
    s_lo = jnp.concatenate([-sin, zh, zeros], axis=1)
    s_hi = jnp.concatenate([zh, sin, zeros], axis=1)
    return c, s_lo, s_hi


def _rope_kernel(p_ref, c_ref, sl_ref, sh_ref, q_ref, rows_ref, win_ref, rows_bf_ref, win_bf_ref):
    c, sl, sh = c_ref[...], sl_ref[...], sh_ref[...]
    half = ROPE_DIMS // 2

    def rot(x):
        return x * c + pltpu.roll(x, LANES - half, 1) * sl + pltpu.roll(x, half, 1) * sh

    d = NSA_HEAD_DIM
    scale = d ** -0.5
    for hd in range(NSA_HEADS):
        q_ref[:, hd * d:(hd + 1) * d] = (rot(p_ref[:, hd * d:(hd + 1) * d]) * scale).astype(BF16)
    for blk in range(12):
        x = p_ref[:, E_Q + blk * d:E_Q + (blk + 1) * d]
        if (blk // 2) % 2 == 0:
            x = rot(x)
        if blk < 8:
            rows_ref[:, blk * d:(blk + 1) * d] = x
            rows_bf_ref[:, blk * d:(blk + 1) * d] = x.astype(BF16)
        else:
            win_ref[:, (blk - 8) * d:(blk - 7) * d] = x
            win_bf_ref[:, (blk - 8) * d:(blk - 7) * d] = x.astype(BF16)


def nsa_rope(proj, pos_rows):
    n = proj.shape[0]
    tm = _row_tile(n, ROPE_ROWS_CAP)
    c, sl, sh = _rope_tables(pos_rows)
    width = E_Q + E_KV
    tab = pl.BlockSpec((tm, LANES), lambda i: (i, 0))
    return pl.pallas_call(
        _rope_kernel,
        grid=(n // tm,),
        in_specs=[pl.BlockSpec((tm, width), lambda i: (i, 0)), tab, tab, tab],
        out_specs=[pl.BlockSpec((tm, E_Q), lambda i: (i, 0)),
                   pl.BlockSpec((tm, 1024), lambda i: (i, 0)),
                   pl.BlockSpec((tm, 512), lambda i: (i, 0)),
                   pl.BlockSpec((tm, 1024), lambda i: (i, 0)),
                   pl.BlockSpec((tm, 512), lambda i: (i, 0))],
        out_shape=[jax.ShapeDtypeStruct((n, E_Q), BF16),
                   jax.ShapeDtypeStruct((n, 1024), F32),
                   jax.ShapeDtypeStruct((n, 512), F32),
                   jax.ShapeDtypeStruct((n, 1024), BF16),
                   jax.ShapeDtypeStruct((n, 512), BF16)],
        compiler_params=pltpu.CompilerParams(dimension_semantics=("parallel",), vmem_limit_bytes=VMEM_LIMIT),
        name="nsa_rope",
    )(proj, c, sl, sh)


def _compress_kernel(x_ref, pe_ref, w1_ref, w2_ref, o_ref):
    n_sub = x_ref.shape[0] // CMP_STRIDE
    acc0 = jnp.zeros((n_sub, NSA_HEAD_DIM), F32)
    acc1 = jnp.zeros((n_sub, NSA_HEAD_DIM), F32)
    for s in range(CMP_STRIDE):
        xs = x_ref[pl.ds(s, n_sub, stride=CMP_STRIDE), :]
        a0 = (xs + pe_ref[0, s:s + 1, :]).astype(BF16)
        a1 = (xs + pe_ref[0, CMP_STRIDE + s:CMP_STRIDE + s + 1, :]).astype(BF16)
        acc0 = acc0 + jnp.dot(a0, w1_ref[0, s], preferred_element_type=F32)
        acc1 = acc1 + jnp.dot(a1, w1_ref[0, CMP_STRIDE + s], preferred_element_type=F32)
    pre = acc0 + pltpu.roll(acc1, n_sub - 1, 0)
    o_ref[0, 0] = jnp.dot(_gelu_tanh(pre).astype(BF16), w2_ref[0], preferred_element_type=F32).astype(BF16)


def nsa_compress(rows, n_batch, t_len, cmp_pe, cmp_w1, cmp_w2):
    n_sub = t_len // CMP_STRIDE
    d = NSA_HEAD_DIM
    w1 = cmp_w1.reshape(2, CMP_BLOCK, d, d).astype(BF16)
    return pl.pallas_call(
        _compress_kernel,
        grid=(n_batch, 4),
        in_specs=[pl.BlockSpec((t_len, d), lambda b, c: (b, c)),
                  pl.BlockSpec((1, CMP_BLOCK, d), lambda b, c: (c // 2, 0, 0)),
                  pl.BlockSpec((1, CMP_BLOCK, d, d), lambda b, c: (c // 2, 0, 0, 0)),
                  pl.BlockSpec((1, d, d), lambda b, c: (c // 2, 0, 0))],
        out_specs=pl.BlockSpec((1, 1, n_sub, d), lambda b, c: (b, c, 0, 0)),
        out_shape=jax.ShapeDtypeStruct((n_batch, 4, n_sub, d), BF16),
        compiler_params=pltpu.CompilerParams(
            dimension_semantics=("parallel", "parallel"), vmem_limit_bytes=VMEM_LIMIT),
        name="nsa_compress",
    )(rows, cmp_pe, w1, cmp_w2.astype(BF16))


NSA_TK = 512
NSA_WTILES = WINDOW // Q_BLOCK + 1


def _masked_softmax_rows(s, mask):
    s = jnp.where(mask, s, -1e30)
    p = jnp.exp(s - jnp.max(s, axis=-1, keepdims=True))
    p = jnp.where(mask, p, 0.0)
    return p / jnp.maximum(jnp.sum(p, axis=-1, keepdims=True), 1e-30)


def _nsa_prompt_kernel(q_ref, kc_ref, vc_ref, ks_ref, vs_ref, kw_ref, vw_ref, gate_ref, o_ref):
    g = pl.program_id(1)
    qi = pl.program_id(2)
    d, r, qb = NSA_HEAD_DIM, NSA_GROUP, Q_BLOCK
    n_cmp = kc_ref.shape[2]
    n_slc = ks_ref.shape[0] // SLC_BLOCK
    nt = (((1,), (1,)), ((), ()))
    q = jnp.concatenate([q_ref[:, h * d:(h + 1) * d] for h in range(r)], axis=0)
    t_col = qi * qb + lax.broadcasted_iota(jnp.int32, (qb, 1), 0)

    s = lax.dot_general(q, kc_ref[0, 0], nt, preferred_element_type=F32).reshape(r, qb, n_cmp)
    n_idx = lax.broadcasted_iota(jnp.int32, (qb, n_cmp), 1)
    c_mask = (n_idx * CMP_STRIDE + (CMP_BLOCK - 1) <= t_col)[None]
    p_c = _masked_softmax_rows(s, c_mask)
    o_cmp = jnp.dot(p_c.reshape(r * qb, n_cmp).astype(BF16), vc_ref[0, 0], preferred_element_type=F32)

    nb = LANES
    assert n_slc <= nb and qb == LANES
    p_sum = jnp.sum(p_c, axis=0)
    si = lax.broadcasted_iota(jnp.int32, (nb, n_cmp), 0) * SLC_BLOCK
    ci = lax.broadcasted_iota(jnp.int32, (nb, n_cmp), 1) * CMP_STRIDE
    overlap_t = jnp.where((ci < si + SLC_BLOCK) & (ci + CMP_BLOCK > si), 1.0, 0.0).astype(BF16)
    p_hi = p_sum.astype(BF16)
    p_lo = (p_sum - p_hi.astype(F32)).astype(BF16)
    imp_t = (lax.dot_general(overlap_t, p_hi, nt, preferred_element_type=F32)
             + lax.dot_general(overlap_t, p_lo, nt, preferred_element_type=F32))
    blk = lax.broadcasted_iota(jnp.int32, (nb, qb), 0)
    blk_f = blk.astype(F32)
    t_row = qi * qb + lax.broadcasted_iota(jnp.int32, (1, qb), 1)
    cur = t_row // SLC_BLOCK
    forced = (blk == 0) | (blk == cur) | (blk == cur - 1)
    work = jnp.where(blk * SLC_BLOCK <= t_row, jnp.where(forced, 1e6, imp_t), -1e6)
    work = jnp.where(blk < n_slc, work, NEG_INF)
    sel_t = jnp.zeros((nb, qb), F32)
    for _ in range(min(SLC_TOPK, n_slc)):
        m = jnp.max(work, axis=0, keepdims=True)
        first = jnp.min(jnp.where(work == m, blk_f, float(nb)), axis=0, keepdims=True)
        pick = blk_f == first
        sel_t = jnp.where(pick, 1.0, sel_t)
        work = jnp.where(pick, NEG_INF, work)
    sel_bf = sel_t.T.astype(BF16)

    tk = NSA_TK
    bpt = tk // SLC_BLOCK

    def slc_step(kt, carry):
        m_run, l_run, acc = carry
        start = pl.multiple_of(kt * tk, tk)
        k = ks_ref[pl.ds(start, tk), :]
        v = vs_ref[pl.ds(start, tk), :]
        s = lax.dot_general(q, k, nt, preferred_element_type=F32).reshape(r, qb, tk)
        ei = lax.broadcasted_iota(jnp.int32, (nb, tk), 0)
        ej = lax.broadcasted_iota(jnp.int32, (nb, tk), 1)
        expand = jnp.where(ei == kt * bpt + ej // SLC_BLOCK, 1.0, 0.0).astype(BF16)
        picked = jnp.dot(sel_bf, expand, preferred_element_type=F32)
        kpos = start + lax.broadcasted_iota(jnp.int32, (qb, tk), 1)
        mask = ((picked > 0.5) & (kpos <= t_col))[None]
        s = jnp.where(mask, s, -1e30)
        m_new = jnp.maximum(m_run, jnp.max(s, axis=-1, keepdims=True))
        alpha = jnp.exp(m_run - m_new)
        p = jnp.where(mask, jnp.exp(s - m_new), 0.0)
        l_new = alpha * l_run + jnp.sum(p, axis=-1, keepdims=True)
        pv = jnp.dot(p.reshape(r * qb, tk).astype(BF16), v, preferred_element_type=F32)
        acc = alpha.reshape(r * qb, 1) * acc + pv
        return m_new, l_new, acc

    init = (jnp.full((r, qb, 1), -1e30, F32), jnp.zeros((r, qb, 1), F32), jnp.zeros((r * qb, d), F32))
    _, l_fin, acc = lax.fori_loop(0, (qi * qb) // tk + 1, slc_step, init)
    o_slc = acc / jnp.maximum(l_fin.reshape(r * qb, 1), 1e-30)

    k_tiles, v_tiles, pos_tiles = [], [], []
    for j in range(NSA_WTILES):
        kt = qi - (NSA_WTILES - 1) + j
        ktc = jnp.maximum(kt, 0)
        start = pl.multiple_of(ktc * qb, qb)
        k_tiles.append(kw_ref[pl.ds(start, qb), :])
        v_tiles.append(vw_ref[pl.ds(start, qb), :])
        lane = lax.broadcasted_iota(jnp.int32, (qb, qb), 1)
        pos_tiles.append(jnp.where(kt >= 0, start + lane, -1))
    k_w = jnp.concatenate(k_tiles, axis=0)
    v_w = jnp.concatenate(v_tiles, axis=0)
    k_pos = jnp.concatenate(pos_tiles, axis=1)
    span = NSA_WTILES * qb
    s = lax.dot_general(q, k_w, nt, preferred_element_type=F32).reshape(r, qb, span)
    dpos = t_col - k_pos
    w_mask = ((dpos >= 0) & (dpos < WINDOW) & (k_pos >= 0))[None]
    p_w = _masked_softmax_rows(s, w_mask)
    o_win = jnp.dot(p_w.reshape(r * qb, span).astype(BF16), v_w, preferred_element_type=F32)

    sig = jax.nn.sigmoid(gate_ref[...])
    lane = lax.broadcasted_iota(jnp.int32, sig.shape, 1)
    for h in range(r):
        rows = slice(h * qb, (h + 1) * qb)
        out = jnp.zeros((qb, d), F32)
        for branch, o_b in enumerate((o_cmp, o_slc, o_win)):
            col = branch * NSA_HEADS + g * r + h
            gate = jnp.sum(jnp.where(lane == col, sig, 0.0), axis=-1, keepdims=True)
            out = out + gate * o_b[rows]
        o_ref[:, h * d:(h + 1) * d] = out.astype(o_ref.dtype)


def nsa_prompt(q_bf, kvc, rows_bf, win_bf, proj, n_batch, t_len):
    d, r, qb = NSA_HEAD_DIM, NSA_GROUP, Q_BLOCK
    nqb = t_len // qb
    n_sub = kvc.shape[2]
    gate_blk = GATE_COL0 // LANES
    seq = lambda col: pl.BlockSpec((t_len, d), lambda b, g, i: (b, col(g)))
    return pl.pallas_call(
        _nsa_prompt_kernel,
        grid=(n_batch, NSA_KV_HEADS, nqb),
        in_specs=[pl.BlockSpec((qb, r * d), lambda b, g, i: (b * nqb + i, g)),
                  pl.BlockSpec((1, 1, n_sub, d), lambda b, g, i: (b, g, 0, 0)),
                  pl.BlockSpec((1, 1, n_sub, d), lambda b, g, i: (b, 2 + g, 0, 0)),
                  seq(lambda g: 4 + g), seq(lambda g: 6 + g),
                  seq(lambda g: g), seq(lambda g: 2 + g),
                  pl.BlockSpec((qb, LANES), lambda b, g, i: (b * nqb + i, gate_blk))],
        out_specs=pl.BlockSpec((qb, r * d), lambda b, g, i: (b * nqb + i, g)),
        out_shape=jax.ShapeDtypeStruct((n_batch * t_len, E_Q), BF16),
        compiler_params=pltpu.CompilerParams(
            dimension_semantics=("parallel", "parallel", "arbitrary"), vmem_limit_bytes=VMEM_LIMIT),
        name="nsa_prompt",
    )(q_bf, kvc, kvc, rows_bf, rows_bf, win_bf, win_bf, proj)


SAMPLE_PAGES = 16
NEW_ROWS_BLK = 16


def _page_specs(kind):
    def spec(i):
        return pl.BlockSpec((1, PAGE_SIZE, 1, NSA_KV_HEADS, NSA_HEAD_DIM),
                            lambda s, c, pt: (pt[s, c * SAMPLE_PAGES + i], 0, kind, 0, 0))
    return [spec(i) for i in range(SAMPLE_PAGES)]


def _sample_compress_kernel(pt_ref, *refs):
    n_in = 2 * SAMPLE_PAGES
    pages = refs[:n_in]
    pe_ref, w1_ref, a0_ref, a1_ref = refs[n_in:]
    d = NSA_HEAD_DIM
    per_page = PAGE_SIZE // CMP_STRIDE
    rows = SAMPLE_PAGES * per_page
    for kv in range(2):
        acc0 = jnp.zeros((NSA_KV_HEADS * rows, d), F32)
        acc1 = jnp.zeros((NSA_KV_HEADS * rows, d), F32)
        for s in range(CMP_STRIDE):
            pieces = []
            for g in range(NSA_KV_HEADS):
                for p in range(SAMPLE_PAGES):
                    pieces.append(pages[kv * SAMPLE_PAGES + p][0, pl.ds(s, per_page, stride=CMP_STRIDE), 0, g, :])
            xs = jnp.concatenate(pieces, axis=0)
            a0 = (xs + pe_ref[kv, s:s + 1, :]).astype(BF16)
            a1 = (xs + pe_ref[kv, CMP_STRIDE + s:CMP_STRIDE + s + 1, :]).astype(BF16)
            acc0 = acc0 + jnp.dot(a0, w1_ref[kv, s], preferred_element_type=F32)
            acc1 = acc1 + jnp.dot(a1, w1_ref[kv, CMP_STRIDE + s], preferred_element_type=F32)
        for g in range(NSA_KV_HEADS):
            a0_ref[0, kv * NSA_KV_HEADS + g] = acc0[g * rows:(g + 1) * rows]
            a1_ref[0, kv * NSA_KV_HEADS + g] = acc1[g * rows:(g + 1) * rows]


def nsa_sample_compress(cache, page_table, cmp_pe, cmp_w1):
    bs, n_pages = page_table.shape
    d = NSA_HEAD_DIM
    per_page = PAGE_SIZE // CMP_STRIDE
    n_sub = n_pages * per_page
    rows = SAMPLE_PAGES * per_page
    w1 = cmp_w1.reshape(2, CMP_BLOCK, d, d).astype(BF16)
    out = jax.ShapeDtypeStruct((bs, 4, n_sub, d), F32)
    ospec = pl.BlockSpec((1, 4, rows, d), lambda s, c, pt: (s, 0, c, 0))
    page_specs = _page_specs(0) + _page_specs(1)
    return pl.pallas_call(
        _sample_compress_kernel,
        grid_spec=pltpu.PrefetchScalarGridSpec(
            num_scalar_prefetch=1,
            grid=(bs, n_pages // SAMPLE_PAGES),
            in_specs=page_specs + [
                pl.BlockSpec((2, CMP_BLOCK, d), lambda s, c, pt: (0, 0, 0)),
                pl.BlockSpec((2, CMP_BLOCK, d, d), lambda s, c, pt: (0, 0, 0, 0))],
            out_specs=[ospec, ospec]),
        out_shape=[out, out],
        compiler_params=pltpu.CompilerParams(
            dimension_semantics=("parallel", "arbitrary"), vmem_limit_bytes=VMEM_LIMIT),
        name="nsa_sample_compress",
    )(page_table, *([cache] * (2 * SAMPLE_PAGES)), cmp_pe, w1)


def _sample_select_kernel(a0_ref, a1_ref, w2_ref, q_ref, ocmp_ref, sel_ref, *, past_len, t_len):
    d, r = NSA_HEAD_DIM, NSA_GROUP
    n_sub = a0_ref.shape[2]
    n_cmp = n_sub - 1
    n_slc = -(-(past_len + t_len) // SLC_BLOCK)
    lanes = sel_ref.shape[3]
    nt = (((1,), (1,)), ((), ()))
    rq = r * t_len
    for g in range(NSA_KV_HEADS):
        kc = jnp.dot(_gelu_tanh(a0_ref[0, g] + pltpu.roll(a1_ref[0, g], n_sub - 1, 0)).astype(BF16), w2_ref[0],
                     preferred_element_type=F32).astype(BF16)
        vc = jnp.dot(_gelu_tanh(a0_ref[0, 2 + g] + pltpu.roll(a1_ref[0, 2 + g], n_sub - 1, 0)).astype(BF16),
                     w2_ref[1], preferred_element_type=F32).astype(BF16)
        q = q_ref[0, g]
        s = lax.dot_general(q, kc, nt, preferred_element_type=F32)
        pos = past_len + lax.broadcasted_iota(jnp.int32, (rq, 1), 0) % t_len
        n_idx = lax.broadcasted_iota(jnp.int32, (rq, n_sub), 1)
        p_c = _masked_softmax_rows(s, (n_idx * CMP_STRIDE + (CMP_BLOCK - 1) <= pos) & (n_idx < n_cmp))
        ocmp_ref[0, g] = jnp.dot(p_c.astype(BF16), vc, preferred_element_type=F32)
        ri = lax.broadcasted_iota(jnp.int32, (8, rq), 0)
        cj = lax.broadcasted_iota(jnp.int32, (8, rq), 1)
        head_sum = jnp.where(cj % t_len == ri, 1.0, 0.0).astype(BF16)
        p_hi = p_c.astype(BF16)
        p_lo = (p_c - p_hi.astype(F32)).astype(BF16)
        p_sum = (jnp.dot(head_sum, p_hi, preferred_element_type=F32)
                 + jnp.dot(head_sum, p_lo, preferred_element_type=F32))
        ci = lax.broadcasted_iota(jnp.int32, (n_sub, lanes), 0)
        mi = lax.broadcasted_iota(jnp.int32, (n_sub, lanes), 1)
        overlap = jnp.where((ci * CMP_STRIDE < mi * SLC_BLOCK + SLC_BLOCK)
                            & (ci * CMP_STRIDE + CMP_BLOCK > mi * SLC_BLOCK) & (ci < n_cmp), 1.0, 0.0).astype(BF16)
        s_hi = p_sum.astype(BF16)
        s_lo = (p_sum - s_hi.astype(F32)).astype(BF16)
        imp = (jnp.dot(s_hi, overlap, preferred_element_type=F32)
               + jnp.dot(s_lo, overlap, preferred_element_type=F32))
        blk = lax.broadcasted_iota(jnp.int32, (8, lanes), 1)
        blk_f = blk.astype(F32)
        tpos = past_len + lax.broadcasted_iota(jnp.int32, (8, 1), 0) % t_len
        cur = tpos // SLC_BLOCK
        forced = (blk == 0) | (blk == cur) | (blk == cur - 1)
        work = jnp.where(blk * SLC_BLOCK <= tpos, jnp.where(forced, 1e6, imp), -1e6)
        work = jnp.where(blk < n_slc, work, NEG_INF)
        sel = jnp.zeros((8, lanes), F32)
        for _ in range(min(SLC_TOPK, n_slc)):
            m = jnp.max(work, axis=-1, keepdims=True)
            first = jnp.min(jnp.where(work == m, blk_f, float(lanes)), axis=-1, keepdims=True)
            pick = blk_f == first
            sel = jnp.where(pick, 1.0, sel)
            work = jnp.where(pick, NEG_INF, work)
        sel_ref[0, g] = sel


def nsa_sample_select(a0, a1, cmp_w2, q_s, past_len, t_len):
    bs, _, n_sub, d = a0.shape
    n_slc = -(-(past_len + t_len) // SLC_BLOCK)
    lanes = _round_up(n_slc, LANES)
    rq = NSA_GROUP * t_len
    aspec = pl.BlockSpec((1, 4, n_sub, d), lambda s: (s, 0, 0, 0))
    return pl.pallas_call(
        functools.partial(_sample_select_kernel, past_len=past_len, t_len=t_len),
        grid=(bs,),
        in_specs=[aspec, aspec, pl.BlockSpec((2, d, d), lambda s: (0, 0, 0)),
                  pl.BlockSpec((1, NSA_KV_HEADS, rq, d), lambda s: (s, 0, 0, 0))],
        out_specs=[pl.BlockSpec((1, NSA_KV_HEADS, rq, d), lambda s: (s, 0, 0, 0)),
                   pl.BlockSpec((1, NSA_KV_HEADS, 8, lanes), lambda s: (s, 0, 0, 0))],
        out_shape=[jax.ShapeDtypeStruct((bs, NSA_KV_HEADS, rq, d), F32),
                   jax.ShapeDtypeStruct((bs, NSA_KV_HEADS, 8, lanes), F32)],
        compiler_params=pltpu.CompilerParams(dimension_semantics=("parallel",), vmem_limit_bytes=VMEM_LIMIT),
        name="nsa_sample_select",
    )(a0, a1, cmp_w2.astype(BF16), q_s)


def _sample_attend_kernel(pt_ref, *refs, past_len, t_len, row0):
    k_pages = refs[:SAMPLE_PAGES]
    v_pages = refs[SAMPLE_PAGES:2 * SAMPLE_PAGES]
    (q_ref, sel_ref, ocmp_ref, newrows_ref, kwin_ref, vwin_ref, newwin_ref, gate_ref, o_ref,
     m_scr, l_scr, acc_scr) = refs[2 * SAMPLE_PAGES:]
    s_idx = pl.program_id(0)
    c = pl.program_id(1)
    d, r = NSA_HEAD_DIM, NSA_GROUP
    rq = r * t_len
    lanes = sel_ref.shape[3]
    tk = SAMPLE_PAGES * PAGE_SIZE
    nt = (((1,), (1,)), ((), ()))

    @pl.when(c == 0)
    def _():
        m_scr[...] = jnp.full(m_scr.shape, -1e30, F32)
        l_scr[...] = jnp.zeros(l_scr.shape, F32)
        acc_scr[...] = jnp.zeros(acc_scr.shape, F32)

    row = lax.broadcasted_iota(jnp.int32, (rq, 1), 0)
    t_row = row % t_len
    pos = past_len + t_row
    ti = lax.broadcasted_iota(jnp.int32, (rq, 8), 1)
    tok_expand = jnp.where(ti == t_row, 1.0, 0.0).astype(BF16)

    def online(g, s, mask, v):
        s = jnp.where(mask, s, -1e30)
        m_old = m_scr[g]
        m_new = jnp.maximum(m_old, jnp.max(s, axis=-1, keepdims=True))
        alpha = jnp.exp(m_old - m_new)
        p = jnp.where(mask, jnp.exp(s - m_new), 0.0)
        l_scr[g] = alpha * l_scr[g] + jnp.sum(p, axis=-1, keepdims=True)
        acc_scr[g] = alpha * acc_scr[g] + jnp.dot(p.astype(BF16), v, preferred_element_type=F32)
        m_scr[g] = m_new

    sel16 = []
    for g in range(NSA_KV_HEADS):
        sel16.append(jnp.dot(tok_expand, sel_ref[0, g].astype(BF16), preferred_element_type=F32))
        k = jnp.concatenate([pg[0, :, 0, g, :] for pg in k_pages], axis=0)
        v = jnp.concatenate([pg[0, :, 0, g, :] for pg in v_pages], axis=0)
        q = q_ref[0, g]
        s = lax.dot_general(q, k.astype(BF16), nt, preferred_element_type=F32)
        ei = lax.broadcasted_iota(jnp.int32, (lanes, tk), 0)
        ej = lax.broadcasted_iota(jnp.int32, (lanes, tk), 1)
        expand = jnp.where(ei == c * (tk // SLC_BLOCK) + ej // SLC_BLOCK, 1.0, 0.0).astype(BF16)
        picked = jnp.dot(sel16[g].astype(BF16), expand, preferred_element_type=F32)
        kpos = c * tk + lax.broadcasted_iota(jnp.int32, (rq, tk), 1)
        online(g, s, (picked > 0.5) & (kpos <= pos), v.astype(BF16))

    @pl.when(c == pl.num_programs(1) - 1)
    def _():
        mine = (row0 // t_len + s_idx) % (NEW_ROWS_BLK // t_len)
        j = lax.broadcasted_iota(jnp.int32, (rq, NEW_ROWS_BLK), 1)
        own = j // t_len == mine
        new_pos = past_len + j % t_len
        sig = jax.nn.sigmoid(gate_ref[...])
        gsel = jnp.where(j == mine * t_len + t_row, 1.0, 0.0).astype(BF16)
        s_hi = sig.astype(BF16)
        s_lo = (sig - s_hi.astype(F32)).astype(BF16)
        sig_rows = (jnp.dot(gsel, s_hi, preferred_element_type=F32)
                    + jnp.dot(gsel, s_lo, preferred_element_type=F32))
        lane = lax.broadcasted_iota(jnp.int32, sig_rows.shape, 1)
        lane_l = lax.broadcasted_iota(jnp.int32, (rq, lanes), 1)
        for g in range(NSA_KV_HEADS):
            q = q_ref[0, g]
            kn = newrows_ref[:, (4 + g) * d:(5 + g) * d].astype(BF16)
            vn = newrows_ref[:, (6 + g) * d:(7 + g) * d].astype(BF16)
            s = lax.dot_general(q, kn, nt, preferred_element_type=F32)
            last_picked = jnp.sum(jnp.where(lane_l == past_len // SLC_BLOCK, sel16[g], 0.0), axis=-1, keepdims=True)
            online(g, s, own & (new_pos <= pos) & (last_picked > 0.5), vn)
            o_slc = acc_scr[g] / jnp.maximum(l_scr[g], 1e-30)
            w_len = kwin_ref.shape[1]
            kw = kwin_ref[0, :, 0, g, :].astype(BF16)
            vw = vwin_ref[0, :, 0, g, :].astype(BF16)
            s1 = lax.dot_general(q, kw, nt, preferred_element_type=F32)
            kp1 = past_len - w_len + lax.broadcasted_iota(jnp.int32, (rq, w_len), 1)
            d1 = pos - kp1
            m1 = (d1 >= 0) & (d1 < WINDOW) & (kp1 >= 0)
            knw = newwin_ref[:, g * d:(g + 1) * d].astype(BF16)
            vnw = newwin_ref[:, (2 + g) * d:(3 + g) * d].astype(BF16)
            s2 = lax.dot_general(q, knw, nt, preferred_element_type=F32)
            d2 = pos - new_pos
            m2 = own & (d2 >= 0) & (d2 < WINDOW)
            s1 = jnp.where(m1, s1, -1e30)
            s2 = jnp.where(m2, s2, -1e30)
            mx = jnp.maximum(jnp.max(s1, axis=-1, keepdims=True), jnp.max(s2, axis=-1, keepdims=True))
            p1 = jnp.where(m1, jnp.exp(s1 - mx), 0.0)
            p2 = jnp.where(m2, jnp.exp(s2 - mx), 0.0)
            den = jnp.sum(p1, axis=-1, keepdims=True) + jnp.sum(p2, axis=-1, keepdims=True)
            o_win = (jnp.dot(p1.astype(BF16), vw, preferred_element_type=F32)
                     + jnp.dot(p2.astype(BF16), vnw, preferred_element_type=F32)) / jnp.maximum(den, 1e-30)
            out = jnp.zeros((rq, d), F32)
            for branch, o_b in enumerate((ocmp_ref[0, g], o_slc, o_win)):
                col = branch * NSA_HEADS + g * r + row // t_len
                gate = jnp.sum(jnp.where(lane == col, sig_rows, 0.0), axis=-1, keepdims=True)
                out = out + gate * o_b
            o_ref[0, g] = out


def nsa_sample_attend(cache, page_table, q_s, sel, o_cmp, rows, win, win_cache, proj, past_len, t_len, row0):
    bs, n_pages = page_table.shape
    d = NSA_HEAD_DIM
    rq = NSA_GROUP * t_len
    lanes = sel.shape[3]
    w_len = win_cache.shape[1]
    assert row0 % t_len == 0 and NEW_ROWS_BLK % t_len == 0 and past_len % SLC_BLOCK == 0
    blk = lambda s: (row0 + s * t_len) // NEW_ROWS_BLK
    per_seq = lambda shape: pl.BlockSpec((1,) + shape, lambda s, c, pt: (s, 0, 0, 0))
    return pl.pallas_call(
        functools.partial(_sample_attend_kernel, past_len=past_len, t_len=t_len, row0=row0),
        grid_spec=pltpu.PrefetchScalarGridSpec(
            num_scalar_prefetch=1,
            grid=(bs, n_pages // SAMPLE_PAGES),
            in_specs=_page_specs(2) + _page_specs(3) + [
                per_seq((NSA_KV_HEADS, rq, d)), per_seq((NSA_KV_HEADS, 8, lanes)), per_seq((NSA_KV_HEADS, rq, d)),
                pl.BlockSpec((NEW_ROWS_BLK, 8 * d), lambda s, c, pt: (blk(s), 0)),
                pl.BlockSpec((1, w_len, 1, NSA_KV_HEADS, d), lambda s, c, pt: (s, 0, 0, 0, 0)),
                pl.BlockSpec((1, w_len, 1, NSA_KV_HEADS, d), lambda s, c, pt: (s, 0, 1, 0, 0)),
                pl.BlockSpec((NEW_ROWS_BLK, 4 * d), lambda s, c, pt: (blk(s), 0)),
                pl.BlockSpec((NEW_ROWS_BLK, LANES), lambda s, c, pt: (blk(s), GATE_COL0 // LANES))],
            out_specs=per_seq((NSA_KV_HEADS, rq, d)),
            scratch_shapes=[pltpu.VMEM((NSA_KV_HEADS, rq, 1), F32), pltpu.VMEM((NSA_KV_HEADS, rq, 1), F32),
                            pltpu.VMEM((NSA_KV_HEADS, rq, d), F32)]),
        out_shape=jax.ShapeDtypeStruct((bs, NSA_KV_HEADS, rq, d), F32),
        compiler_params=pltpu.CompilerParams(
            dimension_semantics=("parallel", "arbitrary"), vmem_limit_bytes=VMEM_LIMIT),
        name="nsa_sample_attend",
    )(page_table, *([cache] * (2 * SAMPLE_PAGES)), q_s, sel, o_cmp, rows, win_cache, win_cache, win, proj)


SC_ROWS = 512
SC_COLS = 512


def _short_conv_kernel(b_ref, c_ref, h_ref, w_ref, o_ref, tail_ref, prev_scr):
    @pl.when(pl.program_id(2) == 0)
    def _():
        prev_scr[...] = jnp.zeros_like(prev_scr)

    rows = c_ref.shape[0]
    u = c_ref[...] * h_ref[...]
    prev = prev_scr[...]
    row8 = lax.broadcasted_iota(jnp.int32, prev.shape, 0)
    conv = u * w_ref[SC_KSIZE - 1:SC_KSIZE, :]
    for k in range(1, SC_KSIZE):
        rolled = pltpu.roll(u, k, 0)
        top = jnp.where(row8 < k, pltpu.roll(prev, k, 0), rolled[0:8])
        shifted = jnp.concatenate([top, rolled[8:]], axis=0)
        conv = conv + shifted * w_ref[SC_KSIZE - 1 - k:SC_KSIZE - k, :]
    prev_scr[...] = u[rows - 8:rows]
    o_ref[...] = (b_ref[...] * conv).astype(o_ref.dtype)
    tail_ref[0] = u[rows - 8:rows]


def short_conv_prompt(proj, n_batch, t_len, sc_w):
    nr = t_len // SC_ROWS
    nh = SC_WIDTH // SC_COLS
    blk0 = SC_COL0 // SC_COLS
    assert SC_COL0 % SC_COLS == 0
    col = lambda part: pl.BlockSpec((SC_ROWS, SC_COLS), lambda b, j, i: (b * nr + i, blk0 + part * nh + j))
    out, tail = pl.pallas_call(
        _short_conv_kernel,
        grid=(n_batch, nh, nr),
        in_specs=[col(0), col(1), col(2), pl.BlockSpec((SC_KSIZE, SC_COLS), lambda b, j, i: (0, j))],
        out_specs=[pl.BlockSpec((SC_ROWS, SC_COLS), lambda b, j, i: (b * nr + i, j)),
                   pl.BlockSpec((1, 8, SC_COLS), lambda b, j, i: (b, 0, j))],
        out_shape=[jax.ShapeDtypeStruct((n_batch * t_len, SC_WIDTH), BF16),
                   jax.ShapeDtypeStruct((n_batch, 8, SC_WIDTH), F32)],
        scratch_shapes=[pltpu.VMEM((8, SC_COLS), F32)],
        compiler_params=pltpu.CompilerParams(
            dimension_semantics=("parallel", "parallel", "arbitrary"), vmem_limit_bytes=VMEM_LIMIT),
        name="short_conv",
    )(proj, proj, proj, sc_w)
    return out, tail[:, 8 - (SC_KSIZE - 1):]


def _ret_tables(pos, chunk):
    half = RET_DQK // 2
    inv = RET_THETA ** (-jnp.arange(half, dtype=F32) / half)
    ang = pos.astype(F32)[:, None] * inv[None, :]
    log_g = jnp.log1p(-(2.0 ** (-5.0 - jnp.arange(RET_HEADS, dtype=F32))))
    i = jnp.arange(chunk, dtype=F32)
    diff = i[:, None] - i[None, :]
    intra = jnp.where(diff >= 0, jnp.exp(jnp.maximum(diff, 0.0)[None] * log_g[:, None, None]), 0.0)
    q_dec = jnp.exp((i[None, :] + 1.0) * log_g[:, None])[..., None]
    k_dec = jnp.exp((chunk - 1.0 - i)[None, :] * log_g[:, None])[..., None]
    c_dec = jnp.exp(chunk * log_g)[:, None, None]
    return jnp.cos(ang), jnp.sin(ang), intra, q_dec, k_dec, c_dec


def _retention_kernel(q_ref, k_ref, v_ref, g_ref, cos_ref, sin_ref, intra_ref, qd_ref, kd_ref, cd_ref, gn_ref,
                      s0_ref, o_ref, s_out_ref, s_scr):
    c_idx = pl.program_id(1)

    @pl.when(c_idx == 0)
    def _():
        s_scr[...] = s0_ref[0]

    cos, sin = cos_ref[...], sin_ref[...]
    half = RET_DQK // 2
    nt = (((1,), (1,)), ((), ()))
    tn = (((0,), (0,)), ((), ()))

    def rot(x):
        x1, x2 = x[:, :half], x[:, half:]
        return jnp.concatenate([x1 * cos - x2 * sin, x1 * sin + x2 * cos], axis=-1)

    for h in range(RET_HEADS):
        cols = slice(h * RET_DQK, (h + 1) * RET_DQK)
        qr = (rot(q_ref[:, cols]) * (RET_DQK ** -0.5)).astype(BF16)
        kr = rot(k_ref[:, cols])
        v = v_ref[:, cols].astype(BF16)
        att = lax.dot_general(qr, kr.astype(BF16), nt, preferred_element_type=F32) * intra_ref[h]
        s_old = s_scr[h]
        o = (jnp.dot(att.astype(BF16), v, preferred_element_type=F32)
             + jnp.dot(qr, s_old.astype(BF16), preferred_element_type=F32) * qd_ref[h])
        s_scr[h] = s_old * cd_ref[h] + lax.dot_general((kr * kd_ref[h]).astype(BF16), v, tn,
                                                       preferred_element_type=F32)
        mu = jnp.mean(o, axis=-1, keepdims=True)
        dev = o - mu
        var = jnp.mean(dev * dev, axis=-1, keepdims=True)
        gate = g_ref[:, cols]
        on = dev * lax.rsqrt(var + EPS) * gn_ref[:, cols] * (gate * jax.nn.sigmoid(gate))
        o_ref[:, cols] = on.astype(o_ref.dtype)
    s_out_ref[0] = s_scr[...]


def retention(proj, pos, n_batch, t_len, chunk, state0, gn_gain):
    nc = t_len // chunk
    cos, sin, intra, q_dec, k_dec, c_dec = _ret_tables(pos, chunk)
    half = RET_DQK // 2
    col = lambda j: pl.BlockSpec((chunk, R_QK), lambda b, c: (b * nc + c, j))
    tab = pl.BlockSpec((chunk, half), lambda b, c: (c, 0))
    full = lambda a: pl.BlockSpec(a.shape, lambda b, c: (0,) * a.ndim)
    st = pl.BlockSpec((1, RET_HEADS, RET_DQK, RET_DV), lambda b, c: (b, 0, 0, 0))
    return pl.pallas_call(
        _retention_kernel,
        grid=(n_batch, nc),
        in_specs=[col(0), col(1), col(2), col(3), tab, tab, full(intra), full(q_dec), full(k_dec), full(c_dec),
                  pl.BlockSpec((1, R_V), lambda b, c: (0, 0)), st],
        out_specs=[pl.BlockSpec((chunk, R_V), lambda b, c: (b * nc + c, 0)), st],
        out_shape=[jax.ShapeDtypeStruct((n_batch * t_len, R_V), BF16),
                   jax.ShapeDtypeStruct((n_batch, RET_HEADS, RET_DQK, RET_DV), F32)],
        scratch_shapes=[pltpu.VMEM((RET_HEADS, RET_DQK, RET_DV), F32)],
        compiler_params=pltpu.CompilerParams(
            dimension_semantics=("parallel", "arbitrary"), vmem_limit_bytes=VMEM_LIMIT),
        name="retention",
    )(proj, proj, proj, proj, cos, sin, intra, q_dec, k_dec, c_dec, gn_gain.reshape(1, R_V), state0)


SSD_COL0 = 2 * R_QK + 2 * R_V
HEADS_PER_GROUP = M2_HEADS // M2_GROUPS


def _split3(x):
    a = x.astype(BF16)
    r = x - a.astype(F32)
    b = r.astype(BF16)
    c = (r - b.astype(F32)).astype(BF16)
    return a, b, c


def _exact_dot(mat_bf, x):
    out = None
    for piece in _split3(x):
        t = jnp.dot(mat_bf, piece, preferred_element_type=F32)
        out = t if out is None else out + t
    return out


def _exact_dot_r(x, mat_bf):
    out = None
    for piece in _split3(x):
        t = jnp.dot(piece, mat_bf, preferred_element_type=F32)
        out = t if out is None else out + t
    return out


def _ssd_kernel(z_ref, xa_ref, xb_ref, xc_ref, dt_ref, cw_ref, cb_ref, dtb_ref, aneg_ref, dskip_ref, norm_ref,
                buf0_ref, s0_ref, o_ref, s_out_ref, s_scr, prev_scr):
    c_idx = pl.program_id(1)
    chunk = z_ref.shape[0]
    nt = (((1,), (1,)), ((), ()))
    tn = (((0,), (0,)), ((), ()))

    @pl.when(c_idx == 0)
    def _():
        s_scr[...] = s0_ref[0]
        prev_scr[...] = buf0_ref[0]

    x = jnp.concatenate([xa_ref[...], xb_ref[...], xc_ref[...]], axis=1)
    prev = prev_scr[...]
    row8 = lax.broadcasted_iota(jnp.int32, prev.shape, 0)
    conv = x * cw_ref[M2_CONV - 1:M2_CONV, :]
    for k in range(1, M2_CONV):
        rolled = pltpu.roll(x, k, 0)
        top = jnp.where(row8 < k, pltpu.roll(prev, k, 0), rolled[0:8])
        shifted = jnp.concatenate([top, rolled[8:]], axis=0)
        conv = conv + shifted * cw_ref[M2_CONV - 1 - k:M2_CONV - k, :]
    prev_scr[...] = x[chunk - 8:chunk]
    conv = conv + cb_ref[...]
    xbc = conv * jax.nn.sigmoid(conv)
    xs = xbc[:, :M2_DINNER]

    dt_raw = dt_ref[...] + dtb_ref[...]
    dt = jnp.where(dt_raw > 20.0, dt_raw, jnp.log1p(jnp.exp(jnp.minimum(dt_raw, 20.0))))
    a = dt * aneg_ref[...]
    ri = lax.broadcasted_iota(jnp.int32, (chunk, chunk), 0)
    ci = lax.broadcasted_iota(jnp.int32, (chunk, chunk), 1)
    tri = ri >= ci
    cum = _exact_dot(jnp.where(tri, 1.0, 0.0).astype(BF16), a)
    cum_t = cum.T
    cum_last = cum[chunk - 1:chunk, :]
    hi = lax.broadcasted_iota(jnp.int32, (LANES, M2_DINNER), 0)
    li = lax.broadcasted_iota(jnp.int32, (LANES, M2_DINNER), 1)
    expand = jnp.where(hi == li // M2_HEADDIM, 1.0, 0.0).astype(BF16)
    dt_x = _exact_dot_r(dt, expand)
    cum_x = _exact_dot_r(cum, expand)
    last_x = _exact_dot_r(cum_last, expand)
    xdt = xs * dt_x
    x_dec = (xdt * jnp.exp(last_x - cum_x)).astype(BF16)
    xdt_bf = xdt.astype(BF16)
    e_cum_x = jnp.exp(cum_x)
    e_last_x = jnp.exp(last_x)

    y_parts = []
    for gi in range(M2_GROUPS):
        b_g = xbc[:, M2_DINNER + gi * M2_STATE:M2_DINNER + (gi + 1) * M2_STATE].astype(BF16)
        c_g = xbc[:, M2_DINNER + (M2_GROUPS + gi) * M2_STATE:M2_DINNER + (M2_GROUPS + gi + 1) * M2_STATE].astype(BF16)
        cb = lax.dot_general(c_g, b_g, nt, preferred_element_type=F32)
        gcols = slice(gi * HEADS_PER_GROUP * M2_HEADDIM, (gi + 1) * HEADS_PER_GROUP * M2_HEADDIM)
        s_old = s_scr[:, gcols]
        y_state = jnp.dot(c_g, s_old.astype(BF16), preferred_element_type=F32) * e_cum_x[:, gcols]
        s_scr[:, gcols] = s_old * e_last_x[:, gcols] + lax.dot_general(b_g, x_dec[:, gcols], tn,
                                                                        preferred_element_type=F32)
        pair_lane = lax.broadcasted_iota(jnp.int32, (chunk, LANES), 1)
        intra = []
        for pr in range(HEADS_PER_GROUP // 2):
            outs = []
            for sub in range(2):
                h = gi * HEADS_PER_GROUP + pr * 2 + sub
                seg = cum[:, h:h + 1] - cum_t[h:h + 1, :]
                l_mat = jnp.where(tri, jnp.exp(jnp.where(tri, seg, 0.0)), 0.0)
                lanes = slice(gi * HEADS_PER_GROUP * M2_HEADDIM + pr * LANES,
                              gi * HEADS_PER_GROUP * M2_HEADDIM + (pr + 1) * LANES)
                outs.append(jnp.dot((cb * l_mat).astype(BF16), xdt_bf[:, lanes], preferred_element_type=F32))
            intra.append(jnp.where(pair_lane < M2_HEADDIM, outs[0], outs[1]))
        y_parts.append(jnp.concatenate(intra, axis=1) + y_state)
    y = jnp.concatenate(y_parts, axis=1) + dskip_ref[...] * xs
    z = z_ref[...]
    y = y * (z * jax.nn.sigmoid(z))
    gw = M2_DINNER // M2_GROUPS
    outs = []
    for gi in range(M2_GROUPS):
        yg = y[:, gi * gw:(gi + 1) * gw]
        outs.append(yg * lax.rsqrt(jnp.mean(yg * yg, axis=-1, keepdims=True) + EPS))
    o_ref[...] = (jnp.concatenate(outs, axis=1) * norm_ref[...]).astype(o_ref.dtype)
    s_out_ref[0] = s_scr[...]


def ssd(proj, n_batch, t_len, chunk, conv_buf, state0, conv_w, conv_b, dt_bias, a_log, d_skip, m2_norm):
    nc = t_len // chunk
    z_blk = SSD_COL0 // M2_DINNER
    xw = M2_CONV_DIM // 3
    xbc_blk = (SSD_COL0 + M2_DINNER) // xw
    assert (SSD_COL0 + M2_DINNER) % xw == 0 and xw % LANES == 0
    dt_blk = (SSD_COL0 + M2_DINNER + M2_CONV_DIM) // LANES
    pad = lambda v: jnp.pad(v.reshape(1, -1), ((0, 0), (0, LANES - v.shape[-1])))
    buf8 = jnp.pad(conv_buf, ((0, 0), (8 - (M2_CONV - 1), 0), (0, 0)))
    st_t = state0.transpose(0, 3, 1, 2).reshape(n_batch, M2_STATE, M2_DINNER)
    row = lambda a: pl.BlockSpec(a.shape, lambda b, c: (0, 0))
    cw = conv_w
    cb = conv_b.reshape(1, -1)
    dtb, aneg = pad(dt_bias), pad(-jnp.exp(a_log))
    dsk = jnp.repeat(d_skip, M2_HEADDIM).reshape(1, -1)
    nrm = m2_norm.reshape(1, -1)
    st = pl.BlockSpec((1, M2_STATE, M2_DINNER), lambda b, c: (b, 0, 0))
    out, s_fin = pl.pallas_call(
        _ssd_kernel,
        grid=(n_batch, nc),
        in_specs=[pl.BlockSpec((chunk, M2_DINNER), lambda b, c: (b * nc + c, z_blk)),
                  pl.BlockSpec((chunk, xw), lambda b, c: (b * nc + c, xbc_blk)),
                  pl.BlockSpec((chunk, xw), lambda b, c: (b * nc + c, xbc_blk + 1)),
                  pl.BlockSpec((chunk, xw), lambda b, c: (b * nc + c, xbc_blk + 2)),
                  pl.BlockSpec((chunk, LANES), lambda b, c: (b * nc + c, dt_blk)),
                  row(cw), row(cb), row(dtb), row(aneg), row(dsk), row(nrm),
                  pl.BlockSpec((1, 8, M2_CONV_DIM), lambda b, c: (b, 0, 0)), st],
        out_specs=[pl.BlockSpec((chunk, M2_DINNER), lambda b, c: (b * nc + c, 0)), st],
        out_shape=[jax.ShapeDtypeStruct((n_batch * t_len, M2_DINNER), BF16),
                   jax.ShapeDtypeStruct((n_batch, M2_STATE, M2_DINNER), F32)],
        scratch_shapes=[pltpu.VMEM((M2_STATE, M2_DINNER), F32), pltpu.VMEM((8, M2_CONV_DIM), F32)],
        compiler_params=pltpu.CompilerParams(
            dimension_semantics=("parallel", "arbitrary"), vmem_limit_bytes=VMEM_LIMIT),
        name="ssd",
    )(proj, proj, proj, proj, proj, cw, cb, dtb, aneg, dsk, nrm, buf8, st_t)
    s_fin = s_fin.reshape(n_batch, M2_STATE, M2_HEADS, M2_HEADDIM).transpose(0, 2, 3, 1)
    return out, s_fin


def _rotary(x, pos, rot_dims, theta):
    half = rot_dims // 2
    inv = theta ** (-jnp.arange(half, dtype=F32) / half)
    ang = pos.astype(F32)[:, None] * inv[None, :]
    shape = (pos.shape[0],) + (1,) * (x.ndim - 3) + (half,)
    cos = jnp.cos(ang).reshape(shape)
    sin = jnp.sin(ang).reshape(shape)
    x1 = x[..., :half]
    x2 = x[..., half:rot_dims]
    rot = jnp.concatenate([x1 * cos - x2 * sin, x1 * sin + x2 * cos], axis=-1)
    return jnp.concatenate([rot, x[..., rot_dims:]], axis=-1)


def _masked_softmax(s, mask, axis=-1):
    s = jnp.where(mask, s, -1e30)
    p = jnp.exp(s - jnp.max(s, axis=axis, keepdims=True)) * mask
    return p / jnp.maximum(jnp.sum(p, axis=axis, keepdims=True), 1e-30)


def _causal_conv(u, buf, w, b=None):
    k_size = w.shape[0]
    t_len = u.shape[1]
    ext = jnp.concatenate([buf.astype(u.dtype), u], axis=1)
    y = ext[:, 0:t_len] * w[0]
    for j in range(1, k_size):
        y = y + ext[:, j:j + t_len] * w[j]
    if b is not None:
        y = y + b
    return y, ext[:, t_len:]


def _to_chunks(a, chunk):
    return a.reshape(a.shape[0], a.shape[1] // chunk, chunk, *a.shape[2:]).swapaxes(0, 1)


def _nsa_compress_jnp(rows, pe, w1, w2):
    bn, length, g, d = rows.shape
    n_part = CMP_BLOCK // CMP_STRIDE
    n_sub = length // CMP_STRIDE
    n_cmp = n_sub - n_part + 1
    sub = rows[:, :n_sub * CMP_STRIDE].reshape(bn, n_sub, CMP_STRIDE, g, d)
    w1p = w1.reshape(n_part, CMP_STRIDE, d, d)
    pre = jnp.einsum('jsd,jsde->e', pe.reshape(n_part, CMP_STRIDE, d), w1p)
    for j in range(n_part):
        pre = pre + jnp.einsum('bnsgd,sde->bnge', sub[:, j:j + n_cmp], w1p[j])
    return jnp.einsum('bnge,ef->bngf', jax.nn.gelu(pre), w2)


def _nsa_sparse_jnp(q, q_pos, rows, cmp_pe, cmp_w1, cmp_w2):
    bn, t_len, g, r, d = q.shape
    length = rows.shape[1]
    kc = _nsa_compress_jnp(rows[:, :, 0], cmp_pe[0], cmp_w1[0], cmp_w2[0])
    vc = _nsa_compress_jnp(rows[:, :, 1], cmp_pe[1], cmp_w1[1], cmp_w2[1])
    n_cmp = kc.shape[1]
    c_start = jnp.arange(n_cmp) * CMP_STRIDE
    c_mask = (c_start + CMP_BLOCK - 1)[None, :] <= q_pos[:, None]
    p_c = _masked_softmax(jnp.einsum('btgrd,bngd->btgrn', q, kc), c_mask[None, :, None, None, :])
    o_cmp = jnp.einsum('btgrn,bngd->btgrd', p_c, vc)
    n_slc = -(-length // SLC_BLOCK)
    s_start = jnp.arange(n_slc) * SLC_BLOCK
    overlap = ((c_start[:, None] < s_start[None, :] + SLC_BLOCK)
               & (c_start[:, None] + CMP_BLOCK > s_start[None, :])).astype(F32)
    imp = jnp.einsum('btgrn,nm->btgm', p_c, overlap)
    blk = jnp.arange(n_slc)[None, :]
    cur = (q_pos // SLC_BLOCK)[:, None]
    valid = (s_start[None, :] <= q_pos[:, None])[None, :, None, :]
    forced = ((blk == 0) | (blk == cur) | (blk == cur - 1))[None, :, None, :]
    score = jnp.where(valid, jnp.where(forced, 1e6, imp), -1e6)
    n_sel = min(SLC_TOPK, n_slc)
    _, idx = lax.top_k(score, n_sel)
    picked = jnp.any(idx[..., None] == jnp.arange(n_slc), axis=-2)
    kmask = jnp.repeat(picked, SLC_BLOCK, axis=-1)[..., :length]
    kmask = kmask & (jnp.arange(length)[None, :] <= q_pos[:, None])[None, :, None, :]
    s = jnp.einsum('btgrd,blgd->btgrl', q, rows[:, :, 2])
    p = _masked_softmax(s, kmask[:, :, :, None, :])
    o_slc = jnp.einsum('btgrl,blgd->btgrd', p, rows[:, :, 3])
    return o_cmp, o_slc


def _window_attention_jnp(q, q_pos, k, v, k_pos):
    s = jnp.einsum('ntgrd,nsgd->ntgrs', q, k)
    dpos = q_pos[:, :, None] - k_pos[:, None, :]
    mask = (dpos >= 0) & (dpos < WINDOW) & (k_pos[:, None, :] >= 0)
    p = _masked_softmax(s, mask[:, :, None, None, :])
    return jnp.einsum('ntgrs,nsgd->ntgrd', p, v)


def _nsa_sample_jnp(q, new_rows, win_rows, gates, pos, past_rows, win_buf, cmp_pe, cmp_w1, cmp_w2):
    bn, t_len, g, r, d = q.shape
    rows = jnp.concatenate([past_rows, new_rows], axis=1)
    o_cmp, o_slc = _nsa_sparse_jnp(q, pos, rows, cmp_pe, cmp_w1, cmp_w2)
    w_len = win_buf.shape[1]
    ext = jnp.concatenate([win_buf, win_rows], axis=1)
    k_pos = (pos[0] - w_len + jnp.arange(w_len + t_len))[None]
    o_win = _window_attention_jnp(q, pos[None], ext[:, :, 0], ext[:, :, 1], k_pos)
    gate = jax.nn.sigmoid(gates).reshape(bn, t_len, 3, g, r, 1)
    o_nsa = gate[:, :, 0] * o_cmp + gate[:, :, 1] * o_slc + gate[:, :, 2] * o_win
    return o_nsa.reshape(bn * t_len, E_Q), ext[:, t_len:]


def _retention_jnp(q, k, v, state, chunk):
    bn, t_len, h, _ = q.shape
    dv = v.shape[-1]
    log_g = jnp.log1p(-(2.0 ** (-5.0 - jnp.arange(h, dtype=F32))))
    i = jnp.arange(chunk, dtype=F32)
    diff = i[:, None] - i[None, :]
    intra = jnp.where(diff >= 0, jnp.exp(jnp.maximum(diff, 0.0)[None] * log_g[:, None, None]), 0.0)
    q_dec = jnp.exp((i[:, None] + 1.0) * log_g[None, :])
    k_dec = jnp.exp((chunk - 1.0 - i)[:, None] * log_g[None, :])
    c_dec = jnp.exp(chunk * log_g)

    def step(s_mat, inp):
        qc, kc, vc = inp
        att = jnp.einsum('bihd,bjhd->bhij', qc, kc) * intra
        o = (jnp.einsum('bhij,bjhv->bihv', att, vc)
             + jnp.einsum('bihd,bhdv->bihv', qc, s_mat) * q_dec[None, :, :, None])
        s_mat = s_mat * c_dec[None, :, None, None] + jnp.einsum('bjhd,bjhv->bhdv', kc * k_dec[None, :, :, None], vc)
        return s_mat, o

    s_fin, o = lax.scan(step, state, (_to_chunks(q, chunk), _to_chunks(k, chunk), _to_chunks(v, chunk)))
    return o.swapaxes(0, 1).reshape(bn, t_len, h, dv), s_fin


def _ssd_jnp(x, dt, a_neg, b_in, c_in, state, chunk):
    bn, t_len, h, p = x.shape
    rep = h // b_in.shape[2]
    a = dt * a_neg
    b_h = jnp.repeat(b_in, rep, axis=2)
    c_h = jnp.repeat(c_in, rep, axis=2)
    xdt = x * dt[..., None]
    tri = jnp.arange(chunk)[:, None] >= jnp.arange(chunk)[None, :]

    def step(s_mat, inp):
        a_c, x_c, b_c, c_c = inp
        cum = jnp.cumsum(a_c, axis=1)
        seg = cum[:, :, None, :] - cum[:, None, :, :]
        l_mat = jnp.exp(jnp.where(tri[None, :, :, None], seg, -1e30))
        cb = jnp.einsum('bihn,bjhn->bijh', c_c, b_c) * l_mat
        y = (jnp.einsum('bijh,bjhp->bihp', cb, x_c)
             + jnp.einsum('bihn,bhpn->bihp', c_c, s_mat) * jnp.exp(cum)[..., None])
        dec_end = jnp.exp(cum[:, -1:, :] - cum)
        s_mat = (s_mat * jnp.exp(cum[:, -1, :])[:, :, None, None]
                 + jnp.einsum('bjhn,bjhp->bhpn', b_c * dec_end[..., None], x_c))
        return s_mat, y

    s_fin, y = lax.scan(step, state, (_to_chunks(a, chunk), _to_chunks(xdt, chunk), _to_chunks(b_h, chunk),
                                      _to_chunks(c_h, chunk)))
    return y.swapaxes(0, 1).reshape(bn, t_len, h, p), s_fin


def _odd_mixer_jnp(proj, pos, ret_state, ssm_state, conv_buf, gn_gain, conv_w, conv_b, dt_bias, a_log, d_skip,
                   m2_norm):
    bn, t_len, _ = proj.shape
    chunk = CHUNK if t_len % CHUNK == 0 else t_len
    q, k, v, g, z, xbc, dt = jnp.split(proj, O_SPLITS, axis=-1)
    q = _rotary(q.reshape(bn, t_len, RET_HEADS, RET_DQK), pos, RET_DQK, RET_THETA) * (RET_DQK ** -0.5)
    k = _rotary(k.reshape(bn, t_len, RET_HEADS, RET_DQK), pos, RET_DQK, RET_THETA)
    v = v.reshape(bn, t_len, RET_HEADS, RET_DV)
    o_ret, ret_new = _retention_jnp(q, k, v, ret_state, chunk)
    mu = jnp.mean(o_ret, axis=-1, keepdims=True)
    var = jnp.mean(jnp.square(o_ret - mu), axis=-1, keepdims=True)
    o_ret = ((o_ret - mu) * lax.rsqrt(var + EPS)).reshape(bn, t_len, R_V) * gn_gain
    o_ret = jax.nn.silu(g) * o_ret
    xbc_c, _ = _causal_conv(xbc, conv_buf, conv_w, conv_b)
    xbc_c = jax.nn.silu(xbc_c)
    xs, b_in, c_in = jnp.split(xbc_c, [M2_DINNER, M2_DINNER + M2_GROUPS * M2_STATE], axis=-1)
    xs = xs.reshape(bn, t_len, M2_HEADS, M2_HEADDIM)
    b_in = b_in.reshape(bn, t_len, M2_GROUPS, M2_STATE)
    c_in = c_in.reshape(bn, t_len, M2_GROUPS, M2_STATE)
    dt = jax.nn.softplus(dt + dt_bias)
    a_neg = -jnp.exp(a_log)
    y, ssm_new = _ssd_jnp(xs, dt, a_neg, b_in, c_in, ssm_state, chunk)
    y = y + d_skip[:, None] * xs
    y = y.reshape(bn, t_len, M2_DINNER) * jax.nn.silu(z)
    yg = y.reshape(bn, t_len, M2_GROUPS, M2_DINNER // M2_GROUPS)
    yg = yg * lax.rsqrt(jnp.mean(yg * yg, axis=-1, keepdims=True) + EPS)
    y = yg.reshape(bn, t_len, M2_DINNER) * m2_norm
    return o_ret.reshape(bn * t_len, R_V), y.reshape(bn * t_len, M2_DINNER), ret_new, ssm_new


def _short_conv_jnp(sc, n_batch, t_len, conv_buf, sc_w):
    b_gate, c_gate, h_in = jnp.split(sc.reshape(n_batch, t_len, E_SC), 3, axis=-1)
    conv_out, new_conv = _causal_conv(c_gate * h_in, conv_buf, sc_w)
    return (b_gate * conv_out).reshape(n_batch * t_len, SC_WIDTH), new_conv


def kernel(x_prompt, x_sample, cache_nsa_kv, cache_nsa_win, state_sc_conv, state_ret, state_ssm, state_m2_conv,
           page_table, norm_w, final_norm, e_w_in, e_w_out, e_cmp_pe, e_cmp_w1, e_cmp_w2, e_sc_conv, o_w_in,
           o_w_out, o_ret_gn, o_m2_conv_w, o_m2_conv_b, o_m2_dt_bias, o_m2_a_log, o_m2_d, o_m2_norm, peer_wq,
           peer_keys, peer_u, peer_v):
    bp, tp, dm = x_prompt.shape
    bs, ts, _ = x_sample.shape
    n_p, n_s = bp * tp, bs * ts
    g, r, d = NSA_KV_HEADS, NSA_GROUP, NSA_HEAD_DIM
    past_len = page_table.shape[1] * PAGE_SIZE
    pos_p = jnp.arange(tp, dtype=jnp.int32)
    pos_s = past_len + jnp.arange(ts, dtype=jnp.int32)
    n_real = n_p + n_s
    n_pad = _round_up(n_real, PEER_TM) - n_real
    s_rows = slice(n_p, n_real)
    pos_rows = jnp.concatenate([jnp.tile(pos_p, bp), jnp.tile(pos_s, bs), jnp.zeros((n_pad,), jnp.int32)])
    xp = x_prompt.reshape(n_p, dm)
    x = jnp.concatenate([xp, x_sample.reshape(n_s, dm), xp[:n_pad]], axis=0)
    pad_bf = jnp.zeros((n_pad, E_Q), BF16)

    w0 = e_w_in[0]
    w_in = jnp.concatenate([w0[:, :SC_COL0], w0[:, SC_COL0 + E_G:], w0[:, SC_COL0:SC_COL0 + E_G],
                            jnp.zeros((dm, _pad_cols(E_IN) - E_IN), F32)], axis=1).astype(BF16)
    proj = norm_matmul(x, norm_w[0, 0], w_in)
    q_bf, rows, win, rows_bf, win_bf = nsa_rope(proj, pos_rows)
    kvc = nsa_compress(rows, bp, tp, e_cmp_pe[0], e_cmp_w1[0], e_cmp_w2[0])
    o_nsa_p = nsa_prompt(q_bf, kvc, rows_bf, win_bf, proj, bp, tp)
    p_kv = rows[:n_p].reshape(bp, tp, 4, g, d)
    s_kv = rows[s_rows].reshape(bs, ts, 4, g, d)
    p_win = win[:n_p].reshape(bp, tp, 2, g, d)[:, tp - min(WINDOW, tp):]
    win_s = win[s_rows].reshape(bs, ts, 2, g, d)
    s_win = jnp.concatenate([cache_nsa_win[0], win_s], axis=1)[:, ts:]
    cache = cache_nsa_kv[0]
    q_s = q_bf[s_rows].reshape(bs, ts, g, r, d).transpose(0, 2, 3, 1, 4).reshape(bs, g, r * ts, d)
    a0, a1 = nsa_sample_compress(cache, page_table, e_cmp_pe[0], e_cmp_w1[0])
    o_cmp_s, sel_s = nsa_sample_select(a0, a1, e_cmp_w2[0], q_s, past_len, ts)
    o_nsa_s = nsa_sample_attend(cache, page_table, q_s, sel_s, o_cmp_s, rows, win,
                                cache_nsa_win[0], proj, past_len, ts, n_p)
    o_nsa_s = o_nsa_s.reshape(bs, g, r, ts, d).transpose(0, 3, 1, 2, 4).reshape(n_s, E_Q)
    o_nsa = jnp.concatenate([o_nsa_p, o_nsa_s.astype(BF16), pad_bf], axis=0)
    o_sc_p, p_sc = short_conv_prompt(proj, bp, tp, e_sc_conv[0])
    o_sc_s, s_sc = _short_conv_jnp(proj[s_rows, SC_COL0:GATE_COL0], bs, ts, state_sc_conv[0], e_sc_conv[0])
    o_sc = jnp.concatenate([o_sc_p, o_sc_s.astype(BF16), pad_bf], axis=0)
    x = matmul2_res(o_nsa, o_sc, e_w_out[0].astype(BF16), x)
    x = peer_layer(x, norm_w[0, 1], peer_wq[0], peer_keys[0], peer_u[0], peer_v[0])

    w_in = jnp.pad(o_w_in[0], ((0, 0), (0, _pad_cols(O_IN) - O_IN))).astype(BF16)
    proj = norm_matmul(x, norm_w[1, 0], w_in)
    odd_w = (o_m2_conv_w[0], o_m2_conv_b[0], o_m2_dt_bias[0], o_m2_a_log[0], o_m2_d[0], o_m2_norm[0])
    o_ret_p, p_ret = retention(proj, pos_p, bp, tp, CHUNK, jnp.zeros((bp, RET_HEADS, RET_DQK, RET_DV), F32),
                               o_ret_gn[0])
    o_ssd_p, p_ssm = ssd(proj, bp, tp, CHUNK, jnp.zeros((bp, M2_CONV - 1, M2_CONV_DIM), F32),
                         jnp.zeros((bp, M2_HEADS, M2_HEADDIM, M2_STATE), F32), *odd_w)
    xbc_cols = slice(O_SPLITS[4], O_SPLITS[5])
    p_m2c = proj[:n_p, xbc_cols].reshape(bp, tp, M2_CONV_DIM)[:, tp - (M2_CONV - 1):]
    xbc_s = proj[s_rows, xbc_cols].reshape(bs, ts, M2_CONV_DIM)
    s_m2c = jnp.concatenate([state_m2_conv[0], xbc_s], axis=1)[:, ts:]
    o_ret_s, o_ssd_s, s_ret, s_ssm = _odd_mixer_jnp(proj[s_rows, :O_IN].reshape(bs, ts, O_IN), pos_s, state_ret[0],
                                                    state_ssm[0], state_m2_conv[0], o_ret_gn[0], *odd_w)
    o_ret = jnp.concatenate([o_ret_p, o_ret_s.astype(BF16), pad_bf], axis=0)
    o_ssd = jnp.concatenate([o_ssd_p, o_ssd_s.astype(BF16), pad_bf], axis=0)
    x = matmul2_res(o_ret, o_ssd, o_w_out[0].astype(BF16), x)
    x = peer_layer(x, norm_w[1, 1], peer_wq[1], peer_keys[1], peer_u[1], peer_v[1])

    y = rmsnorm(x, final_norm)
    y_prompt = y[:n_p].reshape(bp, tp, dm)
    y_sample = y[s_rows].reshape(bs, ts, dm)
    return (y_prompt, y_sample, p_kv[None], p_win[None], p_sc[None], p_ret[None], p_ssm[None], p_m2c[None],
            s_kv[None], s_win[None], s_sc[None], s_ret[None], s_ssm[None], s_m2c[None])
```

```python
import functools
import math

import jax
import jax.numpy as jnp
from jax import lax
from jax.experimental import pallas as pl
from jax.experimental.pallas import tpu as pltpu

F32 = jnp.float32
BF16 = jnp.bfloat16

D_MODEL = 2048
DEPTH = 2
PAGE_SIZE = 128
NSA_HEAD_DIM = 128
NSA_HEADS = 8
NSA_KV_HEADS = 2
NSA_GROUP = 4
CMP_BLOCK = 32
CMP_STRIDE = 16
SLC_BLOCK = 64
SLC_TOPK = 16
WINDOW = 512
ROPE_THETA = 500000.0
ROPE_DIMS = 32
SC_WIDTH = 1024
SC_KSIZE = 3
RET_HEADS = 4
RET_DQK = 256
RET_DV = 256
RET_THETA = 10000.0
M2_DINNER = 1024
M2_HEADDIM = 64
M2_HEADS = 16
M2_STATE = 128
M2_GROUPS = 2
M2_CONV = 4
M2_CONV_DIM = M2_DINNER + 2 * M2_GROUPS * M2_STATE
PEER_HEADS = 8
PEER_KEYS = 128
PEER_QDIM = 256
PEER_TOPK = 16
Q_BLOCK = 128
CHUNK = 128
EPS = 1e-6

E_Q = NSA_HEADS * NSA_HEAD_DIM
E_KV = 6 * NSA_KV_HEADS * NSA_HEAD_DIM
E_G = 3 * NSA_HEADS
E_SC = 3 * SC_WIDTH
E_IN = E_Q + E_KV + E_G + E_SC
R_QK = RET_HEADS * RET_DQK
R_V = RET_HEADS * RET_DV
O_SPLITS = [R_QK, 2 * R_QK, 2 * R_QK + R_V, 2 * R_QK + 2 * R_V,
            2 * R_QK + 2 * R_V + M2_DINNER, 2 * R_QK + 2 * R_V + M2_DINNER + M2_CONV_DIM]
O_IN = O_SPLITS[-1] + M2_HEADS
SC_COL0 = E_Q + E_KV
GATE_COL0 = SC_COL0 + E_SC

LANES = 128
VMEM_LIMIT = 56 * 1024 * 1024
ROW_TILE_CAP = 1056
ROPE_ROWS_CAP = 544
NEG_INF = float("-inf")


def _round_up(n, m):
    return -(-n // m) * m


def _pick_tile(n, cap):
    best = LANES
    for t in range(LANES, cap + 1, LANES):
        if n % t == 0:
            best = t
    return best


def _pad_cols(m):
    return min((_round_up(m, t) for t in (768, 640, 512)))


def _row_tile(n, cap):
    best = None
    for t in range(16, cap + 1, 16):
        if n % t == 0:
            best = t
    assert best is not None
    return best


def _gelu_tanh(x):
    return 0.5 * x * (1.0 + jnp.tanh(math.sqrt(2.0 / math.pi) * (x + 0.044715 * (x * x * x))))


def _norm_matmul_kernel(x_ref, g_ref, w_ref, o_ref, xn_ref):
    @pl.when(pl.program_id(1) == 0)
    def _():
        x = x_ref[...]
        ms = jnp.mean(x * x, axis=-1, keepdims=True)
        xn_ref[...] = (x * lax.rsqrt(ms + EPS) * g_ref[...]).astype(BF16)

    o_ref[...] = jnp.dot(xn_ref[...], w_ref[...], preferred_element_type=F32).reshape(o_ref.shape)


def norm_matmul(x, gain, w_bf):
    n, k = x.shape
    m = w_bf.shape[1]
    tm = _row_tile(n, ROW_TILE_CAP)
    tn = _pick_tile(m, 768)
    return pl.pallas_call(
        _norm_matmul_kernel,
        grid=(n // tm, m // tn),
        in_specs=[pl.BlockSpec((tm, k), lambda i, j: (i, 0)),
                  pl.BlockSpec((1, k), lambda i, j: (0, 0)),
                  pl.BlockSpec((k, tn), lambda i, j: (0, j))],
        out_specs=pl.BlockSpec((tm, tn), lambda i, j: (i, j)),
        out_shape=jax.ShapeDtypeStruct((n, m), F32),
        scratch_shapes=[pltpu.VMEM((tm, k), BF16)],
        compiler_params=pltpu.CompilerParams(
            dimension_semantics=("parallel", "arbitrary"), vmem_limit_bytes=VMEM_LIMIT),
        name="norm_matmul",
    )(x, gain.reshape(1, k), w_bf)


def _matmul2_res_kernel(a1_ref, a2_ref, w1_ref, w2_ref, r_ref, o_ref):
    o_ref[...] = (r_ref[...] + jnp.dot(a1_ref[...], w1_ref[...], preferred_element_type=F32)
                  + jnp.dot(a2_ref[...], w2_ref[...], preferred_element_type=F32))


def matmul2_res(a1, a2, w_bf, res):
    n, k1 = a1.shape
    k2 = a2.shape[1]
    m = w_bf.shape[1]
    assert k1 == k2
    tm = _row_tile(n, ROW_TILE_CAP)
    tn = _pick_tile(m, 1024)
    return pl.pallas_call(
        _matmul2_res_kernel,
        grid=(n // tm, m // tn),
        in_specs=[pl.BlockSpec((tm, k1), lambda i, j: (i, 0)),
                  pl.BlockSpec((tm, k2), lambda i, j: (i, 0)),
                  pl.BlockSpec((k1, tn), lambda i, j: (0, j)),
                  pl.BlockSpec((k2, tn), lambda i, j: (1, j)),
                  pl.BlockSpec((tm, tn), lambda i, j: (i, j))],
        out_specs=pl.BlockSpec((tm, tn), lambda i, j: (i, j)),
        out_shape=jax.ShapeDtypeStruct((n, m), F32),
        compiler_params=pltpu.CompilerParams(
            dimension_semantics=("parallel", "parallel"), vmem_limit_bytes=VMEM_LIMIT),
        name="matmul2_res",
    )(a1, a2, w_bf, w_bf, res)


PEER_TOK = 128
PEER_HEAD_UNROLL = 4
PEER_TM = 768
PEER_TA = 8
PEER_TE = PEER_TA * PEER_KEYS


def _top_desc(work, count, with_rank=False, one_at_a_time=False):
    rows = []
    rank = jnp.full(work.shape, float(count), F32)
    row_id = lax.broadcasted_iota(jnp.int32, work.shape, 0).astype(F32)
    for r in range(count):
        m = jnp.max(work, axis=0, keepdims=True)
        rows.append(m)
        hit = work == m
        if one_at_a_time:
            hit = row_id == jnp.min(jnp.where(hit, row_id, float(work.shape[0])), axis=0, keepdims=True)
        if with_rank:
            rank = jnp.where(hit, float(r), rank)
        work = jnp.where(hit, NEG_INF, work)
    return (rows, rank) if with_rank else rows


def _peer_router_kernel(q_ref, k_ref, cnt_ref, gw_ref, r2_ref, w2_ref):
    def head(h, carry):
        q = q_ref[h]
        scores = []
        for side in range(2):
            qs = q[:, side * PEER_KEYS:(side + 1) * PEER_KEYS]
            qs = qs * lax.rsqrt(jnp.mean(qs * qs, axis=-1, keepdims=True) + EPS)
            scores.append(lax.dot_general(k_ref[h, side], qs, (((1,), (1,)), ((), ())),
                                          preferred_element_type=F32))
        s1, s2 = scores
        v1 = _top_desc(s1, PEER_TOPK + 1)
        v2, rank2 = _top_desc(s2, PEER_TOPK + 1, with_rank=True)
        v2_lo = jnp.concatenate(v2[:8], axis=0)
        v2_hi = jnp.concatenate(v2[8:16], axis=0)
        row = lax.broadcasted_iota(jnp.int32, v2_lo.shape, 0)
        blocks = [v1[0] + v2_lo, v1[0] + v2_hi, v1[1] + v2_lo]
        for a, lim in ((2, 5), (3, 4), (4, 3), (5, 2), (6, 2), (7, 2)):
            blocks.append(jnp.where(row < lim, v1[a] + v2_lo, NEG_INF))
        blocks.append(jnp.concatenate(v1[8:16], axis=0) + v2[0])
        extra = jnp.where(row == 0, v1[0] + v2[16], jnp.where(row == 1, v1[16] + v2[0], NEG_INF))
        blocks.append(extra)
        cand = jnp.concatenate(blocks, axis=0)
        tops = _top_desc(cand, PEER_TOPK + 1, one_at_a_time=True)
        z = jnp.zeros_like(tops[0])
        for r in range(PEER_TOPK):
            z = z + jnp.exp(tops[r] - tops[0])
        tau = 0.5 * (tops[PEER_TOPK - 1] + tops[PEER_TOPK])
        count = jnp.zeros_like(s1)
        for r in range(PEER_TOPK):
            count = count + jnp.where(s1 >= tau - v2[r], 1.0, 0.0)
        cnt_ref[h] = count
        gw_ref[h] = jnp.exp(s1 - v1[0]) / z
        r2_ref[h] = rank2.astype(BF16)
        w2_ref[h] = jnp.exp(s2 - v2[0]).astype(BF16)
        return carry

    lax.fori_loop(0, PEER_HEADS, head, 0, unroll=PEER_HEAD_UNROLL)


def peer_router(q_hm, keys):
    n = q_hm.shape[1]
    out = jax.ShapeDtypeStruct((PEER_HEADS, PEER_KEYS, n), F32)
    out_bf = jax.ShapeDtypeStruct((PEER_HEADS, PEER_KEYS, n), BF16)
    spec = pl.BlockSpec((PEER_HEADS, PEER_KEYS, PEER_TOK), lambda i: (0, 0, i))
    return pl.pallas_call(
        _peer_router_kernel,
        grid=(n // PEER_TOK,),
        in_specs=[pl.BlockSpec((PEER_HEADS, PEER_TOK, PEER_QDIM), lambda i: (0, i, 0)),
                  pl.BlockSpec((PEER_HEADS, 2, PEER_KEYS, PEER_QDIM // 2), lambda i: (0, 0, 0, 0))],
        out_specs=[spec, spec, spec, spec],
        out_shape=[out, out, out_bf, out_bf],
        compiler_params=pltpu.CompilerParams(dimension_semantics=("parallel",), vmem_limit_bytes=VMEM_LIMIT),
        name="peer_router",
    )(q_hm, keys)


def _peer_expert_kernel(xt_ref, u_ref, vt_ref, cnt_ref, gw_ref, r2_ref, w2_ref, o_ref, s_scr, hg_scr):
    e = pl.program_id(1)
    s_scr[...] = jnp.dot(u_ref[...], xt_ref[...], preferred_element_type=F32)
    zero = jnp.zeros((), BF16)
    for al in range(PEER_TA):
        rows = slice(al * PEER_KEYS, (al + 1) * PEER_KEYS)
        g = None
        for h in range(PEER_HEADS):
            cnt = cnt_ref[h, al:al + 1, :].astype(BF16)
            gate = gw_ref[h, al:al + 1, :].astype(BF16)
            t = jnp.where(r2_ref[h] < cnt, w2_ref[h], zero) * gate
            g = t if g is None else g + t
        hg_scr[rows, :] = _gelu_tanh(s_scr[rows, :]).astype(BF16) * g
    part = jnp.dot(vt_ref[...], hg_scr[...], preferred_element_type=F32)

    @pl.when(e == 0)
    def _():
        o_ref[...] = part

    @pl.when(e != 0)
    def _():
        o_ref[...] += part


def peer_experts(xt_bf, u_bf, vt_bf, cnt, gw, r2, w2):
    d, n = xt_bf.shape
    n_exp = u_bf.shape[0]
    tm = PEER_TM
    sel_spec = pl.BlockSpec((PEER_HEADS, PEER_TA, tm), lambda i, e: (0, e, i))
    all_spec = pl.BlockSpec((PEER_HEADS, PEER_KEYS, tm), lambda i, e: (0, 0, i))
    return pl.pallas_call(
        _peer_expert_kernel,
        grid=(n // tm, n_exp // PEER_TE),
        in_specs=[pl.BlockSpec((d, tm), lambda i, e: (0, i)),
                  pl.BlockSpec((PEER_TE, d), lambda i, e: (e, 0)),
                  pl.BlockSpec((d, PEER_TE), lambda i, e: (0, e)),
                  sel_spec, sel_spec, all_spec, all_spec],
        out_specs=pl.BlockSpec((d, tm), lambda i, e: (0, i)),
        out_shape=jax.ShapeDtypeStruct((d, n), F32),
        scratch_shapes=[pltpu.VMEM((PEER_TE, tm), F32), pltpu.VMEM((PEER_TE, tm), BF16)],
        compiler_params=pltpu.CompilerParams(
            dimension_semantics=("parallel", "arbitrary"), vmem_limit_bytes=VMEM_LIMIT),
        name="peer_experts",
    )(xt_bf, u_bf, vt_bf, cnt, gw, r2, w2)


def _norm_matmul_t_kernel(x_ref, g_ref, w_ref, o_ref, xt_ref, xn_ref):
    @pl.when(pl.program_id(1) == 0)
    def _():
        x = x_ref[...]
        ms = jnp.mean(x * x, axis=-1, keepdims=True)
        xn = x * lax.rsqrt(ms + EPS) * g_ref[...]
        xn_ref[...] = xn.astype(BF16)
        xt_ref[...] = xn.T.astype(BF16)

    o_ref[...] = jnp.dot(xn_ref[...], w_ref[...], preferred_element_type=F32).reshape(o_ref.shape)


def norm_matmul_t(x, gain, w_bf, cols):
    n, k = x.shape
    m = w_bf.shape[1]
    tm = PEER_TM
    return pl.pallas_call(
        _norm_matmul_t_kernel,
        grid=(n // tm, m // cols),
        in_specs=[pl.BlockSpec((tm, k), lambda i, j: (i, 0)),
                  pl.BlockSpec((1, k), lambda i, j: (0, 0)),
                  pl.BlockSpec((k, cols), lambda i, j: (0, j))],
        out_specs=[pl.BlockSpec((1, tm, cols), lambda i, j: (j, i, 0)),
                   pl.BlockSpec((k, tm), lambda i, j: (0, i))],
        out_shape=[jax.ShapeDtypeStruct((m // cols, n, cols), F32), jax.ShapeDtypeStruct((k, n), BF16)],
        scratch_shapes=[pltpu.VMEM((tm, k), BF16)],
        compiler_params=pltpu.CompilerParams(
            dimension_semantics=("parallel", "arbitrary"), vmem_limit_bytes=VMEM_LIMIT),
        name="norm_matmul_t",
    )(x, gain.reshape(1, k), w_bf)


def _rmsnorm_kernel(x_ref, g_ref, o_ref):
    x = x_ref[...]
    ms = jnp.mean(x * x, axis=-1, keepdims=True)
    o_ref[...] = (x * lax.rsqrt(ms + EPS) * g_ref[...]).astype(o_ref.dtype)


def rmsnorm(x, gain, dtype=F32):
    n, k = x.shape
    tm = _row_tile(n, ROW_TILE_CAP)
    return pl.pallas_call(
        _rmsnorm_kernel,
        grid=(n // tm,),
        in_specs=[pl.BlockSpec((tm, k), lambda i: (i, 0)), pl.BlockSpec((1, k), lambda i: (0, 0))],
        out_specs=pl.BlockSpec((tm, k), lambda i: (i, 0)),
        out_shape=jax.ShapeDtypeStruct((n, k), dtype),
        compiler_params=pltpu.CompilerParams(dimension_semantics=("parallel",), vmem_limit_bytes=VMEM_LIMIT),
        name="rmsnorm",
    )(x, gain.reshape(1, k))


def peer_layer(x, gain, w_q, keys, u, v):
    q_hm, xt_bf = norm_matmul_t(x, gain, w_q.astype(BF16), PEER_QDIM)
    cnt, gw, r2, w2 = peer_router(q_hm, keys)
    return x + peer_experts(xt_bf, u.astype(BF16), v.T.astype(BF16), cnt, gw, r2, w2).T


def _rope_tables(pos):
    half = ROPE_DIMS // 2
    inv = ROPE_THETA ** (-jnp.arange(half, dtype=F32) / half)
    ang = pos.astype(F32)[:, None] * inv[None, :]
    cos, sin = jnp.cos(ang), jnp.sin(ang)
    t = pos.shape[0]
    ones = jnp.ones((t, NSA_HEAD_DIM - ROPE_DIMS), F32)
    zeros = jnp.zeros((t, NSA_HEAD_DIM - ROPE_DIMS), F32)
    zh = jnp.zeros((t, half), F32)
    c = jnp.concatenate([cos, cos, ones], axis=1)
    s_lo = jnp.concatenate([-sin, zh, zeros], axis=1)
    s_hi = jnp.concatenate([zh, sin, zeros], axis=1)
    return c, s_lo, s_hi


def _rope_kernel(p_ref, c_ref, sl_ref, sh_ref, q_ref, rows_ref, win_ref, rows_bf_ref, win_bf_ref):
    c, sl, sh = c_ref[...], sl_ref[...], sh_ref[...]
    half = ROPE_DIMS // 2

    def rot(x):
        return x * c + pltpu.roll(x, LANES - half, 1) * sl + pltpu.roll(x, half, 1) * sh

    d = NSA_HEAD_DIM
    scale = d ** -0.5
    for hd in range(NSA_HEADS):
        q_ref[:, hd * d:(hd + 1) * d] = (rot(p_ref[:, hd * d:(hd + 1) * d]) * scale).astype(BF16)
    for blk in range(12):
        x = p_ref[:, E_Q + blk * d:E_Q + (blk + 1) * d]
        if (blk // 2) % 2 == 0:
            x = rot(x)
        if blk < 8:
            rows_ref[:, blk * d:(blk + 1) * d] = x
            rows_bf_ref[:, blk * d:(blk + 1) * d] = x.astype(BF16)
        else:
            win_ref[:, (blk - 8) * d:(blk - 7) * d] = x
            win_bf_ref[:, (blk - 8) * d:(blk - 7) * d] = x.astype(BF16)


def nsa_rope(proj, pos_rows):
    n = proj.shape[0]
    tm = _row_tile(n, ROPE_ROWS_CAP)
    c, sl, sh = _rope_tables(pos_rows)
    width = E_Q + E_KV
    tab = pl.BlockSpec((tm, LANES), lambda i: (i, 0))
    return pl.pallas_call(
        _rope_kernel,
        grid=(n // tm,),
        in_specs=[pl.BlockSpec((tm, width), lambda i: (i, 0)), tab, tab, tab],
        out_specs=[pl.BlockSpec((tm, E_Q), lambda i: (i, 0)),
                   pl.BlockSpec((tm, 1024), lambda i: (i, 0)),
                   pl.BlockSpec((tm, 512), lambda i: (i, 0)),
                   pl.BlockSpec((tm, 1024), lambda i: (i, 0)),
                   pl.BlockSpec((tm, 512), lambda i: (i, 0))],
        out_shape=[jax.ShapeDtypeStruct((n, E_Q), BF16),
                   jax.ShapeDtypeStruct((n, 1024), F32),
                   jax.ShapeDtypeStruct((n, 512), F32),
                   jax.ShapeDtypeStruct((n, 1024), BF16),
                   jax.ShapeDtypeStruct((n, 512), BF16)],
        compiler_params=pltpu.CompilerParams(dimension_semantics=("parallel",), vmem_limit_bytes=VMEM_LIMIT),
        name="nsa_rope",
    )(proj, c, sl, sh)


def _compress_kernel(x_ref, pe_ref, w1_ref, w2_ref, o_ref):
    n_sub = x_ref.shape[0] // CMP_STRIDE
    acc0 = jnp.zeros((n_sub, NSA_HEAD_DIM), F32)
    acc1 = jnp.zeros((n_sub, NSA_HEAD_DIM), F32)
    for s in range(CMP_STRIDE):
        xs = x_ref[pl.ds(s, n_sub, stride=CMP_STRIDE), :]
        a0 = (xs + pe_ref[0, s:s + 1, :]).astype(BF16)
        a1 = (xs + pe_ref[0, CMP_STRIDE + s:CMP_STRIDE + s + 1, :]).astype(BF16)
        acc0 = acc0 + jnp.dot(a0, w1_ref[0, s], preferred_element_type=F32)
        acc1 = acc1 + jnp.dot(a1, w1_ref[0, CMP_STRIDE + s], preferred_element_type=F32)
    pre = acc0 + pltpu.roll(acc1, n_sub - 1, 0)
    o_ref[0, 0] = jnp.dot(_gelu_tanh(pre).astype(BF16), w2_ref[0], preferred_element_type=F32).astype(BF16)


def nsa_compress(rows, n_batch, t_len, cmp_pe, cmp_w1, cmp_w2):
    n_sub = t_len // CMP_STRIDE
    d = NSA_HEAD_DIM
    w1 = cmp_w1.reshape(2, CMP_BLOCK, d, d).astype(BF16)
    return pl.pallas_call(
        _compress_kernel,
        grid=(n_batch, 4),
        in_specs=[pl.BlockSpec((t_len, d), lambda b, c: (b, c)),
                  pl.BlockSpec((1, CMP_BLOCK, d), lambda b, c: (c // 2, 0, 0)),
                  pl.BlockSpec((1, CMP_BLOCK, d, d), lambda b, c: (c // 2, 0, 0, 0)),
                  pl.BlockSpec((1, d, d), lambda b, c: (c // 2, 0, 0))],
        out_specs=pl.BlockSpec((1, 1, n_sub, d), lambda b, c: (b, c, 0, 0)),
        out_shape=jax.ShapeDtypeStruct((n_batch, 4, n_sub, d), BF16),
        compiler_params=pltpu.CompilerParams(
            dimension_semantics=("parallel", "parallel"), vmem_limit_bytes=VMEM_LIMIT),
        name="nsa_compress",
    )(rows, cmp_pe, w1, cmp_w2.astype(BF16))


NSA_TK = 512
NSA_WTILES = WINDOW // Q_BLOCK + 1


def _masked_softmax_rows(s, mask):
    s = jnp.where(mask, s, -1e30)
    p = jnp.exp(s - jnp.max(s, axis=-1, keepdims=True))
    p = jnp.where(mask, p, 0.0)
    return p / jnp.maximum(jnp.sum(p, axis=-1, keepdims=True), 1e-30)


def _nsa_prompt_kernel(q_ref, kc_ref, vc_ref, ks_ref, vs_ref, kw_ref, vw_ref, gate_ref, o_ref):
    g = pl.program_id(1)
    qi = pl.program_id(2)
    d, r, qb = NSA_HEAD_DIM, NSA_GROUP, Q_BLOCK
    n_cmp = kc_ref.shape[2]
    n_slc = ks_ref.shape[0] // SLC_BLOCK
    nt = (((1,), (1,)), ((), ()))
    q = jnp.concatenate([q_ref[:, h * d:(h + 1) * d] for h in range(r)], axis=0)
    t_col = qi * qb + lax.broadcasted_iota(jnp.int32, (qb, 1), 0)

    s = lax.dot_general(q, kc_ref[0, 0], nt, preferred_element_type=F32).reshape(r, qb, n_cmp)
    n_idx = lax.broadcasted_iota(jnp.int32, (qb, n_cmp), 1)
    c_mask = (n_idx * CMP_STRIDE + (CMP_BLOCK - 1) <= t_col)[None]
    p_c = _masked_softmax_rows(s, c_mask)
    o_cmp = jnp.dot(p_c.reshape(r * qb, n_cmp).astype(BF16), vc_ref[0, 0], preferred_element_type=F32)

    nb = LANES
    assert n_slc <= nb and qb == LANES
    p_sum = jnp.sum(p_c, axis=0)
    si = lax.broadcasted_iota(jnp.int32, (nb, n_cmp), 0) * SLC_BLOCK
    ci = lax.broadcasted_iota(jnp.int32, (nb, n_cmp), 1) * CMP_STRIDE
    overlap_t = jnp.where((ci < si + SLC_BLOCK) & (ci + CMP_BLOCK > si), 1.0, 0.0).astype(BF16)
    p_hi = p_sum.astype(BF16)
    p_lo = (p_sum - p_hi.astype(F32)).astype(BF16)
    imp_t = (lax.dot_general(overlap_t, p_hi, nt, preferred_element_type=F32)
             + lax.dot_general(overlap_t, p_lo, nt, preferred_element_type=F32))
    blk = lax.broadcasted_iota(jnp.int32, (nb, qb), 0)
    blk_f = blk.astype(F32)
    t_row = qi * qb + lax.broadcasted_iota(jnp.int32, (1, qb), 1)
    cur = t_row // SLC_BLOCK
    forced = (blk == 0) | (blk == cur) | (blk == cur - 1)
    work = jnp.where(blk * SLC_BLOCK <= t_row, jnp.where(forced, 1e6, imp_t), -1e6)
    work = jnp.where(blk < n_slc, work, NEG_INF)
    sel_t = jnp.zeros((nb, qb), F32)
    for _ in range(min(SLC_TOPK, n_slc)):
        m = jnp.max(work, axis=0, keepdims=True)
        first = jnp.min(jnp.where(work == m, blk_f, float(nb)), axis=0, keepdims=True)
        pick = blk_f == first
        sel_t = jnp.where(pick, 1.0, sel_t)
        work = jnp.where(pick, NEG_INF, work)
    sel_bf = sel_t.T.astype(BF16)

    tk = NSA_TK
    bpt = tk // SLC_BLOCK

    def slc_step(kt, carry):
        m_run, l_run, acc = carry
        start = pl.multiple_of(kt * tk, tk)
        k = ks_ref[pl.ds(start, tk), :]
        v = vs_ref[pl.ds(start, tk), :]
        s = lax.dot_general(q, k, nt, preferred_element_type=F32).reshape(r, qb, tk)
        ei = lax.broadcasted_iota(jnp.int32, (nb, tk), 0)
        ej = lax.broadcasted_iota(jnp.int32, (nb, tk), 1)
        expand = jnp.where(ei == kt * bpt + ej // SLC_BLOCK, 1.0, 0.0).astype(BF16)
        picked = jnp.dot(sel_bf, expand, preferred_element_type=F32)
        kpos = start + lax.broadcasted_iota(jnp.int32, (qb, tk), 1)
        mask = ((picked > 0.5) & (kpos <= t_col))[None]
        s = jnp.where(mask, s, -1e30)
        m_new = jnp.maximum(m_run, jnp.max(s, axis=-1, keepdims=True))
        alpha = jnp.exp(m_run - m_new)
        p = jnp.where(mask, jnp.exp(s - m_new), 0.0)
        l_new = alpha * l_run + jnp.sum(p, axis=-1, keepdims=True)
        pv = jnp.dot(p.reshape(r * qb, tk).astype(BF16), v, preferred_element_type=F32)
        acc = alpha.reshape(r * qb, 1) * acc + pv
        return m_new, l_new, acc

    init = (jnp.full((r, qb, 1), -1e30, F32), jnp.zeros((r, qb, 1), F32), jnp.zeros((r * qb, d), F32))
    _, l_fin, acc = lax.fori_loop(0, (qi * qb) // tk + 1, slc_step, init)
    o_slc = acc / jnp.maximum(l_fin.reshape(r * qb, 1), 1e-30)

    k_tiles, v_tiles, pos_tiles = [], [], []
    for j in range(NSA_WTILES):
        kt = qi - (NSA_WTILES - 1) + j
        ktc = jnp.maximum(kt, 0)
        start = pl.multiple_of(ktc * qb, qb)
        k_tiles.append(kw_ref[pl.ds(start, qb), :])
        v_tiles.append(vw_ref[pl.ds(start, qb), :])
        lane = lax.broadcasted_iota(jnp.int32, (qb, qb), 1)
        pos_tiles.append(jnp.where(kt >= 0, start + lane, -1))
    k_w = jnp.concatenate(k_tiles, axis=0)
    v_w = jnp.concatenate(v_tiles, axis=0)
    k_pos = jnp.concatenate(pos_tiles, axis=1)
    span = NSA_WTILES * qb
    s = lax.dot_general(q, k_w, nt, preferred_element_type=F32).reshape(r, qb, span)
    dpos = t_col - k_pos
    w_mask = ((dpos >= 0) & (dpos < WINDOW) & (k_pos >= 0))[None]
    p_w = _masked_softmax_rows(s, w_mask)
    o_win = jnp.dot(p_w.reshape(r * qb, span).astype(BF16), v_w, preferred_element_type=F32)

    sig = jax.nn.sigmoid(gate_ref[...])
    lane = lax.broadcasted_iota(jnp.int32, sig.shape, 1)
    for h in range(r):
        rows = slice(h * qb, (h + 1) * qb)
        out = jnp.zeros((qb, d), F32)
        for branch, o_b in enumerate((o_cmp, o_slc, o_win)):
            col = branch * NSA_HEADS + g * r + h
            gate = jnp.sum(jnp.where(lane == col, sig, 0.0), axis=-1, keepdims=True)
            out = out + gate * o_b[rows]
        o_ref[:, h * d:(h + 1) * d] = out.astype(o_ref.dtype)


def nsa_prompt(q_bf, kvc, rows_bf, win_bf, proj, n_batch, t_len):
    d, r, qb = NSA_HEAD_DIM, NSA_GROUP, Q_BLOCK
    nqb = t_len // qb
    n_sub = kvc.shape[2]
    gate_blk = GATE_COL0 // LANES
    seq = lambda col: pl.BlockSpec((t_len, d), lambda b, g, i: (b, col(g)))
    return pl.pallas_call(
        _nsa_prompt_kernel,
        grid=(n_batch, NSA_KV_HEADS, nqb),
        in_specs=[pl.BlockSpec((qb, r * d), lambda b, g, i: (b * nqb + i, g)),
                  pl.BlockSpec((1, 1, n_sub, d), lambda b, g, i: (b, g, 0, 0)),
                  pl.BlockSpec((1, 1, n_sub, d), lambda b, g, i: (b, 2 + g, 0, 0)),
                  seq(lambda g: 4 + g), seq(lambda g: 6 + g),
                  seq(lambda g: g), seq(lambda g: 2 + g),
                  pl.BlockSpec((qb, LANES), lambda b, g, i: (b * nqb + i, gate_blk))],
        out_specs=pl.BlockSpec((qb, r * d), lambda b, g, i: (b * nqb + i, g)),
        out_shape=jax.ShapeDtypeStruct((n_batch * t_len, E_Q), BF16),
        compiler_params=pltpu.CompilerParams(
            dimension_semantics=("parallel", "parallel", "arbitrary"), vmem_limit_bytes=VMEM_LIMIT),
        name="nsa_prompt",
    )(q_bf, kvc, kvc, rows_bf, rows_bf, win_bf, win_bf, proj)


SAMPLE_PAGES = 16
NEW_ROWS_BLK = 16


def _page_specs(kind):
    def spec(i):
        return pl.BlockSpec((1, PAGE_SIZE, 1, NSA_KV_HEADS, NSA_HEAD_DIM),
                            lambda s, c, pt: (pt[s, c * SAMPLE_PAGES + i], 0, kind, 0, 0))
    return [spec(i) for i in range(SAMPLE_PAGES)]


def _sample_compress_kernel(pt_ref, *refs):
    n_in = 2 * SAMPLE_PAGES
    pages = refs[:n_in]
    pe_ref, w1_ref, a0_ref, a1_ref = refs[n_in:]
    d = NSA_HEAD_DIM
    per_page = PAGE_SIZE // CMP_STRIDE
    rows = SAMPLE_PAGES * per_page
    gs = NSA_KV_HEADS * CMP_STRIDE
    for kv in range(2):
        by_offset = [jnp.swapaxes(pages[kv * SAMPLE_PAGES + p].reshape(NSA_KV_HEADS * PAGE_SIZE, d)[...]
                                  .reshape(per_page, gs, d), 0, 1) for p in range(SAMPLE_PAGES)]
        acc0 = jnp.zeros((NSA_KV_HEADS * rows, d), F32)
        acc1 = jnp.zeros((NSA_KV_HEADS * rows, d), F32)
        for s in range(CMP_STRIDE):
            xs = jnp.concatenate([by_offset[p][NSA_KV_HEADS * s + g]
                                  for g in range(NSA_KV_HEADS) for p in range(SAMPLE_PAGES)],
                                 axis=0)
            a0 = (xs + pe_ref[kv, s:s + 1, :]).astype(BF16)
            a1 = (xs + pe_ref[kv, CMP_STRIDE + s:CMP_STRIDE + s + 1, :]).astype(BF16)
            acc0 = acc0 + jnp.dot(a0, w1_ref[kv, s], preferred_element_type=F32)
            acc1 = acc1 + jnp.dot(a1, w1_ref[kv, CMP_STRIDE + s], preferred_element_type=F32)
        for g in range(NSA_KV_HEADS):
            a0_ref[0, kv * NSA_KV_HEADS + g] = acc0[g * rows:(g + 1) * rows]
            a1_ref[0, kv * NSA_KV_HEADS + g] = acc1[g * rows:(g + 1) * rows]


def nsa_sample_compress(cache, page_table, cmp_pe, cmp_w1):
    bs, n_pages = page_table.shape
    d = NSA_HEAD_DIM
    per_page = PAGE_SIZE // CMP_STRIDE
    n_sub = n_pages * per_page
    rows = SAMPLE_PAGES * per_page
    w1 = cmp_w1.reshape(2, CMP_BLOCK, d, d).astype(BF16)
    out = jax.ShapeDtypeStruct((bs, 4, n_sub, d), F32)
    ospec = pl.BlockSpec((1, 4, rows, d), lambda s, c, pt: (s, 0, c, 0))
    page_specs = _page_specs(0) + _page_specs(1)
    return pl.pallas_call(
        _sample_compress_kernel,
        grid_spec=pltpu.PrefetchScalarGridSpec(
            num_scalar_prefetch=1,
            grid=(bs, n_pages // SAMPLE_PAGES),
            in_specs=page_specs + [
                pl.BlockSpec((2, CMP_BLOCK, d), lambda s, c, pt: (0, 0, 0)),
                pl.BlockSpec((2, CMP_BLOCK, d, d), lambda s, c, pt: (0, 0, 0, 0))],
            out_specs=[ospec, ospec]),
        out_shape=[out, out],
        compiler_params=pltpu.CompilerParams(
            dimension_semantics=("parallel", "arbitrary"), vmem_limit_bytes=VMEM_LIMIT),
        name="nsa_sample_compress",
    )(page_table, *([cache] * (2 * SAMPLE_PAGES)), cmp_pe, w1)


def _sample_select_kernel(a0_ref, a1_ref, w2_ref, q_ref, ocmp_ref, sel_ref, *, past_len, t_len):
    d, r = NSA_HEAD_DIM, NSA_GROUP
    n_sub = a0_ref.shape[2]
    n_cmp = n_sub - 1
    n_slc = -(-(past_len + t_len) // SLC_BLOCK)
    lanes = sel_ref.shape[3]
    nt = (((1,), (1,)), ((), ()))
    rq = r * t_len
    for g in range(NSA_KV_HEADS):
        kc = jnp.dot(_gelu_tanh(a0_ref[0, g] + pltpu.roll(a1_ref[0, g], n_sub - 1, 0)).astype(BF16), w2_ref[0],
                     preferred_element_type=F32).astype(BF16)
        vc = jnp.dot(_gelu_tanh(a0_ref[0, 2 + g] + pltpu.roll(a1_ref[0, 2 + g], n_sub - 1, 0)).astype(BF16),
                     w2_ref[1], preferred_element_type=F32).astype(BF16)
        q = q_ref[0, g]
        s = lax.dot_general(q, kc, nt, preferred_element_type=F32)
        pos = past_len + lax.broadcasted_iota(jnp.int32, (rq, 1), 0) % t_len
        n_idx = lax.broadcasted_iota(jnp.int32, (rq, n_sub), 1)
        p_c = _masked_softmax_rows(s, (n_idx * CMP_STRIDE + (CMP_BLOCK - 1) <= pos) & (n_idx < n_cmp))
        ocmp_ref[0, g] = jnp.dot(p_c.astype(BF16), vc, preferred_element_type=F32)
        ri = lax.broadcasted_iota(jnp.int32, (8, rq), 0)
        cj = lax.broadcasted_iota(jnp.int32, (8, rq), 1)
        head_sum = jnp.where(cj % t_len == ri, 1.0, 0.0).astype(BF16)
        p_hi = p_c.astype(BF16)
        p_lo = (p_c - p_hi.astype(F32)).astype(BF16)
        p_sum = (jnp.dot(head_sum, p_hi, preferred_element_type=F32)
                 + jnp.dot(head_sum, p_lo, preferred_element_type=F32))
        ci = lax.broadcasted_iota(jnp.int32, (n_sub, lanes), 0)
        mi = lax.broadcasted_iota(jnp.int32, (n_sub, lanes), 1)
        overlap = jnp.where((ci * CMP_STRIDE < mi * SLC_BLOCK + SLC_BLOCK)
                            & (ci * CMP_STRIDE + CMP_BLOCK > mi * SLC_BLOCK) & (ci < n_cmp), 1.0, 0.0).astype(BF16)
        s_hi = p_sum.astype(BF16)
        s_lo = (p_sum - s_hi.astype(F32)).astype(BF16)
        imp = (jnp.dot(s_hi, overlap, preferred_element_type=F32)
               + jnp.dot(s_lo, overlap, preferred_element_type=F32))
        blk = lax.broadcasted_iota(jnp.int32, (8, lanes), 1)
        blk_f = blk.astype(F32)
        tpos = past_len + lax.broadcasted_iota(jnp.int32, (8, 1), 0) % t_len
        cur = tpos // SLC_BLOCK
        forced = (blk == 0) | (blk == cur) | (blk == cur - 1)
        work = jnp.where(blk * SLC_BLOCK <= tpos, jnp.where(forced, 1e6, imp), -1e6)
        work = jnp.where(blk < n_slc, work, NEG_INF)
        sel = jnp.zeros((8, lanes), F32)
        for _ in range(min(SLC_TOPK, n_slc)):
            m = jnp.max(work, axis=-1, keepdims=True)
            first = jnp.min(jnp.where(work == m, blk_f, float(lanes)), axis=-1, keepdims=True)
            pick = blk_f == first
            sel = jnp.where(pick, 1.0, sel)
            work = jnp.where(pick, NEG_INF, work)
        sel_ref[0, g] = sel


def nsa_sample_select(a0, a1, cmp_w2, q_s, past_len, t_len):
    bs, _, n_sub, d = a0.shape
    n_slc = -(-(past_len + t_len) // SLC_BLOCK)
    lanes = _round_up(n_slc, LANES)
    rq = NSA_GROUP * t_len
    aspec = pl.BlockSpec((1, 4, n_sub, d), lambda s: (s, 0, 0, 0))
    return pl.pallas_call(
        functools.partial(_sample_select_kernel, past_len=past_len, t_len=t_len),
        grid=(bs,),
        in_specs=[aspec, aspec, pl.BlockSpec((2, d, d), lambda s: (0, 0, 0)),
                  pl.BlockSpec((1, NSA_KV_HEADS, rq, d), lambda s: (s, 0, 0, 0))],
        out_specs=[pl.BlockSpec((1, NSA_KV_HEADS, rq, d), lambda s: (s, 0, 0, 0)),
                   pl.BlockSpec((1, NSA_KV_HEADS, 8, lanes), lambda s: (s, 0, 0, 0))],
        out_shape=[jax.ShapeDtypeStruct((bs, NSA_KV_HEADS, rq, d), F32),
                   jax.ShapeDtypeStruct((bs, NSA_KV_HEADS, 8, lanes), F32)],
        compiler_params=pltpu.CompilerParams(dimension_semantics=("parallel",), vmem_limit_bytes=VMEM_LIMIT),
        name="nsa_sample_select",
    )(a0, a1, cmp_w2.astype(BF16), q_s)


def _sample_attend_kernel(pt_ref, *refs, past_len, t_len, row0):
    k_pages = refs[:SAMPLE_PAGES]
    v_pages = refs[SAMPLE_PAGES:2 * SAMPLE_PAGES]
    (q_ref, sel_ref, ocmp_ref, newrows_ref, kwin_ref, vwin_ref, newwin_ref, gate_ref, o_ref,
     m_scr, l_scr, acc_scr) = refs[2 * SAMPLE_PAGES:]
    s_idx = pl.program_id(0)
    c = pl.program_id(1)
    d, r = NSA_HEAD_DIM, NSA_GROUP
    rq = r * t_len
    lanes = sel_ref.shape[3]
    tk = SAMPLE_PAGES * PAGE_SIZE
    nt = (((1,), (1,)), ((), ()))

    @pl.when(c == 0)
    def _():
        m_scr[...] = jnp.full(m_scr.shape, -1e30, F32)
        l_scr[...] = jnp.zeros(l_scr.shape, F32)
        acc_scr[...] = jnp.zeros(acc_scr.shape, F32)

    row = lax.broadcasted_iota(jnp.int32, (rq, 1), 0)
    t_row = row % t_len
    pos = past_len + t_row
    ti = lax.broadcasted_iota(jnp.int32, (rq, 8), 1)
    tok_expand = jnp.where(ti == t_row, 1.0, 0.0).astype(BF16)

    def online(g, s, mask, v):
        s = jnp.where(mask, s, -1e30)
        m_old = m_scr[g]
        m_new = jnp.maximum(m_old, jnp.max(s, axis=-1, keepdims=True))
        alpha = jnp.exp(m_old - m_new)
        p = jnp.where(mask, jnp.exp(s - m_new), 0.0)
        l_scr[g] = alpha * l_scr[g] + jnp.sum(p, axis=-1, keepdims=True)
        acc_scr[g] = alpha * acc_scr[g] + jnp.dot(p.astype(BF16), v, preferred_element_type=F32)
        m_scr[g] = m_new

    sel16 = []
    for g in range(NSA_KV_HEADS):
        sel16.append(jnp.dot(tok_expand, sel_ref[0, g].astype(BF16), preferred_element_type=F32))
        k = jnp.concatenate([pg.reshape(NSA_KV_HEADS * PAGE_SIZE, d)[pl.ds(g, PAGE_SIZE, stride=NSA_KV_HEADS), :]
                             for pg in k_pages], axis=0)
        v = jnp.concatenate([pg.reshape(NSA_KV_HEADS * PAGE_SIZE, d)[pl.ds(g, PAGE_SIZE, stride=NSA_KV_HEADS), :]
                             for pg in v_pages], axis=0)
        q = q_ref[0, g]
        s = lax.dot_general(q, k.astype(BF16), nt, preferred_element_type=F32)
        ei = lax.broadcasted_iota(jnp.int32, (lanes, tk), 0)
        ej = lax.broadcasted_iota(jnp.int32, (lanes, tk), 1)
        expand = jnp.where(ei == c * (tk // SLC_BLOCK) + ej // SLC_BLOCK, 1.0, 0.0).astype(BF16)
        picked = jnp.dot(sel16[g].astype(BF16), expand, preferred_element_type=F32)
        kpos = c * tk + lax.broadcasted_iota(jnp.int32, (rq, tk), 1)
        online(g, s, (picked > 0.5) & (kpos <= pos), v.astype(BF16))

    @pl.when(c == pl.num_programs(1) - 1)
    def _():
        mine = (row0 // t_len + s_idx) % (NEW_ROWS_BLK // t_len)
        j = lax.broadcasted_iota(jnp.int32, (rq, NEW_ROWS_BLK), 1)
        own = j // t_len == mine
        new_pos = past_len + j % t_len
        sig = jax.nn.sigmoid(gate_ref[...])
        gsel = jnp.where(j == mine * t_len + t_row, 1.0, 0.0).astype(BF16)
        s_hi = sig.astype(BF16)
        s_lo = (sig - s_hi.astype(F32)).astype(BF16)
        sig_rows = (jnp.dot(gsel, s_hi, preferred_element_type=F32)
                    + jnp.dot(gsel, s_lo, preferred_element_type=F32))
        lane = lax.broadcasted_iota(jnp.int32, sig_rows.shape, 1)
        lane_l = lax.broadcasted_iota(jnp.int32, (rq, lanes), 1)
        for g in range(NSA_KV_HEADS):
            q = q_ref[0, g]
            kn = newrows_ref[:, (4 + g) * d:(5 + g) * d].astype(BF16)
            vn = newrows_ref[:, (6 + g) * d:(7 + g) * d].astype(BF16)
            s = lax.dot_general(q, kn, nt, preferred_element_type=F32)
            last_picked = jnp.sum(jnp.where(lane_l == past_len // SLC_BLOCK, sel16[g], 0.0), axis=-1, keepdims=True)
            online(g, s, own & (new_pos <= pos) & (last_picked > 0.5), vn)
            o_slc = acc_scr[g] / jnp.maximum(l_scr[g], 1e-30)
            w_len = kwin_ref.shape[1]
            win_rows = pl.ds(g, w_len, stride=NSA_KV_HEADS)
            kw = kwin_ref.reshape(NSA_KV_HEADS * w_len, d)[win_rows, :].astype(BF16)
            vw = vwin_ref.reshape(NSA_KV_HEADS * w_len, d)[win_rows, :].astype(BF16)
            s1 = lax.dot_general(q, kw, nt, preferred_element_type=F32)
            kp1 = past_len - w_len + lax.broadcasted_iota(jnp.int32, (rq, w_len), 1)
            d1 = pos - kp1
            m1 = (d1 >= 0) & (d1 < WINDOW) & (kp1 >= 0)
            knw = newwin_ref[:, g * d:(g + 1) * d].astype(BF16)
            vnw = newwin_ref[:, (2 + g) * d:(3 + g) * d].astype(BF16)
            s2 = lax.dot_general(q, knw, nt, preferred_element_type=F32)
            d2 = pos - new_pos
            m2 = own & (d2 >= 0) & (d2 < WINDOW)
            s1 = jnp.where(m1, s1, -1e30)
            s2 = jnp.where(m2, s2, -1e30)
            mx = jnp.maximum(jnp.max(s1, axis=-1, keepdims=True), jnp.max(s2, axis=-1, keepdims=True))
            p1 = jnp.where(m1, jnp.exp(s1 - mx), 0.0)
            p2 = jnp.where(m2, jnp.exp(s2 - mx), 0.0)
            den = jnp.sum(p1, axis=-1, keepdims=True) + jnp.sum(p2, axis=-1, keepdims=True)
            o_win = (jnp.dot(p1.astype(BF16), vw, preferred_element_type=F32)
                     + jnp.dot(p2.astype(BF16), vnw, preferred_element_type=F32)) / jnp.maximum(den, 1e-30)
            out = jnp.zeros((rq, d), F32)
            for branch, o_b in enumerate((ocmp_ref[0, g], o_slc, o_win)):
                col = branch * NSA_HEADS + g * r + row // t_len
                gate = jnp.sum(jnp.where(lane == col, sig_rows, 0.0), axis=-1, keepdims=True)
                out = out + gate * o_b
            o_ref[0, g] = out


def nsa_sample_attend(cache, page_table, q_s, sel, o_cmp, rows, win, win_cache, proj, past_len, t_len, row0):
    bs, n_pages = page_table.shape
    d = NSA_HEAD_DIM
    rq = NSA_GROUP * t_len
    lanes = sel.shape[3]
    w_len = win_cache.shape[1]
    assert row0 % t_len == 0 and NEW_ROWS_BLK % t_len == 0 and past_len % SLC_BLOCK == 0
    blk = lambda s: (row0 + s * t_len) // NEW_ROWS_BLK
    per_seq = lambda shape: pl.BlockSpec((1,) + shape, lambda s, c, pt: (s, 0, 0, 0))
    return pl.pallas_call(
        functools.partial(_sample_attend_kernel, past_len=past_len, t_len=t_len, row0=row0),
        grid_spec=pltpu.PrefetchScalarGridSpec(
            num_scalar_prefetch=1,
            grid=(bs, n_pages // SAMPLE_PAGES),
            in_specs=_page_specs(2) + _page_specs(3) + [
                per_seq((NSA_KV_HEADS, rq, d)), per_seq((NSA_KV_HEADS, 8, lanes)), per_seq((NSA_KV_HEADS, rq, d)),
                pl.BlockSpec((NEW_ROWS_BLK, 8 * d), lambda s, c, pt: (blk(s), 0)),
                pl.BlockSpec((1, w_len, 1, NSA_KV_HEADS, d), lambda s, c, pt: (s, 0, 0, 0, 0)),
                pl.BlockSpec((1, w_len, 1, NSA_KV_HEADS, d), lambda s, c, pt: (s, 0, 1, 0, 0)),
                pl.BlockSpec((NEW_ROWS_BLK, 4 * d), lambda s, c, pt: (blk(s), 0)),
                pl.BlockSpec((NEW_ROWS_BLK, LANES), lambda s, c, pt: (blk(s), GATE_COL0 // LANES))],
            out_specs=per_seq((NSA_KV_HEADS, rq, d)),
            scratch_shapes=[pltpu.VMEM((NSA_KV_HEADS, rq, 1), F32), pltpu.VMEM((NSA_KV_HEADS, rq, 1), F32),
                            pltpu.VMEM((NSA_KV_HEADS, rq, d), F32)]),
        out_shape=jax.ShapeDtypeStruct((bs, NSA_KV_HEADS, rq, d), F32),
        compiler_params=pltpu.CompilerParams(
            dimension_semantics=("parallel", "arbitrary"), vmem_limit_bytes=VMEM_LIMIT),
        name="nsa_sample_attend",
    )(page_table, *([cache] * (2 * SAMPLE_PAGES)), q_s, sel, o_cmp, rows, win_cache, win_cache, win, proj)


SC_ROWS = 512
SC_COLS = 512


def _short_conv_kernel(b_ref, c_ref, h_ref, w_ref, o_ref, tail_ref, prev_scr):
    @pl.when(pl.program_id(2) == 0)
    def _():
        prev_scr[...] = jnp.zeros_like(prev_scr)

    rows = c_ref.shape[0]
    u = c_ref[...] * h_ref[...]
    prev = prev_scr[...]
    row8 = lax.broadcasted_iota(jnp.int32, prev.shape, 0)
    conv = u * w_ref[SC_KSIZE - 1:SC_KSIZE, :]
    for k in range(1, SC_KSIZE):
        rolled = pltpu.roll(u, k, 0)
        top = jnp.where(row8 < k, pltpu.roll(prev, k, 0), rolled[0:8])
        shifted = jnp.concatenate([top, rolled[8:]], axis=0)
        conv = conv + shifted * w_ref[SC_KSIZE - 1 - k:SC_KSIZE - k, :]
    prev_scr[...] = u[rows - 8:rows]
    o_ref[...] = (b_ref[...] * conv).astype(o_ref.dtype)
    tail_ref[0] = u[rows - 8:rows]


def short_conv_prompt(proj, n_batch, t_len, sc_w):
    nr = t_len // SC_ROWS
    nh = SC_WIDTH // SC_COLS
    blk0 = SC_COL0 // SC_COLS
    assert SC_COL0 % SC_COLS == 0
    col = lambda part: pl.BlockSpec((SC_ROWS, SC_COLS), lambda b, j, i: (b * nr + i, blk0 + part * nh + j))
    out, tail = pl.pallas_call(
        _short_conv_kernel,
        grid=(n_batch, nh, nr),
        in_specs=[col(0), col(1), col(2), pl.BlockSpec((SC_KSIZE, SC_COLS), lambda b, j, i: (0, j))],
        out_specs=[pl.BlockSpec((SC_ROWS, SC_COLS), lambda b, j, i: (b * nr + i, j)),
                   pl.BlockSpec((1, 8, SC_COLS), lambda b, j, i: (b, 0, j))],
        out_shape=[jax.ShapeDtypeStruct((n_batch * t_len, SC_WIDTH), BF16),
                   jax.ShapeDtypeStruct((n_batch, 8, SC_WIDTH), F32)],
        scratch_shapes=[pltpu.VMEM((8, SC_COLS), F32)],
        compiler_params=pltpu.CompilerParams(
            dimension_semantics=("parallel", "parallel", "arbitrary"), vmem_limit_bytes=VMEM_LIMIT),
        name="short_conv",
    )(proj, proj, proj, sc_w)
    return out, tail[:, 8 - (SC_KSIZE - 1):]


def _ret_tables(pos, chunk):
    half = RET_DQK // 2
    inv = RET_THETA ** (-jnp.arange(half, dtype=F32) / half)
    ang = pos.astype(F32)[:, None] * inv[None, :]
    log_g = jnp.log1p(-(2.0 ** (-5.0 - jnp.arange(RET_HEADS, dtype=F32))))
    i = jnp.arange(chunk, dtype=F32)
    diff = i[:, None] - i[None, :]
    intra = jnp.where(diff >= 0, jnp.exp(jnp.maximum(diff, 0.0)[None] * log_g[:, None, None]), 0.0)
    q_dec = jnp.exp((i[None, :] + 1.0) * log_g[:, None])[..., None]
    k_dec = jnp.exp((chunk - 1.0 - i)[None, :] * log_g[:, None])[..., None]
    c_dec = jnp.exp(chunk * log_g)[:, None, None]
    return jnp.cos(ang), jnp.sin(ang), intra, q_dec, k_dec, c_dec


def _retention_kernel(q_ref, k_ref, v_ref, g_ref, cos_ref, sin_ref, intra_ref, qd_ref, kd_ref, cd_ref, gn_ref,
                      s0_ref, o_ref, s_out_ref, s_scr):
    c_idx = pl.program_id(1)

    @pl.when(c_idx == 0)
    def _():
        s_scr[...] = s0_ref[0]

    cos, sin = cos_ref[...], sin_ref[...]
    half = RET_DQK // 2
    nt = (((1,), (1,)), ((), ()))
    tn = (((0,), (0,)), ((), ()))

    def rot(x):
        x1, x2 = x[:, :half], x[:, half:]
        return jnp.concatenate([x1 * cos - x2 * sin, x1 * sin + x2 * cos], axis=-1)

    for h in range(RET_HEADS):
        cols = slice(h * RET_DQK, (h + 1) * RET_DQK)
        qr = (rot(q_ref[:, cols]) * (RET_DQK ** -0.5)).astype(BF16)
        kr = rot(k_ref[:, cols])
        v = v_ref[:, cols].astype(BF16)
        att = lax.dot_general(qr, kr.astype(BF16), nt, preferred_element_type=F32) * intra_ref[h]
        s_old = s_scr[h]
        o = (jnp.dot(att.astype(BF16), v, preferred_element_type=F32)
             + jnp.dot(qr, s_old.astype(BF16), preferred_element_type=F32) * qd_ref[h])
        s_scr[h] = s_old * cd_ref[h] + lax.dot_general((kr * kd_ref[h]).astype(BF16), v, tn,
                                                       preferred_element_type=F32)
        mu = jnp.mean(o, axis=-1, keepdims=True)
        dev = o - mu
        var = jnp.mean(dev * dev, axis=-1, keepdims=True)
        gate = g_ref[:, cols]
        on = dev * lax.rsqrt(var + EPS) * gn_ref[:, cols] * (gate * jax.nn.sigmoid(gate))
        o_ref[:, cols] = on.astype(o_ref.dtype)
    s_out_ref[0] = s_scr[...]


def retention(proj, pos, n_batch, t_len, chunk, state0, gn_gain):
    nc = t_len // chunk
    cos, sin, intra, q_dec, k_dec, c_dec = _ret_tables(pos, chunk)
    half = RET_DQK // 2
    col = lambda j: pl.BlockSpec((chunk, R_QK), lambda b, c: (b * nc + c, j))
    tab = pl.BlockSpec((chunk, half), lambda b, c: (c, 0))
    full = lambda a: pl.BlockSpec(a.shape, lambda b, c: (0,) * a.ndim)
    st = pl.BlockSpec((1, RET_HEADS, RET_DQK, RET_DV), lambda b, c: (b, 0, 0, 0))
    return pl.pallas_call(
        _retention_kernel,
        grid=(n_batch, nc),
        in_specs=[col(0), col(1), col(2), col(3), tab, tab, full(intra), full(q_dec), full(k_dec), full(c_dec),
                  pl.BlockSpec((1, R_V), lambda b, c: (0, 0)), st],
        out_specs=[pl.BlockSpec((chunk, R_V), lambda b, c: (b * nc + c, 0)), st],
        out_shape=[jax.ShapeDtypeStruct((n_batch * t_len, R_V), BF16),
                   jax.ShapeDtypeStruct((n_batch, RET_HEADS, RET_DQK, RET_DV), F32)],
        scratch_shapes=[pltpu.VMEM((RET_HEADS, RET_DQK, RET_DV), F32)],
        compiler_params=pltpu.CompilerParams(
            dimension_semantics=("parallel", "arbitrary"), vmem_limit_bytes=VMEM_LIMIT),
        name="retention",
    )(proj, proj, proj, proj, cos, sin, intra, q_dec, k_dec, c_dec, gn_gain.reshape(1, R_V), state0)


SSD_COL0 = 2 * R_QK + 2 * R_V
HEADS_PER_GROUP = M2_HEADS // M2_GROUPS


def _split3(x):
    a = x.astype(BF16)
    r = x - a.astype(F32)
    b = r.astype(BF16)
    c = (r - b.astype(F32)).astype(BF16)
    return a, b, c


def _exact_dot(mat_bf, x):
    out = None
    for piece in _split3(x):
        t = jnp.dot(mat_bf, piece, preferred_element_type=F32)
        out = t if out is None else out + t
    return out


def _exact_dot_r(x, mat_bf):
    out = None
    for piece in _split3(x):
        t = jnp.dot(piece, mat_bf, preferred_element_type=F32)
        out = t if out is None else out + t
    return out


def _ssd_kernel(z_ref, xa_ref, xb_ref, xc_ref, dt_ref, cw_ref, cb_ref, dtb_ref, aneg_ref, dskip_ref, norm_ref,
                buf0_ref, s0_ref, o_ref, s_out_ref, s_scr, prev_scr):
    c_idx = pl.program_id(1)
    chunk = z_ref.shape[0]
    nt = (((1,), (1,)), ((), ()))
    tn = (((0,), (0,)), ((), ()))

    @pl.when(c_idx == 0)
    def _():
        s_scr[...] = s0_ref[0]
        prev_scr[...] = buf0_ref[0]

    x = jnp.concatenate([xa_ref[...], xb_ref[...], xc_ref[...]], axis=1)
    prev = prev_scr[...]
    row8 = lax.broadcasted_iota(jnp.int32, prev.shape, 0)
    conv = x * cw_ref[M2_CONV - 1:M2_CONV, :]
    for k in range(1, M2_CONV):
        rolled = pltpu.roll(x, k, 0)
        top = jnp.where(row8 < k, pltpu.roll(prev, k, 0), rolled[0:8])
        shifted = jnp.concatenate([top, rolled[8:]], axis=0)
        conv = conv + shifted * cw_ref[M2_CONV - 1 - k:M2_CONV - k, :]
    prev_scr[...] = x[chunk - 8:chunk]
    conv = conv + cb_ref[...]
    xbc = conv * jax.nn.sigmoid(conv)
    xs = xbc[:, :M2_DINNER]

    dt_raw = dt_ref[...] + dtb_ref[...]
    dt = jnp.where(dt_raw > 20.0, dt_raw, jnp.log1p(jnp.exp(jnp.minimum(dt_raw, 20.0))))
    a = dt * aneg_ref[...]
    ri = lax.broadcasted_iota(jnp.int32, (chunk, chunk), 0)
    ci = lax.broadcasted_iota(jnp.int32, (chunk, chunk), 1)
    tri = ri >= ci
    cum = _exact_dot(jnp.where(tri, 1.0, 0.0).astype(BF16), a)
    cum_t = cum.T
    cum_last = cum[chunk - 1:chunk, :]
    hi = lax.broadcasted_iota(jnp.int32, (LANES, M2_DINNER), 0)
    li = lax.broadcasted_iota(jnp.int32, (LANES, M2_DINNER), 1)
    expand = jnp.where(hi == li // M2_HEADDIM, 1.0, 0.0).astype(BF16)
    dt_x = _exact_dot_r(dt, expand)
    cum_x = _exact_dot_r(cum, expand)
    last_x = _exact_dot_r(cum_last, expand)
    xdt = xs * dt_x
    x_dec = (xdt * jnp.exp(last_x - cum_x)).astype(BF16)
    xdt_bf = xdt.astype(BF16)
    e_cum_x = jnp.exp(cum_x)
    e_last_x = jnp.exp(last_x)

    y_parts = []
    for gi in range(M2_GROUPS):
        b_g = xbc[:, M2_DINNER + gi * M2_STATE:M2_DINNER + (gi + 1) * M2_STATE].astype(BF16)
        c_g = xbc[:, M2_DINNER + (M2_GROUPS + gi) * M2_STATE:M2_DINNER + (M2_GROUPS + gi + 1) * M2_STATE].astype(BF16)
        cb = lax.dot_general(c_g, b_g, nt, preferred_element_type=F32)
        gcols = slice(gi * HEADS_PER_GROUP * M2_HEADDIM, (gi + 1) * HEADS_PER_GROUP * M2_HEADDIM)
        s_old = s_scr[:, gcols]
        y_state = jnp.dot(c_g, s_old.astype(BF16), preferred_element_type=F32) * e_cum_x[:, gcols]
        s_scr[:, gcols] = s_old * e_last_x[:, gcols] + lax.dot_general(b_g, x_dec[:, gcols], tn,
                                                                        preferred_element_type=F32)
        pair_lane = lax.broadcasted_iota(jnp.int32, (chunk, LANES), 1)
        intra = []
        for pr in range(HEADS_PER_GROUP // 2):
            outs = []
            for sub in range(2):
                h = gi * HEADS_PER_GROUP + pr * 2 + sub
                seg = cum[:, h:h + 1] - cum_t[h:h + 1, :]
                l_mat = jnp.where(tri, jnp.exp(jnp.where(tri, seg, 0.0)), 0.0)
                lanes = slice(gi * HEADS_PER_GROUP * M2_HEADDIM + pr * LANES,
                              gi * HEADS_PER_GROUP * M2_HEADDIM + (pr + 1) * LANES)
                outs.append(jnp.dot((cb * l_mat).astype(BF16), xdt_bf[:, lanes], preferred_element_type=F32))
            intra.append(jnp.where(pair_lane < M2_HEADDIM, outs[0], outs[1]))
        y_parts.append(jnp.concatenate(intra, axis=1) + y_state)
    y = jnp.concatenate(y_parts, axis=1) + dskip_ref[...] * xs
    z = z_ref[...]
    y = y * (z * jax.nn.sigmoid(z))
    gw = M2_DINNER // M2_GROUPS
    outs = []
    for gi in range(M2_GROUPS):
        yg = y[:, gi * gw:(gi + 1) * gw]
        outs.append(yg * lax.rsqrt(jnp.mean(yg * yg, axis=-1, keepdims=True) + EPS))
    o_ref[...] = (jnp.concatenate(outs, axis=1) * norm_ref[...]).astype(o_ref.dtype)
    s_out_ref[0] = s_scr[...]


def ssd(proj, n_batch, t_len, chunk, conv_buf, state0, conv_w, conv_b, dt_bias, a_log, d_skip, m2_norm):
    nc = t_len // chunk
    z_blk = SSD_COL0 // M2_DINNER
    xw = M2_CONV_DIM // 3
    xbc_blk = (SSD_COL0 + M2_DINNER) // xw
    assert (SSD_COL0 + M2_DINNER) % xw == 0 and xw % LANES == 0
    dt_blk = (SSD_COL0 + M2_DINNER + M2_CONV_DIM) // LANES
    pad = lambda v: jnp.pad(v.reshape(1, -1), ((0, 0), (0, LANES - v.shape[-1])))
    buf8 = jnp.pad(conv_buf, ((0, 0), (8 - (M2_CONV - 1), 0), (0, 0)))
    st_t = state0.transpose(0, 3, 1, 2).reshape(n_batch, M2_STATE, M2_DINNER)
    row = lambda a: pl.BlockSpec(a.shape, lambda b, c: (0, 0))
    cw = conv_w
    cb = conv_b.reshape(1, -1)
    dtb, aneg = pad(dt_bias), pad(-jnp.exp(a_log))
    dsk = jnp.repeat(d_skip, M2_HEADDIM).reshape(1, -1)
    nrm = m2_norm.reshape(1, -1)
    st = pl.BlockSpec((1, M2_STATE, M2_DINNER), lambda b, c: (b, 0, 0))
    out, s_fin = pl.pallas_call(
        _ssd_kernel,
        grid=(n_batch, nc),
        in_specs=[pl.BlockSpec((chunk, M2_DINNER), lambda b, c: (b * nc + c, z_blk)),
                  pl.BlockSpec((chunk, xw), lambda b, c: (b * nc + c, xbc_blk)),
                  pl.BlockSpec((chunk, xw), lambda b, c: (b * nc + c, xbc_blk + 1)),
                  pl.BlockSpec((chunk, xw), lambda b, c: (b * nc + c, xbc_blk + 2)),
                  pl.BlockSpec((chunk, LANES), lambda b, c: (b * nc + c, dt_blk)),
                  row(cw), row(cb), row(dtb), row(aneg), row(dsk), row(nrm),
                  pl.BlockSpec((1, 8, M2_CONV_DIM), lambda b, c: (b, 0, 0)), st],
        out_specs=[pl.BlockSpec((chunk, M2_DINNER), lambda b, c: (b * nc + c, 0)), st],
        out_shape=[jax.ShapeDtypeStruct((n_batch * t_len, M2_DINNER), BF16),
                   jax.ShapeDtypeStruct((n_batch, M2_STATE, M2_DINNER), F32)],
        scratch_shapes=[pltpu.VMEM((M2_STATE, M2_DINNER), F32), pltpu.VMEM((8, M2_CONV_DIM), F32)],
        compiler_params=pltpu.CompilerParams(
            dimension_semantics=("parallel", "arbitrary"), vmem_limit_bytes=VMEM_LIMIT),
        name="ssd",
    )(proj, proj, proj, proj, proj, cw, cb, dtb, aneg, dsk, nrm, buf8, st_t)
    s_fin = s_fin.reshape(n_batch, M2_STATE, M2_HEADS, M2_HEADDIM).transpose(0, 2, 3, 1)
    return out, s_fin


def _rotary(x, pos, rot_dims, theta):
    half = rot_dims // 2
    inv = theta ** (-jnp.arange(half, dtype=F32) / half)
    ang = pos.astype(F32)[:, None] * inv[None, :]
    shape = (pos.shape[0],) + (1,) * (x.ndim - 3) + (half,)
    cos = jnp.cos(ang).reshape(shape)
    sin = jnp.sin(ang).reshape(shape)
    x1 = x[..., :half]
    x2 = x[..., half:rot_dims]
    rot = jnp.concatenate([x1 * cos - x2 * sin, x1 * sin + x2 * cos], axis=-1)
    return jnp.concatenate([rot, x[..., rot_dims:]], axis=-1)


def _masked_softmax(s, mask, axis=-1):
    s = jnp.where(mask, s, -1e30)
    p = jnp.exp(s - jnp.max(s, axis=axis, keepdims=True)) * mask
    return p / jnp.maximum(jnp.sum(p, axis=axis, keepdims=True), 1e-30)


def _causal_conv(u, buf, w, b=None):
    k_size = w.shape[0]
    t_len = u.shape[1]
    ext = jnp.concatenate([buf.astype(u.dtype), u], axis=1)
    y = ext[:, 0:t_len] * w[0]
    for j in range(1, k_size):
        y = y + ext[:, j:j + t_len] * w[j]
    if b is not None:
        y = y + b
    return y, ext[:, t_len:]


def _to_chunks(a, chunk):
    return a.reshape(a.shape[0], a.shape[1] // chunk, chunk, *a.shape[2:]).swapaxes(0, 1)


def _nsa_compress_jnp(rows, pe, w1, w2):
    bn, length, g, d = rows.shape
    n_part = CMP_BLOCK // CMP_STRIDE
    n_sub = length // CMP_STRIDE
    n_cmp = n_sub - n_part + 1
    sub = rows[:, :n_sub * CMP_STRIDE].reshape(bn, n_sub, CMP_STRIDE, g, d)
    w1p = w1.reshape(n_part, CMP_STRIDE, d, d)
    pre = jnp.einsum('jsd,jsde->e', pe.reshape(n_part, CMP_STRIDE, d), w1p)
    for j in range(n_part):
        pre = pre + jnp.einsum('bnsgd,sde->bnge', sub[:, j:j + n_cmp], w1p[j])
    return jnp.einsum('bnge,ef->bngf', jax.nn.gelu(pre), w2)


def _nsa_sparse_jnp(q, q_pos, rows, cmp_pe, cmp_w1, cmp_w2):
    bn, t_len, g, r, d = q.shape
    length = rows.shape[1]
    kc = _nsa_compress_jnp(rows[:, :, 0], cmp_pe[0], cmp_w1[0], cmp_w2[0])
    vc = _nsa_compress_jnp(rows[:, :, 1], cmp_pe[1], cmp_w1[1], cmp_w2[1])
    n_cmp = kc.shape[1]
    c_start = jnp.arange(n_cmp) * CMP_STRIDE
    c_mask = (c_start + CMP_BLOCK - 1)[None, :] <= q_pos[:, None]
    p_c = _masked_softmax(jnp.einsum('btgrd,bngd->btgrn', q, kc), c_mask[None, :, None, None, :])
    o_cmp = jnp.einsum('btgrn,bngd->btgrd', p_c, vc)
    n_slc = -(-length // SLC_BLOCK)
    s_start = jnp.arange(n_slc) * SLC_BLOCK
    overlap = ((c_start[:, None] < s_start[None, :] + SLC_BLOCK)
               & (c_start[:, None] + CMP_BLOCK > s_start[None, :])).astype(F32)
    imp = jnp.einsum('btgrn,nm->btgm', p_c, overlap)
    blk = jnp.arange(n_slc)[None, :]
    cur = (q_pos // SLC_BLOCK)[:, None]
    valid = (s_start[None, :] <= q_pos[:, None])[None, :, None, :]
    forced = ((blk == 0) | (blk == cur) | (blk == cur - 1))[None, :, None, :]
    score = jnp.where(valid, jnp.where(forced, 1e6, imp), -1e6)
    n_sel = min(SLC_TOPK, n_slc)
    _, idx = lax.top_k(score, n_sel)
    picked = jnp.any(idx[..., None] == jnp.arange(n_slc), axis=-2)
    kmask = jnp.repeat(picked, SLC_BLOCK, axis=-1)[..., :length]
    kmask = kmask & (jnp.arange(length)[None, :] <= q_pos[:, None])[None, :, None, :]
    s = jnp.einsum('btgrd,blgd->btgrl', q, rows[:, :, 2])
    p = _masked_softmax(s, kmask[:, :, :, None, :])
    o_slc = jnp.einsum('btgrl,blgd->btgrd', p, rows[:, :, 3])
    return o_cmp, o_slc


def _window_attention_jnp(q, q_pos, k, v, k_pos):
    s = jnp.einsum('ntgrd,nsgd->ntgrs', q, k)
    dpos = q_pos[:, :, None] - k_pos[:, None, :]
    mask = (dpos >= 0) & (dpos < WINDOW) & (k_pos[:, None, :] >= 0)
    p = _masked_softmax(s, mask[:, :, None, None, :])
    return jnp.einsum('ntgrs,nsgd->ntgrd', p, v)


def _nsa_sample_jnp(q, new_rows, win_rows, gates, pos, past_rows, win_buf, cmp_pe, cmp_w1, cmp_w2):
    bn, t_len, g, r, d = q.shape
    rows = jnp.concatenate([past_rows, new_rows], axis=1)
    o_cmp, o_slc = _nsa_sparse_jnp(q, pos, rows, cmp_pe, cmp_w1, cmp_w2)
    w_len = win_buf.shape[1]
    ext = jnp.concatenate([win_buf, win_rows], axis=1)
    k_pos = (pos[0] - w_len + jnp.arange(w_len + t_len))[None]
    o_win = _window_attention_jnp(q, pos[None], ext[:, :, 0], ext[:, :, 1], k_pos)
    gate = jax.nn.sigmoid(gates).reshape(bn, t_len, 3, g, r, 1)
    o_nsa = gate[:, :, 0] * o_cmp + gate[:, :, 1] * o_slc + gate[:, :, 2] * o_win
    return o_nsa.reshape(bn * t_len, E_Q), ext[:, t_len:]


def _retention_jnp(q, k, v, state, chunk):
    bn, t_len, h, _ = q.shape
    dv = v.shape[-1]
    log_g = jnp.log1p(-(2.0 ** (-5.0 - jnp.arange(h, dtype=F32))))
    i = jnp.arange(chunk, dtype=F32)
    diff = i[:, None] - i[None, :]
    intra = jnp.where(diff >= 0, jnp.exp(jnp.maximum(diff, 0.0)[None] * log_g[:, None, None]), 0.0)
    q_dec = jnp.exp((i[:, None] + 1.0) * log_g[None, :])
    k_dec = jnp.exp((chunk - 1.0 - i)[:, None] * log_g[None, :])
    c_dec = jnp.exp(chunk * log_g)

    def step(s_mat, inp):
        qc, kc, vc = inp
        att = jnp.einsum('bihd,bjhd->bhij', qc, kc) * intra
        o = (jnp.einsum('bhij,bjhv->bihv', att, vc)
             + jnp.einsum('bihd,bhdv->bihv', qc, s_mat) * q_dec[None, :, :, None])
        s_mat = s_mat * c_dec[None, :, None, None] + jnp.einsum('bjhd,bjhv->bhdv', kc * k_dec[None, :, :, None], vc)
        return s_mat, o

    s_fin, o = lax.scan(step, state, (_to_chunks(q, chunk), _to_chunks(k, chunk), _to_chunks(v, chunk)))
    return o.swapaxes(0, 1).reshape(bn, t_len, h, dv), s_fin


def _ssd_jnp(x, dt, a_neg, b_in, c_in, state, chunk):
    bn, t_len, h, p = x.shape
    rep = h // b_in.shape[2]
    a = dt * a_neg
    b_h = jnp.repeat(b_in, rep, axis=2)
    c_h = jnp.repeat(c_in, rep, axis=2)
    xdt = x * dt[..., None]
    tri = jnp.arange(chunk)[:, None] >= jnp.arange(chunk)[None, :]

    def step(s_mat, inp):
        a_c, x_c, b_c, c_c = inp
        cum = jnp.cumsum(a_c, axis=1)
        seg = cum[:, :, None, :] - cum[:, None, :, :]
        l_mat = jnp.exp(jnp.where(tri[None, :, :, None], seg, -1e30))
        cb = jnp.einsum('bihn,bjhn->bijh', c_c, b_c) * l_mat
        y = (jnp.einsum('bijh,bjhp->bihp', cb, x_c)
             + jnp.einsum('bihn,bhpn->bihp', c_c, s_mat) * jnp.exp(cum)[..., None])
        dec_end = jnp.exp(cum[:, -1:, :] - cum)
        s_mat = (s_mat * jnp.exp(cum[:, -1, :])[:, :, None, None]
                 + jnp.einsum('bjhn,bjhp->bhpn', b_c * dec_end[..., None], x_c))
        return s_mat, y

    s_fin, y = lax.scan(step, state, (_to_chunks(a, chunk), _to_chunks(xdt, chunk), _to_chunks(b_h, chunk),
                                      _to_chunks(c_h, chunk)))
    return y.swapaxes(0, 1).reshape(bn, t_len, h, p), s_fin


def _odd_mixer_jnp(proj, pos, ret_state, ssm_state, conv_buf, gn_gain, conv_w, conv_b, dt_bias, a_log, d_skip,
                   m2_norm):
    bn, t_len, _ = proj.shape
    chunk = CHUNK if t_len % CHUNK == 0 else t_len
    q, k, v, g, z, xbc, dt = jnp.split(proj, O_SPLITS, axis=-1)
    q = _rotary(q.reshape(bn, t_len, RET_HEADS, RET_DQK), pos, RET_DQK, RET_THETA) * (RET_DQK ** -0.5)
    k = _rotary(k.reshape(bn, t_len, RET_HEADS, RET_DQK), pos, RET_DQK, RET_THETA)
    v = v.reshape(bn, t_len, RET_HEADS, RET_DV)
    o_ret, ret_new = _retention_jnp(q, k, v, ret_state, chunk)
    mu = jnp.mean(o_ret, axis=-1, keepdims=True)
    var = jnp.mean(jnp.square(o_ret - mu), axis=-1, keepdims=True)
    o_ret = ((o_ret - mu) * lax.rsqrt(var + EPS)).reshape(bn, t_len, R_V) * gn_gain
    o_ret = jax.nn.silu(g) * o_ret
    xbc_c, _ = _causal_conv(xbc, conv_buf, conv_w, conv_b)
    xbc_c = jax.nn.silu(xbc_c)
    xs, b_in, c_in = jnp.split(xbc_c, [M2_DINNER, M2_DINNER + M2_GROUPS * M2_STATE], axis=-1)
    xs = xs.reshape(bn, t_len, M2_HEADS, M2_HEADDIM)
    b_in = b_in.reshape(bn, t_len, M2_GROUPS, M2_STATE)
    c_in = c_in.reshape(bn, t_len, M2_GROUPS, M2_STATE)
    dt = jax.nn.softplus(dt + dt_bias)
    a_neg = -jnp.exp(a_log)
    y, ssm_new = _ssd_jnp(xs, dt, a_neg, b_in, c_in, ssm_state, chunk)
    y = y + d_skip[:, None] * xs
    y = y.reshape(bn, t_len, M2_DINNER) * jax.nn.silu(z)
    yg = y.reshape(bn, t_len, M2_GROUPS, M2_DINNER // M2_GROUPS)
    yg = yg * lax.rsqrt(jnp.mean(yg * yg, axis=-1, keepdims=True) + EPS)
    y = yg.reshape(bn, t_len, M2_DINNER) * m2_norm
    return o_ret.reshape(bn * t_len, R_V), y.reshape(bn * t_len, M2_DINNER), ret_new, ssm_new


def _short_conv_jnp(sc, n_batch, t_len, conv_buf, sc_w):
    b_gate, c_gate, h_in = jnp.split(sc.reshape(n_batch, t_len, E_SC), 3, axis=-1)
    conv_out, new_conv = _causal_conv(c_gate * h_in, conv_buf, sc_w)
    return (b_gate * conv_out).reshape(n_batch * t_len, SC_WIDTH), new_conv


def kernel(x_prompt, x_sample, cache_nsa_kv, cache_nsa_win, state_sc_conv, state_ret, state_ssm, state_m2_conv,
           page_table, norm_w, final_norm, e_w_in, e_w_out, e_cmp_pe, e_cmp_w1, e_cmp_w2, e_sc_conv, o_w_in,
           o_w_out, o_ret_gn, o_m2_conv_w, o_m2_conv_b, o_m2_dt_bias, o_m2_a_log, o_m2_d, o_m2_norm, peer_wq,
           peer_keys, peer_u, peer_v):
    bp, tp, dm = x_prompt.shape
    bs, ts, _ = x_sample.shape
    n_p, n_s = bp * tp, bs * ts
    g, r, d = NSA_KV_HEADS, NSA_GROUP, NSA_HEAD_DIM
    past_len = page_table.shape[1] * PAGE_SIZE
    pos_p = jnp.arange(tp, dtype=jnp.int32)
    pos_s = past_len + jnp.arange(ts, dtype=jnp.int32)
    n_real = n_p + n_s
    n_pad = _round_up(n_real, PEER_TM) - n_real
    s_rows = slice(n_p, n_real)
    pos_rows = jnp.concatenate([jnp.tile(pos_p, bp), jnp.tile(pos_s, bs), jnp.zeros((n_pad,), jnp.int32)])
    xp = x_prompt.reshape(n_p, dm)
    x = jnp.concatenate([xp, x_sample.reshape(n_s, dm), xp[:n_pad]], axis=0)
    pad_bf = jnp.zeros((n_pad, E_Q), BF16)

    w0 = e_w_in[0]
    w_in = jnp.concatenate([w0[:, :SC_COL0], w0[:, SC_COL0 + E_G:], w0[:, SC_COL0:SC_COL0 + E_G],
                            jnp.zeros((dm, _pad_cols(E_IN) - E_IN), F32)], axis=1).astype(BF16)
    proj = norm_matmul(x, norm_w[0, 0], w_in)
    q_bf, rows, win, rows_bf, win_bf = nsa_rope(proj, pos_rows)
    kvc = nsa_compress(rows, bp, tp, e_cmp_pe[0], e_cmp_w1[0], e_cmp_w2[0])
    o_nsa_p = nsa_prompt(q_bf, kvc, rows_bf, win_bf, proj, bp, tp)
    p_kv = rows[:n_p].reshape(bp, tp, 4, g, d)
    s_kv = rows[s_rows].reshape(bs, ts, 4, g, d)
    p_win = win[:n_p].reshape(bp, tp, 2, g, d)[:, tp - min(WINDOW, tp):]
    win_s = win[s_rows].reshape(bs, ts, 2, g, d)
    s_win = jnp.concatenate([cache_nsa_win[0], win_s], axis=1)[:, ts:]
    cache = cache_nsa_kv[0]
    q_s = q_bf[s_rows].reshape(bs, ts, g, r, d).transpose(0, 2, 3, 1, 4).reshape(bs, g, r * ts, d)
    a0, a1 = nsa_sample_compress(cache, page_table, e_cmp_pe[0], e_cmp_w1[0])
    o_cmp_s, sel_s = nsa_sample_select(a0, a1, e_cmp_w2[0], q_s, past_len, ts)
    o_nsa_s = nsa_sample_attend(cache, page_table, q_s, sel_s, o_cmp_s, rows, win,
                                cache_nsa_win[0], proj, past_len, ts, n_p)
    o_nsa_s = o_nsa_s.reshape(bs, g, r, ts, d).transpose(0, 3, 1, 2, 4).reshape(n_s, E_Q)
    o_nsa = jnp.concatenate([o_nsa_p, o_nsa_s.astype(BF16), pad_bf], axis=0)
    o_sc_p, p_sc = short_conv_prompt(proj, bp, tp, e_sc_conv[0])
    o_sc_s, s_sc = _short_conv_jnp(proj[s_rows, SC_COL0:GATE_COL0], bs, ts, state_sc_conv[0], e_sc_conv[0])
    o_sc = jnp.concatenate([o_sc_p, o_sc_s.astype(BF16), pad_bf], axis=0)
    x = matmul2_res(o_nsa, o_sc, e_w_out[0].astype(BF16), x)
    x = peer_layer(x, norm_w[0, 1], peer_wq[0], peer_keys[0], peer_u[0], peer_v[0])

    w_in = jnp.pad(o_w_in[0], ((0, 0), (0, _pad_cols(O_IN) - O_IN))).astype(BF16)
    proj = norm_matmul(x, norm_w[1, 0], w_in)
    odd_w = (o_m2_conv_w[0], o_m2_conv_b[0], o_m2_dt_bias[0], o_m2_a_log[0], o_m2_d[0], o_m2_norm[0])
    o_ret_p, p_ret = retention(proj, pos_p, bp, tp, CHUNK, jnp.zeros((bp, RET_HEADS, RET_DQK, RET_DV), F32),
                               o_ret_gn[0])
    o_ssd_p, p_ssm = ssd(proj, bp, tp, CHUNK, jnp.zeros((bp, M2_CONV - 1, M2_CONV_DIM), F32),
                         jnp.zeros((bp, M2_HEADS, M2_HEADDIM, M2_STATE), F32), *odd_w)
    xbc_cols = slice(O_SPLITS[4], O_SPLITS[5])
    p_m2c = proj[:n_p, xbc_cols].reshape(bp, tp, M2_CONV_DIM)[:, tp - (M2_CONV - 1):]
    xbc_s = proj[s_rows, xbc_cols].reshape(bs, ts, M2_CONV_DIM)
    s_m2c = jnp.concatenate([state_m2_conv[0], xbc_s], axis=1)[:, ts:]
    o_ret_s, o_ssd_s, s_ret, s_ssm = _odd_mixer_jnp(proj[s_rows, :O_IN].reshape(bs, ts, O_IN), pos_s, state_ret[0],
                                                    state_ssm[0], state_m2_conv[0], o_ret_gn[0], *odd_w)
    o_ret = jnp.concatenate([o_ret_p, o_ret_s.astype(BF16), pad_bf], axis=0)
    o_ssd = jnp.concatenate([o_ssd_p, o_ssd_s.astype(BF16), pad_bf], axis=0)
    x = matmul2_res(o_ret, o_ssd, o_w_out[0].astype(BF16), x)
    x = peer_layer(x, norm_w[1, 1], peer_wq[1], peer_keys[1], peer_u[1], peer_v[1])

    y = rmsnorm(x, final_norm)
    y_prompt = y[:n_p].reshape(bp, tp, dm)
    y_sample = y[s_rows].reshape(bs, ts, dm)
    return (y_prompt, y_sample, p_kv[None], p_win[None], p_sc[None], p_ret[None], p_ssm[None], p_m2c[None],
            s_kv[None], s_win[None], s_sc[None], s_ret[None], s_ssm[None], s_m2c[None])
```

```python
import functools
import math

import jax
import jax.numpy as jnp
from jax import lax
from jax.experimental import pallas as pl
from jax.experimental.pallas import tpu as pltpu

F32 = jnp.float32
BF16 = jnp.bfloat16

D_MODEL = 2048
DEPTH = 2
PAGE_SIZE = 128
NSA_HEAD_DIM = 128
NSA_HEADS = 8
NSA_KV_HEADS = 2
NSA_GROUP = 4
CMP_BLOCK = 32
CMP_STRIDE = 16
SLC_BLOCK = 64
SLC_TOPK = 16
WINDOW = 512
ROPE_THETA = 500000.0
ROPE_DIMS = 32
SC_WIDTH = 1024
SC_KSIZE = 3
RET_HEADS = 4
RET_DQK = 256
RET_DV = 256
RET_THETA = 10000.0
M2_DINNER = 1024
M2_HEADDIM = 64
M2_HEADS = 16
M2_STATE = 128
M2_GROUPS = 2
M2_CONV = 4
M2_CONV_DIM = M2_DINNER + 2 * M2_GROUPS * M2_STATE
PEER_HEADS = 8
PEER_KEYS = 128
PEER_QDIM = 256
PEER_TOPK = 16
Q_BLOCK = 128
CHUNK = 128
EPS = 1e-6

E_Q = NSA_HEADS * NSA_HEAD_DIM
E_KV = 6 * NSA_KV_HEADS * NSA_HEAD_DIM
E_G = 3 * NSA_HEADS
E_SC = 3 * SC_WIDTH
E_IN = E_Q + E_KV + E_G + E_SC
R_QK = RET_HEADS * RET_DQK
R_V = RET_HEADS * RET_DV
O_SPLITS = [R_QK, 2 * R_QK, 2 * R_QK + R_V, 2 * R_QK + 2 * R_V,
            2 * R_QK + 2 * R_V + M2_DINNER, 2 * R_QK + 2 * R_V + M2_DINNER + M2_CONV_DIM]
O_IN = O_SPLITS[-1] + M2_HEADS
SC_COL0 = E_Q + E_KV
GATE_COL0 = SC_COL0 + E_SC

LANES = 128
VMEM_LIMIT = 56 * 1024 * 1024
ROW_TILE_CAP = 1056
ROPE_ROWS_CAP = 544
NEG_INF = float("-inf")


def _round_up(n, m):
    return -(-n // m) * m


def _pick_tile(n, cap):
    best = LANES
    for t in range(LANES, cap + 1, LANES):
        if n % t == 0:
            best = t
    return best


def _pad_cols(m):
    return min((_round_up(m, t) for t in (768, 640, 512)))


def _row_tile(n, cap):
    best = None
    for t in range(16, cap + 1, 16):
        if n % t == 0:
            best = t
    assert best is not None
    return best


def _gelu_tanh(x):
    return 0.5 * x * (1.0 + jnp.tanh(math.sqrt(2.0 / math.pi) * (x + 0.044715 * (x * x * x))))


def _norm_matmul_kernel(x_ref, g_ref, w_ref, o_ref, xn_ref):
    @pl.when(pl.program_id(1) == 0)
    def _():
        x = x_ref[...]
        ms = jnp.mean(x * x, axis=-1, keepdims=True)
        xn_ref[...] = (x * lax.rsqrt(ms + EPS) * g_ref[...]).astype(BF16)

    o_ref[...] = jnp.dot(xn_ref[...], w_ref[...], preferred_element_type=F32).reshape(o_ref.shape)


def norm_matmul(x, gain, w_bf):
    n, k = x.shape
    m = w_bf.shape[1]
    tm = _row_tile(n, ROW_TILE_CAP)
    tn = _pick_tile(m, 768)
    return pl.pallas_call(
        _norm_matmul_kernel,
        grid=(n // tm, m // tn),
        in_specs=[pl.BlockSpec((tm, k), lambda i, j: (i, 0)),
                  pl.BlockSpec((1, k), lambda i, j: (0, 0)),
                  pl.BlockSpec((k, tn), lambda i, j: (0, j))],
        out_specs=pl.BlockSpec((tm, tn), lambda i, j: (i, j)),
        out_shape=jax.ShapeDtypeStruct((n, m), F32),
        scratch_shapes=[pltpu.VMEM((tm, k), BF16)],
        compiler_params=pltpu.CompilerParams(
            dimension_semantics=("parallel", "arbitrary"), vmem_limit_bytes=VMEM_LIMIT),
        name="norm_matmul",
    )(x, gain.reshape(1, k), w_bf)


def _matmul2_res_kernel(a1_ref, a2_ref, w1_ref, w2_ref, r_ref, o_ref):
    o_ref[...] = (r_ref[...] + jnp.dot(a1_ref[...], w1_ref[...], preferred_element_type=F32)
                  + jnp.dot(a2_ref[...], w2_ref[...], preferred_element_type=F32))


def matmul2_res(a1, a2, w_bf, res):
    n, k1 = a1.shape
    k2 = a2.shape[1]
    m = w_bf.shape[1]
    assert k1 == k2
    tm = _row_tile(n, ROW_TILE_CAP)
    tn = _pick_tile(m, 1024)
    return pl.pallas_call(
        _matmul2_res_kernel,
        grid=(n // tm, m // tn),
        in_specs=[pl.BlockSpec((tm, k1), lambda i, j: (i, 0)),
                  pl.BlockSpec((tm, k2), lambda i, j: (i, 0)),
                  pl.BlockSpec((k1, tn), lambda i, j: (0, j)),
                  pl.BlockSpec((k2, tn), lambda i, j: (1, j)),
                  pl.BlockSpec((tm, tn), lambda i, j: (i, j))],
        out_specs=pl.BlockSpec((tm, tn), lambda i, j: (i, j)),
        out_shape=jax.ShapeDtypeStruct((n, m), F32),
        compiler_params=pltpu.CompilerParams(
            dimension_semantics=("parallel", "parallel"), vmem_limit_bytes=VMEM_LIMIT),
        name="matmul2_res",
    )(a1, a2, w_bf, w_bf, res)


PEER_TOK = 128
PEER_HEAD_UNROLL = 4
PEER_TM = 768
PEER_TA = 8
PEER_TE = PEER_TA * PEER_KEYS


def _top_desc(work, count, with_rank=False, one_at_a_time=False):
    rows = []
    rank = jnp.full(work.shape, float(count), F32)
    row_id = lax.broadcasted_iota(jnp.int32, work.shape, 0).astype(F32)
    for r in range(count):
        m = jnp.max(work, axis=0, keepdims=True)
        rows.append(m)
        hit = work == m
        if one_at_a_time:
            hit = row_id == jnp.min(jnp.where(hit, row_id, float(work.shape[0])), axis=0, keepdims=True)
        if with_rank:
            rank = jnp.where(hit, float(r), rank)
        work = jnp.where(hit, NEG_INF, work)
    return (rows, rank) if with_rank else rows


def _peer_router_kernel(q_ref, k_ref, cnt_ref, gw_ref, r2_ref, w2_ref):
    def head(h, carry):
        q = q_ref[h]
        scores = []
        for side in range(2):
            qs = q[:, side * PEER_KEYS:(side + 1) * PEER_KEYS]
            qs = qs * lax.rsqrt(jnp.mean(qs * qs, axis=-1, keepdims=True) + EPS)
            scores.append(lax.dot_general(k_ref[h, side], qs, (((1,), (1,)), ((), ())),
                                          preferred_element_type=F32))
        s1, s2 = scores
        v1 = _top_desc(s1, PEER_TOPK + 1)
        v2, rank2 = _top_desc(s2, PEER_TOPK + 1, with_rank=True)
        v2_lo = jnp.concatenate(v2[:8], axis=0)
        v2_hi = jnp.concatenate(v2[8:16], axis=0)
        row = lax.broadcasted_iota(jnp.int32, v2_lo.shape, 0)
        blocks = [v1[0] + v2_lo, v1[0] + v2_hi, v1[1] + v2_lo]
        for a, lim in ((2, 5), (3, 4), (4, 3), (5, 2), (6, 2), (7, 2)):
            blocks.append(jnp.where(row < lim, v1[a] + v2_lo, NEG_INF))
        blocks.append(jnp.concatenate(v1[8:16], axis=0) + v2[0])
        extra = jnp.where(row == 0, v1[0] + v2[16], jnp.where(row == 1, v1[16] + v2[0], NEG_INF))
        blocks.append(extra)
        cand = jnp.concatenate(blocks, axis=0)
        tops = _top_desc(cand, PEER_TOPK + 1, one_at_a_time=True)
        z = jnp.zeros_like(tops[0])
        for r in range(PEER_TOPK):
            z = z + jnp.exp(tops[r] - tops[0])
        tau = 0.5 * (tops[PEER_TOPK - 1] + tops[PEER_TOPK])
        count = jnp.zeros_like(s1)
        for r in range(PEER_TOPK):
            count = count + jnp.where(s1 >= tau - v2[r], 1.0, 0.0)
        cnt_ref[h] = count
        gw_ref[h] = jnp.exp(s1 - v1[0]) / z
        r2_ref[h] = rank2.astype(BF16)
        w2_ref[h] = jnp.exp(s2 - v2[0]).astype(BF16)
        return carry

    lax.fori_loop(0, PEER_HEADS, head, 0, unroll=PEER_HEAD_UNROLL)


def peer_router(q_hm, keys):
    n = q_hm.shape[1]
    out = jax.ShapeDtypeStruct((PEER_HEADS, PEER_KEYS, n), F32)
    out_bf = jax.ShapeDtypeStruct((PEER_HEADS, PEER_KEYS, n), BF16)
    spec = pl.BlockSpec((PEER_HEADS, PEER_KEYS, PEER_TOK), lambda i: (0, 0, i))
    return pl.pallas_call(
        _peer_router_kernel,
        grid=(n // PEER_TOK,),
        in_specs=[pl.BlockSpec((PEER_HEADS, PEER_TOK, PEER_QDIM), lambda i: (0, i, 0)),
                  pl.BlockSpec((PEER_HEADS, 2, PEER_KEYS, PEER_QDIM // 2), lambda i: (0, 0, 0, 0))],
        out_specs=[spec, spec, spec, spec],
        out_shape=[out, out, out_bf, out_bf],
        compiler_params=pltpu.CompilerParams(dimension_semantics=("parallel",), vmem_limit_bytes=VMEM_LIMIT),
        name="peer_router",
    )(q_hm, keys)


def _peer_expert_kernel(xt_ref, u_ref, vt_ref, cnt_ref, gw_ref, r2_ref, w2_ref, o_ref, s_scr, hg_scr):
    e = pl.program_id(1)
    s_scr[...] = jnp.dot(u_ref[...], xt_ref[...], preferred_element_type=F32)
    zero = jnp.zeros((), BF16)
    for al in range(PEER_TA):
        rows = slice(al * PEER_KEYS, (al + 1) * PEER_KEYS)
        g = None
        for h in range(PEER_HEADS):
            cnt = cnt_ref[h, al:al + 1, :].astype(BF16)
            gate = gw_ref[h, al:al + 1, :].astype(BF16)
            t = jnp.where(r2_ref[h] < cnt, w2_ref[h], zero) * gate
            g = t if g is None else g + t
        hg_scr[rows, :] = _gelu_tanh(s_scr[rows, :]).astype(BF16) * g
    part = jnp.dot(vt_ref[...], hg_scr[...], preferred_element_type=F32)

    @pl.when(e == 0)
    def _():
        o_ref[...] = part

    @pl.when(e != 0)
    def _():
        o_ref[...] += part


def peer_experts(xt_bf, u_bf, vt_bf, cnt, gw, r2, w2):
    d, n = xt_bf.shape
    n_exp = u_bf.shape[0]
    tm = PEER_TM
    sel_spec = pl.BlockSpec((PEER_HEADS, PEER_TA, tm), lambda i, e: (0, e, i))
    all_spec = pl.BlockSpec((PEER_HEADS, PEER_KEYS, tm), lambda i, e: (0, 0, i))
    return pl.pallas_call(
        _peer_expert_kernel,
        grid=(n // tm, n_exp // PEER_TE),
        in_specs=[pl.BlockSpec((d, tm), lambda i, e: (0, i)),
                  pl.BlockSpec((PEER_TE, d), lambda i, e: (e, 0)),
                  pl.BlockSpec((d, PEER_TE), lambda i, e: (0, e)),
                  sel_spec, sel_spec, all_spec, all_spec],
        out_specs=pl.BlockSpec((d, tm), lambda i, e: (0, i)),
        out_shape=jax.ShapeDtypeStruct((d, n), F32),
        scratch_shapes=[pltpu.VMEM((PEER_TE, tm), F32), pltpu.VMEM((PEER_TE, tm), BF16)],
        compiler_params=pltpu.CompilerParams(
            dimension_semantics=("parallel", "arbitrary"), vmem_limit_bytes=VMEM_LIMIT),
        name="peer_experts",
    )(xt_bf, u_bf, vt_bf, cnt, gw, r2, w2)


def _norm_matmul_t_kernel(x_ref, g_ref, w_ref, o_ref, xt_ref, xn_ref):
    @pl.when(pl.program_id(1) == 0)
    def _():
        x = x_ref[...]
        ms = jnp.mean(x * x, axis=-1, keepdims=True)
        xn = x * lax.rsqrt(ms + EPS) * g_ref[...]
        xn_ref[...] = xn.astype(BF16)
        xt_ref[...] = xn.T.astype(BF16)

    o_ref[...] = jnp.dot(xn_ref[...], w_ref[...], preferred_element_type=F32).reshape(o_ref.shape)


def norm_matmul_t(x, gain, w_bf, cols):
    n, k = x.shape
    m = w_bf.shape[1]
    tm = PEER_TM
    return pl.pallas_call(
        _norm_matmul_t_kernel,
        grid=(n // tm, m // cols),
        in_specs=[pl.BlockSpec((tm, k), lambda i, j: (i, 0)),
                  pl.BlockSpec((1, k), lambda i, j: (0, 0)),
                  pl.BlockSpec((k, cols), lambda i, j: (0, j))],
        out_specs=[pl.BlockSpec((1, tm, cols), lambda i, j: (j, i, 0)),
                   pl.BlockSpec((k, tm), lambda i, j: (0, i))],
        out_shape=[jax.ShapeDtypeStruct((m // cols, n, cols), F32), jax.ShapeDtypeStruct((k, n), BF16)],
        scratch_shapes=[pltpu.VMEM((tm, k), BF16)],
        compiler_params=pltpu.CompilerParams(
            dimension_semantics=("parallel", "arbitrary"), vmem_limit_bytes=VMEM_LIMIT),
        name="norm_matmul_t",
    )(x, gain.reshape(1, k), w_bf)


def _rmsnorm_kernel(x_ref, g_ref, o_ref):
    x = x_ref[...]
    ms = jnp.mean(x * x, axis=-1, keepdims=True)
    o_ref[...] = (x * lax.rsqrt(ms + EPS) * g_ref[...]).astype(o_ref.dtype)


def rmsnorm(x, gain, dtype=F32):
    n, k = x.shape
    tm = _row_tile(n, ROW_TILE_CAP)
    return pl.pallas_call(
        _rmsnorm_kernel,
        grid=(n // tm,),
        in_specs=[pl.BlockSpec((tm, k), lambda i: (i, 0)), pl.BlockSpec((1, k), lambda i: (0, 0))],
        out_specs=pl.BlockSpec((tm, k), lambda i: (i, 0)),
        out_shape=jax.ShapeDtypeStruct((n, k), dtype),
        compiler_params=pltpu.CompilerParams(dimension_semantics=("parallel",), vmem_limit_bytes=VMEM_LIMIT),
        name="rmsnorm",
    )(x, gain.reshape(1, k))


def peer_layer(x, gain, w_q, keys, u, v):
    q_hm, xt_bf = norm_matmul_t(x, gain, w_q.astype(BF16), PEER_QDIM)
    cnt, gw, r2, w2 = peer_router(q_hm, keys)
    return x + peer_experts(xt_bf, u.astype(BF16), v.T.astype(BF16), cnt, gw, r2, w2).T


def _rope_tables(pos):
    half = ROPE_DIMS // 2
    inv = ROPE_THETA ** (-jnp.arange(half, dtype=F32) / half)
    ang = pos.astype(F32)[:, None] * inv[None, :]
    cos, sin = jnp.cos(ang), jnp.sin(ang)
    t = pos.shape[0]
    ones = jnp.ones((t, NSA_HEAD_DIM - ROPE_DIMS), F32)
    zeros = jnp.zeros((t, NSA_HEAD_DIM - ROPE_DIMS), F32)
    zh = jnp.zeros((t, half), F32)
    c = jnp.concatenate([cos, cos, ones], axis=1)
    s_lo = jnp.concatenate([-sin, zh, zeros], axis=1)
    s_hi = jnp.concatenate([zh, sin, zeros], axis=1)
    return c, s_lo, s_hi


def _rope_kernel(p_ref, c_ref, sl_ref, sh_ref, q_ref, rows_ref, win_ref, rows_bf_ref, win_bf_ref):
    c, sl, sh = c_ref[...], sl_ref[...], sh_ref[...]
    half = ROPE_DIMS // 2

    def rot(x):
        return x * c + pltpu.roll(x, LANES - half, 1) * sl + pltpu.roll(x, half, 1) * sh

    d = NSA_HEAD_DIM
    scale = d ** -0.5
    for hd in range(NSA_HEADS):
        q_ref[:, hd * d:(hd + 1) * d] = (rot(p_ref[:, hd * d:(hd + 1) * d]) * scale).astype(BF16)
    for blk in range(12):
        x = p_ref[:, E_Q + blk * d:E_Q + (blk + 1) * d]
        if (blk // 2) % 2 == 0:
            x = rot(x)
        if blk < 8:
            rows_ref[:, blk * d:(blk + 1) * d] = x
            rows_bf_ref[:, blk * d:(blk + 1) * d] = x.astype(BF16)
        else:
            win_ref[:, (blk - 8) * d:(blk - 7) * d] = x
            win_bf_ref[:, (blk - 8) * d:(blk - 7) * d] = x.astype(BF16)


def nsa_rope(proj, pos_rows):
    n = proj.shape[0]
    tm = _row_tile(n, ROPE_ROWS_CAP)
    c, sl, sh = _rope_tables(pos_rows)
    width = E_Q + E_KV
    tab = pl.BlockSpec((tm, LANES), lambda i: (i, 0))
    return pl.pallas_call(
        _rope_kernel,
        grid=(n // tm,),
        in_specs=[pl.BlockSpec((tm, width), lambda i: (i, 0)), tab, tab, tab],
        out_specs=[pl.BlockSpec((tm, E_Q), lambda i: (i, 0)),
                   pl.BlockSpec((tm, 1024), lambda i: (i, 0)),
                   pl.BlockSpec((tm, 512), lambda i: (i, 0)),
                   pl.BlockSpec((tm, 1024), lambda i: (i, 0)),
                   pl.BlockSpec((tm, 512), lambda i: (i, 0))],
        out_shape=[jax.ShapeDtypeStruct((n, E_Q), BF16),
                   jax.ShapeDtypeStruct((n, 1024), F32),
                   jax.ShapeDtypeStruct((n, 512), F32),
                   jax.ShapeDtypeStruct((n, 1024), BF16),
                   jax.ShapeDtypeStruct((n, 512), BF16)],
        compiler_params=pltpu.CompilerParams(dimension_semantics=("parallel",), vmem_limit_bytes=VMEM_LIMIT),
        name="nsa_rope",
    )(proj, c, sl, sh)


def _compress_kernel(x_ref, pe_ref, w1_ref, w2_ref, o_ref):
    n_sub = x_ref.shape[0] // CMP_STRIDE
    acc0 = jnp.zeros((n_sub, NSA_HEAD_DIM), F32)
    acc1 = jnp.zeros((n_sub, NSA_HEAD_DIM), F32)
    for s in range(CMP_STRIDE):
        xs = x_ref[pl.ds(s, n_sub, stride=CMP_STRIDE), :]
        a0 = (xs + pe_ref[0, s:s + 1, :]).astype(BF16)
        a1 = (xs + pe_ref[0, CMP_STRIDE + s:CMP_STRIDE + s + 1, :]).astype(BF16)
        acc0 = acc0 + jnp.dot(a0, w1_ref[0, s], preferred_element_type=F32)
        acc1 = acc1 + jnp.dot(a1, w1_ref[0, CMP_STRIDE + s], preferred_element_type=F32)
    pre = acc0 + pltpu.roll(acc1, n_sub - 1, 0)
    o_ref[0, 0] = jnp.dot(_gelu_tanh(pre).astype(BF16), w2_ref[0], preferred_element_type=F32).astype(BF16)


def nsa_compress(rows, n_batch, t_len, cmp_pe, cmp_w1, cmp_w2):
    n_sub = t_len // CMP_STRIDE
    d = NSA_HEAD_DIM
    w1 = cmp_w1.reshape(2, CMP_BLOCK, d, d).astype(BF16)
    return pl.pallas_call(
        _compress_kernel,
        grid=(n_batch, 4),
        in_specs=[pl.BlockSpec((t_len, d), lambda b, c: (b, c)),
                  pl.BlockSpec((1, CMP_BLOCK, d), lambda b, c: (c // 2, 0, 0)),
                  pl.BlockSpec((1, CMP_BLOCK, d, d), lambda b, c: (c // 2, 0, 0, 0)),
                  pl.BlockSpec((1, d, d), lambda b, c: (c // 2, 0, 0))],
        out_specs=pl.BlockSpec((1, 1, n_sub, d), lambda b, c: (b, c, 0, 0)),
        out_shape=jax.ShapeDtypeStruct((n_batch, 4, n_sub, d), BF16),
        compiler_params=pltpu.CompilerParams(
            dimension_semantics=("parallel", "parallel"), vmem_limit_bytes=VMEM_LIMIT),
        name="nsa_compress",
    )(rows, cmp_pe, w1, cmp_w2.astype(BF16))


NSA_TK = 512
NSA_WTILES = WINDOW // Q_BLOCK + 1


def _masked_softmax_rows(s, mask):
    s = jnp.where(mask, s, -1e30)
    p = jnp.exp(s - jnp.max(s, axis=-1, keepdims=True))
    p = jnp.where(mask, p, 0.0)
    return p / jnp.maximum(jnp.sum(p, axis=-1, keepdims=True), 1e-30)


def _nsa_prompt_kernel(q_ref, kc_ref, vc_ref, ks_ref, vs_ref, kw_ref, vw_ref, gate_ref, o_ref):
    g = pl.program_id(1)
    qi = pl.program_id(2)
    d, r, qb = NSA_HEAD_DIM, NSA_GROUP, Q_BLOCK
    n_cmp = kc_ref.shape[2]
    n_slc = ks_ref.shape[0] // SLC_BLOCK
    nt = (((1,), (1,)), ((), ()))
    q = jnp.concatenate([q_ref[:, h * d:(h + 1) * d] for h in range(r)], axis=0)
    t_col = qi * qb + lax.broadcasted_iota(jnp.int32, (qb, 1), 0)

    s = lax.dot_general(q, kc_ref[0, 0], nt, preferred_element_type=F32).reshape(r, qb, n_cmp)
    n_idx = lax.broadcasted_iota(jnp.int32, (qb, n_cmp), 1)
    c_mask = (n_idx * CMP_STRIDE + (CMP_BLOCK - 1) <= t_col)[None]
    p_c = _masked_softmax_rows(s, c_mask)
    o_cmp = jnp.dot(p_c.reshape(r * qb, n_cmp).astype(BF16), vc_ref[0, 0], preferred_element_type=F32)

    nb = LANES
    assert n_slc <= nb and qb == LANES
    p_sum = jnp.sum(p_c, axis=0)
    si = lax.broadcasted_iota(jnp.int32, (nb, n_cmp), 0) * SLC_BLOCK
    ci = lax.broadcasted_iota(jnp.int32, (nb, n_cmp), 1) * CMP_STRIDE
    overlap_t = jnp.where((ci < si + SLC_BLOCK) & (ci + CMP_BLOCK > si), 1.0, 0.0).astype(BF16)
    p_hi = p_sum.astype(BF16)
    p_lo = (p_sum - p_hi.astype(F32)).astype(BF16)
    imp_t = (lax.dot_general(overlap_t, p_hi, nt, preferred_element_type=F32)
             + lax.dot_general(overlap_t, p_lo, nt, preferred_element_type=F32))
    blk = lax.broadcasted_iota(jnp.int32, (nb, qb), 0)
    blk_f = blk.astype(F32)
    t_row = qi * qb + lax.broadcasted_iota(jnp.int32, (1, qb), 1)
    cur = t_row // SLC_BLOCK
    forced = (blk == 0) | (blk == cur) | (blk == cur - 1)
    work = jnp.where(blk * SLC_BLOCK <= t_row, jnp.where(forced, 1e6, imp_t), -1e6)
    work = jnp.where(blk < n_slc, work, NEG_INF)
    sel_t = jnp.zeros((nb, qb), F32)
    for _ in range(min(SLC_TOPK, n_slc)):
        m = jnp.max(work, axis=0, keepdims=True)
        first = jnp.min(jnp.where(work == m, blk_f, float(nb)), axis=0, keepdims=True)
        pick = blk_f == first
        sel_t = jnp.where(pick, 1.0, sel_t)
        work = jnp.where(pick, NEG_INF, work)
    sel_bf = sel_t.T.astype(BF16)

    tk = NSA_TK
    bpt = tk // SLC_BLOCK

    def slc_step(kt, carry):
        m_run, l_run, acc = carry
        start = pl.multiple_of(kt * tk, tk)
        k = ks_ref[pl.ds(start, tk), :]
        v = vs_ref[pl.ds(start, tk), :]
        s = lax.dot_general(q, k, nt, preferred_element_type=F32).reshape(r, qb, tk)
        ei = lax.broadcasted_iota(jnp.int32, (nb, tk), 0)
        ej = lax.broadcasted_iota(jnp.int32, (nb, tk), 1)
        expand = jnp.where(ei == kt * bpt + ej // SLC_BLOCK, 1.0, 0.0).astype(BF16)
        picked = jnp.dot(sel_bf, expand, preferred_element_type=F32)
        kpos = start + lax.broadcasted_iota(jnp.int32, (qb, tk), 1)
        mask = ((picked > 0.5) & (kpos <= t_col))[None]
        s = jnp.where(mask, s, -1e30)
        m_new = jnp.maximum(m_run, jnp.max(s, axis=-1, keepdims=True))
        alpha = jnp.exp(m_run - m_new)
        p = jnp.where(mask, jnp.exp(s - m_new), 0.0)
        l_new = alpha * l_run + jnp.sum(p, axis=-1, keepdims=True)
        pv = jnp.dot(p.reshape(r * qb, tk).astype(BF16), v, preferred_element_type=F32)
        acc = alpha.reshape(r * qb, 1) * acc + pv
        return m_new, l_new, acc

    init = (jnp.full((r, qb, 1), -1e30, F32), jnp.zeros((r, qb, 1), F32), jnp.zeros((r * qb, d), F32))
    _, l_fin, acc = lax.fori_loop(0, (qi * qb) // tk + 1, slc_step, init)
    o_slc = acc / jnp.maximum(l_fin.reshape(r * qb, 1), 1e-30)

    k_tiles, v_tiles, pos_tiles = [], [], []
    for j in range(NSA_WTILES):
        kt = qi - (NSA_WTILES - 1) + j
        ktc = jnp.maximum(kt, 0)
        start = pl.multiple_of(ktc * qb, qb)
        k_tiles.append(kw_ref[pl.ds(start, qb), :])
        v_tiles.append(vw_ref[pl.ds(start, qb), :])
        lane = lax.broadcasted_iota(jnp.int32, (qb, qb), 1)
        pos_tiles.append(jnp.where(kt >= 0, start + lane, -1))
    k_w = jnp.concatenate(k_tiles, axis=0)
    v_w = jnp.concatenate(v_tiles, axis=0)
    k_pos = jnp.concatenate(pos_tiles, axis=1)
    span = NSA_WTILES * qb
    s = lax.dot_general(q, k_w, nt, preferred_element_type=F32).reshape(r, qb, span)
    dpos = t_col - k_pos
    w_mask = ((dpos >= 0) & (dpos < WINDOW) & (k_pos >= 0))[None]
    p_w = _masked_softmax_rows(s, w_mask)
    o_win = jnp.dot(p_w.reshape(r * qb, span).astype(BF16), v_w, preferred_element_type=F32)

    sig = jax.nn.sigmoid(gate_ref[...])
    lane = lax.broadcasted_iota(jnp.int32, sig.shape, 1)
    for h in range(r):
        rows = slice(h * qb, (h + 1) * qb)
        out = jnp.zeros((qb, d), F32)
        for branch, o_b in enumerate((o_cmp, o_slc, o_win)):
            col = branch * NSA_HEADS + g * r + h
            gate = jnp.sum(jnp.where(lane == col, sig, 0.0), axis=-1, keepdims=True)
            out = out + gate * o_b[rows]
        o_ref[:, h * d:(h + 1) * d] = out.astype(o_ref.dtype)


def nsa_prompt(q_bf, kvc, rows_bf, win_bf, proj, n_batch, t_len):
    d, r, qb = NSA_HEAD_DIM, NSA_GROUP, Q_BLOCK
    nqb = t_len // qb
    n_sub = kvc.shape[2]
    gate_blk = GATE_COL0 // LANES
    seq = lambda col: pl.BlockSpec((t_len, d), lambda b, g, i: (b, col(g)))
    return pl.pallas_call(
        _nsa_prompt_kernel,
        grid=(n_batch, NSA_KV_HEADS, nqb),
        in_specs=[pl.BlockSpec((qb, r * d), lambda b, g, i: (b * nqb + i, g)),
                  pl.BlockSpec((1, 1, n_sub, d), lambda b, g, i: (b, g, 0, 0)),
                  pl.BlockSpec((1, 1, n_sub, d), lambda b, g, i: (b, 2 + g, 0, 0)),
                  seq(lambda g: 4 + g), seq(lambda g: 6 + g),
                  seq(lambda g: g), seq(lambda g: 2 + g),
                  pl.BlockSpec((qb, LANES), lambda b, g, i: (b * nqb + i, gate_blk))],
        out_specs=pl.BlockSpec((qb, r * d), lambda b, g, i: (b * nqb + i, g)),
        out_shape=jax.ShapeDtypeStruct((n_batch * t_len, E_Q), BF16),
        compiler_params=pltpu.CompilerParams(
            dimension_semantics=("parallel", "parallel", "arbitrary"), vmem_limit_bytes=VMEM_LIMIT),
        name="nsa_prompt",
    )(q_bf, kvc, kvc, rows_bf, rows_bf, win_bf, win_bf, proj)


SAMPLE_PAGES = 16
NEW_ROWS_BLK = 16


def _page_specs(kind):
    def spec(i):
        return pl.BlockSpec((1, PAGE_SIZE, 1, NSA_KV_HEADS, NSA_HEAD_DIM),
                            lambda s, c, pt: (pt[s, c * SAMPLE_PAGES + i], 0, kind, 0, 0))
    return [spec(i) for i in range(SAMPLE_PAGES)]


def _sample_compress_kernel(pt_ref, *refs):
    n_in = 2 * SAMPLE_PAGES
    pages = refs[:n_in]
    pe_ref, w1_ref, a0_ref, a1_ref = refs[n_in:]
    d = NSA_HEAD_DIM
    per_page = PAGE_SIZE // CMP_STRIDE
    rows = SAMPLE_PAGES * per_page
    gs = NSA_KV_HEADS * CMP_STRIDE
    for kv in range(2):
        by_offset = [jnp.swapaxes(pages[kv * SAMPLE_PAGES + p].reshape(NSA_KV_HEADS * PAGE_SIZE, d)[...]
                                  .reshape(per_page, gs, d), 0, 1) for p in range(SAMPLE_PAGES)]
        acc0 = jnp.zeros((NSA_KV_HEADS * rows, d), F32)
        acc1 = jnp.zeros((NSA_KV_HEADS * rows, d), F32)
        for s in range(CMP_STRIDE):
            xs = jnp.concatenate([by_offset[p][NSA_KV_HEADS * s + g]
                                  for g in range(NSA_KV_HEADS) for p in range(SAMPLE_PAGES)],
                                 axis=0)
            a0 = (xs + pe_ref[kv, s:s + 1, :]).astype(BF16)
            a1 = (xs + pe_ref[kv, CMP_STRIDE + s:CMP_STRIDE + s + 1, :]).astype(BF16)
            acc0 = acc0 + jnp.dot(a0, w1_ref[kv, s], preferred_element_type=F32)
            acc1 = acc1 + jnp.dot(a1, w1_ref[kv, CMP_STRIDE + s], preferred_element_type=F32)
        for g in range(NSA_KV_HEADS):
            a0_ref[0, kv * NSA_KV_HEADS + g] = acc0[g * rows:(g + 1) * rows]
            a1_ref[0, kv * NSA_KV_HEADS + g] = acc1[g * rows:(g + 1) * rows]


def nsa_sample_compress(cache, page_table, cmp_pe, cmp_w1):
    bs, n_pages = page_table.shape
    d = NSA_HEAD_DIM
    per_page = PAGE_SIZE // CMP_STRIDE
    n_sub = n_pages * per_page
    rows = SAMPLE_PAGES * per_page
    w1 = cmp_w1.reshape(2, CMP_BLOCK, d, d).astype(BF16)
    out = jax.ShapeDtypeStruct((bs, 4, n_sub, d), F32)
    ospec = pl.BlockSpec((1, 4, rows, d), lambda s, c, pt: (s, 0, c, 0))
    page_specs = _page_specs(0) + _page_specs(1)
    return pl.pallas_call(
        _sample_compress_kernel,
        grid_spec=pltpu.PrefetchScalarGridSpec(
            num_scalar_prefetch=1,
            grid=(bs, n_pages // SAMPLE_PAGES),
            in_specs=page_specs + [
                pl.BlockSpec((2, CMP_BLOCK, d), lambda s, c, pt: (0, 0, 0)),
                pl.BlockSpec((2, CMP_BLOCK, d, d), lambda s, c, pt: (0, 0, 0, 0))],
            out_specs=[ospec, ospec]),
        out_shape=[out, out],
        compiler_params=pltpu.CompilerParams(
            dimension_semantics=("parallel", "arbitrary"), vmem_limit_bytes=VMEM_LIMIT),
        name="nsa_sample_compress",
    )(page_table, *([cache] * (2 * SAMPLE_PAGES)), cmp_pe, w1)


def _sample_select_kernel(a0_ref, a1_ref, w2_ref, q_ref, ocmp_ref, sel_ref, *, past_len, t_len):
    d, r = NSA_HEAD_DIM, NSA_GROUP
    n_sub = a0_ref.shape[2]
    n_cmp = n_sub - 1
    n_slc = -(-(past_len + t_len) // SLC_BLOCK)
    lanes = sel_ref.shape[3]
    nt = (((1,), (1,)), ((), ()))
    rq = r * t_len
    for g in range(NSA_KV_HEADS):
        kc = jnp.dot(_gelu_tanh(a0_ref[0, g] + pltpu.roll(a1_ref[0, g], n_sub - 1, 0)).astype(BF16), w2_ref[0],
                     preferred_element_type=F32).astype(BF16)
        vc = jnp.dot(_gelu_tanh(a0_ref[0, 2 + g] + pltpu.roll(a1_ref[0, 2 + g], n_sub - 1, 0)).astype(BF16),
                     w2_ref[1], preferred_element_type=F32).astype(BF16)
        q = q_ref[0, g]
        s = lax.dot_general(q, kc, nt, preferred_element_type=F32)
        pos = past_len + lax.broadcasted_iota(jnp.int32, (rq, 1), 0) % t_len
        n_idx = lax.broadcasted_iota(jnp.int32, (rq, n_sub), 1)
        p_c = _masked_softmax_rows(s, (n_idx * CMP_STRIDE + (CMP_BLOCK - 1) <= pos) & (n_idx < n_cmp))
        ocmp_ref[0, g] = jnp.dot(p_c.astype(BF16), vc, preferred_element_type=F32)
        ri = lax.broadcasted_iota(jnp.int32, (8, rq), 0)
        cj = lax.broadcasted_iota(jnp.int32, (8, rq), 1)
        head_sum = jnp.where(cj % t_len == ri, 1.0, 0.0).astype(BF16)
        p_hi = p_c.astype(BF16)
        p_lo = (p_c - p_hi.astype(F32)).astype(BF16)
        p_sum = (jnp.dot(head_sum, p_hi, preferred_element_type=F32)
                 + jnp.dot(head_sum, p_lo, preferred_element_type=F32))
        ci = lax.broadcasted_iota(jnp.int32, (n_sub, lanes), 0)
        mi = lax.broadcasted_iota(jnp.int32, (n_sub, lanes), 1)
        overlap = jnp.where((ci * CMP_STRIDE < mi * SLC_BLOCK + SLC_BLOCK)
                            & (ci * CMP_STRIDE + CMP_BLOCK > mi * SLC_BLOCK) & (ci < n_cmp), 1.0, 0.0).astype(BF16)
        s_hi = p_sum.astype(BF16)
        s_lo = (p_sum - s_hi.astype(F32)).astype(BF16)
        imp = (jnp.dot(s_hi, overlap, preferred_element_type=F32)
               + jnp.dot(s_lo, overlap, preferred_element_type=F32))
        blk = lax.broadcasted_iota(jnp.int32, (8, lanes), 1)
        blk_f = blk.astype(F32)
        tpos = past_len + lax.broadcasted_iota(jnp.int32, (8, 1), 0) % t_len
        cur = tpos // SLC_BLOCK
        forced = (blk == 0) | (blk == cur) | (blk == cur - 1)
        work = jnp.where(blk * SLC_BLOCK <= tpos, jnp.where(forced, 1e6, imp), -1e6)
        work = jnp.where(blk < n_slc, work, NEG_INF)
        sel = jnp.zeros((8, lanes), F32)
        for _ in range(min(SLC_TOPK, n_slc)):
            m = jnp.max(work, axis=-1, keepdims=True)
            first = jnp.min(jnp.where(work == m, blk_f, float(lanes)), axis=-1, keepdims=True)
            pick = blk_f == first
            sel = jnp.where(pick, 1.0, sel)
            work = jnp.where(pick, NEG_INF, work)
        sel_ref[0, g] = sel


def nsa_sample_select(a0, a1, cmp_w2, q_s, past_len, t_len):
    bs, _, n_sub, d = a0.shape
    n_slc = -(-(past_len + t_len) // SLC_BLOCK)
    lanes = _round_up(n_slc, LANES)
    rq = NSA_GROUP * t_len
    aspec = pl.BlockSpec((1, 4, n_sub, d), lambda s: (s, 0, 0, 0))
    return pl.pallas_call(
        functools.partial(_sample_select_kernel, past_len=past_len, t_len=t_len),
        grid=(bs,),
        in_specs=[aspec, aspec, pl.BlockSpec((2, d, d), lambda s: (0, 0, 0)),
                  pl.BlockSpec((1, NSA_KV_HEADS, rq, d), lambda s: (s, 0, 0, 0))],
        out_specs=[pl.BlockSpec((1, NSA_KV_HEADS, rq, d), lambda s: (s, 0, 0, 0)),
                   pl.BlockSpec((1, NSA_KV_HEADS, 8, lanes), lambda s: (s, 0, 0, 0))],
        out_shape=[jax.ShapeDtypeStruct((bs, NSA_KV_HEADS, rq, d), F32),
                   jax.ShapeDtypeStruct((bs, NSA_KV_HEADS, 8, lanes), F32)],
        compiler_params=pltpu.CompilerParams(dimension_semantics=("parallel",), vmem_limit_bytes=VMEM_LIMIT),
        name="nsa_sample_select",
    )(a0, a1, cmp_w2.astype(BF16), q_s)


def _sample_attend_kernel(pt_ref, *refs, past_len, t_len, row0):
    k_pages = refs[:SAMPLE_PAGES]
    v_pages = refs[SAMPLE_PAGES:2 * SAMPLE_PAGES]
    (q_ref, sel_ref, ocmp_ref, newrows_ref, kwin_ref, vwin_ref, newwin_ref, gate_ref, o_ref,
     m_scr, l_scr, acc_scr) = refs[2 * SAMPLE_PAGES:]
    s_idx = pl.program_id(0)
    c = pl.program_id(1)
    d, r = NSA_HEAD_DIM, NSA_GROUP
    rq = r * t_len
    lanes = sel_ref.shape[3]
    tk = SAMPLE_PAGES * PAGE_SIZE
    nt = (((1,), (1,)), ((), ()))

    @pl.when(c == 0)
    def _():
        m_scr[...] = jnp.full(m_scr.shape, -1e30, F32)
        l_scr[...] = jnp.zeros(l_scr.shape, F32)
        acc_scr[...] = jnp.zeros(acc_scr.shape, F32)

    row = lax.broadcasted_iota(jnp.int32, (rq, 1), 0)
    t_row = row % t_len
    pos = past_len + t_row
    ti = lax.broadcasted_iota(jnp.int32, (rq, 8), 1)
    tok_expand = jnp.where(ti == t_row, 1.0, 0.0).astype(BF16)

    def online(g, s, mask, v):
        s = jnp.where(mask, s, -1e30)
        m_old = m_scr[g]
        m_new = jnp.maximum(m_old, jnp.max(s, axis=-1, keepdims=True))
        alpha = jnp.exp(m_old - m_new)
        p = jnp.where(mask, jnp.exp(s - m_new), 0.0)
        l_scr[g] = alpha * l_scr[g] + jnp.sum(p, axis=-1, keepdims=True)
        acc_scr[g] = alpha * acc_scr[g] + jnp.dot(p.astype(BF16), v, preferred_element_type=F32)
        m_scr[g] = m_new

    sel16 = []
    for g in range(NSA_KV_HEADS):
        sel16.append(jnp.dot(tok_expand, sel_ref[0, g].astype(BF16), preferred_element_type=F32))
        k = jnp.concatenate([pg.reshape(NSA_KV_HEADS * PAGE_SIZE, d)[pl.ds(g, PAGE_SIZE, stride=NSA_KV_HEADS), :]
                             for pg in k_pages], axis=0)
        v = jnp.concatenate([pg.reshape(NSA_KV_HEADS * PAGE_SIZE, d)[pl.ds(g, PAGE_SIZE, stride=NSA_KV_HEADS), :]
                             for pg in v_pages], axis=0)
        q = q_ref[0, g]
        s = lax.dot_general(q, k.astype(BF16), nt, preferred_element_type=F32)
        ei = lax.broadcasted_iota(jnp.int32, (lanes, tk), 0)
        ej = lax.broadcasted_iota(jnp.int32, (lanes, tk), 1)
        expand = jnp.where(ei == c * (tk // SLC_BLOCK) + ej // SLC_BLOCK, 1.0, 0.0).astype(BF16)
        picked = jnp.dot(sel16[g].astype(BF16), expand, preferred_element_type=F32)
        kpos = c * tk + lax.broadcasted_iota(jnp.int32, (rq, tk), 1)
        online(g, s, (picked > 0.5) & (kpos <= pos), v.astype(BF16))

    @pl.when(c == pl.num_programs(1) - 1)
    def _():
        mine = (row0 // t_len + s_idx) % (NEW_ROWS_BLK // t_len)
        j = lax.broadcasted_iota(jnp.int32, (rq, NEW_ROWS_BLK), 1)
        own = j // t_len == mine
        new_pos = past_len + j % t_len
        sig = jax.nn.sigmoid(gate_ref[...])
        gsel = jnp.where(j == mine * t_len + t_row, 1.0, 0.0).astype(BF16)
        s_hi = sig.astype(BF16)
        s_lo = (sig - s_hi.astype(F32)).astype(BF16)
        sig_rows = (jnp.dot(gsel, s_hi, preferred_element_type=F32)
                    + jnp.dot(gsel, s_lo, preferred_element_type=F32))
        lane = lax.broadcasted_iota(jnp.int32, sig_rows.shape, 1)
        lane_l = lax.broadcasted_iota(jnp.int32, (rq, lanes), 1)
        for g in range(NSA_KV_HEADS):
            q = q_ref[0, g]
            kn = newrows_ref[:, (4 + g) * d:(5 + g) * d].astype(BF16)
            vn = newrows_ref[:, (6 + g) * d:(7 + g) * d].astype(BF16)
            s = lax.dot_general(q, kn, nt, preferred_element_type=F32)
            last_picked = jnp.sum(jnp.where(lane_l == past_len // SLC_BLOCK, sel16[g], 0.0), axis=-1, keepdims=True)
            online(g, s, own & (new_pos <= pos) & (last_picked > 0.5), vn)
            o_slc = acc_scr[g] / jnp.maximum(l_scr[g], 1e-30)
            w_len = kwin_ref.shape[1]
            win_rows = pl.ds(g, w_len, stride=NSA_KV_HEADS)
            kw = kwin_ref.reshape(NSA_KV_HEADS * w_len, d)[win_rows, :].astype(BF16)
            vw = vwin_ref.reshape(NSA_KV_HEADS * w_len, d)[win_rows, :].astype(BF16)
            s1 = lax.dot_general(q, kw, nt, preferred_element_type=F32)
            kp1 = past_len - w_len + lax.broadcasted_iota(jnp.int32, (rq, w_len), 1)
            d1 = pos - kp1
            m1 = (d1 >= 0) & (d1 < WINDOW) & (kp1 >= 0)
            knw = newwin_ref[:, g * d:(g + 1) * d].astype(BF16)
            vnw = newwin_ref[:, (2 + g) * d:(3 + g) * d].astype(BF16)
            s2 = lax.dot_general(q, knw, nt, preferred_element_type=F32)
            d2 = pos - new_pos
            m2 = own & (d2 >= 0) & (d2 < WINDOW)
            s1 = jnp.where(m1, s1, -1e30)
            s2 = jnp.where(m2, s2, -1e30)
            mx = jnp.maximum(jnp.max(s1, axis=-1, keepdims=True), jnp.max(s2, axis=-1, keepdims=True))
            p1 = jnp.where(m1, jnp.exp(s1 - mx), 0.0)
            p2 = jnp.where(m2, jnp.exp(s2 - mx), 0.0)
            den = jnp.sum(p1, axis=-1, keepdims=True) + jnp.sum(p2, axis=-1, keepdims=True)
            o_win = (jnp.dot(p1.astype(BF16), vw, preferred_element_type=F32)
                     + jnp.dot(p2.astype(BF16), vnw, preferred_element_type=F32)) / jnp.maximum(den, 1e-30)
            out = jnp.zeros((rq, d), F32)
            for branch, o_b in enumerate((ocmp_ref[0, g], o_slc, o_win)):
                col = branch * NSA_HEADS + g * r + row // t_len
                gate = jnp.sum(jnp.where(lane == col, sig_rows, 0.0), axis=-1, keepdims=True)
                out = out + gate * o_b
            o_ref[0, g] = out


def nsa_sample_attend(cache, page_table, q_s, sel, o_cmp, rows, win, win_cache, proj, past_len, t_len, row0):
    bs, n_pages = page_table.shape
    d = NSA_HEAD_DIM
    rq = NSA_GROUP * t_len
    lanes = sel.shape[3]
    w_len = win_cache.shape[1]
    assert row0 % t_len == 0 and NEW_ROWS_BLK % t_len == 0 and past_len % SLC_BLOCK == 0
    blk = lambda s: (row0 + s * t_len) // NEW_ROWS_BLK
    per_seq = lambda shape: pl.BlockSpec((1,) + shape, lambda s, c, pt: (s, 0, 0, 0))
    return pl.pallas_call(
        functools.partial(_sample_attend_kernel, past_len=past_len, t_len=t_len, row0=row0),
        grid_spec=pltpu.PrefetchScalarGridSpec(
            num_scalar_prefetch=1,
            grid=(bs, n_pages // SAMPLE_PAGES),
            in_specs=_page_specs(2) + _page_specs(3) + [
                per_seq((NSA_KV_HEADS, rq, d)), per_seq((NSA_KV_HEADS, 8, lanes)), per_seq((NSA_KV_HEADS, rq, d)),
                pl.BlockSpec((NEW_ROWS_BLK, 8 * d), lambda s, c, pt: (blk(s), 0)),
                pl.BlockSpec((1, w_len, 1, NSA_KV_HEADS, d), lambda s, c, pt: (s, 0, 0, 0, 0)),
                pl.BlockSpec((1, w_len, 1, NSA_KV_HEADS, d), lambda s, c, pt: (s, 0, 1, 0, 0)),
                pl.BlockSpec((NEW_ROWS_BLK, 4 * d), lambda s, c, pt: (blk(s), 0)),
                pl.BlockSpec((NEW_ROWS_BLK, LANES), lambda s, c, pt: (blk(s), GATE_COL0 // LANES))],
            out_specs=per_seq((NSA_KV_HEADS, rq, d)),
            scratch_shapes=[pltpu.VMEM((NSA_KV_HEADS, rq, 1), F32), pltpu.VMEM((NSA_KV_HEADS, rq, 1), F32),
                            pltpu.VMEM((NSA_KV_HEADS, rq, d), F32)]),
        out_shape=jax.ShapeDtypeStruct((bs, NSA_KV_HEADS, rq, d), F32),
        compiler_params=pltpu.CompilerParams(
            dimension_semantics=("parallel", "arbitrary"), vmem_limit_bytes=VMEM_LIMIT),
        name="nsa_sample_attend",
    )(page_table, *([cache] * (2 * SAMPLE_PAGES)), q_s, sel, o_cmp, rows, win_cache, win_cache, win, proj)


SC_ROWS = 512
SC_COLS = 512


def _short_conv_kernel(b_ref, c_ref, h_ref, w_ref, buf_ref, o_ref, tail_ref, prev_scr):
    @pl.when(pl.program_id(2) == 0)
    def _():
        prev_scr[...] = buf_ref[0]

    rows = c_ref.shape[0]
    u = c_ref[...] * h_ref[...]
    prev = prev_scr[...]
    row8 = lax.broadcasted_iota(jnp.int32, prev.shape, 0)
    conv = u * w_ref[SC_KSIZE - 1:SC_KSIZE, :]
    for k in range(1, SC_KSIZE):
        rolled = pltpu.roll(u, k, 0)
        top = jnp.where(row8 < k, pltpu.roll(prev, k, 0), rolled[0:8])
        shifted = top if rows == 8 else jnp.concatenate([top, rolled[8:]], axis=0)
        conv = conv + shifted * w_ref[SC_KSIZE - 1 - k:SC_KSIZE - k, :]
    prev_scr[...] = u[rows - 8:rows]
    o_ref[...] = (b_ref[...] * conv).astype(o_ref.dtype)
    tail_ref[0] = u[rows - 8:rows]


def short_conv(proj, n_batch, t_len, rows_per_step, conv_buf, sc_w):
    nr = t_len // rows_per_step
    nh = SC_WIDTH // SC_COLS
    blk0 = SC_COL0 // SC_COLS
    assert SC_COL0 % SC_COLS == 0
    buf8 = jnp.pad(conv_buf, ((0, 0), (8 - (SC_KSIZE - 1), 0), (0, 0)))
    col = lambda part: pl.BlockSpec((rows_per_step, SC_COLS), lambda b, j, i: (b * nr + i, blk0 + part * nh + j))
    return pl.pallas_call(
        _short_conv_kernel,
        grid=(n_batch, nh, nr),
        in_specs=[col(0), col(1), col(2), pl.BlockSpec((SC_KSIZE, SC_COLS), lambda b, j, i: (0, j)),
                  pl.BlockSpec((1, 8, SC_COLS), lambda b, j, i: (b, 0, j))],
        out_specs=[pl.BlockSpec((rows_per_step, SC_COLS), lambda b, j, i: (b * nr + i, j)),
                   pl.BlockSpec((1, 8, SC_COLS), lambda b, j, i: (b, 0, j))],
        out_shape=[jax.ShapeDtypeStruct((n_batch * t_len, SC_WIDTH), BF16),
                   jax.ShapeDtypeStruct((n_batch, 8, SC_WIDTH), F32)],
        scratch_shapes=[pltpu.VMEM((8, SC_COLS), F32)],
        compiler_params=pltpu.CompilerParams(
            dimension_semantics=("parallel", "parallel", "arbitrary"), vmem_limit_bytes=VMEM_LIMIT),
        name="short_conv",
    )(proj, proj, proj, sc_w, buf8)


def _ret_tables(pos, chunk, valid):
    half = RET_DQK // 2
    inv = RET_THETA ** (-jnp.arange(half, dtype=F32) / half)
    ang = pos.astype(F32)[:, None] * inv[None, :]
    log_g = jnp.log1p(-(2.0 ** (-5.0 - jnp.arange(RET_HEADS, dtype=F32))))
    i = jnp.arange(chunk, dtype=F32)
    diff = i[:, None] - i[None, :]
    intra = jnp.where(diff >= 0, jnp.exp(jnp.maximum(diff, 0.0)[None] * log_g[:, None, None]), 0.0)
    q_dec = jnp.exp((i[None, :] + 1.0) * log_g[:, None])[..., None]
    k_dec = jnp.exp((valid - 1.0 - i)[None, :] * log_g[:, None])[..., None]
    c_dec = jnp.exp(valid * log_g)[:, None, None]
    return jnp.cos(ang), jnp.sin(ang), intra, q_dec, k_dec, c_dec


def _retention_kernel(q_ref, k_ref, v_ref, g_ref, cos_ref, sin_ref, intra_ref, qd_ref, kd_ref, cd_ref, gn_ref,
                      s0_ref, o_ref, s_out_ref, s_scr):
    c_idx = pl.program_id(1)

    @pl.when(c_idx == 0)
    def _():
        s_scr[...] = s0_ref[0]

    cos, sin = cos_ref[...], sin_ref[...]
    half = RET_DQK // 2
    nt = (((1,), (1,)), ((), ()))
    tn = (((0,), (0,)), ((), ()))

    def rot(x):
        x1, x2 = x[:, :half], x[:, half:]
        return jnp.concatenate([x1 * cos - x2 * sin, x1 * sin + x2 * cos], axis=-1)

    for h in range(RET_HEADS):
        cols = slice(h * RET_DQK, (h + 1) * RET_DQK)
        qr = (rot(q_ref[:, cols]) * (RET_DQK ** -0.5)).astype(BF16)
        kr = rot(k_ref[:, cols])
        v = v_ref[:, cols].astype(BF16)
        att = lax.dot_general(qr, kr.astype(BF16), nt, preferred_element_type=F32) * intra_ref[h]
        s_old = s_scr[h]
        o = (jnp.dot(att.astype(BF16), v, preferred_element_type=F32)
             + jnp.dot(qr, s_old.astype(BF16), preferred_element_type=F32) * qd_ref[h])
        s_scr[h] = s_old * cd_ref[h] + lax.dot_general((kr * kd_ref[h]).astype(BF16), v, tn,
                                                       preferred_element_type=F32)
        mu = jnp.mean(o, axis=-1, keepdims=True)
        dev = o - mu
        var = jnp.mean(dev * dev, axis=-1, keepdims=True)
        gate = g_ref[:, cols]
        on = dev * lax.rsqrt(var + EPS) * gn_ref[:, cols] * (gate * jax.nn.sigmoid(gate))
        o_ref[:, cols] = on.astype(o_ref.dtype)
    s_out_ref[0] = s_scr[...]


def retention(proj, pos, n_batch, t_len, chunk, state0, gn_gain, valid=None):
    nc = t_len // chunk
    cos, sin, intra, q_dec, k_dec, c_dec = _ret_tables(pos, chunk, chunk if valid is None else valid)
    half = RET_DQK // 2
    col = lambda j: pl.BlockSpec((chunk, R_QK), lambda b, c: (b * nc + c, j))
    tab = pl.BlockSpec((chunk, half), lambda b, c: (c, 0))
    full = lambda a: pl.BlockSpec(a.shape, lambda b, c: (0,) * a.ndim)
    st = pl.BlockSpec((1, RET_HEADS, RET_DQK, RET_DV), lambda b, c: (b, 0, 0, 0))
    return pl.pallas_call(
        _retention_kernel,
        grid=(n_batch, nc),
        in_specs=[col(0), col(1), col(2), col(3), tab, tab, full(intra), full(q_dec), full(k_dec), full(c_dec),
                  pl.BlockSpec((1, R_V), lambda b, c: (0, 0)), st],
        out_specs=[pl.BlockSpec((chunk, R_V), lambda b, c: (b * nc + c, 0)), st],
        out_shape=[jax.ShapeDtypeStruct((n_batch * t_len, R_V), BF16),
                   jax.ShapeDtypeStruct((n_batch, RET_HEADS, RET_DQK, RET_DV), F32)],
        scratch_shapes=[pltpu.VMEM((RET_HEADS, RET_DQK, RET_DV), F32)],
        compiler_params=pltpu.CompilerParams(
            dimension_semantics=("parallel", "arbitrary"), vmem_limit_bytes=VMEM_LIMIT),
        name="retention",
    )(proj, proj, proj, proj, cos, sin, intra, q_dec, k_dec, c_dec, gn_gain.reshape(1, R_V), state0)


SSD_COL0 = 2 * R_QK + 2 * R_V
HEADS_PER_GROUP = M2_HEADS // M2_GROUPS


def _split3(x):
    a = x.astype(BF16)
    r = x - a.astype(F32)
    b = r.astype(BF16)
    c = (r - b.astype(F32)).astype(BF16)
    return a, b, c


def _exact_dot(mat_bf, x):
    out = None
    for piece in _split3(x):
        t = jnp.dot(mat_bf, piece, preferred_element_type=F32)
        out = t if out is None else out + t
    return out


def _exact_dot_r(x, mat_bf):
    out = None
    for piece in _split3(x):
        t = jnp.dot(piece, mat_bf, preferred_element_type=F32)
        out = t if out is None else out + t
    return out


def _ssd_kernel(z_ref, xa_ref, xb_ref, xc_ref, dt_ref, cw_ref, cb_ref, dtb_ref, aneg_ref, dskip_ref, norm_ref,
                buf0_ref, s0_ref, o_ref, s_out_ref, s_scr, prev_scr, *, valid):
    c_idx = pl.program_id(1)
    chunk = z_ref.shape[0]
    nt = (((1,), (1,)), ((), ()))
    tn = (((0,), (0,)), ((), ()))

    @pl.when(c_idx == 0)
    def _():
        s_scr[...] = s0_ref[0]
        prev_scr[...] = buf0_ref[0]

    x = jnp.concatenate([xa_ref[...], xb_ref[...], xc_ref[...]], axis=1)
    prev = prev_scr[...]
    row8 = lax.broadcasted_iota(jnp.int32, prev.shape, 0)
    conv = x * cw_ref[M2_CONV - 1:M2_CONV, :]
    for k in range(1, M2_CONV):
        rolled = pltpu.roll(x, k, 0)
        top = jnp.where(row8 < k, pltpu.roll(prev, k, 0), rolled[0:8])
        shifted = top if chunk == 8 else jnp.concatenate([top, rolled[8:]], axis=0)
        conv = conv + shifted * cw_ref[M2_CONV - 1 - k:M2_CONV - k, :]
    prev_scr[...] = x[chunk - 8:chunk]
    conv = conv + cb_ref[...]
    xbc = conv * jax.nn.sigmoid(conv)
    xs = xbc[:, :M2_DINNER]

    dt_raw = dt_ref[...] + dtb_ref[...]
    dt = jnp.where(dt_raw > 20.0, dt_raw, jnp.log1p(jnp.exp(jnp.minimum(dt_raw, 20.0))))
    if valid < chunk:
        dt = jnp.where(lax.broadcasted_iota(jnp.int32, dt.shape, 0) < valid, dt, 0.0)
    a = dt * aneg_ref[...]
    ri = lax.broadcasted_iota(jnp.int32, (chunk, chunk), 0)
    ci = lax.broadcasted_iota(jnp.int32, (chunk, chunk), 1)
    tri = ri >= ci
    cum = _exact_dot(jnp.where(tri, 1.0, 0.0).astype(BF16), a)
    cum_t = cum.T
    cum_last = cum[chunk - 1:chunk, :]
    hi = lax.broadcasted_iota(jnp.int32, (LANES, M2_DINNER), 0)
    li = lax.broadcasted_iota(jnp.int32, (LANES, M2_DINNER), 1)
    expand = jnp.where(hi == li // M2_HEADDIM, 1.0, 0.0).astype(BF16)
    dt_x = _exact_dot_r(dt, expand)
    cum_x = _exact_dot_r(cum, expand)
    last_x = _exact_dot_r(cum_last, expand)
    xdt = xs * dt_x
    x_dec = (xdt * jnp.exp(last_x - cum_x)).astype(BF16)
    xdt_bf = xdt.astype(BF16)
    e_cum_x = jnp.exp(cum_x)
    e_last_x = jnp.exp(last_x)

    y_parts = []
    for gi in range(M2_GROUPS):
        b_g = xbc[:, M2_DINNER + gi * M2_STATE:M2_DINNER + (gi + 1) * M2_STATE].astype(BF16)
        c_g = xbc[:, M2_DINNER + (M2_GROUPS + gi) * M2_STATE:M2_DINNER + (M2_GROUPS + gi + 1) * M2_STATE].astype(BF16)
        cb = lax.dot_general(c_g, b_g, nt, preferred_element_type=F32)
        gcols = slice(gi * HEADS_PER_GROUP * M2_HEADDIM, (gi + 1) * HEADS_PER_GROUP * M2_HEADDIM)
        s_old = s_scr[:, gcols]
        y_state = jnp.dot(c_g, s_old.astype(BF16), preferred_element_type=F32) * e_cum_x[:, gcols]
        s_scr[:, gcols] = s_old * e_last_x[:, gcols] + lax.dot_general(b_g, x_dec[:, gcols], tn,
                                                                        preferred_element_type=F32)
        pair_lane = lax.broadcasted_iota(jnp.int32, (chunk, LANES), 1)
        intra = []
        for pr in range(HEADS_PER_GROUP // 2):
            outs = []
            for sub in range(2):
                h = gi * HEADS_PER_GROUP + pr * 2 + sub
                seg = cum[:, h:h + 1] - cum_t[h:h + 1, :]
                l_mat = jnp.where(tri, jnp.exp(jnp.where(tri, seg, 0.0)), 0.0)
                lanes = slice(gi * HEADS_PER_GROUP * M2_HEADDIM + pr * LANES,
                              gi * HEADS_PER_GROUP * M2_HEADDIM + (pr + 1) * LANES)
                outs.append(jnp.dot((cb * l_mat).astype(BF16), xdt_bf[:, lanes], preferred_element_type=F32))
            intra.append(jnp.where(pair_lane < M2_HEADDIM, outs[0], outs[1]))
        y_parts.append(jnp.concatenate(intra, axis=1) + y_state)
    y = jnp.concatenate(y_parts, axis=1) + dskip_ref[...] * xs
    z = z_ref[...]
    y = y * (z * jax.nn.sigmoid(z))
    gw = M2_DINNER // M2_GROUPS
    outs = []
    for gi in range(M2_GROUPS):
        yg = y[:, gi * gw:(gi + 1) * gw]
        outs.append(yg * lax.rsqrt(jnp.mean(yg * yg, axis=-1, keepdims=True) + EPS))
    o_ref[...] = (jnp.concatenate(outs, axis=1) * norm_ref[...]).astype(o_ref.dtype)
    s_out_ref[0] = s_scr[...]


def ssd(proj, n_batch, t_len, chunk, conv_buf, state0, conv_w, conv_b, dt_bias, a_log, d_skip, m2_norm, valid=None):
    nc = t_len // chunk
    z_blk = SSD_COL0 // M2_DINNER
    xw = M2_CONV_DIM // 3
    xbc_blk = (SSD_COL0 + M2_DINNER) // xw
    assert (SSD_COL0 + M2_DINNER) % xw == 0 and xw % LANES == 0
    dt_blk = (SSD_COL0 + M2_DINNER + M2_CONV_DIM) // LANES
    pad = lambda v: jnp.pad(v.reshape(1, -1), ((0, 0), (0, LANES - v.shape[-1])))
    buf8 = jnp.pad(conv_buf, ((0, 0), (8 - (M2_CONV - 1), 0), (0, 0)))
    st_t = state0.transpose(0, 3, 1, 2).reshape(n_batch, M2_STATE, M2_DINNER)
    row = lambda a: pl.BlockSpec(a.shape, lambda b, c: (0, 0))
    cw = conv_w
    cb = conv_b.reshape(1, -1)
    dtb, aneg = pad(dt_bias), pad(-jnp.exp(a_log))
    dsk = jnp.repeat(d_skip, M2_HEADDIM).reshape(1, -1)
    nrm = m2_norm.reshape(1, -1)
    st = pl.BlockSpec((1, M2_STATE, M2_DINNER), lambda b, c: (b, 0, 0))
    out, s_fin = pl.pallas_call(
        functools.partial(_ssd_kernel, valid=chunk if valid is None else valid),
        grid=(n_batch, nc),
        in_specs=[pl.BlockSpec((chunk, M2_DINNER), lambda b, c: (b * nc + c, z_blk)),
                  pl.BlockSpec((chunk, xw), lambda b, c: (b * nc + c, xbc_blk)),
                  pl.BlockSpec((chunk, xw), lambda b, c: (b * nc + c, xbc_blk + 1)),
                  pl.BlockSpec((chunk, xw), lambda b, c: (b * nc + c, xbc_blk + 2)),
                  pl.BlockSpec((chunk, LANES), lambda b, c: (b * nc + c, dt_blk)),
                  row(cw), row(cb), row(dtb), row(aneg), row(dsk), row(nrm),
                  pl.BlockSpec((1, 8, M2_CONV_DIM), lambda b, c: (b, 0, 0)), st],
        out_specs=[pl.BlockSpec((chunk, M2_DINNER), lambda b, c: (b * nc + c, 0)), st],
        out_shape=[jax.ShapeDtypeStruct((n_batch * t_len, M2_DINNER), BF16),
                   jax.ShapeDtypeStruct((n_batch, M2_STATE, M2_DINNER), F32)],
        scratch_shapes=[pltpu.VMEM((M2_STATE, M2_DINNER), F32), pltpu.VMEM((8, M2_CONV_DIM), F32)],
        compiler_params=pltpu.CompilerParams(
            dimension_semantics=("parallel", "arbitrary"), vmem_limit_bytes=VMEM_LIMIT),
        name="ssd",
    )(proj, proj, proj, proj, proj, cw, cb, dtb, aneg, dsk, nrm, buf8, st_t)
    s_fin = s_fin.reshape(n_batch, M2_STATE, M2_HEADS, M2_HEADDIM).transpose(0, 2, 3, 1)
    return out, s_fin


SEQ_ROWS = 8


def _pad_seq_rows(a, n_seq, t_len):
    assert t_len <= SEQ_ROWS
    a = a.reshape(n_seq, t_len, a.shape[-1])
    return jnp.pad(a, ((0, 0), (0, SEQ_ROWS - t_len), (0, 0))).reshape(n_seq * SEQ_ROWS, a.shape[-1])


def _unpad_seq_rows(a, n_seq, t_len):
    return a.reshape(n_seq, SEQ_ROWS, a.shape[-1])[:, :t_len].reshape(n_seq * t_len, a.shape[-1])

def kernel(x_prompt, x_sample, cache_nsa_kv, cache_nsa_win, state_sc_conv, state_ret, state_ssm, state_m2_conv,
           page_table, norm_w, final_norm, e_w_in, e_w_out, e_cmp_pe, e_cmp_w1, e_cmp_w2, e_sc_conv, o_w_in,
           o_w_out, o_ret_gn, o_m2_conv_w, o_m2_conv_b, o_m2_dt_bias, o_m2_a_log, o_m2_d, o_m2_norm, peer_wq,
           peer_keys, peer_u, peer_v):
    bp, tp, dm = x_prompt.shape
    bs, ts, _ = x_sample.shape
    n_p, n_s = bp * tp, bs * ts
    g, r, d = NSA_KV_HEADS, NSA_GROUP, NSA_HEAD_DIM
    past_len = page_table.shape[1] * PAGE_SIZE
    pos_p = jnp.arange(tp, dtype=jnp.int32)
    pos_s = past_len + jnp.arange(ts, dtype=jnp.int32)
    n_real = n_p + n_s
    n_pad = _round_up(n_real, PEER_TM) - n_real
    s_rows = slice(n_p, n_real)
    pos_rows = jnp.concatenate([jnp.tile(pos_p, bp), jnp.tile(pos_s, bs), jnp.zeros((n_pad,), jnp.int32)])
    xp = x_prompt.reshape(n_p, dm)
    x = jnp.concatenate([xp, x_sample.reshape(n_s, dm), xp[:n_pad]], axis=0)
    pad_bf = jnp.zeros((n_pad, E_Q), BF16)

    w0 = e_w_in[0]
    w_in = jnp.concatenate([w0[:, :SC_COL0], w0[:, SC_COL0 + E_G:], w0[:, SC_COL0:SC_COL0 + E_G],
                            jnp.zeros((dm, _pad_cols(E_IN) - E_IN), F32)], axis=1).astype(BF16)
    proj = norm_matmul(x, norm_w[0, 0], w_in)
    q_bf, rows, win, rows_bf, win_bf = nsa_rope(proj, pos_rows)
    kvc = nsa_compress(rows, bp, tp, e_cmp_pe[0], e_cmp_w1[0], e_cmp_w2[0])
    o_nsa_p = nsa_prompt(q_bf, kvc, rows_bf, win_bf, proj, bp, tp)
    p_kv = rows[:n_p].reshape(bp, tp, 4, g, d)
    s_kv = rows[s_rows].reshape(bs, ts, 4, g, d)
    p_win = win[:n_p].reshape(bp, tp, 2, g, d)[:, tp - min(WINDOW, tp):]
    win_s = win[s_rows].reshape(bs, ts, 2, g, d)
    s_win = jnp.concatenate([cache_nsa_win[0], win_s], axis=1)[:, ts:]
    cache = cache_nsa_kv[0]
    q_s = q_bf[s_rows].reshape(bs, ts, g, r, d).transpose(0, 2, 3, 1, 4).reshape(bs, g, r * ts, d)
    a0, a1 = nsa_sample_compress(cache, page_table, e_cmp_pe[0], e_cmp_w1[0])
    o_cmp_s, sel_s = nsa_sample_select(a0, a1, e_cmp_w2[0], q_s, past_len, ts)
    o_nsa_s = nsa_sample_attend(cache, page_table, q_s, sel_s, o_cmp_s, rows, win,
                                cache_nsa_win[0], proj, past_len, ts, n_p)
    o_nsa_s = o_nsa_s.reshape(bs, g, r, ts, d).transpose(0, 3, 1, 2, 4).reshape(n_s, E_Q)
    o_nsa = jnp.concatenate([o_nsa_p, o_nsa_s.astype(BF16), pad_bf], axis=0)
    o_sc_p, tail_p = short_conv(proj, bp, tp, SC_ROWS, jnp.zeros((bp, SC_KSIZE - 1, SC_WIDTH), F32), e_sc_conv[0])
    p_sc = tail_p[:, 8 - (SC_KSIZE - 1):]
    o_sc_s, tail_s = short_conv(_pad_seq_rows(proj[s_rows], bs, ts), bs, SEQ_ROWS, SEQ_ROWS, state_sc_conv[0],
                                e_sc_conv[0])
    s_sc = jnp.concatenate([state_sc_conv[0], tail_s[:, :ts]], axis=1)[:, ts:]
    o_sc = jnp.concatenate([_unpad_seq_rows(o_sc_s, bs, ts), pad_bf], axis=0)
    o_sc = jnp.concatenate([o_sc_p, o_sc], axis=0)
    x = matmul2_res(o_nsa, o_sc, e_w_out[0].astype(BF16), x)
    x = peer_layer(x, norm_w[0, 1], peer_wq[0], peer_keys[0], peer_u[0], peer_v[0])

    w_in = jnp.pad(o_w_in[0], ((0, 0), (0, _pad_cols(O_IN) - O_IN))).astype(BF16)
    proj = norm_matmul(x, norm_w[1, 0], w_in)
    odd_w = (o_m2_conv_w[0], o_m2_conv_b[0], o_m2_dt_bias[0], o_m2_a_log[0], o_m2_d[0], o_m2_norm[0])
    o_ret_p, p_ret = retention(proj, pos_p, bp, tp, CHUNK, jnp.zeros((bp, RET_HEADS, RET_DQK, RET_DV), F32),
                               o_ret_gn[0])
    o_ssd_p, p_ssm = ssd(proj, bp, tp, CHUNK, jnp.zeros((bp, M2_CONV - 1, M2_CONV_DIM), F32),
                         jnp.zeros((bp, M2_HEADS, M2_HEADDIM, M2_STATE), F32), *odd_w)
    xbc_cols = slice(O_SPLITS[4], O_SPLITS[5])
    p_m2c = proj[:n_p, xbc_cols].reshape(bp, tp, M2_CONV_DIM)[:, tp - (M2_CONV - 1):]
    xbc_s = proj[s_rows, xbc_cols].reshape(bs, ts, M2_CONV_DIM)
    s_m2c = jnp.concatenate([state_m2_conv[0], xbc_s], axis=1)[:, ts:]
    proj_s = _pad_seq_rows(proj[s_rows], bs, ts)
    pos_s8 = past_len + jnp.arange(SEQ_ROWS, dtype=jnp.int32)
    o_ret_s, s_ret = retention(proj_s, pos_s8, bs, SEQ_ROWS, SEQ_ROWS, state_ret[0], o_ret_gn[0], valid=ts)
    o_ssd_s, s_ssm = ssd(proj_s, bs, SEQ_ROWS, SEQ_ROWS, state_m2_conv[0], state_ssm[0], *odd_w, valid=ts)
    o_ret = jnp.concatenate([o_ret_p, _unpad_seq_rows(o_ret_s, bs, ts), pad_bf], axis=0)
    o_ssd = jnp.concatenate([o_ssd_p, _unpad_seq_rows(o_ssd_s, bs, ts), pad_bf], axis=0)
    x = matmul2_res(o_ret, o_ssd, o_w_out[0].astype(BF16), x)
    x = peer_layer(x, norm_w[1, 1], peer_wq[1], peer_keys[1], peer_u[1], peer_v[1])

    y = rmsnorm(x, final_norm)
    y_prompt = y[:n_p].reshape(bp, tp, dm)
    y_sample = y[s_rows].reshape(bs, ts, dm)
    return (y_prompt, y_sample, p_kv[None], p_win[None], p_sc[None], p_ret[None], p_ssm[None], p_m2c[None],
            s_kv[None], s_win[None], s_sc[None], s_ret[None], s_ssm[None], s_m2c[None])
```

```python
import functools
import math

import jax
import jax.numpy as jnp
from jax import lax
from jax.experimental import pallas as pl
from jax.experimental.pallas import tpu as pltpu

F32 = jnp.float32
BF16 = jnp.bfloat16

D_MODEL = 2048
DEPTH = 2
PAGE_SIZE = 128
NSA_HEAD_DIM = 128
NSA_HEADS = 8
NSA_KV_HEADS = 2
NSA_GROUP = 4
CMP_BLOCK = 32
CMP_STRIDE = 16
SLC_BLOCK = 64
SLC_TOPK = 16
WINDOW = 512
ROPE_THETA = 500000.0
ROPE_DIMS = 32
SC_WIDTH = 1024
SC_KSIZE = 3
RET_HEADS = 4
RET_DQK = 256
RET_DV = 256
RET_THETA = 10000.0
M2_DINNER = 1024
M2_HEADDIM = 64
M2_HEADS = 16
M2_STATE = 128
M2_GROUPS = 2
M2_CONV = 4
M2_CONV_DIM = M2_DINNER + 2 * M2_GROUPS * M2_STATE
PEER_HEADS = 8
PEER_KEYS = 128
PEER_QDIM = 256
PEER_TOPK = 16
Q_BLOCK = 128
CHUNK = 128
EPS = 1e-6

E_Q = NSA_HEADS * NSA_HEAD_DIM
E_KV = 6 * NSA_KV_HEADS * NSA_HEAD_DIM
E_G = 3 * NSA_HEADS
E_SC = 3 * SC_WIDTH
E_IN = E_Q + E_KV + E_G + E_SC
R_QK = RET_HEADS * RET_DQK
R_V = RET_HEADS * RET_DV
O_SPLITS = [R_QK, 2 * R_QK, 2 * R_QK + R_V, 2 * R_QK + 2 * R_V,
            2 * R_QK + 2 * R_V + M2_DINNER, 2 * R_QK + 2 * R_V + M2_DINNER + M2_CONV_DIM]
O_IN = O_SPLITS[-1] + M2_HEADS
SC_COL0 = E_Q + E_KV
GATE_COL0 = SC_COL0 + E_SC

LANES = 128
VMEM_LIMIT = 56 * 1024 * 1024
ROW_TILE_CAP = 1056
ROPE_ROWS_CAP = 544
NEG_INF = float("-inf")


def _round_up(n, m):
    return -(-n // m) * m


def _pick_tile(n, cap):
    best = LANES
    for t in range(LANES, cap + 1, LANES):
        if n % t == 0:
            best = t
    return best


def _pad_cols(m):
    return min((_round_up(m, t) for t in (768, 640, 512)))


def _row_tile(n, cap):
    best = None
    for t in range(16, cap + 1, 16):
        if n % t == 0:
            best = t
    assert best is not None
    return best


def _gelu_tanh(x):
    return 0.5 * x * (1.0 + jnp.tanh(math.sqrt(2.0 / math.pi) * (x + 0.044715 * (x * x * x))))


RESID_ROWS = 384


def _norm_matmul_kernel(*refs, has_delta):
    if has_delta:
        x_ref, dt_ref, g_ref, w_ref, o_ref, xsum_ref, xn_ref = refs
    else:
        x_ref, g_ref, w_ref, o_ref, xn_ref = refs

    @pl.when(pl.program_id(1) == 0)
    def _():
        x = x_ref[...]
        if has_delta:
            x = x + dt_ref[...].T
            xsum_ref[...] = x
        ms = jnp.mean(x * x, axis=-1, keepdims=True)
        xn_ref[...] = (x * lax.rsqrt(ms + EPS) * g_ref[...]).astype(BF16)

    o_ref[...] = jnp.dot(xn_ref[...], w_ref[...], preferred_element_type=F32)


def norm_matmul(x, gain, w_bf, delta_t=None):
    n, k = x.shape
    m = w_bf.shape[1]
    has_delta = delta_t is not None
    tm = _pick_tile(n, RESID_ROWS) if has_delta else _row_tile(n, ROW_TILE_CAP)
    tn = _pick_tile(m, 768)
    row = pl.BlockSpec((tm, k), lambda i, j: (i, 0))
    in_specs = [row] + ([pl.BlockSpec((k, tm), lambda i, j: (0, i))] if has_delta else []) + [
        pl.BlockSpec((1, k), lambda i, j: (0, 0)), pl.BlockSpec((k, tn), lambda i, j: (0, j))]
    out_specs = [pl.BlockSpec((tm, tn), lambda i, j: (i, j))] + ([row] if has_delta else [])
    out_shape = [jax.ShapeDtypeStruct((n, m), F32)] + ([jax.ShapeDtypeStruct((n, k), F32)] if has_delta else [])
    args = (x,) + ((delta_t,) if has_delta else ()) + (gain.reshape(1, k), w_bf)
    out = pl.pallas_call(
        functools.partial(_norm_matmul_kernel, has_delta=has_delta),
        grid=(n // tm, m // tn),
        in_specs=in_specs,
        out_specs=out_specs,
        out_shape=out_shape,
        scratch_shapes=[pltpu.VMEM((tm, k), BF16)],
        compiler_params=pltpu.CompilerParams(
            dimension_semantics=("parallel", "arbitrary"), vmem_limit_bytes=VMEM_LIMIT),
        name="norm_matmul",
    )(*args)
    return tuple(out) if has_delta else out[0]


def _matmul2_res_kernel(a1_ref, a2_ref, w1_ref, w2_ref, r_ref, o_ref):
    o_ref[...] = (r_ref[...] + jnp.dot(a1_ref[...], w1_ref[...], preferred_element_type=F32)
                  + jnp.dot(a2_ref[...], w2_ref[...], preferred_element_type=F32))


def matmul2_res(a1, a2, w_bf, res):
    n, k1 = a1.shape
    k2 = a2.shape[1]
    m = w_bf.shape[1]
    assert k1 == k2
    tm = _row_tile(n, ROW_TILE_CAP)
    tn = _pick_tile(m, 1024)
    return pl.pallas_call(
        _matmul2_res_kernel,
        grid=(n // tm, m // tn),
        in_specs=[pl.BlockSpec((tm, k1), lambda i, j: (i, 0)),
                  pl.BlockSpec((tm, k2), lambda i, j: (i, 0)),
                  pl.BlockSpec((k1, tn), lambda i, j: (0, j)),
                  pl.BlockSpec((k2, tn), lambda i, j: (1, j)),
                  pl.BlockSpec((tm, tn), lambda i, j: (i, j))],
        out_specs=pl.BlockSpec((tm, tn), lambda i, j: (i, j)),
        out_shape=jax.ShapeDtypeStruct((n, m), F32),
        compiler_params=pltpu.CompilerParams(
            dimension_semantics=("parallel", "parallel"), vmem_limit_bytes=VMEM_LIMIT),
        name="matmul2_res",
    )(a1, a2, w_bf, w_bf, res)


PEER_TOK = 128
PEER_HEAD_UNROLL = 4
PEER_TM = 768
PEER_TA = 8
PEER_TE = PEER_TA * PEER_KEYS


def _top_desc(work, count, with_rank=False, one_at_a_time=False):
    rows = []
    rank = jnp.full(work.shape, float(count), F32)
    row_id = lax.broadcasted_iota(jnp.int32, work.shape, 0).astype(F32)
    for r in range(count):
        m = jnp.max(work, axis=0, keepdims=True)
        rows.append(m)
        hit = work == m
        if one_at_a_time:
            hit = row_id == jnp.min(jnp.where(hit, row_id, float(work.shape[0])), axis=0, keepdims=True)
        if with_rank:
            rank = jnp.where(hit, float(r), rank)
        work = jnp.where(hit, NEG_INF, work)
    return (rows, rank) if with_rank else rows


def _peer_router_kernel(q_ref, k_ref, cnt_ref, gw_ref, r2_ref, w2_ref):
    def head(h, carry):
        q = q_ref[h]
        scores = []
        for side in range(2):
            qs = q[:, side * PEER_KEYS:(side + 1) * PEER_KEYS]
            qs = qs * lax.rsqrt(jnp.mean(qs * qs, axis=-1, keepdims=True) + EPS)
            scores.append(lax.dot_general(k_ref[h, side], qs, (((1,), (1,)), ((), ())),
                                          preferred_element_type=F32))
        s1, s2 = scores
        v1 = _top_desc(s1, PEER_TOPK + 1)
        v2, rank2 = _top_desc(s2, PEER_TOPK + 1, with_rank=True)
        v2_lo = jnp.concatenate(v2[:8], axis=0)
        v2_hi = jnp.concatenate(v2[8:16], axis=0)
        row = lax.broadcasted_iota(jnp.int32, v2_lo.shape, 0)
        blocks = [v1[0] + v2_lo, v1[0] + v2_hi, v1[1] + v2_lo]
        for a, lim in ((2, 5), (3, 4), (4, 3), (5, 2), (6, 2), (7, 2)):
            blocks.append(jnp.where(row < lim, v1[a] + v2_lo, NEG_INF))
        blocks.append(jnp.concatenate(v1[8:16], axis=0) + v2[0])
        extra = jnp.where(row == 0, v1[0] + v2[16], jnp.where(row == 1, v1[16] + v2[0], NEG_INF))
        blocks.append(extra)
        cand = jnp.concatenate(blocks, axis=0)
        tops = _top_desc(cand, PEER_TOPK + 1, one_at_a_time=True)
        z = jnp.zeros_like(tops[0])
        for r in range(PEER_TOPK):
            z = z + jnp.exp(tops[r] - tops[0])
        tau = 0.5 * (tops[PEER_TOPK - 1] + tops[PEER_TOPK])
        count = jnp.zeros_like(s1)
        for r in range(PEER_TOPK):
            count = count + jnp.where(s1 >= tau - v2[r], 1.0, 0.0)
        cnt_ref[h] = count
        gw_ref[h] = jnp.exp(s1 - v1[0]) / z
        r2_ref[h] = rank2.astype(BF16)
        w2_ref[h] = jnp.exp(s2 - v2[0]).astype(BF16)
        return carry

    lax.fori_loop(0, PEER_HEADS, head, 0, unroll=PEER_HEAD_UNROLL)


def peer_router(q_hm, keys):
    n = q_hm.shape[1]
    out = jax.ShapeDtypeStruct((PEER_HEADS, PEER_KEYS, n), F32)
    out_bf = jax.ShapeDtypeStruct((PEER_HEADS, PEER_KEYS, n), BF16)
    spec = pl.BlockSpec((PEER_HEADS, PEER_KEYS, PEER_TOK), lambda i: (0, 0, i))
    return pl.pallas_call(
        _peer_router_kernel,
        grid=(n // PEER_TOK,),
        in_specs=[pl.BlockSpec((PEER_HEADS, PEER_TOK, PEER_QDIM), lambda i: (0, i, 0)),
                  pl.BlockSpec((PEER_HEADS, 2, PEER_KEYS, PEER_QDIM // 2), lambda i: (0, 0, 0, 0))],
        out_specs=[spec, spec, spec, spec],
        out_shape=[out, out, out_bf, out_bf],
        compiler_params=pltpu.CompilerParams(dimension_semantics=("parallel",), vmem_limit_bytes=VMEM_LIMIT),
        name="peer_router",
    )(q_hm, keys)


def _peer_expert_kernel(xt_ref, u_ref, vt_ref, cnt_ref, gw_ref, r2_ref, w2_ref, o_ref, s_scr, hg_scr):
    e = pl.program_id(1)
    s_scr[...] = jnp.dot(u_ref[...], xt_ref[...], preferred_element_type=F32)
    zero = jnp.zeros((), BF16)
    for al in range(PEER_TA):
        rows = slice(al * PEER_KEYS, (al + 1) * PEER_KEYS)
        g = None
        for h in range(PEER_HEADS):
            cnt = cnt_ref[h, al:al + 1, :].astype(BF16)
            gate = gw_ref[h, al:al + 1, :].astype(BF16)
            t = jnp.where(r2_ref[h] < cnt, w2_ref[h], zero) * gate
            g = t if g is None else g + t
        hg_scr[rows, :] = _gelu_tanh(s_scr[rows, :]).astype(BF16) * g
    part = jnp.dot(vt_ref[...], hg_scr[...], preferred_element_type=F32)

    @pl.when(e == 0)
    def _():
        o_ref[...] = part

    @pl.when(e != 0)
    def _():
        o_ref[...] += part


def peer_experts(xt_bf, u_bf, vt_bf, layer, cnt, gw, r2, w2):
    d, n = xt_bf.shape
    n_exp = u_bf.shape[1]
    tm = PEER_TM
    sel_spec = pl.BlockSpec((PEER_HEADS, PEER_TA, tm), lambda i, e: (0, e, i))
    all_spec = pl.BlockSpec((PEER_HEADS, PEER_KEYS, tm), lambda i, e: (0, 0, i))
    return pl.pallas_call(
        _peer_expert_kernel,
        grid=(n // tm, n_exp // PEER_TE),
        in_specs=[pl.BlockSpec((d, tm), lambda i, e: (0, i)),
                  pl.BlockSpec((None, PEER_TE, d), lambda i, e: (layer, e, 0)),
                  pl.BlockSpec((None, d, PEER_TE), lambda i, e: (layer, 0, e)),
                  sel_spec, sel_spec, all_spec, all_spec],
        out_specs=pl.BlockSpec((d, tm), lambda i, e: (0, i)),
        out_shape=jax.ShapeDtypeStruct((d, n), F32),
        scratch_shapes=[pltpu.VMEM((PEER_TE, tm), F32), pltpu.VMEM((PEER_TE, tm), BF16)],
        compiler_params=pltpu.CompilerParams(
            dimension_semantics=("parallel", "arbitrary"), vmem_limit_bytes=VMEM_LIMIT),
        name="peer_experts",
    )(xt_bf, u_bf, vt_bf, cnt, gw, r2, w2)


def _norm_matmul_t_kernel(x_ref, g_ref, w_ref, o_ref, xt_ref, xn_ref):
    @pl.when(pl.program_id(1) == 0)
    def _():
        x = x_ref[...]
        ms = jnp.mean(x * x, axis=-1, keepdims=True)
        xn = x * lax.rsqrt(ms + EPS) * g_ref[...]
        xn_ref[...] = xn.astype(BF16)
        xt_ref[...] = xn.T.astype(BF16)

    o_ref[...] = jnp.dot(xn_ref[...], w_ref[...], preferred_element_type=F32).reshape(o_ref.shape)


def norm_matmul_t(x, gain, w_bf, cols):
    n, k = x.shape
    m = w_bf.shape[1]
    tm = PEER_TM
    return pl.pallas_call(
        _norm_matmul_t_kernel,
        grid=(n // tm, m // cols),
        in_specs=[pl.BlockSpec((tm, k), lambda i, j: (i, 0)),
                  pl.BlockSpec((1, k), lambda i, j: (0, 0)),
                  pl.BlockSpec((k, cols), lambda i, j: (0, j))],
        out_specs=[pl.BlockSpec((1, tm, cols), lambda i, j: (j, i, 0)),
                   pl.BlockSpec((k, tm), lambda i, j: (0, i))],
        out_shape=[jax.ShapeDtypeStruct((m // cols, n, cols), F32), jax.ShapeDtypeStruct((k, n), BF16)],
        scratch_shapes=[pltpu.VMEM((tm, k), BF16)],
        compiler_params=pltpu.CompilerParams(
            dimension_semantics=("parallel", "arbitrary"), vmem_limit_bytes=VMEM_LIMIT),
        name="norm_matmul_t",
    )(x, gain.reshape(1, k), w_bf)


def _resid_rmsnorm_kernel(x_ref, dt_ref, g_ref, o_ref):
    x = x_ref[...] + dt_ref[...].T
    ms = jnp.mean(x * x, axis=-1, keepdims=True)
    o_ref[...] = x * lax.rsqrt(ms + EPS) * g_ref[...]


def resid_rmsnorm(x, delta_t, gain):
    n, k = x.shape
    tm = _pick_tile(n, 2 * RESID_ROWS)
    return pl.pallas_call(
        _resid_rmsnorm_kernel,
        grid=(n // tm,),
        in_specs=[pl.BlockSpec((tm, k), lambda i: (i, 0)), pl.BlockSpec((k, tm), lambda i: (0, i)),
                  pl.BlockSpec((1, k), lambda i: (0, 0))],
        out_specs=pl.BlockSpec((tm, k), lambda i: (i, 0)),
        out_shape=jax.ShapeDtypeStruct((n, k), F32),
        compiler_params=pltpu.CompilerParams(dimension_semantics=("parallel",), vmem_limit_bytes=VMEM_LIMIT),
        name="resid_rmsnorm",
    )(x, delta_t, gain.reshape(1, k))


def peer_layer(x, gain, w_q, keys, u_bf, vt_bf, layer):
    q_hm, xt_bf = norm_matmul_t(x, gain, w_q.astype(BF16), PEER_QDIM)
    cnt, gw, r2, w2 = peer_router(q_hm, keys)
    return peer_experts(xt_bf, u_bf, vt_bf, layer, cnt, gw, r2, w2)


def _rope_tables(pos):
    half = ROPE_DIMS // 2
    inv = ROPE_THETA ** (-jnp.arange(half, dtype=F32) / half)
    ang = pos.astype(F32)[:, None] * inv[None, :]
    cos, sin = jnp.cos(ang), jnp.sin(ang)
    t = pos.shape[0]
    ones = jnp.ones((t, NSA_HEAD_DIM - ROPE_DIMS), F32)
    zeros = jnp.zeros((t, NSA_HEAD_DIM - ROPE_DIMS), F32)
    zh = jnp.zeros((t, half), F32)
    c = jnp.concatenate([cos, cos, ones], axis=1)
    s_lo = jnp.concatenate([-sin, zh, zeros], axis=1)
    s_hi = jnp.concatenate([zh, sin, zeros], axis=1)
    return c, s_lo, s_hi


def _rope_kernel(p_ref, c_ref, sl_ref, sh_ref, q_ref, rows_ref, win_ref, rows_bf_ref, win_bf_ref):
    c, sl, sh = c_ref[...], sl_ref[...], sh_ref[...]
    half = ROPE_DIMS // 2

    def rot(x):
        return x * c + pltpu.roll(x, LANES - half, 1) * sl + pltpu.roll(x, half, 1) * sh

    d = NSA_HEAD_DIM
    scale = d ** -0.5
    for hd in range(NSA_HEADS):
        q_ref[:, hd * d:(hd + 1) * d] = (rot(p_ref[:, hd * d:(hd + 1) * d]) * scale).astype(BF16)
    for blk in range(12):
        x = p_ref[:, E_Q + blk * d:E_Q + (blk + 1) * d]
        if (blk // 2) % 2 == 0:
            x = rot(x)
        if blk < 8:
            rows_ref[:, blk * d:(blk + 1) * d] = x
            rows_bf_ref[:, blk * d:(blk + 1) * d] = x.astype(BF16)
        else:
            win_ref[:, (blk - 8) * d:(blk - 7) * d] = x
            win_bf_ref[:, (blk - 8) * d:(blk - 7) * d] = x.astype(BF16)


def nsa_rope(proj, pos_rows):
    n = proj.shape[0]
    tm = _row_tile(n, ROPE_ROWS_CAP)
    c, sl, sh = _rope_tables(pos_rows)
    width = E_Q + E_KV
    tab = pl.BlockSpec((tm, LANES), lambda i: (i, 0))
    return pl.pallas_call(
        _rope_kernel,
        grid=(n // tm,),
        in_specs=[pl.BlockSpec((tm, width), lambda i: (i, 0)), tab, tab, tab],
        out_specs=[pl.BlockSpec((tm, E_Q), lambda i: (i, 0)),
                   pl.BlockSpec((tm, 1024), lambda i: (i, 0)),
                   pl.BlockSpec((tm, 512), lambda i: (i, 0)),
                   pl.BlockSpec((tm, 1024), lambda i: (i, 0)),
                   pl.BlockSpec((tm, 512), lambda i: (i, 0))],
        out_shape=[jax.ShapeDtypeStruct((n, E_Q), BF16),
                   jax.ShapeDtypeStruct((n, 1024), F32),
                   jax.ShapeDtypeStruct((n, 512), F32),
                   jax.ShapeDtypeStruct((n, 1024), BF16),
                   jax.ShapeDtypeStruct((n, 512), BF16)],
        compiler_params=pltpu.CompilerParams(dimension_semantics=("parallel",), vmem_limit_bytes=VMEM_LIMIT),
        name="nsa_rope",
    )(proj, c, sl, sh)


def _compress_kernel(x_ref, pe_ref, w1_ref, w2_ref, o_ref):
    n_sub = x_ref.shape[0] // CMP_STRIDE
    acc0 = jnp.zeros((n_sub, NSA_HEAD_DIM), F32)
    acc1 = jnp.zeros((n_sub, NSA_HEAD_DIM), F32)
    for s in range(CMP_STRIDE):
        xs = x_ref[pl.ds(s, n_sub, stride=CMP_STRIDE), :]
        a0 = (xs + pe_ref[0, s:s + 1, :]).astype(BF16)
        a1 = (xs + pe_ref[0, CMP_STRIDE + s:CMP_STRIDE + s + 1, :]).astype(BF16)
        acc0 = acc0 + jnp.dot(a0, w1_ref[0, s], preferred_element_type=F32)
        acc1 = acc1 + jnp.dot(a1, w1_ref[0, CMP_STRIDE + s], preferred_element_type=F32)
    pre = acc0 + pltpu.roll(acc1, n_sub - 1, 0)
    o_ref[0, 0] = jnp.dot(_gelu_tanh(pre).astype(BF16), w2_ref[0], preferred_element_type=F32).astype(BF16)


def nsa_compress(rows, n_batch, t_len, cmp_pe, cmp_w1, cmp_w2):
    n_sub = t_len // CMP_STRIDE
    d = NSA_HEAD_DIM
    w1 = cmp_w1.reshape(2, CMP_BLOCK, d, d).astype(BF16)
    return pl.pallas_call(
        _compress_kernel,
        grid=(n_batch, 4),
        in_specs=[pl.BlockSpec((t_len, d), lambda b, c: (b, c)),
                  pl.BlockSpec((1, CMP_BLOCK, d), lambda b, c: (c // 2, 0, 0)),
                  pl.BlockSpec((1, CMP_BLOCK, d, d), lambda b, c: (c // 2, 0, 0, 0)),
                  pl.BlockSpec((1, d, d), lambda b, c: (c // 2, 0, 0))],
        out_specs=pl.BlockSpec((1, 1, n_sub, d), lambda b, c: (b, c, 0, 0)),
        out_shape=jax.ShapeDtypeStruct((n_batch, 4, n_sub, d), BF16),
        compiler_params=pltpu.CompilerParams(
            dimension_semantics=("parallel", "parallel"), vmem_limit_bytes=VMEM_LIMIT),
        name="nsa_compress",
    )(rows, cmp_pe, w1, cmp_w2.astype(BF16))


NSA_TK = 512
NSA_WTILES = WINDOW // Q_BLOCK + 1


def _masked_softmax_rows(s, mask):
    s = jnp.where(mask, s, -1e30)
    p = jnp.exp(s - jnp.max(s, axis=-1, keepdims=True))
    p = jnp.where(mask, p, 0.0)
    return p / jnp.maximum(jnp.sum(p, axis=-1, keepdims=True), 1e-30)


def _nsa_prompt_kernel(q_ref, kc_ref, vc_ref, ks_ref, vs_ref, kw_ref, vw_ref, gate_ref, o_ref):
    g = pl.program_id(1)
    qi = pl.program_id(2)
    d, r, qb = NSA_HEAD_DIM, NSA_GROUP, Q_BLOCK
    n_cmp = kc_ref.shape[2]
    n_slc = ks_ref.shape[0] // SLC_BLOCK
    nt = (((1,), (1,)), ((), ()))
    q = jnp.concatenate([q_ref[:, h * d:(h + 1) * d] for h in range(r)], axis=0)
    t_col = qi * qb + lax.broadcasted_iota(jnp.int32, (qb, 1), 0)

    s = lax.dot_general(q, kc_ref[0, 0], nt, preferred_element_type=F32).reshape(r, qb, n_cmp)
    n_idx = lax.broadcasted_iota(jnp.int32, (qb, n_cmp), 1)
    c_mask = (n_idx * CMP_STRIDE + (CMP_BLOCK - 1) <= t_col)[None]
    p_c = _masked_softmax_rows(s, c_mask)
    o_cmp = jnp.dot(p_c.reshape(r * qb, n_cmp).astype(BF16), vc_ref[0, 0], preferred_element_type=F32)

    nb = LANES
    assert n_slc <= nb and qb == LANES
    p_sum = jnp.sum(p_c, axis=0)
    si = lax.broadcasted_iota(jnp.int32, (nb, n_cmp), 0) * SLC_BLOCK
    ci = lax.broadcasted_iota(jnp.int32, (nb, n_cmp), 1) * CMP_STRIDE
    overlap_t = jnp.where((ci < si + SLC_BLOCK) & (ci + CMP_BLOCK > si), 1.0, 0.0).astype(BF16)
    p_hi = p_sum.astype(BF16)
    p_lo = (p_sum - p_hi.astype(F32)).astype(BF16)
    imp_t = (lax.dot_general(overlap_t, p_hi, nt, preferred_element_type=F32)
             + lax.dot_general(overlap_t, p_lo, nt, preferred_element_type=F32))
    blk = lax.broadcasted_iota(jnp.int32, (nb, qb), 0)
    blk_f = blk.astype(F32)
    t_row = qi * qb + lax.broadcasted_iota(jnp.int32, (1, qb), 1)
    cur = t_row // SLC_BLOCK
    forced = (blk == 0) | (blk == cur) | (blk == cur - 1)
    work = jnp.where(blk * SLC_BLOCK <= t_row, jnp.where(forced, 1e6, imp_t), -1e6)
    work = jnp.where(blk < n_slc, work, NEG_INF)
    sel_t = jnp.zeros((nb, qb), F32)
    for _ in range(min(SLC_TOPK, n_slc)):
        m = jnp.max(work, axis=0, keepdims=True)
        first = jnp.min(jnp.where(work == m, blk_f, float(nb)), axis=0, keepdims=True)
        pick = blk_f == first
        sel_t = jnp.where(pick, 1.0, sel_t)
        work = jnp.where(pick, NEG_INF, work)
    sel_bf = sel_t.T.astype(BF16)

    tk = NSA_TK
    bpt = tk // SLC_BLOCK

    def slc_step(kt, carry):
        m_run, l_run, acc = carry
        start = pl.multiple_of(kt * tk, tk)
        k = ks_ref[pl.ds(start, tk), :]
        v = vs_ref[pl.ds(start, tk), :]
        s = lax.dot_general(q, k, nt, preferred_element_type=F32).reshape(r, qb, tk)
        ei = lax.broadcasted_iota(jnp.int32, (nb, tk), 0)
        ej = lax.broadcasted_iota(jnp.int32, (nb, tk), 1)
        expand = jnp.where(ei == kt * bpt + ej // SLC_BLOCK, 1.0, 0.0).astype(BF16)
        picked = jnp.dot(sel_bf, expand, preferred_element_type=F32)
        kpos = start + lax.broadcasted_iota(jnp.int32, (qb, tk), 1)
        mask = ((picked > 0.5) & (kpos <= t_col))[None]
        s = jnp.where(mask, s, -1e30)
        m_new = jnp.maximum(m_run, jnp.max(s, axis=-1, keepdims=True))
        alpha = jnp.exp(m_run - m_new)
        p = jnp.where(mask, jnp.exp(s - m_new), 0.0)
        l_new = alpha * l_run + jnp.sum(p, axis=-1, keepdims=True)
        pv = jnp.dot(p.reshape(r * qb, tk).astype(BF16), v, preferred_element_type=F32)
        acc = alpha.reshape(r * qb, 1) * acc + pv
        return m_new, l_new, acc

    init = (jnp.full((r, qb, 1), -1e30, F32), jnp.zeros((r, qb, 1), F32), jnp.zeros((r * qb, d), F32))
    _, l_fin, acc = lax.fori_loop(0, (qi * qb) // tk + 1, slc_step, init)
    o_slc = acc / jnp.maximum(l_fin.reshape(r * qb, 1), 1e-30)

    k_tiles, v_tiles, pos_tiles = [], [], []
    for j in range(NSA_WTILES):
        kt = qi - (NSA_WTILES - 1) + j
        ktc = jnp.maximum(kt, 0)
        start = pl.multiple_of(ktc * qb, qb)
        k_tiles.append(kw_ref[pl.ds(start, qb), :])
        v_tiles.append(vw_ref[pl.ds(start, qb), :])
        lane = lax.broadcasted_iota(jnp.int32, (qb, qb), 1)
        pos_tiles.append(jnp.where(kt >= 0, start + lane, -1))
    k_w = jnp.concatenate(k_tiles, axis=0)
    v_w = jnp.concatenate(v_tiles, axis=0)
    k_pos = jnp.concatenate(pos_tiles, axis=1)
    span = NSA_WTILES * qb
    s = lax.dot_general(q, k_w, nt, preferred_element_type=F32).reshape(r, qb, span)
    dpos = t_col - k_pos
    w_mask = ((dpos >= 0) & (dpos < WINDOW) & (k_pos >= 0))[None]
    p_w = _masked_softmax_rows(s, w_mask)
    o_win = jnp.dot(p_w.reshape(r * qb, span).astype(BF16), v_w, preferred_element_type=F32)

    sig = jax.nn.sigmoid(gate_ref[...])
    lane = lax.broadcasted_iota(jnp.int32, sig.shape, 1)
    for h in range(r):
        rows = slice(h * qb, (h + 1) * qb)
        out = jnp.zeros((qb, d), F32)
        for branch, o_b in enumerate((o_cmp, o_slc, o_win)):
            col = branch * NSA_HEADS + g * r + h
            gate = jnp.sum(jnp.where(lane == col, sig, 0.0), axis=-1, keepdims=True)
            out = out + gate * o_b[rows]
        o_ref[:, h * d:(h + 1) * d] = out.astype(o_ref.dtype)


def nsa_prompt(q_bf, kvc, rows_bf, win_bf, proj, n_batch, t_len):
    d, r, qb = NSA_HEAD_DIM, NSA_GROUP, Q_BLOCK
    nqb = t_len // qb
    n_sub = kvc.shape[2]
    gate_blk = GATE_COL0 // LANES
    seq = lambda col: pl.BlockSpec((t_len, d), lambda b, g, i: (b, col(g)))
    return pl.pallas_call(
        _nsa_prompt_kernel,
        grid=(n_batch, NSA_KV_HEADS, nqb),
        in_specs=[pl.BlockSpec((qb, r * d), lambda b, g, i: (b * nqb + i, g)),
                  pl.BlockSpec((1, 1, n_sub, d), lambda b, g, i: (b, g, 0, 0)),
                  pl.BlockSpec((1, 1, n_sub, d), lambda b, g, i: (b, 2 + g, 0, 0)),
                  seq(lambda g: 4 + g), seq(lambda g: 6 + g),
                  seq(lambda g: g), seq(lambda g: 2 + g),
                  pl.BlockSpec((qb, LANES), lambda b, g, i: (b * nqb + i, gate_blk))],
        out_specs=pl.BlockSpec((qb, r * d), lambda b, g, i: (b * nqb + i, g)),
        out_shape=jax.ShapeDtypeStruct((n_batch * t_len, E_Q), BF16),
        compiler_params=pltpu.CompilerParams(
            dimension_semantics=("parallel", "parallel", "arbitrary"), vmem_limit_bytes=VMEM_LIMIT),
        name="nsa_prompt",
    )(q_bf, kvc, kvc, rows_bf, rows_bf, win_bf, win_bf, proj)


SAMPLE_PAGES = 16
NEW_ROWS_BLK = 16


def _page_specs(kind):
    def spec(i):
        return pl.BlockSpec((1, PAGE_SIZE, 1, NSA_KV_HEADS, NSA_HEAD_DIM),
                            lambda s, c, pt: (pt[s, c * SAMPLE_PAGES + i], 0, kind, 0, 0))
    return [spec(i) for i in range(SAMPLE_PAGES)]


def _sample_compress_kernel(pt_ref, *refs):
    n_in = 2 * SAMPLE_PAGES
    pages = refs[:n_in]
    pe_ref, w1_ref, a0_ref, a1_ref = refs[n_in:]
    d = NSA_HEAD_DIM
    per_page = PAGE_SIZE // CMP_STRIDE
    rows = SAMPLE_PAGES * per_page
    gs = NSA_KV_HEADS * CMP_STRIDE
    for kv in range(2):
        by_offset = [jnp.swapaxes(pages[kv * SAMPLE_PAGES + p].reshape(NSA_KV_HEADS * PAGE_SIZE, d)[...]
                                  .reshape(per_page, gs, d), 0, 1) for p in range(SAMPLE_PAGES)]
        acc0 = jnp.zeros((NSA_KV_HEADS * rows, d), F32)
        acc1 = jnp.zeros((NSA_KV_HEADS * rows, d), F32)
        for s in range(CMP_STRIDE):
            xs = jnp.concatenate([by_offset[p][NSA_KV_HEADS * s + g]
                                  for g in range(NSA_KV_HEADS) for p in range(SAMPLE_PAGES)],
                                 axis=0)
            a0 = (xs + pe_ref[kv, s:s + 1, :]).astype(BF16)
            a1 = (xs + pe_ref[kv, CMP_STRIDE + s:CMP_STRIDE + s + 1, :]).astype(BF16)
            acc0 = acc0 + jnp.dot(a0, w1_ref[kv, s], preferred_element_type=F32)
            acc1 = acc1 + jnp.dot(a1, w1_ref[kv, CMP_STRIDE + s], preferred_element_type=F32)
        for g in range(NSA_KV_HEADS):
            a0_ref[0, kv * NSA_KV_HEADS + g] = acc0[g * rows:(g + 1) * rows]
            a1_ref[0, kv * NSA_KV_HEADS + g] = acc1[g * rows:(g + 1) * rows]


def nsa_sample_compress(cache, page_table, cmp_pe, cmp_w1):
    bs, n_pages = page_table.shape
    d = NSA_HEAD_DIM
    per_page = PAGE_SIZE // CMP_STRIDE
    n_sub = n_pages * per_page
    rows = SAMPLE_PAGES * per_page
    w1 = cmp_w1.reshape(2, CMP_BLOCK, d, d).astype(BF16)
    out = jax.ShapeDtypeStruct((bs, 4, n_sub, d), F32)
    ospec = pl.BlockSpec((1, 4, rows, d), lambda s, c, pt: (s, 0, c, 0))
    page_specs = _page_specs(0) + _page_specs(1)
    return pl.pallas_call(
        _sample_compress_kernel,
        grid_spec=pltpu.PrefetchScalarGridSpec(
            num_scalar_prefetch=1,
            grid=(bs, n_pages // SAMPLE_PAGES),
            in_specs=page_specs + [
                pl.BlockSpec((2, CMP_BLOCK, d), lambda s, c, pt: (0, 0, 0)),
                pl.BlockSpec((2, CMP_BLOCK, d, d), lambda s, c, pt: (0, 0, 0, 0))],
            out_specs=[ospec, ospec]),
        out_shape=[out, out],
        compiler_params=pltpu.CompilerParams(
            dimension_semantics=("parallel", "arbitrary"), vmem_limit_bytes=VMEM_LIMIT),
        name="nsa_sample_compress",
    )(page_table, *([cache] * (2 * SAMPLE_PAGES)), cmp_pe, w1)


def _sample_select_kernel(a0_ref, a1_ref, w2_ref, q_ref, ocmp_ref, sel_ref, *, past_len, t_len):
    d, r = NSA_HEAD_DIM, NSA_GROUP
    n_sub = a0_ref.shape[2]
    n_cmp = n_sub - 1
    n_slc = -(-(past_len + t_len) // SLC_BLOCK)
    lanes = sel_ref.shape[3]
    nt = (((1,), (1,)), ((), ()))
    rq = r * t_len
    for g in range(NSA_KV_HEADS):
        kc = jnp.dot(_gelu_tanh(a0_ref[0, g] + pltpu.roll(a1_ref[0, g], n_sub - 1, 0)).astype(BF16), w2_ref[0],
                     preferred_element_type=F32).astype(BF16)
        vc = jnp.dot(_gelu_tanh(a0_ref[0, 2 + g] + pltpu.roll(a1_ref[0, 2 + g], n_sub - 1, 0)).astype(BF16),
                     w2_ref[1], preferred_element_type=F32).astype(BF16)
        q = q_ref[0, g]
        s = lax.dot_general(q, kc, nt, preferred_element_type=F32)
        pos = past_len + lax.broadcasted_iota(jnp.int32, (rq, 1), 0) % t_len
        n_idx = lax.broadcasted_iota(jnp.int32, (rq, n_sub), 1)
        p_c = _masked_softmax_rows(s, (n_idx * CMP_STRIDE + (CMP_BLOCK - 1) <= pos) & (n_idx < n_cmp))
        ocmp_ref[0, g] = jnp.dot(p_c.astype(BF16), vc, preferred_element_type=F32)
        ri = lax.broadcasted_iota(jnp.int32, (8, rq), 0)
        cj = lax.broadcasted_iota(jnp.int32, (8, rq), 1)
        head_sum = jnp.where(cj % t_len == ri, 1.0, 0.0).astype(BF16)
        p_hi = p_c.astype(BF16)
        p_lo = (p_c - p_hi.astype(F32)).astype(BF16)
        p_sum = (jnp.dot(head_sum, p_hi, preferred_element_type=F32)
                 + jnp.dot(head_sum, p_lo, preferred_element_type=F32))
        ci = lax.broadcasted_iota(jnp.int32, (n_sub, lanes), 0)
        mi = lax.broadcasted_iota(jnp.int32, (n_sub, lanes), 1)
        overlap = jnp.where((ci * CMP_STRIDE < mi * SLC_BLOCK + SLC_BLOCK)
                            & (ci * CMP_STRIDE + CMP_BLOCK > mi * SLC_BLOCK) & (ci < n_cmp), 1.0, 0.0).astype(BF16)
        s_hi = p_sum.astype(BF16)
        s_lo = (p_sum - s_hi.astype(F32)).astype(BF16)
        imp = (jnp.dot(s_hi, overlap, preferred_element_type=F32)
               + jnp.dot(s_lo, overlap, preferred_element_type=F32))
        blk = lax.broadcasted_iota(jnp.int32, (8, lanes), 1)
        blk_f = blk.astype(F32)
        tpos = past_len + lax.broadcasted_iota(jnp.int32, (8, 1), 0) % t_len
        cur = tpos // SLC_BLOCK
        forced = (blk == 0) | (blk == cur) | (blk == cur - 1)
        work = jnp.where(blk * SLC_BLOCK <= tpos, jnp.where(forced, 1e6, imp), -1e6)
        work = jnp.where(blk < n_slc, work, NEG_INF)
        sel = jnp.zeros((8, lanes), F32)
        for _ in range(min(SLC_TOPK, n_slc)):
            m = jnp.max(work, axis=-1, keepdims=True)
            first = jnp.min(jnp.where(work == m, blk_f, float(lanes)), axis=-1, keepdims=True)
            pick = blk_f == first
            sel = jnp.where(pick, 1.0, sel)
            work = jnp.where(pick, NEG_INF, work)
        sel_ref[0, g] = sel


def nsa_sample_select(a0, a1, cmp_w2, q_s, past_len, t_len):
    bs, _, n_sub, d = a0.shape
    n_slc = -(-(past_len + t_len) // SLC_BLOCK)
    lanes = _round_up(n_slc, LANES)
    rq = NSA_GROUP * t_len
    aspec = pl.BlockSpec((1, 4, n_sub, d), lambda s: (s, 0, 0, 0))
    return pl.pallas_call(
        functools.partial(_sample_select_kernel, past_len=past_len, t_len=t_len),
        grid=(bs,),
        in_specs=[aspec, aspec, pl.BlockSpec((2, d, d), lambda s: (0, 0, 0)),
                  pl.BlockSpec((1, NSA_KV_HEADS, rq, d), lambda s: (s, 0, 0, 0))],
        out_specs=[pl.BlockSpec((1, NSA_KV_HEADS, rq, d), lambda s: (s, 0, 0, 0)),
                   pl.BlockSpec((1, NSA_KV_HEADS, 8, lanes), lambda s: (s, 0, 0, 0))],
        out_shape=[jax.ShapeDtypeStruct((bs, NSA_KV_HEADS, rq, d), F32),
                   jax.ShapeDtypeStruct((bs, NSA_KV_HEADS, 8, lanes), F32)],
        compiler_params=pltpu.CompilerParams(dimension_semantics=("parallel",), vmem_limit_bytes=VMEM_LIMIT),
        name="nsa_sample_select",
    )(a0, a1, cmp_w2.astype(BF16), q_s)


def _sample_attend_kernel(pt_ref, *refs, past_len, t_len, row0):
    k_pages = refs[:SAMPLE_PAGES]
    v_pages = refs[SAMPLE_PAGES:2 * SAMPLE_PAGES]
    (q_ref, sel_ref, ocmp_ref, newrows_ref, kwin_ref, vwin_ref, newwin_ref, gate_ref, o_ref,
     m_scr, l_scr, acc_scr) = refs[2 * SAMPLE_PAGES:]
    s_idx = pl.program_id(0)
    c = pl.program_id(1)
    d, r = NSA_HEAD_DIM, NSA_GROUP
    rq = r * t_len
    lanes = sel_ref.shape[3]
    tk = SAMPLE_PAGES * PAGE_SIZE
    nt = (((1,), (1,)), ((), ()))

    @pl.when(c == 0)
    def _():
        m_scr[...] = jnp.full(m_scr.shape, -1e30, F32)
        l_scr[...] = jnp.zeros(l_scr.shape, F32)
        acc_scr[...] = jnp.zeros(acc_scr.shape, F32)

    row = lax.broadcasted_iota(jnp.int32, (rq, 1), 0)
    t_row = row % t_len
    pos = past_len + t_row
    ti = lax.broadcasted_iota(jnp.int32, (rq, 8), 1)
    tok_expand = jnp.where(ti == t_row, 1.0, 0.0).astype(BF16)

    def online(g, s, mask, v):
        s = jnp.where(mask, s, -1e30)
        m_old = m_scr[g]
        m_new = jnp.maximum(m_old, jnp.max(s, axis=-1, keepdims=True))
        alpha = jnp.exp(m_old - m_new)
        p = jnp.where(mask, jnp.exp(s - m_new), 0.0)
        l_scr[g] = alpha * l_scr[g] + jnp.sum(p, axis=-1, keepdims=True)
        acc_scr[g] = alpha * acc_scr[g] + jnp.dot(p.astype(BF16), v, preferred_element_type=F32)
        m_scr[g] = m_new

    sel16 = []
    for g in range(NSA_KV_HEADS):
        sel16.append(jnp.dot(tok_expand, sel_ref[0, g].astype(BF16), preferred_element_type=F32))
        k = jnp.concatenate([pg.reshape(NSA_KV_HEADS * PAGE_SIZE, d)[pl.ds(g, PAGE_SIZE, stride=NSA_KV_HEADS), :]
                             for pg in k_pages], axis=0)
        v = jnp.concatenate([pg.reshape(NSA_KV_HEADS * PAGE_SIZE, d)[pl.ds(g, PAGE_SIZE, stride=NSA_KV_HEADS), :]
                             for pg in v_pages], axis=0)
        q = q_ref[0, g]
        s = lax.dot_general(q, k.astype(BF16), nt, preferred_element_type=F32)
        ei = lax.broadcasted_iota(jnp.int32, (lanes, tk), 0)
        ej = lax.broadcasted_iota(jnp.int32, (lanes, tk), 1)
        expand = jnp.where(ei == c * (tk // SLC_BLOCK) + ej // SLC_BLOCK, 1.0, 0.0).astype(BF16)
        picked = jnp.dot(sel16[g].astype(BF16), expand, preferred_element_type=F32)
        kpos = c * tk + lax.broadcasted_iota(jnp.int32, (rq, tk), 1)
        online(g, s, (picked > 0.5) & (kpos <= pos), v.astype(BF16))

    @pl.when(c == pl.num_programs(1) - 1)
    def _():
        mine = (row0 // t_len + s_idx) % (NEW_ROWS_BLK // t_len)
        j = lax.broadcasted_iota(jnp.int32, (rq, NEW_ROWS_BLK), 1)
        own = j // t_len == mine
        new_pos = past_len + j % t_len
        sig = jax.nn.sigmoid(gate_ref[...])
        gsel = jnp.where(j == mine * t_len + t_row, 1.0, 0.0).astype(BF16)
        s_hi = sig.astype(BF16)
        s_lo = (sig - s_hi.astype(F32)).astype(BF16)
        sig_rows = (jnp.dot(gsel, s_hi, preferred_element_type=F32)
                    + jnp.dot(gsel, s_lo, preferred_element_type=F32))
        lane = lax.broadcasted_iota(jnp.int32, sig_rows.shape, 1)
        lane_l = lax.broadcasted_iota(jnp.int32, (rq, lanes), 1)
        for g in range(NSA_KV_HEADS):
            q = q_ref[0, g]
            kn = newrows_ref[:, (4 + g) * d:(5 + g) * d].astype(BF16)
            vn = newrows_ref[:, (6 + g) * d:(7 + g) * d].astype(BF16)
            s = lax.dot_general(q, kn, nt, preferred_element_type=F32)
            last_picked = jnp.sum(jnp.where(lane_l == past_len // SLC_BLOCK, sel16[g], 0.0), axis=-1, keepdims=True)
            online(g, s, own & (new_pos <= pos) & (last_picked > 0.5), vn)
            o_slc = acc_scr[g] / jnp.maximum(l_scr[g], 1e-30)
            w_len = kwin_ref.shape[1]
            win_rows = pl.ds(g, w_len, stride=NSA_KV_HEADS)
            kw = kwin_ref.reshape(NSA_KV_HEADS * w_len, d)[win_rows, :].astype(BF16)
            vw = vwin_ref.reshape(NSA_KV_HEADS * w_len, d)[win_rows, :].astype(BF16)
            s1 = lax.dot_general(q, kw, nt, preferred_element_type=F32)
            kp1 = past_len - w_len + lax.broadcasted_iota(jnp.int32, (rq, w_len), 1)
            d1 = pos - kp1
            m1 = (d1 >= 0) & (d1 < WINDOW) & (kp1 >= 0)
            knw = newwin_ref[:, g * d:(g + 1) * d].astype(BF16)
            vnw = newwin_ref[:, (2 + g) * d:(3 + g) * d].astype(BF16)
            s2 = lax.dot_general(q, knw, nt, preferred_element_type=F32)
            d2 = pos - new_pos
            m2 = own & (d2 >= 0) & (d2 < WINDOW)
            s1 = jnp.where(m1, s1, -1e30)
            s2 = jnp.where(m2, s2, -1e30)
            mx = jnp.maximum(jnp.max(s1, axis=-1, keepdims=True), jnp.max(s2, axis=-1, keepdims=True))
            p1 = jnp.where(m1, jnp.exp(s1 - mx), 0.0)
            p2 = jnp.where(m2, jnp.exp(s2 - mx), 0.0)
            den = jnp.sum(p1, axis=-1, keepdims=True) + jnp.sum(p2, axis=-1, keepdims=True)
            o_win = (jnp.dot(p1.astype(BF16), vw, preferred_element_type=F32)
                     + jnp.dot(p2.astype(BF16), vnw, preferred_element_type=F32)) / jnp.maximum(den, 1e-30)
            out = jnp.zeros((rq, d), F32)
            for branch, o_b in enumerate((ocmp_ref[0, g], o_slc, o_win)):
                col = branch * NSA_HEADS + g * r + row // t_len
                gate = jnp.sum(jnp.where(lane == col, sig_rows, 0.0), axis=-1, keepdims=True)
                out = out + gate * o_b
            o_ref[0, g] = out


def nsa_sample_attend(cache, page_table, q_s, sel, o_cmp, rows, win, win_cache, proj, past_len, t_len, row0):
    bs, n_pages = page_table.shape
    d = NSA_HEAD_DIM
    rq = NSA_GROUP * t_len
    lanes = sel.shape[3]
    w_len = win_cache.shape[1]
    assert row0 % t_len == 0 and NEW_ROWS_BLK % t_len == 0 and past_len % SLC_BLOCK == 0
    blk = lambda s: (row0 + s * t_len) // NEW_ROWS_BLK
    per_seq = lambda shape: pl.BlockSpec((1,) + shape, lambda s, c, pt: (s, 0, 0, 0))
    return pl.pallas_call(
        functools.partial(_sample_attend_kernel, past_len=past_len, t_len=t_len, row0=row0),
        grid_spec=pltpu.PrefetchScalarGridSpec(
            num_scalar_prefetch=1,
            grid=(bs, n_pages // SAMPLE_PAGES),
            in_specs=_page_specs(2) + _page_specs(3) + [
                per_seq((NSA_KV_HEADS, rq, d)), per_seq((NSA_KV_HEADS, 8, lanes)), per_seq((NSA_KV_HEADS, rq, d)),
                pl.BlockSpec((NEW_ROWS_BLK, 8 * d), lambda s, c, pt: (blk(s), 0)),
                pl.BlockSpec((1, w_len, 1, NSA_KV_HEADS, d), lambda s, c, pt: (s, 0, 0, 0, 0)),
                pl.BlockSpec((1, w_len, 1, NSA_KV_HEADS, d), lambda s, c, pt: (s, 0, 1, 0, 0)),
                pl.BlockSpec((NEW_ROWS_BLK, 4 * d), lambda s, c, pt: (blk(s), 0)),
                pl.BlockSpec((NEW_ROWS_BLK, LANES), lambda s, c, pt: (blk(s), GATE_COL0 // LANES))],
            out_specs=per_seq((NSA_KV_HEADS, rq, d)),
            scratch_shapes=[pltpu.VMEM((NSA_KV_HEADS, rq, 1), F32), pltpu.VMEM((NSA_KV_HEADS, rq, 1), F32),
                            pltpu.VMEM((NSA_KV_HEADS, rq, d), F32)]),
        out_shape=jax.ShapeDtypeStruct((bs, NSA_KV_HEADS, rq, d), F32),
        compiler_params=pltpu.CompilerParams(
            dimension_semantics=("parallel", "arbitrary"), vmem_limit_bytes=VMEM_LIMIT),
        name="nsa_sample_attend",
    )(page_table, *([cache] * (2 * SAMPLE_PAGES)), q_s, sel, o_cmp, rows, win_cache, win_cache, win, proj)


SC_ROWS = 512
SC_COLS = 512


def _short_conv_kernel(b_ref, c_ref, h_ref, w_ref, buf_ref, o_ref, tail_ref, prev_scr):
    @pl.when(pl.program_id(2) == 0)
    def _():
        prev_scr[...] = buf_ref[0]

    rows = c_ref.shape[0]
    u = c_ref[...] * h_ref[...]
    prev = prev_scr[...]
    row8 = lax.broadcasted_iota(jnp.int32, prev.shape, 0)
    conv = u * w_ref[SC_KSIZE - 1:SC_KSIZE, :]
    for k in range(1, SC_KSIZE):
        rolled = pltpu.roll(u, k, 0)
        top = jnp.where(row8 < k, pltpu.roll(prev, k, 0), rolled[0:8])
        shifted = top if rows == 8 else jnp.concatenate([top, rolled[8:]], axis=0)
        conv = conv + shifted * w_ref[SC_KSIZE - 1 - k:SC_KSIZE - k, :]
    prev_scr[...] = u[rows - 8:rows]
    o_ref[...] = (b_ref[...] * conv).astype(o_ref.dtype)
    tail_ref[0] = u[rows - 8:rows]


def short_conv(proj, n_batch, t_len, rows_per_step, conv_buf, sc_w):
    nr = t_len // rows_per_step
    nh = SC_WIDTH // SC_COLS
    blk0 = SC_COL0 // SC_COLS
    assert SC_COL0 % SC_COLS == 0
    buf8 = jnp.pad(conv_buf, ((0, 0), (8 - (SC_KSIZE - 1), 0), (0, 0)))
    col = lambda part: pl.BlockSpec((rows_per_step, SC_COLS), lambda b, j, i: (b * nr + i, blk0 + part * nh + j))
    return pl.pallas_call(
        _short_conv_kernel,
        grid=(n_batch, nh, nr),
        in_specs=[col(0), col(1), col(2), pl.BlockSpec((SC_KSIZE, SC_COLS), lambda b, j, i: (0, j)),
                  pl.BlockSpec((1, 8, SC_COLS), lambda b, j, i: (b, 0, j))],
        out_specs=[pl.BlockSpec((rows_per_step, SC_COLS), lambda b, j, i: (b * nr + i, j)),
                   pl.BlockSpec((1, 8, SC_COLS), lambda b, j, i: (b, 0, j))],
        out_shape=[jax.ShapeDtypeStruct((n_batch * t_len, SC_WIDTH), BF16),
                   jax.ShapeDtypeStruct((n_batch, 8, SC_WIDTH), F32)],
        scratch_shapes=[pltpu.VMEM((8, SC_COLS), F32)],
        compiler_params=pltpu.CompilerParams(
            dimension_semantics=("parallel", "parallel", "arbitrary"), vmem_limit_bytes=VMEM_LIMIT),
        name="short_conv",
    )(proj, proj, proj, sc_w, buf8)


def _ret_tables(pos, chunk, valid):
    half = RET_DQK // 2
    inv = RET_THETA ** (-jnp.arange(half, dtype=F32) / half)
    ang = pos.astype(F32)[:, None] * inv[None, :]
    log_g = jnp.log1p(-(2.0 ** (-5.0 - jnp.arange(RET_HEADS, dtype=F32))))
    i = jnp.arange(chunk, dtype=F32)
    diff = i[:, None] - i[None, :]
    intra = jnp.where(diff >= 0, jnp.exp(jnp.maximum(diff, 0.0)[None] * log_g[:, None, None]), 0.0)
    q_dec = jnp.exp((i[None, :] + 1.0) * log_g[:, None])[..., None]
    k_dec = jnp.exp((valid - 1.0 - i)[None, :] * log_g[:, None])[..., None]
    c_dec = jnp.exp(valid * log_g)[:, None, None]
    return jnp.cos(ang), jnp.sin(ang), intra, q_dec, k_dec, c_dec


def _retention_kernel(q_ref, k_ref, v_ref, g_ref, cos_ref, sin_ref, intra_ref, qd_ref, kd_ref, cd_ref, gn_ref,
                      s0_ref, o_ref, s_out_ref, s_scr):
    c_idx = pl.program_id(1)

    @pl.when(c_idx == 0)
    def _():
        s_scr[...] = s0_ref[0]

    cos, sin = cos_ref[...], sin_ref[...]
    half = RET_DQK // 2
    nt = (((1,), (1,)), ((), ()))
    tn = (((0,), (0,)), ((), ()))

    def rot(x):
        x1, x2 = x[:, :half], x[:, half:]
        return jnp.concatenate([x1 * cos - x2 * sin, x1 * sin + x2 * cos], axis=-1)

    for h in range(RET_HEADS):
        cols = slice(h * RET_DQK, (h + 1) * RET_DQK)
        qr = (rot(q_ref[:, cols]) * (RET_DQK ** -0.5)).astype(BF16)
        kr = rot(k_ref[:, cols])
        v = v_ref[:, cols].astype(BF16)
        att = lax.dot_general(qr, kr.astype(BF16), nt, preferred_element_type=F32) * intra_ref[h]
        s_old = s_scr[h]
        o = (jnp.dot(att.astype(BF16), v, preferred_element_type=F32)
             + jnp.dot(qr, s_old.astype(BF16), preferred_element_type=F32) * qd_ref[h])
        s_scr[h] = s_old * cd_ref[h] + lax.dot_general((kr * kd_ref[h]).astype(BF16), v, tn,
                                                       preferred_element_type=F32)
        mu = jnp.mean(o, axis=-1, keepdims=True)
        dev = o - mu
        var = jnp.mean(dev * dev, axis=-1, keepdims=True)
        gate = g_ref[:, cols]
        on = dev * lax.rsqrt(var + EPS) * gn_ref[:, cols] * (gate * jax.nn.sigmoid(gate))
        o_ref[:, cols] = on.astype(o_ref.dtype)
    s_out_ref[0] = s_scr[...]


def retention(proj, pos, n_batch, t_len, chunk, state0, gn_gain, valid=None):
    nc = t_len // chunk
    cos, sin, intra, q_dec, k_dec, c_dec = _ret_tables(pos, chunk, chunk if valid is None else valid)
    half = RET_DQK // 2
    col = lambda j: pl.BlockSpec((chunk, R_QK), lambda b, c: (b * nc + c, j))
    tab = pl.BlockSpec((chunk, half), lambda b, c: (c, 0))
    full = lambda a: pl.BlockSpec(a.shape, lambda b, c: (0,) * a.ndim)
    st = pl.BlockSpec((1, RET_HEADS, RET_DQK, RET_DV), lambda b, c: (b, 0, 0, 0))
    return pl.pallas_call(
        _retention_kernel,
        grid=(n_batch, nc),
        in_specs=[col(0), col(1), col(2), col(3), tab, tab, full(intra), full(q_dec), full(k_dec), full(c_dec),
                  pl.BlockSpec((1, R_V), lambda b, c: (0, 0)), st],
        out_specs=[pl.BlockSpec((chunk, R_V), lambda b, c: (b * nc + c, 0)), st],
        out_shape=[jax.ShapeDtypeStruct((n_batch * t_len, R_V), BF16),
                   jax.ShapeDtypeStruct((n_batch, RET_HEADS, RET_DQK, RET_DV), F32)],
        scratch_shapes=[pltpu.VMEM((RET_HEADS, RET_DQK, RET_DV), F32)],
        compiler_params=pltpu.CompilerParams(
            dimension_semantics=("parallel", "arbitrary"), vmem_limit_bytes=VMEM_LIMIT),
        name="retention",
    )(proj, proj, proj, proj, cos, sin, intra, q_dec, k_dec, c_dec, gn_gain.reshape(1, R_V), state0)


SSD_COL0 = 2 * R_QK + 2 * R_V
HEADS_PER_GROUP = M2_HEADS // M2_GROUPS


def _split3(x):
    a = x.astype(BF16)
    r = x - a.astype(F32)
    b = r.astype(BF16)
    c = (r - b.astype(F32)).astype(BF16)
    return a, b, c


def _exact_dot(mat_bf, x):
    out = None
    for piece in _split3(x):
        t = jnp.dot(mat_bf, piece, preferred_element_type=F32)
        out = t if out is None else out + t
    return out


def _exact_dot_r(x, mat_bf):
    out = None
    for piece in _split3(x):
        t = jnp.dot(piece, mat_bf, preferred_element_type=F32)
        out = t if out is None else out + t
    return out


def _ssd_kernel(z_ref, xa_ref, xb_ref, xc_ref, dt_ref, cw_ref, cb_ref, dtb_ref, aneg_ref, dskip_ref, norm_ref,
                buf0_ref, s0_ref, o_ref, s_out_ref, s_scr, prev_scr, *, valid):
    c_idx = pl.program_id(1)
    chunk = z_ref.shape[0]
    nt = (((1,), (1,)), ((), ()))
    tn = (((0,), (0,)), ((), ()))

    @pl.when(c_idx == 0)
    def _():
        s_scr[...] = s0_ref[0]
        prev_scr[...] = buf0_ref[0]

    x = jnp.concatenate([xa_ref[...], xb_ref[...], xc_ref[...]], axis=1)
    prev = prev_scr[...]
    row8 = lax.broadcasted_iota(jnp.int32, prev.shape, 0)
    conv = x * cw_ref[M2_CONV - 1:M2_CONV, :]
    for k in range(1, M2_CONV):
        rolled = pltpu.roll(x, k, 0)
        top = jnp.where(row8 < k, pltpu.roll(prev, k, 0), rolled[0:8])
        shifted = top if chunk == 8 else jnp.concatenate([top, rolled[8:]], axis=0)
        conv = conv + shifted * cw_ref[M2_CONV - 1 - k:M2_CONV - k, :]
    prev_scr[...] = x[chunk - 8:chunk]
    conv = conv + cb_ref[...]
    xbc = conv * jax.nn.sigmoid(conv)
    xs = xbc[:, :M2_DINNER]

    dt_raw = dt_ref[...] + dtb_ref[...]
    dt = jnp.where(dt_raw > 20.0, dt_raw, jnp.log1p(jnp.exp(jnp.minimum(dt_raw, 20.0))))
    if valid < chunk:
        dt = jnp.where(lax.broadcasted_iota(jnp.int32, dt.shape, 0) < valid, dt, 0.0)
    a = dt * aneg_ref[...]
    ri = lax.broadcasted_iota(jnp.int32, (chunk, chunk), 0)
    ci = lax.broadcasted_iota(jnp.int32, (chunk, chunk), 1)
    tri = ri >= ci
    cum = _exact_dot(jnp.where(tri, 1.0, 0.0).astype(BF16), a)
    cum_t = cum.T
    cum_last = cum[chunk - 1:chunk, :]
    hi = lax.broadcasted_iota(jnp.int32, (LANES, M2_DINNER), 0)
    li = lax.broadcasted_iota(jnp.int32, (LANES, M2_DINNER), 1)
    expand = jnp.where(hi == li // M2_HEADDIM, 1.0, 0.0).astype(BF16)
    dt_x = _exact_dot_r(dt, expand)
    cum_x = _exact_dot_r(cum, expand)
    last_x = _exact_dot_r(cum_last, expand)
    xdt = xs * dt_x
    x_dec = (xdt * jnp.exp(last_x - cum_x)).astype(BF16)
    xdt_bf = xdt.astype(BF16)
    e_cum_x = jnp.exp(cum_x)
    e_last_x = jnp.exp(last_x)

    y_parts = []
    for gi in range(M2_GROUPS):
        b_g = xbc[:, M2_DINNER + gi * M2_STATE:M2_DINNER + (gi + 1) * M2_STATE].astype(BF16)
        c_g = xbc[:, M2_DINNER + (M2_GROUPS + gi) * M2_STATE:M2_DINNER + (M2_GROUPS + gi + 1) * M2_STATE].astype(BF16)
        cb = lax.dot_general(c_g, b_g, nt, preferred_element_type=F32)
        gcols = slice(gi * HEADS_PER_GROUP * M2_HEADDIM, (gi + 1) * HEADS_PER_GROUP * M2_HEADDIM)
        s_old = s_scr[:, gcols]
        y_state = jnp.dot(c_g, s_old.astype(BF16), preferred_element_type=F32) * e_cum_x[:, gcols]
        s_scr[:, gcols] = s_old * e_last_x[:, gcols] + lax.dot_general(b_g, x_dec[:, gcols], tn,
                                                                        preferred_element_type=F32)
        pair_lane = lax.broadcasted_iota(jnp.int32, (chunk, LANES), 1)
        intra = []
        for pr in range(HEADS_PER_GROUP // 2):
            outs = []
            for sub in range(2):
                h = gi * HEADS_PER_GROUP + pr * 2 + sub
                seg = cum[:, h:h + 1] - cum_t[h:h + 1, :]
                l_mat = jnp.where(tri, jnp.exp(jnp.where(tri, seg, 0.0)), 0.0)
                lanes = slice(gi * HEADS_PER_GROUP * M2_HEADDIM + pr * LANES,
                              gi * HEADS_PER_GROUP * M2_HEADDIM + (pr + 1) * LANES)
                outs.append(jnp.dot((cb * l_mat).astype(BF16), xdt_bf[:, lanes], preferred_element_type=F32))
            intra.append(jnp.where(pair_lane < M2_HEADDIM, outs[0], outs[1]))
        y_parts.append(jnp.concatenate(intra, axis=1) + y_state)
    y = jnp.concatenate(y_parts, axis=1) + dskip_ref[...] * xs
    z = z_ref[...]
    y = y * (z * jax.nn.sigmoid(z))
    gw = M2_DINNER // M2_GROUPS
    outs = []
    for gi in range(M2_GROUPS):
        yg = y[:, gi * gw:(gi + 1) * gw]
        outs.append(yg * lax.rsqrt(jnp.mean(yg * yg, axis=-1, keepdims=True) + EPS))
    o_ref[...] = (jnp.concatenate(outs, axis=1) * norm_ref[...]).astype(o_ref.dtype)
    s_out_ref[0] = s_scr[...]


def ssd(proj, n_batch, t_len, chunk, conv_buf, state0, conv_w, conv_b, dt_bias, a_log, d_skip, m2_norm, valid=None):
    nc = t_len // chunk
    z_blk = SSD_COL0 // M2_DINNER
    xw = M2_CONV_DIM // 3
    xbc_blk = (SSD_COL0 + M2_DINNER) // xw
    assert (SSD_COL0 + M2_DINNER) % xw == 0 and xw % LANES == 0
    dt_blk = (SSD_COL0 + M2_DINNER + M2_CONV_DIM) // LANES
    pad = lambda v: jnp.pad(v.reshape(1, -1), ((0, 0), (0, LANES - v.shape[-1])))
    buf8 = jnp.pad(conv_buf, ((0, 0), (8 - (M2_CONV - 1), 0), (0, 0)))
    st_t = state0.transpose(0, 3, 1, 2).reshape(n_batch, M2_STATE, M2_DINNER)
    row = lambda a: pl.BlockSpec(a.shape, lambda b, c: (0, 0))
    cw = conv_w
    cb = conv_b.reshape(1, -1)
    dtb, aneg = pad(dt_bias), pad(-jnp.exp(a_log))
    dsk = jnp.repeat(d_skip, M2_HEADDIM).reshape(1, -1)
    nrm = m2_norm.reshape(1, -1)
    st = pl.BlockSpec((1, M2_STATE, M2_DINNER), lambda b, c: (b, 0, 0))
    out, s_fin = pl.pallas_call(
        functools.partial(_ssd_kernel, valid=chunk if valid is None else valid),
        grid=(n_batch, nc),
        in_specs=[pl.BlockSpec((chunk, M2_DINNER), lambda b, c: (b * nc + c, z_blk)),
                  pl.BlockSpec((chunk, xw), lambda b, c: (b * nc + c, xbc_blk)),
                  pl.BlockSpec((chunk, xw), lambda b, c: (b * nc + c, xbc_blk + 1)),
                  pl.BlockSpec((chunk, xw), lambda b, c: (b * nc + c, xbc_blk + 2)),
                  pl.BlockSpec((chunk, LANES), lambda b, c: (b * nc + c, dt_blk)),
                  row(cw), row(cb), row(dtb), row(aneg), row(dsk), row(nrm),
                  pl.BlockSpec((1, 8, M2_CONV_DIM), lambda b, c: (b, 0, 0)), st],
        out_specs=[pl.BlockSpec((chunk, M2_DINNER), lambda b, c: (b * nc + c, 0)), st],
        out_shape=[jax.ShapeDtypeStruct((n_batch * t_len, M2_DINNER), BF16),
                   jax.ShapeDtypeStruct((n_batch, M2_STATE, M2_DINNER), F32)],
        scratch_shapes=[pltpu.VMEM((M2_STATE, M2_DINNER), F32), pltpu.VMEM((8, M2_CONV_DIM), F32)],
        compiler_params=pltpu.CompilerParams(
            dimension_semantics=("parallel", "arbitrary"), vmem_limit_bytes=VMEM_LIMIT),
        name="ssd",
    )(proj, proj, proj, proj, proj, cw, cb, dtb, aneg, dsk, nrm, buf8, st_t)
    s_fin = s_fin.reshape(n_batch, M2_STATE, M2_HEADS, M2_HEADDIM).transpose(0, 2, 3, 1)
    return out, s_fin


SEQ_ROWS = 8


def _pad_seq_rows(a, n_seq, t_len):
    assert t_len <= SEQ_ROWS
    a = a.reshape(n_seq, t_len, a.shape[-1])
    return jnp.pad(a, ((0, 0), (0, SEQ_ROWS - t_len), (0, 0))).reshape(n_seq * SEQ_ROWS, a.shape[-1])


def _unpad_seq_rows(a, n_seq, t_len):
    return a.reshape(n_seq, SEQ_ROWS, a.shape[-1])[:, :t_len].reshape(n_seq * t_len, a.shape[-1])

def kernel(x_prompt, x_sample, cache_nsa_kv, cache_nsa_win, state_sc_conv, state_ret, state_ssm, state_m2_conv,
           page_table, norm_w, final_norm, e_w_in, e_w_out, e_cmp_pe, e_cmp_w1, e_cmp_w2, e_sc_conv, o_w_in,
           o_w_out, o_ret_gn, o_m2_conv_w, o_m2_conv_b, o_m2_dt_bias, o_m2_a_log, o_m2_d, o_m2_norm, peer_wq,
           peer_keys, peer_u, peer_v):
    bp, tp, dm = x_prompt.shape
    bs, ts, _ = x_sample.shape
    n_p, n_s = bp * tp, bs * ts
    g, r, d = NSA_KV_HEADS, NSA_GROUP, NSA_HEAD_DIM
    past_len = page_table.shape[1] * PAGE_SIZE
    pos_p = jnp.arange(tp, dtype=jnp.int32)
    pos_s = past_len + jnp.arange(ts, dtype=jnp.int32)
    n_real = n_p + n_s
    n_pad = _round_up(n_real, PEER_TM) - n_real
    s_rows = slice(n_p, n_real)
    pos_rows = jnp.concatenate([jnp.tile(pos_p, bp), jnp.tile(pos_s, bs), jnp.zeros((n_pad,), jnp.int32)])
    xp = x_prompt.reshape(n_p, dm)
    x = jnp.concatenate([xp, x_sample.reshape(n_s, dm), xp[:n_pad]], axis=0)
    pad_bf = jnp.zeros((n_pad, E_Q), BF16)

    w0 = e_w_in[0]
    w_in = jnp.concatenate([w0[:, :SC_COL0], w0[:, SC_COL0 + E_G:], w0[:, SC_COL0:SC_COL0 + E_G],
                            jnp.zeros((dm, _pad_cols(E_IN) - E_IN), F32)], axis=1).astype(BF16)
    proj = norm_matmul(x, norm_w[0, 0], w_in)
    q_bf, rows, win, rows_bf, win_bf = nsa_rope(proj, pos_rows)
    kvc = nsa_compress(rows, bp, tp, e_cmp_pe[0], e_cmp_w1[0], e_cmp_w2[0])
    o_nsa_p = nsa_prompt(q_bf, kvc, rows_bf, win_bf, proj, bp, tp)
    p_kv = rows[:n_p].reshape(bp, tp, 4, g, d)
    s_kv = rows[s_rows].reshape(bs, ts, 4, g, d)
    p_win = win[:n_p].reshape(bp, tp, 2, g, d)[:, tp - min(WINDOW, tp):]
    win_s = win[s_rows].reshape(bs, ts, 2, g, d)
    s_win = jnp.concatenate([cache_nsa_win[0], win_s], axis=1)[:, ts:]
    cache = cache_nsa_kv[0]
    q_s = q_bf[s_rows].reshape(bs, ts, g, r, d).transpose(0, 2, 3, 1, 4).reshape(bs, g, r * ts, d)
    a0, a1 = nsa_sample_compress(cache, page_table, e_cmp_pe[0], e_cmp_w1[0])
    o_cmp_s, sel_s = nsa_sample_select(a0, a1, e_cmp_w2[0], q_s, past_len, ts)
    o_nsa_s = nsa_sample_attend(cache, page_table, q_s, sel_s, o_cmp_s, rows, win,
                                cache_nsa_win[0], proj, past_len, ts, n_p)
    o_nsa_s = o_nsa_s.reshape(bs, g, r, ts, d).transpose(0, 3, 1, 2, 4).reshape(n_s, E_Q)
    o_nsa = jnp.concatenate([o_nsa_p, o_nsa_s.astype(BF16), pad_bf], axis=0)
    o_sc_p, tail_p = short_conv(proj, bp, tp, SC_ROWS, jnp.zeros((bp, SC_KSIZE - 1, SC_WIDTH), F32), e_sc_conv[0])
    p_sc = tail_p[:, 8 - (SC_KSIZE - 1):]
    o_sc_s, tail_s = short_conv(_pad_seq_rows(proj[s_rows], bs, ts), bs, SEQ_ROWS, SEQ_ROWS, state_sc_conv[0],
                                e_sc_conv[0])
    s_sc = jnp.concatenate([state_sc_conv[0], tail_s[:, :ts]], axis=1)[:, ts:]
    o_sc = jnp.concatenate([_unpad_seq_rows(o_sc_s, bs, ts), pad_bf], axis=0)
    o_sc = jnp.concatenate([o_sc_p, o_sc], axis=0)
    x = matmul2_res(o_nsa, o_sc, e_w_out[0].astype(BF16), x)
    u_bf = peer_u.astype(BF16)
    vt_bf = peer_v.transpose(0, 2, 1).astype(BF16)
    peer_t = peer_layer(x, norm_w[0, 1], peer_wq[0], peer_keys[0], u_bf, vt_bf, 0)

    w_in = jnp.pad(o_w_in[0], ((0, 0), (0, _pad_cols(O_IN) - O_IN))).astype(BF16)
    proj, x = norm_matmul(x, norm_w[1, 0], w_in, delta_t=peer_t)
    odd_w = (o_m2_conv_w[0], o_m2_conv_b[0], o_m2_dt_bias[0], o_m2_a_log[0], o_m2_d[0], o_m2_norm[0])
    o_ret_p, p_ret = retention(proj, pos_p, bp, tp, CHUNK, jnp.zeros((bp, RET_HEADS, RET_DQK, RET_DV), F32),
                               o_ret_gn[0])
    o_ssd_p, p_ssm = ssd(proj, bp, tp, CHUNK, jnp.zeros((bp, M2_CONV - 1, M2_CONV_DIM), F32),
                         jnp.zeros((bp, M2_HEADS, M2_HEADDIM, M2_STATE), F32), *odd_w)
    xbc_cols = slice(O_SPLITS[4], O_SPLITS[5])
    p_m2c = jnp.stack([proj[(b + 1) * tp - (M2_CONV - 1):(b + 1) * tp, xbc_cols] for b in range(bp)])
    xbc_s = proj[s_rows, xbc_cols].reshape(bs, ts, M2_CONV_DIM)
    s_m2c = jnp.concatenate([state_m2_conv[0], xbc_s], axis=1)[:, ts:]
    proj_s = _pad_seq_rows(proj[s_rows], bs, ts)
    pos_s8 = past_len + jnp.arange(SEQ_ROWS, dtype=jnp.int32)
    o_ret_s, s_ret = retention(proj_s, pos_s8, bs, SEQ_ROWS, SEQ_ROWS, state_ret[0], o_ret_gn[0], valid=ts)
    o_ssd_s, s_ssm = ssd(proj_s, bs, SEQ_ROWS, SEQ_ROWS, state_m2_conv[0], state_ssm[0], *odd_w, valid=ts)
    o_ret = jnp.concatenate([o_ret_p, _unpad_seq_rows(o_ret_s, bs, ts), pad_bf], axis=0)
    o_ssd = jnp.concatenate([o_ssd_p, _unpad_seq_rows(o_ssd_s, bs, ts), pad_bf], axis=0)
    x = matmul2_res(o_ret, o_ssd, o_w_out[0].astype(BF16), x)
    peer_t = peer_layer(x, norm_w[1, 1], peer_wq[1], peer_keys[1], u_bf, vt_bf, 1)

    y = resid_rmsnorm(x, peer_t, final_norm)
    y_prompt = y[:n_p].reshape(bp, tp, dm)
    y_sample = y[s_rows].reshape(bs, ts, dm)
    return (y_prompt, y_sample, p_kv[None], p_win[None], p_sc[None], p_ret[None], p_ssm[None], p_m2c[None],
            s_kv[None], s_win[None], s_sc[None], s_ret[None], s_ssm[None], s_m2c[None])
```

```python
import functools
import math

import jax
import jax.numpy as jnp
from jax import lax
from jax.experimental import pallas as pl
from jax.experimental.pallas import tpu as pltpu

F32 = jnp.float32
BF16 = jnp.bfloat16

D_MODEL = 2048
DEPTH = 2
PAGE_SIZE = 128
NSA_HEAD_DIM = 128
NSA_HEADS = 8
NSA_KV_HEADS = 2
NSA_GROUP = 4
CMP_BLOCK = 32
CMP_STRIDE = 16
SLC_BLOCK = 64
SLC_TOPK = 16
WINDOW = 512
ROPE_THETA = 500000.0
ROPE_DIMS = 32
SC_WIDTH = 1024
SC_KSIZE = 3
RET_HEADS = 4
RET_DQK = 256
RET_DV = 256
RET_THETA = 10000.0
M2_DINNER = 1024
M2_HEADDIM = 64
M2_HEADS = 16
M2_STATE = 128
M2_GROUPS = 2
M2_CONV = 4
M2_CONV_DIM = M2_DINNER + 2 * M2_GROUPS * M2_STATE
PEER_HEADS = 8
PEER_KEYS = 128
PEER_QDIM = 256
PEER_TOPK = 16
Q_BLOCK = 128
CHUNK = 128
EPS = 1e-6

E_Q = NSA_HEADS * NSA_HEAD_DIM
E_KV = 6 * NSA_KV_HEADS * NSA_HEAD_DIM
E_G = 3 * NSA_HEADS
E_SC = 3 * SC_WIDTH
E_IN = E_Q + E_KV + E_G + E_SC
R_QK = RET_HEADS * RET_DQK
R_V = RET_HEADS * RET_DV
O_SPLITS = [R_QK, 2 * R_QK, 2 * R_QK + R_V, 2 * R_QK + 2 * R_V,
            2 * R_QK + 2 * R_V + M2_DINNER, 2 * R_QK + 2 * R_V + M2_DINNER + M2_CONV_DIM]
O_IN = O_SPLITS[-1] + M2_HEADS
SC_COL0 = E_Q + E_KV
GATE_COL0 = SC_COL0 + E_SC

LANES = 128
VMEM_LIMIT = 56 * 1024 * 1024
ROW_TILE_CAP = 1056
ROPE_ROWS_CAP = 544
NEG_INF = float("-inf")


def _round_up(n, m):
    return -(-n // m) * m


def _pick_tile(n, cap):
    best = LANES
    for t in range(LANES, cap + 1, LANES):
        if n % t == 0:
            best = t
    return best


def _pad_cols(m):
    return min((_round_up(m, t) for t in (768, 640, 512)))


def _row_tile(n, cap):
    best = None
    for t in range(16, cap + 1, 16):
        if n % t == 0:
            best = t
    assert best is not None
    return best


def _gelu_tanh(x):
    return 0.5 * x * (1.0 + jnp.tanh(math.sqrt(2.0 / math.pi) * (x + 0.044715 * (x * x * x))))


RESID_ROWS = 384


def _norm_matmul_kernel(*refs, has_delta):
    if has_delta:
        x_ref, dt_ref, g_ref, w_ref, o_ref, xsum_ref, xn_ref = refs
    else:
        x_ref, g_ref, w_ref, o_ref, xn_ref = refs

    @pl.when(pl.program_id(1) == 0)
    def _():
        x = x_ref[...]
        if has_delta:
            x = x + dt_ref[...].T
            xsum_ref[...] = x
        ms = jnp.mean(x * x, axis=-1, keepdims=True)
        xn_ref[...] = (x * lax.rsqrt(ms + EPS) * g_ref[...]).astype(BF16)

    o_ref[...] = jnp.dot(xn_ref[...], w_ref[...], preferred_element_type=F32)


def norm_matmul(x, gain, w_bf, delta_t=None):
    n, k = x.shape
    m = w_bf.shape[1]
    has_delta = delta_t is not None
    tm = _pick_tile(n, 2 * RESID_ROWS) if has_delta else _row_tile(n, ROW_TILE_CAP)
    tn = _pick_tile(m, RESID_ROWS if has_delta else 768)
    row = pl.BlockSpec((tm, k), lambda i, j: (i, 0))
    in_specs = [row] + ([pl.BlockSpec((k, tm), lambda i, j: (0, i))] if has_delta else []) + [
        pl.BlockSpec((1, k), lambda i, j: (0, 0)), pl.BlockSpec((k, tn), lambda i, j: (0, j))]
    out_specs = [pl.BlockSpec((tm, tn), lambda i, j: (i, j))] + ([row] if has_delta else [])
    out_shape = [jax.ShapeDtypeStruct((n, m), F32)] + ([jax.ShapeDtypeStruct((n, k), F32)] if has_delta else [])
    args = (x,) + ((delta_t,) if has_delta else ()) + (gain.reshape(1, k), w_bf)
    out = pl.pallas_call(
        functools.partial(_norm_matmul_kernel, has_delta=has_delta),
        grid=(n // tm, m // tn),
        in_specs=in_specs,
        out_specs=out_specs,
        out_shape=out_shape,
        scratch_shapes=[pltpu.VMEM((tm, k), BF16)],
        compiler_params=pltpu.CompilerParams(
            dimension_semantics=("parallel", "arbitrary"), vmem_limit_bytes=VMEM_LIMIT),
        name="norm_matmul",
    )(*args)
    return tuple(out) if has_delta else out[0]


def _matmul2_res_kernel(a1_ref, a2_ref, w1_ref, w2_ref, r_ref, o_ref):
    o_ref[...] = (r_ref[...] + jnp.dot(a1_ref[...], w1_ref[...], preferred_element_type=F32)
                  + jnp.dot(a2_ref[...], w2_ref[...], preferred_element_type=F32))


def matmul2_res(a1, a2, w_bf, res):
    n, k1 = a1.shape
    k2 = a2.shape[1]
    m = w_bf.shape[1]
    assert k1 == k2
    tm = _row_tile(n, ROW_TILE_CAP)
    tn = _pick_tile(m, 1024)
    return pl.pallas_call(
        _matmul2_res_kernel,
        grid=(n // tm, m // tn),
        in_specs=[pl.BlockSpec((tm, k1), lambda i, j: (i, 0)),
                  pl.BlockSpec((tm, k2), lambda i, j: (i, 0)),
                  pl.BlockSpec((k1, tn), lambda i, j: (0, j)),
                  pl.BlockSpec((k2, tn), lambda i, j: (1, j)),
                  pl.BlockSpec((tm, tn), lambda i, j: (i, j))],
        out_specs=pl.BlockSpec((tm, tn), lambda i, j: (i, j)),
        out_shape=jax.ShapeDtypeStruct((n, m), F32),
        compiler_params=pltpu.CompilerParams(
            dimension_semantics=("parallel", "parallel"), vmem_limit_bytes=VMEM_LIMIT),
        name="matmul2_res",
    )(a1, a2, w_bf, w_bf, res)


PEER_TOK = 128
PEER_HEAD_UNROLL = 4
PEER_TM = 768
PEER_TA = 8
PEER_TE = PEER_TA * PEER_KEYS


def _top_desc(work, count, with_rank=False, one_at_a_time=False):
    rows = []
    rank = jnp.full(work.shape, float(count), F32)
    row_id = lax.broadcasted_iota(jnp.int32, work.shape, 0).astype(F32)
    for r in range(count):
        m = jnp.max(work, axis=0, keepdims=True)
        rows.append(m)
        hit = work == m
        if one_at_a_time:
            hit = row_id == jnp.min(jnp.where(hit, row_id, float(work.shape[0])), axis=0, keepdims=True)
        if with_rank:
            rank = jnp.where(hit, float(r), rank)
        work = jnp.where(hit, NEG_INF, work)
    return (rows, rank) if with_rank else rows


def _peer_router_kernel(q_ref, k_ref, cnt_ref, gw_ref, r2_ref, w2_ref):
    def head(h, carry):
        q = q_ref[h]
        scores = []
        for side in range(2):
            qs = q[:, side * PEER_KEYS:(side + 1) * PEER_KEYS]
            qs = qs * lax.rsqrt(jnp.mean(qs * qs, axis=-1, keepdims=True) + EPS)
            scores.append(lax.dot_general(k_ref[h, side], qs, (((1,), (1,)), ((), ())),
                                          preferred_element_type=F32))
        s1, s2 = scores
        v1 = _top_desc(s1, PEER_TOPK + 1)
        v2, rank2 = _top_desc(s2, PEER_TOPK + 1, with_rank=True)
        v2_lo = jnp.concatenate(v2[:8], axis=0)
        v2_hi = jnp.concatenate(v2[8:16], axis=0)
        row = lax.broadcasted_iota(jnp.int32, v2_lo.shape, 0)
        blocks = [v1[0] + v2_lo, v1[0] + v2_hi, v1[1] + v2_lo]
        for a, lim in ((2, 5), (3, 4), (4, 3), (5, 2), (6, 2), (7, 2)):
            blocks.append(jnp.where(row < lim, v1[a] + v2_lo, NEG_INF))
        blocks.append(jnp.concatenate(v1[8:16], axis=0) + v2[0])
        extra = jnp.where(row == 0, v1[0] + v2[16], jnp.where(row == 1, v1[16] + v2[0], NEG_INF))
        blocks.append(extra)
        cand = jnp.concatenate(blocks, axis=0)
        tops = _top_desc(cand, PEER_TOPK + 1, one_at_a_time=True)
        z = jnp.zeros_like(tops[0])
        for r in range(PEER_TOPK):
            z = z + jnp.exp(tops[r] - tops[0])
        tau = 0.5 * (tops[PEER_TOPK - 1] + tops[PEER_TOPK])
        count = jnp.zeros_like(s1)
        for r in range(PEER_TOPK):
            count = count + jnp.where(s1 >= tau - v2[r], 1.0, 0.0)
        cnt_ref[h] = count
        gw_ref[h] = jnp.exp(s1 - v1[0]) / z
        r2_ref[h] = rank2.astype(BF16)
        w2_ref[h] = jnp.exp(s2 - v2[0]).astype(BF16)
        return carry

    lax.fori_loop(0, PEER_HEADS, head, 0, unroll=PEER_HEAD_UNROLL)


def peer_router(q_hm, keys):
    n = q_hm.shape[1]
    out = jax.ShapeDtypeStruct((PEER_HEADS, PEER_KEYS, n), F32)
    out_bf = jax.ShapeDtypeStruct((PEER_HEADS, PEER_KEYS, n), BF16)
    spec = pl.BlockSpec((PEER_HEADS, PEER_KEYS, PEER_TOK), lambda i: (0, 0, i))
    return pl.pallas_call(
        _peer_router_kernel,
        grid=(n // PEER_TOK,),
        in_specs=[pl.BlockSpec((PEER_HEADS, PEER_TOK, PEER_QDIM), lambda i: (0, i, 0)),
                  pl.BlockSpec((PEER_HEADS, 2, PEER_KEYS, PEER_QDIM // 2), lambda i: (0, 0, 0, 0))],
        out_specs=[spec, spec, spec, spec],
        out_shape=[out, out, out_bf, out_bf],
        compiler_params=pltpu.CompilerParams(dimension_semantics=("parallel",), vmem_limit_bytes=VMEM_LIMIT),
        name="peer_router",
    )(q_hm, keys)


def _peer_expert_kernel(xt_ref, u_ref, vt_ref, cnt_ref, gw_ref, r2_ref, w2_ref, o_ref, s_scr, hg_scr):
    e = pl.program_id(1)
    s_scr[...] = jnp.dot(u_ref[...], xt_ref[...], preferred_element_type=F32)
    zero = jnp.zeros((), BF16)
    for al in range(PEER_TA):
        rows = slice(al * PEER_KEYS, (al + 1) * PEER_KEYS)
        g = None
        for h in range(PEER_HEADS):
            cnt = cnt_ref[h, al:al + 1, :].astype(BF16)
            gate = gw_ref[h, al:al + 1, :].astype(BF16)
            t = jnp.where(r2_ref[h] < cnt, w2_ref[h], zero) * gate
            g = t if g is None else g + t
        hg_scr[rows, :] = _gelu_tanh(s_scr[rows, :]).astype(BF16) * g
    part = jnp.dot(vt_ref[...], hg_scr[...], preferred_element_type=F32)

    @pl.when(e == 0)
    def _():
        o_ref[...] = part

    @pl.when(e != 0)
    def _():
        o_ref[...] += part


def peer_experts(xt_bf, u_bf, vt_bf, layer, cnt, gw, r2, w2):
    d, n = xt_bf.shape
    n_exp = u_bf.shape[1]
    tm = PEER_TM
    sel_spec = pl.BlockSpec((PEER_HEADS, PEER_TA, tm), lambda i, e: (0, e, i))
    all_spec = pl.BlockSpec((PEER_HEADS, PEER_KEYS, tm), lambda i, e: (0, 0, i))
    return pl.pallas_call(
        _peer_expert_kernel,
        grid=(n // tm, n_exp // PEER_TE),
        in_specs=[pl.BlockSpec((d, tm), lambda i, e: (0, i)),
                  pl.BlockSpec((None, PEER_TE, d), lambda i, e: (layer, e, 0)),
                  pl.BlockSpec((None, d, PEER_TE), lambda i, e: (layer, 0, e)),
                  sel_spec, sel_spec, all_spec, all_spec],
        out_specs=pl.BlockSpec((d, tm), lambda i, e: (0, i)),
        out_shape=jax.ShapeDtypeStruct((d, n), F32),
        scratch_shapes=[pltpu.VMEM((PEER_TE, tm), F32), pltpu.VMEM((PEER_TE, tm), BF16)],
        compiler_params=pltpu.CompilerParams(
            dimension_semantics=("parallel", "arbitrary"), vmem_limit_bytes=VMEM_LIMIT),
        name="peer_experts",
    )(xt_bf, u_bf, vt_bf, cnt, gw, r2, w2)


def _norm_matmul_t_kernel(x_ref, g_ref, w_ref, o_ref, xt_ref, xn_ref):
    @pl.when(pl.program_id(1) == 0)
    def _():
        x = x_ref[...]
        ms = jnp.mean(x * x, axis=-1, keepdims=True)
        xn = x * lax.rsqrt(ms + EPS) * g_ref[...]
        xn_ref[...] = xn.astype(BF16)
        xt_ref[...] = xn.T.astype(BF16)

    o_ref[...] = jnp.dot(xn_ref[...], w_ref[...], preferred_element_type=F32).reshape(o_ref.shape)


def norm_matmul_t(x, gain, w_bf, cols):
    n, k = x.shape
    m = w_bf.shape[1]
    tm = PEER_TM
    return pl.pallas_call(
        _norm_matmul_t_kernel,
        grid=(n // tm, m // cols),
        in_specs=[pl.BlockSpec((tm, k), lambda i, j: (i, 0)),
                  pl.BlockSpec((1, k), lambda i, j: (0, 0)),
                  pl.BlockSpec((k, cols), lambda i, j: (0, j))],
        out_specs=[pl.BlockSpec((1, tm, cols), lambda i, j: (j, i, 0)),
                   pl.BlockSpec((k, tm), lambda i, j: (0, i))],
        out_shape=[jax.ShapeDtypeStruct((m // cols, n, cols), F32), jax.ShapeDtypeStruct((k, n), BF16)],
        scratch_shapes=[pltpu.VMEM((tm, k), BF16)],
        compiler_params=pltpu.CompilerParams(
            dimension_semantics=("parallel", "arbitrary"), vmem_limit_bytes=VMEM_LIMIT),
        name="norm_matmul_t",
    )(x, gain.reshape(1, k), w_bf)


def _resid_rmsnorm_kernel(x_ref, dt_ref, g_ref, o_ref):
    x = x_ref[...] + dt_ref[...].T
    ms = jnp.mean(x * x, axis=-1, keepdims=True)
    o_ref[...] = x * lax.rsqrt(ms + EPS) * g_ref[...]


def resid_rmsnorm(x, delta_t, gain):
    n, k = x.shape
    tm = _pick_tile(n, 2 * RESID_ROWS)
    return pl.pallas_call(
        _resid_rmsnorm_kernel,
        grid=(n // tm,),
        in_specs=[pl.BlockSpec((tm, k), lambda i: (i, 0)), pl.BlockSpec((k, tm), lambda i: (0, i)),
                  pl.BlockSpec((1, k), lambda i: (0, 0))],
        out_specs=pl.BlockSpec((tm, k), lambda i: (i, 0)),
        out_shape=jax.ShapeDtypeStruct((n, k), F32),
        compiler_params=pltpu.CompilerParams(dimension_semantics=("parallel",), vmem_limit_bytes=VMEM_LIMIT),
        name="resid_rmsnorm",
    )(x, delta_t, gain.reshape(1, k))


def peer_layer(x, gain, w_q, keys, u_bf, vt_bf, layer):
    q_hm, xt_bf = norm_matmul_t(x, gain, w_q.astype(BF16), PEER_QDIM)
    cnt, gw, r2, w2 = peer_router(q_hm, keys)
    return peer_experts(xt_bf, u_bf, vt_bf, layer, cnt, gw, r2, w2)


def _rope_tables(pos):
    half = ROPE_DIMS // 2
    inv = ROPE_THETA ** (-jnp.arange(half, dtype=F32) / half)
    ang = pos.astype(F32)[:, None] * inv[None, :]
    cos, sin = jnp.cos(ang), jnp.sin(ang)
    t = pos.shape[0]
    ones = jnp.ones((t, NSA_HEAD_DIM - ROPE_DIMS), F32)
    zeros = jnp.zeros((t, NSA_HEAD_DIM - ROPE_DIMS), F32)
    zh = jnp.zeros((t, half), F32)
    c = jnp.concatenate([cos, cos, ones], axis=1)
    s_lo = jnp.concatenate([-sin, zh, zeros], axis=1)
    s_hi = jnp.concatenate([zh, sin, zeros], axis=1)
    return c, s_lo, s_hi


def _rope_kernel(p_ref, c_ref, sl_ref, sh_ref, q_ref, rows_ref, win_ref, rows_bf_ref, win_bf_ref):
    c, sl, sh = c_ref[...], sl_ref[...], sh_ref[...]
    half = ROPE_DIMS // 2

    def rot(x):
        return x * c + pltpu.roll(x, LANES - half, 1) * sl + pltpu.roll(x, half, 1) * sh

    d = NSA_HEAD_DIM
    scale = d ** -0.5
    for hd in range(NSA_HEADS):
        q_ref[:, hd * d:(hd + 1) * d] = (rot(p_ref[:, hd * d:(hd + 1) * d]) * scale).astype(BF16)
    for blk in range(12):
        x = p_ref[:, E_Q + blk * d:E_Q + (blk + 1) * d]
        if (blk // 2) % 2 == 0:
            x = rot(x)
        if blk < 8:
            rows_ref[:, blk * d:(blk + 1) * d] = x
            rows_bf_ref[:, blk * d:(blk + 1) * d] = x.astype(BF16)
        else:
            win_ref[:, (blk - 8) * d:(blk - 7) * d] = x
            win_bf_ref[:, (blk - 8) * d:(blk - 7) * d] = x.astype(BF16)


def nsa_rope(proj, pos_rows):
    n = proj.shape[0]
    tm = _row_tile(n, ROPE_ROWS_CAP)
    c, sl, sh = _rope_tables(pos_rows)
    width = E_Q + E_KV
    tab = pl.BlockSpec((tm, LANES), lambda i: (i, 0))
    return pl.pallas_call(
        _rope_kernel,
        grid=(n // tm,),
        in_specs=[pl.BlockSpec((tm, width), lambda i: (i, 0)), tab, tab, tab],
        out_specs=[pl.BlockSpec((tm, E_Q), lambda i: (i, 0)),
                   pl.BlockSpec((tm, 1024), lambda i: (i, 0)),
                   pl.BlockSpec((tm, 512), lambda i: (i, 0)),
                   pl.BlockSpec((tm, 1024), lambda i: (i, 0)),
                   pl.BlockSpec((tm, 512), lambda i: (i, 0))],
        out_shape=[jax.ShapeDtypeStruct((n, E_Q), BF16),
                   jax.ShapeDtypeStruct((n, 1024), F32),
                   jax.ShapeDtypeStruct((n, 512), F32),
                   jax.ShapeDtypeStruct((n, 1024), BF16),
                   jax.ShapeDtypeStruct((n, 512), BF16)],
        compiler_params=pltpu.CompilerParams(dimension_semantics=("parallel",), vmem_limit_bytes=VMEM_LIMIT),
        name="nsa_rope",
    )(proj, c, sl, sh)


def _compress_kernel(x_ref, pe_ref, w1_ref, w2_ref, o_ref):
    n_sub = x_ref.shape[0] // CMP_STRIDE
    acc0 = jnp.zeros((n_sub, NSA_HEAD_DIM), F32)
    acc1 = jnp.zeros((n_sub, NSA_HEAD_DIM), F32)
    for s in range(CMP_STRIDE):
        xs = x_ref[pl.ds(s, n_sub, stride=CMP_STRIDE), :]
        a0 = (xs + pe_ref[0, s:s + 1, :]).astype(BF16)
        a1 = (xs + pe_ref[0, CMP_STRIDE + s:CMP_STRIDE + s + 1, :]).astype(BF16)
        acc0 = acc0 + jnp.dot(a0, w1_ref[0, s], preferred_element_type=F32)
        acc1 = acc1 + jnp.dot(a1, w1_ref[0, CMP_STRIDE + s], preferred_element_type=F32)
    pre = acc0 + pltpu.roll(acc1, n_sub - 1, 0)
    o_ref[0, 0] = jnp.dot(_gelu_tanh(pre).astype(BF16), w2_ref[0], preferred_element_type=F32).astype(BF16)


def nsa_compress(rows, n_batch, t_len, cmp_pe, cmp_w1, cmp_w2):
    n_sub = t_len // CMP_STRIDE
    d = NSA_HEAD_DIM
    w1 = cmp_w1.reshape(2, CMP_BLOCK, d, d).astype(BF16)
    return pl.pallas_call(
        _compress_kernel,
        grid=(n_batch, 4),
        in_specs=[pl.BlockSpec((t_len, d), lambda b, c: (b, c)),
                  pl.BlockSpec((1, CMP_BLOCK, d), lambda b, c: (c // 2, 0, 0)),
                  pl.BlockSpec((1, CMP_BLOCK, d, d), lambda b, c: (c // 2, 0, 0, 0)),
                  pl.BlockSpec((1, d, d), lambda b, c: (c // 2, 0, 0))],
        out_specs=pl.BlockSpec((1, 1, n_sub, d), lambda b, c: (b, c, 0, 0)),
        out_shape=jax.ShapeDtypeStruct((n_batch, 4, n_sub, d), BF16),
        compiler_params=pltpu.CompilerParams(
            dimension_semantics=("parallel", "parallel"), vmem_limit_bytes=VMEM_LIMIT),
        name="nsa_compress",
    )(rows, cmp_pe, w1, cmp_w2.astype(BF16))


NSA_TK = 512
NSA_WTILES = WINDOW // Q_BLOCK + 1


def _masked_softmax_rows(s, mask):
    s = jnp.where(mask, s, -1e30)
    p = jnp.exp(s - jnp.max(s, axis=-1, keepdims=True))
    p = jnp.where(mask, p, 0.0)
    return p / jnp.maximum(jnp.sum(p, axis=-1, keepdims=True), 1e-30)


def _nsa_prompt_kernel(q_ref, kc_ref, vc_ref, ks_ref, vs_ref, kw_ref, vw_ref, gate_ref, o_ref):
    g = pl.program_id(1)
    qi = pl.program_id(2)
    d, r, qb = NSA_HEAD_DIM, NSA_GROUP, Q_BLOCK
    n_cmp = kc_ref.shape[2]
    n_slc = ks_ref.shape[0] // SLC_BLOCK
    nt = (((1,), (1,)), ((), ()))
    q = jnp.concatenate([q_ref[:, h * d:(h + 1) * d] for h in range(r)], axis=0)
    t_col = qi * qb + lax.broadcasted_iota(jnp.int32, (qb, 1), 0)

    s = lax.dot_general(q, kc_ref[0, 0], nt, preferred_element_type=F32).reshape(r, qb, n_cmp)
    n_idx = lax.broadcasted_iota(jnp.int32, (qb, n_cmp), 1)
    c_mask = (n_idx * CMP_STRIDE + (CMP_BLOCK - 1) <= t_col)[None]
    p_c = _masked_softmax_rows(s, c_mask)
    o_cmp = jnp.dot(p_c.reshape(r * qb, n_cmp).astype(BF16), vc_ref[0, 0], preferred_element_type=F32)

    nb = LANES
    assert n_slc <= nb and qb == LANES
    p_sum = jnp.sum(p_c, axis=0)
    si = lax.broadcasted_iota(jnp.int32, (nb, n_cmp), 0) * SLC_BLOCK
    ci = lax.broadcasted_iota(jnp.int32, (nb, n_cmp), 1) * CMP_STRIDE
    overlap_t = jnp.where((ci < si + SLC_BLOCK) & (ci + CMP_BLOCK > si), 1.0, 0.0).astype(BF16)
    p_hi = p_sum.astype(BF16)
    p_lo = (p_sum - p_hi.astype(F32)).astype(BF16)
    imp_t = (lax.dot_general(overlap_t, p_hi, nt, preferred_element_type=F32)
             + lax.dot_general(overlap_t, p_lo, nt, preferred_element_type=F32))
    blk = lax.broadcasted_iota(jnp.int32, (nb, qb), 0)
    blk_f = blk.astype(F32)
    t_row = qi * qb + lax.broadcasted_iota(jnp.int32, (1, qb), 1)
    cur = t_row // SLC_BLOCK
    forced = (blk == 0) | (blk == cur) | (blk == cur - 1)
    work = jnp.where(blk * SLC_BLOCK <= t_row, jnp.where(forced, 1e6, imp_t), -1e6)
    work = jnp.where(blk < n_slc, work, NEG_INF)
    sel_t = jnp.zeros((nb, qb), F32)
    for _ in range(min(SLC_TOPK, n_slc)):
        m = jnp.max(work, axis=0, keepdims=True)
        first = jnp.min(jnp.where(work == m, blk_f, float(nb)), axis=0, keepdims=True)
        pick = blk_f == first
        sel_t = jnp.where(pick, 1.0, sel_t)
        work = jnp.where(pick, NEG_INF, work)
    sel_bf = sel_t.T.astype(BF16)

    tk = NSA_TK
    bpt = tk // SLC_BLOCK

    def slc_step(kt, carry):
        m_run, l_run, acc = carry
        start = pl.multiple_of(kt * tk, tk)
        k = ks_ref[pl.ds(start, tk), :]
        v = vs_ref[pl.ds(start, tk), :]
        s = lax.dot_general(q, k, nt, preferred_element_type=F32).reshape(r, qb, tk)
        ei = lax.broadcasted_iota(jnp.int32, (nb, tk), 0)
        ej = lax.broadcasted_iota(jnp.int32, (nb, tk), 1)
        expand = jnp.where(ei == kt * bpt + ej // SLC_BLOCK, 1.0, 0.0).astype(BF16)
        picked = jnp.dot(sel_bf, expand, preferred_element_type=F32)
        kpos = start + lax.broadcasted_iota(jnp.int32, (qb, tk), 1)
        mask = ((picked > 0.5) & (kpos <= t_col))[None]
        s = jnp.where(mask, s, -1e30)
        m_new = jnp.maximum(m_run, jnp.max(s, axis=-1, keepdims=True))
        alpha = jnp.exp(m_run - m_new)
        p = jnp.where(mask, jnp.exp(s - m_new), 0.0)
        l_new = alpha * l_run + jnp.sum(p, axis=-1, keepdims=True)
        pv = jnp.dot(p.reshape(r * qb, tk).astype(BF16), v, preferred_element_type=F32)
        acc = alpha.reshape(r * qb, 1) * acc + pv
        return m_new, l_new, acc

    init = (jnp.full((r, qb, 1), -1e30, F32), jnp.zeros((r, qb, 1), F32), jnp.zeros((r * qb, d), F32))
    _, l_fin, acc = lax.fori_loop(0, (qi * qb) // tk + 1, slc_step, init)
    o_slc = acc / jnp.maximum(l_fin.reshape(r * qb, 1), 1e-30)

    k_tiles, v_tiles, pos_tiles = [], [], []
    for j in range(NSA_WTILES):
        kt = qi - (NSA_WTILES - 1) + j
        ktc = jnp.maximum(kt, 0)
        start = pl.multiple_of(ktc * qb, qb)
        k_tiles.append(kw_ref[pl.ds(start, qb), :])
        v_tiles.append(vw_ref[pl.ds(start, qb), :])
        lane = lax.broadcasted_iota(jnp.int32, (qb, qb), 1)
        pos_tiles.append(jnp.where(kt >= 0, start + lane, -1))
    k_w = jnp.concatenate(k_tiles, axis=0)
    v_w = jnp.concatenate(v_tiles, axis=0)
    k_pos = jnp.concatenate(pos_tiles, axis=1)
    span = NSA_WTILES * qb
    s = lax.dot_general(q, k_w, nt, preferred_element_type=F32).reshape(r, qb, span)
    dpos = t_col - k_pos
    w_mask = ((dpos >= 0) & (dpos < WINDOW) & (k_pos >= 0))[None]
    p_w = _masked_softmax_rows(s, w_mask)
    o_win = jnp.dot(p_w.reshape(r * qb, span).astype(BF16), v_w, preferred_element_type=F32)

    sig = jax.nn.sigmoid(gate_ref[...])
    lane = lax.broadcasted_iota(jnp.int32, sig.shape, 1)
    for h in range(r):
        rows = slice(h * qb, (h + 1) * qb)
        out = jnp.zeros((qb, d), F32)
        for branch, o_b in enumerate((o_cmp, o_slc, o_win)):
            col = branch * NSA_HEADS + g * r + h
            gate = jnp.sum(jnp.where(lane == col, sig, 0.0), axis=-1, keepdims=True)
            out = out + gate * o_b[rows]
        o_ref[:, h * d:(h + 1) * d] = out.astype(o_ref.dtype)


def nsa_prompt(q_bf, kvc, rows_bf, win_bf, proj, n_batch, t_len):
    d, r, qb = NSA_HEAD_DIM, NSA_GROUP, Q_BLOCK
    nqb = t_len // qb
    n_sub = kvc.shape[2]
    gate_blk = GATE_COL0 // LANES
    seq = lambda col: pl.BlockSpec((t_len, d), lambda b, g, i: (b, col(g)))
    return pl.pallas_call(
        _nsa_prompt_kernel,
        grid=(n_batch, NSA_KV_HEADS, nqb),
        in_specs=[pl.BlockSpec((qb, r * d), lambda b, g, i: (b * nqb + i, g)),
                  pl.BlockSpec((1, 1, n_sub, d), lambda b, g, i: (b, g, 0, 0)),
                  pl.BlockSpec((1, 1, n_sub, d), lambda b, g, i: (b, 2 + g, 0, 0)),
                  seq(lambda g: 4 + g), seq(lambda g: 6 + g),
                  seq(lambda g: g), seq(lambda g: 2 + g),
                  pl.BlockSpec((qb, LANES), lambda b, g, i: (b * nqb + i, gate_blk))],
        out_specs=pl.BlockSpec((qb, r * d), lambda b, g, i: (b * nqb + i, g)),
        out_shape=jax.ShapeDtypeStruct((n_batch * t_len, E_Q), BF16),
        compiler_params=pltpu.CompilerParams(
            dimension_semantics=("parallel", "parallel", "arbitrary"), vmem_limit_bytes=VMEM_LIMIT),
        name="nsa_prompt",
    )(q_bf, kvc, kvc, rows_bf, rows_bf, win_bf, win_bf, proj)


SAMPLE_PAGES = 16
NEW_ROWS_BLK = 16


def _page_specs(kind):
    def spec(i):
        return pl.BlockSpec((1, PAGE_SIZE, 1, NSA_KV_HEADS, NSA_HEAD_DIM),
                            lambda s, c, pt: (pt[s, c * SAMPLE_PAGES + i], 0, kind, 0, 0))
    return [spec(i) for i in range(SAMPLE_PAGES)]


def _sample_compress_kernel(pt_ref, *refs):
    n_in = 2 * SAMPLE_PAGES
    pages = refs[:n_in]
    pe_ref, w1_ref, a0_ref, a1_ref = refs[n_in:]
    d = NSA_HEAD_DIM
    per_page = PAGE_SIZE // CMP_STRIDE
    rows = SAMPLE_PAGES * per_page
    gs = NSA_KV_HEADS * CMP_STRIDE
    for kv in range(2):
        by_offset = [jnp.swapaxes(pages[kv * SAMPLE_PAGES + p].reshape(NSA_KV_HEADS * PAGE_SIZE, d)[...]
                                  .reshape(per_page, gs, d), 0, 1) for p in range(SAMPLE_PAGES)]
        acc0 = jnp.zeros((NSA_KV_HEADS * rows, d), F32)
        acc1 = jnp.zeros((NSA_KV_HEADS * rows, d), F32)
        for s in range(CMP_STRIDE):
            xs = jnp.concatenate([by_offset[p][NSA_KV_HEADS * s + g]
                                  for g in range(NSA_KV_HEADS) for p in range(SAMPLE_PAGES)],
                                 axis=0)
            a0 = (xs + pe_ref[kv, s:s + 1, :]).astype(BF16)
            a1 = (xs + pe_ref[kv, CMP_STRIDE + s:CMP_STRIDE + s + 1, :]).astype(BF16)
            acc0 = acc0 + jnp.dot(a0, w1_ref[kv, s], preferred_element_type=F32)
            acc1 = acc1 + jnp.dot(a1, w1_ref[kv, CMP_STRIDE + s], preferred_element_type=F32)
        for g in range(NSA_KV_HEADS):
            a0_ref[0, kv * NSA_KV_HEADS + g] = acc0[g * rows:(g + 1) * rows]
            a1_ref[0, kv * NSA_KV_HEADS + g] = acc1[g * rows:(g + 1) * rows]


def nsa_sample_compress(cache, page_table, cmp_pe, cmp_w1):
    bs, n_pages = page_table.shape
    d = NSA_HEAD_DIM
    per_page = PAGE_SIZE // CMP_STRIDE
    n_sub = n_pages * per_page
    rows = SAMPLE_PAGES * per_page
    w1 = cmp_w1.reshape(2, CMP_BLOCK, d, d).astype(BF16)
    out = jax.ShapeDtypeStruct((bs, 4, n_sub, d), F32)
    ospec = pl.BlockSpec((1, 4, rows, d), lambda s, c, pt: (s, 0, c, 0))
    page_specs = _page_specs(0) + _page_specs(1)
    return pl.pallas_call(
        _sample_compress_kernel,
        grid_spec=pltpu.PrefetchScalarGridSpec(
            num_scalar_prefetch=1,
            grid=(bs, n_pages // SAMPLE_PAGES),
            in_specs=page_specs + [
                pl.BlockSpec((2, CMP_BLOCK, d), lambda s, c, pt: (0, 0, 0)),
                pl.BlockSpec((2, CMP_BLOCK, d, d), lambda s, c, pt: (0, 0, 0, 0))],
            out_specs=[ospec, ospec]),
        out_shape=[out, out],
        compiler_params=pltpu.CompilerParams(
            dimension_semantics=("parallel", "arbitrary"), vmem_limit_bytes=VMEM_LIMIT),
        name="nsa_sample_compress",
    )(page_table, *([cache] * (2 * SAMPLE_PAGES)), cmp_pe, w1)


def _sample_select_kernel(a0_ref, a1_ref, w2_ref, q_ref, ocmp_ref, sel_ref, *, past_len, t_len):
    d, r = NSA_HEAD_DIM, NSA_GROUP
    n_sub = a0_ref.shape[2]
    n_cmp = n_sub - 1
    n_slc = -(-(past_len + t_len) // SLC_BLOCK)
    lanes = sel_ref.shape[3]
    nt = (((1,), (1,)), ((), ()))
    rq = r * t_len
    for g in range(NSA_KV_HEADS):
        kc = jnp.dot(_gelu_tanh(a0_ref[0, g] + pltpu.roll(a1_ref[0, g], n_sub - 1, 0)).astype(BF16), w2_ref[0],
                     preferred_element_type=F32).astype(BF16)
        vc = jnp.dot(_gelu_tanh(a0_ref[0, 2 + g] + pltpu.roll(a1_ref[0, 2 + g], n_sub - 1, 0)).astype(BF16),
                     w2_ref[1], preferred_element_type=F32).astype(BF16)
        q = q_ref[0, g]
        s = lax.dot_general(q, kc, nt, preferred_element_type=F32)
        pos = past_len + lax.broadcasted_iota(jnp.int32, (rq, 1), 0) % t_len
        n_idx = lax.broadcasted_iota(jnp.int32, (rq, n_sub), 1)
        p_c = _masked_softmax_rows(s, (n_idx * CMP_STRIDE + (CMP_BLOCK - 1) <= pos) & (n_idx < n_cmp))
        ocmp_ref[0, g] = jnp.dot(p_c.astype(BF16), vc, preferred_element_type=F32)
        ri = lax.broadcasted_iota(jnp.int32, (8, rq), 0)
        cj = lax.broadcasted_iota(jnp.int32, (8, rq), 1)
        head_sum = jnp.where(cj % t_len == ri, 1.0, 0.0).astype(BF16)
        p_hi = p_c.astype(BF16)
        p_lo = (p_c - p_hi.astype(F32)).astype(BF16)
        p_sum = (jnp.dot(head_sum, p_hi, preferred_element_type=F32)
                 + jnp.dot(head_sum, p_lo, preferred_element_type=F32))
        ci = lax.broadcasted_iota(jnp.int32, (n_sub, lanes), 0)
        mi = lax.broadcasted_iota(jnp.int32, (n_sub, lanes), 1)
        overlap = jnp.where((ci * CMP_STRIDE < mi * SLC_BLOCK + SLC_BLOCK)
                            & (ci * CMP_STRIDE + CMP_BLOCK > mi * SLC_BLOCK) & (ci < n_cmp), 1.0, 0.0).astype(BF16)
        s_hi = p_sum.astype(BF16)
        s_lo = (p_sum - s_hi.astype(F32)).astype(BF16)
        imp = (jnp.dot(s_hi, overlap, preferred_element_type=F32)
               + jnp.dot(s_lo, overlap, preferred_element_type=F32))
        blk = lax.broadcasted_iota(jnp.int32, (8, lanes), 1)
        blk_f = blk.astype(F32)
        tpos = past_len + lax.broadcasted_iota(jnp.int32, (8, 1), 0) % t_len
        cur = tpos // SLC_BLOCK
        forced = (blk == 0) | (blk == cur) | (blk == cur - 1)
        work = jnp.where(blk * SLC_BLOCK <= tpos, jnp.where(forced, 1e6, imp), -1e6)
        work = jnp.where(blk < n_slc, work, NEG_INF)
        sel = jnp.zeros((8, lanes), F32)
        for _ in range(min(SLC_TOPK, n_slc)):
            m = jnp.max(work, axis=-1, keepdims=True)
            first = jnp.min(jnp.where(work == m, blk_f, float(lanes)), axis=-1, keepdims=True)
            pick = blk_f == first
            sel = jnp.where(pick, 1.0, sel)
            work = jnp.where(pick, NEG_INF, work)
        sel_ref[0, g] = sel


def nsa_sample_select(a0, a1, cmp_w2, q_s, past_len, t_len):
    bs, _, n_sub, d = a0.shape
    n_slc = -(-(past_len + t_len) // SLC_BLOCK)
    lanes = _round_up(n_slc, LANES)
    rq = NSA_GROUP * t_len
    aspec = pl.BlockSpec((1, 4, n_sub, d), lambda s: (s, 0, 0, 0))
    return pl.pallas_call(
        functools.partial(_sample_select_kernel, past_len=past_len, t_len=t_len),
        grid=(bs,),
        in_specs=[aspec, aspec, pl.BlockSpec((2, d, d), lambda s: (0, 0, 0)),
                  pl.BlockSpec((1, NSA_KV_HEADS, rq, d), lambda s: (s, 0, 0, 0))],
        out_specs=[pl.BlockSpec((1, NSA_KV_HEADS, rq, d), lambda s: (s, 0, 0, 0)),
                   pl.BlockSpec((1, NSA_KV_HEADS, 8, lanes), lambda s: (s, 0, 0, 0))],
        out_shape=[jax.ShapeDtypeStruct((bs, NSA_KV_HEADS, rq, d), F32),
                   jax.ShapeDtypeStruct((bs, NSA_KV_HEADS, 8, lanes), F32)],
        compiler_params=pltpu.CompilerParams(dimension_semantics=("parallel",), vmem_limit_bytes=VMEM_LIMIT),
        name="nsa_sample_select",
    )(a0, a1, cmp_w2.astype(BF16), q_s)


def _sample_attend_kernel(pt_ref, *refs, past_len, t_len, row0):
    k_pages = refs[:SAMPLE_PAGES]
    v_pages = refs[SAMPLE_PAGES:2 * SAMPLE_PAGES]
    (q_ref, sel_ref, ocmp_ref, newrows_ref, kwin_ref, vwin_ref, newwin_ref, gate_ref, o_ref,
     m_scr, l_scr, acc_scr) = refs[2 * SAMPLE_PAGES:]
    s_idx = pl.program_id(0)
    c = pl.program_id(1)
    d, r = NSA_HEAD_DIM, NSA_GROUP
    rq = r * t_len
    lanes = sel_ref.shape[3]
    tk = SAMPLE_PAGES * PAGE_SIZE
    nt = (((1,), (1,)), ((), ()))

    @pl.when(c == 0)
    def _():
        m_scr[...] = jnp.full(m_scr.shape, -1e30, F32)
        l_scr[...] = jnp.zeros(l_scr.shape, F32)
        acc_scr[...] = jnp.zeros(acc_scr.shape, F32)

    row = lax.broadcasted_iota(jnp.int32, (rq, 1), 0)
    t_row = row % t_len
    pos = past_len + t_row
    ti = lax.broadcasted_iota(jnp.int32, (rq, 8), 1)
    tok_expand = jnp.where(ti == t_row, 1.0, 0.0).astype(BF16)

    def online(g, s, mask, v):
        s = jnp.where(mask, s, -1e30)
        m_old = m_scr[g]
        m_new = jnp.maximum(m_old, jnp.max(s, axis=-1, keepdims=True))
        alpha = jnp.exp(m_old - m_new)
        p = jnp.where(mask, jnp.exp(s - m_new), 0.0)
        l_scr[g] = alpha * l_scr[g] + jnp.sum(p, axis=-1, keepdims=True)
        acc_scr[g] = alpha * acc_scr[g] + jnp.dot(p.astype(BF16), v, preferred_element_type=F32)
        m_scr[g] = m_new

    sel16 = []
    for g in range(NSA_KV_HEADS):
        sel16.append(jnp.dot(tok_expand, sel_ref[0, g].astype(BF16), preferred_element_type=F32))
        k = jnp.concatenate([pg.reshape(NSA_KV_HEADS * PAGE_SIZE, d)[pl.ds(g, PAGE_SIZE, stride=NSA_KV_HEADS), :]
                             for pg in k_pages], axis=0)
        v = jnp.concatenate([pg.reshape(NSA_KV_HEADS * PAGE_SIZE, d)[pl.ds(g, PAGE_SIZE, stride=NSA_KV_HEADS), :]
                             for pg in v_pages], axis=0)
        q = q_ref[0, g]
        s = lax.dot_general(q, k.astype(BF16), nt, preferred_element_type=F32)
        ei = lax.broadcasted_iota(jnp.int32, (lanes, tk), 0)
        ej = lax.broadcasted_iota(jnp.int32, (lanes, tk), 1)
        expand = jnp.where(ei == c * (tk // SLC_BLOCK) + ej // SLC_BLOCK, 1.0, 0.0).astype(BF16)
        picked = jnp.dot(sel16[g].astype(BF16), expand, preferred_element_type=F32)
        kpos = c * tk + lax.broadcasted_iota(jnp.int32, (rq, tk), 1)
        online(g, s, (picked > 0.5) & (kpos <= pos), v.astype(BF16))

    @pl.when(c == pl.num_programs(1) - 1)
    def _():
        mine = (row0 // t_len + s_idx) % (NEW_ROWS_BLK // t_len)
        j = lax.broadcasted_iota(jnp.int32, (rq, NEW_ROWS_BLK), 1)
        own = j // t_len == mine
        new_pos = past_len + j % t_len
        sig = jax.nn.sigmoid(gate_ref[...])
        gsel = jnp.where(j == mine * t_len + t_row, 1.0, 0.0).astype(BF16)
        s_hi = sig.astype(BF16)
        s_lo = (sig - s_hi.astype(F32)).astype(BF16)
        sig_rows = (jnp.dot(gsel, s_hi, preferred_element_type=F32)
                    + jnp.dot(gsel, s_lo, preferred_element_type=F32))
        lane = lax.broadcasted_iota(jnp.int32, sig_rows.shape, 1)
        lane_l = lax.broadcasted_iota(jnp.int32, (rq, lanes), 1)
        for g in range(NSA_KV_HEADS):
            q = q_ref[0, g]
            kn = newrows_ref[:, (4 + g) * d:(5 + g) * d].astype(BF16)
            vn = newrows_ref[:, (6 + g) * d:(7 + g) * d].astype(BF16)
            s = lax.dot_general(q, kn, nt, preferred_element_type=F32)
            last_picked = jnp.sum(jnp.where(lane_l == past_len // SLC_BLOCK, sel16[g], 0.0), axis=-1, keepdims=True)
            online(g, s, own & (new_pos <= pos) & (last_picked > 0.5), vn)
            o_slc = acc_scr[g] / jnp.maximum(l_scr[g], 1e-30)
            w_len = kwin_ref.shape[1]
            win_rows = pl.ds(g, w_len, stride=NSA_KV_HEADS)
            kw = kwin_ref.reshape(NSA_KV_HEADS * w_len, d)[win_rows, :].astype(BF16)
            vw = vwin_ref.reshape(NSA_KV_HEADS * w_len, d)[win_rows, :].astype(BF16)
            s1 = lax.dot_general(q, kw, nt, preferred_element_type=F32)
            kp1 = past_len - w_len + lax.broadcasted_iota(jnp.int32, (rq, w_len), 1)
            d1 = pos - kp1
            m1 = (d1 >= 0) & (d1 < WINDOW) & (kp1 >= 0)
            knw = newwin_ref[:, g * d:(g + 1) * d].astype(BF16)
            vnw = newwin_ref[:, (2 + g) * d:(3 + g) * d].astype(BF16)
            s2 = lax.dot_general(q, knw, nt, preferred_element_type=F32)
            d2 = pos - new_pos
            m2 = own & (d2 >= 0) & (d2 < WINDOW)
            s1 = jnp.where(m1, s1, -1e30)
            s2 = jnp.where(m2, s2, -1e30)
            mx = jnp.maximum(jnp.max(s1, axis=-1, keepdims=True), jnp.max(s2, axis=-1, keepdims=True))
            p1 = jnp.where(m1, jnp.exp(s1 - mx), 0.0)
            p2 = jnp.where(m2, jnp.exp(s2 - mx), 0.0)
            den = jnp.sum(p1, axis=-1, keepdims=True) + jnp.sum(p2, axis=-1, keepdims=True)
            o_win = (jnp.dot(p1.astype(BF16), vw, preferred_element_type=F32)
                     + jnp.dot(p2.astype(BF16), vnw, preferred_element_type=F32)) / jnp.maximum(den, 1e-30)
            out = jnp.zeros((rq, d), F32)
            for branch, o_b in enumerate((ocmp_ref[0, g], o_slc, o_win)):
                col = branch * NSA_HEADS + g * r + row // t_len
                gate = jnp.sum(jnp.where(lane == col, sig_rows, 0.0), axis=-1, keepdims=True)
                out = out + gate * o_b
            o_ref[0, g] = out


def nsa_sample_attend(cache, page_table, q_s, sel, o_cmp, rows, win, win_cache, proj, past_len, t_len, row0):
    bs, n_pages = page_table.shape
    d = NSA_HEAD_DIM
    rq = NSA_GROUP * t_len
    lanes = sel.shape[3]
    w_len = win_cache.shape[1]
    assert row0 % t_len == 0 and NEW_ROWS_BLK % t_len == 0 and past_len % SLC_BLOCK == 0
    blk = lambda s: (row0 + s * t_len) // NEW_ROWS_BLK
    per_seq = lambda shape: pl.BlockSpec((1,) + shape, lambda s, c, pt: (s, 0, 0, 0))
    return pl.pallas_call(
        functools.partial(_sample_attend_kernel, past_len=past_len, t_len=t_len, row0=row0),
        grid_spec=pltpu.PrefetchScalarGridSpec(
            num_scalar_prefetch=1,
            grid=(bs, n_pages // SAMPLE_PAGES),
            in_specs=_page_specs(2) + _page_specs(3) + [
                per_seq((NSA_KV_HEADS, rq, d)), per_seq((NSA_KV_HEADS, 8, lanes)), per_seq((NSA_KV_HEADS, rq, d)),
                pl.BlockSpec((NEW_ROWS_BLK, 8 * d), lambda s, c, pt: (blk(s), 0)),
                pl.BlockSpec((1, w_len, 1, NSA_KV_HEADS, d), lambda s, c, pt: (s, 0, 0, 0, 0)),
                pl.BlockSpec((1, w_len, 1, NSA_KV_HEADS, d), lambda s, c, pt: (s, 0, 1, 0, 0)),
                pl.BlockSpec((NEW_ROWS_BLK, 4 * d), lambda s, c, pt: (blk(s), 0)),
                pl.BlockSpec((NEW_ROWS_BLK, LANES), lambda s, c, pt: (blk(s), GATE_COL0 // LANES))],
            out_specs=per_seq((NSA_KV_HEADS, rq, d)),
            scratch_shapes=[pltpu.VMEM((NSA_KV_HEADS, rq, 1), F32), pltpu.VMEM((NSA_KV_HEADS, rq, 1), F32),
                            pltpu.VMEM((NSA_KV_HEADS, rq, d), F32)]),
        out_shape=jax.ShapeDtypeStruct((bs, NSA_KV_HEADS, rq, d), F32),
        compiler_params=pltpu.CompilerParams(
            dimension_semantics=("parallel", "arbitrary"), vmem_limit_bytes=VMEM_LIMIT),
        name="nsa_sample_attend",
    )(page_table, *([cache] * (2 * SAMPLE_PAGES)), q_s, sel, o_cmp, rows, win_cache, win_cache, win, proj)


SC_ROWS = 512
SC_COLS = 512


def _short_conv_kernel(b_ref, c_ref, h_ref, w_ref, buf_ref, o_ref, tail_ref, prev_scr):
    @pl.when(pl.program_id(2) == 0)
    def _():
        prev_scr[...] = buf_ref[0]

    rows = c_ref.shape[0]
    u = c_ref[...] * h_ref[...]
    prev = prev_scr[...]
    row8 = lax.broadcasted_iota(jnp.int32, prev.shape, 0)
    conv = u * w_ref[SC_KSIZE - 1:SC_KSIZE, :]
    for k in range(1, SC_KSIZE):
        rolled = pltpu.roll(u, k, 0)
        top = jnp.where(row8 < k, pltpu.roll(prev, k, 0), rolled[0:8])
        shifted = top if rows == 8 else jnp.concatenate([top, rolled[8:]], axis=0)
        conv = conv + shifted * w_ref[SC_KSIZE - 1 - k:SC_KSIZE - k, :]
    prev_scr[...] = u[rows - 8:rows]
    o_ref[...] = (b_ref[...] * conv).astype(o_ref.dtype)
    tail_ref[0] = u[rows - 8:rows]


def short_conv(proj, n_batch, t_len, rows_per_step, conv_buf, sc_w):
    nr = t_len // rows_per_step
    nh = SC_WIDTH // SC_COLS
    blk0 = SC_COL0 // SC_COLS
    assert SC_COL0 % SC_COLS == 0
    buf8 = jnp.pad(conv_buf, ((0, 0), (8 - (SC_KSIZE - 1), 0), (0, 0)))
    col = lambda part: pl.BlockSpec((rows_per_step, SC_COLS), lambda b, j, i: (b * nr + i, blk0 + part * nh + j))
    return pl.pallas_call(
        _short_conv_kernel,
        grid=(n_batch, nh, nr),
        in_specs=[col(0), col(1), col(2), pl.BlockSpec((SC_KSIZE, SC_COLS), lambda b, j, i: (0, j)),
                  pl.BlockSpec((1, 8, SC_COLS), lambda b, j, i: (b, 0, j))],
        out_specs=[pl.BlockSpec((rows_per_step, SC_COLS), lambda b, j, i: (b * nr + i, j)),
                   pl.BlockSpec((1, 8, SC_COLS), lambda b, j, i: (b, 0, j))],
        out_shape=[jax.ShapeDtypeStruct((n_batch * t_len, SC_WIDTH), BF16),
                   jax.ShapeDtypeStruct((n_batch, 8, SC_WIDTH), F32)],
        scratch_shapes=[pltpu.VMEM((8, SC_COLS), F32)],
        compiler_params=pltpu.CompilerParams(
            dimension_semantics=("parallel", "parallel", "arbitrary"), vmem_limit_bytes=VMEM_LIMIT),
        name="short_conv",
    )(proj, proj, proj, sc_w, buf8)


def _ret_tables(pos, chunk, valid):
    half = RET_DQK // 2
    inv = RET_THETA ** (-jnp.arange(half, dtype=F32) / half)
    ang = pos.astype(F32)[:, None] * inv[None, :]
    log_g = jnp.log1p(-(2.0 ** (-5.0 - jnp.arange(RET_HEADS, dtype=F32))))
    i = jnp.arange(chunk, dtype=F32)
    diff = i[:, None] - i[None, :]
    intra = jnp.where(diff >= 0, jnp.exp(jnp.maximum(diff, 0.0)[None] * log_g[:, None, None]), 0.0)
    q_dec = jnp.exp((i[None, :] + 1.0) * log_g[:, None])[..., None]
    k_dec = jnp.exp((valid - 1.0 - i)[None, :] * log_g[:, None])[..., None]
    c_dec = jnp.exp(valid * log_g)[:, None, None]
    return jnp.cos(ang), jnp.sin(ang), intra, q_dec, k_dec, c_dec


def _retention_kernel(q_ref, k_ref, v_ref, g_ref, cos_ref, sin_ref, intra_ref, qd_ref, kd_ref, cd_ref, gn_ref,
                      s0_ref, o_ref, s_out_ref, s_scr):
    c_idx = pl.program_id(1)

    @pl.when(c_idx == 0)
    def _():
        s_scr[...] = s0_ref[0]

    cos, sin = cos_ref[...], sin_ref[...]
    half = RET_DQK // 2
    nt = (((1,), (1,)), ((), ()))
    tn = (((0,), (0,)), ((), ()))

    def rot(x):
        x1, x2 = x[:, :half], x[:, half:]
        return jnp.concatenate([x1 * cos - x2 * sin, x1 * sin + x2 * cos], axis=-1)

    for h in range(RET_HEADS):
        cols = slice(h * RET_DQK, (h + 1) * RET_DQK)
        qr = (rot(q_ref[:, cols]) * (RET_DQK ** -0.5)).astype(BF16)
        kr = rot(k_ref[:, cols])
        v = v_ref[:, cols].astype(BF16)
        att = lax.dot_general(qr, kr.astype(BF16), nt, preferred_element_type=F32) * intra_ref[h]
        s_old = s_scr[h]
        o = (jnp.dot(att.astype(BF16), v, preferred_element_type=F32)
             + jnp.dot(qr, s_old.astype(BF16), preferred_element_type=F32) * qd_ref[h])
        s_scr[h] = s_old * cd_ref[h] + lax.dot_general((kr * kd_ref[h]).astype(BF16), v, tn,
                                                       preferred_element_type=F32)
        mu = jnp.mean(o, axis=-1, keepdims=True)
        dev = o - mu
        var = jnp.mean(dev * dev, axis=-1, keepdims=True)
        gate = g_ref[:, cols]
        on = dev * lax.rsqrt(var + EPS) * gn_ref[:, cols] * (gate * jax.nn.sigmoid(gate))
        o_ref[:, cols] = on.astype(o_ref.dtype)
    s_out_ref[0] = s_scr[...]


def retention(proj, pos, n_batch, t_len, chunk, state0, gn_gain, valid=None):
    nc = t_len // chunk
    cos, sin, intra, q_dec, k_dec, c_dec = _ret_tables(pos, chunk, chunk if valid is None else valid)
    half = RET_DQK // 2
    col = lambda j: pl.BlockSpec((chunk, R_QK), lambda b, c: (b * nc + c, j))
    tab = pl.BlockSpec((chunk, half), lambda b, c: (c, 0))
    full = lambda a: pl.BlockSpec(a.shape, lambda b, c: (0,) * a.ndim)
    st = pl.BlockSpec((1, RET_HEADS, RET_DQK, RET_DV), lambda b, c: (b, 0, 0, 0))
    return pl.pallas_call(
        _retention_kernel,
        grid=(n_batch, nc),
        in_specs=[col(0), col(1), col(2), col(3), tab, tab, full(intra), full(q_dec), full(k_dec), full(c_dec),
                  pl.BlockSpec((1, R_V), lambda b, c: (0, 0)), st],
        out_specs=[pl.BlockSpec((chunk, R_V), lambda b, c: (b * nc + c, 0)), st],
        out_shape=[jax.ShapeDtypeStruct((n_batch * t_len, R_V), BF16),
                   jax.ShapeDtypeStruct((n_batch, RET_HEADS, RET_DQK, RET_DV), F32)],
        scratch_shapes=[pltpu.VMEM((RET_HEADS, RET_DQK, RET_DV), F32)],
        compiler_params=pltpu.CompilerParams(
            dimension_semantics=("parallel", "arbitrary"), vmem_limit_bytes=VMEM_LIMIT),
        name="retention",
    )(proj, proj, proj, proj, cos, sin, intra, q_dec, k_dec, c_dec, gn_gain.reshape(1, R_V), state0)


SSD_COL0 = 2 * R_QK + 2 * R_V
HEADS_PER_GROUP = M2_HEADS // M2_GROUPS


def _split3(x):
    a = x.astype(BF16)
    r = x - a.astype(F32)
    b = r.astype(BF16)
    c = (r - b.astype(F32)).astype(BF16)
    return a, b, c


def _exact_dot(mat_bf, x):
    out = None
    for piece in _split3(x):
        t = jnp.dot(mat_bf, piece, preferred_element_type=F32)
        out = t if out is None else out + t
    return out


def _exact_dot_r(x, mat_bf):
    out = None
    for piece in _split3(x):
        t = jnp.dot(piece, mat_bf, preferred_element_type=F32)
        out = t if out is None else out + t
    return out


def _ssd_kernel(z_ref, xa_ref, xb_ref, xc_ref, dt_ref, cw_ref, cb_ref, dtb_ref, aneg_ref, dskip_ref, norm_ref,
                buf0_ref, s0_ref, o_ref, s_out_ref, s_scr, prev_scr, *, valid):
    c_idx = pl.program_id(1)
    chunk = z_ref.shape[0]
    nt = (((1,), (1,)), ((), ()))
    tn = (((0,), (0,)), ((), ()))

    @pl.when(c_idx == 0)
    def _():
        s_scr[...] = s0_ref[0]
        prev_scr[...] = buf0_ref[0]

    x = jnp.concatenate([xa_ref[...], xb_ref[...], xc_ref[...]], axis=1)
    prev = prev_scr[...]
    row8 = lax.broadcasted_iota(jnp.int32, prev.shape, 0)
    conv = x * cw_ref[M2_CONV - 1:M2_CONV, :]
    for k in range(1, M2_CONV):
        rolled = pltpu.roll(x, k, 0)
        top = jnp.where(row8 < k, pltpu.roll(prev, k, 0), rolled[0:8])
        shifted = top if chunk == 8 else jnp.concatenate([top, rolled[8:]], axis=0)
        conv = conv + shifted * cw_ref[M2_CONV - 1 - k:M2_CONV - k, :]
    prev_scr[...] = x[chunk - 8:chunk]
    conv = conv + cb_ref[...]
    xbc = conv * jax.nn.sigmoid(conv)
    xs = xbc[:, :M2_DINNER]

    dt_raw = dt_ref[...] + dtb_ref[...]
    dt = jnp.where(dt_raw > 20.0, dt_raw, jnp.log1p(jnp.exp(jnp.minimum(dt_raw, 20.0))))
    if valid < chunk:
        dt = jnp.where(lax.broadcasted_iota(jnp.int32, dt.shape, 0) < valid, dt, 0.0)
    a = dt * aneg_ref[...]
    ri = lax.broadcasted_iota(jnp.int32, (chunk, chunk), 0)
    ci = lax.broadcasted_iota(jnp.int32, (chunk, chunk), 1)
    tri = ri >= ci
    cum = _exact_dot(jnp.where(tri, 1.0, 0.0).astype(BF16), a)
    cum_t = cum.T
    cum_last = cum[chunk - 1:chunk, :]
    hi = lax.broadcasted_iota(jnp.int32, (LANES, M2_DINNER), 0)
    li = lax.broadcasted_iota(jnp.int32, (LANES, M2_DINNER), 1)
    expand = jnp.where(hi == li // M2_HEADDIM, 1.0, 0.0).astype(BF16)
    dt_x = _exact_dot_r(dt, expand)
    cum_x = _exact_dot_r(cum, expand)
    last_x = _exact_dot_r(cum_last, expand)
    xdt = xs * dt_x
    x_dec = (xdt * jnp.exp(last_x - cum_x)).astype(BF16)
    xdt_bf = xdt.astype(BF16)
    e_cum_x = jnp.exp(cum_x)
    e_last_x = jnp.exp(last_x)

    y_parts = []
    for gi in range(M2_GROUPS):
        b_g = xbc[:, M2_DINNER + gi * M2_STATE:M2_DINNER + (gi + 1) * M2_STATE].astype(BF16)
        c_g = xbc[:, M2_DINNER + (M2_GROUPS + gi) * M2_STATE:M2_DINNER + (M2_GROUPS + gi + 1) * M2_STATE].astype(BF16)
        cb = lax.dot_general(c_g, b_g, nt, preferred_element_type=F32)
        gcols = slice(gi * HEADS_PER_GROUP * M2_HEADDIM, (gi + 1) * HEADS_PER_GROUP * M2_HEADDIM)
        s_old = s_scr[:, gcols]
        y_state = jnp.dot(c_g, s_old.astype(BF16), preferred_element_type=F32) * e_cum_x[:, gcols]
        s_scr[:, gcols] = s_old * e_last_x[:, gcols] + lax.dot_general(b_g, x_dec[:, gcols], tn,
                                                                        preferred_element_type=F32)
        pair_lane = lax.broadcasted_iota(jnp.int32, (chunk, LANES), 1)
        intra = []
        for pr in range(HEADS_PER_GROUP // 2):
            outs = []
            for sub in range(2):
                h = gi * HEADS_PER_GROUP + pr * 2 + sub
                seg = cum[:, h:h + 1] - cum_t[h:h + 1, :]
                l_mat = jnp.where(tri, jnp.exp(jnp.where(tri, seg, 0.0)), 0.0)
                lanes = slice(gi * HEADS_PER_GROUP * M2_HEADDIM + pr * LANES,
                              gi * HEADS_PER_GROUP * M2_HEADDIM + (pr + 1) * LANES)
                outs.append(jnp.dot((cb * l_mat).astype(BF16), xdt_bf[:, lanes], preferred_element_type=F32))
            intra.append(jnp.where(pair_lane < M2_HEADDIM, outs[0], outs[1]))
        y_parts.append(jnp.concatenate(intra, axis=1) + y_state)
    y = jnp.concatenate(y_parts, axis=1) + dskip_ref[...] * xs
    z = z_ref[...]
    y = y * (z * jax.nn.sigmoid(z))
    gw = M2_DINNER // M2_GROUPS
    outs = []
    for gi in range(M2_GROUPS):
        yg = y[:, gi * gw:(gi + 1) * gw]
        outs.append(yg * lax.rsqrt(jnp.mean(yg * yg, axis=-1, keepdims=True) + EPS))
    o_ref[...] = (jnp.concatenate(outs, axis=1) * norm_ref[...]).astype(o_ref.dtype)
    s_out_ref[0] = s_scr[...]


def ssd(proj, n_batch, t_len, chunk, conv_buf, state0, conv_w, conv_b, dt_bias, a_log, d_skip, m2_norm, valid=None):
    nc = t_len // chunk
    z_blk = SSD_COL0 // M2_DINNER
    xw = M2_CONV_DIM // 3
    xbc_blk = (SSD_COL0 + M2_DINNER) // xw
    assert (SSD_COL0 + M2_DINNER) % xw == 0 and xw % LANES == 0
    dt_blk = (SSD_COL0 + M2_DINNER + M2_CONV_DIM) // LANES
    pad = lambda v: jnp.pad(v.reshape(1, -1), ((0, 0), (0, LANES - v.shape[-1])))
    buf8 = jnp.pad(conv_buf, ((0, 0), (8 - (M2_CONV - 1), 0), (0, 0)))
    st_t = state0.transpose(0, 3, 1, 2).reshape(n_batch, M2_STATE, M2_DINNER)
    row = lambda a: pl.BlockSpec(a.shape, lambda b, c: (0, 0))
    cw = conv_w
    cb = conv_b.reshape(1, -1)
    dtb, aneg = pad(dt_bias), pad(-jnp.exp(a_log))
    dsk = jnp.repeat(d_skip, M2_HEADDIM).reshape(1, -1)
    nrm = m2_norm.reshape(1, -1)
    st = pl.BlockSpec((1, M2_STATE, M2_DINNER), lambda b, c: (b, 0, 0))
    out, s_fin = pl.pallas_call(
        functools.partial(_ssd_kernel, valid=chunk if valid is None else valid),
        grid=(n_batch, nc),
        in_specs=[pl.BlockSpec((chunk, M2_DINNER), lambda b, c: (b * nc + c, z_blk)),
                  pl.BlockSpec((chunk, xw), lambda b, c: (b * nc + c, xbc_blk)),
                  pl.BlockSpec((chunk, xw), lambda b, c: (b * nc + c, xbc_blk + 1)),
                  pl.BlockSpec((chunk, xw), lambda b, c: (b * nc + c, xbc_blk + 2)),
                  pl.BlockSpec((chunk, LANES), lambda b, c: (b * nc + c, dt_blk)),
                  row(cw), row(cb), row(dtb), row(aneg), row(dsk), row(nrm),
                  pl.BlockSpec((1, 8, M2_CONV_DIM), lambda b, c: (b, 0, 0)), st],
        out_specs=[pl.BlockSpec((chunk, M2_DINNER), lambda b, c: (b * nc + c, 0)), st],
        out_shape=[jax.ShapeDtypeStruct((n_batch * t_len, M2_DINNER), BF16),
                   jax.ShapeDtypeStruct((n_batch, M2_STATE, M2_DINNER), F32)],
        scratch_shapes=[pltpu.VMEM((M2_STATE, M2_DINNER), F32), pltpu.VMEM((8, M2_CONV_DIM), F32)],
        compiler_params=pltpu.CompilerParams(
            dimension_semantics=("parallel", "arbitrary"), vmem_limit_bytes=VMEM_LIMIT),
        name="ssd",
    )(proj, proj, proj, proj, proj, cw, cb, dtb, aneg, dsk, nrm, buf8, st_t)
    s_fin = s_fin.reshape(n_batch, M2_STATE, M2_HEADS, M2_HEADDIM).transpose(0, 2, 3, 1)
    return out, s_fin


SEQ_ROWS = 8


def _pad_seq_rows(a, n_seq, t_len):
    assert t_len <= SEQ_ROWS
    a = a.reshape(n_seq, t_len, a.shape[-1])
    return jnp.pad(a, ((0, 0), (0, SEQ_ROWS - t_len), (0, 0))).reshape(n_seq * SEQ_ROWS, a.shape[-1])


def _unpad_seq_rows(a, n_seq, t_len):
    return a.reshape(n_seq, SEQ_ROWS, a.shape[-1])[:, :t_len].reshape(n_seq * t_len, a.shape[-1])

def kernel(x_prompt, x_sample, cache_nsa_kv, cache_nsa_win, state_sc_conv, state_ret, state_ssm, state_m2_conv,
           page_table, norm_w, final_norm, e_w_in, e_w_out, e_cmp_pe, e_cmp_w1, e_cmp_w2, e_sc_conv, o_w_in,
           o_w_out, o_ret_gn, o_m2_conv_w, o_m2_conv_b, o_m2_dt_bias, o_m2_a_log, o_m2_d, o_m2_norm, peer_wq,
           peer_keys, peer_u, peer_v):
    bp, tp, dm = x_prompt.shape
    bs, ts, _ = x_sample.shape
    n_p, n_s = bp * tp, bs * ts
    g, r, d = NSA_KV_HEADS, NSA_GROUP, NSA_HEAD_DIM
    past_len = page_table.shape[1] * PAGE_SIZE
    pos_p = jnp.arange(tp, dtype=jnp.int32)
    pos_s = past_len + jnp.arange(ts, dtype=jnp.int32)
    n_real = n_p + n_s
    n_pad = _round_up(n_real, PEER_TM) - n_real
    s_rows = slice(n_p, n_real)
    pos_rows = jnp.concatenate([jnp.tile(pos_p, bp), jnp.tile(pos_s, bs), jnp.zeros((n_pad,), jnp.int32)])
    xp = x_prompt.reshape(n_p, dm)
    x = jnp.concatenate([xp, x_sample.reshape(n_s, dm), xp[:n_pad]], axis=0)
    pad_bf = jnp.zeros((n_pad, E_Q), BF16)

    w0 = e_w_in[0]
    w_in = jnp.concatenate([w0[:, :SC_COL0], w0[:, SC_COL0 + E_G:], w0[:, SC_COL0:SC_COL0 + E_G],
                            jnp.zeros((dm, _pad_cols(E_IN) - E_IN), F32)], axis=1).astype(BF16)
    proj = norm_matmul(x, norm_w[0, 0], w_in)
    q_bf, rows, win, rows_bf, win_bf = nsa_rope(proj, pos_rows)
    kvc = nsa_compress(rows, bp, tp, e_cmp_pe[0], e_cmp_w1[0], e_cmp_w2[0])
    o_nsa_p = nsa_prompt(q_bf, kvc, rows_bf, win_bf, proj, bp, tp)
    p_kv = rows[:n_p].reshape(bp, tp, 4, g, d)
    s_kv = rows[s_rows].reshape(bs, ts, 4, g, d)
    p_win = win[:n_p].reshape(bp, tp, 2, g, d)[:, tp - min(WINDOW, tp):]
    win_s = win[s_rows].reshape(bs, ts, 2, g, d)
    s_win = jnp.concatenate([cache_nsa_win[0], win_s], axis=1)[:, ts:]
    cache = cache_nsa_kv[0]
    q_s = q_bf[s_rows].reshape(bs, ts, g, r, d).transpose(0, 2, 3, 1, 4).reshape(bs, g, r * ts, d)
    a0, a1 = nsa_sample_compress(cache, page_table, e_cmp_pe[0], e_cmp_w1[0])
    o_cmp_s, sel_s = nsa_sample_select(a0, a1, e_cmp_w2[0], q_s, past_len, ts)
    o_nsa_s = nsa_sample_attend(cache, page_table, q_s, sel_s, o_cmp_s, rows, win,
                                cache_nsa_win[0], proj, past_len, ts, n_p)
    o_nsa_s = o_nsa_s.reshape(bs, g, r, ts, d).transpose(0, 3, 1, 2, 4).reshape(n_s, E_Q)
    o_nsa = jnp.concatenate([o_nsa_p, o_nsa_s.astype(BF16), pad_bf], axis=0)
    o_sc_p, tail_p = short_conv(proj, bp, tp, SC_ROWS, jnp.zeros((bp, SC_KSIZE - 1, SC_WIDTH), F32), e_sc_conv[0])
    p_sc = tail_p[:, 8 - (SC_KSIZE - 1):]
    o_sc_s, tail_s = short_conv(_pad_seq_rows(proj[s_rows], bs, ts), bs, SEQ_ROWS, SEQ_ROWS, state_sc_conv[0],
                                e_sc_conv[0])
    s_sc = jnp.concatenate([state_sc_conv[0], tail_s[:, :ts]], axis=1)[:, ts:]
    o_sc = jnp.concatenate([_unpad_seq_rows(o_sc_s, bs, ts), pad_bf], axis=0)
    o_sc = jnp.concatenate([o_sc_p, o_sc], axis=0)
    x = matmul2_res(o_nsa, o_sc, e_w_out[0].astype(BF16), x)
    u_bf = peer_u.astype(BF16)
    vt_bf = peer_v.transpose(0, 2, 1).astype(BF16)
    peer_t = peer_layer(x, norm_w[0, 1], peer_wq[0], peer_keys[0], u_bf, vt_bf, 0)

    w_in = jnp.pad(o_w_in[0], ((0, 0), (0, _pad_cols(O_IN) - O_IN))).astype(BF16)
    proj, x = norm_matmul(x, norm_w[1, 0], w_in, delta_t=peer_t)
    odd_w = (o_m2_conv_w[0], o_m2_conv_b[0], o_m2_dt_bias[0], o_m2_a_log[0], o_m2_d[0], o_m2_norm[0])
    o_ret_p, p_ret = retention(proj, pos_p, bp, tp, CHUNK, jnp.zeros((bp, RET_HEADS, RET_DQK, RET_DV), F32),
                               o_ret_gn[0])
    o_ssd_p, p_ssm = ssd(proj, bp, tp, CHUNK, jnp.zeros((bp, M2_CONV - 1, M2_CONV_DIM), F32),
                         jnp.zeros((bp, M2_HEADS, M2_HEADDIM, M2_STATE), F32), *odd_w)
    xbc_cols = slice(O_SPLITS[4], O_SPLITS[5])
    p_m2c = jnp.stack([proj[(b + 1) * tp - (M2_CONV - 1):(b + 1) * tp, xbc_cols] for b in range(bp)])
    xbc_s = proj[s_rows, xbc_cols].reshape(bs, ts, M2_CONV_DIM)
    s_m2c = jnp.concatenate([state_m2_conv[0], xbc_s], axis=1)[:, ts:]
    proj_s = _pad_seq_rows(proj[s_rows], bs, ts)
    pos_s8 = past_len + jnp.arange(SEQ_ROWS, dtype=jnp.int32)
    o_ret_s, s_ret = retention(proj_s, pos_s8, bs, SEQ_ROWS, SEQ_ROWS, state_ret[0], o_ret_gn[0], valid=ts)
    o_ssd_s, s_ssm = ssd(proj_s, bs, SEQ_ROWS, SEQ_ROWS, state_m2_conv[0], state_ssm[0], *odd_w, valid=ts)
    o_ret = jnp.concatenate([o_ret_p, _unpad_seq_rows(o_ret_s, bs, ts), pad_bf], axis=0)
    o_ssd = jnp.concatenate([o_ssd_p, _unpad_seq_rows(o_ssd_s, bs, ts), pad_bf], axis=0)
    x = matmul2_res(o_ret, o_ssd, o_w_out[0].astype(BF16), x)
    peer_t = peer_layer(x, norm_w[1, 1], peer_wq[1], peer_keys[1], u_bf, vt_bf, 1)

    y = resid_rmsnorm(x, peer_t, final_norm)
    y_prompt = y[:n_p].reshape(bp, tp, dm)
    y_sample = y[s_rows].reshape(bs, ts, dm)
    return (y_prompt, y_sample, p_kv[None], p_win[None], p_sc[None], p_ret[None], p_ssm[None], p_m2c[None],
            s_kv[None], s_win[None], s_sc[None], s_ret[None], s_ssm[None], s_m2c[None])
```

```python
import functools
import math

import jax
import jax.numpy as jnp
from jax import lax
from jax.experimental import pallas as pl
from jax.experimental.pallas import tpu as pltpu

F32 = jnp.float32
BF16 = jnp.bfloat16

D_MODEL = 2048
DEPTH = 2
PAGE_SIZE = 128
NSA_HEAD_DIM = 128
NSA_HEADS = 8
NSA_KV_HEADS = 2
NSA_GROUP = 4
CMP_BLOCK = 32
CMP_STRIDE = 16
SLC_BLOCK = 64
SLC_TOPK = 16
WINDOW = 512
ROPE_THETA = 500000.0
ROPE_DIMS = 32
SC_WIDTH = 1024
SC_KSIZE = 3
RET_HEADS = 4
RET_DQK = 256
RET_DV = 256
RET_THETA = 10000.0
M2_DINNER = 1024
M2_HEADDIM = 64
M2_HEADS = 16
M2_STATE = 128
M2_GROUPS = 2
M2_CONV = 4
M2_CONV_DIM = M2_DINNER + 2 * M2_GROUPS * M2_STATE
PEER_HEADS = 8
PEER_KEYS = 128
PEER_QDIM = 256
PEER_TOPK = 16
Q_BLOCK = 128
CHUNK = 128
EPS = 1e-6

E_Q = NSA_HEADS * NSA_HEAD_DIM
E_KV = 6 * NSA_KV_HEADS * NSA_HEAD_DIM
E_G = 3 * NSA_HEADS
E_SC = 3 * SC_WIDTH
E_IN = E_Q + E_KV + E_G + E_SC
R_QK = RET_HEADS * RET_DQK
R_V = RET_HEADS * RET_DV
O_SPLITS = [R_QK, 2 * R_QK, 2 * R_QK + R_V, 2 * R_QK + 2 * R_V,
            2 * R_QK + 2 * R_V + M2_DINNER, 2 * R_QK + 2 * R_V + M2_DINNER + M2_CONV_DIM]
O_IN = O_SPLITS[-1] + M2_HEADS
SC_COL0 = E_Q + E_KV
GATE_COL0 = SC_COL0 + E_SC

LANES = 128
VMEM_LIMIT = 56 * 1024 * 1024
ROW_TILE_CAP = 1056
ROPE_ROWS_CAP = 544
RESID_ROWS = 768
NEG_INF = float("-inf")


def _round_up(n, m):
    return -(-n // m) * m


def _pick_tile(n, cap):
    best = LANES
    for t in range(LANES, cap + 1, LANES):
        if n % t == 0:
            best = t
    return best


def _pad_cols(m):
    return min((_round_up(m, t) for t in (768, 640, 512)))


def _row_tile(n, cap):
    best = None
    for t in range(16, cap + 1, 16):
        if n % t == 0:
            best = t
    assert best is not None
    return best


def _gelu_tanh(x):
    return 0.5 * x * (1.0 + jnp.tanh(math.sqrt(2.0 / math.pi) * (x + 0.044715 * (x * x * x))))


def _norm_matmul_kernel(x_ref, g_ref, w_ref, o_ref, xn_ref):
    @pl.when(pl.program_id(1) == 0)
    def _():
        x = x_ref[...]
        ms = jnp.mean(x * x, axis=-1, keepdims=True)
        xn_ref[...] = (x * lax.rsqrt(ms + EPS) * g_ref[...]).astype(BF16)

    o_ref[...] = jnp.dot(xn_ref[...], w_ref[...], preferred_element_type=F32)


def norm_matmul(x, gain, w_bf):
    n, k = x.shape
    m = w_bf.shape[1]
    tm = _row_tile(n, ROW_TILE_CAP)
    tn = _pick_tile(m, 768)
    return pl.pallas_call(
        _norm_matmul_kernel,
        grid=(n // tm, m // tn),
        in_specs=[pl.BlockSpec((tm, k), lambda i, j: (i, 0)),
                  pl.BlockSpec((1, k), lambda i, j: (0, 0)),
                  pl.BlockSpec((k, tn), lambda i, j: (0, j))],
        out_specs=pl.BlockSpec((tm, tn), lambda i, j: (i, j)),
        out_shape=jax.ShapeDtypeStruct((n, m), F32),
        scratch_shapes=[pltpu.VMEM((tm, k), BF16)],
        compiler_params=pltpu.CompilerParams(
            dimension_semantics=("parallel", "arbitrary"), vmem_limit_bytes=VMEM_LIMIT),
        name="norm_matmul",
    )(x, gain.reshape(1, k), w_bf)


def _matmul2_res_kernel(a1_ref, a2_ref, w1_ref, w2_ref, r_ref, o_ref):
    o_ref[...] = (r_ref[...] + jnp.dot(a1_ref[...], w1_ref[...], preferred_element_type=F32)
                  + jnp.dot(a2_ref[...], w2_ref[...], preferred_element_type=F32))


def matmul2_res(a1, a2, w_bf, res):
    n, k1 = a1.shape
    k2 = a2.shape[1]
    m = w_bf.shape[1]
    assert k1 == k2
    tm = _row_tile(n, ROW_TILE_CAP)
    tn = _pick_tile(m, 1024)
    return pl.pallas_call(
        _matmul2_res_kernel,
        grid=(n // tm, m // tn),
        in_specs=[pl.BlockSpec((tm, k1), lambda i, j: (i, 0)),
                  pl.BlockSpec((tm, k2), lambda i, j: (i, 0)),
                  pl.BlockSpec((k1, tn), lambda i, j: (0, j)),
                  pl.BlockSpec((k2, tn), lambda i, j: (1, j)),
                  pl.BlockSpec((tm, tn), lambda i, j: (i, j))],
        out_specs=pl.BlockSpec((tm, tn), lambda i, j: (i, j)),
        out_shape=jax.ShapeDtypeStruct((n, m), F32),
        compiler_params=pltpu.CompilerParams(
            dimension_semantics=("parallel", "parallel"), vmem_limit_bytes=VMEM_LIMIT),
        name="matmul2_res",
    )(a1, a2, w_bf, w_bf, res)


PEER_TOK = 128
PEER_HEAD_UNROLL = 4
PEER_TM = 768
PEER_TA = 8
PEER_TE = PEER_TA * PEER_KEYS


def _top_desc(work, count, with_rank=False, one_at_a_time=False):
    rows = []
    rank = jnp.full(work.shape, float(count), F32)
    row_id = lax.broadcasted_iota(jnp.int32, work.shape, 0).astype(F32)
    for r in range(count):
        m = jnp.max(work, axis=0, keepdims=True)
        rows.append(m)
        hit = work == m
        if one_at_a_time:
            hit = row_id == jnp.min(jnp.where(hit, row_id, float(work.shape[0])), axis=0, keepdims=True)
        if with_rank:
            rank = jnp.where(hit, float(r), rank)
        work = jnp.where(hit, NEG_INF, work)
    return (rows, rank) if with_rank else rows


def _peer_router_kernel(q_ref, k_ref, cnt_ref, gw_ref, r2_ref, w2_ref):
    def head(h, carry):
        q = q_ref[h]
        scores = []
        for side in range(2):
            qs = q[:, side * PEER_KEYS:(side + 1) * PEER_KEYS]
            qs = qs * lax.rsqrt(jnp.mean(qs * qs, axis=-1, keepdims=True) + EPS)
            scores.append(lax.dot_general(k_ref[h, side], qs, (((1,), (1,)), ((), ())),
                                          preferred_element_type=F32))
        s1, s2 = scores
        v1 = _top_desc(s1, PEER_TOPK + 1)
        v2, rank2 = _top_desc(s2, PEER_TOPK + 1, with_rank=True)
        v2_lo = jnp.concatenate(v2[:8], axis=0)
        v2_hi = jnp.concatenate(v2[8:16], axis=0)
        row = lax.broadcasted_iota(jnp.int32, v2_lo.shape, 0)
        blocks = [v1[0] + v2_lo, v1[0] + v2_hi, v1[1] + v2_lo]
        for a, lim in ((2, 5), (3, 4), (4, 3), (5, 2), (6, 2), (7, 2)):
            blocks.append(jnp.where(row < lim, v1[a] + v2_lo, NEG_INF))
        blocks.append(jnp.concatenate(v1[8:16], axis=0) + v2[0])
        extra = jnp.where(row == 0, v1[0] + v2[16], jnp.where(row == 1, v1[16] + v2[0], NEG_INF))
        blocks.append(extra)
        cand = jnp.concatenate(blocks, axis=0)
        tops = _top_desc(cand, PEER_TOPK + 1, one_at_a_time=True)
        z = jnp.zeros_like(tops[0])
        for r in range(PEER_TOPK):
            z = z + jnp.exp(tops[r] - tops[0])
        tau = 0.5 * (tops[PEER_TOPK - 1] + tops[PEER_TOPK])
        count = jnp.zeros_like(s1)
        for r in range(PEER_TOPK):
            count = count + jnp.where(s1 >= tau - v2[r], 1.0, 0.0)
        cnt_ref[h] = count
        gw_ref[h] = jnp.exp(s1 - v1[0]) / z
        r2_ref[h] = rank2.astype(BF16)
        w2_ref[h] = jnp.exp(s2 - v2[0]).astype(BF16)
        return carry

    lax.fori_loop(0, PEER_HEADS, head, 0, unroll=PEER_HEAD_UNROLL)


def peer_router(q_hm, keys):
    n = q_hm.shape[1]
    out = jax.ShapeDtypeStruct((PEER_HEADS, PEER_KEYS, n), F32)
    out_bf = jax.ShapeDtypeStruct((PEER_HEADS, PEER_KEYS, n), BF16)
    spec = pl.BlockSpec((PEER_HEADS, PEER_KEYS, PEER_TOK), lambda i: (0, 0, i))
    return pl.pallas_call(
        _peer_router_kernel,
        grid=(n // PEER_TOK,),
        in_specs=[pl.BlockSpec((PEER_HEADS, PEER_TOK, PEER_QDIM), lambda i: (0, i, 0)),
                  pl.BlockSpec((PEER_HEADS, 2, PEER_KEYS, PEER_QDIM // 2), lambda i: (0, 0, 0, 0))],
        out_specs=[spec, spec, spec, spec],
        out_shape=[out, out, out_bf, out_bf],
        compiler_params=pltpu.CompilerParams(dimension_semantics=("parallel",), vmem_limit_bytes=VMEM_LIMIT),
        name="peer_router",
    )(q_hm, keys)


def _peer_expert_kernel(xt_ref, u_ref, vt_ref, cnt_ref, gw_ref, r2_ref, w2_ref, o_ref, s_scr, hg_scr):
    e = pl.program_id(1)
    s_scr[...] = jnp.dot(u_ref[...], xt_ref[...], preferred_element_type=F32)
    zero = jnp.zeros((), BF16)
    for al in range(PEER_TA):
        rows = slice(al * PEER_KEYS, (al + 1) * PEER_KEYS)
        g = None
        for h in range(PEER_HEADS):
            cnt = cnt_ref[h, al:al + 1, :].astype(BF16)
            gate = gw_ref[h, al:al + 1, :].astype(BF16)
            t = jnp.where(r2_ref[h] < cnt, w2_ref[h], zero) * gate
            g = t if g is None else g + t
        hg_scr[rows, :] = _gelu_tanh(s_scr[rows, :]).astype(BF16) * g
    part = jnp.dot(vt_ref[...], hg_scr[...], preferred_element_type=F32)

    @pl.when(e == 0)
    def _():
        o_ref[...] = part

    @pl.when(e != 0)
    def _():
        o_ref[...] += part


def peer_experts(xt_bf, u_bf, vt_bf, layer, cnt, gw, r2, w2):
    d, n = xt_bf.shape
    n_exp = u_bf.shape[1]
    tm = PEER_TM
    sel_spec = pl.BlockSpec((PEER_HEADS, PEER_TA, tm), lambda i, e: (0, e, i))
    all_spec = pl.BlockSpec((PEER_HEADS, PEER_KEYS, tm), lambda i, e: (0, 0, i))
    return pl.pallas_call(
        _peer_expert_kernel,
        grid=(n // tm, n_exp // PEER_TE),
        in_specs=[pl.BlockSpec((d, tm), lambda i, e: (0, i)),
                  pl.BlockSpec((None, PEER_TE, d), lambda i, e: (layer, e, 0)),
                  pl.BlockSpec((None, d, PEER_TE), lambda i, e: (layer, 0, e)),
                  sel_spec, sel_spec, all_spec, all_spec],
        out_specs=pl.BlockSpec((d, tm), lambda i, e: (0, i)),
        out_shape=jax.ShapeDtypeStruct((d, n), F32),
        scratch_shapes=[pltpu.VMEM((PEER_TE, tm), F32), pltpu.VMEM((PEER_TE, tm), BF16)],
        compiler_params=pltpu.CompilerParams(
            dimension_semantics=("parallel", "arbitrary"), vmem_limit_bytes=VMEM_LIMIT),
        name="peer_experts",
    )(xt_bf, u_bf, vt_bf, cnt, gw, r2, w2)


def _norm_matmul_t_kernel(x_ref, g_ref, w_ref, o_ref, xt_ref, xn_ref):
    @pl.when(pl.program_id(1) == 0)
    def _():
        x = x_ref[...]
        ms = jnp.mean(x * x, axis=-1, keepdims=True)
        xn = x * lax.rsqrt(ms + EPS) * g_ref[...]
        xn_ref[...] = xn.astype(BF16)
        xt_ref[...] = xn.T.astype(BF16)

    o_ref[...] = jnp.dot(xn_ref[...], w_ref[...], preferred_element_type=F32).reshape(o_ref.shape)


def norm_matmul_t(x, gain, w_bf, cols):
    n, k = x.shape
    m = w_bf.shape[1]
    tm = PEER_TM
    return pl.pallas_call(
        _norm_matmul_t_kernel,
        grid=(n // tm, m // cols),
        in_specs=[pl.BlockSpec((tm, k), lambda i, j: (i, 0)),
                  pl.BlockSpec((1, k), lambda i, j: (0, 0)),
                  pl.BlockSpec((k, cols), lambda i, j: (0, j))],
        out_specs=[pl.BlockSpec((1, tm, cols), lambda i, j: (j, i, 0)),
                   pl.BlockSpec((k, tm), lambda i, j: (0, i))],
        out_shape=[jax.ShapeDtypeStruct((m // cols, n, cols), F32), jax.ShapeDtypeStruct((k, n), BF16)],
        scratch_shapes=[pltpu.VMEM((tm, k), BF16)],
        compiler_params=pltpu.CompilerParams(
            dimension_semantics=("parallel", "arbitrary"), vmem_limit_bytes=VMEM_LIMIT),
        name="norm_matmul_t",
    )(x, gain.reshape(1, k), w_bf)


def _resid_rmsnorm_kernel(x_ref, dt_ref, g_ref, o_ref):
    x = x_ref[...] + dt_ref[...].T
    ms = jnp.mean(x * x, axis=-1, keepdims=True)
    o_ref[...] = x * lax.rsqrt(ms + EPS) * g_ref[...]


def resid_rmsnorm(x, delta_t, gain):
    n, k = x.shape
    tm = _pick_tile(n, RESID_ROWS)
    return pl.pallas_call(
        _resid_rmsnorm_kernel,
        grid=(n // tm,),
        in_specs=[pl.BlockSpec((tm, k), lambda i: (i, 0)), pl.BlockSpec((k, tm), lambda i: (0, i)),
                  pl.BlockSpec((1, k), lambda i: (0, 0))],
        out_specs=pl.BlockSpec((tm, k), lambda i: (i, 0)),
        out_shape=jax.ShapeDtypeStruct((n, k), F32),
        compiler_params=pltpu.CompilerParams(dimension_semantics=("parallel",), vmem_limit_bytes=VMEM_LIMIT),
        name="resid_rmsnorm",
    )(x, delta_t, gain.reshape(1, k))


def peer_layer(x, gain, w_q, keys, u_bf, vt_bf, layer):
    q_hm, xt_bf = norm_matmul_t(x, gain, w_q.astype(BF16), PEER_QDIM)
    cnt, gw, r2, w2 = peer_router(q_hm, keys)
    return peer_experts(xt_bf, u_bf, vt_bf, layer, cnt, gw, r2, w2)


def _rope_tables(pos):
    half = ROPE_DIMS // 2
    inv = ROPE_THETA ** (-jnp.arange(half, dtype=F32) / half)
    ang = pos.astype(F32)[:, None] * inv[None, :]
    cos, sin = jnp.cos(ang), jnp.sin(ang)
    t = pos.shape[0]
    ones = jnp.ones((t, NSA_HEAD_DIM - ROPE_DIMS), F32)
    zeros = jnp.zeros((t, NSA_HEAD_DIM - ROPE_DIMS), F32)
    zh = jnp.zeros((t, half), F32)
    c = jnp.concatenate([cos, cos, ones], axis=1)
    s_lo = jnp.concatenate([-sin, zh, zeros], axis=1)
    s_hi = jnp.concatenate([zh, sin, zeros], axis=1)
    return c, s_lo, s_hi


def _rope_kernel(p_ref, c_ref, sl_ref, sh_ref, q_ref, rows_ref, win_ref, rows_bf_ref, win_bf_ref):
    c, sl, sh = c_ref[...], sl_ref[...], sh_ref[...]
    half = ROPE_DIMS // 2

    def rot(x):
        return x * c + pltpu.roll(x, LANES - half, 1) * sl + pltpu.roll(x, half, 1) * sh

    d = NSA_HEAD_DIM
    scale = d ** -0.5
    for hd in range(NSA_HEADS):
        q_ref[:, hd * d:(hd + 1) * d] = (rot(p_ref[:, hd * d:(hd + 1) * d]) * scale).astype(BF16)
    for blk in range(12):
        x = p_ref[:, E_Q + blk * d:E_Q + (blk + 1) * d]
        if (blk // 2) % 2 == 0:
            x = rot(x)
        if blk < 8:
            rows_ref[:, blk * d:(blk + 1) * d] = x
            rows_bf_ref[:, blk * d:(blk + 1) * d] = x.astype(BF16)
        else:
            win_ref[:, (blk - 8) * d:(blk - 7) * d] = x
            win_bf_ref[:, (blk - 8) * d:(blk - 7) * d] = x.astype(BF16)


def nsa_rope(proj, pos_rows):
    n = proj.shape[0]
    tm = _row_tile(n, ROPE_ROWS_CAP)
    c, sl, sh = _rope_tables(pos_rows)
    width = E_Q + E_KV
    tab = pl.BlockSpec((tm, LANES), lambda i: (i, 0))
    return pl.pallas_call(
        _rope_kernel,
        grid=(n // tm,),
        in_specs=[pl.BlockSpec((tm, width), lambda i: (i, 0)), tab, tab, tab],
        out_specs=[pl.BlockSpec((tm, E_Q), lambda i: (i, 0)),
                   pl.BlockSpec((tm, 1024), lambda i: (i, 0)),
                   pl.BlockSpec((tm, 512), lambda i: (i, 0)),
                   pl.BlockSpec((tm, 1024), lambda i: (i, 0)),
                   pl.BlockSpec((tm, 512), lambda i: (i, 0))],
        out_shape=[jax.ShapeDtypeStruct((n, E_Q), BF16),
                   jax.ShapeDtypeStruct((n, 1024), F32),
                   jax.ShapeDtypeStruct((n, 512), F32),
                   jax.ShapeDtypeStruct((n, 1024), BF16),
                   jax.ShapeDtypeStruct((n, 512), BF16)],
        compiler_params=pltpu.CompilerParams(dimension_semantics=("parallel",), vmem_limit_bytes=VMEM_LIMIT),
        name="nsa_rope",
    )(proj, c, sl, sh)


def _compress_kernel(x_ref, pe_ref, w1_ref, w2_ref, o_ref):
    n_sub = x_ref.shape[0] // CMP_STRIDE
    acc0 = jnp.zeros((n_sub, NSA_HEAD_DIM), F32)
    acc1 = jnp.zeros((n_sub, NSA_HEAD_DIM), F32)
    for s in range(CMP_STRIDE):
        xs = x_ref[pl.ds(s, n_sub, stride=CMP_STRIDE), :]
        a0 = (xs + pe_ref[0, s:s + 1, :]).astype(BF16)
        a1 = (xs + pe_ref[0, CMP_STRIDE + s:CMP_STRIDE + s + 1, :]).astype(BF16)
        acc0 = acc0 + jnp.dot(a0, w1_ref[0, s], preferred_element_type=F32)
        acc1 = acc1 + jnp.dot(a1, w1_ref[0, CMP_STRIDE + s], preferred_element_type=F32)
    pre = acc0 + pltpu.roll(acc1, n_sub - 1, 0)
    o_ref[0, 0] = jnp.dot(_gelu_tanh(pre).astype(BF16), w2_ref[0], preferred_element_type=F32).astype(BF16)


def nsa_compress(rows, n_batch, t_len, cmp_pe, cmp_w1, cmp_w2):
    n_sub = t_len // CMP_STRIDE
    d = NSA_HEAD_DIM
    w1 = cmp_w1.reshape(2, CMP_BLOCK, d, d).astype(BF16)
    return pl.pallas_call(
        _compress_kernel,
        grid=(n_batch, 4),
        in_specs=[pl.BlockSpec((t_len, d), lambda b, c: (b, c)),
                  pl.BlockSpec((1, CMP_BLOCK, d), lambda b, c: (c // 2, 0, 0)),
                  pl.BlockSpec((1, CMP_BLOCK, d, d), lambda b, c: (c // 2, 0, 0, 0)),
                  pl.BlockSpec((1, d, d), lambda b, c: (c // 2, 0, 0))],
        out_specs=pl.BlockSpec((1, 1, n_sub, d), lambda b, c: (b, c, 0, 0)),
        out_shape=jax.ShapeDtypeStruct((n_batch, 4, n_sub, d), BF16),
        compiler_params=pltpu.CompilerParams(
            dimension_semantics=("parallel", "parallel"), vmem_limit_bytes=VMEM_LIMIT),
        name="nsa_compress",
    )(rows, cmp_pe, w1, cmp_w2.astype(BF16))


NSA_TK = 512
NSA_WTILES = WINDOW // Q_BLOCK + 1


def _masked_softmax_rows(s, mask):
    s = jnp.where(mask, s, -1e30)
    p = jnp.exp(s - jnp.max(s, axis=-1, keepdims=True))
    p = jnp.where(mask, p, 0.0)
    return p / jnp.maximum(jnp.sum(p, axis=-1, keepdims=True), 1e-30)


def _nsa_prompt_kernel(q_ref, kc_ref, vc_ref, ks_ref, vs_ref, kw_ref, vw_ref, gate_ref, o_ref):
    g = pl.program_id(1)
    qi = pl.program_id(2)
    d, r, qb = NSA_HEAD_DIM, NSA_GROUP, Q_BLOCK
    n_cmp = kc_ref.shape[2]
    n_slc = ks_ref.shape[0] // SLC_BLOCK
    nt = (((1,), (1,)), ((), ()))
    q = jnp.concatenate([q_ref[:, h * d:(h + 1) * d] for h in range(r)], axis=0)
    t_col = qi * qb + lax.broadcasted_iota(jnp.int32, (qb, 1), 0)

    s = lax.dot_general(q, kc_ref[0, 0], nt, preferred_element_type=F32).reshape(r, qb, n_cmp)
    n_idx = lax.broadcasted_iota(jnp.int32, (qb, n_cmp), 1)
    c_mask = (n_idx * CMP_STRIDE + (CMP_BLOCK - 1) <= t_col)[None]
    p_c = _masked_softmax_rows(s, c_mask)
    o_cmp = jnp.dot(p_c.reshape(r * qb, n_cmp).astype(BF16), vc_ref[0, 0], preferred_element_type=F32)

    nb = LANES
    assert n_slc <= nb and qb == LANES
    p_sum = jnp.sum(p_c, axis=0)
    si = lax.broadcasted_iota(jnp.int32, (nb, n_cmp), 0) * SLC_BLOCK
    ci = lax.broadcasted_iota(jnp.int32, (nb, n_cmp), 1) * CMP_STRIDE
    overlap_t = jnp.where((ci < si + SLC_BLOCK) & (ci + CMP_BLOCK > si), 1.0, 0.0).astype(BF16)
    p_hi = p_sum.astype(BF16)
    p_lo = (p_sum - p_hi.astype(F32)).astype(BF16)
    imp_t = (lax.dot_general(overlap_t, p_hi, nt, preferred_element_type=F32)
             + lax.dot_general(overlap_t, p_lo, nt, preferred_element_type=F32))
    blk = lax.broadcasted_iota(jnp.int32, (nb, qb), 0)
    blk_f = blk.astype(F32)
    t_row = qi * qb + lax.broadcasted_iota(jnp.int32, (1, qb), 1)
    cur = t_row // SLC_BLOCK
    forced = (blk == 0) | (blk == cur) | (blk == cur - 1)
    work = jnp.where(blk * SLC_BLOCK <= t_row, jnp.where(forced, 1e6, imp_t), -1e6)
    work = jnp.where(blk < n_slc, work, NEG_INF)
    sel_t = jnp.zeros((nb, qb), F32)
    for _ in range(min(SLC_TOPK, n_slc)):
        m = jnp.max(work, axis=0, keepdims=True)
        first = jnp.min(jnp.where(work == m, blk_f, float(nb)), axis=0, keepdims=True)
        pick = blk_f == first
        sel_t = jnp.where(pick, 1.0, sel_t)
        work = jnp.where(pick, NEG_INF, work)
    sel_bf = sel_t.T.astype(BF16)

    tk = NSA_TK
    bpt = tk // SLC_BLOCK

    def slc_step(kt, carry):
        m_run, l_run, acc = carry
        start = pl.multiple_of(kt * tk, tk)
        k = ks_ref[pl.ds(start, tk), :]
        v = vs_ref[pl.ds(start, tk), :]
        s = lax.dot_general(q, k, nt, preferred_element_type=F32).reshape(r, qb, tk)
        ei = lax.broadcasted_iota(jnp.int32, (nb, tk), 0)
        ej = lax.broadcasted_iota(jnp.int32, (nb, tk), 1)
        expand = jnp.where(ei == kt * bpt + ej // SLC_BLOCK, 1.0, 0.0).astype(BF16)
        picked = jnp.dot(sel_bf, expand, preferred_element_type=F32)
        kpos = start + lax.broadcasted_iota(jnp.int32, (qb, tk), 1)
        mask = ((picked > 0.5) & (kpos <= t_col))[None]
        s = jnp.where(mask, s, -1e30)
        m_new = jnp.maximum(m_run, jnp.max(s, axis=-1, keepdims=True))
        alpha = jnp.exp(m_run - m_new)
        p = jnp.where(mask, jnp.exp(s - m_new), 0.0)
        l_new = alpha * l_run + jnp.sum(p, axis=-1, keepdims=True)
        pv = jnp.dot(p.reshape(r * qb, tk).astype(BF16), v, preferred_element_type=F32)
        acc = alpha.reshape(r * qb, 1) * acc + pv
        return m_new, l_new, acc

    init = (jnp.full((r, qb, 1), -1e30, F32), jnp.zeros((r, qb, 1), F32), jnp.zeros((r * qb, d), F32))
    _, l_fin, acc = lax.fori_loop(0, (qi * qb) // tk + 1, slc_step, init)
    o_slc = acc / jnp.maximum(l_fin.reshape(r * qb, 1), 1e-30)

    k_tiles, v_tiles, pos_tiles = [], [], []
    for j in range(NSA_WTILES):
        kt = qi - (NSA_WTILES - 1) + j
        ktc = jnp.maximum(kt, 0)
        start = pl.multiple_of(ktc * qb, qb)
        k_tiles.append(kw_ref[pl.ds(start, qb), :])
        v_tiles.append(vw_ref[pl.ds(start, qb), :])
        lane = lax.broadcasted_iota(jnp.int32, (qb, qb), 1)
        pos_tiles.append(jnp.where(kt >= 0, start + lane, -1))
    k_w = jnp.concatenate(k_tiles, axis=0)
    v_w = jnp.concatenate(v_tiles, axis=0)
    k_pos = jnp.concatenate(pos_tiles, axis=1)
    span = NSA_WTILES * qb
    s = lax.dot_general(q, k_w, nt, preferred_element_type=F32).reshape(r, qb, span)
    dpos = t_col - k_pos
    w_mask = ((dpos >= 0) & (dpos < WINDOW) & (k_pos >= 0))[None]
    p_w = _masked_softmax_rows(s, w_mask)
    o_win = jnp.dot(p_w.reshape(r * qb, span).astype(BF16), v_w, preferred_element_type=F32)

    sig = jax.nn.sigmoid(gate_ref[...])
    lane = lax.broadcasted_iota(jnp.int32, sig.shape, 1)
    for h in range(r):
        rows = slice(h * qb, (h + 1) * qb)
        out = jnp.zeros((qb, d), F32)
        for branch, o_b in enumerate((o_cmp, o_slc, o_win)):
            col = branch * NSA_HEADS + g * r + h
            gate = jnp.sum(jnp.where(lane == col, sig, 0.0), axis=-1, keepdims=True)
            out = out + gate * o_b[rows]
        o_ref[:, h * d:(h + 1) * d] = out.astype(o_ref.dtype)


def nsa_prompt(q_bf, kvc, rows_bf, win_bf, proj, n_batch, t_len):
    d, r, qb = NSA_HEAD_DIM, NSA_GROUP, Q_BLOCK
    nqb = t_len // qb
    n_sub = kvc.shape[2]
    gate_blk = GATE_COL0 // LANES
    seq = lambda col: pl.BlockSpec((t_len, d), lambda b, g, i: (b, col(g)))
    return pl.pallas_call(
        _nsa_prompt_kernel,
        grid=(n_batch, NSA_KV_HEADS, nqb),
        in_specs=[pl.BlockSpec((qb, r * d), lambda b, g, i: (b * nqb + i, g)),
                  pl.BlockSpec((1, 1, n_sub, d), lambda b, g, i: (b, g, 0, 0)),
                  pl.BlockSpec((1, 1, n_sub, d), lambda b, g, i: (b, 2 + g, 0, 0)),
                  seq(lambda g: 4 + g), seq(lambda g: 6 + g),
                  seq(lambda g: g), seq(lambda g: 2 + g),
                  pl.BlockSpec((qb, LANES), lambda b, g, i: (b * nqb + i, gate_blk))],
        out_specs=pl.BlockSpec((qb, r * d), lambda b, g, i: (b * nqb + i, g)),
        out_shape=jax.ShapeDtypeStruct((n_batch * t_len, E_Q), BF16),
        compiler_params=pltpu.CompilerParams(
            dimension_semantics=("parallel", "parallel", "arbitrary"), vmem_limit_bytes=VMEM_LIMIT),
        name="nsa_prompt",
    )(q_bf, kvc, kvc, rows_bf, rows_bf, win_bf, win_bf, proj)


SAMPLE_PAGES = 16
NEW_ROWS_BLK = 16


def _page_specs(kind):
    def spec(i):
        return pl.BlockSpec((1, PAGE_SIZE, 1, NSA_KV_HEADS, NSA_HEAD_DIM),
                            lambda s, c, pt: (pt[s, c * SAMPLE_PAGES + i], 0, kind, 0, 0))
    return [spec(i) for i in range(SAMPLE_PAGES)]


def _sample_compress_kernel(pt_ref, *refs):
    n_in = 2 * SAMPLE_PAGES
    pages = refs[:n_in]
    pe_ref, w1_ref, a0_ref, a1_ref = refs[n_in:]
    d = NSA_HEAD_DIM
    per_page = PAGE_SIZE // CMP_STRIDE
    rows = SAMPLE_PAGES * per_page
    gs = NSA_KV_HEADS * CMP_STRIDE
    for kv in range(2):
        by_offset = [jnp.swapaxes(pages[kv * SAMPLE_PAGES + p].reshape(NSA_KV_HEADS * PAGE_SIZE, d)[...]
                                  .reshape(per_page, gs, d), 0, 1) for p in range(SAMPLE_PAGES)]
        acc0 = jnp.zeros((NSA_KV_HEADS * rows, d), F32)
        acc1 = jnp.zeros((NSA_KV_HEADS * rows, d), F32)
        for s in range(CMP_STRIDE):
            xs = jnp.concatenate([by_offset[p][NSA_KV_HEADS * s + g]
                                  for g in range(NSA_KV_HEADS) for p in range(SAMPLE_PAGES)],
                                 axis=0)
            a0 = (xs + pe_ref[kv, s:s + 1, :]).astype(BF16)
            a1 = (xs + pe_ref[kv, CMP_STRIDE + s:CMP_STRIDE + s + 1, :]).astype(BF16)
            acc0 = acc0 + jnp.dot(a0, w1_ref[kv, s], preferred_element_type=F32)
            acc1 = acc1 + jnp.dot(a1, w1_ref[kv, CMP_STRIDE + s], preferred_element_type=F32)
        for g in range(NSA_KV_HEADS):
            a0_ref[0, kv * NSA_KV_HEADS + g] = acc0[g * rows:(g + 1) * rows]
            a1_ref[0, kv * NSA_KV_HEADS + g] = acc1[g * rows:(g + 1) * rows]


def nsa_sample_compress(cache, page_table, cmp_pe, cmp_w1):
    bs, n_pages = page_table.shape
    d = NSA_HEAD_DIM
    per_page = PAGE_SIZE // CMP_STRIDE
    n_sub = n_pages * per_page
    rows = SAMPLE_PAGES * per_page
    w1 = cmp_w1.reshape(2, CMP_BLOCK, d, d).astype(BF16)
    out = jax.ShapeDtypeStruct((bs, 4, n_sub, d), F32)
    ospec = pl.BlockSpec((1, 4, rows, d), lambda s, c, pt: (s, 0, c, 0))
    page_specs = _page_specs(0) + _page_specs(1)
    return pl.pallas_call(
        _sample_compress_kernel,
        grid_spec=pltpu.PrefetchScalarGridSpec(
            num_scalar_prefetch=1,
            grid=(bs, n_pages // SAMPLE_PAGES),
            in_specs=page_specs + [
                pl.BlockSpec((2, CMP_BLOCK, d), lambda s, c, pt: (0, 0, 0)),
                pl.BlockSpec((2, CMP_BLOCK, d, d), lambda s, c, pt: (0, 0, 0, 0))],
            out_specs=[ospec, ospec]),
        out_shape=[out, out],
        compiler_params=pltpu.CompilerParams(
            dimension_semantics=("parallel", "arbitrary"), vmem_limit_bytes=VMEM_LIMIT),
        name="nsa_sample_compress",
    )(page_table, *([cache] * (2 * SAMPLE_PAGES)), cmp_pe, w1)


def _sample_select_kernel(a0_ref, a1_ref, w2_ref, q_ref, ocmp_ref, sel_ref, *, past_len, t_len):
    d, r = NSA_HEAD_DIM, NSA_GROUP
    n_sub = a0_ref.shape[2]
    n_cmp = n_sub - 1
    n_slc = -(-(past_len + t_len) // SLC_BLOCK)
    lanes = sel_ref.shape[3]
    nt = (((1,), (1,)), ((), ()))
    rq = r * t_len
    for g in range(NSA_KV_HEADS):
        kc = jnp.dot(_gelu_tanh(a0_ref[0, g] + pltpu.roll(a1_ref[0, g], n_sub - 1, 0)).astype(BF16), w2_ref[0],
                     preferred_element_type=F32).astype(BF16)
        vc = jnp.dot(_gelu_tanh(a0_ref[0, 2 + g] + pltpu.roll(a1_ref[0, 2 + g], n_sub - 1, 0)).astype(BF16),
                     w2_ref[1], preferred_element_type=F32).astype(BF16)
        q = q_ref[0, g]
        s = lax.dot_general(q, kc, nt, preferred_element_type=F32)
        pos = past_len + lax.broadcasted_iota(jnp.int32, (rq, 1), 0) % t_len
        n_idx = lax.broadcasted_iota(jnp.int32, (rq, n_sub), 1)
        p_c = _masked_softmax_rows(s, (n_idx * CMP_STRIDE + (CMP_BLOCK - 1) <= pos) & (n_idx < n_cmp))
        ocmp_ref[0, g] = jnp.dot(p_c.astype(BF16), vc, preferred_element_type=F32)
        ri = lax.broadcasted_iota(jnp.int32, (8, rq), 0)
        cj = lax.broadcasted_iota(jnp.int32, (8, rq), 1)
        head_sum = jnp.where(cj % t_len == ri, 1.0, 0.0).astype(BF16)
        p_hi = p_c.astype(BF16)
        p_lo = (p_c - p_hi.astype(F32)).astype(BF16)
        p_sum = (jnp.dot(head_sum, p_hi, preferred_element_type=F32)
                 + jnp.dot(head_sum, p_lo, preferred_element_type=F32))
        ci = lax.broadcasted_iota(jnp.int32, (n_sub, lanes), 0)
        mi = lax.broadcasted_iota(jnp.int32, (n_sub, lanes), 1)
        overlap = jnp.where((ci * CMP_STRIDE < mi * SLC_BLOCK + SLC_BLOCK)
                            & (ci * CMP_STRIDE + CMP_BLOCK > mi * SLC_BLOCK) & (ci < n_cmp), 1.0, 0.0).astype(BF16)
        s_hi = p_sum.astype(BF16)
        s_lo = (p_sum - s_hi.astype(F32)).astype(BF16)
        imp = (jnp.dot(s_hi, overlap, preferred_element_type=F32)
               + jnp.dot(s_lo, overlap, preferred_element_type=F32))
        blk = lax.broadcasted_iota(jnp.int32, (8, lanes), 1)
        blk_f = blk.astype(F32)
        tpos = past_len + lax.broadcasted_iota(jnp.int32, (8, 1), 0) % t_len
        cur = tpos // SLC_BLOCK
        forced = (blk == 0) | (blk == cur) | (blk == cur - 1)
        work = jnp.where(blk * SLC_BLOCK <= tpos, jnp.where(forced, 1e6, imp), -1e6)
        work = jnp.where(blk < n_slc, work, NEG_INF)
        sel = jnp.zeros((8, lanes), F32)
        for _ in range(min(SLC_TOPK, n_slc)):
            m = jnp.max(work, axis=-1, keepdims=True)
            first = jnp.min(jnp.where(work == m, blk_f, float(lanes)), axis=-1, keepdims=True)
            pick = blk_f == first
            sel = jnp.where(pick, 1.0, sel)
            work = jnp.where(pick, NEG_INF, work)
        sel_ref[0, g] = sel


def nsa_sample_select(a0, a1, cmp_w2, q_s, past_len, t_len):
    bs, _, n_sub, d = a0.shape
    n_slc = -(-(past_len + t_len) // SLC_BLOCK)
    lanes = _round_up(n_slc, LANES)
    rq = NSA_GROUP * t_len
    aspec = pl.BlockSpec((1, 4, n_sub, d), lambda s: (s, 0, 0, 0))
    return pl.pallas_call(
        functools.partial(_sample_select_kernel, past_len=past_len, t_len=t_len),
        grid=(bs,),
        in_specs=[aspec, aspec, pl.BlockSpec((2, d, d), lambda s: (0, 0, 0)),
                  pl.BlockSpec((1, NSA_KV_HEADS, rq, d), lambda s: (s, 0, 0, 0))],
        out_specs=[pl.BlockSpec((1, NSA_KV_HEADS, rq, d), lambda s: (s, 0, 0, 0)),
                   pl.BlockSpec((1, NSA_KV_HEADS, 8, lanes), lambda s: (s, 0, 0, 0))],
        out_shape=[jax.ShapeDtypeStruct((bs, NSA_KV_HEADS, rq, d), F32),
                   jax.ShapeDtypeStruct((bs, NSA_KV_HEADS, 8, lanes), F32)],
        compiler_params=pltpu.CompilerParams(dimension_semantics=("parallel",), vmem_limit_bytes=VMEM_LIMIT),
        name="nsa_sample_select",
    )(a0, a1, cmp_w2.astype(BF16), q_s)


def _sample_attend_kernel(pt_ref, *refs, past_len, t_len, row0):
    k_pages = refs[:SAMPLE_PAGES]
    v_pages = refs[SAMPLE_PAGES:2 * SAMPLE_PAGES]
    (q_ref, sel_ref, ocmp_ref, newrows_ref, kwin_ref, vwin_ref, newwin_ref, gate_ref, o_ref,
     m_scr, l_scr, acc_scr) = refs[2 * SAMPLE_PAGES:]
    s_idx = pl.program_id(0)
    c = pl.program_id(1)
    d, r = NSA_HEAD_DIM, NSA_GROUP
    rq = r * t_len
    lanes = sel_ref.shape[3]
    tk = SAMPLE_PAGES * PAGE_SIZE
    nt = (((1,), (1,)), ((), ()))

    @pl.when(c == 0)
    def _():
        m_scr[...] = jnp.full(m_scr.shape, -1e30, F32)
        l_scr[...] = jnp.zeros(l_scr.shape, F32)
        acc_scr[...] = jnp.zeros(acc_scr.shape, F32)

    row = lax.broadcasted_iota(jnp.int32, (rq, 1), 0)
    t_row = row % t_len
    pos = past_len + t_row
    ti = lax.broadcasted_iota(jnp.int32, (rq, 8), 1)
    tok_expand = jnp.where(ti == t_row, 1.0, 0.0).astype(BF16)

    def online(g, s, mask, v):
        s = jnp.where(mask, s, -1e30)
        m_old = m_scr[g]
        m_new = jnp.maximum(m_old, jnp.max(s, axis=-1, keepdims=True))
        alpha = jnp.exp(m_old - m_new)
        p = jnp.where(mask, jnp.exp(s - m_new), 0.0)
        l_scr[g] = alpha * l_scr[g] + jnp.sum(p, axis=-1, keepdims=True)
        acc_scr[g] = alpha * acc_scr[g] + jnp.dot(p.astype(BF16), v, preferred_element_type=F32)
        m_scr[g] = m_new

    sel16 = []
    for g in range(NSA_KV_HEADS):
        sel16.append(jnp.dot(tok_expand, sel_ref[0, g].astype(BF16), preferred_element_type=F32))
        k = jnp.concatenate([pg.reshape(NSA_KV_HEADS * PAGE_SIZE, d)[pl.ds(g, PAGE_SIZE, stride=NSA_KV_HEADS), :]
                             for pg in k_pages], axis=0)
        v = jnp.concatenate([pg.reshape(NSA_KV_HEADS * PAGE_SIZE, d)[pl.ds(g, PAGE_SIZE, stride=NSA_KV_HEADS), :]
                             for pg in v_pages], axis=0)
        q = q_ref[0, g]
        s = lax.dot_general(q, k.astype(BF16), nt, preferred_element_type=F32)
        ei = lax.broadcasted_iota(jnp.int32, (lanes, tk), 0)
        ej = lax.broadcasted_iota(jnp.int32, (lanes, tk), 1)
        expand = jnp.where(ei == c * (tk // SLC_BLOCK) + ej // SLC_BLOCK, 1.0, 0.0).astype(BF16)
        picked = jnp.dot(sel16[g].astype(BF16), expand, preferred_element_type=F32)
        kpos = c * tk + lax.broadcasted_iota(jnp.int32, (rq, tk), 1)
        online(g, s, (picked > 0.5) & (kpos <= pos), v.astype(BF16))

    @pl.when(c == pl.num_programs(1) - 1)
    def _():
        mine = (row0 // t_len + s_idx) % (NEW_ROWS_BLK // t_len)
        j = lax.broadcasted_iota(jnp.int32, (rq, NEW_ROWS_BLK), 1)
        own = j // t_len == mine
        new_pos = past_len + j % t_len
        sig = jax.nn.sigmoid(gate_ref[...])
        gsel = jnp.where(j == mine * t_len + t_row, 1.0, 0.0).astype(BF16)
        s_hi = sig.astype(BF16)
        s_lo = (sig - s_hi.astype(F32)).astype(BF16)
        sig_rows = (jnp.dot(gsel, s_hi, preferred_element_type=F32)
                    + jnp.dot(gsel, s_lo, preferred_element_type=F32))
        lane = lax.broadcasted_iota(jnp.int32, sig_rows.shape, 1)
        lane_l = lax.broadcasted_iota(jnp.int32, (rq, lanes), 1)
        for g in range(NSA_KV_HEADS):
            q = q_ref[0, g]
            kn = newrows_ref[:, (4 + g) * d:(5 + g) * d].astype(BF16)
            vn = newrows_ref[:, (6 + g) * d:(7 + g) * d].astype(BF16)
            s = lax.dot_general(q, kn, nt, preferred_element_type=F32)
            last_picked = jnp.sum(jnp.where(lane_l == past_len // SLC_BLOCK, sel16[g], 0.0), axis=-1, keepdims=True)
            online(g, s, own & (new_pos <= pos) & (last_picked > 0.5), vn)
            o_slc = acc_scr[g] / jnp.maximum(l_scr[g], 1e-30)
            w_len = kwin_ref.shape[1]
            win_rows = pl.ds(g, w_len, stride=NSA_KV_HEADS)
            kw = kwin_ref.reshape(NSA_KV_HEADS * w_len, d)[win_rows, :].astype(BF16)
            vw = vwin_ref.reshape(NSA_KV_HEADS * w_len, d)[win_rows, :].astype(BF16)
            s1 = lax.dot_general(q, kw, nt, preferred_element_type=F32)
            kp1 = past_len - w_len + lax.broadcasted_iota(jnp.int32, (rq, w_len), 1)
            d1 = pos - kp1
            m1 = (d1 >= 0) & (d1 < WINDOW) & (kp1 >= 0)
            knw = newwin_ref[:, g * d:(g + 1) * d].astype(BF16)
            vnw = newwin_ref[:, (2 + g) * d:(3 + g) * d].astype(BF16)
            s2 = lax.dot_general(q, knw, nt, preferred_element_type=F32)
            d2 = pos - new_pos
            m2 = own & (d2 >= 0) & (d2 < WINDOW)
            s1 = jnp.where(m1, s1, -1e30)
            s2 = jnp.where(m2, s2, -1e30)
            mx = jnp.maximum(jnp.max(s1, axis=-1, keepdims=True), jnp.max(s2, axis=-1, keepdims=True))
            p1 = jnp.where(m1, jnp.exp(s1 - mx), 0.0)
            p2 = jnp.where(m2, jnp.exp(s2 - mx), 0.0)
            den = jnp.sum(p1, axis=-1, keepdims=True) + jnp.sum(p2, axis=-1, keepdims=True)
            o_win = (jnp.dot(p1.astype(BF16), vw, preferred_element_type=F32)
                     + jnp.dot(p2.astype(BF16), vnw, preferred_element_type=F32)) / jnp.maximum(den, 1e-30)
            out = jnp.zeros((rq, d), F32)
            for branch, o_b in enumerate((ocmp_ref[0, g], o_slc, o_win)):
                col = branch * NSA_HEADS + g * r + row // t_len
                gate = jnp.sum(jnp.where(lane == col, sig_rows, 0.0), axis=-1, keepdims=True)
                out = out + gate * o_b
            o_ref[0, g] = out


def nsa_sample_attend(cache, page_table, q_s, sel, o_cmp, rows, win, win_cache, proj, past_len, t_len, row0):
    bs, n_pages = page_table.shape
    d = NSA_HEAD_DIM
    rq = NSA_GROUP * t_len
    lanes = sel.shape[3]
    w_len = win_cache.shape[1]
    assert row0 % t_len == 0 and NEW_ROWS_BLK % t_len == 0 and past_len % SLC_BLOCK == 0
    blk = lambda s: (row0 + s * t_len) // NEW_ROWS_BLK
    per_seq = lambda shape: pl.BlockSpec((1,) + shape, lambda s, c, pt: (s, 0, 0, 0))
    return pl.pallas_call(
        functools.partial(_sample_attend_kernel, past_len=past_len, t_len=t_len, row0=row0),
        grid_spec=pltpu.PrefetchScalarGridSpec(
            num_scalar_prefetch=1,
            grid=(bs, n_pages // SAMPLE_PAGES),
            in_specs=_page_specs(2) + _page_specs(3) + [
                per_seq((NSA_KV_HEADS, rq, d)), per_seq((NSA_KV_HEADS, 8, lanes)), per_seq((NSA_KV_HEADS, rq, d)),
                pl.BlockSpec((NEW_ROWS_BLK, 8 * d), lambda s, c, pt: (blk(s), 0)),
                pl.BlockSpec((1, w_len, 1, NSA_KV_HEADS, d), lambda s, c, pt: (s, 0, 0, 0, 0)),
                pl.BlockSpec((1, w_len, 1, NSA_KV_HEADS, d), lambda s, c, pt: (s, 0, 1, 0, 0)),
                pl.BlockSpec((NEW_ROWS_BLK, 4 * d), lambda s, c, pt: (blk(s), 0)),
                pl.BlockSpec((NEW_ROWS_BLK, LANES), lambda s, c, pt: (blk(s), GATE_COL0 // LANES))],
            out_specs=per_seq((NSA_KV_HEADS, rq, d)),
            scratch_shapes=[pltpu.VMEM((NSA_KV_HEADS, rq, 1), F32), pltpu.VMEM((NSA_KV_HEADS, rq, 1), F32),
                            pltpu.VMEM((NSA_KV_HEADS, rq, d), F32)]),
        out_shape=jax.ShapeDtypeStruct((bs, NSA_KV_HEADS, rq, d), F32),
        compiler_params=pltpu.CompilerParams(
            dimension_semantics=("parallel", "arbitrary"), vmem_limit_bytes=VMEM_LIMIT),
        name="nsa_sample_attend",
    )(page_table, *([cache] * (2 * SAMPLE_PAGES)), q_s, sel, o_cmp, rows, win_cache, win_cache, win, proj)


SC_ROWS = 512
SC_COLS = 512


def _short_conv_kernel(b_ref, c_ref, h_ref, w_ref, buf_ref, o_ref, tail_ref, prev_scr):
    @pl.when(pl.program_id(2) == 0)
    def _():
        prev_scr[...] = buf_ref[0]

    rows = c_ref.shape[0]
    u = c_ref[...] * h_ref[...]
    prev = prev_scr[...]
    row8 = lax.broadcasted_iota(jnp.int32, prev.shape, 0)
    conv = u * w_ref[SC_KSIZE - 1:SC_KSIZE, :]
    for k in range(1, SC_KSIZE):
        rolled = pltpu.roll(u, k, 0)
        top = jnp.where(row8 < k, pltpu.roll(prev, k, 0), rolled[0:8])
        shifted = top if rows == 8 else jnp.concatenate([top, rolled[8:]], axis=0)
        conv = conv + shifted * w_ref[SC_KSIZE - 1 - k:SC_KSIZE - k, :]
    prev_scr[...] = u[rows - 8:rows]
    o_ref[...] = (b_ref[...] * conv).astype(o_ref.dtype)
    tail_ref[0] = u[rows - 8:rows]


def short_conv(proj, n_batch, t_len, rows_per_step, conv_buf, sc_w):
    nr = t_len // rows_per_step
    nh = SC_WIDTH // SC_COLS
    blk0 = SC_COL0 // SC_COLS
    assert SC_COL0 % SC_COLS == 0
    buf8 = jnp.pad(conv_buf, ((0, 0), (8 - (SC_KSIZE - 1), 0), (0, 0)))
    col = lambda part: pl.BlockSpec((rows_per_step, SC_COLS), lambda b, j, i: (b * nr + i, blk0 + part * nh + j))
    return pl.pallas_call(
        _short_conv_kernel,
        grid=(n_batch, nh, nr),
        in_specs=[col(0), col(1), col(2), pl.BlockSpec((SC_KSIZE, SC_COLS), lambda b, j, i: (0, j)),
                  pl.BlockSpec((1, 8, SC_COLS), lambda b, j, i: (b, 0, j))],
        out_specs=[pl.BlockSpec((rows_per_step, SC_COLS), lambda b, j, i: (b * nr + i, j)),
                   pl.BlockSpec((1, 8, SC_COLS), lambda b, j, i: (b, 0, j))],
        out_shape=[jax.ShapeDtypeStruct((n_batch * t_len, SC_WIDTH), BF16),
                   jax.ShapeDtypeStruct((n_batch, 8, SC_WIDTH), F32)],
        scratch_shapes=[pltpu.VMEM((8, SC_COLS), F32)],
        compiler_params=pltpu.CompilerParams(
            dimension_semantics=("parallel", "parallel", "arbitrary"), vmem_limit_bytes=VMEM_LIMIT),
        name="short_conv",
    )(proj, proj, proj, sc_w, buf8)


def _ret_tables(pos, chunk, valid):
    half = RET_DQK // 2
    inv = RET_THETA ** (-jnp.arange(half, dtype=F32) / half)
    ang = pos.astype(F32)[:, None] * inv[None, :]
    log_g = jnp.log1p(-(2.0 ** (-5.0 - jnp.arange(RET_HEADS, dtype=F32))))
    i = jnp.arange(chunk, dtype=F32)
    diff = i[:, None] - i[None, :]
    intra = jnp.where(diff >= 0, jnp.exp(jnp.maximum(diff, 0.0)[None] * log_g[:, None, None]), 0.0)
    q_dec = jnp.exp((i[None, :] + 1.0) * log_g[:, None])[..., None]
    k_dec = jnp.exp((valid - 1.0 - i)[None, :] * log_g[:, None])[..., None]
    c_dec = jnp.exp(valid * log_g)[:, None, None]
    return jnp.cos(ang), jnp.sin(ang), intra, q_dec, k_dec, c_dec


def _retention_kernel(q_ref, k_ref, v_ref, g_ref, cos_ref, sin_ref, intra_ref, qd_ref, kd_ref, cd_ref, gn_ref,
                      s0_ref, o_ref, s_out_ref, s_scr):
    c_idx = pl.program_id(1)

    @pl.when(c_idx == 0)
    def _():
        s_scr[...] = s0_ref[0]

    cos, sin = cos_ref[...], sin_ref[...]
    half = RET_DQK // 2
    nt = (((1,), (1,)), ((), ()))
    tn = (((0,), (0,)), ((), ()))

    def rot(x):
        x1, x2 = x[:, :half], x[:, half:]
        return jnp.concatenate([x1 * cos - x2 * sin, x1 * sin + x2 * cos], axis=-1)

    for h in range(RET_HEADS):
        cols = slice(h * RET_DQK, (h + 1) * RET_DQK)
        qr = (rot(q_ref[:, cols]) * (RET_DQK ** -0.5)).astype(BF16)
        kr = rot(k_ref[:, cols])
        v = v_ref[:, cols].astype(BF16)
        att = lax.dot_general(qr, kr.astype(BF16), nt, preferred_element_type=F32) * intra_ref[h]
        s_old = s_scr[h]
        o = (jnp.dot(att.astype(BF16), v, preferred_element_type=F32)
             + jnp.dot(qr, s_old.astype(BF16), preferred_element_type=F32) * qd_ref[h])
        s_scr[h] = s_old * cd_ref[h] + lax.dot_general((kr * kd_ref[h]).astype(BF16), v, tn,
                                                       preferred_element_type=F32)
        mu = jnp.mean(o, axis=-1, keepdims=True)
        dev = o - mu
        var = jnp.mean(dev * dev, axis=-1, keepdims=True)
        gate = g_ref[:, cols]
        on = dev * lax.rsqrt(var + EPS) * gn_ref[:, cols] * (gate * jax.nn.sigmoid(gate))
        o_ref[:, cols] = on.astype(o_ref.dtype)
    s_out_ref[0] = s_scr[...]


def retention(proj, pos, n_batch, t_len, chunk, state0, gn_gain, valid=None):
    nc = t_len // chunk
    cos, sin, intra, q_dec, k_dec, c_dec = _ret_tables(pos, chunk, chunk if valid is None else valid)
    half = RET_DQK // 2
    col = lambda j: pl.BlockSpec((chunk, R_QK), lambda b, c: (b * nc + c, j))
    tab = pl.BlockSpec((chunk, half), lambda b, c: (c, 0))
    full = lambda a: pl.BlockSpec(a.shape, lambda b, c: (0,) * a.ndim)
    st = pl.BlockSpec((1, RET_HEADS, RET_DQK, RET_DV), lambda b, c: (b, 0, 0, 0))
    return pl.pallas_call(
        _retention_kernel,
        grid=(n_batch, nc),
        in_specs=[col(0), col(1), col(2), col(3), tab, tab, full(intra), full(q_dec), full(k_dec), full(c_dec),
                  pl.BlockSpec((1, R_V), lambda b, c: (0, 0)), st],
        out_specs=[pl.BlockSpec((chunk, R_V), lambda b, c: (b * nc + c, 0)), st],
        out_shape=[jax.ShapeDtypeStruct((n_batch * t_len, R_V), BF16),
                   jax.ShapeDtypeStruct((n_batch, RET_HEADS, RET_DQK, RET_DV), F32)],
        scratch_shapes=[pltpu.VMEM((RET_HEADS, RET_DQK, RET_DV), F32)],
        compiler_params=pltpu.CompilerParams(
            dimension_semantics=("parallel", "arbitrary"), vmem_limit_bytes=VMEM_LIMIT),
        name="retention",
    )(proj, proj, proj, proj, cos, sin, intra, q_dec, k_dec, c_dec, gn_gain.reshape(1, R_V), state0)


SSD_COL0 = 2 * R_QK + 2 * R_V
HEADS_PER_GROUP = M2_HEADS // M2_GROUPS


def _split3(x):
    a = x.astype(BF16)
    r = x - a.astype(F32)
    b = r.astype(BF16)
    c = (r - b.astype(F32)).astype(BF16)
    return a, b, c


def _exact_dot(mat_bf, x):
    out = None
    for piece in _split3(x):
        t = jnp.dot(mat_bf, piece, preferred_element_type=F32)
        out = t if out is None else out + t
    return out


def _exact_dot_r(x, mat_bf):
    out = None
    for piece in _split3(x):
        t = jnp.dot(piece, mat_bf, preferred_element_type=F32)
        out = t if out is None else out + t
    return out


def _ssd_kernel(z_ref, xa_ref, xb_ref, xc_ref, dt_ref, cw_ref, cb_ref, dtb_ref, aneg_ref, dskip_ref, norm_ref,
                buf0_ref, s0_ref, o_ref, s_out_ref, s_scr, prev_scr, *, valid):
    c_idx = pl.program_id(1)
    chunk = z_ref.shape[0]
    nt = (((1,), (1,)), ((), ()))
    tn = (((0,), (0,)), ((), ()))

    @pl.when(c_idx == 0)
    def _():
        s_scr[...] = s0_ref[0]
        prev_scr[...] = buf0_ref[0]

    x = jnp.concatenate([xa_ref[...], xb_ref[...], xc_ref[...]], axis=1)
    prev = prev_scr[...]
    row8 = lax.broadcasted_iota(jnp.int32, prev.shape, 0)
    conv = x * cw_ref[M2_CONV - 1:M2_CONV, :]
    for k in range(1, M2_CONV):
        rolled = pltpu.roll(x, k, 0)
        top = jnp.where(row8 < k, pltpu.roll(prev, k, 0), rolled[0:8])
        shifted = top if chunk == 8 else jnp.concatenate([top, rolled[8:]], axis=0)
        conv = conv + shifted * cw_ref[M2_CONV - 1 - k:M2_CONV - k, :]
    prev_scr[...] = x[chunk - 8:chunk]
    conv = conv + cb_ref[...]
    xbc = conv * jax.nn.sigmoid(conv)
    xs = xbc[:, :M2_DINNER]

    dt_raw = dt_ref[...] + dtb_ref[...]
    dt = jnp.where(dt_raw > 20.0, dt_raw, jnp.log1p(jnp.exp(jnp.minimum(dt_raw, 20.0))))
    if valid < chunk:
        dt = jnp.where(lax.broadcasted_iota(jnp.int32, dt.shape, 0) < valid, dt, 0.0)
    a = dt * aneg_ref[...]
    ri = lax.broadcasted_iota(jnp.int32, (chunk, chunk), 0)
    ci = lax.broadcasted_iota(jnp.int32, (chunk, chunk), 1)
    tri = ri >= ci
    cum = _exact_dot(jnp.where(tri, 1.0, 0.0).astype(BF16), a)
    cum_t = cum.T
    cum_last = cum[chunk - 1:chunk, :]
    hi = lax.broadcasted_iota(jnp.int32, (LANES, M2_DINNER), 0)
    li = lax.broadcasted_iota(jnp.int32, (LANES, M2_DINNER), 1)
    expand = jnp.where(hi == li // M2_HEADDIM, 1.0, 0.0).astype(BF16)
    dt_x = _exact_dot_r(dt, expand)
    cum_x = _exact_dot_r(cum, expand)
    last_x = _exact_dot_r(cum_last, expand)
    xdt = xs * dt_x
    x_dec = (xdt * jnp.exp(last_x - cum_x)).astype(BF16)
    xdt_bf = xdt.astype(BF16)
    e_cum_x = jnp.exp(cum_x)
    e_last_x = jnp.exp(last_x)

    y_parts = []
    for gi in range(M2_GROUPS):
        b_g = xbc[:, M2_DINNER + gi * M2_STATE:M2_DINNER + (gi + 1) * M2_STATE].astype(BF16)
        c_g = xbc[:, M2_DINNER + (M2_GROUPS + gi) * M2_STATE:M2_DINNER + (M2_GROUPS + gi + 1) * M2_STATE].astype(BF16)
        cb = lax.dot_general(c_g, b_g, nt, preferred_element_type=F32)
        gcols = slice(gi * HEADS_PER_GROUP * M2_HEADDIM, (gi + 1) * HEADS_PER_GROUP * M2_HEADDIM)
        s_old = s_scr[:, gcols]
        y_state = jnp.dot(c_g, s_old.astype(BF16), preferred_element_type=F32) * e_cum_x[:, gcols]
        s_scr[:, gcols] = s_old * e_last_x[:, gcols] + lax.dot_general(b_g, x_dec[:, gcols], tn,
                                                                        preferred_element_type=F32)
        pair_lane = lax.broadcasted_iota(jnp.int32, (chunk, LANES), 1)
        intra = []
        for pr in range(HEADS_PER_GROUP // 2):
            outs = []
            for sub in range(2):
                h = gi * HEADS_PER_GROUP + pr * 2 + sub
                seg = cum[:, h:h + 1] - cum_t[h:h + 1, :]
                l_mat = jnp.where(tri, jnp.exp(jnp.where(tri, seg, 0.0)), 0.0)
                lanes = slice(gi * HEADS_PER_GROUP * M2_HEADDIM + pr * LANES,
                              gi * HEADS_PER_GROUP * M2_HEADDIM + (pr + 1) * LANES)
                outs.append(jnp.dot((cb * l_mat).astype(BF16), xdt_bf[:, lanes], preferred_element_type=F32))
            intra.append(jnp.where(pair_lane < M2_HEADDIM, outs[0], outs[1]))
        y_parts.append(jnp.concatenate(intra, axis=1) + y_state)
    y = jnp.concatenate(y_parts, axis=1) + dskip_ref[...] * xs
    z = z_ref[...]
    y = y * (z * jax.nn.sigmoid(z))
    gw = M2_DINNER // M2_GROUPS
    outs = []
    for gi in range(M2_GROUPS):
        yg = y[:, gi * gw:(gi + 1) * gw]
        outs.append(yg * lax.rsqrt(jnp.mean(yg * yg, axis=-1, keepdims=True) + EPS))
    o_ref[...] = (jnp.concatenate(outs, axis=1) * norm_ref[...]).astype(o_ref.dtype)
    s_out_ref[0] = s_scr[...]


def ssd(proj, n_batch, t_len, chunk, conv_buf, state0, conv_w, conv_b, dt_bias, a_log, d_skip, m2_norm, valid=None):
    nc = t_len // chunk
    z_blk = SSD_COL0 // M2_DINNER
    xw = M2_CONV_DIM // 3
    xbc_blk = (SSD_COL0 + M2_DINNER) // xw
    assert (SSD_COL0 + M2_DINNER) % xw == 0 and xw % LANES == 0
    dt_blk = (SSD_COL0 + M2_DINNER + M2_CONV_DIM) // LANES
    pad = lambda v: jnp.pad(v.reshape(1, -1), ((0, 0), (0, LANES - v.shape[-1])))
    buf8 = jnp.pad(conv_buf, ((0, 0), (8 - (M2_CONV - 1), 0), (0, 0)))
    st_t = state0.transpose(0, 3, 1, 2).reshape(n_batch, M2_STATE, M2_DINNER)
    row = lambda a: pl.BlockSpec(a.shape, lambda b, c: (0, 0))
    cw = conv_w
    cb = conv_b.reshape(1, -1)
    dtb, aneg = pad(dt_bias), pad(-jnp.exp(a_log))
    dsk = jnp.repeat(d_skip, M2_HEADDIM).reshape(1, -1)
    nrm = m2_norm.reshape(1, -1)
    st = pl.BlockSpec((1, M2_STATE, M2_DINNER), lambda b, c: (b, 0, 0))
    out, s_fin = pl.pallas_call(
        functools.partial(_ssd_kernel, valid=chunk if valid is None else valid),
        grid=(n_batch, nc),
        in_specs=[pl.BlockSpec((chunk, M2_DINNER), lambda b, c: (b * nc + c, z_blk)),
                  pl.BlockSpec((chunk, xw), lambda b, c: (b * nc + c, xbc_blk)),
                  pl.BlockSpec((chunk, xw), lambda b, c: (b * nc + c, xbc_blk + 1)),
                  pl.BlockSpec((chunk, xw), lambda b, c: (b * nc + c, xbc_blk + 2)),
                  pl.BlockSpec((chunk, LANES), lambda b, c: (b * nc + c, dt_blk)),
                  row(cw), row(cb), row(dtb), row(aneg), row(dsk), row(nrm),
                  pl.BlockSpec((1, 8, M2_CONV_DIM), lambda b, c: (b, 0, 0)), st],
        out_specs=[pl.BlockSpec((chunk, M2_DINNER), lambda b, c: (b * nc + c, 0)), st],
        out_shape=[jax.ShapeDtypeStruct((n_batch * t_len, M2_DINNER), BF16),
                   jax.ShapeDtypeStruct((n_batch, M2_STATE, M2_DINNER), F32)],
        scratch_shapes=[pltpu.VMEM((M2_STATE, M2_DINNER), F32), pltpu.VMEM((8, M2_CONV_DIM), F32)],
        compiler_params=pltpu.CompilerParams(
            dimension_semantics=("parallel", "arbitrary"), vmem_limit_bytes=VMEM_LIMIT),
        name="ssd",
    )(proj, proj, proj, proj, proj, cw, cb, dtb, aneg, dsk, nrm, buf8, st_t)
    s_fin = s_fin.reshape(n_batch, M2_STATE, M2_HEADS, M2_HEADDIM).transpose(0, 2, 3, 1)
    return out, s_fin


SEQ_ROWS = 8


def _pad_seq_rows(a, n_seq, t_len):
    assert t_len <= SEQ_ROWS
    a = a.reshape(n_seq, t_len, a.shape[-1])
    return jnp.pad(a, ((0, 0), (0, SEQ_ROWS - t_len), (0, 0))).reshape(n_seq * SEQ_ROWS, a.shape[-1])


def _unpad_seq_rows(a, n_seq, t_len):
    return a.reshape(n_seq, SEQ_ROWS, a.shape[-1])[:, :t_len].reshape(n_seq * t_len, a.shape[-1])

def kernel(x_prompt, x_sample, cache_nsa_kv, cache_nsa_win, state_sc_conv, state_ret, state_ssm, state_m2_conv,
           page_table, norm_w, final_norm, e_w_in, e_w_out, e_cmp_pe, e_cmp_w1, e_cmp_w2, e_sc_conv, o_w_in,
           o_w_out, o_ret_gn, o_m2_conv_w, o_m2_conv_b, o_m2_dt_bias, o_m2_a_log, o_m2_d, o_m2_norm, peer_wq,
           peer_keys, peer_u, peer_v):
    bp, tp, dm = x_prompt.shape
    bs, ts, _ = x_sample.shape
    n_p, n_s = bp * tp, bs * ts
    g, r, d = NSA_KV_HEADS, NSA_GROUP, NSA_HEAD_DIM
    past_len = page_table.shape[1] * PAGE_SIZE
    pos_p = jnp.arange(tp, dtype=jnp.int32)
    pos_s = past_len + jnp.arange(ts, dtype=jnp.int32)
    n_real = n_p + n_s
    n_pad = _round_up(n_real, PEER_TM) - n_real
    s_rows = slice(n_p, n_real)
    pos_rows = jnp.concatenate([jnp.tile(pos_p, bp), jnp.tile(pos_s, bs), jnp.zeros((n_pad,), jnp.int32)])
    xp = x_prompt.reshape(n_p, dm)
    x = jnp.concatenate([xp, x_sample.reshape(n_s, dm), xp[:n_pad]], axis=0)
    pad_bf = jnp.zeros((n_pad, E_Q), BF16)

    w0 = e_w_in[0]
    w_in = jnp.concatenate([w0[:, :SC_COL0], w0[:, SC_COL0 + E_G:], w0[:, SC_COL0:SC_COL0 + E_G],
                            jnp.zeros((dm, _pad_cols(E_IN) - E_IN), F32)], axis=1).astype(BF16)
    proj = norm_matmul(x, norm_w[0, 0], w_in)
    q_bf, rows, win, rows_bf, win_bf = nsa_rope(proj, pos_rows)
    kvc = nsa_compress(rows, bp, tp, e_cmp_pe[0], e_cmp_w1[0], e_cmp_w2[0])
    o_nsa_p = nsa_prompt(q_bf, kvc, rows_bf, win_bf, proj, bp, tp)
    p_kv = rows[:n_p].reshape(bp, tp, 4, g, d)
    s_kv = rows[s_rows].reshape(bs, ts, 4, g, d)
    p_win = win[:n_p].reshape(bp, tp, 2, g, d)[:, tp - min(WINDOW, tp):]
    win_s = win[s_rows].reshape(bs, ts, 2, g, d)
    s_win = jnp.concatenate([cache_nsa_win[0], win_s], axis=1)[:, ts:]
    cache = cache_nsa_kv[0]
    q_s = q_bf[s_rows].reshape(bs, ts, g, r, d).transpose(0, 2, 3, 1, 4).reshape(bs, g, r * ts, d)
    a0, a1 = nsa_sample_compress(cache, page_table, e_cmp_pe[0], e_cmp_w1[0])
    o_cmp_s, sel_s = nsa_sample_select(a0, a1, e_cmp_w2[0], q_s, past_len, ts)
    o_nsa_s = nsa_sample_attend(cache, page_table, q_s, sel_s, o_cmp_s, rows, win,
                                cache_nsa_win[0], proj, past_len, ts, n_p)
    o_nsa_s = o_nsa_s.reshape(bs, g, r, ts, d).transpose(0, 3, 1, 2, 4).reshape(n_s, E_Q)
    o_nsa = jnp.concatenate([o_nsa_p, o_nsa_s.astype(BF16), pad_bf], axis=0)
    o_sc_p, tail_p = short_conv(proj, bp, tp, SC_ROWS, jnp.zeros((bp, SC_KSIZE - 1, SC_WIDTH), F32), e_sc_conv[0])
    p_sc = tail_p[:, 8 - (SC_KSIZE - 1):]
    o_sc_s, tail_s = short_conv(_pad_seq_rows(proj[s_rows], bs, ts), bs, SEQ_ROWS, SEQ_ROWS, state_sc_conv[0],
                                e_sc_conv[0])
    s_sc = jnp.concatenate([state_sc_conv[0], tail_s[:, :ts]], axis=1)[:, ts:]
    o_sc = jnp.concatenate([_unpad_seq_rows(o_sc_s, bs, ts), pad_bf], axis=0)
    o_sc = jnp.concatenate([o_sc_p, o_sc], axis=0)
    x = matmul2_res(o_nsa, o_sc, e_w_out[0].astype(BF16), x)
    u_bf = peer_u.astype(BF16)
    vt_bf = peer_v.transpose(0, 2, 1).astype(BF16)
    peer_t = peer_layer(x, norm_w[0, 1], peer_wq[0], peer_keys[0], u_bf, vt_bf, 0)

    w_in = jnp.pad(o_w_in[0], ((0, 0), (0, _pad_cols(O_IN) - O_IN))).astype(BF16)
    x = x + peer_t.T
    proj = norm_matmul(x, norm_w[1, 0], w_in)
    odd_w = (o_m2_conv_w[0], o_m2_conv_b[0], o_m2_dt_bias[0], o_m2_a_log[0], o_m2_d[0], o_m2_norm[0])
    o_ret_p, p_ret = retention(proj, pos_p, bp, tp, CHUNK, jnp.zeros((bp, RET_HEADS, RET_DQK, RET_DV), F32),
                               o_ret_gn[0])
    o_ssd_p, p_ssm = ssd(proj, bp, tp, CHUNK, jnp.zeros((bp, M2_CONV - 1, M2_CONV_DIM), F32),
                         jnp.zeros((bp, M2_HEADS, M2_HEADDIM, M2_STATE), F32), *odd_w)
    xbc_cols = slice(O_SPLITS[4], O_SPLITS[5])
    p_m2c = jnp.stack([proj[(b + 1) * tp - (M2_CONV - 1):(b + 1) * tp, xbc_cols] for b in range(bp)])
    xbc_s = proj[s_rows, xbc_cols].reshape(bs, ts, M2_CONV_DIM)
    s_m2c = jnp.concatenate([state_m2_conv[0], xbc_s], axis=1)[:, ts:]
    proj_s = _pad_seq_rows(proj[s_rows], bs, ts)
    pos_s8 = past_len + jnp.arange(SEQ_ROWS, dtype=jnp.int32)
    o_ret_s, s_ret = retention(proj_s, pos_s8, bs, SEQ_ROWS, SEQ_ROWS, state_ret[0], o_ret_gn[0], valid=ts)
    o_ssd_s, s_ssm = ssd(proj_s, bs, SEQ_ROWS, SEQ_ROWS, state_m2_conv[0], state_ssm[0], *odd_w, valid=ts)
    o_ret = jnp.concatenate([o_ret_p, _unpad_seq_rows(o_ret_s, bs, ts), pad_bf], axis=0)
    o_ssd = jnp.concatenate([o_ssd_p, _unpad_seq_rows(o_ssd_s, bs, ts), pad_bf], axis=0)
    x = matmul2_res(o_ret, o_ssd, o_w_out[0].astype(BF16), x)
    peer_t = peer_layer(x, norm_w[1, 1], peer_wq[1], peer_keys[1], u_bf, vt_bf, 1)

    y = resid_rmsnorm(x, peer_t, final_norm)
    y_prompt = y[:n_p].reshape(bp, tp, dm)
    y_sample = y[s_rows].reshape(bs, ts, dm)
    return (y_prompt, y_sample, p_kv[None], p_win[None], p_sc[None], p_ret[None], p_ssm[None], p_m2c[None],
            s_kv[None], s_win[None], s_sc[None], s_ret[None], s_ssm[None], s_m2c[None])
```

```python
import functools
import math

import jax
import jax.numpy as jnp
from jax import lax
from jax.experimental import pallas as pl
from jax.experimental.pallas import tpu as pltpu

F32 = jnp.float32
BF16 = jnp.bfloat16

D_MODEL = 2048
DEPTH = 2
PAGE_SIZE = 128
NSA_HEAD_DIM = 128
NSA_HEADS = 8
NSA_KV_HEADS = 2
NSA_GROUP = 4
CMP_BLOCK = 32
CMP_STRIDE = 16
SLC_BLOCK = 64
SLC_TOPK = 16
WINDOW = 512
ROPE_THETA = 500000.0
ROPE_DIMS = 32
SC_WIDTH = 1024
SC_KSIZE = 3
RET_HEADS = 4
RET_DQK = 256
RET_DV = 256
RET_THETA = 10000.0
M2_DINNER = 1024
M2_HEADDIM = 64
M2_HEADS = 16
M2_STATE = 128
M2_GROUPS = 2
M2_CONV = 4
M2_CONV_DIM = M2_DINNER + 2 * M2_GROUPS * M2_STATE
PEER_HEADS = 8
PEER_KEYS = 128
PEER_QDIM = 256
PEER_TOPK = 16
Q_BLOCK = 128
CHUNK = 128
EPS = 1e-6

E_Q = NSA_HEADS * NSA_HEAD_DIM
E_KV = 6 * NSA_KV_HEADS * NSA_HEAD_DIM
E_G = 3 * NSA_HEADS
E_SC = 3 * SC_WIDTH
E_IN = E_Q + E_KV + E_G + E_SC
R_QK = RET_HEADS * RET_DQK
R_V = RET_HEADS * RET_DV
O_SPLITS = [R_QK, 2 * R_QK, 2 * R_QK + R_V, 2 * R_QK + 2 * R_V,
            2 * R_QK + 2 * R_V + M2_DINNER, 2 * R_QK + 2 * R_V + M2_DINNER + M2_CONV_DIM]
O_IN = O_SPLITS[-1] + M2_HEADS
SC_COL0 = E_Q + E_KV
GATE_COL0 = SC_COL0 + E_SC

LANES = 128
SUBLANES = 8
VMEM_LIMIT = 56 * 1024 * 1024
ROW_TILE_CAP = 1056
ROPE_ROWS_CAP = 544
RESID_ROWS = 768
NEG_INF = float("-inf")


def _round_up(n, m):
    return -(-n // m) * m


def _pick_tile(n, cap):
    best = LANES
    for t in range(LANES, cap + 1, LANES):
        if n % t == 0:
            best = t
    return best


def _pad_cols(m):
    return min((_round_up(m, t) for t in (768, 640, 512)))


def _row_tile(n, cap):
    best = None
    for t in range(16, cap + 1, 16):
        if n % t == 0:
            best = t
    assert best is not None
    return best


def _gelu_tanh(x):
    return 0.5 * x * (1.0 + jnp.tanh(math.sqrt(2.0 / math.pi) * (x + 0.044715 * (x * x * x))))


def _norm_matmul_kernel(x_ref, g_ref, w_ref, o_ref, xn_ref):
    @pl.when(pl.program_id(1) == 0)
    def _():
        x = x_ref[...]
        ms = jnp.mean(x * x, axis=-1, keepdims=True)
        xn_ref[...] = (x * lax.rsqrt(ms + EPS) * g_ref[...]).astype(BF16)

    o_ref[...] = jnp.dot(xn_ref[...], w_ref[...], preferred_element_type=F32)


def norm_matmul(x, gain, w_bf):
    n, k = x.shape
    m = w_bf.shape[1]
    tm = _row_tile(n, ROW_TILE_CAP)
    tn = _pick_tile(m, 768)
    return pl.pallas_call(
        _norm_matmul_kernel,
        grid=(n // tm, m // tn),
        in_specs=[pl.BlockSpec((tm, k), lambda i, j: (i, 0)),
                  pl.BlockSpec((1, k), lambda i, j: (0, 0)),
                  pl.BlockSpec((k, tn), lambda i, j: (0, j))],
        out_specs=pl.BlockSpec((tm, tn), lambda i, j: (i, j)),
        out_shape=jax.ShapeDtypeStruct((n, m), F32),
        scratch_shapes=[pltpu.VMEM((tm, k), BF16)],
        compiler_params=pltpu.CompilerParams(
            dimension_semantics=("parallel", "arbitrary"), vmem_limit_bytes=VMEM_LIMIT),
        name="norm_matmul",
    )(x, gain.reshape(1, k), w_bf)


def _matmul2_res_kernel(a1_ref, a2_ref, w1_ref, w2_ref, r_ref, o_ref):
    o_ref[...] = (r_ref[...] + jnp.dot(a1_ref[...], w1_ref[...], preferred_element_type=F32)
                  + jnp.dot(a2_ref[...], w2_ref[...], preferred_element_type=F32))


def matmul2_res(a1, a2, w_bf, res):
    n, k1 = a1.shape
    k2 = a2.shape[1]
    m = w_bf.shape[1]
    assert k1 == k2
    tm = _row_tile(n, ROW_TILE_CAP)
    tn = _pick_tile(m, 1024)
    return pl.pallas_call(
        _matmul2_res_kernel,
        grid=(n // tm, m // tn),
        in_specs=[pl.BlockSpec((tm, k1), lambda i, j: (i, 0)),
                  pl.BlockSpec((tm, k2), lambda i, j: (i, 0)),
                  pl.BlockSpec((k1, tn), lambda i, j: (0, j)),
                  pl.BlockSpec((k2, tn), lambda i, j: (1, j)),
                  pl.BlockSpec((tm, tn), lambda i, j: (i, j))],
        out_specs=pl.BlockSpec((tm, tn), lambda i, j: (i, j)),
        out_shape=jax.ShapeDtypeStruct((n, m), F32),
        compiler_params=pltpu.CompilerParams(
            dimension_semantics=("parallel", "parallel"), vmem_limit_bytes=VMEM_LIMIT),
        name="matmul2_res",
    )(a1, a2, w_bf, w_bf, res)


PEER_TOK = 128
PEER_HEAD_UNROLL = 4
PEER_TM = 768
PEER_TA = 8
PEER_TE = PEER_TA * PEER_KEYS


def _top_desc(work, count, with_rank=False, one_at_a_time=False):
    rows = []
    rank = jnp.full(work.shape, float(count), F32)
    row_id = lax.broadcasted_iota(jnp.int32, work.shape, 0).astype(F32)
    for r in range(count):
        m = jnp.max(work, axis=0, keepdims=True)
        rows.append(m)
        hit = work == m
        if one_at_a_time:
            hit = row_id == jnp.min(jnp.where(hit, row_id, float(work.shape[0])), axis=0, keepdims=True)
        if with_rank:
            rank = jnp.where(hit, float(r), rank)
        work = jnp.where(hit, NEG_INF, work)
    return (rows, rank) if with_rank else rows


def _peer_router_kernel(q_ref, k_ref, cnt_ref, gw_ref, r2_ref, w2_ref):
    def head(h, carry):
        q = q_ref[h]
        scores = []
        for side in range(2):
            qs = q[:, side * PEER_KEYS:(side + 1) * PEER_KEYS]
            qs = qs * lax.rsqrt(jnp.mean(qs * qs, axis=-1, keepdims=True) + EPS)
            scores.append(lax.dot_general(k_ref[h, side], qs, (((1,), (1,)), ((), ())),
                                          preferred_element_type=F32))
        s1, s2 = scores
        v1 = _top_desc(s1, PEER_TOPK + 1)
        v2, rank2 = _top_desc(s2, PEER_TOPK + 1, with_rank=True)
        v2_lo = jnp.concatenate(v2[:8], axis=0)
        v2_hi = jnp.concatenate(v2[8:16], axis=0)
        row = lax.broadcasted_iota(jnp.int32, v2_lo.shape, 0)
        blocks = [v1[0] + v2_lo, v1[0] + v2_hi, v1[1] + v2_lo]
        for a, lim in ((2, 5), (3, 4), (4, 3), (5, 2), (6, 2), (7, 2)):
            blocks.append(jnp.where(row < lim, v1[a] + v2_lo, NEG_INF))
        blocks.append(jnp.concatenate(v1[8:16], axis=0) + v2[0])
        extra = jnp.where(row == 0, v1[0] + v2[16], jnp.where(row == 1, v1[16] + v2[0], NEG_INF))
        blocks.append(extra)
        cand = jnp.concatenate(blocks, axis=0)
        tops = _top_desc(cand, PEER_TOPK + 1, one_at_a_time=True)
        z = jnp.zeros_like(tops[0])
        for r in range(PEER_TOPK):
            z = z + jnp.exp(tops[r] - tops[0])
        tau = 0.5 * (tops[PEER_TOPK - 1] + tops[PEER_TOPK])
        count = jnp.zeros_like(s1)
        for r in range(PEER_TOPK):
            count = count + jnp.where(s1 >= tau - v2[r], 1.0, 0.0)
        cnt_ref[h] = count
        gw_ref[h] = jnp.exp(s1 - v1[0]) / z
        r2_ref[h] = rank2.astype(BF16)
        w2_ref[h] = jnp.exp(s2 - v2[0]).astype(BF16)
        return carry

    lax.fori_loop(0, PEER_HEADS, head, 0, unroll=PEER_HEAD_UNROLL)


def peer_router(q_hm, keys):
    n = q_hm.shape[1]
    out = jax.ShapeDtypeStruct((PEER_HEADS, PEER_KEYS, n), F32)
    out_bf = jax.ShapeDtypeStruct((PEER_HEADS, PEER_KEYS, n), BF16)
    spec = pl.BlockSpec((PEER_HEADS, PEER_KEYS, PEER_TOK), lambda i: (0, 0, i))
    return pl.pallas_call(
        _peer_router_kernel,
        grid=(n // PEER_TOK,),
        in_specs=[pl.BlockSpec((PEER_HEADS, PEER_TOK, PEER_QDIM), lambda i: (0, i, 0)),
                  pl.BlockSpec((PEER_HEADS, 2, PEER_KEYS, PEER_QDIM // 2), lambda i: (0, 0, 0, 0))],
        out_specs=[spec, spec, spec, spec],
        out_shape=[out, out, out_bf, out_bf],
        compiler_params=pltpu.CompilerParams(dimension_semantics=("parallel",), vmem_limit_bytes=VMEM_LIMIT),
        name="peer_router",
    )(q_hm, keys)


def _peer_expert_kernel(xt_ref, u_ref, vt_ref, cnt_ref, gw_ref, r2_ref, w2_ref, o_ref, s_scr, hg_scr):
    e = pl.program_id(1)
    s_scr[...] = jnp.dot(u_ref[...], xt_ref[...], preferred_element_type=F32)
    zero = jnp.zeros((), BF16)
    for al in range(PEER_TA):
        rows = slice(al * PEER_KEYS, (al + 1) * PEER_KEYS)
        g = None
        for h in range(PEER_HEADS):
            cnt = cnt_ref[h, al:al + 1, :].astype(BF16)
            gate = gw_ref[h, al:al + 1, :].astype(BF16)
            t = jnp.where(r2_ref[h] < cnt, w2_ref[h], zero) * gate
            g = t if g is None else g + t
        hg_scr[rows, :] = _gelu_tanh(s_scr[rows, :]).astype(BF16) * g
    part = jnp.dot(vt_ref[...], hg_scr[...], preferred_element_type=F32)

    @pl.when(e == 0)
    def _():
        o_ref[...] = part

    @pl.when(e != 0)
    def _():
        o_ref[...] += part


def peer_experts(xt_bf, u_bf, vt_bf, layer, cnt, gw, r2, w2):
    d, n = xt_bf.shape
    n_exp = u_bf.shape[1]
    tm = PEER_TM
    sel_spec = pl.BlockSpec((PEER_HEADS, PEER_TA, tm), lambda i, e: (0, e, i))
    all_spec = pl.BlockSpec((PEER_HEADS, PEER_KEYS, tm), lambda i, e: (0, 0, i))
    return pl.pallas_call(
        _peer_expert_kernel,
        grid=(n // tm, n_exp // PEER_TE),
        in_specs=[pl.BlockSpec((d, tm), lambda i, e: (0, i)),
                  pl.BlockSpec((None, PEER_TE, d), lambda i, e: (layer, e, 0)),
                  pl.BlockSpec((None, d, PEER_TE), lambda i, e: (layer, 0, e)),
                  sel_spec, sel_spec, all_spec, all_spec],
        out_specs=pl.BlockSpec((d, tm), lambda i, e: (0, i)),
        out_shape=jax.ShapeDtypeStruct((d, n), F32),
        scratch_shapes=[pltpu.VMEM((PEER_TE, tm), F32), pltpu.VMEM((PEER_TE, tm), BF16)],
        compiler_params=pltpu.CompilerParams(
            dimension_semantics=("parallel", "arbitrary"), vmem_limit_bytes=VMEM_LIMIT),
        name="peer_experts",
    )(xt_bf, u_bf, vt_bf, cnt, gw, r2, w2)


def _norm_matmul_t_kernel(x_ref, g_ref, w_ref, o_ref, xt_ref, xn_ref):
    @pl.when(pl.program_id(1) == 0)
    def _():
        x = x_ref[...]
        ms = jnp.mean(x * x, axis=-1, keepdims=True)
        xn = x * lax.rsqrt(ms + EPS) * g_ref[...]
        xn_ref[...] = xn.astype(BF16)
        xt_ref[...] = xn.T.astype(BF16)

    o_ref[...] = jnp.dot(xn_ref[...], w_ref[...], preferred_element_type=F32).reshape(o_ref.shape)


def norm_matmul_t(x, gain, w_bf, cols):
    n, k = x.shape
    m = w_bf.shape[1]
    tm = PEER_TM
    return pl.pallas_call(
        _norm_matmul_t_kernel,
        grid=(n // tm, m // cols),
        in_specs=[pl.BlockSpec((tm, k), lambda i, j: (i, 0)),
                  pl.BlockSpec((1, k), lambda i, j: (0, 0)),
                  pl.BlockSpec((k, cols), lambda i, j: (0, j))],
        out_specs=[pl.BlockSpec((1, tm, cols), lambda i, j: (j, i, 0)),
                   pl.BlockSpec((k, tm), lambda i, j: (0, i))],
        out_shape=[jax.ShapeDtypeStruct((m // cols, n, cols), F32), jax.ShapeDtypeStruct((k, n), BF16)],
        scratch_shapes=[pltpu.VMEM((tm, k), BF16)],
        compiler_params=pltpu.CompilerParams(
            dimension_semantics=("parallel", "arbitrary"), vmem_limit_bytes=VMEM_LIMIT),
        name="norm_matmul_t",
    )(x, gain.reshape(1, k), w_bf)


def _resid_rmsnorm_kernel(x_ref, dt_ref, g_ref, o_ref):
    x = x_ref[...] + dt_ref[...].T
    ms = jnp.mean(x * x, axis=-1, keepdims=True)
    o_ref[...] = x * lax.rsqrt(ms + EPS) * g_ref[...]


def resid_rmsnorm(x, delta_t, gain):
    n, k = x.shape
    tm = _pick_tile(n, RESID_ROWS)
    return pl.pallas_call(
        _resid_rmsnorm_kernel,
        grid=(n // tm,),
        in_specs=[pl.BlockSpec((tm, k), lambda i: (i, 0)), pl.BlockSpec((k, tm), lambda i: (0, i)),
                  pl.BlockSpec((1, k), lambda i: (0, 0))],
        out_specs=pl.BlockSpec((tm, k), lambda i: (i, 0)),
        out_shape=jax.ShapeDtypeStruct((n, k), F32),
        compiler_params=pltpu.CompilerParams(dimension_semantics=("parallel",), vmem_limit_bytes=VMEM_LIMIT),
        name="resid_rmsnorm",
    )(x, delta_t, gain.reshape(1, k))


def peer_layer(x, gain, w_q, keys, u_bf, vt_bf, layer):
    q_hm, xt_bf = norm_matmul_t(x, gain, w_q.astype(BF16), PEER_QDIM)
    cnt, gw, r2, w2 = peer_router(q_hm, keys)
    return peer_experts(xt_bf, u_bf, vt_bf, layer, cnt, gw, r2, w2)


def _rope_tables(pos):
    half = ROPE_DIMS // 2
    inv = ROPE_THETA ** (-jnp.arange(half, dtype=F32) / half)
    ang = pos.astype(F32)[:, None] * inv[None, :]
    cos, sin = jnp.cos(ang), jnp.sin(ang)
    t = pos.shape[0]
    ones = jnp.ones((t, NSA_HEAD_DIM - ROPE_DIMS), F32)
    zeros = jnp.zeros((t, NSA_HEAD_DIM - ROPE_DIMS), F32)
    zh = jnp.zeros((t, half), F32)
    c = jnp.concatenate([cos, cos, ones], axis=1)
    s_lo = jnp.concatenate([-sin, zh, zeros], axis=1)
    s_hi = jnp.concatenate([zh, sin, zeros], axis=1)
    return c, s_lo, s_hi


def _rope_kernel(p_ref, c_ref, sl_ref, sh_ref, q_ref, rows_ref, win_ref, rows_bf_ref, win_bf_ref):
    c, sl, sh = c_ref[...], sl_ref[...], sh_ref[...]
    half = ROPE_DIMS // 2

    def rot(x):
        return x * c + pltpu.roll(x, LANES - half, 1) * sl + pltpu.roll(x, half, 1) * sh

    d = NSA_HEAD_DIM
    scale = d ** -0.5
    for hd in range(NSA_HEADS):
        q_ref[:, hd * d:(hd + 1) * d] = (rot(p_ref[:, hd * d:(hd + 1) * d]) * scale).astype(BF16)
    for blk in range(12):
        x = p_ref[:, E_Q + blk * d:E_Q + (blk + 1) * d]
        if (blk // 2) % 2 == 0:
            x = rot(x)
        if blk < 8:
            rows_ref[:, blk * d:(blk + 1) * d] = x
            rows_bf_ref[:, blk * d:(blk + 1) * d] = x.astype(BF16)
        else:
            win_ref[:, (blk - 8) * d:(blk - 7) * d] = x
            win_bf_ref[:, (blk - 8) * d:(blk - 7) * d] = x.astype(BF16)


def nsa_rope(proj, pos_rows):
    n = proj.shape[0]
    tm = _row_tile(n, ROPE_ROWS_CAP)
    c, sl, sh = _rope_tables(pos_rows)
    width = E_Q + E_KV
    tab = pl.BlockSpec((tm, LANES), lambda i: (i, 0))
    return pl.pallas_call(
        _rope_kernel,
        grid=(n // tm,),
        in_specs=[pl.BlockSpec((tm, width), lambda i: (i, 0)), tab, tab, tab],
        out_specs=[pl.BlockSpec((tm, E_Q), lambda i: (i, 0)),
                   pl.BlockSpec((tm, 1024), lambda i: (i, 0)),
                   pl.BlockSpec((tm, 512), lambda i: (i, 0)),
                   pl.BlockSpec((tm, 1024), lambda i: (i, 0)),
                   pl.BlockSpec((tm, 512), lambda i: (i, 0))],
        out_shape=[jax.ShapeDtypeStruct((n, E_Q), BF16),
                   jax.ShapeDtypeStruct((n, 1024), F32),
                   jax.ShapeDtypeStruct((n, 512), F32),
                   jax.ShapeDtypeStruct((n, 1024), BF16),
                   jax.ShapeDtypeStruct((n, 512), BF16)],
        compiler_params=pltpu.CompilerParams(dimension_semantics=("parallel",), vmem_limit_bytes=VMEM_LIMIT),
        name="nsa_rope",
    )(proj, c, sl, sh)


def _compress_kernel(x_ref, pe_ref, w1_ref, w2_ref, o_ref):
    n_sub = x_ref.shape[0] // CMP_STRIDE
    acc0 = jnp.zeros((n_sub, NSA_HEAD_DIM), F32)
    acc1 = jnp.zeros((n_sub, NSA_HEAD_DIM), F32)
    for s in range(CMP_STRIDE):
        xs = x_ref[pl.ds(s, n_sub, stride=CMP_STRIDE), :]
        a0 = (xs + pe_ref[0, s:s + 1, :]).astype(BF16)
        a1 = (xs + pe_ref[0, CMP_STRIDE + s:CMP_STRIDE + s + 1, :]).astype(BF16)
        acc0 = acc0 + jnp.dot(a0, w1_ref[0, s], preferred_element_type=F32)
        acc1 = acc1 + jnp.dot(a1, w1_ref[0, CMP_STRIDE + s], preferred_element_type=F32)
    pre = acc0 + pltpu.roll(acc1, n_sub - 1, 0)
    o_ref[0, 0] = jnp.dot(_gelu_tanh(pre).astype(BF16), w2_ref[0], preferred_element_type=F32).astype(BF16)


def nsa_compress(rows, n_batch, t_len, cmp_pe, cmp_w1, cmp_w2):
    n_sub = t_len // CMP_STRIDE
    d = NSA_HEAD_DIM
    w1 = cmp_w1.reshape(2, CMP_BLOCK, d, d).astype(BF16)
    return pl.pallas_call(
        _compress_kernel,
        grid=(n_batch, 4),
        in_specs=[pl.BlockSpec((t_len, d), lambda b, c: (b, c)),
                  pl.BlockSpec((1, CMP_BLOCK, d), lambda b, c: (c // 2, 0, 0)),
                  pl.BlockSpec((1, CMP_BLOCK, d, d), lambda b, c: (c // 2, 0, 0, 0)),
                  pl.BlockSpec((1, d, d), lambda b, c: (c // 2, 0, 0))],
        out_specs=pl.BlockSpec((1, 1, n_sub, d), lambda b, c: (b, c, 0, 0)),
        out_shape=jax.ShapeDtypeStruct((n_batch, 4, n_sub, d), BF16),
        compiler_params=pltpu.CompilerParams(
            dimension_semantics=("parallel", "parallel"), vmem_limit_bytes=VMEM_LIMIT),
        name="nsa_compress",
    )(rows, cmp_pe, w1, cmp_w2.astype(BF16))


NSA_TK = 512
NSA_WTILES = WINDOW // Q_BLOCK + 1


def _masked_softmax_rows(s, mask):
    s = jnp.where(mask, s, -1e30)
    p = jnp.exp(s - jnp.max(s, axis=-1, keepdims=True))
    p = jnp.where(mask, p, 0.0)
    return p / jnp.maximum(jnp.sum(p, axis=-1, keepdims=True), 1e-30)


def _nsa_prompt_kernel(q_ref, kc_ref, vc_ref, ks_ref, vs_ref, kw_ref, vw_ref, gate_ref, o_ref):
    g = pl.program_id(1)
    qi = pl.program_id(2)
    d, r, qb = NSA_HEAD_DIM, NSA_GROUP, Q_BLOCK
    n_cmp = kc_ref.shape[2]
    n_slc = ks_ref.shape[0] // SLC_BLOCK
    nt = (((1,), (1,)), ((), ()))
    q = jnp.concatenate([q_ref[:, h * d:(h + 1) * d] for h in range(r)], axis=0)
    t_col = qi * qb + lax.broadcasted_iota(jnp.int32, (qb, 1), 0)

    s = lax.dot_general(q, kc_ref[0, 0], nt, preferred_element_type=F32).reshape(r, qb, n_cmp)
    n_idx = lax.broadcasted_iota(jnp.int32, (qb, n_cmp), 1)
    c_mask = (n_idx * CMP_STRIDE + (CMP_BLOCK - 1) <= t_col)[None]
    p_c = _masked_softmax_rows(s, c_mask)
    o_cmp = jnp.dot(p_c.reshape(r * qb, n_cmp).astype(BF16), vc_ref[0, 0], preferred_element_type=F32)

    nb = LANES
    assert n_slc <= nb and qb == LANES
    p_sum = jnp.sum(p_c, axis=0)
    si = lax.broadcasted_iota(jnp.int32, (nb, n_cmp), 0) * SLC_BLOCK
    ci = lax.broadcasted_iota(jnp.int32, (nb, n_cmp), 1) * CMP_STRIDE
    overlap_t = jnp.where((ci < si + SLC_BLOCK) & (ci + CMP_BLOCK > si), 1.0, 0.0).astype(BF16)
    p_hi = p_sum.astype(BF16)
    p_lo = (p_sum - p_hi.astype(F32)).astype(BF16)
    imp_t = (lax.dot_general(overlap_t, p_hi, nt, preferred_element_type=F32)
             + lax.dot_general(overlap_t, p_lo, nt, preferred_element_type=F32))
    blk = lax.broadcasted_iota(jnp.int32, (nb, qb), 0)
    blk_f = blk.astype(F32)
    t_row = qi * qb + lax.broadcasted_iota(jnp.int32, (1, qb), 1)
    cur = t_row // SLC_BLOCK
    forced = (blk == 0) | (blk == cur) | (blk == cur - 1)
    work = jnp.where(blk * SLC_BLOCK <= t_row, jnp.where(forced, 1e6, imp_t), -1e6)
    work = jnp.where(blk < n_slc, work, NEG_INF)
    sel_t = jnp.zeros((nb, qb), F32)
    for _ in range(min(SLC_TOPK, n_slc)):
        m = jnp.max(work, axis=0, keepdims=True)
        first = jnp.min(jnp.where(work == m, blk_f, float(nb)), axis=0, keepdims=True)
        pick = blk_f == first
        sel_t = jnp.where(pick, 1.0, sel_t)
        work = jnp.where(pick, NEG_INF, work)
    sel_bf = sel_t.T.astype(BF16)

    tk = NSA_TK
    bpt = tk // SLC_BLOCK

    def slc_step(kt, carry):
        m_run, l_run, acc = carry
        start = pl.multiple_of(kt * tk, tk)
        k = ks_ref[pl.ds(start, tk), :]
        v = vs_ref[pl.ds(start, tk), :]
        s = lax.dot_general(q, k, nt, preferred_element_type=F32).reshape(r, qb, tk)
        ei = lax.broadcasted_iota(jnp.int32, (nb, tk), 0)
        ej = lax.broadcasted_iota(jnp.int32, (nb, tk), 1)
        expand = jnp.where(ei == kt * bpt + ej // SLC_BLOCK, 1.0, 0.0).astype(BF16)
        picked = jnp.dot(sel_bf, expand, preferred_element_type=F32)
        kpos = start + lax.broadcasted_iota(jnp.int32, (qb, tk), 1)
        mask = ((picked > 0.5) & (kpos <= t_col))[None]
        s = jnp.where(mask, s, -1e30)
        m_new = jnp.maximum(m_run, jnp.max(s, axis=-1, keepdims=True))
        alpha = jnp.exp(m_run - m_new)
        p = jnp.where(mask, jnp.exp(s - m_new), 0.0)
        l_new = alpha * l_run + jnp.sum(p, axis=-1, keepdims=True)
        pv = jnp.dot(p.reshape(r * qb, tk).astype(BF16), v, preferred_element_type=F32)
        acc = alpha.reshape(r * qb, 1) * acc + pv
        return m_new, l_new, acc

    init = (jnp.full((r, qb, 1), -1e30, F32), jnp.zeros((r, qb, 1), F32), jnp.zeros((r * qb, d), F32))
    _, l_fin, acc = lax.fori_loop(0, (qi * qb) // tk + 1, slc_step, init)
    o_slc = acc / jnp.maximum(l_fin.reshape(r * qb, 1), 1e-30)

    k_tiles, v_tiles, pos_tiles = [], [], []
    for j in range(NSA_WTILES):
        kt = qi - (NSA_WTILES - 1) + j
        ktc = jnp.maximum(kt, 0)
        start = pl.multiple_of(ktc * qb, qb)
        k_tiles.append(kw_ref[pl.ds(start, qb), :])
        v_tiles.append(vw_ref[pl.ds(start, qb), :])
        lane = lax.broadcasted_iota(jnp.int32, (qb, qb), 1)
        pos_tiles.append(jnp.where(kt >= 0, start + lane, -1))
    k_w = jnp.concatenate(k_tiles, axis=0)
    v_w = jnp.concatenate(v_tiles, axis=0)
    k_pos = jnp.concatenate(pos_tiles, axis=1)
    span = NSA_WTILES * qb
    s = lax.dot_general(q, k_w, nt, preferred_element_type=F32).reshape(r, qb, span)
    dpos = t_col - k_pos
    w_mask = ((dpos >= 0) & (dpos < WINDOW) & (k_pos >= 0))[None]
    p_w = _masked_softmax_rows(s, w_mask)
    o_win = jnp.dot(p_w.reshape(r * qb, span).astype(BF16), v_w, preferred_element_type=F32)

    sig = jax.nn.sigmoid(gate_ref[...])
    lane = lax.broadcasted_iota(jnp.int32, sig.shape, 1)
    for h in range(r):
        rows = slice(h * qb, (h + 1) * qb)
        out = jnp.zeros((qb, d), F32)
        for branch, o_b in enumerate((o_cmp, o_slc, o_win)):
            col = branch * NSA_HEADS + g * r + h
            gate = jnp.sum(jnp.where(lane == col, sig, 0.0), axis=-1, keepdims=True)
            out = out + gate * o_b[rows]
        o_ref[:, h * d:(h + 1) * d] = out.astype(o_ref.dtype)


def nsa_prompt(q_bf, kvc, rows_bf, win_bf, proj, n_batch, t_len):
    d, r, qb = NSA_HEAD_DIM, NSA_GROUP, Q_BLOCK
    nqb = t_len // qb
    n_sub = kvc.shape[2]
    gate_blk = GATE_COL0 // LANES
    seq = lambda col: pl.BlockSpec((t_len, d), lambda b, g, i: (b, col(g)))
    return pl.pallas_call(
        _nsa_prompt_kernel,
        grid=(n_batch, NSA_KV_HEADS, nqb),
        in_specs=[pl.BlockSpec((qb, r * d), lambda b, g, i: (b * nqb + i, g)),
                  pl.BlockSpec((1, 1, n_sub, d), lambda b, g, i: (b, g, 0, 0)),
                  pl.BlockSpec((1, 1, n_sub, d), lambda b, g, i: (b, 2 + g, 0, 0)),
                  seq(lambda g: 4 + g), seq(lambda g: 6 + g),
                  seq(lambda g: g), seq(lambda g: 2 + g),
                  pl.BlockSpec((qb, LANES), lambda b, g, i: (b * nqb + i, gate_blk))],
        out_specs=pl.BlockSpec((qb, r * d), lambda b, g, i: (b * nqb + i, g)),
        out_shape=jax.ShapeDtypeStruct((n_batch * t_len, E_Q), BF16),
        compiler_params=pltpu.CompilerParams(
            dimension_semantics=("parallel", "parallel", "arbitrary"), vmem_limit_bytes=VMEM_LIMIT),
        name="nsa_prompt",
    )(q_bf, kvc, kvc, rows_bf, rows_bf, win_bf, win_bf, proj)


SAMPLE_PAGES = 16
NEW_ROWS_BLK = 16


def _page_specs(kind):
    def spec(i):
        return pl.BlockSpec((1, PAGE_SIZE, 1, NSA_KV_HEADS, NSA_HEAD_DIM),
                            lambda s, c, pt: (pt[s, c * SAMPLE_PAGES + i], 0, kind, 0, 0))
    return [spec(i) for i in range(SAMPLE_PAGES)]


def _sample_compress_kernel(pt_ref, *refs):
    n_in = 2 * SAMPLE_PAGES
    pages = refs[:n_in]
    pe_ref, w1_ref, a0_ref, a1_ref = refs[n_in:]
    d = NSA_HEAD_DIM
    per_page = PAGE_SIZE // CMP_STRIDE
    rows = SAMPLE_PAGES * per_page
    gs = NSA_KV_HEADS * CMP_STRIDE
    for kv in range(2):
        by_offset = [jnp.swapaxes(pages[kv * SAMPLE_PAGES + p].reshape(NSA_KV_HEADS * PAGE_SIZE, d)[...]
                                  .reshape(per_page, gs, d), 0, 1) for p in range(SAMPLE_PAGES)]
        acc0 = jnp.zeros((NSA_KV_HEADS * rows, d), F32)
        acc1 = jnp.zeros((NSA_KV_HEADS * rows, d), F32)
        for s in range(CMP_STRIDE):
            xs = jnp.concatenate([by_offset[p][NSA_KV_HEADS * s + g]
                                  for g in range(NSA_KV_HEADS) for p in range(SAMPLE_PAGES)],
                                 axis=0)
            a0 = (xs + pe_ref[kv, s:s + 1, :]).astype(BF16)
            a1 = (xs + pe_ref[kv, CMP_STRIDE + s:CMP_STRIDE + s + 1, :]).astype(BF16)
            acc0 = acc0 + jnp.dot(a0, w1_ref[kv, s], preferred_element_type=F32)
            acc1 = acc1 + jnp.dot(a1, w1_ref[kv, CMP_STRIDE + s], preferred_element_type=F32)
        for g in range(NSA_KV_HEADS):
            a0_ref[0, kv * NSA_KV_HEADS + g] = acc0[g * rows:(g + 1) * rows]
            a1_ref[0, kv * NSA_KV_HEADS + g] = acc1[g * rows:(g + 1) * rows]


def nsa_sample_compress(cache, page_table, cmp_pe, cmp_w1):
    bs, n_pages = page_table.shape
    d = NSA_HEAD_DIM
    per_page = PAGE_SIZE // CMP_STRIDE
    n_sub = n_pages * per_page
    rows = SAMPLE_PAGES * per_page
    w1 = cmp_w1.reshape(2, CMP_BLOCK, d, d).astype(BF16)
    out = jax.ShapeDtypeStruct((bs, 4, n_sub, d), F32)
    ospec = pl.BlockSpec((1, 4, rows, d), lambda s, c, pt: (s, 0, c, 0))
    page_specs = _page_specs(0) + _page_specs(1)
    return pl.pallas_call(
        _sample_compress_kernel,
        grid_spec=pltpu.PrefetchScalarGridSpec(
            num_scalar_prefetch=1,
            grid=(bs, n_pages // SAMPLE_PAGES),
            in_specs=page_specs + [
                pl.BlockSpec((2, CMP_BLOCK, d), lambda s, c, pt: (0, 0, 0)),
                pl.BlockSpec((2, CMP_BLOCK, d, d), lambda s, c, pt: (0, 0, 0, 0))],
            out_specs=[ospec, ospec]),
        out_shape=[out, out],
        compiler_params=pltpu.CompilerParams(
            dimension_semantics=("parallel", "arbitrary"), vmem_limit_bytes=VMEM_LIMIT),
        name="nsa_sample_compress",
    )(page_table, *([cache] * (2 * SAMPLE_PAGES)), cmp_pe, w1)


def _sample_select_kernel(a0_ref, a1_ref, w2_ref, q_ref, ocmp_ref, sel_ref, *, past_len, t_len):
    d, r = NSA_HEAD_DIM, NSA_GROUP
    n_sub = a0_ref.shape[2]
    n_cmp = n_sub - 1
    n_slc = -(-(past_len + t_len) // SLC_BLOCK)
    lanes = sel_ref.shape[3]
    nt = (((1,), (1,)), ((), ()))
    rq = r * t_len
    for g in range(NSA_KV_HEADS):
        kc = jnp.dot(_gelu_tanh(a0_ref[0, g] + pltpu.roll(a1_ref[0, g], n_sub - 1, 0)).astype(BF16), w2_ref[0],
                     preferred_element_type=F32).astype(BF16)
        vc = jnp.dot(_gelu_tanh(a0_ref[0, 2 + g] + pltpu.roll(a1_ref[0, 2 + g], n_sub - 1, 0)).astype(BF16),
                     w2_ref[1], preferred_element_type=F32).astype(BF16)
        q = q_ref[0, g]
        s = lax.dot_general(q, kc, nt, preferred_element_type=F32)
        pos = past_len + lax.broadcasted_iota(jnp.int32, (rq, 1), 0) % t_len
        n_idx = lax.broadcasted_iota(jnp.int32, (rq, n_sub), 1)
        p_c = _masked_softmax_rows(s, (n_idx * CMP_STRIDE + (CMP_BLOCK - 1) <= pos) & (n_idx < n_cmp))
        ocmp_ref[0, g] = jnp.dot(p_c.astype(BF16), vc, preferred_element_type=F32)
        ri = lax.broadcasted_iota(jnp.int32, (SUBLANES, rq), 0)
        cj = lax.broadcasted_iota(jnp.int32, (SUBLANES, rq), 1)
        head_sum = jnp.where(cj % t_len == ri, 1.0, 0.0).astype(BF16)
        p_hi = p_c.astype(BF16)
        p_lo = (p_c - p_hi.astype(F32)).astype(BF16)
        p_sum = (jnp.dot(head_sum, p_hi, preferred_element_type=F32)
                 + jnp.dot(head_sum, p_lo, preferred_element_type=F32))
        ci = lax.broadcasted_iota(jnp.int32, (n_sub, lanes), 0)
        mi = lax.broadcasted_iota(jnp.int32, (n_sub, lanes), 1)
        overlap = jnp.where((ci * CMP_STRIDE < mi * SLC_BLOCK + SLC_BLOCK)
                            & (ci * CMP_STRIDE + CMP_BLOCK > mi * SLC_BLOCK) & (ci < n_cmp), 1.0, 0.0).astype(BF16)
        s_hi = p_sum.astype(BF16)
        s_lo = (p_sum - s_hi.astype(F32)).astype(BF16)
        imp = (jnp.dot(s_hi, overlap, preferred_element_type=F32)
               + jnp.dot(s_lo, overlap, preferred_element_type=F32))
        blk = lax.broadcasted_iota(jnp.int32, (SUBLANES, lanes), 1)
        blk_f = blk.astype(F32)
        tpos = past_len + lax.broadcasted_iota(jnp.int32, (SUBLANES, 1), 0) % t_len
        cur = tpos // SLC_BLOCK
        forced = (blk == 0) | (blk == cur) | (blk == cur - 1)
        work = jnp.where(blk * SLC_BLOCK <= tpos, jnp.where(forced, 1e6, imp), -1e6)
        work = jnp.where(blk < n_slc, work, NEG_INF)
        sel = jnp.zeros((SUBLANES, lanes), F32)
        for _ in range(min(SLC_TOPK, n_slc)):
            m = jnp.max(work, axis=-1, keepdims=True)
            first = jnp.min(jnp.where(work == m, blk_f, float(lanes)), axis=-1, keepdims=True)
            pick = blk_f == first
            sel = jnp.where(pick, 1.0, sel)
            work = jnp.where(pick, NEG_INF, work)
        sel_ref[0, g] = sel


def nsa_sample_select(a0, a1, cmp_w2, q_s, past_len, t_len):
    bs, _, n_sub, d = a0.shape
    n_slc = -(-(past_len + t_len) // SLC_BLOCK)
    lanes = _round_up(n_slc, LANES)
    rq = NSA_GROUP * t_len
    aspec = pl.BlockSpec((1, 4, n_sub, d), lambda s: (s, 0, 0, 0))
    return pl.pallas_call(
        functools.partial(_sample_select_kernel, past_len=past_len, t_len=t_len),
        grid=(bs,),
        in_specs=[aspec, aspec, pl.BlockSpec((2, d, d), lambda s: (0, 0, 0)),
                  pl.BlockSpec((1, NSA_KV_HEADS, rq, d), lambda s: (s, 0, 0, 0))],
        out_specs=[pl.BlockSpec((1, NSA_KV_HEADS, rq, d), lambda s: (s, 0, 0, 0)),
                   pl.BlockSpec((1, NSA_KV_HEADS, SUBLANES, lanes), lambda s: (s, 0, 0, 0))],
        out_shape=[jax.ShapeDtypeStruct((bs, NSA_KV_HEADS, rq, d), F32),
                   jax.ShapeDtypeStruct((bs, NSA_KV_HEADS, SUBLANES, lanes), F32)],
        compiler_params=pltpu.CompilerParams(dimension_semantics=("parallel",), vmem_limit_bytes=VMEM_LIMIT),
        name="nsa_sample_select",
    )(a0, a1, cmp_w2.astype(BF16), q_s)


def _sample_attend_kernel(pt_ref, *refs, past_len, t_len, row0):
    k_pages = refs[:SAMPLE_PAGES]
    v_pages = refs[SAMPLE_PAGES:2 * SAMPLE_PAGES]
    (q_ref, sel_ref, ocmp_ref, newrows_ref, kwin_ref, vwin_ref, newwin_ref, gate_ref, o_ref,
     m_scr, l_scr, acc_scr) = refs[2 * SAMPLE_PAGES:]
    s_idx = pl.program_id(0)
    c = pl.program_id(1)
    d, r = NSA_HEAD_DIM, NSA_GROUP
    rq = r * t_len
    lanes = sel_ref.shape[3]
    tk = SAMPLE_PAGES * PAGE_SIZE
    nt = (((1,), (1,)), ((), ()))

    @pl.when(c == 0)
    def _():
        m_scr[...] = jnp.full(m_scr.shape, -1e30, F32)
        l_scr[...] = jnp.zeros(l_scr.shape, F32)
        acc_scr[...] = jnp.zeros(acc_scr.shape, F32)

    row = lax.broadcasted_iota(jnp.int32, (rq, 1), 0)
    t_row = row % t_len
    pos = past_len + t_row
    ti = lax.broadcasted_iota(jnp.int32, (rq, SUBLANES), 1)
    tok_expand = jnp.where(ti == t_row, 1.0, 0.0).astype(BF16)

    def online(g, s, mask, v):
        s = jnp.where(mask, s, -1e30)
        m_old = m_scr[g]
        m_new = jnp.maximum(m_old, jnp.max(s, axis=-1, keepdims=True))
        alpha = jnp.exp(m_old - m_new)
        p = jnp.where(mask, jnp.exp(s - m_new), 0.0)
        l_scr[g] = alpha * l_scr[g] + jnp.sum(p, axis=-1, keepdims=True)
        acc_scr[g] = alpha * acc_scr[g] + jnp.dot(p.astype(BF16), v, preferred_element_type=F32)
        m_scr[g] = m_new

    sel16 = []
    for g in range(NSA_KV_HEADS):
        sel16.append(jnp.dot(tok_expand, sel_ref[0, g].astype(BF16), preferred_element_type=F32))
        k = jnp.concatenate([pg.reshape(NSA_KV_HEADS * PAGE_SIZE, d)[pl.ds(g, PAGE_SIZE, stride=NSA_KV_HEADS), :]
                             for pg in k_pages], axis=0)
        v = jnp.concatenate([pg.reshape(NSA_KV_HEADS * PAGE_SIZE, d)[pl.ds(g, PAGE_SIZE, stride=NSA_KV_HEADS), :]
                             for pg in v_pages], axis=0)
        q = q_ref[0, g]
        s = lax.dot_general(q, k.astype(BF16), nt, preferred_element_type=F32)
        ei = lax.broadcasted_iota(jnp.int32, (lanes, tk), 0)
        ej = lax.broadcasted_iota(jnp.int32, (lanes, tk), 1)
        expand = jnp.where(ei == c * (tk // SLC_BLOCK) + ej // SLC_BLOCK, 1.0, 0.0).astype(BF16)
        picked = jnp.dot(sel16[g].astype(BF16), expand, preferred_element_type=F32)
        kpos = c * tk + lax.broadcasted_iota(jnp.int32, (rq, tk), 1)
        online(g, s, (picked > 0.5) & (kpos <= pos), v.astype(BF16))

    @pl.when(c == pl.num_programs(1) - 1)
    def _():
        mine = (row0 // t_len + s_idx) % (NEW_ROWS_BLK // t_len)
        j = lax.broadcasted_iota(jnp.int32, (rq, NEW_ROWS_BLK), 1)
        own = j // t_len == mine
        new_pos = past_len + j % t_len
        sig = jax.nn.sigmoid(gate_ref[...])
        gsel = jnp.where(j == mine * t_len + t_row, 1.0, 0.0).astype(BF16)
        s_hi = sig.astype(BF16)
        s_lo = (sig - s_hi.astype(F32)).astype(BF16)
        sig_rows = (jnp.dot(gsel, s_hi, preferred_element_type=F32)
                    + jnp.dot(gsel, s_lo, preferred_element_type=F32))
        lane = lax.broadcasted_iota(jnp.int32, sig_rows.shape, 1)
        lane_l = lax.broadcasted_iota(jnp.int32, (rq, lanes), 1)
        for g in range(NSA_KV_HEADS):
            q = q_ref[0, g]
            kn = newrows_ref[:, (4 + g) * d:(5 + g) * d].astype(BF16)
            vn = newrows_ref[:, (6 + g) * d:(7 + g) * d].astype(BF16)
            s = lax.dot_general(q, kn, nt, preferred_element_type=F32)
            last_picked = jnp.sum(jnp.where(lane_l == past_len // SLC_BLOCK, sel16[g], 0.0), axis=-1, keepdims=True)
            online(g, s, own & (new_pos <= pos) & (last_picked > 0.5), vn)
            o_slc = acc_scr[g] / jnp.maximum(l_scr[g], 1e-30)
            w_len = kwin_ref.shape[1]
            win_rows = pl.ds(g, w_len, stride=NSA_KV_HEADS)
            kw = kwin_ref.reshape(NSA_KV_HEADS * w_len, d)[win_rows, :].astype(BF16)
            vw = vwin_ref.reshape(NSA_KV_HEADS * w_len, d)[win_rows, :].astype(BF16)
            s1 = lax.dot_general(q, kw, nt, preferred_element_type=F32)
            kp1 = past_len - w_len + lax.broadcasted_iota(jnp.int32, (rq, w_len), 1)
            d1 = pos - kp1
            m1 = (d1 >= 0) & (d1 < WINDOW) & (kp1 >= 0)
            knw = newwin_ref[:, g * d:(g + 1) * d].astype(BF16)
            vnw = newwin_ref[:, (2 + g) * d:(3 + g) * d].astype(BF16)
            s2 = lax.dot_general(q, knw, nt, preferred_element_type=F32)
            d2 = pos - new_pos
            m2 = own & (d2 >= 0) & (d2 < WINDOW)
            s1 = jnp.where(m1, s1, -1e30)
            s2 = jnp.where(m2, s2, -1e30)
            mx = jnp.maximum(jnp.max(s1, axis=-1, keepdims=True), jnp.max(s2, axis=-1, keepdims=True))
            p1 = jnp.where(m1, jnp.exp(s1 - mx), 0.0)
            p2 = jnp.where(m2, jnp.exp(s2 - mx), 0.0)
            den = jnp.sum(p1, axis=-1, keepdims=True) + jnp.sum(p2, axis=-1, keepdims=True)
            o_win = (jnp.dot(p1.astype(BF16), vw, preferred_element_type=F32)
                     + jnp.dot(p2.astype(BF16), vnw, preferred_element_type=F32)) / jnp.maximum(den, 1e-30)
            out = jnp.zeros((rq, d), F32)
            for branch, o_b in enumerate((ocmp_ref[0, g], o_slc, o_win)):
                col = branch * NSA_HEADS + g * r + row // t_len
                gate = jnp.sum(jnp.where(lane == col, sig_rows, 0.0), axis=-1, keepdims=True)
                out = out + gate * o_b
            o_ref[0, g] = out


def nsa_sample_attend(cache, page_table, q_s, sel, o_cmp, rows, win, win_cache, proj, past_len, t_len, row0):
    bs, n_pages = page_table.shape
    d = NSA_HEAD_DIM
    rq = NSA_GROUP * t_len
    lanes = sel.shape[3]
    w_len = win_cache.shape[1]
    assert row0 % t_len == 0 and NEW_ROWS_BLK % t_len == 0 and past_len % SLC_BLOCK == 0
    blk = lambda s: (row0 + s * t_len) // NEW_ROWS_BLK
    per_seq = lambda shape: pl.BlockSpec((1,) + shape, lambda s, c, pt: (s, 0, 0, 0))
    return pl.pallas_call(
        functools.partial(_sample_attend_kernel, past_len=past_len, t_len=t_len, row0=row0),
        grid_spec=pltpu.PrefetchScalarGridSpec(
            num_scalar_prefetch=1,
            grid=(bs, n_pages // SAMPLE_PAGES),
            in_specs=_page_specs(2) + _page_specs(3) + [
                per_seq((NSA_KV_HEADS, rq, d)), per_seq((NSA_KV_HEADS, SUBLANES, lanes)), per_seq((NSA_KV_HEADS, rq, d)),
                pl.BlockSpec((NEW_ROWS_BLK, 4 * NSA_KV_HEADS * d), lambda s, c, pt: (blk(s), 0)),
                pl.BlockSpec((1, w_len, 1, NSA_KV_HEADS, d), lambda s, c, pt: (s, 0, 0, 0, 0)),
                pl.BlockSpec((1, w_len, 1, NSA_KV_HEADS, d), lambda s, c, pt: (s, 0, 1, 0, 0)),
                pl.BlockSpec((NEW_ROWS_BLK, 4 * d), lambda s, c, pt: (blk(s), 0)),
                pl.BlockSpec((NEW_ROWS_BLK, LANES), lambda s, c, pt: (blk(s), GATE_COL0 // LANES))],
            out_specs=per_seq((NSA_KV_HEADS, rq, d)),
            scratch_shapes=[pltpu.VMEM((NSA_KV_HEADS, rq, 1), F32), pltpu.VMEM((NSA_KV_HEADS, rq, 1), F32),
                            pltpu.VMEM((NSA_KV_HEADS, rq, d), F32)]),
        out_shape=jax.ShapeDtypeStruct((bs, NSA_KV_HEADS, rq, d), F32),
        compiler_params=pltpu.CompilerParams(
            dimension_semantics=("parallel", "arbitrary"), vmem_limit_bytes=VMEM_LIMIT),
        name="nsa_sample_attend",
    )(page_table, *([cache] * (2 * SAMPLE_PAGES)), q_s, sel, o_cmp, rows, win_cache, win_cache, win, proj)


SC_ROWS = 512
SC_COLS = 512


def _short_conv_kernel(b_ref, c_ref, h_ref, w_ref, buf_ref, o_ref, tail_ref, prev_scr):
    @pl.when(pl.program_id(2) == 0)
    def _():
        prev_scr[...] = buf_ref[0]

    rows = c_ref.shape[0]
    u = c_ref[...] * h_ref[...]
    prev = prev_scr[...]
    row8 = lax.broadcasted_iota(jnp.int32, prev.shape, 0)
    conv = u * w_ref[SC_KSIZE - 1:SC_KSIZE, :]
    for k in range(1, SC_KSIZE):
        rolled = pltpu.roll(u, k, 0)
        top = jnp.where(row8 < k, pltpu.roll(prev, k, 0), rolled[0:SUBLANES])
        shifted = top if rows == SUBLANES else jnp.concatenate([top, rolled[SUBLANES:]], axis=0)
        conv = conv + shifted * w_ref[SC_KSIZE - 1 - k:SC_KSIZE - k, :]
    prev_scr[...] = u[rows - SUBLANES:rows]
    o_ref[...] = (b_ref[...] * conv).astype(o_ref.dtype)
    tail_ref[0] = u[rows - SUBLANES:rows]


def short_conv(proj, n_batch, t_len, rows_per_step, conv_buf, sc_w):
    nr = t_len // rows_per_step
    nh = SC_WIDTH // SC_COLS
    blk0 = SC_COL0 // SC_COLS
    assert SC_COL0 % SC_COLS == 0
    buf8 = jnp.pad(conv_buf, ((0, 0), (SUBLANES - (SC_KSIZE - 1), 0), (0, 0)))
    col = lambda part: pl.BlockSpec((rows_per_step, SC_COLS), lambda b, j, i: (b * nr + i, blk0 + part * nh + j))
    return pl.pallas_call(
        _short_conv_kernel,
        grid=(n_batch, nh, nr),
        in_specs=[col(0), col(1), col(2), pl.BlockSpec((SC_KSIZE, SC_COLS), lambda b, j, i: (0, j)),
                  pl.BlockSpec((1, SUBLANES, SC_COLS), lambda b, j, i: (b, 0, j))],
        out_specs=[pl.BlockSpec((rows_per_step, SC_COLS), lambda b, j, i: (b * nr + i, j)),
                   pl.BlockSpec((1, SUBLANES, SC_COLS), lambda b, j, i: (b, 0, j))],
        out_shape=[jax.ShapeDtypeStruct((n_batch * t_len, SC_WIDTH), BF16),
                   jax.ShapeDtypeStruct((n_batch, SUBLANES, SC_WIDTH), F32)],
        scratch_shapes=[pltpu.VMEM((SUBLANES, SC_COLS), F32)],
        compiler_params=pltpu.CompilerParams(
            dimension_semantics=("parallel", "parallel", "arbitrary"), vmem_limit_bytes=VMEM_LIMIT),
        name="short_conv",
    )(proj, proj, proj, sc_w, buf8)


def _ret_tables(pos, chunk, valid):
    half = RET_DQK // 2
    inv = RET_THETA ** (-jnp.arange(half, dtype=F32) / half)
    ang = pos.astype(F32)[:, None] * inv[None, :]
    log_g = jnp.log1p(-(2.0 ** (-5.0 - jnp.arange(RET_HEADS, dtype=F32))))
    i = jnp.arange(chunk, dtype=F32)
    diff = i[:, None] - i[None, :]
    intra = jnp.where(diff >= 0, jnp.exp(jnp.maximum(diff, 0.0)[None] * log_g[:, None, None]), 0.0)
    q_dec = jnp.exp((i[None, :] + 1.0) * log_g[:, None])[..., None]
    k_dec = jnp.exp((valid - 1.0 - i)[None, :] * log_g[:, None])[..., None]
    c_dec = jnp.exp(valid * log_g)[:, None, None]
    return jnp.cos(ang), jnp.sin(ang), intra, q_dec, k_dec, c_dec


def _retention_kernel(q_ref, k_ref, v_ref, g_ref, cos_ref, sin_ref, intra_ref, qd_ref, kd_ref, cd_ref, gn_ref,
                      s0_ref, o_ref, s_out_ref, s_scr):
    c_idx = pl.program_id(1)

    @pl.when(c_idx == 0)
    def _():
        s_scr[...] = s0_ref[0]

    cos, sin = cos_ref[...], sin_ref[...]
    half = RET_DQK // 2
    nt = (((1,), (1,)), ((), ()))
    tn = (((0,), (0,)), ((), ()))

    def rot(x):
        x1, x2 = x[:, :half], x[:, half:]
        return jnp.concatenate([x1 * cos - x2 * sin, x1 * sin + x2 * cos], axis=-1)

    for h in range(RET_HEADS):
        cols = slice(h * RET_DQK, (h + 1) * RET_DQK)
        qr = (rot(q_ref[:, cols]) * (RET_DQK ** -0.5)).astype(BF16)
        kr = rot(k_ref[:, cols])
        v = v_ref[:, cols].astype(BF16)
        att = lax.dot_general(qr, kr.astype(BF16), nt, preferred_element_type=F32) * intra_ref[h]
        s_old = s_scr[h]
        o = (jnp.dot(att.astype(BF16), v, preferred_element_type=F32)
             + jnp.dot(qr, s_old.astype(BF16), preferred_element_type=F32) * qd_ref[h])
        s_scr[h] = s_old * cd_ref[h] + lax.dot_general((kr * kd_ref[h]).astype(BF16), v, tn,
                                                       preferred_element_type=F32)
        mu = jnp.mean(o, axis=-1, keepdims=True)
        dev = o - mu
        var = jnp.mean(dev * dev, axis=-1, keepdims=True)
        gate = g_ref[:, cols]
        on = dev * lax.rsqrt(var + EPS) * gn_ref[:, cols] * (gate * jax.nn.sigmoid(gate))
        o_ref[:, cols] = on.astype(o_ref.dtype)
    s_out_ref[0] = s_scr[...]


def retention(proj, pos, n_batch, t_len, chunk, state0, gn_gain, valid=None):
    nc = t_len // chunk
    cos, sin, intra, q_dec, k_dec, c_dec = _ret_tables(pos, chunk, chunk if valid is None else valid)
    half = RET_DQK // 2
    col = lambda j: pl.BlockSpec((chunk, R_QK), lambda b, c: (b * nc + c, j))
    tab = pl.BlockSpec((chunk, half), lambda b, c: (c, 0))
    full = lambda a: pl.BlockSpec(a.shape, lambda b, c: (0,) * a.ndim)
    st = pl.BlockSpec((1, RET_HEADS, RET_DQK, RET_DV), lambda b, c: (b, 0, 0, 0))
    return pl.pallas_call(
        _retention_kernel,
        grid=(n_batch, nc),
        in_specs=[col(0), col(1), col(2), col(3), tab, tab, full(intra), full(q_dec), full(k_dec), full(c_dec),
                  pl.BlockSpec((1, R_V), lambda b, c: (0, 0)), st],
        out_specs=[pl.BlockSpec((chunk, R_V), lambda b, c: (b * nc + c, 0)), st],
        out_shape=[jax.ShapeDtypeStruct((n_batch * t_len, R_V), BF16),
                   jax.ShapeDtypeStruct((n_batch, RET_HEADS, RET_DQK, RET_DV), F32)],
        scratch_shapes=[pltpu.VMEM((RET_HEADS, RET_DQK, RET_DV), F32)],
        compiler_params=pltpu.CompilerParams(
            dimension_semantics=("parallel", "arbitrary"), vmem_limit_bytes=VMEM_LIMIT),
        name="retention",
    )(proj, proj, proj, proj, cos, sin, intra, q_dec, k_dec, c_dec, gn_gain.reshape(1, R_V), state0)


SSD_COL0 = 2 * R_QK + 2 * R_V
HEADS_PER_GROUP = M2_HEADS // M2_GROUPS


def _split3(x):
    a = x.astype(BF16)
    r = x - a.astype(F32)
    b = r.astype(BF16)
    c = (r - b.astype(F32)).astype(BF16)
    return a, b, c


def _exact_dot(mat_bf, x):
    out = None
    for piece in _split3(x):
        t = jnp.dot(mat_bf, piece, preferred_element_type=F32)
        out = t if out is None else out + t
    return out


def _exact_dot_r(x, mat_bf):
    out = None
    for piece in _split3(x):
        t = jnp.dot(piece, mat_bf, preferred_element_type=F32)
        out = t if out is None else out + t
    return out


def _ssd_kernel(z_ref, xa_ref, xb_ref, xc_ref, dt_ref, cw_ref, cb_ref, dtb_ref, aneg_ref, dskip_ref, norm_ref,
                buf0_ref, s0_ref, o_ref, s_out_ref, s_scr, prev_scr, *, valid):
    c_idx = pl.program_id(1)
    chunk = z_ref.shape[0]
    nt = (((1,), (1,)), ((), ()))
    tn = (((0,), (0,)), ((), ()))

    @pl.when(c_idx == 0)
    def _():
        s_scr[...] = s0_ref[0]
        prev_scr[...] = buf0_ref[0]

    x = jnp.concatenate([xa_ref[...], xb_ref[...], xc_ref[...]], axis=1)
    prev = prev_scr[...]
    row8 = lax.broadcasted_iota(jnp.int32, prev.shape, 0)
    conv = x * cw_ref[M2_CONV - 1:M2_CONV, :]
    for k in range(1, M2_CONV):
        rolled = pltpu.roll(x, k, 0)
        top = jnp.where(row8 < k, pltpu.roll(prev, k, 0), rolled[0:SUBLANES])
        shifted = top if chunk == SUBLANES else jnp.concatenate([top, rolled[SUBLANES:]], axis=0)
        conv = conv + shifted * cw_ref[M2_CONV - 1 - k:M2_CONV - k, :]
    prev_scr[...] = x[chunk - SUBLANES:chunk]
    conv = conv + cb_ref[...]
    xbc = conv * jax.nn.sigmoid(conv)
    xs = xbc[:, :M2_DINNER]

    dt_raw = dt_ref[...] + dtb_ref[...]
    dt = jnp.where(dt_raw > 20.0, dt_raw, jnp.log1p(jnp.exp(jnp.minimum(dt_raw, 20.0))))
    if valid < chunk:
        dt = jnp.where(lax.broadcasted_iota(jnp.int32, dt.shape, 0) < valid, dt, 0.0)
    a = dt * aneg_ref[...]
    ri = lax.broadcasted_iota(jnp.int32, (chunk, chunk), 0)
    ci = lax.broadcasted_iota(jnp.int32, (chunk, chunk), 1)
    tri = ri >= ci
    cum = _exact_dot(jnp.where(tri, 1.0, 0.0).astype(BF16), a)
    cum_t = cum.T
    cum_last = cum[chunk - 1:chunk, :]
    hi = lax.broadcasted_iota(jnp.int32, (LANES, M2_DINNER), 0)
    li = lax.broadcasted_iota(jnp.int32, (LANES, M2_DINNER), 1)
    expand = jnp.where(hi == li // M2_HEADDIM, 1.0, 0.0).astype(BF16)
    dt_x = _exact_dot_r(dt, expand)
    cum_x = _exact_dot_r(cum, expand)
    last_x = _exact_dot_r(cum_last, expand)
    xdt = xs * dt_x
    x_dec = (xdt * jnp.exp(last_x - cum_x)).astype(BF16)
    xdt_bf = xdt.astype(BF16)
    e_cum_x = jnp.exp(cum_x)
    e_last_x = jnp.exp(last_x)

    y_parts = []
    for gi in range(M2_GROUPS):
        b_g = xbc[:, M2_DINNER + gi * M2_STATE:M2_DINNER + (gi + 1) * M2_STATE].astype(BF16)
        c_g = xbc[:, M2_DINNER + (M2_GROUPS + gi) * M2_STATE:M2_DINNER + (M2_GROUPS + gi + 1) * M2_STATE].astype(BF16)
        cb = lax.dot_general(c_g, b_g, nt, preferred_element_type=F32)
        gcols = slice(gi * HEADS_PER_GROUP * M2_HEADDIM, (gi + 1) * HEADS_PER_GROUP * M2_HEADDIM)
        s_old = s_scr[:, gcols]
        y_state = jnp.dot(c_g, s_old.astype(BF16), preferred_element_type=F32) * e_cum_x[:, gcols]
        s_scr[:, gcols] = s_old * e_last_x[:, gcols] + lax.dot_general(b_g, x_dec[:, gcols], tn,
                                                                        preferred_element_type=F32)
        pair_lane = lax.broadcasted_iota(jnp.int32, (chunk, LANES), 1)
        intra = []
        for pr in range(HEADS_PER_GROUP // 2):
            outs = []
            for sub in range(2):
                h = gi * HEADS_PER_GROUP + pr * 2 + sub
                seg = cum[:, h:h + 1] - cum_t[h:h + 1, :]
                l_mat = jnp.where(tri, jnp.exp(jnp.where(tri, seg, 0.0)), 0.0)
                lanes = slice(gi * HEADS_PER_GROUP * M2_HEADDIM + pr * LANES,
                              gi * HEADS_PER_GROUP * M2_HEADDIM + (pr + 1) * LANES)
                outs.append(jnp.dot((cb * l_mat).astype(BF16), xdt_bf[:, lanes], preferred_element_type=F32))
            intra.append(jnp.where(pair_lane < M2_HEADDIM, outs[0], outs[1]))
        y_parts.append(jnp.concatenate(intra, axis=1) + y_state)
    y = jnp.concatenate(y_parts, axis=1) + dskip_ref[...] * xs
    z = z_ref[...]
    y = y * (z * jax.nn.sigmoid(z))
    gw = M2_DINNER // M2_GROUPS
    outs = []
    for gi in range(M2_GROUPS):
        yg = y[:, gi * gw:(gi + 1) * gw]
        outs.append(yg * lax.rsqrt(jnp.mean(yg * yg, axis=-1, keepdims=True) + EPS))
    o_ref[...] = (jnp.concatenate(outs, axis=1) * norm_ref[...]).astype(o_ref.dtype)
    s_out_ref[0] = s_scr[...]


def ssd(proj, n_batch, t_len, chunk, conv_buf, state0, conv_w, conv_b, dt_bias, a_log, d_skip, m2_norm, valid=None):
    nc = t_len // chunk
    z_blk = SSD_COL0 // M2_DINNER
    xw = M2_CONV_DIM // 3
    xbc_blk = (SSD_COL0 + M2_DINNER) // xw
    assert (SSD_COL0 + M2_DINNER) % xw == 0 and xw % LANES == 0
    dt_blk = (SSD_COL0 + M2_DINNER + M2_CONV_DIM) // LANES
    pad = lambda v: jnp.pad(v.reshape(1, -1), ((0, 0), (0, LANES - v.shape[-1])))
    buf8 = jnp.pad(conv_buf, ((0, 0), (SUBLANES - (M2_CONV - 1), 0), (0, 0)))
    st_t = state0.transpose(0, 3, 1, 2).reshape(n_batch, M2_STATE, M2_DINNER)
    row = lambda a: pl.BlockSpec(a.shape, lambda b, c: (0, 0))
    cw = conv_w
    cb = conv_b.reshape(1, -1)
    dtb, aneg = pad(dt_bias), pad(-jnp.exp(a_log))
    dsk = jnp.repeat(d_skip, M2_HEADDIM).reshape(1, -1)
    nrm = m2_norm.reshape(1, -1)
    st = pl.BlockSpec((1, M2_STATE, M2_DINNER), lambda b, c: (b, 0, 0))
    out, s_fin = pl.pallas_call(
        functools.partial(_ssd_kernel, valid=chunk if valid is None else valid),
        grid=(n_batch, nc),
        in_specs=[pl.BlockSpec((chunk, M2_DINNER), lambda b, c: (b * nc + c, z_blk)),
                  pl.BlockSpec((chunk, xw), lambda b, c: (b * nc + c, xbc_blk)),
                  pl.BlockSpec((chunk, xw), lambda b, c: (b * nc + c, xbc_blk + 1)),
                  pl.BlockSpec((chunk, xw), lambda b, c: (b * nc + c, xbc_blk + 2)),
                  pl.BlockSpec((chunk, LANES), lambda b, c: (b * nc + c, dt_blk)),
                  row(cw), row(cb), row(dtb), row(aneg), row(dsk), row(nrm),
                  pl.BlockSpec((1, SUBLANES, M2_CONV_DIM), lambda b, c: (b, 0, 0)), st],
        out_specs=[pl.BlockSpec((chunk, M2_DINNER), lambda b, c: (b * nc + c, 0)), st],
        out_shape=[jax.ShapeDtypeStruct((n_batch * t_len, M2_DINNER), BF16),
                   jax.ShapeDtypeStruct((n_batch, M2_STATE, M2_DINNER), F32)],
        scratch_shapes=[pltpu.VMEM((M2_STATE, M2_DINNER), F32), pltpu.VMEM((SUBLANES, M2_CONV_DIM), F32)],
        compiler_params=pltpu.CompilerParams(
            dimension_semantics=("parallel", "arbitrary"), vmem_limit_bytes=VMEM_LIMIT),
        name="ssd",
    )(proj, proj, proj, proj, proj, cw, cb, dtb, aneg, dsk, nrm, buf8, st_t)
    s_fin = s_fin.reshape(n_batch, M2_STATE, M2_HEADS, M2_HEADDIM).transpose(0, 2, 3, 1)
    return out, s_fin


SEQ_ROWS = SUBLANES


def _pad_seq_rows(a, n_seq, t_len):
    assert t_len <= SEQ_ROWS
    a = a.reshape(n_seq, t_len, a.shape[-1])
    return jnp.pad(a, ((0, 0), (0, SEQ_ROWS - t_len), (0, 0))).reshape(n_seq * SEQ_ROWS, a.shape[-1])


def _unpad_seq_rows(a, n_seq, t_len):
    return a.reshape(n_seq, SEQ_ROWS, a.shape[-1])[:, :t_len].reshape(n_seq * t_len, a.shape[-1])

def kernel(x_prompt, x_sample, cache_nsa_kv, cache_nsa_win, state_sc_conv, state_ret, state_ssm, state_m2_conv,
           page_table, norm_w, final_norm, e_w_in, e_w_out, e_cmp_pe, e_cmp_w1, e_cmp_w2, e_sc_conv, o_w_in,
           o_w_out, o_ret_gn, o_m2_conv_w, o_m2_conv_b, o_m2_dt_bias, o_m2_a_log, o_m2_d, o_m2_norm, peer_wq,
           peer_keys, peer_u, peer_v):
    bp, tp, dm = x_prompt.shape
    bs, ts, _ = x_sample.shape
    n_p, n_s = bp * tp, bs * ts
    g, r, d = NSA_KV_HEADS, NSA_GROUP, NSA_HEAD_DIM
    past_len = page_table.shape[1] * PAGE_SIZE
    pos_p = jnp.arange(tp, dtype=jnp.int32)
    pos_s = past_len + jnp.arange(ts, dtype=jnp.int32)
    n_real = n_p + n_s
    n_pad = _round_up(n_real, PEER_TM) - n_real
    s_rows = slice(n_p, n_real)
    pos_rows = jnp.concatenate([jnp.tile(pos_p, bp), jnp.tile(pos_s, bs), jnp.zeros((n_pad,), jnp.int32)])
    xp = x_prompt.reshape(n_p, dm)
    x = jnp.concatenate([xp, x_sample.reshape(n_s, dm), xp[:n_pad]], axis=0)
    pad_bf = jnp.zeros((n_pad, E_Q), BF16)

    w0 = e_w_in[0]
    w_in = jnp.concatenate([w0[:, :SC_COL0], w0[:, SC_COL0 + E_G:], w0[:, SC_COL0:SC_COL0 + E_G],
                            jnp.zeros((dm, _pad_cols(E_IN) - E_IN), F32)], axis=1).astype(BF16)
    proj = norm_matmul(x, norm_w[0, 0], w_in)
    q_bf, rows, win, rows_bf, win_bf = nsa_rope(proj, pos_rows)
    kvc = nsa_compress(rows, bp, tp, e_cmp_pe[0], e_cmp_w1[0], e_cmp_w2[0])
    o_nsa_p = nsa_prompt(q_bf, kvc, rows_bf, win_bf, proj, bp, tp)
    p_kv = rows[:n_p].reshape(bp, tp, 4, g, d)
    s_kv = rows[s_rows].reshape(bs, ts, 4, g, d)
    p_win = win[:n_p].reshape(bp, tp, 2, g, d)[:, tp - min(WINDOW, tp):]
    win_s = win[s_rows].reshape(bs, ts, 2, g, d)
    s_win = jnp.concatenate([cache_nsa_win[0], win_s], axis=1)[:, ts:]
    cache = cache_nsa_kv[0]
    q_s = q_bf[s_rows].reshape(bs, ts, g, r, d).transpose(0, 2, 3, 1, 4).reshape(bs, g, r * ts, d)
    a0, a1 = nsa_sample_compress(cache, page_table, e_cmp_pe[0], e_cmp_w1[0])
    o_cmp_s, sel_s = nsa_sample_select(a0, a1, e_cmp_w2[0], q_s, past_len, ts)
    o_nsa_s = nsa_sample_attend(cache, page_table, q_s, sel_s, o_cmp_s, rows, win,
                                cache_nsa_win[0], proj, past_len, ts, n_p)
    o_nsa_s = o_nsa_s.reshape(bs, g, r, ts, d).transpose(0, 3, 1, 2, 4).reshape(n_s, E_Q)
    o_nsa = jnp.concatenate([o_nsa_p, o_nsa_s.astype(BF16), pad_bf], axis=0)
    o_sc_p, tail_p = short_conv(proj, bp, tp, SC_ROWS, jnp.zeros((bp, SC_KSIZE - 1, SC_WIDTH), F32), e_sc_conv[0])
    p_sc = tail_p[:, SUBLANES - (SC_KSIZE - 1):]
    o_sc_s, tail_s = short_conv(_pad_seq_rows(proj[s_rows], bs, ts), bs, SEQ_ROWS, SEQ_ROWS, state_sc_conv[0],
                                e_sc_conv[0])
    s_sc = jnp.concatenate([state_sc_conv[0], tail_s[:, :ts]], axis=1)[:, ts:]
    o_sc = jnp.concatenate([_unpad_seq_rows(o_sc_s, bs, ts), pad_bf], axis=0)
    o_sc = jnp.concatenate([o_sc_p, o_sc], axis=0)
    x = matmul2_res(o_nsa, o_sc, e_w_out[0].astype(BF16), x)
    u_bf = peer_u.astype(BF16)
    vt_bf = peer_v.transpose(0, 2, 1).astype(BF16)
    peer_t = peer_layer(x, norm_w[0, 1], peer_wq[0], peer_keys[0], u_bf, vt_bf, 0)

    w_in = jnp.pad(o_w_in[0], ((0, 0), (0, _pad_cols(O_IN) - O_IN))).astype(BF16)
    x = x + peer_t.T
    proj = norm_matmul(x, norm_w[1, 0], w_in)
    odd_w = (o_m2_conv_w[0], o_m2_conv_b[0], o_m2_dt_bias[0], o_m2_a_log[0], o_m2_d[0], o_m2_norm[0])
    o_ret_p, p_ret = retention(proj, pos_p, bp, tp, CHUNK, jnp.zeros((bp, RET_HEADS, RET_DQK, RET_DV), F32),
                               o_ret_gn[0])
    o_ssd_p, p_ssm = ssd(proj, bp, tp, CHUNK, jnp.zeros((bp, M2_CONV - 1, M2_CONV_DIM), F32),
                         jnp.zeros((bp, M2_HEADS, M2_HEADDIM, M2_STATE), F32), *odd_w)
    xbc_cols = slice(O_SPLITS[4], O_SPLITS[5])
    p_m2c = jnp.stack([proj[(b + 1) * tp - (M2_CONV - 1):(b + 1) * tp, xbc_cols] for b in range(bp)])
    xbc_s = proj[s_rows, xbc_cols].reshape(bs, ts, M2_CONV_DIM)
    s_m2c = jnp.concatenate([state_m2_conv[0], xbc_s], axis=1)[:, ts:]
    proj_s = _pad_seq_rows(proj[s_rows], bs, ts)
    pos_s8 = past_len + jnp.arange(SEQ_ROWS, dtype=jnp.int32)
    o_ret_s, s_ret = retention(proj_s, pos_s8, bs, SEQ_ROWS, SEQ_ROWS, state_ret[0], o_ret_gn[0], valid=ts)
    o_ssd_s, s_ssm = ssd(proj_s, bs, SEQ_ROWS, SEQ_ROWS, state_m2_conv[0], state_ssm[0], *odd_w, valid=ts)
    o_ret = jnp.concatenate([o_ret_p, _unpad_seq_rows(o_ret_s, bs, ts), pad_bf], axis=0)
    o_ssd = jnp.concatenate([o_ssd_p, _unpad_seq_rows(o_ssd_s, bs, ts), pad_bf], axis=0)
    x = matmul2_res(o_ret, o_ssd, o_w_out[0].astype(BF16), x)
    peer_t = peer_layer(x, norm_w[1, 1], peer_wq[1], peer_keys[1], u_bf, vt_bf, 1)

    y = resid_rmsnorm(x, peer_t, final_norm)
    y_prompt = y[:n_p].reshape(bp, tp, dm)
    y_sample = y[s_rows].reshape(bs, ts, dm)
    return (y_prompt, y_sample, p_kv[None], p_win[None], p_sc[None], p_ret[None], p_ssm[None], p_m2c[None],
            s_kv[None], s_win[None], s_sc[None], s_ret[None], s_ssm[None], s_m2c[None])
```

```python
import functools
import math

import jax
import jax.numpy as jnp
from jax import lax
from jax.experimental import pallas as pl
from jax.experimental.pallas import tpu as pltpu

F32 = jnp.float32
BF16 = jnp.bfloat16

D_MODEL = 2048
DEPTH = 2
PAGE_SIZE = 128
NSA_HEAD_DIM = 128
NSA_HEADS = 8
NSA_KV_HEADS = 2
NSA_GROUP = 4
CMP_BLOCK = 32
CMP_STRIDE = 16
SLC_BLOCK = 64
SLC_TOPK = 16
WINDOW = 512
ROPE_THETA = 500000.0
ROPE_DIMS = 32
SC_WIDTH = 1024
SC_KSIZE = 3
RET_HEADS = 4
RET_DQK = 256
RET_DV = 256
RET_THETA = 10000.0
M2_DINNER = 1024
M2_HEADDIM = 64
M2_HEADS = 16
M2_STATE = 128
M2_GROUPS = 2
M2_CONV = 4
M2_CONV_DIM = M2_DINNER + 2 * M2_GROUPS * M2_STATE
PEER_HEADS = 8
PEER_KEYS = 128
PEER_QDIM = 256
PEER_TOPK = 16
Q_BLOCK = 128
CHUNK = 128
EPS = 1e-6

E_Q = NSA_HEADS * NSA_HEAD_DIM
E_KV = 6 * NSA_KV_HEADS * NSA_HEAD_DIM
E_G = 3 * NSA_HEADS
E_SC = 3 * SC_WIDTH
E_IN = E_Q + E_KV + E_G + E_SC
R_QK = RET_HEADS * RET_DQK
R_V = RET_HEADS * RET_DV
O_SPLITS = [R_QK, 2 * R_QK, 2 * R_QK + R_V, 2 * R_QK + 2 * R_V,
            2 * R_QK + 2 * R_V + M2_DINNER, 2 * R_QK + 2 * R_V + M2_DINNER + M2_CONV_DIM]
O_IN = O_SPLITS[-1] + M2_HEADS
SC_COL0 = E_Q + E_KV
GATE_COL0 = SC_COL0 + E_SC

LANES = 128
SUBLANES = 8
VMEM_LIMIT = 56 * 1024 * 1024
ROW_TILE_CAP = 1056
ROPE_ROWS_CAP = 544
RESID_ROWS = 768
NEG_INF = float("-inf")


def _round_up(n, m):
    return -(-n // m) * m


def _pick_tile(n, cap):
    best = LANES
    for t in range(LANES, cap + 1, LANES):
        if n % t == 0:
            best = t
    return best


def _pad_cols(m):
    return min((_round_up(m, t) for t in (768, 640, 512)))


def _row_tile(n, cap):
    best = None
    for t in range(16, cap + 1, 16):
        if n % t == 0:
            best = t
    assert best is not None
    return best


def _gelu_tanh(x):
    return 0.5 * x * (1.0 + jnp.tanh(math.sqrt(2.0 / math.pi) * (x + 0.044715 * (x * x * x))))


def _norm_matmul_kernel(x_ref, g_ref, w_ref, o_ref, xn_ref):
    @pl.when(pl.program_id(1) == 0)
    def _():
        x = x_ref[...]
        ms = jnp.mean(x * x, axis=-1, keepdims=True)
        xn_ref[...] = (x * lax.rsqrt(ms + EPS) * g_ref[...]).astype(BF16)

    o_ref[...] = jnp.dot(xn_ref[...], w_ref[...], preferred_element_type=F32)


def norm_matmul(x, gain, w_bf):
    n, k = x.shape
    m = w_bf.shape[1]
    tm = _row_tile(n, ROW_TILE_CAP)
    tn = _pick_tile(m, 768)
    return pl.pallas_call(
        _norm_matmul_kernel,
        grid=(n // tm, m // tn),
        in_specs=[pl.BlockSpec((tm, k), lambda i, j: (i, 0)),
                  pl.BlockSpec((1, k), lambda i, j: (0, 0)),
                  pl.BlockSpec((k, tn), lambda i, j: (0, j))],
        out_specs=pl.BlockSpec((tm, tn), lambda i, j: (i, j)),
        out_shape=jax.ShapeDtypeStruct((n, m), F32),
        scratch_shapes=[pltpu.VMEM((tm, k), BF16)],
        compiler_params=pltpu.CompilerParams(
            dimension_semantics=("parallel", "arbitrary"), vmem_limit_bytes=VMEM_LIMIT),
        name="norm_matmul",
    )(x, gain.reshape(1, k), w_bf)


def _matmul2_res_kernel(a1_ref, a2_ref, w1_ref, w2_ref, r_ref, o_ref):
    o_ref[...] = (r_ref[...] + jnp.dot(a1_ref[...], w1_ref[...], preferred_element_type=F32)
                  + jnp.dot(a2_ref[...], w2_ref[...], preferred_element_type=F32))


def matmul2_res(a1, a2, w_bf, res):
    n, k1 = a1.shape
    k2 = a2.shape[1]
    m = w_bf.shape[1]
    assert k1 == k2
    tm = _row_tile(n, ROW_TILE_CAP)
    tn = _pick_tile(m, 1024)
    return pl.pallas_call(
        _matmul2_res_kernel,
        grid=(n // tm, m // tn),
        in_specs=[pl.BlockSpec((tm, k1), lambda i, j: (i, 0)),
                  pl.BlockSpec((tm, k2), lambda i, j: (i, 0)),
                  pl.BlockSpec((k1, tn), lambda i, j: (0, j)),
                  pl.BlockSpec((k2, tn), lambda i, j: (1, j)),
                  pl.BlockSpec((tm, tn), lambda i, j: (i, j))],
        out_specs=pl.BlockSpec((tm, tn), lambda i, j: (i, j)),
        out_shape=jax.ShapeDtypeStruct((n, m), F32),
        compiler_params=pltpu.CompilerParams(
            dimension_semantics=("parallel", "parallel"), vmem_limit_bytes=VMEM_LIMIT),
        name="matmul2_res",
    )(a1, a2, w_bf, w_bf, res)


PEER_TOK = 128
PEER_HEAD_UNROLL = 8
PEER_TM = 768
PEER_TA = 8
PEER_TE = PEER_TA * PEER_KEYS


def _top_desc(work, count, with_rank=False, one_at_a_time=False):
    rows = []
    rank = jnp.full(work.shape, float(count), F32)
    row_id = lax.broadcasted_iota(jnp.int32, work.shape, 0).astype(F32)
    for r in range(count):
        m = jnp.max(work, axis=0, keepdims=True)
        rows.append(m)
        hit = work == m
        if one_at_a_time:
            hit = row_id == jnp.min(jnp.where(hit, row_id, float(work.shape[0])), axis=0, keepdims=True)
        if with_rank:
            rank = jnp.where(hit, float(r), rank)
        work = jnp.where(hit, NEG_INF, work)
    return (rows, rank) if with_rank else rows


def _peer_router_kernel(q_ref, k_ref, cnt_ref, gw_ref, r2_ref, w2_ref):
    def head(h, carry):
        q = q_ref[h]
        scores = []
        for side in range(2):
            qs = q[:, side * PEER_KEYS:(side + 1) * PEER_KEYS]
            qs = qs * lax.rsqrt(jnp.mean(qs * qs, axis=-1, keepdims=True) + EPS)
            scores.append(lax.dot_general(k_ref[h, side], qs, (((1,), (1,)), ((), ())),
                                          preferred_element_type=F32))
        s1, s2 = scores
        v1 = _top_desc(s1, PEER_TOPK + 1)
        v2, rank2 = _top_desc(s2, PEER_TOPK + 1, with_rank=True)
        v2_lo = jnp.concatenate(v2[:8], axis=0)
        v2_hi = jnp.concatenate(v2[8:16], axis=0)
        row = lax.broadcasted_iota(jnp.int32, v2_lo.shape, 0)
        blocks = [v1[0] + v2_lo, v1[0] + v2_hi, v1[1] + v2_lo]
        for a, lim in ((2, 5), (3, 4), (4, 3), (5, 2), (6, 2), (7, 2)):
            blocks.append(jnp.where(row < lim, v1[a] + v2_lo, NEG_INF))
        blocks.append(jnp.concatenate(v1[8:16], axis=0) + v2[0])
        extra = jnp.where(row == 0, v1[0] + v2[16], jnp.where(row == 1, v1[16] + v2[0], NEG_INF))
        blocks.append(extra)
        cand = jnp.concatenate(blocks, axis=0)
        tops = _top_desc(cand, PEER_TOPK + 1, one_at_a_time=True)
        z = jnp.zeros_like(tops[0])
        for r in range(PEER_TOPK):
            z = z + jnp.exp(tops[r] - tops[0])
        tau = 0.5 * (tops[PEER_TOPK - 1] + tops[PEER_TOPK])
        count = jnp.zeros_like(s1)
        for r in range(PEER_TOPK):
            count = count + jnp.where(s1 >= tau - v2[r], 1.0, 0.0)
        cnt_ref[h] = count
        gw_ref[h] = jnp.exp(s1 - v1[0]) / z
        r2_ref[h] = rank2.astype(BF16)
        w2_ref[h] = jnp.exp(s2 - v2[0]).astype(BF16)
        return carry

    lax.fori_loop(0, PEER_HEADS, head, 0, unroll=PEER_HEAD_UNROLL)


def peer_router(q_hm, keys):
    n = q_hm.shape[1]
    out = jax.ShapeDtypeStruct((PEER_HEADS, PEER_KEYS, n), F32)
    out_bf = jax.ShapeDtypeStruct((PEER_HEADS, PEER_KEYS, n), BF16)
    spec = pl.BlockSpec((PEER_HEADS, PEER_KEYS, PEER_TOK), lambda i: (0, 0, i))
    return pl.pallas_call(
        _peer_router_kernel,
        grid=(n // PEER_TOK,),
        in_specs=[pl.BlockSpec((PEER_HEADS, PEER_TOK, PEER_QDIM), lambda i: (0, i, 0)),
                  pl.BlockSpec((PEER_HEADS, 2, PEER_KEYS, PEER_QDIM // 2), lambda i: (0, 0, 0, 0))],
        out_specs=[spec, spec, spec, spec],
        out_shape=[out, out, out_bf, out_bf],
        compiler_params=pltpu.CompilerParams(dimension_semantics=("parallel",), vmem_limit_bytes=VMEM_LIMIT),
        name="peer_router",
    )(q_hm, keys)


def _peer_expert_kernel(xt_ref, u_ref, vt_ref, cnt_ref, gw_ref, r2_ref, w2_ref, o_ref, s_scr, hg_scr):
    e = pl.program_id(1)
    s_scr[...] = jnp.dot(u_ref[...], xt_ref[...], preferred_element_type=F32)
    zero = jnp.zeros((), BF16)
    for al in range(PEER_TA):
        rows = slice(al * PEER_KEYS, (al + 1) * PEER_KEYS)
        g = None
        for h in range(PEER_HEADS):
            cnt = cnt_ref[h, al:al + 1, :].astype(BF16)
            gate = gw_ref[h, al:al + 1, :].astype(BF16)
            t = jnp.where(r2_ref[h] < cnt, w2_ref[h], zero) * gate
            g = t if g is None else g + t
        hg_scr[rows, :] = _gelu_tanh(s_scr[rows, :]).astype(BF16) * g
    part = jnp.dot(vt_ref[...], hg_scr[...], preferred_element_type=F32)

    @pl.when(e == 0)
    def _():
        o_ref[...] = part

    @pl.when(e != 0)
    def _():
        o_ref[...] += part


def peer_experts(xt_bf, u_bf, vt_bf, layer, cnt, gw, r2, w2):
    d, n = xt_bf.shape
    n_exp = u_bf.shape[1]
    tm = PEER_TM
    sel_spec = pl.BlockSpec((PEER_HEADS, PEER_TA, tm), lambda i, e: (0, e, i))
    all_spec = pl.BlockSpec((PEER_HEADS, PEER_KEYS, tm), lambda i, e: (0, 0, i))
    return pl.pallas_call(
        _peer_expert_kernel,
        grid=(n // tm, n_exp // PEER_TE),
        in_specs=[pl.BlockSpec((d, tm), lambda i, e: (0, i)),
                  pl.BlockSpec((None, PEER_TE, d), lambda i, e: (layer, e, 0)),
                  pl.BlockSpec((None, d, PEER_TE), lambda i, e: (layer, 0, e)),
                  sel_spec, sel_spec, all_spec, all_spec],
        out_specs=pl.BlockSpec((d, tm), lambda i, e: (0, i)),
        out_shape=jax.ShapeDtypeStruct((d, n), F32),
        scratch_shapes=[pltpu.VMEM((PEER_TE, tm), F32), pltpu.VMEM((PEER_TE, tm), BF16)],
        compiler_params=pltpu.CompilerParams(
            dimension_semantics=("parallel", "arbitrary"), vmem_limit_bytes=VMEM_LIMIT),
        name="peer_experts",
    )(xt_bf, u_bf, vt_bf, cnt, gw, r2, w2)


def _norm_matmul_t_kernel(x_ref, g_ref, w_ref, o_ref, xt_ref, xn_ref):
    @pl.when(pl.program_id(1) == 0)
    def _():
        x = x_ref[...]
        ms = jnp.mean(x * x, axis=-1, keepdims=True)
        xn = x * lax.rsqrt(ms + EPS) * g_ref[...]
        xn_ref[...] = xn.astype(BF16)
        xt_ref[...] = xn.T.astype(BF16)

    o_ref[...] = jnp.dot(xn_ref[...], w_ref[...], preferred_element_type=F32).reshape(o_ref.shape)


def norm_matmul_t(x, gain, w_bf, cols):
    n, k = x.shape
    m = w_bf.shape[1]
    tm = PEER_TM
    return pl.pallas_call(
        _norm_matmul_t_kernel,
        grid=(n // tm, m // cols),
        in_specs=[pl.BlockSpec((tm, k), lambda i, j: (i, 0)),
                  pl.BlockSpec((1, k), lambda i, j: (0, 0)),
                  pl.BlockSpec((k, cols), lambda i, j: (0, j))],
        out_specs=[pl.BlockSpec((1, tm, cols), lambda i, j: (j, i, 0)),
                   pl.BlockSpec((k, tm), lambda i, j: (0, i))],
        out_shape=[jax.ShapeDtypeStruct((m // cols, n, cols), F32), jax.ShapeDtypeStruct((k, n), BF16)],
        scratch_shapes=[pltpu.VMEM((tm, k), BF16)],
        compiler_params=pltpu.CompilerParams(
            dimension_semantics=("parallel", "arbitrary"), vmem_limit_bytes=VMEM_LIMIT),
        name="norm_matmul_t",
    )(x, gain.reshape(1, k), w_bf)


def _resid_rmsnorm_kernel(x_ref, dt_ref, g_ref, o_ref):
    x = x_ref[...] + dt_ref[...].T
    ms = jnp.mean(x * x, axis=-1, keepdims=True)
    o_ref[...] = x * lax.rsqrt(ms + EPS) * g_ref[...]


def resid_rmsnorm(x, delta_t, gain):
    n, k = x.shape
    tm = _pick_tile(n, RESID_ROWS)
    return pl.pallas_call(
        _resid_rmsnorm_kernel,
        grid=(n // tm,),
        in_specs=[pl.BlockSpec((tm, k), lambda i: (i, 0)), pl.BlockSpec((k, tm), lambda i: (0, i)),
                  pl.BlockSpec((1, k), lambda i: (0, 0))],
        out_specs=pl.BlockSpec((tm, k), lambda i: (i, 0)),
        out_shape=jax.ShapeDtypeStruct((n, k), F32),
        compiler_params=pltpu.CompilerParams(dimension_semantics=("parallel",), vmem_limit_bytes=VMEM_LIMIT),
        name="resid_rmsnorm",
    )(x, delta_t, gain.reshape(1, k))


def peer_layer(x, gain, w_q, keys, u_bf, vt_bf, layer):
    q_hm, xt_bf = norm_matmul_t(x, gain, w_q.astype(BF16), PEER_QDIM)
    cnt, gw, r2, w2 = peer_router(q_hm, keys)
    return peer_experts(xt_bf, u_bf, vt_bf, layer, cnt, gw, r2, w2)


def _rope_tables(pos):
    half = ROPE_DIMS // 2
    inv = ROPE_THETA ** (-jnp.arange(half, dtype=F32) / half)
    ang = pos.astype(F32)[:, None] * inv[None, :]
    cos, sin = jnp.cos(ang), jnp.sin(ang)
    t = pos.shape[0]
    ones = jnp.ones((t, NSA_HEAD_DIM - ROPE_DIMS), F32)
    zeros = jnp.zeros((t, NSA_HEAD_DIM - ROPE_DIMS), F32)
    zh = jnp.zeros((t, half), F32)
    c = jnp.concatenate([cos, cos, ones], axis=1)
    s_lo = jnp.concatenate([-sin, zh, zeros], axis=1)
    s_hi = jnp.concatenate([zh, sin, zeros], axis=1)
    return c, s_lo, s_hi


def _rope_kernel(p_ref, c_ref, sl_ref, sh_ref, q_ref, rows_ref, win_ref, rows_bf_ref, win_bf_ref):
    c, sl, sh = c_ref[...], sl_ref[...], sh_ref[...]
    half = ROPE_DIMS // 2

    def rot(x):
        return x * c + pltpu.roll(x, LANES - half, 1) * sl + pltpu.roll(x, half, 1) * sh

    d = NSA_HEAD_DIM
    scale = d ** -0.5
    for hd in range(NSA_HEADS):
        q_ref[:, hd * d:(hd + 1) * d] = (rot(p_ref[:, hd * d:(hd + 1) * d]) * scale).astype(BF16)
    for blk in range(12):
        x = p_ref[:, E_Q + blk * d:E_Q + (blk + 1) * d]
        if (blk // 2) % 2 == 0:
            x = rot(x)
        if blk < 8:
            rows_ref[:, blk * d:(blk + 1) * d] = x
            rows_bf_ref[:, blk * d:(blk + 1) * d] = x.astype(BF16)
        else:
            win_ref[:, (blk - 8) * d:(blk - 7) * d] = x
            win_bf_ref[:, (blk - 8) * d:(blk - 7) * d] = x.astype(BF16)


def nsa_rope(proj, pos_rows):
    n = proj.shape[0]
    tm = _row_tile(n, ROPE_ROWS_CAP)
    c, sl, sh = _rope_tables(pos_rows)
    width = E_Q + E_KV
    tab = pl.BlockSpec((tm, LANES), lambda i: (i, 0))
    return pl.pallas_call(
        _rope_kernel,
        grid=(n // tm,),
        in_specs=[pl.BlockSpec((tm, width), lambda i: (i, 0)), tab, tab, tab],
        out_specs=[pl.BlockSpec((tm, E_Q), lambda i: (i, 0)),
                   pl.BlockSpec((tm, 1024), lambda i: (i, 0)),
                   pl.BlockSpec((tm, 512), lambda i: (i, 0)),
                   pl.BlockSpec((tm, 1024), lambda i: (i, 0)),
                   pl.BlockSpec((tm, 512), lambda i: (i, 0))],
        out_shape=[jax.ShapeDtypeStruct((n, E_Q), BF16),
                   jax.ShapeDtypeStruct((n, 1024), F32),
                   jax.ShapeDtypeStruct((n, 512), F32),
                   jax.ShapeDtypeStruct((n, 1024), BF16),
                   jax.ShapeDtypeStruct((n, 512), BF16)],
        compiler_params=pltpu.CompilerParams(dimension_semantics=("parallel",), vmem_limit_bytes=VMEM_LIMIT),
        name="nsa_rope",
    )(proj, c, sl, sh)


def _compress_kernel(x_ref, pe_ref, w1_ref, w2_ref, o_ref):
    n_sub = x_ref.shape[0] // CMP_STRIDE
    acc0 = jnp.zeros((n_sub, NSA_HEAD_DIM), F32)
    acc1 = jnp.zeros((n_sub, NSA_HEAD_DIM), F32)
    for s in range(CMP_STRIDE):
        xs = x_ref[pl.ds(s, n_sub, stride=CMP_STRIDE), :]
        a0 = (xs + pe_ref[0, s:s + 1, :]).astype(BF16)
        a1 = (xs + pe_ref[0, CMP_STRIDE + s:CMP_STRIDE + s + 1, :]).astype(BF16)
        acc0 = acc0 + jnp.dot(a0, w1_ref[0, s], preferred_element_type=F32)
        acc1 = acc1 + jnp.dot(a1, w1_ref[0, CMP_STRIDE + s], preferred_element_type=F32)
    pre = acc0 + pltpu.roll(acc1, n_sub - 1, 0)
    o_ref[0, 0] = jnp.dot(_gelu_tanh(pre).astype(BF16), w2_ref[0], preferred_element_type=F32).astype(BF16)


def nsa_compress(rows, n_batch, t_len, cmp_pe, cmp_w1, cmp_w2):
    n_sub = t_len // CMP_STRIDE
    d = NSA_HEAD_DIM
    w1 = cmp_w1.reshape(2, CMP_BLOCK, d, d).astype(BF16)
    return pl.pallas_call(
        _compress_kernel,
        grid=(n_batch, 4),
        in_specs=[pl.BlockSpec((t_len, d), lambda b, c: (b, c)),
                  pl.BlockSpec((1, CMP_BLOCK, d), lambda b, c: (c // 2, 0, 0)),
                  pl.BlockSpec((1, CMP_BLOCK, d, d), lambda b, c: (c // 2, 0, 0, 0)),
                  pl.BlockSpec((1, d, d), lambda b, c: (c // 2, 0, 0))],
        out_specs=pl.BlockSpec((1, 1, n_sub, d), lambda b, c: (b, c, 0, 0)),
        out_shape=jax.ShapeDtypeStruct((n_batch, 4, n_sub, d), BF16),
        compiler_params=pltpu.CompilerParams(
            dimension_semantics=("parallel", "parallel"), vmem_limit_bytes=VMEM_LIMIT),
        name="nsa_compress",
    )(rows, cmp_pe, w1, cmp_w2.astype(BF16))


NSA_TK = 512
NSA_WTILES = WINDOW // Q_BLOCK + 1


def _masked_softmax_rows(s, mask):
    s = jnp.where(mask, s, -1e30)
    p = jnp.exp(s - jnp.max(s, axis=-1, keepdims=True))
    p = jnp.where(mask, p, 0.0)
    return p / jnp.maximum(jnp.sum(p, axis=-1, keepdims=True), 1e-30)


def _nsa_prompt_kernel(q_ref, kc_ref, vc_ref, ks_ref, vs_ref, kw_ref, vw_ref, gate_ref, o_ref):
    g = pl.program_id(1)
    qi = pl.program_id(2)
    d, r, qb = NSA_HEAD_DIM, NSA_GROUP, Q_BLOCK
    n_cmp = kc_ref.shape[2]
    n_slc = ks_ref.shape[0] // SLC_BLOCK
    nt = (((1,), (1,)), ((), ()))
    q = jnp.concatenate([q_ref[:, h * d:(h + 1) * d] for h in range(r)], axis=0)
    t_col = qi * qb + lax.broadcasted_iota(jnp.int32, (qb, 1), 0)

    s = lax.dot_general(q, kc_ref[0, 0], nt, preferred_element_type=F32).reshape(r, qb, n_cmp)
    n_idx = lax.broadcasted_iota(jnp.int32, (qb, n_cmp), 1)
    c_mask = (n_idx * CMP_STRIDE + (CMP_BLOCK - 1) <= t_col)[None]
    p_c = _masked_softmax_rows(s, c_mask)
    o_cmp = jnp.dot(p_c.reshape(r * qb, n_cmp).astype(BF16), vc_ref[0, 0], preferred_element_type=F32)

    nb = LANES
    assert n_slc <= nb and qb == LANES
    p_sum = jnp.sum(p_c, axis=0)
    si = lax.broadcasted_iota(jnp.int32, (nb, n_cmp), 0) * SLC_BLOCK
    ci = lax.broadcasted_iota(jnp.int32, (nb, n_cmp), 1) * CMP_STRIDE
    overlap_t = jnp.where((ci < si + SLC_BLOCK) & (ci + CMP_BLOCK > si), 1.0, 0.0).astype(BF16)
    p_hi = p_sum.astype(BF16)
    p_lo = (p_sum - p_hi.astype(F32)).astype(BF16)
    imp_t = (lax.dot_general(overlap_t, p_hi, nt, preferred_element_type=F32)
             + lax.dot_general(overlap_t, p_lo, nt, preferred_element_type=F32))
    blk = lax.broadcasted_iota(jnp.int32, (nb, qb), 0)
    blk_f = blk.astype(F32)
    t_row = qi * qb + lax.broadcasted_iota(jnp.int32, (1, qb), 1)
    cur = t_row // SLC_BLOCK
    forced = (blk == 0) | (blk == cur) | (blk == cur - 1)
    work = jnp.where(blk * SLC_BLOCK <= t_row, jnp.where(forced, 1e6, imp_t), -1e6)
    work = jnp.where(blk < n_slc, work, NEG_INF)
    sel_t = jnp.zeros((nb, qb), F32)
    for _ in range(min(SLC_TOPK, n_slc)):
        m = jnp.max(work, axis=0, keepdims=True)
        first = jnp.min(jnp.where(work == m, blk_f, float(nb)), axis=0, keepdims=True)
        pick = blk_f == first
        sel_t = jnp.where(pick, 1.0, sel_t)
        work = jnp.where(pick, NEG_INF, work)
    sel_bf = sel_t.T.astype(BF16)

    tk = NSA_TK
    bpt = tk // SLC_BLOCK

    def slc_step(kt, carry):
        m_run, l_run, acc = carry
        start = pl.multiple_of(kt * tk, tk)
        k = ks_ref[pl.ds(start, tk), :]
        v = vs_ref[pl.ds(start, tk), :]
        s = lax.dot_general(q, k, nt, preferred_element_type=F32).reshape(r, qb, tk)
        ei = lax.broadcasted_iota(jnp.int32, (nb, tk), 0)
        ej = lax.broadcasted_iota(jnp.int32, (nb, tk), 1)
        expand = jnp.where(ei == kt * bpt + ej // SLC_BLOCK, 1.0, 0.0).astype(BF16)
        picked = jnp.dot(sel_bf, expand, preferred_element_type=F32)
        kpos = start + lax.broadcasted_iota(jnp.int32, (qb, tk), 1)
        mask = ((picked > 0.5) & (kpos <= t_col))[None]
        s = jnp.where(mask, s, -1e30)
        m_new = jnp.maximum(m_run, jnp.max(s, axis=-1, keepdims=True))
        alpha = jnp.exp(m_run - m_new)
        p = jnp.where(mask, jnp.exp(s - m_new), 0.0)
        l_new = alpha * l_run + jnp.sum(p, axis=-1, keepdims=True)
        pv = jnp.dot(p.reshape(r * qb, tk).astype(BF16), v, preferred_element_type=F32)
        acc = alpha.reshape(r * qb, 1) * acc + pv
        return m_new, l_new, acc

    init = (jnp.full((r, qb, 1), -1e30, F32), jnp.zeros((r, qb, 1), F32), jnp.zeros((r * qb, d), F32))
    _, l_fin, acc = lax.fori_loop(0, (qi * qb) // tk + 1, slc_step, init)
    o_slc = acc / jnp.maximum(l_fin.reshape(r * qb, 1), 1e-30)

    k_tiles, v_tiles, pos_tiles = [], [], []
    for j in range(NSA_WTILES):
        kt = qi - (NSA_WTILES - 1) + j
        ktc = jnp.maximum(kt, 0)
        start = pl.multiple_of(ktc * qb, qb)
        k_tiles.append(kw_ref[pl.ds(start, qb), :])
        v_tiles.append(vw_ref[pl.ds(start, qb), :])
        lane = lax.broadcasted_iota(jnp.int32, (qb, qb), 1)
        pos_tiles.append(jnp.where(kt >= 0, start + lane, -1))
    k_w = jnp.concatenate(k_tiles, axis=0)
    v_w = jnp.concatenate(v_tiles, axis=0)
    k_pos = jnp.concatenate(pos_tiles, axis=1)
    span = NSA_WTILES * qb
    s = lax.dot_general(q, k_w, nt, preferred_element_type=F32).reshape(r, qb, span)
    dpos = t_col - k_pos
    w_mask = ((dpos >= 0) & (dpos < WINDOW) & (k_pos >= 0))[None]
    p_w = _masked_softmax_rows(s, w_mask)
    o_win = jnp.dot(p_w.reshape(r * qb, span).astype(BF16), v_w, preferred_element_type=F32)

    sig = jax.nn.sigmoid(gate_ref[...])
    lane = lax.broadcasted_iota(jnp.int32, sig.shape, 1)
    for h in range(r):
        rows = slice(h * qb, (h + 1) * qb)
        out = jnp.zeros((qb, d), F32)
        for branch, o_b in enumerate((o_cmp, o_slc, o_win)):
            col = branch * NSA_HEADS + g * r + h
            gate = jnp.sum(jnp.where(lane == col, sig, 0.0), axis=-1, keepdims=True)
            out = out + gate * o_b[rows]
        o_ref[:, h * d:(h + 1) * d] = out.astype(o_ref.dtype)


def nsa_prompt(q_bf, kvc, rows_bf, win_bf, proj, n_batch, t_len):
    d, r, qb = NSA_HEAD_DIM, NSA_GROUP, Q_BLOCK
    nqb = t_len // qb
    n_sub = kvc.shape[2]
    gate_blk = GATE_COL0 // LANES
    seq = lambda col: pl.BlockSpec((t_len, d), lambda b, g, i: (b, col(g)))
    return pl.pallas_call(
        _nsa_prompt_kernel,
        grid=(n_batch, NSA_KV_HEADS, nqb),
        in_specs=[pl.BlockSpec((qb, r * d), lambda b, g, i: (b * nqb + i, g)),
                  pl.BlockSpec((1, 1, n_sub, d), lambda b, g, i: (b, g, 0, 0)),
                  pl.BlockSpec((1, 1, n_sub, d), lambda b, g, i: (b, 2 + g, 0, 0)),
                  seq(lambda g: 4 + g), seq(lambda g: 6 + g),
                  seq(lambda g: g), seq(lambda g: 2 + g),
                  pl.BlockSpec((qb, LANES), lambda b, g, i: (b * nqb + i, gate_blk))],
        out_specs=pl.BlockSpec((qb, r * d), lambda b, g, i: (b * nqb + i, g)),
        out_shape=jax.ShapeDtypeStruct((n_batch * t_len, E_Q), BF16),
        compiler_params=pltpu.CompilerParams(
            dimension_semantics=("parallel", "parallel", "arbitrary"), vmem_limit_bytes=VMEM_LIMIT),
        name="nsa_prompt",
    )(q_bf, kvc, kvc, rows_bf, rows_bf, win_bf, win_bf, proj)


SAMPLE_PAGES = 16
NEW_ROWS_BLK = 16


def _page_specs(kind):
    def spec(i):
        return pl.BlockSpec((1, PAGE_SIZE, 1, NSA_KV_HEADS, NSA_HEAD_DIM),
                            lambda s, c, pt: (pt[s, c * SAMPLE_PAGES + i], 0, kind, 0, 0))
    return [spec(i) for i in range(SAMPLE_PAGES)]


def _sample_compress_kernel(pt_ref, *refs):
    n_in = 2 * SAMPLE_PAGES
    pages = refs[:n_in]
    pe_ref, w1_ref, a0_ref, a1_ref = refs[n_in:]
    d = NSA_HEAD_DIM
    per_page = PAGE_SIZE // CMP_STRIDE
    rows = SAMPLE_PAGES * per_page
    gs = NSA_KV_HEADS * CMP_STRIDE
    for kv in range(2):
        by_offset = [jnp.swapaxes(pages[kv * SAMPLE_PAGES + p].reshape(NSA_KV_HEADS * PAGE_SIZE, d)[...]
                                  .reshape(per_page, gs, d), 0, 1) for p in range(SAMPLE_PAGES)]
        acc0 = jnp.zeros((NSA_KV_HEADS * rows, d), F32)
        acc1 = jnp.zeros((NSA_KV_HEADS * rows, d), F32)
        for s in range(CMP_STRIDE):
            xs = jnp.concatenate([by_offset[p][NSA_KV_HEADS * s + g]
                                  for g in range(NSA_KV_HEADS) for p in range(SAMPLE_PAGES)],
                                 axis=0)
            a0 = (xs + pe_ref[kv, s:s + 1, :]).astype(BF16)
            a1 = (xs + pe_ref[kv, CMP_STRIDE + s:CMP_STRIDE + s + 1, :]).astype(BF16)
            acc0 = acc0 + jnp.dot(a0, w1_ref[kv, s], preferred_element_type=F32)
            acc1 = acc1 + jnp.dot(a1, w1_ref[kv, CMP_STRIDE + s], preferred_element_type=F32)
        for g in range(NSA_KV_HEADS):
            a0_ref[0, kv * NSA_KV_HEADS + g] = acc0[g * rows:(g + 1) * rows]
            a1_ref[0, kv * NSA_KV_HEADS + g] = acc1[g * rows:(g + 1) * rows]


def nsa_sample_compress(cache, page_table, cmp_pe, cmp_w1):
    bs, n_pages = page_table.shape
    d = NSA_HEAD_DIM
    per_page = PAGE_SIZE // CMP_STRIDE
    n_sub = n_pages * per_page
    rows = SAMPLE_PAGES * per_page
    w1 = cmp_w1.reshape(2, CMP_BLOCK, d, d).astype(BF16)
    out = jax.ShapeDtypeStruct((bs, 4, n_sub, d), F32)
    ospec = pl.BlockSpec((1, 4, rows, d), lambda s, c, pt: (s, 0, c, 0))
    page_specs = _page_specs(0) + _page_specs(1)
    return pl.pallas_call(
        _sample_compress_kernel,
        grid_spec=pltpu.PrefetchScalarGridSpec(
            num_scalar_prefetch=1,
            grid=(bs, n_pages // SAMPLE_PAGES),
            in_specs=page_specs + [
                pl.BlockSpec((2, CMP_BLOCK, d), lambda s, c, pt: (0, 0, 0)),
                pl.BlockSpec((2, CMP_BLOCK, d, d), lambda s, c, pt: (0, 0, 0, 0))],
            out_specs=[ospec, ospec]),
        out_shape=[out, out],
        compiler_params=pltpu.CompilerParams(
            dimension_semantics=("parallel", "arbitrary"), vmem_limit_bytes=VMEM_LIMIT),
        name="nsa_sample_compress",
    )(page_table, *([cache] * (2 * SAMPLE_PAGES)), cmp_pe, w1)


def _sample_select_kernel(a0_ref, a1_ref, w2_ref, q_ref, ocmp_ref, sel_ref, *, past_len, t_len):
    d, r = NSA_HEAD_DIM, NSA_GROUP
    n_sub = a0_ref.shape[2]
    n_cmp = n_sub - 1
    n_slc = -(-(past_len + t_len) // SLC_BLOCK)
    lanes = sel_ref.shape[3]
    nt = (((1,), (1,)), ((), ()))
    rq = r * t_len
    for g in range(NSA_KV_HEADS):
        kc = jnp.dot(_gelu_tanh(a0_ref[0, g] + pltpu.roll(a1_ref[0, g], n_sub - 1, 0)).astype(BF16), w2_ref[0],
                     preferred_element_type=F32).astype(BF16)
        vc = jnp.dot(_gelu_tanh(a0_ref[0, 2 + g] + pltpu.roll(a1_ref[0, 2 + g], n_sub - 1, 0)).astype(BF16),
                     w2_ref[1], preferred_element_type=F32).astype(BF16)
        q = q_ref[0, g]
        s = lax.dot_general(q, kc, nt, preferred_element_type=F32)
        pos = past_len + lax.broadcasted_iota(jnp.int32, (rq, 1), 0) % t_len
        n_idx = lax.broadcasted_iota(jnp.int32, (rq, n_sub), 1)
        p_c = _masked_softmax_rows(s, (n_idx * CMP_STRIDE + (CMP_BLOCK - 1) <= pos) & (n_idx < n_cmp))
        ocmp_ref[0, g] = jnp.dot(p_c.astype(BF16), vc, preferred_element_type=F32)
        ri = lax.broadcasted_iota(jnp.int32, (SUBLANES, rq), 0)
        cj = lax.broadcasted_iota(jnp.int32, (SUBLANES, rq), 1)
        head_sum = jnp.where(cj % t_len == ri, 1.0, 0.0).astype(BF16)
        p_hi = p_c.astype(BF16)
        p_lo = (p_c - p_hi.astype(F32)).astype(BF16)
        p_sum = (jnp.dot(head_sum, p_hi, preferred_element_type=F32)
                 + jnp.dot(head_sum, p_lo, preferred_element_type=F32))
        ci = lax.broadcasted_iota(jnp.int32, (n_sub, lanes), 0)
        mi = lax.broadcasted_iota(jnp.int32, (n_sub, lanes), 1)
        overlap = jnp.where((ci * CMP_STRIDE < mi * SLC_BLOCK + SLC_BLOCK)
                            & (ci * CMP_STRIDE + CMP_BLOCK > mi * SLC_BLOCK) & (ci < n_cmp), 1.0, 0.0).astype(BF16)
        s_hi = p_sum.astype(BF16)
        s_lo = (p_sum - s_hi.astype(F32)).astype(BF16)
        imp = (jnp.dot(s_hi, overlap, preferred_element_type=F32)
               + jnp.dot(s_lo, overlap, preferred_element_type=F32))
        blk = lax.broadcasted_iota(jnp.int32, (SUBLANES, lanes), 1)
        blk_f = blk.astype(F32)
        tpos = past_len + lax.broadcasted_iota(jnp.int32, (SUBLANES, 1), 0) % t_len
        cur = tpos // SLC_BLOCK
        forced = (blk == 0) | (blk == cur) | (blk == cur - 1)
        work = jnp.where(blk * SLC_BLOCK <= tpos, jnp.where(forced, 1e6, imp), -1e6)
        work = jnp.where(blk < n_slc, work, NEG_INF)
        sel = jnp.zeros((SUBLANES, lanes), F32)
        for _ in range(min(SLC_TOPK, n_slc)):
            m = jnp.max(work, axis=-1, keepdims=True)
            first = jnp.min(jnp.where(work == m, blk_f, float(lanes)), axis=-1, keepdims=True)
            pick = blk_f == first
            sel = jnp.where(pick, 1.0, sel)
            work = jnp.where(pick, NEG_INF, work)
        sel_ref[0, g] = sel


def nsa_sample_select(a0, a1, cmp_w2, q_s, past_len, t_len):
    bs, _, n_sub, d = a0.shape
    n_slc = -(-(past_len + t_len) // SLC_BLOCK)
    lanes = _round_up(n_slc, LANES)
    rq = NSA_GROUP * t_len
    aspec = pl.BlockSpec((1, 4, n_sub, d), lambda s: (s, 0, 0, 0))
    return pl.pallas_call(
        functools.partial(_sample_select_kernel, past_len=past_len, t_len=t_len),
        grid=(bs,),
        in_specs=[aspec, aspec, pl.BlockSpec((2, d, d), lambda s: (0, 0, 0)),
                  pl.BlockSpec((1, NSA_KV_HEADS, rq, d), lambda s: (s, 0, 0, 0))],
        out_specs=[pl.BlockSpec((1, NSA_KV_HEADS, rq, d), lambda s: (s, 0, 0, 0)),
                   pl.BlockSpec((1, NSA_KV_HEADS, SUBLANES, lanes), lambda s: (s, 0, 0, 0))],
        out_shape=[jax.ShapeDtypeStruct((bs, NSA_KV_HEADS, rq, d), F32),
                   jax.ShapeDtypeStruct((bs, NSA_KV_HEADS, SUBLANES, lanes), F32)],
        compiler_params=pltpu.CompilerParams(dimension_semantics=("parallel",), vmem_limit_bytes=VMEM_LIMIT),
        name="nsa_sample_select",
    )(a0, a1, cmp_w2.astype(BF16), q_s)


def _sample_attend_kernel(pt_ref, *refs, past_len, t_len, row0):
    k_pages = refs[:SAMPLE_PAGES]
    v_pages = refs[SAMPLE_PAGES:2 * SAMPLE_PAGES]
    (q_ref, sel_ref, ocmp_ref, newrows_ref, kwin_ref, vwin_ref, newwin_ref, gate_ref, o_ref,
     m_scr, l_scr, acc_scr) = refs[2 * SAMPLE_PAGES:]
    s_idx = pl.program_id(0)
    c = pl.program_id(1)
    d, r = NSA_HEAD_DIM, NSA_GROUP
    rq = r * t_len
    lanes = sel_ref.shape[3]
    tk = SAMPLE_PAGES * PAGE_SIZE
    nt = (((1,), (1,)), ((), ()))

    @pl.when(c == 0)
    def _():
        m_scr[...] = jnp.full(m_scr.shape, -1e30, F32)
        l_scr[...] = jnp.zeros(l_scr.shape, F32)
        acc_scr[...] = jnp.zeros(acc_scr.shape, F32)

    row = lax.broadcasted_iota(jnp.int32, (rq, 1), 0)
    t_row = row % t_len
    pos = past_len + t_row
    ti = lax.broadcasted_iota(jnp.int32, (rq, SUBLANES), 1)
    tok_expand = jnp.where(ti == t_row, 1.0, 0.0).astype(BF16)

    def online(g, s, mask, v):
        s = jnp.where(mask, s, -1e30)
        m_old = m_scr[g]
        m_new = jnp.maximum(m_old, jnp.max(s, axis=-1, keepdims=True))
        alpha = jnp.exp(m_old - m_new)
        p = jnp.where(mask, jnp.exp(s - m_new), 0.0)
        l_scr[g] = alpha * l_scr[g] + jnp.sum(p, axis=-1, keepdims=True)
        acc_scr[g] = alpha * acc_scr[g] + jnp.dot(p.astype(BF16), v, preferred_element_type=F32)
        m_scr[g] = m_new

    sel16 = []
    for g in range(NSA_KV_HEADS):
        sel16.append(jnp.dot(tok_expand, sel_ref[0, g].astype(BF16), preferred_element_type=F32))
        k = jnp.concatenate([pg.reshape(NSA_KV_HEADS * PAGE_SIZE, d)[pl.ds(g, PAGE_SIZE, stride=NSA_KV_HEADS), :]
                             for pg in k_pages], axis=0)
        v = jnp.concatenate([pg.reshape(NSA_KV_HEADS * PAGE_SIZE, d)[pl.ds(g, PAGE_SIZE, stride=NSA_KV_HEADS), :]
                             for pg in v_pages], axis=0)
        q = q_ref[0, g]
        s = lax.dot_general(q, k.astype(BF16), nt, preferred_element_type=F32)
        ei = lax.broadcasted_iota(jnp.int32, (lanes, tk), 0)
        ej = lax.broadcasted_iota(jnp.int32, (lanes, tk), 1)
        expand = jnp.where(ei == c * (tk // SLC_BLOCK) + ej // SLC_BLOCK, 1.0, 0.0).astype(BF16)
        picked = jnp.dot(sel16[g].astype(BF16), expand, preferred_element_type=F32)
        kpos = c * tk + lax.broadcasted_iota(jnp.int32, (rq, tk), 1)
        online(g, s, (picked > 0.5) & (kpos <= pos), v.astype(BF16))

    @pl.when(c == pl.num_programs(1) - 1)
    def _():
        mine = (row0 // t_len + s_idx) % (NEW_ROWS_BLK // t_len)
        j = lax.broadcasted_iota(jnp.int32, (rq, NEW_ROWS_BLK), 1)
        own = j // t_len == mine
        new_pos = past_len + j % t_len
        sig = jax.nn.sigmoid(gate_ref[...])
        gsel = jnp.where(j == mine * t_len + t_row, 1.0, 0.0).astype(BF16)
        s_hi = sig.astype(BF16)
        s_lo = (sig - s_hi.astype(F32)).astype(BF16)
        sig_rows = (jnp.dot(gsel, s_hi, preferred_element_type=F32)
                    + jnp.dot(gsel, s_lo, preferred_element_type=F32))
        lane = lax.broadcasted_iota(jnp.int32, sig_rows.shape, 1)
        lane_l = lax.broadcasted_iota(jnp.int32, (rq, lanes), 1)
        for g in range(NSA_KV_HEADS):
            q = q_ref[0, g]
            kn = newrows_ref[:, (4 + g) * d:(5 + g) * d].astype(BF16)
            vn = newrows_ref[:, (6 + g) * d:(7 + g) * d].astype(BF16)
            s = lax.dot_general(q, kn, nt, preferred_element_type=F32)
            last_picked = jnp.sum(jnp.where(lane_l == past_len // SLC_BLOCK, sel16[g], 0.0), axis=-1, keepdims=True)
            online(g, s, own & (new_pos <= pos) & (last_picked > 0.5), vn)
            o_slc = acc_scr[g] / jnp.maximum(l_scr[g], 1e-30)
            w_len = kwin_ref.shape[1]
            win_rows = pl.ds(g, w_len, stride=NSA_KV_HEADS)
            kw = kwin_ref.reshape(NSA_KV_HEADS * w_len, d)[win_rows, :].astype(BF16)
            vw = vwin_ref.reshape(NSA_KV_HEADS * w_len, d)[win_rows, :].astype(BF16)
            s1 = lax.dot_general(q, kw, nt, preferred_element_type=F32)
            kp1 = past_len - w_len + lax.broadcasted_iota(jnp.int32, (rq, w_len), 1)
            d1 = pos - kp1
            m1 = (d1 >= 0) & (d1 < WINDOW) & (kp1 >= 0)
            knw = newwin_ref[:, g * d:(g + 1) * d].astype(BF16)
            vnw = newwin_ref[:, (2 + g) * d:(3 + g) * d].astype(BF16)
            s2 = lax.dot_general(q, knw, nt, preferred_element_type=F32)
            d2 = pos - new_pos
            m2 = own & (d2 >= 0) & (d2 < WINDOW)
            s1 = jnp.where(m1, s1, -1e30)
            s2 = jnp.where(m2, s2, -1e30)
            mx = jnp.maximum(jnp.max(s1, axis=-1, keepdims=True), jnp.max(s2, axis=-1, keepdims=True))
            p1 = jnp.where(m1, jnp.exp(s1 - mx), 0.0)
            p2 = jnp.where(m2, jnp.exp(s2 - mx), 0.0)
            den = jnp.sum(p1, axis=-1, keepdims=True) + jnp.sum(p2, axis=-1, keepdims=True)
            o_win = (jnp.dot(p1.astype(BF16), vw, preferred_element_type=F32)
                     + jnp.dot(p2.astype(BF16), vnw, preferred_element_type=F32)) / jnp.maximum(den, 1e-30)
            out = jnp.zeros((rq, d), F32)
            for branch, o_b in enumerate((ocmp_ref[0, g], o_slc, o_win)):
                col = branch * NSA_HEADS + g * r + row // t_len
                gate = jnp.sum(jnp.where(lane == col, sig_rows, 0.0), axis=-1, keepdims=True)
                out = out + gate * o_b
            o_ref[0, g] = out


def nsa_sample_attend(cache, page_table, q_s, sel, o_cmp, rows, win, win_cache, proj, past_len, t_len, row0):
    bs, n_pages = page_table.shape
    d = NSA_HEAD_DIM
    rq = NSA_GROUP * t_len
    lanes = sel.shape[3]
    w_len = win_cache.shape[1]
    assert row0 % t_len == 0 and NEW_ROWS_BLK % t_len == 0 and past_len % SLC_BLOCK == 0
    blk = lambda s: (row0 + s * t_len) // NEW_ROWS_BLK
    per_seq = lambda shape: pl.BlockSpec((1,) + shape, lambda s, c, pt: (s, 0, 0, 0))
    return pl.pallas_call(
        functools.partial(_sample_attend_kernel, past_len=past_len, t_len=t_len, row0=row0),
        grid_spec=pltpu.PrefetchScalarGridSpec(
            num_scalar_prefetch=1,
            grid=(bs, n_pages // SAMPLE_PAGES),
            in_specs=_page_specs(2) + _page_specs(3) + [
                per_seq((NSA_KV_HEADS, rq, d)), per_seq((NSA_KV_HEADS, SUBLANES, lanes)), per_seq((NSA_KV_HEADS, rq, d)),
                pl.BlockSpec((NEW_ROWS_BLK, 4 * NSA_KV_HEADS * d), lambda s, c, pt: (blk(s), 0)),
                pl.BlockSpec((1, w_len, 1, NSA_KV_HEADS, d), lambda s, c, pt: (s, 0, 0, 0, 0)),
                pl.BlockSpec((1, w_len, 1, NSA_KV_HEADS, d), lambda s, c, pt: (s, 0, 1, 0, 0)),
                pl.BlockSpec((NEW_ROWS_BLK, 4 * d), lambda s, c, pt: (blk(s), 0)),
                pl.BlockSpec((NEW_ROWS_BLK, LANES), lambda s, c, pt: (blk(s), GATE_COL0 // LANES))],
            out_specs=per_seq((NSA_KV_HEADS, rq, d)),
            scratch_shapes=[pltpu.VMEM((NSA_KV_HEADS, rq, 1), F32), pltpu.VMEM((NSA_KV_HEADS, rq, 1), F32),
                            pltpu.VMEM((NSA_KV_HEADS, rq, d), F32)]),
        out_shape=jax.ShapeDtypeStruct((bs, NSA_KV_HEADS, rq, d), F32),
        compiler_params=pltpu.CompilerParams(
            dimension_semantics=("parallel", "arbitrary"), vmem_limit_bytes=VMEM_LIMIT),
        name="nsa_sample_attend",
    )(page_table, *([cache] * (2 * SAMPLE_PAGES)), q_s, sel, o_cmp, rows, win_cache, win_cache, win, proj)


SC_ROWS = 512
SC_COLS = 512


def _short_conv_kernel(b_ref, c_ref, h_ref, w_ref, buf_ref, o_ref, tail_ref, prev_scr):
    @pl.when(pl.program_id(2) == 0)
    def _():
        prev_scr[...] = buf_ref[0]

    rows = c_ref.shape[0]
    u = c_ref[...] * h_ref[...]
    prev = prev_scr[...]
    row8 = lax.broadcasted_iota(jnp.int32, prev.shape, 0)
    conv = u * w_ref[SC_KSIZE - 1:SC_KSIZE, :]
    for k in range(1, SC_KSIZE):
        rolled = pltpu.roll(u, k, 0)
        top = jnp.where(row8 < k, pltpu.roll(prev, k, 0), rolled[0:SUBLANES])
        shifted = top if rows == SUBLANES else jnp.concatenate([top, rolled[SUBLANES:]], axis=0)
        conv = conv + shifted * w_ref[SC_KSIZE - 1 - k:SC_KSIZE - k, :]
    prev_scr[...] = u[rows - SUBLANES:rows]
    o_ref[...] = (b_ref[...] * conv).astype(o_ref.dtype)
    tail_ref[0] = u[rows - SUBLANES:rows]


def short_conv(proj, n_batch, t_len, rows_per_step, conv_buf, sc_w):
    nr = t_len // rows_per_step
    nh = SC_WIDTH // SC_COLS
    blk0 = SC_COL0 // SC_COLS
    assert SC_COL0 % SC_COLS == 0
    buf8 = jnp.pad(conv_buf, ((0, 0), (SUBLANES - (SC_KSIZE - 1), 0), (0, 0)))
    col = lambda part: pl.BlockSpec((rows_per_step, SC_COLS), lambda b, j, i: (b * nr + i, blk0 + part * nh + j))
    return pl.pallas_call(
        _short_conv_kernel,
        grid=(n_batch, nh, nr),
        in_specs=[col(0), col(1), col(2), pl.BlockSpec((SC_KSIZE, SC_COLS), lambda b, j, i: (0, j)),
                  pl.BlockSpec((1, SUBLANES, SC_COLS), lambda b, j, i: (b, 0, j))],
        out_specs=[pl.BlockSpec((rows_per_step, SC_COLS), lambda b, j, i: (b * nr + i, j)),
                   pl.BlockSpec((1, SUBLANES, SC_COLS), lambda b, j, i: (b, 0, j))],
        out_shape=[jax.ShapeDtypeStruct((n_batch * t_len, SC_WIDTH), BF16),
                   jax.ShapeDtypeStruct((n_batch, SUBLANES, SC_WIDTH), F32)],
        scratch_shapes=[pltpu.VMEM((SUBLANES, SC_COLS), F32)],
        compiler_params=pltpu.CompilerParams(
            dimension_semantics=("parallel", "parallel", "arbitrary"), vmem_limit_bytes=VMEM_LIMIT),
        name="short_conv",
    )(proj, proj, proj, sc_w, buf8)


def _ret_tables(pos, chunk, valid):
    half = RET_DQK // 2
    inv = RET_THETA ** (-jnp.arange(half, dtype=F32) / half)
    ang = pos.astype(F32)[:, None] * inv[None, :]
    log_g = jnp.log1p(-(2.0 ** (-5.0 - jnp.arange(RET_HEADS, dtype=F32))))
    i = jnp.arange(chunk, dtype=F32)
    diff = i[:, None] - i[None, :]
    intra = jnp.where(diff >= 0, jnp.exp(jnp.maximum(diff, 0.0)[None] * log_g[:, None, None]), 0.0)
    q_dec = jnp.exp((i[None, :] + 1.0) * log_g[:, None])[..., None]
    k_dec = jnp.exp((valid - 1.0 - i)[None, :] * log_g[:, None])[..., None]
    c_dec = jnp.exp(valid * log_g)[:, None, None]
    return jnp.cos(ang), jnp.sin(ang), intra, q_dec, k_dec, c_dec


def _retention_kernel(q_ref, k_ref, v_ref, g_ref, cos_ref, sin_ref, intra_ref, qd_ref, kd_ref, cd_ref, gn_ref,
                      s0_ref, o_ref, s_out_ref, s_scr):
    c_idx = pl.program_id(1)

    @pl.when(c_idx == 0)
    def _():
        s_scr[...] = s0_ref[0]

    cos, sin = cos_ref[...], sin_ref[...]
    half = RET_DQK // 2
    nt = (((1,), (1,)), ((), ()))
    tn = (((0,), (0,)), ((), ()))

    def rot(x):
        x1, x2 = x[:, :half], x[:, half:]
        return jnp.concatenate([x1 * cos - x2 * sin, x1 * sin + x2 * cos], axis=-1)

    for h in range(RET_HEADS):
        cols = slice(h * RET_DQK, (h + 1) * RET_DQK)
        qr = (rot(q_ref[:, cols]) * (RET_DQK ** -0.5)).astype(BF16)
        kr = rot(k_ref[:, cols])
        v = v_ref[:, cols].astype(BF16)
        att = lax.dot_general(qr, kr.astype(BF16), nt, preferred_element_type=F32) * intra_ref[h]
        s_old = s_scr[h]
        o = (jnp.dot(att.astype(BF16), v, preferred_element_type=F32)
             + jnp.dot(qr, s_old.astype(BF16), preferred_element_type=F32) * qd_ref[h])
        s_scr[h] = s_old * cd_ref[h] + lax.dot_general((kr * kd_ref[h]).astype(BF16), v, tn,
                                                       preferred_element_type=F32)
        mu = jnp.mean(o, axis=-1, keepdims=True)
        dev = o - mu
        var = jnp.mean(dev * dev, axis=-1, keepdims=True)
        gate = g_ref[:, cols]
        on = dev * lax.rsqrt(var + EPS) * gn_ref[:, cols] * (gate * jax.nn.sigmoid(gate))
        o_ref[:, cols] = on.astype(o_ref.dtype)
    s_out_ref[0] = s_scr[...]


def retention(proj, pos, n_batch, t_len, chunk, state0, gn_gain, valid=None):
    nc = t_len // chunk
    cos, sin, intra, q_dec, k_dec, c_dec = _ret_tables(pos, chunk, chunk if valid is None else valid)
    half = RET_DQK // 2
    col = lambda j: pl.BlockSpec((chunk, R_QK), lambda b, c: (b * nc + c, j))
    tab = pl.BlockSpec((chunk, half), lambda b, c: (c, 0))
    full = lambda a: pl.BlockSpec(a.shape, lambda b, c: (0,) * a.ndim)
    st = pl.BlockSpec((1, RET_HEADS, RET_DQK, RET_DV), lambda b, c: (b, 0, 0, 0))
    return pl.pallas_call(
        _retention_kernel,
        grid=(n_batch, nc),
        in_specs=[col(0), col(1), col(2), col(3), tab, tab, full(intra), full(q_dec), full(k_dec), full(c_dec),
                  pl.BlockSpec((1, R_V), lambda b, c: (0, 0)), st],
        out_specs=[pl.BlockSpec((chunk, R_V), lambda b, c: (b * nc + c, 0)), st],
        out_shape=[jax.ShapeDtypeStruct((n_batch * t_len, R_V), BF16),
                   jax.ShapeDtypeStruct((n_batch, RET_HEADS, RET_DQK, RET_DV), F32)],
        scratch_shapes=[pltpu.VMEM((RET_HEADS, RET_DQK, RET_DV), F32)],
        compiler_params=pltpu.CompilerParams(
            dimension_semantics=("parallel", "arbitrary"), vmem_limit_bytes=VMEM_LIMIT),
        name="retention",
    )(proj, proj, proj, proj, cos, sin, intra, q_dec, k_dec, c_dec, gn_gain.reshape(1, R_V), state0)


SSD_COL0 = 2 * R_QK + 2 * R_V
HEADS_PER_GROUP = M2_HEADS // M2_GROUPS


def _split3(x):
    a = x.astype(BF16)
    r = x - a.astype(F32)
    b = r.astype(BF16)
    c = (r - b.astype(F32)).astype(BF16)
    return a, b, c


def _exact_dot(mat_bf, x):
    out = None
    for piece in _split3(x):
        t = jnp.dot(mat_bf, piece, preferred_element_type=F32)
        out = t if out is None else out + t
    return out


def _exact_dot_r(x, mat_bf):
    out = None
    for piece in _split3(x):
        t = jnp.dot(piece, mat_bf, preferred_element_type=F32)
        out = t if out is None else out + t
    return out


def _ssd_kernel(z_ref, xa_ref, xb_ref, xc_ref, dt_ref, cw_ref, cb_ref, dtb_ref, aneg_ref, dskip_ref, norm_ref,
                buf0_ref, s0_ref, o_ref, s_out_ref, s_scr, prev_scr, *, valid):
    c_idx = pl.program_id(1)
    chunk = z_ref.shape[0]
    nt = (((1,), (1,)), ((), ()))
    tn = (((0,), (0,)), ((), ()))

    @pl.when(c_idx == 0)
    def _():
        s_scr[...] = s0_ref[0]
        prev_scr[...] = buf0_ref[0]

    x = jnp.concatenate([xa_ref[...], xb_ref[...], xc_ref[...]], axis=1)
    prev = prev_scr[...]
    row8 = lax.broadcasted_iota(jnp.int32, prev.shape, 0)
    conv = x * cw_ref[M2_CONV - 1:M2_CONV, :]
    for k in range(1, M2_CONV):
        rolled = pltpu.roll(x, k, 0)
        top = jnp.where(row8 < k, pltpu.roll(prev, k, 0), rolled[0:SUBLANES])
        shifted = top if chunk == SUBLANES else jnp.concatenate([top, rolled[SUBLANES:]], axis=0)
        conv = conv + shifted * cw_ref[M2_CONV - 1 - k:M2_CONV - k, :]
    prev_scr[...] = x[chunk - SUBLANES:chunk]
    conv = conv + cb_ref[...]
    xbc = conv * jax.nn.sigmoid(conv)
    xs = xbc[:, :M2_DINNER]

    dt_raw = dt_ref[...] + dtb_ref[...]
    dt = jnp.where(dt_raw > 20.0, dt_raw, jnp.log1p(jnp.exp(jnp.minimum(dt_raw, 20.0))))
    if valid < chunk:
        dt = jnp.where(lax.broadcasted_iota(jnp.int32, dt.shape, 0) < valid, dt, 0.0)
    a = dt * aneg_ref[...]
    ri = lax.broadcasted_iota(jnp.int32, (chunk, chunk), 0)
    ci = lax.broadcasted_iota(jnp.int32, (chunk, chunk), 1)
    tri = ri >= ci
    cum = _exact_dot(jnp.where(tri, 1.0, 0.0).astype(BF16), a)
    cum_t = cum.T
    cum_last = cum[chunk - 1:chunk, :]
    hi = lax.broadcasted_iota(jnp.int32, (LANES, M2_DINNER), 0)
    li = lax.broadcasted_iota(jnp.int32, (LANES, M2_DINNER), 1)
    expand = jnp.where(hi == li // M2_HEADDIM, 1.0, 0.0).astype(BF16)
    dt_x = _exact_dot_r(dt, expand)
    cum_x = _exact_dot_r(cum, expand)
    last_x = _exact_dot_r(cum_last, expand)
    xdt = xs * dt_x
    x_dec = (xdt * jnp.exp(last_x - cum_x)).astype(BF16)
    xdt_bf = xdt.astype(BF16)
    e_cum_x = jnp.exp(cum_x)
    e_last_x = jnp.exp(last_x)

    y_parts = []
    for gi in range(M2_GROUPS):
        b_g = xbc[:, M2_DINNER + gi * M2_STATE:M2_DINNER + (gi + 1) * M2_STATE].astype(BF16)
        c_g = xbc[:, M2_DINNER + (M2_GROUPS + gi) * M2_STATE:M2_DINNER + (M2_GROUPS + gi + 1) * M2_STATE].astype(BF16)
        cb = lax.dot_general(c_g, b_g, nt, preferred_element_type=F32)
        gcols = slice(gi * HEADS_PER_GROUP * M2_HEADDIM, (gi + 1) * HEADS_PER_GROUP * M2_HEADDIM)
        s_old = s_scr[:, gcols]
        y_state = jnp.dot(c_g, s_old.astype(BF16), preferred_element_type=F32) * e_cum_x[:, gcols]
        s_scr[:, gcols] = s_old * e_last_x[:, gcols] + lax.dot_general(b_g, x_dec[:, gcols], tn,
                                                                        preferred_element_type=F32)
        pair_lane = lax.broadcasted_iota(jnp.int32, (chunk, LANES), 1)
        intra = []
        for pr in range(HEADS_PER_GROUP // 2):
            outs = []
            for sub in range(2):
                h = gi * HEADS_PER_GROUP + pr * 2 + sub
                seg = cum[:, h:h + 1] - cum_t[h:h + 1, :]
                l_mat = jnp.where(tri, jnp.exp(jnp.where(tri, seg, 0.0)), 0.0)
                lanes = slice(gi * HEADS_PER_GROUP * M2_HEADDIM + pr * LANES,
                              gi * HEADS_PER_GROUP * M2_HEADDIM + (pr + 1) * LANES)
                outs.append(jnp.dot((cb * l_mat).astype(BF16), xdt_bf[:, lanes], preferred_element_type=F32))
            intra.append(jnp.where(pair_lane < M2_HEADDIM, outs[0], outs[1]))
        y_parts.append(jnp.concatenate(intra, axis=1) + y_state)
    y = jnp.concatenate(y_parts, axis=1) + dskip_ref[...] * xs
    z = z_ref[...]
    y = y * (z * jax.nn.sigmoid(z))
    gw = M2_DINNER // M2_GROUPS
    outs = []
    for gi in range(M2_GROUPS):
        yg = y[:, gi * gw:(gi + 1) * gw]
        outs.append(yg * lax.rsqrt(jnp.mean(yg * yg, axis=-1, keepdims=True) + EPS))
    o_ref[...] = (jnp.concatenate(outs, axis=1) * norm_ref[...]).astype(o_ref.dtype)
    s_out_ref[0] = s_scr[...]


def ssd(proj, n_batch, t_len, chunk, conv_buf, state0, conv_w, conv_b, dt_bias, a_log, d_skip, m2_norm, valid=None):
    nc = t_len // chunk
    z_blk = SSD_COL0 // M2_DINNER
    xw = M2_CONV_DIM // 3
    xbc_blk = (SSD_COL0 + M2_DINNER) // xw
    assert (SSD_COL0 + M2_DINNER) % xw == 0 and xw % LANES == 0
    dt_blk = (SSD_COL0 + M2_DINNER + M2_CONV_DIM) // LANES
    pad = lambda v: jnp.pad(v.reshape(1, -1), ((0, 0), (0, LANES - v.shape[-1])))
    buf8 = jnp.pad(conv_buf, ((0, 0), (SUBLANES - (M2_CONV - 1), 0), (0, 0)))
    st_t = state0.transpose(0, 3, 1, 2).reshape(n_batch, M2_STATE, M2_DINNER)
    row = lambda a: pl.BlockSpec(a.shape, lambda b, c: (0, 0))
    cw = conv_w
    cb = conv_b.reshape(1, -1)
    dtb, aneg = pad(dt_bias), pad(-jnp.exp(a_log))
    dsk = jnp.repeat(d_skip, M2_HEADDIM).reshape(1, -1)
    nrm = m2_norm.reshape(1, -1)
    st = pl.BlockSpec((1, M2_STATE, M2_DINNER), lambda b, c: (b, 0, 0))
    out, s_fin = pl.pallas_call(
        functools.partial(_ssd_kernel, valid=chunk if valid is None else valid),
        grid=(n_batch, nc),
        in_specs=[pl.BlockSpec((chunk, M2_DINNER), lambda b, c: (b * nc + c, z_blk)),
                  pl.BlockSpec((chunk, xw), lambda b, c: (b * nc + c, xbc_blk)),
                  pl.BlockSpec((chunk, xw), lambda b, c: (b * nc + c, xbc_blk + 1)),
                  pl.BlockSpec((chunk, xw), lambda b, c: (b * nc + c, xbc_blk + 2)),
                  pl.BlockSpec((chunk, LANES), lambda b, c: (b * nc + c, dt_blk)),
                  row(cw), row(cb), row(dtb), row(aneg), row(dsk), row(nrm),
                  pl.BlockSpec((1, SUBLANES, M2_CONV_DIM), lambda b, c: (b, 0, 0)), st],
        out_specs=[pl.BlockSpec((chunk, M2_DINNER), lambda b, c: (b * nc + c, 0)), st],
        out_shape=[jax.ShapeDtypeStruct((n_batch * t_len, M2_DINNER), BF16),
                   jax.ShapeDtypeStruct((n_batch, M2_STATE, M2_DINNER), F32)],
        scratch_shapes=[pltpu.VMEM((M2_STATE, M2_DINNER), F32), pltpu.VMEM((SUBLANES, M2_CONV_DIM), F32)],
        compiler_params=pltpu.CompilerParams(
            dimension_semantics=("parallel", "arbitrary"), vmem_limit_bytes=VMEM_LIMIT),
        name="ssd",
    )(proj, proj, proj, proj, proj, cw, cb, dtb, aneg, dsk, nrm, buf8, st_t)
    s_fin = s_fin.reshape(n_batch, M2_STATE, M2_HEADS, M2_HEADDIM).transpose(0, 2, 3, 1)
    return out, s_fin


SEQ_ROWS = SUBLANES


def _pad_seq_rows(a, n_seq, t_len):
    assert t_len <= SEQ_ROWS
    a = a.reshape(n_seq, t_len, a.shape[-1])
    return jnp.pad(a, ((0, 0), (0, SEQ_ROWS - t_len), (0, 0))).reshape(n_seq * SEQ_ROWS, a.shape[-1])


def _unpad_seq_rows(a, n_seq, t_len):
    return a.reshape(n_seq, SEQ_ROWS, a.shape[-1])[:, :t_len].reshape(n_seq * t_len, a.shape[-1])

def kernel(x_prompt, x_sample, cache_nsa_kv, cache_nsa_win, state_sc_conv, state_ret, state_ssm, state_m2_conv,
           page_table, norm_w, final_norm, e_w_in, e_w_out, e_cmp_pe, e_cmp_w1, e_cmp_w2, e_sc_conv, o_w_in,
           o_w_out, o_ret_gn, o_m2_conv_w, o_m2_conv_b, o_m2_dt_bias, o_m2_a_log, o_m2_d, o_m2_norm, peer_wq,
           peer_keys, peer_u, peer_v):
    bp, tp, dm = x_prompt.shape
    bs, ts, _ = x_sample.shape
    n_p, n_s = bp * tp, bs * ts
    g, r, d = NSA_KV_HEADS, NSA_GROUP, NSA_HEAD_DIM
    past_len = page_table.shape[1] * PAGE_SIZE
    pos_p = jnp.arange(tp, dtype=jnp.int32)
    pos_s = past_len + jnp.arange(ts, dtype=jnp.int32)
    n_real = n_p + n_s
    n_pad = _round_up(n_real, PEER_TM) - n_real
    s_rows = slice(n_p, n_real)
    pos_rows = jnp.concatenate([jnp.tile(pos_p, bp), jnp.tile(pos_s, bs), jnp.zeros((n_pad,), jnp.int32)])
    xp = x_prompt.reshape(n_p, dm)
    x = jnp.concatenate([xp, x_sample.reshape(n_s, dm), xp[:n_pad]], axis=0)
    pad_bf = jnp.zeros((n_pad, E_Q), BF16)

    w0 = e_w_in[0]
    w_in = jnp.concatenate([w0[:, :SC_COL0], w0[:, SC_COL0 + E_G:], w0[:, SC_COL0:SC_COL0 + E_G],
                            jnp.zeros((dm, _pad_cols(E_IN) - E_IN), F32)], axis=1).astype(BF16)
    proj = norm_matmul(x, norm_w[0, 0], w_in)
    q_bf, rows, win, rows_bf, win_bf = nsa_rope(proj, pos_rows)
    kvc = nsa_compress(rows, bp, tp, e_cmp_pe[0], e_cmp_w1[0], e_cmp_w2[0])
    o_nsa_p = nsa_prompt(q_bf, kvc, rows_bf, win_bf, proj, bp, tp)
    p_kv = rows[:n_p].reshape(bp, tp, 4, g, d)
    s_kv = rows[s_rows].reshape(bs, ts, 4, g, d)
    p_win = win[:n_p].reshape(bp, tp, 2, g, d)[:, tp - min(WINDOW, tp):]
    win_s = win[s_rows].reshape(bs, ts, 2, g, d)
    s_win = jnp.concatenate([cache_nsa_win[0], win_s], axis=1)[:, ts:]
    cache = cache_nsa_kv[0]
    q_s = q_bf[s_rows].reshape(bs, ts, g, r, d).transpose(0, 2, 3, 1, 4).reshape(bs, g, r * ts, d)
    a0, a1 = nsa_sample_compress(cache, page_table, e_cmp_pe[0], e_cmp_w1[0])
    o_cmp_s, sel_s = nsa_sample_select(a0, a1, e_cmp_w2[0], q_s, past_len, ts)
    o_nsa_s = nsa_sample_attend(cache, page_table, q_s, sel_s, o_cmp_s, rows, win,
                                cache_nsa_win[0], proj, past_len, ts, n_p)
    o_nsa_s = o_nsa_s.reshape(bs, g, r, ts, d).transpose(0, 3, 1, 2, 4).reshape(n_s, E_Q)
    o_nsa = jnp.concatenate([o_nsa_p, o_nsa_s.astype(BF16), pad_bf], axis=0)
    o_sc_p, tail_p = short_conv(proj, bp, tp, SC_ROWS, jnp.zeros((bp, SC_KSIZE - 1, SC_WIDTH), F32), e_sc_conv[0])
    p_sc = tail_p[:, SUBLANES - (SC_KSIZE - 1):]
    o_sc_s, tail_s = short_conv(_pad_seq_rows(proj[s_rows], bs, ts), bs, SEQ_ROWS, SEQ_ROWS, state_sc_conv[0],
                                e_sc_conv[0])
    s_sc = jnp.concatenate([state_sc_conv[0], tail_s[:, :ts]], axis=1)[:, ts:]
    o_sc = jnp.concatenate([_unpad_seq_rows(o_sc_s, bs, ts), pad_bf], axis=0)
    o_sc = jnp.concatenate([o_sc_p, o_sc], axis=0)
    x = matmul2_res(o_nsa, o_sc, e_w_out[0].astype(BF16), x)
    u_bf = peer_u.astype(BF16)
    vt_bf = peer_v.transpose(0, 2, 1).astype(BF16)
    peer_t = peer_layer(x, norm_w[0, 1], peer_wq[0], peer_keys[0], u_bf, vt_bf, 0)

    w_in = jnp.pad(o_w_in[0], ((0, 0), (0, _pad_cols(O_IN) - O_IN))).astype(BF16)
    x = x + peer_t.T
    proj = norm_matmul(x, norm_w[1, 0], w_in)
    odd_w = (o_m2_conv_w[0], o_m2_conv_b[0], o_m2_dt_bias[0], o_m2_a_log[0], o_m2_d[0], o_m2_norm[0])
    o_ret_p, p_ret = retention(proj, pos_p, bp, tp, CHUNK, jnp.zeros((bp, RET_HEADS, RET_DQK, RET_DV), F32),
                               o_ret_gn[0])
    o_ssd_p, p_ssm = ssd(proj, bp, tp, CHUNK, jnp.zeros((bp, M2_CONV - 1, M2_CONV_DIM), F32),
                         jnp.zeros((bp, M2_HEADS, M2_HEADDIM, M2_STATE), F32), *odd_w)
    xbc_cols = slice(O_SPLITS[4], O_SPLITS[5])
    p_m2c = jnp.stack([proj[(b + 1) * tp - (M2_CONV - 1):(b + 1) * tp, xbc_cols] for b in range(bp)])
    xbc_s = proj[s_rows, xbc_cols].reshape(bs, ts, M2_CONV_DIM)
    s_m2c = jnp.concatenate([state_m2_conv[0], xbc_s], axis=1)[:, ts:]
    proj_s = _pad_seq_rows(proj[s_rows], bs, ts)
    pos_s8 = past_len + jnp.arange(SEQ_ROWS, dtype=jnp.int32)
    o_ret_s, s_ret = retention(proj_s, pos_s8, bs, SEQ_ROWS, SEQ_ROWS, state_ret[0], o_ret_gn[0], valid=ts)
    o_ssd_s, s_ssm = ssd(proj_s, bs, SEQ_ROWS, SEQ_ROWS, state_m2_conv[0], state_ssm[0], *odd_w, valid=ts)
    o_ret = jnp.concatenate([o_ret_p, _unpad_seq_rows(o_ret_s, bs, ts), pad_bf], axis=0)
    o_ssd = jnp.concatenate([o_ssd_p, _unpad_seq_rows(o_ssd_s, bs, ts), pad_bf], axis=0)
    x = matmul2_res(o_ret, o_ssd, o_w_out[0].astype(BF16), x)
    peer_t = peer_layer(x, norm_w[1, 1], peer_wq[1], peer_keys[1], u_bf, vt_bf, 1)

    y = resid_rmsnorm(x, peer_t, final_norm)
    y_prompt = y[:n_p].reshape(bp, tp, dm)
    y_sample = y[s_rows].reshape(bs, ts, dm)
    return (y_prompt, y_sample, p_kv[None], p_win[None], p_sc[None], p_ret[None], p_ssm[None], p_m2c[None],
            s_kv[None], s_win[None], s_sc[None], s_ret[None], s_ssm[None], s_m2c[None])
```

```python
import functools
import math

import jax
import jax.numpy as jnp
from jax import lax
from jax.experimental import pallas as pl
from jax.experimental.pallas import tpu as pltpu

F32 = jnp.float32
BF16 = jnp.bfloat16

D_MODEL = 2048
DEPTH = 2
PAGE_SIZE = 128
NSA_HEAD_DIM = 128
NSA_HEADS = 8
NSA_KV_HEADS = 2
NSA_GROUP = 4
CMP_BLOCK = 32
CMP_STRIDE = 16
SLC_BLOCK = 64
SLC_TOPK = 16
WINDOW = 512
ROPE_THETA = 500000.0
ROPE_DIMS = 32
SC_WIDTH = 1024
SC_KSIZE = 3
RET_HEADS = 4
RET_DQK = 256
RET_DV = 256
RET_THETA = 10000.0
M2_DINNER = 1024
M2_HEADDIM = 64
M2_HEADS = 16
M2_STATE = 128
M2_GROUPS = 2
M2_CONV = 4
M2_CONV_DIM = M2_DINNER + 2 * M2_GROUPS * M2_STATE
PEER_HEADS = 8
PEER_KEYS = 128
PEER_QDIM = 256
PEER_TOPK = 16
Q_BLOCK = 128
CHUNK = 128
EPS = 1e-6

E_Q = NSA_HEADS * NSA_HEAD_DIM
E_KV = 6 * NSA_KV_HEADS * NSA_HEAD_DIM
E_G = 3 * NSA_HEADS
E_SC = 3 * SC_WIDTH
E_IN = E_Q + E_KV + E_G + E_SC
R_QK = RET_HEADS * RET_DQK
R_V = RET_HEADS * RET_DV
O_SPLITS = [R_QK, 2 * R_QK, 2 * R_QK + R_V, 2 * R_QK + 2 * R_V,
            2 * R_QK + 2 * R_V + M2_DINNER, 2 * R_QK + 2 * R_V + M2_DINNER + M2_CONV_DIM]
O_IN = O_SPLITS[-1] + M2_HEADS
SC_COL0 = E_Q + E_KV
GATE_COL0 = SC_COL0 + E_SC

LANES = 128
SUBLANES = 8
VMEM_LIMIT = 56 * 1024 * 1024
ROW_TILE_CAP = 1056
ROPE_ROWS_CAP = 544
RESID_ROWS = 768
NEG_INF = float("-inf")


def _round_up(n, m):
    return -(-n // m) * m


def _pick_tile(n, cap):
    best = LANES
    for t in range(LANES, cap + 1, LANES):
        if n % t == 0:
            best = t
    return best


def _pad_cols(m):
    return min((_round_up(m, t) for t in (768, 640, 512)))


def _row_tile(n, cap):
    best = None
    for t in range(16, cap + 1, 16):
        if n % t == 0:
            best = t
    assert best is not None
    return best


def _gelu_tanh(x):
    return 0.5 * x * (1.0 + jnp.tanh(math.sqrt(2.0 / math.pi) * (x + 0.044715 * (x * x * x))))


def _norm_matmul_kernel(x_ref, g_ref, w_ref, o_ref, xn_ref):
    @pl.when(pl.program_id(1) == 0)
    def _():
        x = x_ref[...]
        ms = jnp.mean(x * x, axis=-1, keepdims=True)
        xn_ref[...] = (x * lax.rsqrt(ms + EPS) * g_ref[...]).astype(BF16)

    o_ref[...] = jnp.dot(xn_ref[...], w_ref[...], preferred_element_type=F32)


def norm_matmul(x, gain, w_bf):
    n, k = x.shape
    m = w_bf.shape[1]
    tm = _row_tile(n, ROW_TILE_CAP)
    tn = _pick_tile(m, 768)
    return pl.pallas_call(
        _norm_matmul_kernel,
        grid=(n // tm, m // tn),
        in_specs=[pl.BlockSpec((tm, k), lambda i, j: (i, 0)),
                  pl.BlockSpec((1, k), lambda i, j: (0, 0)),
                  pl.BlockSpec((k, tn), lambda i, j: (0, j))],
        out_specs=pl.BlockSpec((tm, tn), lambda i, j: (i, j)),
        out_shape=jax.ShapeDtypeStruct((n, m), F32),
        scratch_shapes=[pltpu.VMEM((tm, k), BF16)],
        compiler_params=pltpu.CompilerParams(
            dimension_semantics=("parallel", "arbitrary"), vmem_limit_bytes=VMEM_LIMIT),
        name="norm_matmul",
    )(x, gain.reshape(1, k), w_bf)


def _matmul2_res_kernel(a1_ref, a2_ref, w1_ref, w2_ref, r_ref, o_ref):
    o_ref[...] = (r_ref[...] + jnp.dot(a1_ref[...], w1_ref[...], preferred_element_type=F32)
                  + jnp.dot(a2_ref[...], w2_ref[...], preferred_element_type=F32))


def matmul2_res(a1, a2, w_bf, res):
    n, k1 = a1.shape
    k2 = a2.shape[1]
    m = w_bf.shape[1]
    assert k1 == k2
    tm = _row_tile(n, ROW_TILE_CAP)
    tn = _pick_tile(m, 1024)
    return pl.pallas_call(
        _matmul2_res_kernel,
        grid=(n // tm, m // tn),
        in_specs=[pl.BlockSpec((tm, k1), lambda i, j: (i, 0)),
                  pl.BlockSpec((tm, k2), lambda i, j: (i, 0)),
                  pl.BlockSpec((k1, tn), lambda i, j: (0, j)),
                  pl.BlockSpec((k2, tn), lambda i, j: (1, j)),
                  pl.BlockSpec((tm, tn), lambda i, j: (i, j))],
        out_specs=pl.BlockSpec((tm, tn), lambda i, j: (i, j)),
        out_shape=jax.ShapeDtypeStruct((n, m), F32),
        compiler_params=pltpu.CompilerParams(
            dimension_semantics=("parallel", "parallel"), vmem_limit_bytes=VMEM_LIMIT),
        name="matmul2_res",
    )(a1, a2, w_bf, w_bf, res)


PEER_TOK = 128
PEER_HEAD_UNROLL = 8
PEER_TM = 768
PEER_TA = 8
PEER_TE = PEER_TA * PEER_KEYS


def _top_desc(work, count, with_rank=False, one_at_a_time=False):
    rows = []
    rank = jnp.full(work.shape, float(count), F32)
    row_id = lax.broadcasted_iota(jnp.int32, work.shape, 0).astype(F32)
    for r in range(count):
        m = jnp.max(work, axis=0, keepdims=True)
        rows.append(m)
        hit = work == m
        if one_at_a_time:
            hit = row_id == jnp.min(jnp.where(hit, row_id, float(work.shape[0])), axis=0, keepdims=True)
        if with_rank:
            rank = jnp.where(hit, float(r), rank)
        work = jnp.where(hit, NEG_INF, work)
    return (rows, rank) if with_rank else rows


def _peer_router_kernel(q_ref, k_ref, cnt_ref, gw_ref, r2_ref, w2_ref):
    def head(h, carry):
        q = q_ref[h]
        scores = []
        for side in range(2):
            qs = q[:, side * PEER_KEYS:(side + 1) * PEER_KEYS]
            qs = qs * lax.rsqrt(jnp.mean(qs * qs, axis=-1, keepdims=True) + EPS)
            scores.append(lax.dot_general(k_ref[h, side], qs, (((1,), (1,)), ((), ())),
                                          preferred_element_type=F32))
        s1, s2 = scores
        v1 = _top_desc(s1, PEER_TOPK + 1)
        v2, rank2 = _top_desc(s2, PEER_TOPK + 1, with_rank=True)
        v2_lo = jnp.concatenate(v2[:8], axis=0)
        v2_hi = jnp.concatenate(v2[8:16], axis=0)
        row = lax.broadcasted_iota(jnp.int32, v2_lo.shape, 0)
        blocks = [v1[0] + v2_lo, v1[0] + v2_hi, v1[1] + v2_lo]
        for a, lim in ((2, 5), (3, 4), (4, 3), (5, 2), (6, 2), (7, 2)):
            blocks.append(jnp.where(row < lim, v1[a] + v2_lo, NEG_INF))
        blocks.append(jnp.concatenate(v1[8:16], axis=0) + v2[0])
        extra = jnp.where(row == 0, v1[0] + v2[16], jnp.where(row == 1, v1[16] + v2[0], NEG_INF))
        blocks.append(extra)
        cand = jnp.concatenate(blocks, axis=0)
        tops = _top_desc(cand, PEER_TOPK + 1, one_at_a_time=True)
        z = jnp.zeros_like(tops[0])
        for r in range(PEER_TOPK):
            z = z + jnp.exp(tops[r] - tops[0])
        tau = 0.5 * (tops[PEER_TOPK - 1] + tops[PEER_TOPK])
        count = jnp.zeros_like(s1)
        for r in range(PEER_TOPK):
            count = count + jnp.where(s1 >= tau - v2[r], 1.0, 0.0)
        cnt_ref[h] = count
        gw_ref[h] = jnp.exp(s1 - v1[0]) / z
        r2_ref[h] = rank2.astype(BF16)
        w2_ref[h] = jnp.exp(s2 - v2[0]).astype(BF16)
        return carry

    lax.fori_loop(0, PEER_HEADS, head, 0, unroll=PEER_HEAD_UNROLL)


def peer_router(q_hm, keys):
    n = q_hm.shape[1]
    out = jax.ShapeDtypeStruct((PEER_HEADS, PEER_KEYS, n), F32)
    out_bf = jax.ShapeDtypeStruct((PEER_HEADS, PEER_KEYS, n), BF16)
    spec = pl.BlockSpec((PEER_HEADS, PEER_KEYS, PEER_TOK), lambda i: (0, 0, i))
    return pl.pallas_call(
        _peer_router_kernel,
        grid=(n // PEER_TOK,),
        in_specs=[pl.BlockSpec((PEER_HEADS, PEER_TOK, PEER_QDIM), lambda i: (0, i, 0)),
                  pl.BlockSpec((PEER_HEADS, 2, PEER_KEYS, PEER_QDIM // 2), lambda i: (0, 0, 0, 0))],
        out_specs=[spec, spec, spec, spec],
        out_shape=[out, out, out_bf, out_bf],
        compiler_params=pltpu.CompilerParams(dimension_semantics=("parallel",), vmem_limit_bytes=VMEM_LIMIT),
        name="peer_router",
    )(q_hm, keys)


def _peer_expert_kernel(xt_ref, u_ref, vt_ref, cnt_ref, gw_ref, r2_ref, w2_ref, o_ref, s_scr, hg_scr):
    e = pl.program_id(1)
    s_scr[...] = jnp.dot(u_ref[...], xt_ref[...], preferred_element_type=F32)
    zero = jnp.zeros((), BF16)
    for al in range(PEER_TA):
        rows = slice(al * PEER_KEYS, (al + 1) * PEER_KEYS)
        g = None
        for h in range(PEER_HEADS):
            cnt = cnt_ref[h, al:al + 1, :].astype(BF16)
            gate = gw_ref[h, al:al + 1, :].astype(BF16)
            t = jnp.where(r2_ref[h] < cnt, w2_ref[h], zero) * gate
            g = t if g is None else g + t
        hg_scr[rows, :] = _gelu_tanh(s_scr[rows, :]).astype(BF16) * g
    part = jnp.dot(vt_ref[...], hg_scr[...], preferred_element_type=F32)

    @pl.when(e == 0)
    def _():
        o_ref[...] = part

    @pl.when(e != 0)
    def _():
        o_ref[...] += part


def peer_experts(xt_bf, u_bf, vt_bf, layer, cnt, gw, r2, w2):
    d, n = xt_bf.shape
    n_exp = u_bf.shape[1]
    tm = PEER_TM
    sel_spec = pl.BlockSpec((PEER_HEADS, PEER_TA, tm), lambda i, e: (0, e, i))
    all_spec = pl.BlockSpec((PEER_HEADS, PEER_KEYS, tm), lambda i, e: (0, 0, i))
    return pl.pallas_call(
        _peer_expert_kernel,
        grid=(n // tm, n_exp // PEER_TE),
        in_specs=[pl.BlockSpec((d, tm), lambda i, e: (0, i)),
                  pl.BlockSpec((None, PEER_TE, d), lambda i, e: (layer, e, 0)),
                  pl.BlockSpec((None, d, PEER_TE), lambda i, e: (layer, 0, e)),
                  sel_spec, sel_spec, all_spec, all_spec],
        out_specs=pl.BlockSpec((d, tm), lambda i, e: (0, i)),
        out_shape=jax.ShapeDtypeStruct((d, n), F32),
        scratch_shapes=[pltpu.VMEM((PEER_TE, tm), F32), pltpu.VMEM((PEER_TE, tm), BF16)],
        compiler_params=pltpu.CompilerParams(
            dimension_semantics=("parallel", "arbitrary"), vmem_limit_bytes=VMEM_LIMIT),
        name="peer_experts",
    )(xt_bf, u_bf, vt_bf, cnt, gw, r2, w2)


def _norm_matmul_t_kernel(x_ref, g_ref, w_ref, o_ref, xt_ref, xn_ref):
    @pl.when(pl.program_id(1) == 0)
    def _():
        x = x_ref[...]
        ms = jnp.mean(x * x, axis=-1, keepdims=True)
        xn = x * lax.rsqrt(ms + EPS) * g_ref[...]
        xn_ref[...] = xn.astype(BF16)
        xt_ref[...] = xn.T.astype(BF16)

    o_ref[...] = jnp.dot(xn_ref[...], w_ref[...], preferred_element_type=F32).reshape(o_ref.shape)


def norm_matmul_t(x, gain, w_bf, cols):
    n, k = x.shape
    m = w_bf.shape[1]
    tm = PEER_TM
    return pl.pallas_call(
        _norm_matmul_t_kernel,
        grid=(n // tm, m // cols),
        in_specs=[pl.BlockSpec((tm, k), lambda i, j: (i, 0)),
                  pl.BlockSpec((1, k), lambda i, j: (0, 0)),
                  pl.BlockSpec((k, cols), lambda i, j: (0, j))],
        out_specs=[pl.BlockSpec((1, tm, cols), lambda i, j: (j, i, 0)),
                   pl.BlockSpec((k, tm), lambda i, j: (0, i))],
        out_shape=[jax.ShapeDtypeStruct((m // cols, n, cols), F32), jax.ShapeDtypeStruct((k, n), BF16)],
        scratch_shapes=[pltpu.VMEM((tm, k), BF16)],
        compiler_params=pltpu.CompilerParams(
            dimension_semantics=("parallel", "arbitrary"), vmem_limit_bytes=VMEM_LIMIT),
        name="norm_matmul_t",
    )(x, gain.reshape(1, k), w_bf)


def _resid_rmsnorm_kernel(x_ref, dt_ref, g_ref, o_ref):
    x = x_ref[...] + dt_ref[...].T
    ms = jnp.mean(x * x, axis=-1, keepdims=True)
    o_ref[...] = x * lax.rsqrt(ms + EPS) * g_ref[...]


def resid_rmsnorm(x, delta_t, gain):
    n, k = x.shape
    tm = _pick_tile(n, RESID_ROWS)
    return pl.pallas_call(
        _resid_rmsnorm_kernel,
        grid=(n // tm,),
        in_specs=[pl.BlockSpec((tm, k), lambda i: (i, 0)), pl.BlockSpec((k, tm), lambda i: (0, i)),
                  pl.BlockSpec((1, k), lambda i: (0, 0))],
        out_specs=pl.BlockSpec((tm, k), lambda i: (i, 0)),
        out_shape=jax.ShapeDtypeStruct((n, k), F32),
        compiler_params=pltpu.CompilerParams(dimension_semantics=("parallel",), vmem_limit_bytes=VMEM_LIMIT),
        name="resid_rmsnorm",
    )(x, delta_t, gain.reshape(1, k))


def peer_layer(x, gain, w_q, keys, u_bf, vt_bf, layer):
    q_hm, xt_bf = norm_matmul_t(x, gain, w_q.astype(BF16), PEER_QDIM)
    cnt, gw, r2, w2 = peer_router(q_hm, keys)
    return peer_experts(xt_bf, u_bf, vt_bf, layer, cnt, gw, r2, w2)


def _rope_tables(pos):
    half = ROPE_DIMS // 2
    inv = ROPE_THETA ** (-jnp.arange(half, dtype=F32) / half)
    ang = pos.astype(F32)[:, None] * inv[None, :]
    cos, sin = jnp.cos(ang), jnp.sin(ang)
    t = pos.shape[0]
    ones = jnp.ones((t, NSA_HEAD_DIM - ROPE_DIMS), F32)
    zeros = jnp.zeros((t, NSA_HEAD_DIM - ROPE_DIMS), F32)
    zh = jnp.zeros((t, half), F32)
    c = jnp.concatenate([cos, cos, ones], axis=1)
    s_lo = jnp.concatenate([-sin, zh, zeros], axis=1)
    s_hi = jnp.concatenate([zh, sin, zeros], axis=1)
    return c, s_lo, s_hi


def _rope_kernel(p_ref, c_ref, sl_ref, sh_ref, q_ref, rows_ref, win_ref, rows_bf_ref, win_bf_ref):
    c, sl, sh = c_ref[...], sl_ref[...], sh_ref[...]
    half = ROPE_DIMS // 2

    def rot(x):
        return x * c + pltpu.roll(x, LANES - half, 1) * sl + pltpu.roll(x, half, 1) * sh

    d = NSA_HEAD_DIM
    scale = d ** -0.5
    for hd in range(NSA_HEADS):
        q_ref[:, hd * d:(hd + 1) * d] = (rot(p_ref[:, hd * d:(hd + 1) * d]) * scale).astype(BF16)
    for blk in range(12):
        x = p_ref[:, E_Q + blk * d:E_Q + (blk + 1) * d]
        if (blk // 2) % 2 == 0:
            x = rot(x)
        if blk < 8:
            rows_ref[:, blk * d:(blk + 1) * d] = x
            rows_bf_ref[:, blk * d:(blk + 1) * d] = x.astype(BF16)
        else:
            win_ref[:, (blk - 8) * d:(blk - 7) * d] = x
            win_bf_ref[:, (blk - 8) * d:(blk - 7) * d] = x.astype(BF16)


def nsa_rope(proj, pos_rows):
    n = proj.shape[0]
    tm = _row_tile(n, ROPE_ROWS_CAP)
    c, sl, sh = _rope_tables(pos_rows)
    width = E_Q + E_KV
    tab = pl.BlockSpec((tm, LANES), lambda i: (i, 0))
    return pl.pallas_call(
        _rope_kernel,
        grid=(n // tm,),
        in_specs=[pl.BlockSpec((tm, width), lambda i: (i, 0)), tab, tab, tab],
        out_specs=[pl.BlockSpec((tm, E_Q), lambda i: (i, 0)),
                   pl.BlockSpec((tm, 1024), lambda i: (i, 0)),
                   pl.BlockSpec((tm, 512), lambda i: (i, 0)),
                   pl.BlockSpec((tm, 1024), lambda i: (i, 0)),
                   pl.BlockSpec((tm, 512), lambda i: (i, 0))],
        out_shape=[jax.ShapeDtypeStruct((n, E_Q), BF16),
                   jax.ShapeDtypeStruct((n, 1024), F32),
                   jax.ShapeDtypeStruct((n, 512), F32),
                   jax.ShapeDtypeStruct((n, 1024), BF16),
                   jax.ShapeDtypeStruct((n, 512), BF16)],
        compiler_params=pltpu.CompilerParams(dimension_semantics=("parallel",), vmem_limit_bytes=VMEM_LIMIT),
        name="nsa_rope",
    )(proj, c, sl, sh)


def _compress_kernel(x_ref, pe_ref, w1_ref, w2_ref, o_ref):
    n_sub = x_ref.shape[0] // CMP_STRIDE
    acc0 = jnp.zeros((n_sub, NSA_HEAD_DIM), F32)
    acc1 = jnp.zeros((n_sub, NSA_HEAD_DIM), F32)
    for s in range(CMP_STRIDE):
        xs = x_ref[pl.ds(s, n_sub, stride=CMP_STRIDE), :]
        a0 = (xs + pe_ref[0, s:s + 1, :]).astype(BF16)
        a1 = (xs + pe_ref[0, CMP_STRIDE + s:CMP_STRIDE + s + 1, :]).astype(BF16)
        acc0 = acc0 + jnp.dot(a0, w1_ref[0, s], preferred_element_type=F32)
        acc1 = acc1 + jnp.dot(a1, w1_ref[0, CMP_STRIDE + s], preferred_element_type=F32)
    pre = acc0 + pltpu.roll(acc1, n_sub - 1, 0)
    o_ref[0, 0] = jnp.dot(_gelu_tanh(pre).astype(BF16), w2_ref[0], preferred_element_type=F32).astype(BF16)


def nsa_compress(rows, n_batch, t_len, cmp_pe, cmp_w1, cmp_w2):
    n_sub = t_len // CMP_STRIDE
    d = NSA_HEAD_DIM
    w1 = cmp_w1.reshape(2, CMP_BLOCK, d, d).astype(BF16)
    return pl.pallas_call(
        _compress_kernel,
        grid=(n_batch, 4),
        in_specs=[pl.BlockSpec((t_len, d), lambda b, c: (b, c)),
                  pl.BlockSpec((1, CMP_BLOCK, d), lambda b, c: (c // 2, 0, 0)),
                  pl.BlockSpec((1, CMP_BLOCK, d, d), lambda b, c: (c // 2, 0, 0, 0)),
                  pl.BlockSpec((1, d, d), lambda b, c: (c // 2, 0, 0))],
        out_specs=pl.BlockSpec((1, 1, n_sub, d), lambda b, c: (b, c, 0, 0)),
        out_shape=jax.ShapeDtypeStruct((n_batch, 4, n_sub, d), BF16),
        compiler_params=pltpu.CompilerParams(
            dimension_semantics=("parallel", "parallel"), vmem_limit_bytes=VMEM_LIMIT),
        name="nsa_compress",
    )(rows, cmp_pe, w1, cmp_w2.astype(BF16))


NSA_TK = 512
NSA_WTILES = WINDOW // Q_BLOCK + 1


def _masked_softmax_rows(s, mask):
    s = jnp.where(mask, s, -1e30)
    p = jnp.exp(s - jnp.max(s, axis=-1, keepdims=True))
    p = jnp.where(mask, p, 0.0)
    return p / jnp.maximum(jnp.sum(p, axis=-1, keepdims=True), 1e-30)


def _nsa_prompt_kernel(q_ref, kc_ref, vc_ref, ks_ref, vs_ref, kw_ref, vw_ref, gate_ref, o_ref):
    g = pl.program_id(1)
    qi = pl.program_id(2)
    d, r, qb = NSA_HEAD_DIM, NSA_GROUP, Q_BLOCK
    n_cmp = kc_ref.shape[2]
    n_slc = ks_ref.shape[0] // SLC_BLOCK
    nt = (((1,), (1,)), ((), ()))
    q = jnp.concatenate([q_ref[:, h * d:(h + 1) * d] for h in range(r)], axis=0)
    t_col = qi * qb + lax.broadcasted_iota(jnp.int32, (qb, 1), 0)

    s = lax.dot_general(q, kc_ref[0, 0], nt, preferred_element_type=F32).reshape(r, qb, n_cmp)
    n_idx = lax.broadcasted_iota(jnp.int32, (qb, n_cmp), 1)
    c_mask = (n_idx * CMP_STRIDE + (CMP_BLOCK - 1) <= t_col)[None]
    p_c = _masked_softmax_rows(s, c_mask)
    o_cmp = jnp.dot(p_c.reshape(r * qb, n_cmp).astype(BF16), vc_ref[0, 0], preferred_element_type=F32)

    nb = LANES
    assert n_slc <= nb and qb == LANES
    p_sum = jnp.sum(p_c, axis=0)
    si = lax.broadcasted_iota(jnp.int32, (nb, n_cmp), 0) * SLC_BLOCK
    ci = lax.broadcasted_iota(jnp.int32, (nb, n_cmp), 1) * CMP_STRIDE
    overlap_t = jnp.where((ci < si + SLC_BLOCK) & (ci + CMP_BLOCK > si), 1.0, 0.0).astype(BF16)
    p_hi = p_sum.astype(BF16)
    p_lo = (p_sum - p_hi.astype(F32)).astype(BF16)
    imp_t = (lax.dot_general(overlap_t, p_hi, nt, preferred_element_type=F32)
             + lax.dot_general(overlap_t, p_lo, nt, preferred_element_type=F32))
    blk = lax.broadcasted_iota(jnp.int32, (nb, qb), 0)
    blk_f = blk.astype(F32)
    t_row = qi * qb + lax.broadcasted_iota(jnp.int32, (1, qb), 1)
    cur = t_row // SLC_BLOCK
    forced = (blk == 0) | (blk == cur) | (blk == cur - 1)
    work = jnp.where(blk * SLC_BLOCK <= t_row, jnp.where(forced, 1e6, imp_t), -1e6)
    work = jnp.where(blk < n_slc, work, NEG_INF)
    sel_t = jnp.zeros((nb, qb), F32)
    for _ in range(min(SLC_TOPK, n_slc)):
        m = jnp.max(work, axis=0, keepdims=True)
        first = jnp.min(jnp.where(work == m, blk_f, float(nb)), axis=0, keepdims=True)
        pick = blk_f == first
        sel_t = jnp.where(pick, 1.0, sel_t)
        work = jnp.where(pick, NEG_INF, work)
    sel_bf = sel_t.T.astype(BF16)

    tk = NSA_TK
    bpt = tk // SLC_BLOCK

    def slc_step(kt, carry):
        m_run, l_run, acc = carry
        start = pl.multiple_of(kt * tk, tk)
        k = ks_ref[pl.ds(start, tk), :]
        v = vs_ref[pl.ds(start, tk), :]
        s = lax.dot_general(q, k, nt, preferred_element_type=F32).reshape(r, qb, tk)
        ei = lax.broadcasted_iota(jnp.int32, (nb, tk), 0)
        ej = lax.broadcasted_iota(jnp.int32, (nb, tk), 1)
        expand = jnp.where(ei == kt * bpt + ej // SLC_BLOCK, 1.0, 0.0).astype(BF16)
        picked = jnp.dot(sel_bf, expand, preferred_element_type=F32)
        kpos = start + lax.broadcasted_iota(jnp.int32, (qb, tk), 1)
        mask = ((picked > 0.5) & (kpos <= t_col))[None]
        s = jnp.where(mask, s, -1e30)
        m_new = jnp.maximum(m_run, jnp.max(s, axis=-1, keepdims=True))
        alpha = jnp.exp(m_run - m_new)
        p = jnp.where(mask, jnp.exp(s - m_new), 0.0)
        l_new = alpha * l_run + jnp.sum(p, axis=-1, keepdims=True)
        pv = jnp.dot(p.reshape(r * qb, tk).astype(BF16), v, preferred_element_type=F32)
        acc = alpha.reshape(r * qb, 1) * acc + pv
        return m_new, l_new, acc

    init = (jnp.full((r, qb, 1), -1e30, F32), jnp.zeros((r, qb, 1), F32), jnp.zeros((r * qb, d), F32))
    _, l_fin, acc = lax.fori_loop(0, (qi * qb) // tk + 1, slc_step, init)
    o_slc = acc / jnp.maximum(l_fin.reshape(r * qb, 1), 1e-30)

    k_tiles, v_tiles, pos_tiles = [], [], []
    for j in range(NSA_WTILES):
        kt = qi - (NSA_WTILES - 1) + j
        ktc = jnp.maximum(kt, 0)
        start = pl.multiple_of(ktc * qb, qb)
        k_tiles.append(kw_ref[pl.ds(start, qb), :])
        v_tiles.append(vw_ref[pl.ds(start, qb), :])
        lane = lax.broadcasted_iota(jnp.int32, (qb, qb), 1)
        pos_tiles.append(jnp.where(kt >= 0, start + lane, -1))
    k_w = jnp.concatenate(k_tiles, axis=0)
    v_w = jnp.concatenate(v_tiles, axis=0)
    k_pos = jnp.concatenate(pos_tiles, axis=1)
    span = NSA_WTILES * qb
    s = lax.dot_general(q, k_w, nt, preferred_element_type=F32).reshape(r, qb, span)
    dpos = t_col - k_pos
    w_mask = ((dpos >= 0) & (dpos < WINDOW) & (k_pos >= 0))[None]
    p_w = _masked_softmax_rows(s, w_mask)
    o_win = jnp.dot(p_w.reshape(r * qb, span).astype(BF16), v_w, preferred_element_type=F32)

    sig = jax.nn.sigmoid(gate_ref[...])
    lane = lax.broadcasted_iota(jnp.int32, sig.shape, 1)
    for h in range(r):
        rows = slice(h * qb, (h + 1) * qb)
        out = jnp.zeros((qb, d), F32)
        for branch, o_b in enumerate((o_cmp, o_slc, o_win)):
            col = branch * NSA_HEADS + g * r + h
            gate = jnp.sum(jnp.where(lane == col, sig, 0.0), axis=-1, keepdims=True)
            out = out + gate * o_b[rows]
        o_ref[:, h * d:(h + 1) * d] = out.astype(o_ref.dtype)


def nsa_prompt(q_bf, kvc, rows_bf, win_bf, proj, n_batch, t_len):
    d, r, qb = NSA_HEAD_DIM, NSA_GROUP, Q_BLOCK
    nqb = t_len // qb
    n_sub = kvc.shape[2]
    gate_blk = GATE_COL0 // LANES
    seq = lambda col: pl.BlockSpec((t_len, d), lambda b, g, i: (b, col(g)))
    return pl.pallas_call(
        _nsa_prompt_kernel,
        grid=(n_batch, NSA_KV_HEADS, nqb),
        in_specs=[pl.BlockSpec((qb, r * d), lambda b, g, i: (b * nqb + i, g)),
                  pl.BlockSpec((1, 1, n_sub, d), lambda b, g, i: (b, g, 0, 0)),
                  pl.BlockSpec((1, 1, n_sub, d), lambda b, g, i: (b, 2 + g, 0, 0)),
                  seq(lambda g: 4 + g), seq(lambda g: 6 + g),
                  seq(lambda g: g), seq(lambda g: 2 + g),
                  pl.BlockSpec((qb, LANES), lambda b, g, i: (b * nqb + i, gate_blk))],
        out_specs=pl.BlockSpec((qb, r * d), lambda b, g, i: (b * nqb + i, g)),
        out_shape=jax.ShapeDtypeStruct((n_batch * t_len, E_Q), BF16),
        compiler_params=pltpu.CompilerParams(
            dimension_semantics=("parallel", "parallel", "arbitrary"), vmem_limit_bytes=VMEM_LIMIT),
        name="nsa_prompt",
    )(q_bf, kvc, kvc, rows_bf, rows_bf, win_bf, win_bf, proj)


SAMPLE_PAGES = 32
NEW_ROWS_BLK = 16


def _page_specs(kind):
    def spec(i):
        return pl.BlockSpec((1, PAGE_SIZE, 1, NSA_KV_HEADS, NSA_HEAD_DIM),
                            lambda s, c, pt: (pt[s, c * SAMPLE_PAGES + i], 0, kind, 0, 0))
    return [spec(i) for i in range(SAMPLE_PAGES)]


def _sample_compress_kernel(pt_ref, *refs):
    n_in = 2 * SAMPLE_PAGES
    pages = refs[:n_in]
    pe_ref, w1_ref, a0_ref, a1_ref = refs[n_in:]
    d = NSA_HEAD_DIM
    per_page = PAGE_SIZE // CMP_STRIDE
    rows = SAMPLE_PAGES * per_page
    gs = NSA_KV_HEADS * CMP_STRIDE
    for kv in range(2):
        by_offset = [jnp.swapaxes(pages[kv * SAMPLE_PAGES + p].reshape(NSA_KV_HEADS * PAGE_SIZE, d)[...]
                                  .reshape(per_page, gs, d), 0, 1) for p in range(SAMPLE_PAGES)]
        acc0 = jnp.zeros((NSA_KV_HEADS * rows, d), F32)
        acc1 = jnp.zeros((NSA_KV_HEADS * rows, d), F32)
        for s in range(CMP_STRIDE):
            xs = jnp.concatenate([by_offset[p][NSA_KV_HEADS * s + g]
                                  for g in range(NSA_KV_HEADS) for p in range(SAMPLE_PAGES)],
                                 axis=0)
            a0 = (xs + pe_ref[kv, s:s + 1, :]).astype(BF16)
            a1 = (xs + pe_ref[kv, CMP_STRIDE + s:CMP_STRIDE + s + 1, :]).astype(BF16)
            acc0 = acc0 + jnp.dot(a0, w1_ref[kv, s], preferred_element_type=F32)
            acc1 = acc1 + jnp.dot(a1, w1_ref[kv, CMP_STRIDE + s], preferred_element_type=F32)
        for g in range(NSA_KV_HEADS):
            a0_ref[0, kv * NSA_KV_HEADS + g] = acc0[g * rows:(g + 1) * rows]
            a1_ref[0, kv * NSA_KV_HEADS + g] = acc1[g * rows:(g + 1) * rows]


def nsa_sample_compress(cache, page_table, cmp_pe, cmp_w1):
    bs, n_pages = page_table.shape
    d = NSA_HEAD_DIM
    per_page = PAGE_SIZE // CMP_STRIDE
    n_sub = n_pages * per_page
    rows = SAMPLE_PAGES * per_page
    w1 = cmp_w1.reshape(2, CMP_BLOCK, d, d).astype(BF16)
    out = jax.ShapeDtypeStruct((bs, 4, n_sub, d), F32)
    ospec = pl.BlockSpec((1, 4, rows, d), lambda s, c, pt: (s, 0, c, 0))
    page_specs = _page_specs(0) + _page_specs(1)
    return pl.pallas_call(
        _sample_compress_kernel,
        grid_spec=pltpu.PrefetchScalarGridSpec(
            num_scalar_prefetch=1,
            grid=(bs, n_pages // SAMPLE_PAGES),
            in_specs=page_specs + [
                pl.BlockSpec((2, CMP_BLOCK, d), lambda s, c, pt: (0, 0, 0)),
                pl.BlockSpec((2, CMP_BLOCK, d, d), lambda s, c, pt: (0, 0, 0, 0))],
            out_specs=[ospec, ospec]),
        out_shape=[out, out],
        compiler_params=pltpu.CompilerParams(
            dimension_semantics=("parallel", "arbitrary"), vmem_limit_bytes=VMEM_LIMIT),
        name="nsa_sample_compress",
    )(page_table, *([cache] * (2 * SAMPLE_PAGES)), cmp_pe, w1)


def _sample_select_kernel(a0_ref, a1_ref, w2_ref, q_ref, ocmp_ref, sel_ref, *, past_len, t_len):
    d, r = NSA_HEAD_DIM, NSA_GROUP
    n_sub = a0_ref.shape[2]
    n_cmp = n_sub - 1
    n_slc = -(-(past_len + t_len) // SLC_BLOCK)
    lanes = sel_ref.shape[3]
    nt = (((1,), (1,)), ((), ()))
    rq = r * t_len
    for g in range(NSA_KV_HEADS):
        kc = jnp.dot(_gelu_tanh(a0_ref[0, g] + pltpu.roll(a1_ref[0, g], n_sub - 1, 0)).astype(BF16), w2_ref[0],
                     preferred_element_type=F32).astype(BF16)
        vc = jnp.dot(_gelu_tanh(a0_ref[0, 2 + g] + pltpu.roll(a1_ref[0, 2 + g], n_sub - 1, 0)).astype(BF16),
                     w2_ref[1], preferred_element_type=F32).astype(BF16)
        q = q_ref[0, g]
        s = lax.dot_general(q, kc, nt, preferred_element_type=F32)
        pos = past_len + lax.broadcasted_iota(jnp.int32, (rq, 1), 0) % t_len
        n_idx = lax.broadcasted_iota(jnp.int32, (rq, n_sub), 1)
        p_c = _masked_softmax_rows(s, (n_idx * CMP_STRIDE + (CMP_BLOCK - 1) <= pos) & (n_idx < n_cmp))
        ocmp_ref[0, g] = jnp.dot(p_c.astype(BF16), vc, preferred_element_type=F32)
        ri = lax.broadcasted_iota(jnp.int32, (SUBLANES, rq), 0)
        cj = lax.broadcasted_iota(jnp.int32, (SUBLANES, rq), 1)
        head_sum = jnp.where(cj % t_len == ri, 1.0, 0.0).astype(BF16)
        p_hi = p_c.astype(BF16)
        p_lo = (p_c - p_hi.astype(F32)).astype(BF16)
        p_sum = (jnp.dot(head_sum, p_hi, preferred_element_type=F32)
                 + jnp.dot(head_sum, p_lo, preferred_element_type=F32))
        ci = lax.broadcasted_iota(jnp.int32, (n_sub, lanes), 0)
        mi = lax.broadcasted_iota(jnp.int32, (n_sub, lanes), 1)
        overlap = jnp.where((ci * CMP_STRIDE < mi * SLC_BLOCK + SLC_BLOCK)
                            & (ci * CMP_STRIDE + CMP_BLOCK > mi * SLC_BLOCK) & (ci < n_cmp), 1.0, 0.0).astype(BF16)
        s_hi = p_sum.astype(BF16)
        s_lo = (p_sum - s_hi.astype(F32)).astype(BF16)
        imp = (jnp.dot(s_hi, overlap, preferred_element_type=F32)
               + jnp.dot(s_lo, overlap, preferred_element_type=F32))
        blk = lax.broadcasted_iota(jnp.int32, (SUBLANES, lanes), 1)
        blk_f = blk.astype(F32)
        tpos = past_len + lax.broadcasted_iota(jnp.int32, (SUBLANES, 1), 0) % t_len
        cur = tpos // SLC_BLOCK
        forced = (blk == 0) | (blk == cur) | (blk == cur - 1)
        work = jnp.where(blk * SLC_BLOCK <= tpos, jnp.where(forced, 1e6, imp), -1e6)
        work = jnp.where(blk < n_slc, work, NEG_INF)
        sel = jnp.zeros((SUBLANES, lanes), F32)
        for _ in range(min(SLC_TOPK, n_slc)):
            m = jnp.max(work, axis=-1, keepdims=True)
            first = jnp.min(jnp.where(work == m, blk_f, float(lanes)), axis=-1, keepdims=True)
            pick = blk_f == first
            sel = jnp.where(pick, 1.0, sel)
            work = jnp.where(pick, NEG_INF, work)
        sel_ref[0, g] = sel


def nsa_sample_select(a0, a1, cmp_w2, q_s, past_len, t_len):
    bs, _, n_sub, d = a0.shape
    n_slc = -(-(past_len + t_len) // SLC_BLOCK)
    lanes = _round_up(n_slc, LANES)
    rq = NSA_GROUP * t_len
    aspec = pl.BlockSpec((1, 4, n_sub, d), lambda s: (s, 0, 0, 0))
    return pl.pallas_call(
        functools.partial(_sample_select_kernel, past_len=past_len, t_len=t_len),
        grid=(bs,),
        in_specs=[aspec, aspec, pl.BlockSpec((2, d, d), lambda s: (0, 0, 0)),
                  pl.BlockSpec((1, NSA_KV_HEADS, rq, d), lambda s: (s, 0, 0, 0))],
        out_specs=[pl.BlockSpec((1, NSA_KV_HEADS, rq, d), lambda s: (s, 0, 0, 0)),
                   pl.BlockSpec((1, NSA_KV_HEADS, SUBLANES, lanes), lambda s: (s, 0, 0, 0))],
        out_shape=[jax.ShapeDtypeStruct((bs, NSA_KV_HEADS, rq, d), F32),
                   jax.ShapeDtypeStruct((bs, NSA_KV_HEADS, SUBLANES, lanes), F32)],
        compiler_params=pltpu.CompilerParams(dimension_semantics=("parallel",), vmem_limit_bytes=VMEM_LIMIT),
        name="nsa_sample_select",
    )(a0, a1, cmp_w2.astype(BF16), q_s)


def _sample_attend_kernel(pt_ref, *refs, past_len, t_len, row0):
    k_pages = refs[:SAMPLE_PAGES]
    v_pages = refs[SAMPLE_PAGES:2 * SAMPLE_PAGES]
    (q_ref, sel_ref, ocmp_ref, newrows_ref, kwin_ref, vwin_ref, newwin_ref, gate_ref, o_ref,
     m_scr, l_scr, acc_scr) = refs[2 * SAMPLE_PAGES:]
    s_idx = pl.program_id(0)
    c = pl.program_id(1)
    d, r = NSA_HEAD_DIM, NSA_GROUP
    rq = r * t_len
    lanes = sel_ref.shape[3]
    tk = SAMPLE_PAGES * PAGE_SIZE
    nt = (((1,), (1,)), ((), ()))

    @pl.when(c == 0)
    def _():
        m_scr[...] = jnp.full(m_scr.shape, -1e30, F32)
        l_scr[...] = jnp.zeros(l_scr.shape, F32)
        acc_scr[...] = jnp.zeros(acc_scr.shape, F32)

    row = lax.broadcasted_iota(jnp.int32, (rq, 1), 0)
    t_row = row % t_len
    pos = past_len + t_row
    ti = lax.broadcasted_iota(jnp.int32, (rq, SUBLANES), 1)
    tok_expand = jnp.where(ti == t_row, 1.0, 0.0).astype(BF16)

    def online(g, s, mask, v):
        s = jnp.where(mask, s, -1e30)
        m_old = m_scr[g]
        m_new = jnp.maximum(m_old, jnp.max(s, axis=-1, keepdims=True))
        alpha = jnp.exp(m_old - m_new)
        p = jnp.where(mask, jnp.exp(s - m_new), 0.0)
        l_scr[g] = alpha * l_scr[g] + jnp.sum(p, axis=-1, keepdims=True)
        acc_scr[g] = alpha * acc_scr[g] + jnp.dot(p.astype(BF16), v, preferred_element_type=F32)
        m_scr[g] = m_new

    sel16 = []
    for g in range(NSA_KV_HEADS):
        sel16.append(jnp.dot(tok_expand, sel_ref[0, g].astype(BF16), preferred_element_type=F32))
        k = jnp.concatenate([pg.reshape(NSA_KV_HEADS * PAGE_SIZE, d)[pl.ds(g, PAGE_SIZE, stride=NSA_KV_HEADS), :]
                             for pg in k_pages], axis=0)
        v = jnp.concatenate([pg.reshape(NSA_KV_HEADS * PAGE_SIZE, d)[pl.ds(g, PAGE_SIZE, stride=NSA_KV_HEADS), :]
                             for pg in v_pages], axis=0)
        q = q_ref[0, g]
        s = lax.dot_general(q, k.astype(BF16), nt, preferred_element_type=F32)
        ei = lax.broadcasted_iota(jnp.int32, (lanes, tk), 0)
        ej = lax.broadcasted_iota(jnp.int32, (lanes, tk), 1)
        expand = jnp.where(ei == c * (tk // SLC_BLOCK) + ej // SLC_BLOCK, 1.0, 0.0).astype(BF16)
        picked = jnp.dot(sel16[g].astype(BF16), expand, preferred_element_type=F32)
        kpos = c * tk + lax.broadcasted_iota(jnp.int32, (rq, tk), 1)
        online(g, s, (picked > 0.5) & (kpos <= pos), v.astype(BF16))

    @pl.when(c == pl.num_programs(1) - 1)
    def _():
        mine = (row0 // t_len + s_idx) % (NEW_ROWS_BLK // t_len)
        j = lax.broadcasted_iota(jnp.int32, (rq, NEW_ROWS_BLK), 1)
        own = j // t_len == mine
        new_pos = past_len + j % t_len
        sig = jax.nn.sigmoid(gate_ref[...])
        gsel = jnp.where(j == mine * t_len + t_row, 1.0, 0.0).astype(BF16)
        s_hi = sig.astype(BF16)
        s_lo = (sig - s_hi.astype(F32)).astype(BF16)
        sig_rows = (jnp.dot(gsel, s_hi, preferred_element_type=F32)
                    + jnp.dot(gsel, s_lo, preferred_element_type=F32))
        lane = lax.broadcasted_iota(jnp.int32, sig_rows.shape, 1)
        lane_l = lax.broadcasted_iota(jnp.int32, (rq, lanes), 1)
        for g in range(NSA_KV_HEADS):
            q = q_ref[0, g]
            kn = newrows_ref[:, (4 + g) * d:(5 + g) * d].astype(BF16)
            vn = newrows_ref[:, (6 + g) * d:(7 + g) * d].astype(BF16)
            s = lax.dot_general(q, kn, nt, preferred_element_type=F32)
            last_picked = jnp.sum(jnp.where(lane_l == past_len // SLC_BLOCK, sel16[g], 0.0), axis=-1, keepdims=True)
            online(g, s, own & (new_pos <= pos) & (last_picked > 0.5), vn)
            o_slc = acc_scr[g] / jnp.maximum(l_scr[g], 1e-30)
            w_len = kwin_ref.shape[1]
            win_rows = pl.ds(g, w_len, stride=NSA_KV_HEADS)
            kw = kwin_ref.reshape(NSA_KV_HEADS * w_len, d)[win_rows, :].astype(BF16)
            vw = vwin_ref.reshape(NSA_KV_HEADS * w_len, d)[win_rows, :].astype(BF16)
            s1 = lax.dot_general(q, kw, nt, preferred_element_type=F32)
            kp1 = past_len - w_len + lax.broadcasted_iota(jnp.int32, (rq, w_len), 1)
            d1 = pos - kp1
            m1 = (d1 >= 0) & (d1 < WINDOW) & (kp1 >= 0)
            knw = newwin_ref[:, g * d:(g + 1) * d].astype(BF16)
            vnw = newwin_ref[:, (2 + g) * d:(3 + g) * d].astype(BF16)
            s2 = lax.dot_general(q, knw, nt, preferred_element_type=F32)
            d2 = pos - new_pos
            m2 = own & (d2 >= 0) & (d2 < WINDOW)
            s1 = jnp.where(m1, s1, -1e30)
            s2 = jnp.where(m2, s2, -1e30)
            mx = jnp.maximum(jnp.max(s1, axis=-1, keepdims=True), jnp.max(s2, axis=-1, keepdims=True))
            p1 = jnp.where(m1, jnp.exp(s1 - mx), 0.0)
            p2 = jnp.where(m2, jnp.exp(s2 - mx), 0.0)
            den = jnp.sum(p1, axis=-1, keepdims=True) + jnp.sum(p2, axis=-1, keepdims=True)
            o_win = (jnp.dot(p1.astype(BF16), vw, preferred_element_type=F32)
                     + jnp.dot(p2.astype(BF16), vnw, preferred_element_type=F32)) / jnp.maximum(den, 1e-30)
            out = jnp.zeros((rq, d), F32)
            for branch, o_b in enumerate((ocmp_ref[0, g], o_slc, o_win)):
                col = branch * NSA_HEADS + g * r + row // t_len
                gate = jnp.sum(jnp.where(lane == col, sig_rows, 0.0), axis=-1, keepdims=True)
                out = out + gate * o_b
            o_ref[0, g] = out


def nsa_sample_attend(cache, page_table, q_s, sel, o_cmp, rows, win, win_cache, proj, past_len, t_len, row0):
    bs, n_pages = page_table.shape
    d = NSA_HEAD_DIM
    rq = NSA_GROUP * t_len
    lanes = sel.shape[3]
    w_len = win_cache.shape[1]
    assert row0 % t_len == 0 and NEW_ROWS_BLK % t_len == 0 and past_len % SLC_BLOCK == 0
    blk = lambda s: (row0 + s * t_len) // NEW_ROWS_BLK
    per_seq = lambda shape: pl.BlockSpec((1,) + shape, lambda s, c, pt: (s, 0, 0, 0))
    return pl.pallas_call(
        functools.partial(_sample_attend_kernel, past_len=past_len, t_len=t_len, row0=row0),
        grid_spec=pltpu.PrefetchScalarGridSpec(
            num_scalar_prefetch=1,
            grid=(bs, n_pages // SAMPLE_PAGES),
            in_specs=_page_specs(2) + _page_specs(3) + [
                per_seq((NSA_KV_HEADS, rq, d)), per_seq((NSA_KV_HEADS, SUBLANES, lanes)), per_seq((NSA_KV_HEADS, rq, d)),
                pl.BlockSpec((NEW_ROWS_BLK, 4 * NSA_KV_HEADS * d), lambda s, c, pt: (blk(s), 0)),
                pl.BlockSpec((1, w_len, 1, NSA_KV_HEADS, d), lambda s, c, pt: (s, 0, 0, 0, 0)),
                pl.BlockSpec((1, w_len, 1, NSA_KV_HEADS, d), lambda s, c, pt: (s, 0, 1, 0, 0)),
                pl.BlockSpec((NEW_ROWS_BLK, 4 * d), lambda s, c, pt: (blk(s), 0)),
                pl.BlockSpec((NEW_ROWS_BLK, LANES), lambda s, c, pt: (blk(s), GATE_COL0 // LANES))],
            out_specs=per_seq((NSA_KV_HEADS, rq, d)),
            scratch_shapes=[pltpu.VMEM((NSA_KV_HEADS, rq, 1), F32), pltpu.VMEM((NSA_KV_HEADS, rq, 1), F32),
                            pltpu.VMEM((NSA_KV_HEADS, rq, d), F32)]),
        out_shape=jax.ShapeDtypeStruct((bs, NSA_KV_HEADS, rq, d), F32),
        compiler_params=pltpu.CompilerParams(
            dimension_semantics=("parallel", "arbitrary"), vmem_limit_bytes=VMEM_LIMIT),
        name="nsa_sample_attend",
    )(page_table, *([cache] * (2 * SAMPLE_PAGES)), q_s, sel, o_cmp, rows, win_cache, win_cache, win, proj)


SC_ROWS = 512
SC_COLS = 512


def _short_conv_kernel(b_ref, c_ref, h_ref, w_ref, buf_ref, o_ref, tail_ref, prev_scr):
    @pl.when(pl.program_id(2) == 0)
    def _():
        prev_scr[...] = buf_ref[0]

    rows = c_ref.shape[0]
    u = c_ref[...] * h_ref[...]
    prev = prev_scr[...]
    row8 = lax.broadcasted_iota(jnp.int32, prev.shape, 0)
    conv = u * w_ref[SC_KSIZE - 1:SC_KSIZE, :]
    for k in range(1, SC_KSIZE):
        rolled = pltpu.roll(u, k, 0)
        top = jnp.where(row8 < k, pltpu.roll(prev, k, 0), rolled[0:SUBLANES])
        shifted = top if rows == SUBLANES else jnp.concatenate([top, rolled[SUBLANES:]], axis=0)
        conv = conv + shifted * w_ref[SC_KSIZE - 1 - k:SC_KSIZE - k, :]
    prev_scr[...] = u[rows - SUBLANES:rows]
    o_ref[...] = (b_ref[...] * conv).astype(o_ref.dtype)
    tail_ref[0] = u[rows - SUBLANES:rows]


def short_conv(proj, n_batch, t_len, rows_per_step, conv_buf, sc_w):
    nr = t_len // rows_per_step
    nh = SC_WIDTH // SC_COLS
    blk0 = SC_COL0 // SC_COLS
    assert SC_COL0 % SC_COLS == 0
    buf8 = jnp.pad(conv_buf, ((0, 0), (SUBLANES - (SC_KSIZE - 1), 0), (0, 0)))
    col = lambda part: pl.BlockSpec((rows_per_step, SC_COLS), lambda b, j, i: (b * nr + i, blk0 + part * nh + j))
    return pl.pallas_call(
        _short_conv_kernel,
        grid=(n_batch, nh, nr),
        in_specs=[col(0), col(1), col(2), pl.BlockSpec((SC_KSIZE, SC_COLS), lambda b, j, i: (0, j)),
                  pl.BlockSpec((1, SUBLANES, SC_COLS), lambda b, j, i: (b, 0, j))],
        out_specs=[pl.BlockSpec((rows_per_step, SC_COLS), lambda b, j, i: (b * nr + i, j)),
                   pl.BlockSpec((1, SUBLANES, SC_COLS), lambda b, j, i: (b, 0, j))],
        out_shape=[jax.ShapeDtypeStruct((n_batch * t_len, SC_WIDTH), BF16),
                   jax.ShapeDtypeStruct((n_batch, SUBLANES, SC_WIDTH), F32)],
        scratch_shapes=[pltpu.VMEM((SUBLANES, SC_COLS), F32)],
        compiler_params=pltpu.CompilerParams(
            dimension_semantics=("parallel", "parallel", "arbitrary"), vmem_limit_bytes=VMEM_LIMIT),
        name="short_conv",
    )(proj, proj, proj, sc_w, buf8)


def _ret_tables(pos, chunk, valid):
    half = RET_DQK // 2
    inv = RET_THETA ** (-jnp.arange(half, dtype=F32) / half)
    ang = pos.astype(F32)[:, None] * inv[None, :]
    log_g = jnp.log1p(-(2.0 ** (-5.0 - jnp.arange(RET_HEADS, dtype=F32))))
    i = jnp.arange(chunk, dtype=F32)
    diff = i[:, None] - i[None, :]
    intra = jnp.where(diff >= 0, jnp.exp(jnp.maximum(diff, 0.0)[None] * log_g[:, None, None]), 0.0)
    q_dec = jnp.exp((i[None, :] + 1.0) * log_g[:, None])[..., None]
    k_dec = jnp.exp((valid - 1.0 - i)[None, :] * log_g[:, None])[..., None]
    c_dec = jnp.exp(valid * log_g)[:, None, None]
    return jnp.cos(ang), jnp.sin(ang), intra, q_dec, k_dec, c_dec


def _retention_kernel(q_ref, k_ref, v_ref, g_ref, cos_ref, sin_ref, intra_ref, qd_ref, kd_ref, cd_ref, gn_ref,
                      s0_ref, o_ref, s_out_ref, s_scr):
    c_idx = pl.program_id(1)

    @pl.when(c_idx == 0)
    def _():
        s_scr[...] = s0_ref[0]

    cos, sin = cos_ref[...], sin_ref[...]
    half = RET_DQK // 2
    nt = (((1,), (1,)), ((), ()))
    tn = (((0,), (0,)), ((), ()))

    def rot(x):
        x1, x2 = x[:, :half], x[:, half:]
        return jnp.concatenate([x1 * cos - x2 * sin, x1 * sin + x2 * cos], axis=-1)

    for h in range(RET_HEADS):
        cols = slice(h * RET_DQK, (h + 1) * RET_DQK)
        qr = (rot(q_ref[:, cols]) * (RET_DQK ** -0.5)).astype(BF16)
        kr = rot(k_ref[:, cols])
        v = v_ref[:, cols].astype(BF16)
        att = lax.dot_general(qr, kr.astype(BF16), nt, preferred_element_type=F32) * intra_ref[h]
        s_old = s_scr[h]
        o = (jnp.dot(att.astype(BF16), v, preferred_element_type=F32)
             + jnp.dot(qr, s_old.astype(BF16), preferred_element_type=F32) * qd_ref[h])
        s_scr[h] = s_old * cd_ref[h] + lax.dot_general((kr * kd_ref[h]).astype(BF16), v, tn,
                                                       preferred_element_type=F32)
        mu = jnp.mean(o, axis=-1, keepdims=True)
        dev = o - mu
        var = jnp.mean(dev * dev, axis=-1, keepdims=True)
        gate = g_ref[:, cols]
        on = dev * lax.rsqrt(var + EPS) * gn_ref[:, cols] * (gate * jax.nn.sigmoid(gate))
        o_ref[:, cols] = on.astype(o_ref.dtype)
    s_out_ref[0] = s_scr[...]


def retention(proj, pos, n_batch, t_len, chunk, state0, gn_gain, valid=None):
    nc = t_len // chunk
    cos, sin, intra, q_dec, k_dec, c_dec = _ret_tables(pos, chunk, chunk if valid is None else valid)
    half = RET_DQK // 2
    col = lambda j: pl.BlockSpec((chunk, R_QK), lambda b, c: (b * nc + c, j))
    tab = pl.BlockSpec((chunk, half), lambda b, c: (c, 0))
    full = lambda a: pl.BlockSpec(a.shape, lambda b, c: (0,) * a.ndim)
    st = pl.BlockSpec((1, RET_HEADS, RET_DQK, RET_DV), lambda b, c: (b, 0, 0, 0))
    return pl.pallas_call(
        _retention_kernel,
        grid=(n_batch, nc),
        in_specs=[col(0), col(1), col(2), col(3), tab, tab, full(intra), full(q_dec), full(k_dec), full(c_dec),
                  pl.BlockSpec((1, R_V), lambda b, c: (0, 0)), st],
        out_specs=[pl.BlockSpec((chunk, R_V), lambda b, c: (b * nc + c, 0)), st],
        out_shape=[jax.ShapeDtypeStruct((n_batch * t_len, R_V), BF16),
                   jax.ShapeDtypeStruct((n_batch, RET_HEADS, RET_DQK, RET_DV), F32)],
        scratch_shapes=[pltpu.VMEM((RET_HEADS, RET_DQK, RET_DV), F32)],
        compiler_params=pltpu.CompilerParams(
            dimension_semantics=("parallel", "arbitrary"), vmem_limit_bytes=VMEM_LIMIT),
        name="retention",
    )(proj, proj, proj, proj, cos, sin, intra, q_dec, k_dec, c_dec, gn_gain.reshape(1, R_V), state0)


SSD_COL0 = 2 * R_QK + 2 * R_V
HEADS_PER_GROUP = M2_HEADS // M2_GROUPS


def _split3(x):
    a = x.astype(BF16)
    r = x - a.astype(F32)
    b = r.astype(BF16)
    c = (r - b.astype(F32)).astype(BF16)
    return a, b, c


def _exact_dot(mat_bf, x):
    out = None
    for piece in _split3(x):
        t = jnp.dot(mat_bf, piece, preferred_element_type=F32)
        out = t if out is None else out + t
    return out


def _exact_dot_r(x, mat_bf):
    out = None
    for piece in _split3(x):
        t = jnp.dot(piece, mat_bf, preferred_element_type=F32)
        out = t if out is None else out + t
    return out


def _ssd_kernel(z_ref, xa_ref, xb_ref, xc_ref, dt_ref, cw_ref, cb_ref, dtb_ref, aneg_ref, dskip_ref, norm_ref,
                buf0_ref, s0_ref, o_ref, s_out_ref, s_scr, prev_scr, *, valid):
    c_idx = pl.program_id(1)
    chunk = z_ref.shape[0]
    nt = (((1,), (1,)), ((), ()))
    tn = (((0,), (0,)), ((), ()))

    @pl.when(c_idx == 0)
    def _():
        s_scr[...] = s0_ref[0]
        prev_scr[...] = buf0_ref[0]

    x = jnp.concatenate([xa_ref[...], xb_ref[...], xc_ref[...]], axis=1)
    prev = prev_scr[...]
    row8 = lax.broadcasted_iota(jnp.int32, prev.shape, 0)
    conv = x * cw_ref[M2_CONV - 1:M2_CONV, :]
    for k in range(1, M2_CONV):
        rolled = pltpu.roll(x, k, 0)
        top = jnp.where(row8 < k, pltpu.roll(prev, k, 0), rolled[0:SUBLANES])
        shifted = top if chunk == SUBLANES else jnp.concatenate([top, rolled[SUBLANES:]], axis=0)
        conv = conv + shifted * cw_ref[M2_CONV - 1 - k:M2_CONV - k, :]
    prev_scr[...] = x[chunk - SUBLANES:chunk]
    conv = conv + cb_ref[...]
    xbc = conv * jax.nn.sigmoid(conv)
    xs = xbc[:, :M2_DINNER]

    dt_raw = dt_ref[...] + dtb_ref[...]
    dt = jnp.where(dt_raw > 20.0, dt_raw, jnp.log1p(jnp.exp(jnp.minimum(dt_raw, 20.0))))
    if valid < chunk:
        dt = jnp.where(lax.broadcasted_iota(jnp.int32, dt.shape, 0) < valid, dt, 0.0)
    a = dt * aneg_ref[...]
    ri = lax.broadcasted_iota(jnp.int32, (chunk, chunk), 0)
    ci = lax.broadcasted_iota(jnp.int32, (chunk, chunk), 1)
    tri = ri >= ci
    cum = _exact_dot(jnp.where(tri, 1.0, 0.0).astype(BF16), a)
    cum_t = cum.T
    cum_last = cum[chunk - 1:chunk, :]
    hi = lax.broadcasted_iota(jnp.int32, (LANES, M2_DINNER), 0)
    li = lax.broadcasted_iota(jnp.int32, (LANES, M2_DINNER), 1)
    expand = jnp.where(hi == li // M2_HEADDIM, 1.0, 0.0).astype(BF16)
    dt_x = _exact_dot_r(dt, expand)
    cum_x = _exact_dot_r(cum, expand)
    last_x = _exact_dot_r(cum_last, expand)
    xdt = xs * dt_x
    x_dec = (xdt * jnp.exp(last_x - cum_x)).astype(BF16)
    xdt_bf = xdt.astype(BF16)
    e_cum_x = jnp.exp(cum_x)
    e_last_x = jnp.exp(last_x)

    y_parts = []
    for gi in range(M2_GROUPS):
        b_g = xbc[:, M2_DINNER + gi * M2_STATE:M2_DINNER + (gi + 1) * M2_STATE].astype(BF16)
        c_g = xbc[:, M2_DINNER + (M2_GROUPS + gi) * M2_STATE:M2_DINNER + (M2_GROUPS + gi + 1) * M2_STATE].astype(BF16)
        cb = lax.dot_general(c_g, b_g, nt, preferred_element_type=F32)
        gcols = slice(gi * HEADS_PER_GROUP * M2_HEADDIM, (gi + 1) * HEADS_PER_GROUP * M2_HEADDIM)
        s_old = s_scr[:, gcols]
        y_state = jnp.dot(c_g, s_old.astype(BF16), preferred_element_type=F32) * e_cum_x[:, gcols]
        s_scr[:, gcols] = s_old * e_last_x[:, gcols] + lax.dot_general(b_g, x_dec[:, gcols], tn,
                                                                        preferred_element_type=F32)
        pair_lane = lax.broadcasted_iota(jnp.int32, (chunk, LANES), 1)
        intra = []
        for pr in range(HEADS_PER_GROUP // 2):
            outs = []
            for sub in range(2):
                h = gi * HEADS_PER_GROUP + pr * 2 + sub
                seg = cum[:, h:h + 1] - cum_t[h:h + 1, :]
                l_mat = jnp.where(tri, jnp.exp(jnp.where(tri, seg, 0.0)), 0.0)
                lanes = slice(gi * HEADS_PER_GROUP * M2_HEADDIM + pr * LANES,
                              gi * HEADS_PER_GROUP * M2_HEADDIM + (pr + 1) * LANES)
                outs.append(jnp.dot((cb * l_mat).astype(BF16), xdt_bf[:, lanes], preferred_element_type=F32))
            intra.append(jnp.where(pair_lane < M2_HEADDIM, outs[0], outs[1]))
        y_parts.append(jnp.concatenate(intra, axis=1) + y_state)
    y = jnp.concatenate(y_parts, axis=1) + dskip_ref[...] * xs
    z = z_ref[...]
    y = y * (z * jax.nn.sigmoid(z))
    gw = M2_DINNER // M2_GROUPS
    outs = []
    for gi in range(M2_GROUPS):
        yg = y[:, gi * gw:(gi + 1) * gw]
        outs.append(yg * lax.rsqrt(jnp.mean(yg * yg, axis=-1, keepdims=True) + EPS))
    o_ref[...] = (jnp.concatenate(outs, axis=1) * norm_ref[...]).astype(o_ref.dtype)
    s_out_ref[0] = s_scr[...]


def ssd(proj, n_batch, t_len, chunk, conv_buf, state0, conv_w, conv_b, dt_bias, a_log, d_skip, m2_norm, valid=None):
    nc = t_len // chunk
    z_blk = SSD_COL0 // M2_DINNER
    xw = M2_CONV_DIM // 3
    xbc_blk = (SSD_COL0 + M2_DINNER) // xw
    assert (SSD_COL0 + M2_DINNER) % xw == 0 and xw % LANES == 0
    dt_blk = (SSD_COL0 + M2_DINNER + M2_CONV_DIM) // LANES
    pad = lambda v: jnp.pad(v.reshape(1, -1), ((0, 0), (0, LANES - v.shape[-1])))
    buf8 = jnp.pad(conv_buf, ((0, 0), (SUBLANES - (M2_CONV - 1), 0), (0, 0)))
    st_t = state0.transpose(0, 3, 1, 2).reshape(n_batch, M2_STATE, M2_DINNER)
    row = lambda a: pl.BlockSpec(a.shape, lambda b, c: (0, 0))
    cw = conv_w
    cb = conv_b.reshape(1, -1)
    dtb, aneg = pad(dt_bias), pad(-jnp.exp(a_log))
    dsk = jnp.repeat(d_skip, M2_HEADDIM).reshape(1, -1)
    nrm = m2_norm.reshape(1, -1)
    st = pl.BlockSpec((1, M2_STATE, M2_DINNER), lambda b, c: (b, 0, 0))
    out, s_fin = pl.pallas_call(
        functools.partial(_ssd_kernel, valid=chunk if valid is None else valid),
        grid=(n_batch, nc),
        in_specs=[pl.BlockSpec((chunk, M2_DINNER), lambda b, c: (b * nc + c, z_blk)),
                  pl.BlockSpec((chunk, xw), lambda b, c: (b * nc + c, xbc_blk)),
                  pl.BlockSpec((chunk, xw), lambda b, c: (b * nc + c, xbc_blk + 1)),
                  pl.BlockSpec((chunk, xw), lambda b, c: (b * nc + c, xbc_blk + 2)),
                  pl.BlockSpec((chunk, LANES), lambda b, c: (b * nc + c, dt_blk)),
                  row(cw), row(cb), row(dtb), row(aneg), row(dsk), row(nrm),
                  pl.BlockSpec((1, SUBLANES, M2_CONV_DIM), lambda b, c: (b, 0, 0)), st],
        out_specs=[pl.BlockSpec((chunk, M2_DINNER), lambda b, c: (b * nc + c, 0)), st],
        out_shape=[jax.ShapeDtypeStruct((n_batch * t_len, M2_DINNER), BF16),
                   jax.ShapeDtypeStruct((n_batch, M2_STATE, M2_DINNER), F32)],
        scratch_shapes=[pltpu.VMEM((M2_STATE, M2_DINNER), F32), pltpu.VMEM((SUBLANES, M2_CONV_DIM), F32)],
        compiler_params=pltpu.CompilerParams(
            dimension_semantics=("parallel", "arbitrary"), vmem_limit_bytes=VMEM_LIMIT),
        name="ssd",
    )(proj, proj, proj, proj, proj, cw, cb, dtb, aneg, dsk, nrm, buf8, st_t)
    s_fin = s_fin.reshape(n_batch, M2_STATE, M2_HEADS, M2_HEADDIM).transpose(0, 2, 3, 1)
    return out, s_fin


SEQ_ROWS = SUBLANES


def _pad_seq_rows(a, n_seq, t_len):
    assert t_len <= SEQ_ROWS
    a = a.reshape(n_seq, t_len, a.shape[-1])
    return jnp.pad(a, ((0, 0), (0, SEQ_ROWS - t_len), (0, 0))).reshape(n_seq * SEQ_ROWS, a.shape[-1])


def _unpad_seq_rows(a, n_seq, t_len):
    return a.reshape(n_seq, SEQ_ROWS, a.shape[-1])[:, :t_len].reshape(n_seq * t_len, a.shape[-1])

def kernel(x_prompt, x_sample, cache_nsa_kv, cache_nsa_win, state_sc_conv, state_ret, state_ssm, state_m2_conv,
           page_table, norm_w, final_norm, e_w_in, e_w_out, e_cmp_pe, e_cmp_w1, e_cmp_w2, e_sc_conv, o_w_in,
           o_w_out, o_ret_gn, o_m2_conv_w, o_m2_conv_b, o_m2_dt_bias, o_m2_a_log, o_m2_d, o_m2_norm, peer_wq,
           peer_keys, peer_u, peer_v):
    bp, tp, dm = x_prompt.shape
    bs, ts, _ = x_sample.shape
    n_p, n_s = bp * tp, bs * ts
    g, r, d = NSA_KV_HEADS, NSA_GROUP, NSA_HEAD_DIM
    past_len = page_table.shape[1] * PAGE_SIZE
    pos_p = jnp.arange(tp, dtype=jnp.int32)
    pos_s = past_len + jnp.arange(ts, dtype=jnp.int32)
    n_real = n_p + n_s
    n_pad = _round_up(n_real, PEER_TM) - n_real
    s_rows = slice(n_p, n_real)
    pos_rows = jnp.concatenate([jnp.tile(pos_p, bp), jnp.tile(pos_s, bs), jnp.zeros((n_pad,), jnp.int32)])
    xp = x_prompt.reshape(n_p, dm)
    x = jnp.concatenate([xp, x_sample.reshape(n_s, dm), xp[:n_pad]], axis=0)
    pad_bf = jnp.zeros((n_pad, E_Q), BF16)

    w0 = e_w_in[0]
    w_in = jnp.concatenate([w0[:, :SC_COL0], w0[:, SC_COL0 + E_G:], w0[:, SC_COL0:SC_COL0 + E_G],
                            jnp.zeros((dm, _pad_cols(E_IN) - E_IN), F32)], axis=1).astype(BF16)
    proj = norm_matmul(x, norm_w[0, 0], w_in)
    q_bf, rows, win, rows_bf, win_bf = nsa_rope(proj, pos_rows)
    kvc = nsa_compress(rows, bp, tp, e_cmp_pe[0], e_cmp_w1[0], e_cmp_w2[0])
    o_nsa_p = nsa_prompt(q_bf, kvc, rows_bf, win_bf, proj, bp, tp)
    p_kv = rows[:n_p].reshape(bp, tp, 4, g, d)
    s_kv = rows[s_rows].reshape(bs, ts, 4, g, d)
    p_win = win[:n_p].reshape(bp, tp, 2, g, d)[:, tp - min(WINDOW, tp):]
    win_s = win[s_rows].reshape(bs, ts, 2, g, d)
    s_win = jnp.concatenate([cache_nsa_win[0], win_s], axis=1)[:, ts:]
    cache = cache_nsa_kv[0]
    q_s = q_bf[s_rows].reshape(bs, ts, g, r, d).transpose(0, 2, 3, 1, 4).reshape(bs, g, r * ts, d)
    a0, a1 = nsa_sample_compress(cache, page_table, e_cmp_pe[0], e_cmp_w1[0])
    o_cmp_s, sel_s = nsa_sample_select(a0, a1, e_cmp_w2[0], q_s, past_len, ts)
    o_nsa_s = nsa_sample_attend(cache, page_table, q_s, sel_s, o_cmp_s, rows, win,
                                cache_nsa_win[0], proj, past_len, ts, n_p)
    o_nsa_s = o_nsa_s.reshape(bs, g, r, ts, d).transpose(0, 3, 1, 2, 4).reshape(n_s, E_Q)
    o_nsa = jnp.concatenate([o_nsa_p, o_nsa_s.astype(BF16), pad_bf], axis=0)
    o_sc_p, tail_p = short_conv(proj, bp, tp, SC_ROWS, jnp.zeros((bp, SC_KSIZE - 1, SC_WIDTH), F32), e_sc_conv[0])
    p_sc = tail_p[:, SUBLANES - (SC_KSIZE - 1):]
    o_sc_s, tail_s = short_conv(_pad_seq_rows(proj[s_rows], bs, ts), bs, SEQ_ROWS, SEQ_ROWS, state_sc_conv[0],
                                e_sc_conv[0])
    s_sc = jnp.concatenate([state_sc_conv[0], tail_s[:, :ts]], axis=1)[:, ts:]
    o_sc = jnp.concatenate([_unpad_seq_rows(o_sc_s, bs, ts), pad_bf], axis=0)
    o_sc = jnp.concatenate([o_sc_p, o_sc], axis=0)
    x = matmul2_res(o_nsa, o_sc, e_w_out[0].astype(BF16), x)
    u_bf = peer_u.astype(BF16)
    vt_bf = peer_v.transpose(0, 2, 1).astype(BF16)
    peer_t = peer_layer(x, norm_w[0, 1], peer_wq[0], peer_keys[0], u_bf, vt_bf, 0)

    w_in = jnp.pad(o_w_in[0], ((0, 0), (0, _pad_cols(O_IN) - O_IN))).astype(BF16)
    x = x + peer_t.T
    proj = norm_matmul(x, norm_w[1, 0], w_in)
    odd_w = (o_m2_conv_w[0], o_m2_conv_b[0], o_m2_dt_bias[0], o_m2_a_log[0], o_m2_d[0], o_m2_norm[0])
    o_ret_p, p_ret = retention(proj, pos_p, bp, tp, CHUNK, jnp.zeros((bp, RET_HEADS, RET_DQK, RET_DV), F32),
                               o_ret_gn[0])
    o_ssd_p, p_ssm = ssd(proj, bp, tp, CHUNK, jnp.zeros((bp, M2_CONV - 1, M2_CONV_DIM), F32),
                         jnp.zeros((bp, M2_HEADS, M2_HEADDIM, M2_STATE), F32), *odd_w)
    xbc_cols = slice(O_SPLITS[4], O_SPLITS[5])
    p_m2c = jnp.stack([proj[(b + 1) * tp - (M2_CONV - 1):(b + 1) * tp, xbc_cols] for b in range(bp)])
    xbc_s = proj[s_rows, xbc_cols].reshape(bs, ts, M2_CONV_DIM)
    s_m2c = jnp.concatenate([state_m2_conv[0], xbc_s], axis=1)[:, ts:]
    proj_s = _pad_seq_rows(proj[s_rows], bs, ts)
    pos_s8 = past_len + jnp.arange(SEQ_ROWS, dtype=jnp.int32)
    o_ret_s, s_ret = retention(proj_s, pos_s8, bs, SEQ_ROWS, SEQ_ROWS, state_ret[0], o_ret_gn[0], valid=ts)
    o_ssd_s, s_ssm = ssd(proj_s, bs, SEQ_ROWS, SEQ_ROWS, state_m2_conv[0], state_ssm[0], *odd_w, valid=ts)
    o_ret = jnp.concatenate([o_ret_p, _unpad_seq_rows(o_ret_s, bs, ts), pad_bf], axis=0)
    o_ssd = jnp.concatenate([o_ssd_p, _unpad_seq_rows(o_ssd_s, bs, ts), pad_bf], axis=0)
    x = matmul2_res(o_ret, o_ssd, o_w_out[0].astype(BF16), x)
    peer_t = peer_layer(x, norm_w[1, 1], peer_wq[1], peer_keys[1], u_bf, vt_bf, 1)

    y = resid_rmsnorm(x, peer_t, final_norm)
    y_prompt = y[:n_p].reshape(bp, tp, dm)
    y_sample = y[s_rows].reshape(bs, ts, dm)
    return (y_prompt, y_sample, p_kv[None], p_win[None], p_sc[None], p_ret[None], p_ssm[None], p_m2c[None],
            s_kv[None], s_win[None], s_sc[None], s_ret[None], s_ssm[None], s_m2c[None])
```

```python
import functools
import math

import jax
import jax.numpy as jnp
from jax import lax
from jax.experimental import pallas as pl
from jax.experimental.pallas import tpu as pltpu

F32 = jnp.float32
BF16 = jnp.bfloat16

D_MODEL = 2048
DEPTH = 2
PAGE_SIZE = 128
NSA_HEAD_DIM = 128
NSA_HEADS = 8
NSA_KV_HEADS = 2
NSA_GROUP = 4
CMP_BLOCK = 32
CMP_STRIDE = 16
SLC_BLOCK = 64
SLC_TOPK = 16
WINDOW = 512
ROPE_THETA = 500000.0
ROPE_DIMS = 32
SC_WIDTH = 1024
SC_KSIZE = 3
RET_HEADS = 4
RET_DQK = 256
RET_DV = 256
RET_THETA = 10000.0
M2_DINNER = 1024
M2_HEADDIM = 64
M2_HEADS = 16
M2_STATE = 128
M2_GROUPS = 2
M2_CONV = 4
M2_CONV_DIM = M2_DINNER + 2 * M2_GROUPS * M2_STATE
PEER_HEADS = 8
PEER_KEYS = 128
PEER_QDIM = 256
PEER_TOPK = 16
Q_BLOCK = 128
CHUNK = 128
EPS = 1e-6

E_Q = NSA_HEADS * NSA_HEAD_DIM
E_KV = 6 * NSA_KV_HEADS * NSA_HEAD_DIM
E_G = 3 * NSA_HEADS
E_SC = 3 * SC_WIDTH
E_IN = E_Q + E_KV + E_G + E_SC
R_QK = RET_HEADS * RET_DQK
R_V = RET_HEADS * RET_DV
O_SPLITS = [R_QK, 2 * R_QK, 2 * R_QK + R_V, 2 * R_QK + 2 * R_V,
            2 * R_QK + 2 * R_V + M2_DINNER, 2 * R_QK + 2 * R_V + M2_DINNER + M2_CONV_DIM]
O_IN = O_SPLITS[-1] + M2_HEADS
SC_COL0 = E_Q + E_KV
GATE_COL0 = SC_COL0 + E_SC

LANES = 128
SUBLANES = 8
VMEM_LIMIT = 56 * 1024 * 1024
ROW_TILE_CAP = 1056
ROPE_ROWS_CAP = 544
RESID_ROWS = 768
NEG_INF = float("-inf")


def _round_up(n, m):
    return -(-n // m) * m


def _pick_tile(n, cap):
    best = LANES
    for t in range(LANES, cap + 1, LANES):
        if n % t == 0:
            best = t
    return best


def _pad_cols(m):
    return min((_round_up(m, t) for t in (768, 640, 512)))


def _row_tile(n, cap):
    best = None
    for t in range(16, cap + 1, 16):
        if n % t == 0:
            best = t
    assert best is not None
    return best


def _gelu_tanh(x):
    return 0.5 * x * (1.0 + jnp.tanh(math.sqrt(2.0 / math.pi) * (x + 0.044715 * (x * x * x))))


def _norm_matmul_kernel(x_ref, g_ref, w_ref, o_ref, xn_ref):
    @pl.when(pl.program_id(1) == 0)
    def _():
        x = x_ref[...]
        ms = jnp.mean(x * x, axis=-1, keepdims=True)
        xn_ref[...] = (x * lax.rsqrt(ms + EPS) * g_ref[...]).astype(BF16)

    o_ref[...] = jnp.dot(xn_ref[...], w_ref[...], preferred_element_type=F32)


def norm_matmul(x, gain, w_bf):
    n, k = x.shape
    m = w_bf.shape[1]
    tm = _row_tile(n, ROW_TILE_CAP)
    tn = _pick_tile(m, 768)
    return pl.pallas_call(
        _norm_matmul_kernel,
        grid=(n // tm, m // tn),
        in_specs=[pl.BlockSpec((tm, k), lambda i, j: (i, 0)),
                  pl.BlockSpec((1, k), lambda i, j: (0, 0)),
                  pl.BlockSpec((k, tn), lambda i, j: (0, j))],
        out_specs=pl.BlockSpec((tm, tn), lambda i, j: (i, j)),
        out_shape=jax.ShapeDtypeStruct((n, m), F32),
        scratch_shapes=[pltpu.VMEM((tm, k), BF16)],
        compiler_params=pltpu.CompilerParams(
            dimension_semantics=("parallel", "arbitrary"), vmem_limit_bytes=VMEM_LIMIT),
        name="norm_matmul",
    )(x, gain.reshape(1, k), w_bf)


def _matmul2_res_kernel(a1_ref, a2_ref, w1_ref, w2_ref, r_ref, o_ref):
    o_ref[...] = (r_ref[...] + jnp.dot(a1_ref[...], w1_ref[...], preferred_element_type=F32)
                  + jnp.dot(a2_ref[...], w2_ref[...], preferred_element_type=F32))


def matmul2_res(a1, a2, w_bf, res):
    n, k1 = a1.shape
    k2 = a2.shape[1]
    m = w_bf.shape[1]
    assert k1 == k2
    tm = _row_tile(n, ROW_TILE_CAP)
    tn = _pick_tile(m, 1024)
    return pl.pallas_call(
        _matmul2_res_kernel,
        grid=(n // tm, m // tn),
        in_specs=[pl.BlockSpec((tm, k1), lambda i, j: (i, 0)),
                  pl.BlockSpec((tm, k2), lambda i, j: (i, 0)),
                  pl.BlockSpec((k1, tn), lambda i, j: (0, j)),
                  pl.BlockSpec((k2, tn), lambda i, j: (1, j)),
                  pl.BlockSpec((tm, tn), lambda i, j: (i, j))],
        out_specs=pl.BlockSpec((tm, tn), lambda i, j: (i, j)),
        out_shape=jax.ShapeDtypeStruct((n, m), F32),
        compiler_params=pltpu.CompilerParams(
            dimension_semantics=("parallel", "parallel"), vmem_limit_bytes=VMEM_LIMIT),
        name="matmul2_res",
    )(a1, a2, w_bf, w_bf, res)


PEER_TOK = 128
PEER_HEAD_UNROLL = 8
PEER_TM = 768
PEER_TA = 8
PEER_TE = PEER_TA * PEER_KEYS


def _top_desc(work, count, with_rank=False, one_at_a_time=False):
    rows = []
    rank = jnp.full(work.shape, float(count), F32)
    row_id = lax.broadcasted_iota(jnp.int32, work.shape, 0).astype(F32)
    for r in range(count):
        m = jnp.max(work, axis=0, keepdims=True)
        rows.append(m)
        hit = work == m
        if one_at_a_time:
            hit = row_id == jnp.min(jnp.where(hit, row_id, float(work.shape[0])), axis=0, keepdims=True)
        if with_rank:
            rank = jnp.where(hit, float(r), rank)
        work = jnp.where(hit, NEG_INF, work)
    return (rows, rank) if with_rank else rows


def _peer_router_kernel(q_ref, k_ref, cnt_ref, gw_ref, r2_ref, w2_ref):
    def head(h, carry):
        q = q_ref[h]
        scores = []
        for side in range(2):
            qs = q[:, side * PEER_KEYS:(side + 1) * PEER_KEYS]
            qs = qs * lax.rsqrt(jnp.mean(qs * qs, axis=-1, keepdims=True) + EPS)
            scores.append(lax.dot_general(k_ref[h, side], qs, (((1,), (1,)), ((), ())),
                                          preferred_element_type=F32))
        s1, s2 = scores
        v1 = _top_desc(s1, PEER_TOPK + 1)
        v2, rank2 = _top_desc(s2, PEER_TOPK + 1, with_rank=True)
        v2_lo = jnp.concatenate(v2[:8], axis=0)
        v2_hi = jnp.concatenate(v2[8:16], axis=0)
        row = lax.broadcasted_iota(jnp.int32, v2_lo.shape, 0)
        blocks = [v1[0] + v2_lo, v1[0] + v2_hi, v1[1] + v2_lo]
        for a, lim in ((2, 5), (3, 4), (4, 3), (5, 2), (6, 2), (7, 2)):
            blocks.append(jnp.where(row < lim, v1[a] + v2_lo, NEG_INF))
        blocks.append(jnp.concatenate(v1[8:16], axis=0) + v2[0])
        extra = jnp.where(row == 0, v1[0] + v2[16], jnp.where(row == 1, v1[16] + v2[0], NEG_INF))
        blocks.append(extra)
        cand = jnp.concatenate(blocks, axis=0)
        tops = _top_desc(cand, PEER_TOPK + 1, one_at_a_time=True)
        z = jnp.zeros_like(tops[0])
        for r in range(PEER_TOPK):
            z = z + jnp.exp(tops[r] - tops[0])
        tau = 0.5 * (tops[PEER_TOPK - 1] + tops[PEER_TOPK])
        count = jnp.zeros_like(s1)
        for r in range(PEER_TOPK):
            count = count + jnp.where(s1 >= tau - v2[r], 1.0, 0.0)
        cnt_ref[h] = count
        gw_ref[h] = jnp.exp(s1 - v1[0]) / z
        r2_ref[h] = rank2.astype(BF16)
        w2_ref[h] = jnp.exp(s2 - v2[0]).astype(BF16)
        return carry

    lax.fori_loop(0, PEER_HEADS, head, 0, unroll=PEER_HEAD_UNROLL)


def peer_router(q_hm, keys):
    n = q_hm.shape[1]
    out = jax.ShapeDtypeStruct((PEER_HEADS, PEER_KEYS, n), F32)
    out_bf = jax.ShapeDtypeStruct((PEER_HEADS, PEER_KEYS, n), BF16)
    spec = pl.BlockSpec((PEER_HEADS, PEER_KEYS, PEER_TOK), lambda i: (0, 0, i))
    return pl.pallas_call(
        _peer_router_kernel,
        grid=(n // PEER_TOK,),
        in_specs=[pl.BlockSpec((PEER_HEADS, PEER_TOK, PEER_QDIM), lambda i: (0, i, 0)),
                  pl.BlockSpec((PEER_HEADS, 2, PEER_KEYS, PEER_QDIM // 2), lambda i: (0, 0, 0, 0))],
        out_specs=[spec, spec, spec, spec],
        out_shape=[out, out, out_bf, out_bf],
        compiler_params=pltpu.CompilerParams(dimension_semantics=("parallel",), vmem_limit_bytes=VMEM_LIMIT),
        name="peer_router",
    )(q_hm, keys)


def _peer_expert_kernel(xt_ref, u_ref, vt_ref, cnt_ref, gw_ref, r2_ref, w2_ref, o_ref, s_scr, hg_scr):
    e = pl.program_id(1)
    s_scr[...] = jnp.dot(u_ref[...], xt_ref[...], preferred_element_type=F32)
    zero = jnp.zeros((), BF16)
    for al in range(PEER_TA):
        rows = slice(al * PEER_KEYS, (al + 1) * PEER_KEYS)
        g = None
        for h in range(PEER_HEADS):
            cnt = cnt_ref[h, al:al + 1, :].astype(BF16)
            gate = gw_ref[h, al:al + 1, :].astype(BF16)
            t = jnp.where(r2_ref[h] < cnt, w2_ref[h], zero) * gate
            g = t if g is None else g + t
        hg_scr[rows, :] = _gelu_tanh(s_scr[rows, :]).astype(BF16) * g
    part = jnp.dot(vt_ref[...], hg_scr[...], preferred_element_type=F32)

    @pl.when(e == 0)
    def _():
        o_ref[...] = part

    @pl.when(e != 0)
    def _():
        o_ref[...] += part


def peer_experts(xt_bf, u_bf, vt_bf, layer, cnt, gw, r2, w2):
    d, n = xt_bf.shape
    n_exp = u_bf.shape[1]
    tm = PEER_TM
    sel_spec = pl.BlockSpec((PEER_HEADS, PEER_TA, tm), lambda i, e: (0, e, i))
    all_spec = pl.BlockSpec((PEER_HEADS, PEER_KEYS, tm), lambda i, e: (0, 0, i))
    return pl.pallas_call(
        _peer_expert_kernel,
        grid=(n // tm, n_exp // PEER_TE),
        in_specs=[pl.BlockSpec((d, tm), lambda i, e: (0, i)),
                  pl.BlockSpec((None, PEER_TE, d), lambda i, e: (layer, e, 0)),
                  pl.BlockSpec((None, d, PEER_TE), lambda i, e: (layer, 0, e)),
                  sel_spec, sel_spec, all_spec, all_spec],
        out_specs=pl.BlockSpec((d, tm), lambda i, e: (0, i)),
        out_shape=jax.ShapeDtypeStruct((d, n), F32),
        scratch_shapes=[pltpu.VMEM((PEER_TE, tm), F32), pltpu.VMEM((PEER_TE, tm), BF16)],
        compiler_params=pltpu.CompilerParams(
            dimension_semantics=("parallel", "arbitrary"), vmem_limit_bytes=VMEM_LIMIT),
        name="peer_experts",
    )(xt_bf, u_bf, vt_bf, cnt, gw, r2, w2)


def _norm_matmul_t_kernel(x_ref, g_ref, w_ref, o_ref, xt_ref, xn_ref):
    @pl.when(pl.program_id(1) == 0)
    def _():
        x = x_ref[...]
        ms = jnp.mean(x * x, axis=-1, keepdims=True)
        xn = x * lax.rsqrt(ms + EPS) * g_ref[...]
        xn_ref[...] = xn.astype(BF16)
        xt_ref[...] = xn.T.astype(BF16)

    o_ref[...] = jnp.dot(xn_ref[...], w_ref[...], preferred_element_type=F32).reshape(o_ref.shape)


def norm_matmul_t(x, gain, w_bf, cols):
    n, k = x.shape
    m = w_bf.shape[1]
    tm = PEER_TM
    return pl.pallas_call(
        _norm_matmul_t_kernel,
        grid=(n // tm, m // cols),
        in_specs=[pl.BlockSpec((tm, k), lambda i, j: (i, 0)),
                  pl.BlockSpec((1, k), lambda i, j: (0, 0)),
                  pl.BlockSpec((k, cols), lambda i, j: (0, j))],
        out_specs=[pl.BlockSpec((1, tm, cols), lambda i, j: (j, i, 0)),
                   pl.BlockSpec((k, tm), lambda i, j: (0, i))],
        out_shape=[jax.ShapeDtypeStruct((m // cols, n, cols), F32), jax.ShapeDtypeStruct((k, n), BF16)],
        scratch_shapes=[pltpu.VMEM((tm, k), BF16)],
        compiler_params=pltpu.CompilerParams(
            dimension_semantics=("parallel", "arbitrary"), vmem_limit_bytes=VMEM_LIMIT),
        name="norm_matmul_t",
    )(x, gain.reshape(1, k), w_bf)


def _resid_rmsnorm_kernel(x_ref, dt_ref, g_ref, o_ref):
    x = x_ref[...] + dt_ref[...].T
    ms = jnp.mean(x * x, axis=-1, keepdims=True)
    o_ref[...] = x * lax.rsqrt(ms + EPS) * g_ref[...]


def resid_rmsnorm(x, delta_t, gain):
    n, k = x.shape
    tm = _pick_tile(n, RESID_ROWS)
    return pl.pallas_call(
        _resid_rmsnorm_kernel,
        grid=(n // tm,),
        in_specs=[pl.BlockSpec((tm, k), lambda i: (i, 0)), pl.BlockSpec((k, tm), lambda i: (0, i)),
                  pl.BlockSpec((1, k), lambda i: (0, 0))],
        out_specs=pl.BlockSpec((tm, k), lambda i: (i, 0)),
        out_shape=jax.ShapeDtypeStruct((n, k), F32),
        compiler_params=pltpu.CompilerParams(dimension_semantics=("parallel",), vmem_limit_bytes=VMEM_LIMIT),
        name="resid_rmsnorm",
    )(x, delta_t, gain.reshape(1, k))


def peer_layer(x, gain, w_q, keys, u_bf, vt_bf, layer):
    q_hm, xt_bf = norm_matmul_t(x, gain, w_q.astype(BF16), PEER_QDIM)
    cnt, gw, r2, w2 = peer_router(q_hm, keys)
    return peer_experts(xt_bf, u_bf, vt_bf, layer, cnt, gw, r2, w2)


def _rope_tables(pos):
    half = ROPE_DIMS // 2
    inv = ROPE_THETA ** (-jnp.arange(half, dtype=F32) / half)
    ang = pos.astype(F32)[:, None] * inv[None, :]
    cos, sin = jnp.cos(ang), jnp.sin(ang)
    t = pos.shape[0]
    ones = jnp.ones((t, NSA_HEAD_DIM - ROPE_DIMS), F32)
    zeros = jnp.zeros((t, NSA_HEAD_DIM - ROPE_DIMS), F32)
    zh = jnp.zeros((t, half), F32)
    c = jnp.concatenate([cos, cos, ones], axis=1)
    s_lo = jnp.concatenate([-sin, zh, zeros], axis=1)
    s_hi = jnp.concatenate([zh, sin, zeros], axis=1)
    return c, s_lo, s_hi


def _rope_kernel(p_ref, c_ref, sl_ref, sh_ref, q_ref, rows_ref, win_ref, rows_bf_ref, win_bf_ref):
    c, sl, sh = c_ref[...], sl_ref[...], sh_ref[...]
    half = ROPE_DIMS // 2

    def rot(x):
        return x * c + pltpu.roll(x, LANES - half, 1) * sl + pltpu.roll(x, half, 1) * sh

    d = NSA_HEAD_DIM
    scale = d ** -0.5
    for hd in range(NSA_HEADS):
        q_ref[:, hd * d:(hd + 1) * d] = (rot(p_ref[:, hd * d:(hd + 1) * d]) * scale).astype(BF16)
    for blk in range(12):
        x = p_ref[:, E_Q + blk * d:E_Q + (blk + 1) * d]
        if (blk // 2) % 2 == 0:
            x = rot(x)
        if blk < 8:
            rows_ref[:, blk * d:(blk + 1) * d] = x
            rows_bf_ref[:, blk * d:(blk + 1) * d] = x.astype(BF16)
        else:
            win_ref[:, (blk - 8) * d:(blk - 7) * d] = x
            win_bf_ref[:, (blk - 8) * d:(blk - 7) * d] = x.astype(BF16)


def nsa_rope(proj, pos_rows):
    n = proj.shape[0]
    tm = _row_tile(n, ROPE_ROWS_CAP)
    c, sl, sh = _rope_tables(pos_rows)
    width = E_Q + E_KV
    tab = pl.BlockSpec((tm, LANES), lambda i: (i, 0))
    return pl.pallas_call(
        _rope_kernel,
        grid=(n // tm,),
        in_specs=[pl.BlockSpec((tm, width), lambda i: (i, 0)), tab, tab, tab],
        out_specs=[pl.BlockSpec((tm, E_Q), lambda i: (i, 0)),
                   pl.BlockSpec((tm, 1024), lambda i: (i, 0)),
                   pl.BlockSpec((tm, 512), lambda i: (i, 0)),
                   pl.BlockSpec((tm, 1024), lambda i: (i, 0)),
                   pl.BlockSpec((tm, 512), lambda i: (i, 0))],
        out_shape=[jax.ShapeDtypeStruct((n, E_Q), BF16),
                   jax.ShapeDtypeStruct((n, 1024), F32),
                   jax.ShapeDtypeStruct((n, 512), F32),
                   jax.ShapeDtypeStruct((n, 1024), BF16),
                   jax.ShapeDtypeStruct((n, 512), BF16)],
        compiler_params=pltpu.CompilerParams(dimension_semantics=("parallel",), vmem_limit_bytes=VMEM_LIMIT),
        name="nsa_rope",
    )(proj, c, sl, sh)


def _compress_kernel(x_ref, pe_ref, w1_ref, w2_ref, o_ref):
    n_sub = x_ref.shape[0] // CMP_STRIDE
    acc0 = jnp.zeros((n_sub, NSA_HEAD_DIM), F32)
    acc1 = jnp.zeros((n_sub, NSA_HEAD_DIM), F32)
    for s in range(CMP_STRIDE):
        xs = x_ref[pl.ds(s, n_sub, stride=CMP_STRIDE), :]
        a0 = (xs + pe_ref[0, s:s + 1, :]).astype(BF16)
        a1 = (xs + pe_ref[0, CMP_STRIDE + s:CMP_STRIDE + s + 1, :]).astype(BF16)
        acc0 = acc0 + jnp.dot(a0, w1_ref[0, s], preferred_element_type=F32)
        acc1 = acc1 + jnp.dot(a1, w1_ref[0, CMP_STRIDE + s], preferred_element_type=F32)
    pre = acc0 + pltpu.roll(acc1, n_sub - 1, 0)
    o_ref[0, 0] = jnp.dot(_gelu_tanh(pre).astype(BF16), w2_ref[0], preferred_element_type=F32).astype(BF16)


def nsa_compress(rows, n_batch, t_len, cmp_pe, cmp_w1, cmp_w2):
    n_sub = t_len // CMP_STRIDE
    d = NSA_HEAD_DIM
    w1 = cmp_w1.reshape(2, CMP_BLOCK, d, d).astype(BF16)
    return pl.pallas_call(
        _compress_kernel,
        grid=(n_batch, 4),
        in_specs=[pl.BlockSpec((t_len, d), lambda b, c: (b, c)),
                  pl.BlockSpec((1, CMP_BLOCK, d), lambda b, c: (c // 2, 0, 0)),
                  pl.BlockSpec((1, CMP_BLOCK, d, d), lambda b, c: (c // 2, 0, 0, 0)),
                  pl.BlockSpec((1, d, d), lambda b, c: (c // 2, 0, 0))],
        out_specs=pl.BlockSpec((1, 1, n_sub, d), lambda b, c: (b, c, 0, 0)),
        out_shape=jax.ShapeDtypeStruct((n_batch, 4, n_sub, d), BF16),
        compiler_params=pltpu.CompilerParams(
            dimension_semantics=("parallel", "parallel"), vmem_limit_bytes=VMEM_LIMIT),
        name="nsa_compress",
    )(rows, cmp_pe, w1, cmp_w2.astype(BF16))


NSA_TK = 512
NSA_WTILES = WINDOW // Q_BLOCK + 1


def _masked_softmax_rows(s, mask):
    s = jnp.where(mask, s, -1e30)
    p = jnp.exp(s - jnp.max(s, axis=-1, keepdims=True))
    p = jnp.where(mask, p, 0.0)
    return p / jnp.maximum(jnp.sum(p, axis=-1, keepdims=True), 1e-30)


def _nsa_prompt_kernel(q_ref, *refs):
    gate_ref, o_ref = refs[-2:]
    per_group = (len(refs) - 2) // NSA_KV_HEADS
    for g in range(NSA_KV_HEADS):
        _nsa_prompt_group(g, q_ref, *refs[g * per_group:(g + 1) * per_group], gate_ref, o_ref)


def _nsa_prompt_group(g, q_ref, kc_ref, vc_ref, ks_ref, vs_ref, kw_ref, vw_ref, gate_ref, o_ref):
    qi = pl.program_id(1)
    d, r, qb = NSA_HEAD_DIM, NSA_GROUP, Q_BLOCK
    c0 = g * r * d
    n_cmp = kc_ref.shape[2]
    n_slc = ks_ref.shape[0] // SLC_BLOCK
    nt = (((1,), (1,)), ((), ()))
    q = jnp.concatenate([q_ref[:, c0 + h * d:c0 + (h + 1) * d] for h in range(r)], axis=0)
    t_col = qi * qb + lax.broadcasted_iota(jnp.int32, (qb, 1), 0)

    s = lax.dot_general(q, kc_ref[0, 0], nt, preferred_element_type=F32).reshape(r, qb, n_cmp)
    n_idx = lax.broadcasted_iota(jnp.int32, (qb, n_cmp), 1)
    c_mask = (n_idx * CMP_STRIDE + (CMP_BLOCK - 1) <= t_col)[None]
    p_c = _masked_softmax_rows(s, c_mask)
    o_cmp = jnp.dot(p_c.reshape(r * qb, n_cmp).astype(BF16), vc_ref[0, 0], preferred_element_type=F32)

    nb = LANES
    assert n_slc <= nb and qb == LANES
    p_sum = jnp.sum(p_c, axis=0)
    si = lax.broadcasted_iota(jnp.int32, (nb, n_cmp), 0) * SLC_BLOCK
    ci = lax.broadcasted_iota(jnp.int32, (nb, n_cmp), 1) * CMP_STRIDE
    overlap_t = jnp.where((ci < si + SLC_BLOCK) & (ci + CMP_BLOCK > si), 1.0, 0.0).astype(BF16)
    p_hi = p_sum.astype(BF16)
    p_lo = (p_sum - p_hi.astype(F32)).astype(BF16)
    imp_t = (lax.dot_general(overlap_t, p_hi, nt, preferred_element_type=F32)
             + lax.dot_general(overlap_t, p_lo, nt, preferred_element_type=F32))
    blk = lax.broadcasted_iota(jnp.int32, (nb, qb), 0)
    blk_f = blk.astype(F32)
    t_row = qi * qb + lax.broadcasted_iota(jnp.int32, (1, qb), 1)
    cur = t_row // SLC_BLOCK
    forced = (blk == 0) | (blk == cur) | (blk == cur - 1)
    work = jnp.where(blk * SLC_BLOCK <= t_row, jnp.where(forced, 1e6, imp_t), -1e6)
    work = jnp.where(blk < n_slc, work, NEG_INF)
    sel_t = jnp.zeros((nb, qb), F32)
    for _ in range(min(SLC_TOPK, n_slc)):
        m = jnp.max(work, axis=0, keepdims=True)
        first = jnp.min(jnp.where(work == m, blk_f, float(nb)), axis=0, keepdims=True)
        pick = blk_f == first
        sel_t = jnp.where(pick, 1.0, sel_t)
        work = jnp.where(pick, NEG_INF, work)
    sel_bf = sel_t.T.astype(BF16)

    tk = NSA_TK
    bpt = tk // SLC_BLOCK

    def slc_step(kt, carry):
        m_run, l_run, acc = carry
        start = pl.multiple_of(kt * tk, tk)
        k = ks_ref[pl.ds(start, tk), :]
        v = vs_ref[pl.ds(start, tk), :]
        s = lax.dot_general(q, k, nt, preferred_element_type=F32).reshape(r, qb, tk)
        ei = lax.broadcasted_iota(jnp.int32, (nb, tk), 0)
        ej = lax.broadcasted_iota(jnp.int32, (nb, tk), 1)
        expand = jnp.where(ei == kt * bpt + ej // SLC_BLOCK, 1.0, 0.0).astype(BF16)
        picked = jnp.dot(sel_bf, expand, preferred_element_type=F32)
        kpos = start + lax.broadcasted_iota(jnp.int32, (qb, tk), 1)
        mask = ((picked > 0.5) & (kpos <= t_col))[None]
        s = jnp.where(mask, s, -1e30)
        m_new = jnp.maximum(m_run, jnp.max(s, axis=-1, keepdims=True))
        alpha = jnp.exp(m_run - m_new)
        p = jnp.where(mask, jnp.exp(s - m_new), 0.0)
        l_new = alpha * l_run + jnp.sum(p, axis=-1, keepdims=True)
        pv = jnp.dot(p.reshape(r * qb, tk).astype(BF16), v, preferred_element_type=F32)
        acc = alpha.reshape(r * qb, 1) * acc + pv
        return m_new, l_new, acc

    init = (jnp.full((r, qb, 1), -1e30, F32), jnp.zeros((r, qb, 1), F32), jnp.zeros((r * qb, d), F32))
    _, l_fin, acc = lax.fori_loop(0, (qi * qb) // tk + 1, slc_step, init)
    o_slc = acc / jnp.maximum(l_fin.reshape(r * qb, 1), 1e-30)

    k_tiles, v_tiles, pos_tiles = [], [], []
    for j in range(NSA_WTILES):
        kt = qi - (NSA_WTILES - 1) + j
        ktc = jnp.maximum(kt, 0)
        start = pl.multiple_of(ktc * qb, qb)
        k_tiles.append(kw_ref[pl.ds(start, qb), :])
        v_tiles.append(vw_ref[pl.ds(start, qb), :])
        lane = lax.broadcasted_iota(jnp.int32, (qb, qb), 1)
        pos_tiles.append(jnp.where(kt >= 0, start + lane, -1))
    k_w = jnp.concatenate(k_tiles, axis=0)
    v_w = jnp.concatenate(v_tiles, axis=0)
    k_pos = jnp.concatenate(pos_tiles, axis=1)
    span = NSA_WTILES * qb
    s = lax.dot_general(q, k_w, nt, preferred_element_type=F32).reshape(r, qb, span)
    dpos = t_col - k_pos
    w_mask = ((dpos >= 0) & (dpos < WINDOW) & (k_pos >= 0))[None]
    p_w = _masked_softmax_rows(s, w_mask)
    o_win = jnp.dot(p_w.reshape(r * qb, span).astype(BF16), v_w, preferred_element_type=F32)

    sig = jax.nn.sigmoid(gate_ref[...])
    lane = lax.broadcasted_iota(jnp.int32, sig.shape, 1)
    for h in range(r):
        rows = slice(h * qb, (h + 1) * qb)
        out = jnp.zeros((qb, d), F32)
        for branch, o_b in enumerate((o_cmp, o_slc, o_win)):
            col = branch * NSA_HEADS + g * r + h
            gate = jnp.sum(jnp.where(lane == col, sig, 0.0), axis=-1, keepdims=True)
            out = out + gate * o_b[rows]
        o_ref[:, c0 + h * d:c0 + (h + 1) * d] = out.astype(o_ref.dtype)


def nsa_prompt(q_bf, kvc, rows_bf, win_bf, proj, n_batch, t_len):
    d, r, qb = NSA_HEAD_DIM, NSA_GROUP, Q_BLOCK
    nqb = t_len // qb
    n_sub = kvc.shape[2]
    gate_blk = GATE_COL0 // LANES
    seq = lambda col: pl.BlockSpec((t_len, d), lambda b, i: (b, col))
    cmp_spec = lambda kind: pl.BlockSpec((1, 1, n_sub, d), lambda b, i: (b, kind, 0, 0))
    group_specs, group_args = [], []
    for g in range(NSA_KV_HEADS):
        group_specs += [cmp_spec(g), cmp_spec(2 + g), seq(4 + g), seq(6 + g), seq(g), seq(2 + g)]
        group_args += [kvc, kvc, rows_bf, rows_bf, win_bf, win_bf]
    return pl.pallas_call(
        _nsa_prompt_kernel,
        grid=(n_batch, nqb),
        in_specs=[pl.BlockSpec((qb, E_Q), lambda b, i: (b * nqb + i, 0))] + group_specs + [
            pl.BlockSpec((qb, LANES), lambda b, i: (b * nqb + i, gate_blk))],
        out_specs=pl.BlockSpec((qb, E_Q), lambda b, i: (b * nqb + i, 0)),
        out_shape=jax.ShapeDtypeStruct((n_batch * t_len, E_Q), BF16),
        compiler_params=pltpu.CompilerParams(
            dimension_semantics=("parallel", "arbitrary"), vmem_limit_bytes=VMEM_LIMIT),
        name="nsa_prompt",
    )(q_bf, *group_args, proj)


SAMPLE_PAGES = 32
NEW_ROWS_BLK = 16


def _page_specs(kind):
    def spec(i):
        return pl.BlockSpec((1, PAGE_SIZE, 1, NSA_KV_HEADS, NSA_HEAD_DIM),
                            lambda s, c, pt: (pt[s, c * SAMPLE_PAGES + i], 0, kind, 0, 0))
    return [spec(i) for i in range(SAMPLE_PAGES)]


def _sample_compress_kernel(pt_ref, *refs):
    n_in = 2 * SAMPLE_PAGES
    pages = refs[:n_in]
    pe_ref, w1_ref, a0_ref, a1_ref = refs[n_in:]
    d = NSA_HEAD_DIM
    per_page = PAGE_SIZE // CMP_STRIDE
    rows = SAMPLE_PAGES * per_page
    gs = NSA_KV_HEADS * CMP_STRIDE
    for kv in range(2):
        by_offset = [jnp.swapaxes(pages[kv * SAMPLE_PAGES + p].reshape(NSA_KV_HEADS * PAGE_SIZE, d)[...]
                                  .reshape(per_page, gs, d), 0, 1) for p in range(SAMPLE_PAGES)]
        acc0 = jnp.zeros((NSA_KV_HEADS * rows, d), F32)
        acc1 = jnp.zeros((NSA_KV_HEADS * rows, d), F32)
        for s in range(CMP_STRIDE):
            xs = jnp.concatenate([by_offset[p][NSA_KV_HEADS * s + g]
                                  for g in range(NSA_KV_HEADS) for p in range(SAMPLE_PAGES)],
                                 axis=0)
            a0 = (xs + pe_ref[kv, s:s + 1, :]).astype(BF16)
            a1 = (xs + pe_ref[kv, CMP_STRIDE + s:CMP_STRIDE + s + 1, :]).astype(BF16)
            acc0 = acc0 + jnp.dot(a0, w1_ref[kv, s], preferred_element_type=F32)
            acc1 = acc1 + jnp.dot(a1, w1_ref[kv, CMP_STRIDE + s], preferred_element_type=F32)
        for g in range(NSA_KV_HEADS):
            a0_ref[0, kv * NSA_KV_HEADS + g] = acc0[g * rows:(g + 1) * rows]
            a1_ref[0, kv * NSA_KV_HEADS + g] = acc1[g * rows:(g + 1) * rows]


def nsa_sample_compress(cache, page_table, cmp_pe, cmp_w1):
    bs, n_pages = page_table.shape
    d = NSA_HEAD_DIM
    per_page = PAGE_SIZE // CMP_STRIDE
    n_sub = n_pages * per_page
    rows = SAMPLE_PAGES * per_page
    w1 = cmp_w1.reshape(2, CMP_BLOCK, d, d).astype(BF16)
    out = jax.ShapeDtypeStruct((bs, 4, n_sub, d), F32)
    ospec = pl.BlockSpec((1, 4, rows, d), lambda s, c, pt: (s, 0, c, 0))
    page_specs = _page_specs(0) + _page_specs(1)
    return pl.pallas_call(
        _sample_compress_kernel,
        grid_spec=pltpu.PrefetchScalarGridSpec(
            num_scalar_prefetch=1,
            grid=(bs, n_pages // SAMPLE_PAGES),
            in_specs=page_specs + [
                pl.BlockSpec((2, CMP_BLOCK, d), lambda s, c, pt: (0, 0, 0)),
                pl.BlockSpec((2, CMP_BLOCK, d, d), lambda s, c, pt: (0, 0, 0, 0))],
            out_specs=[ospec, ospec]),
        out_shape=[out, out],
        compiler_params=pltpu.CompilerParams(
            dimension_semantics=("parallel", "arbitrary"), vmem_limit_bytes=VMEM_LIMIT),
        name="nsa_sample_compress",
    )(page_table, *([cache] * (2 * SAMPLE_PAGES)), cmp_pe, w1)


def _sample_select_kernel(a0_ref, a1_ref, w2_ref, q_ref, ocmp_ref, sel_ref, *, past_len, t_len):
    d, r = NSA_HEAD_DIM, NSA_GROUP
    n_sub = a0_ref.shape[2]
    n_cmp = n_sub - 1
    n_slc = -(-(past_len + t_len) // SLC_BLOCK)
    lanes = sel_ref.shape[3]
    nt = (((1,), (1,)), ((), ()))
    rq = r * t_len
    for g in range(NSA_KV_HEADS):
        kc = jnp.dot(_gelu_tanh(a0_ref[0, g] + pltpu.roll(a1_ref[0, g], n_sub - 1, 0)).astype(BF16), w2_ref[0],
                     preferred_element_type=F32).astype(BF16)
        vc = jnp.dot(_gelu_tanh(a0_ref[0, 2 + g] + pltpu.roll(a1_ref[0, 2 + g], n_sub - 1, 0)).astype(BF16),
                     w2_ref[1], preferred_element_type=F32).astype(BF16)
        q = q_ref[0, g]
        s = lax.dot_general(q, kc, nt, preferred_element_type=F32)
        pos = past_len + lax.broadcasted_iota(jnp.int32, (rq, 1), 0) % t_len
        n_idx = lax.broadcasted_iota(jnp.int32, (rq, n_sub), 1)
        p_c = _masked_softmax_rows(s, (n_idx * CMP_STRIDE + (CMP_BLOCK - 1) <= pos) & (n_idx < n_cmp))
        ocmp_ref[0, g] = jnp.dot(p_c.astype(BF16), vc, preferred_element_type=F32)
        ri = lax.broadcasted_iota(jnp.int32, (SUBLANES, rq), 0)
        cj = lax.broadcasted_iota(jnp.int32, (SUBLANES, rq), 1)
        head_sum = jnp.where(cj % t_len == ri, 1.0, 0.0).astype(BF16)
        p_hi = p_c.astype(BF16)
        p_lo = (p_c - p_hi.astype(F32)).astype(BF16)
        p_sum = (jnp.dot(head_sum, p_hi, preferred_element_type=F32)
                 + jnp.dot(head_sum, p_lo, preferred_element_type=F32))
        ci = lax.broadcasted_iota(jnp.int32, (n_sub, lanes), 0)
        mi = lax.broadcasted_iota(jnp.int32, (n_sub, lanes), 1)
        overlap = jnp.where((ci * CMP_STRIDE < mi * SLC_BLOCK + SLC_BLOCK)
                            & (ci * CMP_STRIDE + CMP_BLOCK > mi * SLC_BLOCK) & (ci < n_cmp), 1.0, 0.0).astype(BF16)
        s_hi = p_sum.astype(BF16)
        s_lo = (p_sum - s_hi.astype(F32)).astype(BF16)
        imp = (jnp.dot(s_hi, overlap, preferred_element_type=F32)
               + jnp.dot(s_lo, overlap, preferred_element_type=F32))
        blk = lax.broadcasted_iota(jnp.int32, (SUBLANES, lanes), 1)
        blk_f = blk.astype(F32)
        tpos = past_len + lax.broadcasted_iota(jnp.int32, (SUBLANES, 1), 0) % t_len
        cur = tpos // SLC_BLOCK
        forced = (blk == 0) | (blk == cur) | (blk == cur - 1)
        work = jnp.where(blk * SLC_BLOCK <= tpos, jnp.where(forced, 1e6, imp), -1e6)
        work = jnp.where(blk < n_slc, work, NEG_INF)
        sel = jnp.zeros((SUBLANES, lanes), F32)
        for _ in range(min(SLC_TOPK, n_slc)):
            m = jnp.max(work, axis=-1, keepdims=True)
            first = jnp.min(jnp.where(work == m, blk_f, float(lanes)), axis=-1, keepdims=True)
            pick = blk_f == first
            sel = jnp.where(pick, 1.0, sel)
            work = jnp.where(pick, NEG_INF, work)
        sel_ref[0, g] = sel


def nsa_sample_select(a0, a1, cmp_w2, q_s, past_len, t_len):
    bs, _, n_sub, d = a0.shape
    n_slc = -(-(past_len + t_len) // SLC_BLOCK)
    lanes = _round_up(n_slc, LANES)
    rq = NSA_GROUP * t_len
    aspec = pl.BlockSpec((1, 4, n_sub, d), lambda s: (s, 0, 0, 0))
    return pl.pallas_call(
        functools.partial(_sample_select_kernel, past_len=past_len, t_len=t_len),
        grid=(bs,),
        in_specs=[aspec, aspec, pl.BlockSpec((2, d, d), lambda s: (0, 0, 0)),
                  pl.BlockSpec((1, NSA_KV_HEADS, rq, d), lambda s: (s, 0, 0, 0))],
        out_specs=[pl.BlockSpec((1, NSA_KV_HEADS, rq, d), lambda s: (s, 0, 0, 0)),
                   pl.BlockSpec((1, NSA_KV_HEADS, SUBLANES, lanes), lambda s: (s, 0, 0, 0))],
        out_shape=[jax.ShapeDtypeStruct((bs, NSA_KV_HEADS, rq, d), F32),
                   jax.ShapeDtypeStruct((bs, NSA_KV_HEADS, SUBLANES, lanes), F32)],
        compiler_params=pltpu.CompilerParams(dimension_semantics=("parallel",), vmem_limit_bytes=VMEM_LIMIT),
        name="nsa_sample_select",
    )(a0, a1, cmp_w2.astype(BF16), q_s)


def _sample_attend_kernel(pt_ref, *refs, past_len, t_len, row0):
    k_pages = refs[:SAMPLE_PAGES]
    v_pages = refs[SAMPLE_PAGES:2 * SAMPLE_PAGES]
    (q_ref, sel_ref, ocmp_ref, newrows_ref, kwin_ref, vwin_ref, newwin_ref, gate_ref, o_ref,
     m_scr, l_scr, acc_scr) = refs[2 * SAMPLE_PAGES:]
    s_idx = pl.program_id(0)
    c = pl.program_id(1)
    d, r = NSA_HEAD_DIM, NSA_GROUP
    rq = r * t_len
    lanes = sel_ref.shape[3]
    tk = SAMPLE_PAGES * PAGE_SIZE
    nt = (((1,), (1,)), ((), ()))

    @pl.when(c == 0)
    def _():
        m_scr[...] = jnp.full(m_scr.shape, -1e30, F32)
        l_scr[...] = jnp.zeros(l_scr.shape, F32)
        acc_scr[...] = jnp.zeros(acc_scr.shape, F32)

    row = lax.broadcasted_iota(jnp.int32, (rq, 1), 0)
    t_row = row % t_len
    pos = past_len + t_row
    ti = lax.broadcasted_iota(jnp.int32, (rq, SUBLANES), 1)
    tok_expand = jnp.where(ti == t_row, 1.0, 0.0).astype(BF16)

    def online(g, s, mask, v):
        s = jnp.where(mask, s, -1e30)
        m_old = m_scr[g]
        m_new = jnp.maximum(m_old, jnp.max(s, axis=-1, keepdims=True))
        alpha = jnp.exp(m_old - m_new)
        p = jnp.where(mask, jnp.exp(s - m_new), 0.0)
        l_scr[g] = alpha * l_scr[g] + jnp.sum(p, axis=-1, keepdims=True)
        acc_scr[g] = alpha * acc_scr[g] + jnp.dot(p.astype(BF16), v, preferred_element_type=F32)
        m_scr[g] = m_new

    sel16 = []
    for g in range(NSA_KV_HEADS):
        sel16.append(jnp.dot(tok_expand, sel_ref[0, g].astype(BF16), preferred_element_type=F32))
        k = jnp.concatenate([pg.reshape(NSA_KV_HEADS * PAGE_SIZE, d)[pl.ds(g, PAGE_SIZE, stride=NSA_KV_HEADS), :]
                             for pg in k_pages], axis=0)
        v = jnp.concatenate([pg.reshape(NSA_KV_HEADS * PAGE_SIZE, d)[pl.ds(g, PAGE_SIZE, stride=NSA_KV_HEADS), :]
                             for pg in v_pages], axis=0)
        q = q_ref[0, g]
        s = lax.dot_general(q, k.astype(BF16), nt, preferred_element_type=F32)
        ei = lax.broadcasted_iota(jnp.int32, (lanes, tk), 0)
        ej = lax.broadcasted_iota(jnp.int32, (lanes, tk), 1)
        expand = jnp.where(ei == c * (tk // SLC_BLOCK) + ej // SLC_BLOCK, 1.0, 0.0).astype(BF16)
        picked = jnp.dot(sel16[g].astype(BF16), expand, preferred_element_type=F32)
        kpos = c * tk + lax.broadcasted_iota(jnp.int32, (rq, tk), 1)
        online(g, s, (picked > 0.5) & (kpos <= pos), v.astype(BF16))

    @pl.when(c == pl.num_programs(1) - 1)
    def _():
        mine = (row0 // t_len + s_idx) % (NEW_ROWS_BLK // t_len)
        j = lax.broadcasted_iota(jnp.int32, (rq, NEW_ROWS_BLK), 1)
        own = j // t_len == mine
        new_pos = past_len + j % t_len
        sig = jax.nn.sigmoid(gate_ref[...])
        gsel = jnp.where(j == mine * t_len + t_row, 1.0, 0.0).astype(BF16)
        s_hi = sig.astype(BF16)
        s_lo = (sig - s_hi.astype(F32)).astype(BF16)
        sig_rows = (jnp.dot(gsel, s_hi, preferred_element_type=F32)
                    + jnp.dot(gsel, s_lo, preferred_element_type=F32))
        lane = lax.broadcasted_iota(jnp.int32, sig_rows.shape, 1)
        lane_l = lax.broadcasted_iota(jnp.int32, (rq, lanes), 1)
        for g in range(NSA_KV_HEADS):
            q = q_ref[0, g]
            kn = newrows_ref[:, (4 + g) * d:(5 + g) * d].astype(BF16)
            vn = newrows_ref[:, (6 + g) * d:(7 + g) * d].astype(BF16)
            s = lax.dot_general(q, kn, nt, preferred_element_type=F32)
            last_picked = jnp.sum(jnp.where(lane_l == past_len // SLC_BLOCK, sel16[g], 0.0), axis=-1, keepdims=True)
            online(g, s, own & (new_pos <= pos) & (last_picked > 0.5), vn)
            o_slc = acc_scr[g] / jnp.maximum(l_scr[g], 1e-30)
            w_len = kwin_ref.shape[1]
            win_rows = pl.ds(g, w_len, stride=NSA_KV_HEADS)
            kw = kwin_ref.reshape(NSA_KV_HEADS * w_len, d)[win_rows, :].astype(BF16)
            vw = vwin_ref.reshape(NSA_KV_HEADS * w_len, d)[win_rows, :].astype(BF16)
            s1 = lax.dot_general(q, kw, nt, preferred_element_type=F32)
            kp1 = past_len - w_len + lax.broadcasted_iota(jnp.int32, (rq, w_len), 1)
            d1 = pos - kp1
            m1 = (d1 >= 0) & (d1 < WINDOW) & (kp1 >= 0)
            knw = newwin_ref[:, g * d:(g + 1) * d].astype(BF16)
            vnw = newwin_ref[:, (2 + g) * d:(3 + g) * d].astype(BF16)
            s2 = lax.dot_general(q, knw, nt, preferred_element_type=F32)
            d2 = pos - new_pos
            m2 = own & (d2 >= 0) & (d2 < WINDOW)
            s1 = jnp.where(m1, s1, -1e30)
            s2 = jnp.where(m2, s2, -1e30)
            mx = jnp.maximum(jnp.max(s1, axis=-1, keepdims=True), jnp.max(s2, axis=-1, keepdims=True))
            p1 = jnp.where(m1, jnp.exp(s1 - mx), 0.0)
            p2 = jnp.where(m2, jnp.exp(s2 - mx), 0.0)
            den = jnp.sum(p1, axis=-1, keepdims=True) + jnp.sum(p2, axis=-1, keepdims=True)
            o_win = (jnp.dot(p1.astype(BF16), vw, preferred_element_type=F32)
                     + jnp.dot(p2.astype(BF16), vnw, preferred_element_type=F32)) / jnp.maximum(den, 1e-30)
            out = jnp.zeros((rq, d), F32)
            for branch, o_b in enumerate((ocmp_ref[0, g], o_slc, o_win)):
                col = branch * NSA_HEADS + g * r + row // t_len
                gate = jnp.sum(jnp.where(lane == col, sig_rows, 0.0), axis=-1, keepdims=True)
                out = out + gate * o_b
            o_ref[0, g] = out


def nsa_sample_attend(cache, page_table, q_s, sel, o_cmp, rows, win, win_cache, proj, past_len, t_len, row0):
    bs, n_pages = page_table.shape
    d = NSA_HEAD_DIM
    rq = NSA_GROUP * t_len
    lanes = sel.shape[3]
    w_len = win_cache.shape[1]
    assert row0 % t_len == 0 and NEW_ROWS_BLK % t_len == 0 and past_len % SLC_BLOCK == 0
    blk = lambda s: (row0 + s * t_len) // NEW_ROWS_BLK
    per_seq = lambda shape: pl.BlockSpec((1,) + shape, lambda s, c, pt: (s, 0, 0, 0))
    return pl.pallas_call(
        functools.partial(_sample_attend_kernel, past_len=past_len, t_len=t_len, row0=row0),
        grid_spec=pltpu.PrefetchScalarGridSpec(
            num_scalar_prefetch=1,
            grid=(bs, n_pages // SAMPLE_PAGES),
            in_specs=_page_specs(2) + _page_specs(3) + [
                per_seq((NSA_KV_HEADS, rq, d)), per_seq((NSA_KV_HEADS, SUBLANES, lanes)), per_seq((NSA_KV_HEADS, rq, d)),
                pl.BlockSpec((NEW_ROWS_BLK, 4 * NSA_KV_HEADS * d), lambda s, c, pt: (blk(s), 0)),
                pl.BlockSpec((1, w_len, 1, NSA_KV_HEADS, d), lambda s, c, pt: (s, 0, 0, 0, 0)),
                pl.BlockSpec((1, w_len, 1, NSA_KV_HEADS, d), lambda s, c, pt: (s, 0, 1, 0, 0)),
                pl.BlockSpec((NEW_ROWS_BLK, 4 * d), lambda s, c, pt: (blk(s), 0)),
                pl.BlockSpec((NEW_ROWS_BLK, LANES), lambda s, c, pt: (blk(s), GATE_COL0 // LANES))],
            out_specs=per_seq((NSA_KV_HEADS, rq, d)),
            scratch_shapes=[pltpu.VMEM((NSA_KV_HEADS, rq, 1), F32), pltpu.VMEM((NSA_KV_HEADS, rq, 1), F32),
                            pltpu.VMEM((NSA_KV_HEADS, rq, d), F32)]),
        out_shape=jax.ShapeDtypeStruct((bs, NSA_KV_HEADS, rq, d), F32),
        compiler_params=pltpu.CompilerParams(
            dimension_semantics=("parallel", "arbitrary"), vmem_limit_bytes=VMEM_LIMIT),
        name="nsa_sample_attend",
    )(page_table, *([cache] * (2 * SAMPLE_PAGES)), q_s, sel, o_cmp, rows, win_cache, win_cache, win, proj)


SC_ROWS = 512
SC_COLS = 512


def _short_conv_kernel(b_ref, c_ref, h_ref, w_ref, buf_ref, o_ref, tail_ref, prev_scr):
    @pl.when(pl.program_id(2) == 0)
    def _():
        prev_scr[...] = buf_ref[0]

    rows = c_ref.shape[0]
    u = c_ref[...] * h_ref[...]
    prev = prev_scr[...]
    row8 = lax.broadcasted_iota(jnp.int32, prev.shape, 0)
    conv = u * w_ref[SC_KSIZE - 1:SC_KSIZE, :]
    for k in range(1, SC_KSIZE):
        rolled = pltpu.roll(u, k, 0)
        top = jnp.where(row8 < k, pltpu.roll(prev, k, 0), rolled[0:SUBLANES])
        shifted = top if rows == SUBLANES else jnp.concatenate([top, rolled[SUBLANES:]], axis=0)
        conv = conv + shifted * w_ref[SC_KSIZE - 1 - k:SC_KSIZE - k, :]
    prev_scr[...] = u[rows - SUBLANES:rows]
    o_ref[...] = (b_ref[...] * conv).astype(o_ref.dtype)
    tail_ref[0] = u[rows - SUBLANES:rows]


def short_conv(proj, n_batch, t_len, rows_per_step, conv_buf, sc_w):
    nr = t_len // rows_per_step
    nh = SC_WIDTH // SC_COLS
    blk0 = SC_COL0 // SC_COLS
    assert SC_COL0 % SC_COLS == 0
    buf8 = jnp.pad(conv_buf, ((0, 0), (SUBLANES - (SC_KSIZE - 1), 0), (0, 0)))
    col = lambda part: pl.BlockSpec((rows_per_step, SC_COLS), lambda b, j, i: (b * nr + i, blk0 + part * nh + j))
    return pl.pallas_call(
        _short_conv_kernel,
        grid=(n_batch, nh, nr),
        in_specs=[col(0), col(1), col(2), pl.BlockSpec((SC_KSIZE, SC_COLS), lambda b, j, i: (0, j)),
                  pl.BlockSpec((1, SUBLANES, SC_COLS), lambda b, j, i: (b, 0, j))],
        out_specs=[pl.BlockSpec((rows_per_step, SC_COLS), lambda b, j, i: (b * nr + i, j)),
                   pl.BlockSpec((1, SUBLANES, SC_COLS), lambda b, j, i: (b, 0, j))],
        out_shape=[jax.ShapeDtypeStruct((n_batch * t_len, SC_WIDTH), BF16),
                   jax.ShapeDtypeStruct((n_batch, SUBLANES, SC_WIDTH), F32)],
        scratch_shapes=[pltpu.VMEM((SUBLANES, SC_COLS), F32)],
        compiler_params=pltpu.CompilerParams(
            dimension_semantics=("parallel", "parallel", "arbitrary"), vmem_limit_bytes=VMEM_LIMIT),
        name="short_conv",
    )(proj, proj, proj, sc_w, buf8)


def _ret_tables(pos, chunk, valid):
    half = RET_DQK // 2
    inv = RET_THETA ** (-jnp.arange(half, dtype=F32) / half)
    ang = pos.astype(F32)[:, None] * inv[None, :]
    log_g = jnp.log1p(-(2.0 ** (-5.0 - jnp.arange(RET_HEADS, dtype=F32))))
    i = jnp.arange(chunk, dtype=F32)
    diff = i[:, None] - i[None, :]
    intra = jnp.where(diff >= 0, jnp.exp(jnp.maximum(diff, 0.0)[None] * log_g[:, None, None]), 0.0)
    q_dec = jnp.exp((i[None, :] + 1.0) * log_g[:, None])[..., None]
    k_dec = jnp.exp((valid - 1.0 - i)[None, :] * log_g[:, None])[..., None]
    c_dec = jnp.exp(valid * log_g)[:, None, None]
    return jnp.cos(ang), jnp.sin(ang), intra, q_dec, k_dec, c_dec


def _retention_kernel(q_ref, k_ref, v_ref, g_ref, cos_ref, sin_ref, intra_ref, qd_ref, kd_ref, cd_ref, gn_ref,
                      s0_ref, o_ref, s_out_ref, s_scr):
    c_idx = pl.program_id(1)

    @pl.when(c_idx == 0)
    def _():
        s_scr[...] = s0_ref[0]

    cos, sin = cos_ref[...], sin_ref[...]
    half = RET_DQK // 2
    nt = (((1,), (1,)), ((), ()))
    tn = (((0,), (0,)), ((), ()))

    def rot(x):
        x1, x2 = x[:, :half], x[:, half:]
        return jnp.concatenate([x1 * cos - x2 * sin, x1 * sin + x2 * cos], axis=-1)

    for h in range(RET_HEADS):
        cols = slice(h * RET_DQK, (h + 1) * RET_DQK)
        qr = (rot(q_ref[:, cols]) * (RET_DQK ** -0.5)).astype(BF16)
        kr = rot(k_ref[:, cols])
        v = v_ref[:, cols].astype(BF16)
        att = lax.dot_general(qr, kr.astype(BF16), nt, preferred_element_type=F32) * intra_ref[h]
        s_old = s_scr[h]
        o = (jnp.dot(att.astype(BF16), v, preferred_element_type=F32)
             + jnp.dot(qr, s_old.astype(BF16), preferred_element_type=F32) * qd_ref[h])
        s_scr[h] = s_old * cd_ref[h] + lax.dot_general((kr * kd_ref[h]).astype(BF16), v, tn,
                                                       preferred_element_type=F32)
        mu = jnp.mean(o, axis=-1, keepdims=True)
        dev = o - mu
        var = jnp.mean(dev * dev, axis=-1, keepdims=True)
        gate = g_ref[:, cols]
        on = dev * lax.rsqrt(var + EPS) * gn_ref[:, cols] * (gate * jax.nn.sigmoid(gate))
        o_ref[:, cols] = on.astype(o_ref.dtype)
    s_out_ref[0] = s_scr[...]


def retention(proj, pos, n_batch, t_len, chunk, state0, gn_gain, valid=None):
    nc = t_len // chunk
    cos, sin, intra, q_dec, k_dec, c_dec = _ret_tables(pos, chunk, chunk if valid is None else valid)
    half = RET_DQK // 2
    col = lambda j: pl.BlockSpec((chunk, R_QK), lambda b, c: (b * nc + c, j))
    tab = pl.BlockSpec((chunk, half), lambda b, c: (c, 0))
    full = lambda a: pl.BlockSpec(a.shape, lambda b, c: (0,) * a.ndim)
    st = pl.BlockSpec((1, RET_HEADS, RET_DQK, RET_DV), lambda b, c: (b, 0, 0, 0))
    return pl.pallas_call(
        _retention_kernel,
        grid=(n_batch, nc),
        in_specs=[col(0), col(1), col(2), col(3), tab, tab, full(intra), full(q_dec), full(k_dec), full(c_dec),
                  pl.BlockSpec((1, R_V), lambda b, c: (0, 0)), st],
        out_specs=[pl.BlockSpec((chunk, R_V), lambda b, c: (b * nc + c, 0)), st],
        out_shape=[jax.ShapeDtypeStruct((n_batch * t_len, R_V), BF16),
                   jax.ShapeDtypeStruct((n_batch, RET_HEADS, RET_DQK, RET_DV), F32)],
        scratch_shapes=[pltpu.VMEM((RET_HEADS, RET_DQK, RET_DV), F32)],
        compiler_params=pltpu.CompilerParams(
            dimension_semantics=("parallel", "arbitrary"), vmem_limit_bytes=VMEM_LIMIT),
        name="retention",
    )(proj, proj, proj, proj, cos, sin, intra, q_dec, k_dec, c_dec, gn_gain.reshape(1, R_V), state0)


SSD_COL0 = 2 * R_QK + 2 * R_V
HEADS_PER_GROUP = M2_HEADS // M2_GROUPS


def _split3(x):
    a = x.astype(BF16)
    r = x - a.astype(F32)
    b = r.astype(BF16)
    c = (r - b.astype(F32)).astype(BF16)
    return a, b, c


def _exact_dot(mat_bf, x):
    out = None
    for piece in _split3(x):
        t = jnp.dot(mat_bf, piece, preferred_element_type=F32)
        out = t if out is None else out + t
    return out


def _exact_dot_r(x, mat_bf):
    out = None
    for piece in _split3(x):
        t = jnp.dot(piece, mat_bf, preferred_element_type=F32)
        out = t if out is None else out + t
    return out


def _ssd_kernel(z_ref, xa_ref, xb_ref, xc_ref, dt_ref, cw_ref, cb_ref, dtb_ref, aneg_ref, dskip_ref, norm_ref,
                buf0_ref, s0_ref, o_ref, s_out_ref, s_scr, prev_scr, *, valid):
    c_idx = pl.program_id(1)
    chunk = z_ref.shape[0]
    nt = (((1,), (1,)), ((), ()))
    tn = (((0,), (0,)), ((), ()))

    @pl.when(c_idx == 0)
    def _():
        s_scr[...] = s0_ref[0]
        prev_scr[...] = buf0_ref[0]

    x = jnp.concatenate([xa_ref[...], xb_ref[...], xc_ref[...]], axis=1)
    prev = prev_scr[...]
    row8 = lax.broadcasted_iota(jnp.int32, prev.shape, 0)
    conv = x * cw_ref[M2_CONV - 1:M2_CONV, :]
    for k in range(1, M2_CONV):
        rolled = pltpu.roll(x, k, 0)
        top = jnp.where(row8 < k, pltpu.roll(prev, k, 0), rolled[0:SUBLANES])
        shifted = top if chunk == SUBLANES else jnp.concatenate([top, rolled[SUBLANES:]], axis=0)
        conv = conv + shifted * cw_ref[M2_CONV - 1 - k:M2_CONV - k, :]
    prev_scr[...] = x[chunk - SUBLANES:chunk]
    conv = conv + cb_ref[...]
    xbc = conv * jax.nn.sigmoid(conv)
    xs = xbc[:, :M2_DINNER]

    dt_raw = dt_ref[...] + dtb_ref[...]
    dt = jnp.where(dt_raw > 20.0, dt_raw, jnp.log1p(jnp.exp(jnp.minimum(dt_raw, 20.0))))
    if valid < chunk:
        dt = jnp.where(lax.broadcasted_iota(jnp.int32, dt.shape, 0) < valid, dt, 0.0)
    a = dt * aneg_ref[...]
    ri = lax.broadcasted_iota(jnp.int32, (chunk, chunk), 0)
    ci = lax.broadcasted_iota(jnp.int32, (chunk, chunk), 1)
    tri = ri >= ci
    cum = _exact_dot(jnp.where(tri, 1.0, 0.0).astype(BF16), a)
    cum_t = cum.T
    cum_last = cum[chunk - 1:chunk, :]
    hi = lax.broadcasted_iota(jnp.int32, (LANES, M2_DINNER), 0)
    li = lax.broadcasted_iota(jnp.int32, (LANES, M2_DINNER), 1)
    expand = jnp.where(hi == li // M2_HEADDIM, 1.0, 0.0).astype(BF16)
    dt_x = _exact_dot_r(dt, expand)
    cum_x = _exact_dot_r(cum, expand)
    last_x = _exact_dot_r(cum_last, expand)
    xdt = xs * dt_x
    x_dec = (xdt * jnp.exp(last_x - cum_x)).astype(BF16)
    xdt_bf = xdt.astype(BF16)
    e_cum_x = jnp.exp(cum_x)
    e_last_x = jnp.exp(last_x)

    y_parts = []
    for gi in range(M2_GROUPS):
        b_g = xbc[:, M2_DINNER + gi * M2_STATE:M2_DINNER + (gi + 1) * M2_STATE].astype(BF16)
        c_g = xbc[:, M2_DINNER + (M2_GROUPS + gi) * M2_STATE:M2_DINNER + (M2_GROUPS + gi + 1) * M2_STATE].astype(BF16)
        cb = lax.dot_general(c_g, b_g, nt, preferred_element_type=F32)
        gcols = slice(gi * HEADS_PER_GROUP * M2_HEADDIM, (gi + 1) * HEADS_PER_GROUP * M2_HEADDIM)
        s_old = s_scr[:, gcols]
        y_state = jnp.dot(c_g, s_old.astype(BF16), preferred_element_type=F32) * e_cum_x[:, gcols]
        s_scr[:, gcols] = s_old * e_last_x[:, gcols] + lax.dot_general(b_g, x_dec[:, gcols], tn,
                                                                        preferred_element_type=F32)
        pair_lane = lax.broadcasted_iota(jnp.int32, (chunk, LANES), 1)
        intra = []
        for pr in range(HEADS_PER_GROUP // 2):
            outs = []
            for sub in range(2):
                h = gi * HEADS_PER_GROUP + pr * 2 + sub
                seg = cum[:, h:h + 1] - cum_t[h:h + 1, :]
                l_mat = jnp.where(tri, jnp.exp(jnp.where(tri, seg, 0.0)), 0.0)
                lanes = slice(gi * HEADS_PER_GROUP * M2_HEADDIM + pr * LANES,
                              gi * HEADS_PER_GROUP * M2_HEADDIM + (pr + 1) * LANES)
                outs.append(jnp.dot((cb * l_mat).astype(BF16), xdt_bf[:, lanes], preferred_element_type=F32))
            intra.append(jnp.where(pair_lane < M2_HEADDIM, outs[0], outs[1]))
        y_parts.append(jnp.concatenate(intra, axis=1) + y_state)
    y = jnp.concatenate(y_parts, axis=1) + dskip_ref[...] * xs
    z = z_ref[...]
    y = y * (z * jax.nn.sigmoid(z))
    gw = M2_DINNER // M2_GROUPS
    outs = []
    for gi in range(M2_GROUPS):
        yg = y[:, gi * gw:(gi + 1) * gw]
        outs.append(yg * lax.rsqrt(jnp.mean(yg * yg, axis=-1, keepdims=True) + EPS))
    o_ref[...] = (jnp.concatenate(outs, axis=1) * norm_ref[...]).astype(o_ref.dtype)
    s_out_ref[0] = s_scr[...]


def ssd(proj, n_batch, t_len, chunk, conv_buf, state0, conv_w, conv_b, dt_bias, a_log, d_skip, m2_norm, valid=None):
    nc = t_len // chunk
    z_blk = SSD_COL0 // M2_DINNER
    xw = M2_CONV_DIM // 3
    xbc_blk = (SSD_COL0 + M2_DINNER) // xw
    assert (SSD_COL0 + M2_DINNER) % xw == 0 and xw % LANES == 0
    dt_blk = (SSD_COL0 + M2_DINNER + M2_CONV_DIM) // LANES
    pad = lambda v: jnp.pad(v.reshape(1, -1), ((0, 0), (0, LANES - v.shape[-1])))
    buf8 = jnp.pad(conv_buf, ((0, 0), (SUBLANES - (M2_CONV - 1), 0), (0, 0)))
    st_t = state0.transpose(0, 3, 1, 2).reshape(n_batch, M2_STATE, M2_DINNER)
    row = lambda a: pl.BlockSpec(a.shape, lambda b, c: (0, 0))
    cw = conv_w
    cb = conv_b.reshape(1, -1)
    dtb, aneg = pad(dt_bias), pad(-jnp.exp(a_log))
    dsk = jnp.repeat(d_skip, M2_HEADDIM).reshape(1, -1)
    nrm = m2_norm.reshape(1, -1)
    st = pl.BlockSpec((1, M2_STATE, M2_DINNER), lambda b, c: (b, 0, 0))
    out, s_fin = pl.pallas_call(
        functools.partial(_ssd_kernel, valid=chunk if valid is None else valid),
        grid=(n_batch, nc),
        in_specs=[pl.BlockSpec((chunk, M2_DINNER), lambda b, c: (b * nc + c, z_blk)),
                  pl.BlockSpec((chunk, xw), lambda b, c: (b * nc + c, xbc_blk)),
                  pl.BlockSpec((chunk, xw), lambda b, c: (b * nc + c, xbc_blk + 1)),
                  pl.BlockSpec((chunk, xw), lambda b, c: (b * nc + c, xbc_blk + 2)),
                  pl.BlockSpec((chunk, LANES), lambda b, c: (b * nc + c, dt_blk)),
                  row(cw), row(cb), row(dtb), row(aneg), row(dsk), row(nrm),
                  pl.BlockSpec((1, SUBLANES, M2_CONV_DIM), lambda b, c: (b, 0, 0)), st],
        out_specs=[pl.BlockSpec((chunk, M2_DINNER), lambda b, c: (b * nc + c, 0)), st],
        out_shape=[jax.ShapeDtypeStruct((n_batch * t_len, M2_DINNER), BF16),
                   jax.ShapeDtypeStruct((n_batch, M2_STATE, M2_DINNER), F32)],
        scratch_shapes=[pltpu.VMEM((M2_STATE, M2_DINNER), F32), pltpu.VMEM((SUBLANES, M2_CONV_DIM), F32)],
        compiler_params=pltpu.CompilerParams(
            dimension_semantics=("parallel", "arbitrary"), vmem_limit_bytes=VMEM_LIMIT),
        name="ssd",
    )(proj, proj, proj, proj, proj, cw, cb, dtb, aneg, dsk, nrm, buf8, st_t)
    s_fin = s_fin.reshape(n_batch, M2_STATE, M2_HEADS, M2_HEADDIM).transpose(0, 2, 3, 1)
    return out, s_fin


SEQ_ROWS = SUBLANES


def _pad_seq_rows(a, n_seq, t_len):
    assert t_len <= SEQ_ROWS
    a = a.reshape(n_seq, t_len, a.shape[-1])
    return jnp.pad(a, ((0, 0), (0, SEQ_ROWS - t_len), (0, 0))).reshape(n_seq * SEQ_ROWS, a.shape[-1])


def _unpad_seq_rows(a, n_seq, t_len):
    return a.reshape(n_seq, SEQ_ROWS, a.shape[-1])[:, :t_len].reshape(n_seq * t_len, a.shape[-1])

def kernel(x_prompt, x_sample, cache_nsa_kv, cache_nsa_win, state_sc_conv, state_ret, state_ssm, state_m2_conv,
           page_table, norm_w, final_norm, e_w_in, e_w_out, e_cmp_pe, e_cmp_w1, e_cmp_w2, e_sc_conv, o_w_in,
           o_w_out, o_ret_gn, o_m2_conv_w, o_m2_conv_b, o_m2_dt_bias, o_m2_a_log, o_m2_d, o_m2_norm, peer_wq,
           peer_keys, peer_u, peer_v):
    bp, tp, dm = x_prompt.shape
    bs, ts, _ = x_sample.shape
    n_p, n_s = bp * tp, bs * ts
    g, r, d = NSA_KV_HEADS, NSA_GROUP, NSA_HEAD_DIM
    past_len = page_table.shape[1] * PAGE_SIZE
    pos_p = jnp.arange(tp, dtype=jnp.int32)
    pos_s = past_len + jnp.arange(ts, dtype=jnp.int32)
    n_real = n_p + n_s
    n_pad = _round_up(n_real, PEER_TM) - n_real
    s_rows = slice(n_p, n_real)
    pos_rows = jnp.concatenate([jnp.tile(pos_p, bp), jnp.tile(pos_s, bs), jnp.zeros((n_pad,), jnp.int32)])
    xp = x_prompt.reshape(n_p, dm)
    x = jnp.concatenate([xp, x_sample.reshape(n_s, dm), xp[:n_pad]], axis=0)
    pad_bf = jnp.zeros((n_pad, E_Q), BF16)

    w0 = e_w_in[0]
    w_in = jnp.concatenate([w0[:, :SC_COL0], w0[:, SC_COL0 + E_G:], w0[:, SC_COL0:SC_COL0 + E_G],
                            jnp.zeros((dm, _pad_cols(E_IN) - E_IN), F32)], axis=1).astype(BF16)
    proj = norm_matmul(x, norm_w[0, 0], w_in)
    q_bf, rows, win, rows_bf, win_bf = nsa_rope(proj, pos_rows)
    kvc = nsa_compress(rows, bp, tp, e_cmp_pe[0], e_cmp_w1[0], e_cmp_w2[0])
    o_nsa_p = nsa_prompt(q_bf, kvc, rows_bf, win_bf, proj, bp, tp)
    p_kv = rows[:n_p].reshape(bp, tp, 4, g, d)
    s_kv = rows[s_rows].reshape(bs, ts, 4, g, d)
    p_win = win[:n_p].reshape(bp, tp, 2, g, d)[:, tp - min(WINDOW, tp):]
    win_s = win[s_rows].reshape(bs, ts, 2, g, d)
    s_win = jnp.concatenate([cache_nsa_win[0], win_s], axis=1)[:, ts:]
    cache = cache_nsa_kv[0]
    q_s = q_bf[s_rows].reshape(bs, ts, g, r, d).transpose(0, 2, 3, 1, 4).reshape(bs, g, r * ts, d)
    a0, a1 = nsa_sample_compress(cache, page_table, e_cmp_pe[0], e_cmp_w1[0])
    o_cmp_s, sel_s = nsa_sample_select(a0, a1, e_cmp_w2[0], q_s, past_len, ts)
    o_nsa_s = nsa_sample_attend(cache, page_table, q_s, sel_s, o_cmp_s, rows, win,
                                cache_nsa_win[0], proj, past_len, ts, n_p)
    o_nsa_s = o_nsa_s.reshape(bs, g, r, ts, d).transpose(0, 3, 1, 2, 4).reshape(n_s, E_Q)
    o_nsa = jnp.concatenate([o_nsa_p, o_nsa_s.astype(BF16), pad_bf], axis=0)
    o_sc_p, tail_p = short_conv(proj, bp, tp, SC_ROWS, jnp.zeros((bp, SC_KSIZE - 1, SC_WIDTH), F32), e_sc_conv[0])
    p_sc = tail_p[:, SUBLANES - (SC_KSIZE - 1):]
    o_sc_s, tail_s = short_conv(_pad_seq_rows(proj[s_rows], bs, ts), bs, SEQ_ROWS, SEQ_ROWS, state_sc_conv[0],
                                e_sc_conv[0])
    s_sc = jnp.concatenate([state_sc_conv[0], tail_s[:, :ts]], axis=1)[:, ts:]
    o_sc = jnp.concatenate([_unpad_seq_rows(o_sc_s, bs, ts), pad_bf], axis=0)
    o_sc = jnp.concatenate([o_sc_p, o_sc], axis=0)
    x = matmul2_res(o_nsa, o_sc, e_w_out[0].astype(BF16), x)
    u_bf = peer_u.astype(BF16)
    vt_bf = peer_v.transpose(0, 2, 1).astype(BF16)
    peer_t = peer_layer(x, norm_w[0, 1], peer_wq[0], peer_keys[0], u_bf, vt_bf, 0)

    w_in = jnp.pad(o_w_in[0], ((0, 0), (0, _pad_cols(O_IN) - O_IN))).astype(BF16)
    x = x + peer_t.T
    proj = norm_matmul(x, norm_w[1, 0], w_in)
    odd_w = (o_m2_conv_w[0], o_m2_conv_b[0], o_m2_dt_bias[0], o_m2_a_log[0], o_m2_d[0], o_m2_norm[0])
    o_ret_p, p_ret = retention(proj, pos_p, bp, tp, CHUNK, jnp.zeros((bp, RET_HEADS, RET_DQK, RET_DV), F32),
                               o_ret_gn[0])
    o_ssd_p, p_ssm = ssd(proj, bp, tp, CHUNK, jnp.zeros((bp, M2_CONV - 1, M2_CONV_DIM), F32),
                         jnp.zeros((bp, M2_HEADS, M2_HEADDIM, M2_STATE), F32), *odd_w)
    xbc_cols = slice(O_SPLITS[4], O_SPLITS[5])
    p_m2c = jnp.stack([proj[(b + 1) * tp - (M2_CONV - 1):(b + 1) * tp, xbc_cols] for b in range(bp)])
    xbc_s = proj[s_rows, xbc_cols].reshape(bs, ts, M2_CONV_DIM)
    s_m2c = jnp.concatenate([state_m2_conv[0], xbc_s], axis=1)[:, ts:]
    proj_s = _pad_seq_rows(proj[s_rows], bs, ts)
    pos_s8 = past_len + jnp.arange(SEQ_ROWS, dtype=jnp.int32)
    o_ret_s, s_ret = retention(proj_s, pos_s8, bs, SEQ_ROWS, SEQ_ROWS, state_ret[0], o_ret_gn[0], valid=ts)
    o_ssd_s, s_ssm = ssd(proj_s, bs, SEQ_ROWS, SEQ_ROWS, state_m2_conv[0], state_ssm[0], *odd_w, valid=ts)
    o_ret = jnp.concatenate([o_ret_p, _unpad_seq_rows(o_ret_s, bs, ts), pad_bf], axis=0)
    o_ssd = jnp.concatenate([o_ssd_p, _unpad_seq_rows(o_ssd_s, bs, ts), pad_bf], axis=0)
    x = matmul2_res(o_ret, o_ssd, o_w_out[0].astype(BF16), x)
    peer_t = peer_layer(x, norm_w[1, 1], peer_wq[1], peer_keys[1], u_bf, vt_bf, 1)

    y = resid_rmsnorm(x, peer_t, final_norm)
    y_prompt = y[:n_p].reshape(bp, tp, dm)
    y_sample = y[s_rows].reshape(bs, ts, dm)
    return (y_prompt, y_sample, p_kv[None], p_win[None], p_sc[None], p_ret[None], p_ssm[None], p_m2c[None],
            s_kv[None], s_win[None], s_sc[None], s_ret[None], s_ssm[None], s_m2c[None])
```
